```python
import math
import jax
import jax.numpy as jnp
from jax import lax
import numpy as np

D_MODEL = 1024
BATCH = 8
SEQ = 4096
DEPTH = 1

D_MIX = D_MODEL
D_MLSTM = D_MIX // 2
N_MLSTM_HEADS = 4
MLSTM_HEAD_DIM = D_MLSTM // N_MLSTM_HEADS
D_CONV = D_MIX - D_MLSTM
CONV_WIDTH = 3
CHUNK = 64
FORGET_BIAS_LO = 3.0
FORGET_BIAS_HI = 6.0
N_GROUPS = 4
EXPERTS_PER_GROUP = 8
N_EXPERTS = N_GROUPS * EXPERTS_PER_GROUP
TOP_K = 2
D_FF_EXPERT = D_MODEL // 2
MOE_BLOCK = 128
RMS_EPS = 1e-6

F_OFF = 4 * D_MLSTM + N_MLSTM_HEADS
D_PROJ = 4 * D_MLSTM + 2 * N_MLSTM_HEADS + 3 * D_CONV
SPLIT_POINTS = (
    D_MLSTM,
    2 * D_MLSTM,
    3 * D_MLSTM,
    4 * D_MLSTM,
    4 * D_MLSTM + N_MLSTM_HEADS,
    4 * D_MLSTM + 2 * N_MLSTM_HEADS,
    4 * D_MLSTM + 2 * N_MLSTM_HEADS + D_CONV,
    4 * D_MLSTM + 2 * N_MLSTM_HEADS + 2 * D_CONV,
)

kernel_name = "hymba_mlstm_shortconv_hmoe_block"


def rms_norm(x, g):
    xf = x.astype(jnp.float32)
    y = xf * lax.rsqrt(jnp.mean(xf * xf, axis=-1, keepdims=True) + RMS_EPS)
    return (y * g.astype(jnp.float32)).astype(x.dtype)


def mlstm_chunkwise(q, k, v, log_i, log_f):
    b_sz, s_len, n_h, d_h = q.shape
    n_c = s_len // CHUNK

    def to_chunks(t):
        return t.reshape(b_sz, n_c, CHUNK, n_h, d_h).transpose(0, 3, 1, 2, 4)

    def gate_chunks(t):
        return t.reshape(b_sz, n_c, CHUNK, n_h).transpose(0, 3, 1, 2)

    qc, kc, vc = to_chunks(q), to_chunks(k), to_chunks(v)
    li, lf = gate_chunks(log_i), gate_chunks(log_f)

    b_cum = jnp.cumsum(lf, axis=-1)
    g_tot = b_cum[..., -1]
    a_end = g_tot[..., None] - b_cum + li
    m_loc = jnp.max(a_end, axis=-1)

    def step(carry, xs):
        c_st, n_st, m_st = carry
        k_c, v_c, a_c, g_c, ml_c = xs
        m_new = jnp.maximum(g_c + m_st, ml_c)
        decay = jnp.exp(g_c + m_st - m_new)
        w = jnp.exp(a_c - m_new[..., None])
        c_new = decay[..., None, None] * c_st + jnp.einsum('bhl,bhle,bhld->bhed', w, v_c, k_c)
        n_new = decay[..., None] * n_st + jnp.einsum('bhl,bhld->bhd', w, k_c)
        return (c_new, n_new, m_new), (c_st, n_st, m_st)

    init = (
        jnp.zeros((b_sz, n_h, d_h, d_h), jnp.float32),
        jnp.zeros((b_sz, n_h, d_h), jnp.float32),
        jnp.zeros((b_sz, n_h), jnp.float32),
    )
    xs = (
        jnp.moveaxis(kc, 2, 0),
        jnp.moveaxis(vc, 2, 0),
        jnp.moveaxis(a_end, 2, 0),
        jnp.moveaxis(g_tot, 2, 0),
        jnp.moveaxis(m_loc, 2, 0),
    )
    _, (c_prev, n_prev, m_prev) = lax.scan(step, init, xs)
    c_prev = jnp.moveaxis(c_prev, 0, 2)
    n_prev = jnp.moveaxis(n_prev, 0, 2)
    m_prev = jnp.moveaxis(m_prev, 0, -1)

    causal = jnp.tril(jnp.ones((CHUNK, CHUNK), dtype=bool))
    d_log = b_cum[..., :, None] - b_cum[..., None, :] + li[..., None, :]
    d_log = jnp.where(causal, d_log, -jnp.inf)
    inter_log = b_cum + m_prev[..., None]
    m_t = jnp.maximum(inter_log, jnp.max(d_log, axis=-1))
    scores = jnp.einsum('bhcld,bhcjd->bhclj', qc, kc) * jnp.exp(d_log - m_t[..., None])
    w_inter = jnp.exp(inter_log - m_t)
    num = jnp.einsum('bhclj,bhcje->bhcle', scores, vc) + w_inter[..., None] * jnp.einsum(
        'bhced,bhcld->bhcle', c_prev, qc)
    den = jnp.sum(scores, axis=-1) + w_inter * jnp.einsum('bhcd,bhcld->bhcl', n_prev, qc)
    h = num / jnp.maximum(jnp.abs(den), jnp.exp(-m_t))[..., None]
    return h.transpose(0, 2, 3, 1, 4).reshape(b_sz, s_len, n_h, d_h)


def short_gated_conv(gate_b, gate_c, xin, conv_w):
    u = gate_c * xin
    up = jnp.pad(u, ((0, 0), (CONV_WIDTH - 1, 0), (0, 0)))
    s_len = u.shape[1]
    z = conv_w[0] * up[:, 0:s_len] + conv_w[1] * up[:, 1:s_len + 1] + conv_w[2] * up[:, 2:s_len + 2]
    return gate_b * z


def hier_moe(h, w_group, b_group, w_expert, b_expert, w_gate, w_up, w_down):
    n_tok, d = h.shape
    hf = h.astype(jnp.float32)
    group_logits = hf @ w_group.astype(jnp.float32) + b_group.astype(jnp.float32)
    group_prob = jax.nn.softmax(group_logits, axis=-1)
    g_sel = jnp.argmax(group_logits, axis=-1)
    p_g = jnp.take_along_axis(group_prob, g_sel[:, None], axis=1)[:, 0]
    exp_logits = (hf @ w_expert.astype(jnp.float32) + b_expert.astype(jnp.float32)).reshape(
        n_tok, N_GROUPS, EXPERTS_PER_GROUP)
    in_group = jnp.take_along_axis(exp_logits, g_sel[:, None, None], axis=1)[:, 0]
    top_val, top_idx = lax.top_k(in_group, TOP_K)
    top_w = jax.nn.softmax(top_val, axis=-1) * p_g[:, None]
    expert_id = g_sel[:, None] * EXPERTS_PER_GROUP + top_idx

    nk = n_tok * TOP_K
    flat_e = expert_id.reshape(nk).astype(jnp.int32)
    flat_tok = jnp.repeat(jnp.arange(n_tok, dtype=jnp.int32), TOP_K)
    flat_w = top_w.reshape(nk).astype(h.dtype)

    counts = jnp.bincount(flat_e, length=N_EXPERTS)
    starts = jnp.cumsum(counts) - counts
    padded = ((counts + MOE_BLOCK - 1) // MOE_BLOCK) * MOE_BLOCK
    pends = jnp.cumsum(padded)
    pstarts = pends - padded
    order = jnp.argsort(flat_e)
    sorted_e = flat_e[order]
    rank = jnp.arange(nk, dtype=jnp.int32) - starts[sorted_e]
    dest = pstarts[sorted_e] + rank

    n_blocks = -(-nk // MOE_BLOCK) + N_EXPERTS
    cap = n_blocks * MOE_BLOCK
    buf_tok = jnp.zeros((cap,), jnp.int32).at[dest].set(flat_tok[order])
    buf_w = jnp.zeros((cap,), h.dtype).at[dest].set(flat_w[order])
    block_start = jnp.arange(n_blocks, dtype=jnp.int32) * MOE_BLOCK
    block_e = jnp.clip(jnp.searchsorted(pends, block_start, side='right'), 0, N_EXPERTS - 1)

    xb = h[buf_tok].reshape(n_blocks, MOE_BLOCK, d)

    def expert_block(args):
        x_blk, e = args
        a = x_blk @ w_gate[e]
        u = x_blk @ w_up[e]
        return (jax.nn.silu(a) * u) @ w_down[e]

    yb = lax.map(expert_block, (xb, block_e)).reshape(cap, d)
    return jnp.zeros((n_tok, d), h.dtype).at[buf_tok].add(buf_w[:, None] * yb)


def hybrid_layer(x, norm_mix, w_in, b_in, conv_w, w_out, norm_ffn,
                 w_group, b_group, w_expert, b_expert, w_gate, w_up, w_down):
    b_sz, s_len, d = x.shape
    hn = rms_norm(x, norm_mix)
    proj = hn @ w_in + b_in
    q, k, v, o, ig, fg, cb, cc, cx = jnp.split(proj, SPLIT_POINTS, axis=-1)
    heads = (b_sz, s_len, N_MLSTM_HEADS, MLSTM_HEAD_DIM)
    qf = q.astype(jnp.float32).reshape(heads) * (MLSTM_HEAD_DIM ** -0.5)
    kf = k.astype(jnp.float32).reshape(heads)
    vf = v.astype(jnp.float32).reshape(heads)
    log_i = ig.astype(jnp.float32)
    log_f = jax.nn.log_sigmoid(fg.astype(jnp.float32))
    h_m = mlstm_chunkwise(qf, kf, vf, log_i, log_f).reshape(b_sz, s_len, D_MLSTM)
    y_m = (jax.nn.sigmoid(o.astype(jnp.float32)) * h_m).astype(x.dtype)
    y_c = short_gated_conv(cb, cc, cx, conv_w)
    x = x + jnp.concatenate([y_m, y_c], axis=-1) @ w_out
    hn2 = rms_norm(x, norm_ffn)
    moe = hier_moe(hn2.reshape(b_sz * s_len, d), w_group, b_group, w_expert, b_expert,
                   w_gate, w_up, w_down)
    return x + moe.reshape(b_sz, s_len, d)


def setup_inputs(seed: int = 0) -> dict:
    key = jax.random.key(seed)
    ks = jax.random.split(key, 16)
    nrm = jax.random.normal
    x = nrm(ks[0], (BATCH, SEQ, D_MODEL), jnp.float32)
    norm_mix = 1.0 + 0.02 * nrm(ks[1], (DEPTH, D_MODEL), jnp.float32)
    w_in = nrm(ks[2], (DEPTH, D_MODEL, D_PROJ), jnp.float32) * (D_MODEL ** -0.5)
    b_in = 0.01 * nrm(ks[3], (DEPTH, D_PROJ), jnp.float32)
    b_in = b_in.at[:, F_OFF:F_OFF + N_MLSTM_HEADS].add(
        jnp.linspace(FORGET_BIAS_LO, FORGET_BIAS_HI, N_MLSTM_HEADS, dtype=jnp.float32))
    conv_w = nrm(ks[4], (DEPTH, CONV_WIDTH, D_CONV), jnp.float32) * (CONV_WIDTH ** -0.5)
    w_out = nrm(ks[5], (DEPTH, D_MIX, D_MODEL), jnp.float32) * (D_MIX ** -0.5)
    norm_ffn = 1.0 + 0.02 * nrm(ks[6], (DEPTH, D_MODEL), jnp.float32)
    w_group = nrm(ks[7], (DEPTH, D_MODEL, N_GROUPS), jnp.float32) * (D_MODEL ** -0.5)
    b_group = 0.01 * nrm(ks[8], (DEPTH, N_GROUPS), jnp.float32)
    w_expert = nrm(ks[9], (DEPTH, D_MODEL, N_EXPERTS), jnp.float32) * (D_MODEL ** -0.5)
    b_expert = 0.01 * nrm(ks[10], (DEPTH, N_EXPERTS), jnp.float32)
    w_gate = nrm(ks[11], (DEPTH, N_EXPERTS, D_MODEL, D_FF_EXPERT), jnp.float32) * (D_MODEL ** -0.5)
    w_up = nrm(ks[12], (DEPTH, N_EXPERTS, D_MODEL, D_FF_EXPERT), jnp.float32) * (D_MODEL ** -0.5)
    w_down = nrm(ks[13], (DEPTH, N_EXPERTS, D_FF_EXPERT, D_MODEL), jnp.float32) * (D_FF_EXPERT ** -0.5)
    norm_final = 1.0 + 0.02 * nrm(ks[14], (D_MODEL,), jnp.float32)
    return {
        'x': x, 'norm_mix': norm_mix, 'w_in': w_in, 'b_in': b_in, 'conv_w': conv_w,
        'w_out': w_out, 'norm_ffn': norm_ffn, 'w_group': w_group, 'b_group': b_group,
        'w_expert': w_expert, 'b_expert': b_expert, 'w_gate': w_gate, 'w_up': w_up,
        'w_down': w_down, 'norm_final': norm_final,
    }


def reference(x, norm_mix, w_in, b_in, conv_w, w_out, norm_ffn, w_group, b_group,
              w_expert, b_expert, w_gate, w_up, w_down, norm_final):
    for l in range(DEPTH):
        x = hybrid_layer(x, norm_mix[l], w_in[l], b_in[l], conv_w[l], w_out[l], norm_ffn[l],
                         w_group[l], b_group[l], w_expert[l], b_expert[l],
                         w_gate[l], w_up[l], w_down[l])
    return rms_norm(x, norm_final)
```

```python
import jax
import jax.numpy as jnp
from jax import lax
from jax.experimental import pallas as pl
from jax.experimental.pallas import tpu as pltpu

F32 = jnp.float32
BF16 = jnp.bfloat16
I32 = jnp.int32

N_HEADS = 4
HEAD_DIM = 128
D_MLSTM = N_HEADS * HEAD_DIM
D_CONV = 512
N_GROUPS = 4
EXPERTS_PER_GROUP = 8
N_EXPERTS = N_GROUPS * EXPERTS_PER_GROUP
D_FF = 512
RMS_EPS = 1e-6
Q_SCALE = HEAD_DIM ** -0.5

LANES = 128
CHUNK = LANES
ROUTER_COLS = 128
EXPERT_COL0 = 8

TM_INPROJ = 512
TS_MIXER = 512
TD_DISPATCH = 512
TM_EXPERT = 256
TC_COMBINE = 256

VMEM_LIMIT = 56 * 1024 * 1024


def _nt_dot(a, b):
    return lax.dot_general(a, b, (((1,), (1,)), ((), ())), preferred_element_type=F32)


def _lane_cumsum(x, lane):
    s = 1
    while s < LANES:
        x = x + jnp.where(lane >= s, pltpu.roll(x, s, axis=1), jnp.zeros_like(x))
        s *= 2
    return x


def _inproj_kernel(x_ref, g_ref, wa_ref, ba_ref, wk_ref, bk_ref, wg_ref, bg_ref,
                   a_ref, kt_ref, gt_ref):
    x = x_ref[...]
    ms = jnp.mean(x * x, axis=-1, keepdims=True)
    hn = (x * lax.rsqrt(ms + RMS_EPS) * g_ref[...]).astype(BF16)
    n_blk = wa_ref.shape[1] // D_MLSTM
    for j in range(n_blk):
        sl = slice(j * D_MLSTM, (j + 1) * D_MLSTM)
        acc = jnp.dot(hn, wa_ref[:, sl], preferred_element_type=F32) + ba_ref[:, sl]
        if j == 0:
            acc = acc * Q_SCALE
        a_ref[:, sl] = acc.astype(BF16)
    kt_ref[...] = (_nt_dot(wk_ref[...], hn) + bk_ref[...]).astype(BF16)
    gt_ref[...] = _nt_dot(wg_ref[...], hn) + bg_ref[...]


def _inproj(x, norm_mix, w_in, b_in):
    b_sz, s_len, d = x.shape
    tm = TM_INPROJ
    dm = D_MLSTM
    g0 = 4 * dm
    c0 = g0 + 2 * N_HEADS
    cols_a = jnp.concatenate([w_in[:, 0:dm], w_in[:, 2 * dm:4 * dm], w_in[:, c0:c0 + 3 * D_CONV]], axis=1)
    bias_a = jnp.concatenate([b_in[0:dm], b_in[2 * dm:4 * dm], b_in[c0:c0 + 3 * D_CONV]])[None, :]
    wk_t = w_in[:, dm:2 * dm].T
    bk = b_in[dm:2 * dm][:, None]
    wg_t = w_in[:, g0:c0].T
    bg = b_in[g0:c0][:, None]
    n_a = cols_a.shape[1]
    return pl.pallas_call(
        _inproj_kernel,
        grid=(b_sz, s_len // tm),
        in_specs=[
            pl.BlockSpec((None, tm, d), lambda b, s: (b, s, 0)),
            pl.BlockSpec((1, d), lambda b, s: (0, 0)),
            pl.BlockSpec((d, n_a), lambda b, s: (0, 0)),
            pl.BlockSpec((1, n_a), lambda b, s: (0, 0)),
            pl.BlockSpec((dm, d), lambda b, s: (0, 0)),
            pl.BlockSpec((dm, 1), lambda b, s: (0, 0)),
            pl.BlockSpec((2 * N_HEADS, d), lambda b, s: (0, 0)),
            pl.BlockSpec((2 * N_HEADS, 1), lambda b, s: (0, 0)),
        ],
        out_specs=[
            pl.BlockSpec((None, tm, n_a), lambda b, s: (b, s, 0)),
            pl.BlockSpec((None, dm, tm), lambda b, s: (b, 0, s)),
            pl.BlockSpec((None, 2 * N_HEADS, tm), lambda b, s: (b, 0, s)),
        ],
        out_shape=[
            jax.ShapeDtypeStruct((b_sz, s_len, n_a), BF16),
            jax.ShapeDtypeStruct((b_sz, dm, s_len), BF16),
            jax.ShapeDtypeStruct((b_sz, 2 * N_HEADS, s_len), F32),
        ],
        compiler_params=pltpu.CompilerParams(
            dimension_semantics=("arbitrary", "arbitrary"), vmem_limit_bytes=VMEM_LIMIT),
        name="inproj",
    )(x, norm_mix[None, :], cols_a.astype(BF16), bias_a, wk_t.astype(BF16), bk, wg_t.astype(BF16), bg)


def _mixer_kernel(q_ref, v_ref, o_ref, cb_ref, cc_ref, cx_ref, kt_ref, gt_ref, x_ref,
                  cw_ref, wo_ref, g2_ref, wr_ref, br_ref,
                  x2_ref, hn2_ref, ri_ref, rw_ref, cnt_out_ref,
                  ct_ref, m_ref, carry_ref, cnt_ref, mix_ref):
    ts = x_ref.shape[0]
    n_chunks = ts // CHUNK
    first_tile = pl.program_id(1) == 0

    @pl.when(first_tile)
    def _():
        ct_ref[...] = jnp.zeros_like(ct_ref)
        m_ref[...] = jnp.zeros_like(m_ref)
        carry_ref[...] = jnp.zeros_like(carry_ref)

    @pl.when(jnp.logical_and(first_tile, pl.program_id(0) == 0))
    def _():
        cnt_ref[...] = jnp.zeros_like(cnt_ref)

    gates = gt_ref[...]
    li_all = gates[0:N_HEADS]
    fg = gates[N_HEADS:2 * N_HEADS]
    lf_all = jnp.minimum(fg, 0.0) - jnp.log1p(jnp.exp(-jnp.abs(fg)))
    lane4 = lax.broadcasted_iota(I32, (N_HEADS, CHUNK), 1)
    row = lax.broadcasted_iota(I32, (CHUNK, CHUNK), 0)
    col = lax.broadcasted_iota(I32, (CHUNK, CHUNK), 1)
    causal = row >= col
    ones_col = (lax.broadcasted_iota(I32, (CHUNK, HEAD_DIM), 1) == 0).astype(BF16)
    neg_inf = jnp.float32(-jnp.inf)

    m_prev = m_ref[0:N_HEADS, 0:1]
    for c in range(n_chunks):
        rows = slice(c * CHUNK, (c + 1) * CHUNK)
        li = li_all[:, rows]
        lf = lf_all[:, rows]
        b_cum = _lane_cumsum(lf, lane4)
        g_tot = b_cum[:, CHUNK - 1:CHUNK]
        a_end = g_tot - b_cum + li
        m_loc = jnp.max(a_end, axis=1, keepdims=True)
        u = li - b_cum
        m_new = jnp.maximum(g_tot + m_prev, m_loc)
        decay = jnp.exp(g_tot + m_prev - m_new)
        s_fac = jnp.exp(m_loc - m_new)
        w_key = jnp.exp(a_end - m_loc)
        for h in range(N_HEADS):
            hc = slice(h * HEAD_DIM, (h + 1) * HEAD_DIM)
            qh = q_ref[rows, hc]
            kth = kt_ref[hc, rows]
            v_aug = jnp.concatenate([v_ref[rows, hc], ones_col], axis=1)
            s_qk = jnp.dot(qh, kth, preferred_element_type=F32)
            u_m = jnp.where(causal, u[h:h + 1, :], neg_inf)
            m_row = jnp.max(u_m, axis=1, keepdims=True)
            p = (s_qk * jnp.exp(u_m - m_row)).astype(BF16)
            r_intra = jnp.dot(p, v_aug, preferred_element_type=F32)
            ct_old = ct_ref[h]
            r_inter = jnp.dot(qh, ct_old.astype(BF16), preferred_element_type=F32)
            ktw = (kth.astype(F32) * w_key[h:h + 1, :]).astype(BF16)
            kv = jnp.dot(ktw, v_aug, preferred_element_type=F32)
            ct_ref[h] = decay[h:h + 1, :] * ct_old + s_fac[h:h + 1, :] * kv

            mp = m_prev[h:h + 1, :]
            m_tot = jnp.maximum(m_row, mp)
            f_intra = jnp.exp(m_row - m_tot)
            f_inter = jnp.exp(mp - m_tot)
            b_col = jnp.sum(jnp.where(causal, lf[h:h + 1, :], 0.0), axis=1, keepdims=True)
            num = f_intra * r_intra[:, 0:HEAD_DIM] + f_inter * r_inter[:, 0:HEAD_DIM]
            den = (f_intra * r_intra[:, HEAD_DIM:HEAD_DIM + 1]
                   + f_inter * r_inter[:, HEAD_DIM:HEAD_DIM + 1])
            inv = 1.0 / jnp.maximum(jnp.abs(den), jnp.exp(-b_col - m_tot))
            o_gate = jax.nn.sigmoid(o_ref[rows, hc].astype(F32))
            mix_ref[rows, hc] = (o_gate * (num * inv)).astype(BF16)
        m_prev = m_new
    m_ref[0:N_HEADS, :] = jnp.broadcast_to(m_prev, (N_HEADS, LANES))

    u_c = cc_ref[...].astype(F32) * cx_ref[...].astype(F32)
    rid = lax.broadcasted_iota(I32, u_c.shape, 0)
    prev1 = carry_ref[0:1, :]
    prev2 = carry_ref[1:2, :]
    u1 = jnp.where(rid == 0, prev1, pltpu.roll(u_c, 1, axis=0))
    u2 = jnp.where(rid == 0, prev2, jnp.where(rid == 1, prev1, pltpu.roll(u_c, 2, axis=0)))
    z = cw_ref[0:1, :] * u2 + cw_ref[1:2, :] * u1 + cw_ref[2:3, :] * u_c
    mix_ref[:, D_MLSTM:] = (cb_ref[...].astype(F32) * z).astype(BF16)
    carry_ref[0:1, :] = u_c[ts - 1:ts, :]
    carry_ref[1:2, :] = u_c[ts - 2:ts - 1, :]

    x2 = x_ref[...] + jnp.dot(mix_ref[...], wo_ref[...], preferred_element_type=F32)
    x2_ref[...] = x2
    ms = jnp.mean(x2 * x2, axis=-1, keepdims=True)
    hn2 = x2 * lax.rsqrt(ms + RMS_EPS) * g2_ref[...]
    hn2_ref[...] = hn2

    logits = jnp.dot(hn2, wr_ref[...], preferred_element_type=F32,
                     precision=lax.Precision.HIGHEST) + br_ref[...]
    lt = jnp.transpose(logits)
    gl = lt[0:N_GROUPS]
    gmax = jnp.max(gl, axis=0, keepdims=True)
    gi = lax.broadcasted_iota(I32, gl.shape, 0)
    g_sel = jnp.min(jnp.where(gl == gmax, gi, N_GROUPS), axis=0, keepdims=True)
    p_g = 1.0 / jnp.sum(jnp.exp(gl - gmax), axis=0, keepdims=True)
    in_g = lt[EXPERT_COL0:EXPERT_COL0 + EXPERTS_PER_GROUP]
    for g in range(1, N_GROUPS):
        lo = EXPERT_COL0 + g * EXPERTS_PER_GROUP
        in_g = jnp.where(g_sel == g, lt[lo:lo + EXPERTS_PER_GROUP], in_g)
    ei = lax.broadcasted_iota(I32, in_g.shape, 0)
    v1 = jnp.max(in_g, axis=0, keepdims=True)
    i1 = jnp.min(jnp.where(in_g == v1, ei, EXPERTS_PER_GROUP), axis=0, keepdims=True)
    rest = jnp.where(ei == i1, neg_inf, in_g)
    v2 = jnp.max(rest, axis=0, keepdims=True)
    i2 = jnp.min(jnp.where(rest == v2, ei, EXPERTS_PER_GROUP), axis=0, keepdims=True)
    t = jnp.exp(v2 - v1)
    w1 = p_g / (1.0 + t)
    w2 = p_g * t / (1.0 + t)
    e1 = g_sel * EXPERTS_PER_GROUP + i1
    e2 = g_sel * EXPERTS_PER_GROUP + i2

    eio = lax.broadcasted_iota(I32, (N_EXPERTS, LANES), 0)
    lane_e = lax.broadcasted_iota(I32, (N_EXPERTS, LANES), 1)
    base = cnt_ref[:, 0:1]
    zero_i = jnp.zeros((1, LANES), I32)
    for k in range(ts // LANES):
        ls = slice(k * LANES, (k + 1) * LANES)
        is1 = eio == e1[:, ls]
        is2 = eio == e2[:, ls]
        onehot = jnp.logical_or(is1, is2).astype(I32)
        incl = _lane_cumsum(onehot, lane_e)
        rank_e = base + incl - onehot
        r1 = jnp.sum(jnp.where(is1, rank_e, 0), axis=0, keepdims=True)
        r2 = jnp.sum(jnp.where(is2, rank_e, 0), axis=0, keepdims=True)
        ri_ref[:, ls] = jnp.concatenate(
            [e1[:, ls], e2[:, ls], r1, r2, zero_i, zero_i, zero_i, zero_i], axis=0)
        base = base + incl[:, LANES - 1:LANES]
    zero_f = jnp.zeros_like(w1)
    rw_ref[...] = jnp.concatenate([w1, w2, zero_f, zero_f, zero_f, zero_f, zero_f, zero_f], axis=0)
    cnt_new = jnp.broadcast_to(base, (N_EXPERTS, LANES))
    cnt_ref[...] = cnt_new
    cnt_out_ref[...] = cnt_new


def _mixer(x, a, kt, gt, conv_w, w_out, norm_ffn, w_router, b_router):
    b_sz, s_len, d = x.shape
    ts = TS_MIXER
    dm = D_MLSTM

    def a_spec(j):
        return pl.BlockSpec((None, ts, dm), lambda b, s, j=j: (b, s, j))

    tok = lambda w: pl.BlockSpec((None, ts, w), lambda b, s: (b, s, 0))
    rowsp = pl.BlockSpec((None, 2 * N_HEADS, ts), lambda b, s: (b, 0, s))
    const = lambda shape: pl.BlockSpec(shape, lambda b, s: tuple(0 for _ in shape))
    return pl.pallas_call(
        _mixer_kernel,
        grid=(b_sz, s_len // ts),
        in_specs=[a_spec(0), a_spec(1), a_spec(2), a_spec(3), a_spec(4), a_spec(5),
                  pl.BlockSpec((None, dm, ts), lambda b, s: (b, 0, s)),
                  rowsp, tok(d),
                  const((3, D_CONV)), const((d, d)), const((1, d)),
                  const((d, ROUTER_COLS)), const((1, ROUTER_COLS))],
        out_specs=[tok(d), tok(d), rowsp, rowsp, const((N_EXPERTS, LANES))],
        out_shape=[
            jax.ShapeDtypeStruct((b_sz, s_len, d), F32),
            jax.ShapeDtypeStruct((b_sz, s_len, d), F32),
            jax.ShapeDtypeStruct((b_sz, 2 * N_HEADS, s_len), I32),
            jax.ShapeDtypeStruct((b_sz, 2 * N_HEADS, s_len), F32),
            jax.ShapeDtypeStruct((N_EXPERTS, LANES), I32),
        ],
        scratch_shapes=[
            pltpu.VMEM((N_HEADS, HEAD_DIM, 2 * HEAD_DIM), F32),
            pltpu.VMEM((2 * N_HEADS, LANES), F32),
            pltpu.VMEM((8, D_CONV), F32),
            pltpu.VMEM((N_EXPERTS, LANES), I32),
            pltpu.VMEM((ts, d), BF16),
        ],
        compiler_params=pltpu.CompilerParams(
            dimension_semantics=("arbitrary", "arbitrary"), vmem_limit_bytes=VMEM_LIMIT),
        name="mixer",
    )(a, a, a, a, a, a, kt, gt, x, conv_w, w_out.astype(BF16), norm_ffn[None, :], w_router, b_router)


def _dispatch_kernel(pstart_ref, ri_ref, h_ref, xs_in_ref, xs_ref, sem):
    del xs_in_ref
    td = h_ref.shape[0]

    def copies(r):
        p1 = pstart_ref[ri_ref[0, r]] + ri_ref[2, r]
        p2 = pstart_ref[ri_ref[1, r]] + ri_ref[3, r]
        src = h_ref.at[pl.ds(r, 1)]
        return (pltpu.make_async_copy(src, xs_ref.at[pl.ds(p1, 1)], sem),
                pltpu.make_async_copy(src, xs_ref.at[pl.ds(p2, 1)], sem))

    def start(r, carry):
        c1, c2 = copies(r)
        c1.start()
        c2.start()
        return carry

    def wait(r, carry):
        c1, c2 = copies(r)
        c1.wait()
        c2.wait()
        return carry

    lax.fori_loop(0, td, start, 0)
    lax.fori_loop(0, td, wait, 0)


def _dispatch(hn2, ri, pstart, cap):
    b_sz, s_len, d = hn2.shape
    td = TD_DISPATCH
    n_s = s_len // td
    xs0 = jnp.zeros((cap, d), hn2.dtype)
    return pl.pallas_call(
        _dispatch_kernel,
        grid_spec=pltpu.PrefetchScalarGridSpec(
            num_scalar_prefetch=1,
            grid=(b_sz, n_s),
            in_specs=[
                pl.BlockSpec((None, 2 * N_HEADS, td), lambda b, s, ps: (b, 0, s), memory_space=pltpu.SMEM),
                pl.BlockSpec((td, d), lambda b, s, ps: (b * n_s + s, 0)),
                pl.BlockSpec(memory_space=pl.ANY),
            ],
            out_specs=pl.BlockSpec(memory_space=pl.ANY),
            scratch_shapes=[pltpu.SemaphoreType.DMA(())],
        ),
        out_shape=jax.ShapeDtypeStruct((cap, d), hn2.dtype),
        input_output_aliases={3: 0},
        compiler_params=pltpu.CompilerParams(
            dimension_semantics=("arbitrary", "arbitrary"), vmem_limit_bytes=VMEM_LIMIT),
        name="dispatch",
    )(pstart, ri, hn2.reshape(b_sz * s_len, d), xs0)


def _experts_kernel(te_ref, nu_ref, xs_ref, wg_ref, wu_ref, wd_ref, y_ref, wgb_ref, wub_ref, wdb_ref):
    i = pl.program_id(0)
    e = te_ref[i]
    e_prev = te_ref[jnp.maximum(i - 1, 0)]
    new_expert = jnp.logical_or(i == 0, e != e_prev)
    used = i < nu_ref[0]

    @pl.when(jnp.logical_and(used, new_expert))
    def _():
        wgb_ref[...] = wg_ref[...].astype(BF16)
        wub_ref[...] = wu_ref[...].astype(BF16)
        wdb_ref[...] = wd_ref[...].astype(BF16)

    @pl.when(used)
    def _():
        xb = xs_ref[...].astype(BF16)
        gate = jnp.dot(xb, wgb_ref[...], preferred_element_type=F32)
        up = jnp.dot(xb, wub_ref[...], preferred_element_type=F32)
        hid = (gate * jax.nn.sigmoid(gate) * up).astype(BF16)
        y_ref[...] = jnp.dot(hid, wdb_ref[...], preferred_element_type=F32)

    @pl.when(jnp.logical_not(used))
    def _():
        y_ref[...] = jnp.zeros_like(y_ref)


def _experts(xs, tile_e, n_used, w_gate, w_up, w_down):
    cap, d = xs.shape
    tm = TM_EXPERT
    n_tiles = cap // tm

    def x_map(i, te, nu):
        return (jnp.minimum(i, jnp.maximum(nu[0] - 1, 0)), 0)

    def w_map(i, te, nu):
        return (te[i], 0, 0)

    return pl.pallas_call(
        _experts_kernel,
        grid_spec=pltpu.PrefetchScalarGridSpec(
            num_scalar_prefetch=2,
            grid=(n_tiles,),
            in_specs=[
                pl.BlockSpec((tm, d), x_map),
                pl.BlockSpec((None, d, D_FF), w_map),
                pl.BlockSpec((None, d, D_FF), w_map),
                pl.BlockSpec((None, D_FF, d), w_map),
            ],
            out_specs=pl.BlockSpec((tm, d), lambda i, te, nu: (i, 0)),
            scratch_shapes=[pltpu.VMEM((d, D_FF), BF16), pltpu.VMEM((d, D_FF), BF16),
                            pltpu.VMEM((D_FF, d), BF16)],
        ),
        out_shape=jax.ShapeDtypeStruct((cap, d), F32),
        compiler_params=pltpu.CompilerParams(
            dimension_semantics=("arbitrary",), vmem_limit_bytes=VMEM_LIMIT),
        name="experts",
    )(tile_e, n_used, xs, w_gate, w_up, w_down)


def _combine_kernel(pstart_ref, ri_ref, x2_ref, rw_ref, gf_ref, y_ref, out_ref, y1_ref, y2_ref, sem):
    tc = x2_ref.shape[0]

    def copies(r):
        p1 = pstart_ref[ri_ref[0, r]] + ri_ref[2, r]
        p2 = pstart_ref[ri_ref[1, r]] + ri_ref[3, r]
        return (pltpu.make_async_copy(y_ref.at[pl.ds(p1, 1)], y1_ref.at[pl.ds(r, 1)], sem),
                pltpu.make_async_copy(y_ref.at[pl.ds(p2, 1)], y2_ref.at[pl.ds(r, 1)], sem))

    def start(r, carry):
        c1, c2 = copies(r)
        c1.start()
        c2.start()
        return carry

    def wait(r, carry):
        c1, c2 = copies(r)
        c1.wait()
        c2.wait()
        return carry

    lax.fori_loop(0, tc, start, 0)
    w_t = jnp.transpose(rw_ref[...])
    lax.fori_loop(0, tc, wait, 0)
    x3 = x2_ref[...] + w_t[:, 0:1] * y1_ref[...] + w_t[:, 1:2] * y2_ref[...]
    ms = jnp.mean(x3 * x3, axis=-1, keepdims=True)
    out_ref[...] = x3 * lax.rsqrt(ms + RMS_EPS) * gf_ref[...]


def _combine(x2, ri, rw, y, pstart, norm_final):
    b_sz, s_len, d = x2.shape
    tc = TC_COMBINE
    return pl.pallas_call(
        _combine_kernel,
        grid_spec=pltpu.PrefetchScalarGridSpec(
            num_scalar_prefetch=1,
            grid=(b_sz, s_len // tc),
            in_specs=[
                pl.BlockSpec((None, 2 * N_HEADS, tc), lambda b, s, ps: (b, 0, s), memory_space=pltpu.SMEM),
                pl.BlockSpec((None, tc, d), lambda b, s, ps: (b, s, 0)),
                pl.BlockSpec((None, 2 * N_HEADS, tc), lambda b, s, ps: (b, 0, s)),
                pl.BlockSpec((1, d), lambda b, s, ps: (0, 0)),
                pl.BlockSpec(memory_space=pl.ANY),
            ],
            out_specs=pl.BlockSpec((None, tc, d), lambda b, s, ps: (b, s, 0)),
            scratch_shapes=[pltpu.VMEM((tc, d), F32), pltpu.VMEM((tc, d), F32),
                            pltpu.SemaphoreType.DMA(())],
        ),
        out_shape=jax.ShapeDtypeStruct((b_sz, s_len, d), F32),
        compiler_params=pltpu.CompilerParams(
            dimension_semantics=("arbitrary", "arbitrary"), vmem_limit_bytes=VMEM_LIMIT),
        name="combine",
    )(pstart, ri, x2, rw, norm_final[None, :], y)


def _layer(x, norm_mix, w_in, b_in, conv_w, w_out, norm_ffn, w_group, b_group, w_expert, b_expert,
           w_gate, w_up, w_down, norm_out):
    b_sz, s_len, d = x.shape
    n_tok = b_sz * s_len
    a, kt, gt = _inproj(x, norm_mix, w_in, b_in)

    w_router = jnp.zeros((d, ROUTER_COLS), F32)
    w_router = w_router.at[:, 0:N_GROUPS].set(w_group).at[:, EXPERT_COL0:EXPERT_COL0 + N_EXPERTS].set(w_expert)
    b_router = jnp.zeros((1, ROUTER_COLS), F32)
    b_router = b_router.at[0, 0:N_GROUPS].set(b_group).at[0, EXPERT_COL0:EXPERT_COL0 + N_EXPERTS].set(b_expert)
    x2, hn2, ri, rw, counts = _mixer(x, a, kt, gt, conv_w, w_out, norm_ffn, w_router, b_router)

    tm = TM_EXPERT
    cnt = counts[:, 0]
    padded = ((cnt + tm - 1) // tm) * tm
    pends = jnp.cumsum(padded)
    pstart = (pends - padded).astype(I32)
    n_tiles = (n_tok * 2) // tm + N_EXPERTS
    cap = n_tiles * tm
    tile_start = jnp.arange(n_tiles, dtype=I32) * tm
    tile_e = jnp.clip(jnp.searchsorted(pends, tile_start, side='right'), 0, N_EXPERTS - 1).astype(I32)
    n_used = (pends[-1:] // tm).astype(I32)

    xs = _dispatch(hn2, ri, pstart, cap)
    y = _experts(xs, tile_e, n_used, w_gate, w_up, w_down)
    return _combine(x2, ri, rw, y, pstart, norm_out)


def kernel(x, norm_mix, w_in, b_in, conv_w, w_out, norm_ffn, w_group, b_group, w_expert, b_expert,
           w_gate, w_up, w_down, norm_final):
    depth = norm_mix.shape[0]
    assert depth == 1, "final-norm fusion below assumes a single layer"
    assert x.shape[-1] == 2 * D_MLSTM and x.shape[1] % TS_MIXER == 0
    return _layer(x, norm_mix[0], w_in[0], b_in[0], conv_w[0], w_out[0], norm_ffn[0],
                  w_group[0], b_group[0], w_expert[0], b_expert[0],
                  w_gate[0], w_up[0], w_down[0], norm_final)
```

```python
import jax
import jax.numpy as jnp
from jax import lax
from jax.experimental import pallas as pl
from jax.experimental.pallas import tpu as pltpu

F32 = jnp.float32
BF16 = jnp.bfloat16
I32 = jnp.int32

N_HEADS = 4
HEAD_DIM = 128
D_MLSTM = N_HEADS * HEAD_DIM
D_CONV = 512
N_GROUPS = 4
EXPERTS_PER_GROUP = 8
N_EXPERTS = N_GROUPS * EXPERTS_PER_GROUP
D_FF = 512
RMS_EPS = 1e-6
Q_SCALE = HEAD_DIM ** -0.5

LANES = 128
CHUNK = LANES
ROUTER_COLS = 128
EXPERT_COL0 = 8

TM_INPROJ = 512
TS_MIXER = 512
TD_DISPATCH = 512
TM_EXPERT = 256
TC_COMBINE = 256
ROW_DMA_UNROLL = 8

VMEM_LIMIT = 56 * 1024 * 1024


def _nt_dot(a, b):
    return lax.dot_general(a, b, (((1,), (1,)), ((), ())), preferred_element_type=F32)


def _lane_cumsum(x, lane):
    s = 1
    while s < LANES:
        x = x + jnp.where(lane >= s, pltpu.roll(x, s, axis=1), jnp.zeros_like(x))
        s *= 2
    return x


def _inproj_kernel(x_ref, g_ref, wa_ref, ba_ref, wk_ref, bk_ref, wg_ref, bg_ref,
                   a_ref, kt_ref, gt_ref):
    x = x_ref[...]
    ms = jnp.mean(x * x, axis=-1, keepdims=True)
    hn = (x * lax.rsqrt(ms + RMS_EPS) * g_ref[...]).astype(BF16)
    n_blk = wa_ref.shape[1] // D_MLSTM
    for j in range(n_blk):
        sl = slice(j * D_MLSTM, (j + 1) * D_MLSTM)
        acc = jnp.dot(hn, wa_ref[:, sl], preferred_element_type=F32) + ba_ref[:, sl]
        if j == 0:
            acc = acc * Q_SCALE
        a_ref[:, sl] = acc.astype(BF16)
    kt_ref[...] = (_nt_dot(wk_ref[...], hn) + bk_ref[...]).astype(BF16)
    gt_ref[...] = _nt_dot(wg_ref[...], hn) + bg_ref[...]


def _inproj(x, norm_mix, w_in, b_in):
    b_sz, s_len, d = x.shape
    tm = TM_INPROJ
    dm = D_MLSTM
    g0 = 4 * dm
    c0 = g0 + 2 * N_HEADS
    cols_a = jnp.concatenate([w_in[:, 0:dm], w_in[:, 2 * dm:4 * dm], w_in[:, c0:c0 + 3 * D_CONV]], axis=1)
    bias_a = jnp.concatenate([b_in[0:dm], b_in[2 * dm:4 * dm], b_in[c0:c0 + 3 * D_CONV]])[None, :]
    wk_t = w_in[:, dm:2 * dm].T
    bk = b_in[dm:2 * dm][:, None]
    wg_t = w_in[:, g0:c0].T
    bg = b_in[g0:c0][:, None]
    n_a = cols_a.shape[1]
    return pl.pallas_call(
        _inproj_kernel,
        grid=(b_sz, s_len // tm),
        in_specs=[
            pl.BlockSpec((None, tm, d), lambda b, s: (b, s, 0)),
            pl.BlockSpec((1, d), lambda b, s: (0, 0)),
            pl.BlockSpec((d, n_a), lambda b, s: (0, 0)),
            pl.BlockSpec((1, n_a), lambda b, s: (0, 0)),
            pl.BlockSpec((dm, d), lambda b, s: (0, 0)),
            pl.BlockSpec((dm, 1), lambda b, s: (0, 0)),
            pl.BlockSpec((2 * N_HEADS, d), lambda b, s: (0, 0)),
            pl.BlockSpec((2 * N_HEADS, 1), lambda b, s: (0, 0)),
        ],
        out_specs=[
            pl.BlockSpec((None, tm, n_a), lambda b, s: (b, s, 0)),
            pl.BlockSpec((None, dm, tm), lambda b, s: (b, 0, s)),
            pl.BlockSpec((None, 2 * N_HEADS, tm), lambda b, s: (b, 0, s)),
        ],
        out_shape=[
            jax.ShapeDtypeStruct((b_sz, s_len, n_a), BF16),
            jax.ShapeDtypeStruct((b_sz, dm, s_len), BF16),
            jax.ShapeDtypeStruct((b_sz, 2 * N_HEADS, s_len), F32),
        ],
        compiler_params=pltpu.CompilerParams(
            dimension_semantics=("arbitrary", "arbitrary"), vmem_limit_bytes=VMEM_LIMIT),
        name="inproj",
    )(x, norm_mix[None, :], cols_a.astype(BF16), bias_a, wk_t.astype(BF16), bk, wg_t.astype(BF16), bg)


def _mixer_kernel(q_ref, v_ref, o_ref, cb_ref, cc_ref, cx_ref, kt_ref, gt_ref, x_ref,
                  cw_ref, wo_ref, g2_ref, wr_ref, br_ref,
                  x2_ref, hn2_ref, ri_ref, rw_ref, cnt_out_ref,
                  ct_ref, m_ref, carry_ref, cnt_ref, mix_ref):
    ts = x_ref.shape[0]
    n_chunks = ts // CHUNK
    first_tile = pl.program_id(1) == 0

    @pl.when(first_tile)
    def _():
        ct_ref[...] = jnp.zeros_like(ct_ref)
        m_ref[...] = jnp.zeros_like(m_ref)
        carry_ref[...] = jnp.zeros_like(carry_ref)

    @pl.when(jnp.logical_and(first_tile, pl.program_id(0) == 0))
    def _():
        cnt_ref[...] = jnp.zeros_like(cnt_ref)

    gates = gt_ref[...]
    li_all = gates[0:N_HEADS]
    fg = gates[N_HEADS:2 * N_HEADS]
    lf_all = jnp.minimum(fg, 0.0) - jnp.log1p(jnp.exp(-jnp.abs(fg)))
    lane4 = lax.broadcasted_iota(I32, (N_HEADS, CHUNK), 1)
    row = lax.broadcasted_iota(I32, (CHUNK, CHUNK), 0)
    col = lax.broadcasted_iota(I32, (CHUNK, CHUNK), 1)
    causal = row >= col
    ones_col = (lax.broadcasted_iota(I32, (CHUNK, HEAD_DIM), 1) == 0).astype(BF16)
    neg_inf = jnp.float32(-jnp.inf)

    m_prev = m_ref[0:N_HEADS, 0:1]
    for c in range(n_chunks):
        rows = slice(c * CHUNK, (c + 1) * CHUNK)
        li = li_all[:, rows]
        lf = lf_all[:, rows]
        b_cum = _lane_cumsum(lf, lane4)
        g_tot = b_cum[:, CHUNK - 1:CHUNK]
        a_end = g_tot - b_cum + li
        m_loc = jnp.max(a_end, axis=1, keepdims=True)
        u = li - b_cum
        m_new = jnp.maximum(g_tot + m_prev, m_loc)
        decay = jnp.exp(g_tot + m_prev - m_new)
        s_fac = jnp.exp(m_loc - m_new)
        w_key = jnp.exp(a_end - m_loc)
        for h in range(N_HEADS):
            hc = slice(h * HEAD_DIM, (h + 1) * HEAD_DIM)
            qh = q_ref[rows, hc]
            kth = kt_ref[hc, rows]
            v_aug = jnp.concatenate([v_ref[rows, hc], ones_col], axis=1)
            s_qk = jnp.dot(qh, kth, preferred_element_type=F32)
            u_m = jnp.where(causal, u[h:h + 1, :], neg_inf)
            m_row = jnp.max(u_m, axis=1, keepdims=True)
            p = (s_qk * jnp.exp(u_m - m_row)).astype(BF16)
            r_intra = jnp.dot(p, v_aug, preferred_element_type=F32)
            ct_old = ct_ref[h]
            r_inter = jnp.dot(qh, ct_old.astype(BF16), preferred_element_type=F32)
            ktw = (kth.astype(F32) * w_key[h:h + 1, :]).astype(BF16)
            kv = jnp.dot(ktw, v_aug, preferred_element_type=F32)
            ct_ref[h] = decay[h:h + 1, :] * ct_old + s_fac[h:h + 1, :] * kv

            mp = m_prev[h:h + 1, :]
            m_tot = jnp.maximum(m_row, mp)
            f_intra = jnp.exp(m_row - m_tot)
            f_inter = jnp.exp(mp - m_tot)
            b_col = jnp.sum(jnp.where(causal, lf[h:h + 1, :], 0.0), axis=1, keepdims=True)
            num = f_intra * r_intra[:, 0:HEAD_DIM] + f_inter * r_inter[:, 0:HEAD_DIM]
            den = (f_intra * r_intra[:, HEAD_DIM:HEAD_DIM + 1]
                   + f_inter * r_inter[:, HEAD_DIM:HEAD_DIM + 1])
            inv = 1.0 / jnp.maximum(jnp.abs(den), jnp.exp(-b_col - m_tot))
            o_gate = jax.nn.sigmoid(o_ref[rows, hc].astype(F32))
            mix_ref[rows, hc] = (o_gate * (num * inv)).astype(BF16)
        m_prev = m_new
    m_ref[0:N_HEADS, :] = jnp.broadcast_to(m_prev, (N_HEADS, LANES))

    u_c = cc_ref[...].astype(F32) * cx_ref[...].astype(F32)
    rid = lax.broadcasted_iota(I32, u_c.shape, 0)
    prev1 = carry_ref[0:1, :]
    prev2 = carry_ref[1:2, :]
    u1 = jnp.where(rid == 0, prev1, pltpu.roll(u_c, 1, axis=0))
    u2 = jnp.where(rid == 0, prev2, jnp.where(rid == 1, prev1, pltpu.roll(u_c, 2, axis=0)))
    z = cw_ref[0:1, :] * u2 + cw_ref[1:2, :] * u1 + cw_ref[2:3, :] * u_c
    mix_ref[:, D_MLSTM:] = (cb_ref[...].astype(F32) * z).astype(BF16)
    carry_ref[0:1, :] = u_c[ts - 1:ts, :]
    carry_ref[1:2, :] = u_c[ts - 2:ts - 1, :]

    x2 = x_ref[...] + jnp.dot(mix_ref[...], wo_ref[...], preferred_element_type=F32)
    x2_ref[...] = x2
    ms = jnp.mean(x2 * x2, axis=-1, keepdims=True)
    hn2 = x2 * lax.rsqrt(ms + RMS_EPS) * g2_ref[...]
    hn2_ref[...] = hn2

    logits = jnp.dot(hn2, wr_ref[...], preferred_element_type=F32,
                     precision=lax.Precision.HIGHEST) + br_ref[...]
    lt = jnp.transpose(logits)
    gl = lt[0:N_GROUPS]
    gmax = jnp.max(gl, axis=0, keepdims=True)
    gi = lax.broadcasted_iota(I32, gl.shape, 0)
    g_sel = jnp.min(jnp.where(gl == gmax, gi, N_GROUPS), axis=0, keepdims=True)
    p_g = 1.0 / jnp.sum(jnp.exp(gl - gmax), axis=0, keepdims=True)
    in_g = lt[EXPERT_COL0:EXPERT_COL0 + EXPERTS_PER_GROUP]
    for g in range(1, N_GROUPS):
        lo = EXPERT_COL0 + g * EXPERTS_PER_GROUP
        in_g = jnp.where(g_sel == g, lt[lo:lo + EXPERTS_PER_GROUP], in_g)
    ei = lax.broadcasted_iota(I32, in_g.shape, 0)
    v1 = jnp.max(in_g, axis=0, keepdims=True)
    i1 = jnp.min(jnp.where(in_g == v1, ei, EXPERTS_PER_GROUP), axis=0, keepdims=True)
    rest = jnp.where(ei == i1, neg_inf, in_g)
    v2 = jnp.max(rest, axis=0, keepdims=True)
    i2 = jnp.min(jnp.where(rest == v2, ei, EXPERTS_PER_GROUP), axis=0, keepdims=True)
    t = jnp.exp(v2 - v1)
    w1 = p_g / (1.0 + t)
    w2 = p_g * t / (1.0 + t)
    e1 = g_sel * EXPERTS_PER_GROUP + i1
    e2 = g_sel * EXPERTS_PER_GROUP + i2

    eio = lax.broadcasted_iota(I32, (N_EXPERTS, LANES), 0)
    lane_e = lax.broadcasted_iota(I32, (N_EXPERTS, LANES), 1)
    base = cnt_ref[:, 0:1]
    zero_i = jnp.zeros((1, LANES), I32)
    for k in range(ts // LANES):
        ls = slice(k * LANES, (k + 1) * LANES)
        is1 = eio == e1[:, ls]
        is2 = eio == e2[:, ls]
        onehot = jnp.logical_or(is1, is2).astype(I32)
        incl = _lane_cumsum(onehot, lane_e)
        rank_e = base + incl - onehot
        r1 = jnp.sum(jnp.where(is1, rank_e, 0), axis=0, keepdims=True)
        r2 = jnp.sum(jnp.where(is2, rank_e, 0), axis=0, keepdims=True)
        ri_ref[:, ls] = jnp.concatenate(
            [e1[:, ls], e2[:, ls], r1, r2, zero_i, zero_i, zero_i, zero_i], axis=0)
        base = base + incl[:, LANES - 1:LANES]
    zero_f = jnp.zeros_like(w1)
    rw_ref[...] = jnp.concatenate([w1, w2, zero_f, zero_f, zero_f, zero_f, zero_f, zero_f], axis=0)
    cnt_new = jnp.broadcast_to(base, (N_EXPERTS, LANES))
    cnt_ref[...] = cnt_new
    cnt_out_ref[...] = cnt_new


def _mixer(x, a, kt, gt, conv_w, w_out, norm_ffn, w_router, b_router):
    b_sz, s_len, d = x.shape
    ts = TS_MIXER
    dm = D_MLSTM

    def a_spec(j):
        return pl.BlockSpec((None, ts, dm), lambda b, s, j=j: (b, s, j))

    tok = lambda w: pl.BlockSpec((None, ts, w), lambda b, s: (b, s, 0))
    rowsp = pl.BlockSpec((None, 2 * N_HEADS, ts), lambda b, s: (b, 0, s))
    const = lambda shape: pl.BlockSpec(shape, lambda b, s: tuple(0 for _ in shape))
    return pl.pallas_call(
        _mixer_kernel,
        grid=(b_sz, s_len // ts),
        in_specs=[a_spec(0), a_spec(1), a_spec(2), a_spec(3), a_spec(4), a_spec(5),
                  pl.BlockSpec((None, dm, ts), lambda b, s: (b, 0, s)),
                  rowsp, tok(d),
                  const((3, D_CONV)), const((d, d)), const((1, d)),
                  const((d, ROUTER_COLS)), const((1, ROUTER_COLS))],
        out_specs=[tok(d), tok(d), rowsp, rowsp, const((N_EXPERTS, LANES))],
        out_shape=[
            jax.ShapeDtypeStruct((b_sz, s_len, d), F32),
            jax.ShapeDtypeStruct((b_sz, s_len, d), F32),
            jax.ShapeDtypeStruct((b_sz, 2 * N_HEADS, s_len), I32),
            jax.ShapeDtypeStruct((b_sz, 2 * N_HEADS, s_len), F32),
            jax.ShapeDtypeStruct((N_EXPERTS, LANES), I32),
        ],
        scratch_shapes=[
            pltpu.VMEM((N_HEADS, HEAD_DIM, 2 * HEAD_DIM), F32),
            pltpu.VMEM((2 * N_HEADS, LANES), F32),
            pltpu.VMEM((8, D_CONV), F32),
            pltpu.VMEM((N_EXPERTS, LANES), I32),
            pltpu.VMEM((ts, d), BF16),
        ],
        compiler_params=pltpu.CompilerParams(
            dimension_semantics=("arbitrary", "arbitrary"), vmem_limit_bytes=VMEM_LIMIT),
        name="mixer",
    )(a, a, a, a, a, a, kt, gt, x, conv_w, w_out.astype(BF16), norm_ffn[None, :], w_router, b_router)


def _dispatch_kernel(seg_ref, pos_ref, h_ref, xs_ref, zero_ref, sem, zsem):
    td = h_ref.shape[0]
    tm = zero_ref.shape[0]

    @pl.when(jnp.logical_and(pl.program_id(0) == 0, pl.program_id(1) == 0))
    def _():
        zero_ref[...] = jnp.zeros_like(zero_ref)

        def zero_copy(e):
            last_tile = pl.multiple_of(seg_ref[e + 1] - tm, tm)
            return pltpu.make_async_copy(zero_ref, xs_ref.at[pl.ds(last_tile, tm)], zsem)

        for e in range(N_EXPERTS):
            @pl.when(seg_ref[e + 1] > seg_ref[e])
            def _(e=e):
                zero_copy(e).start()
        for e in range(N_EXPERTS):
            @pl.when(seg_ref[e + 1] > seg_ref[e])
            def _(e=e):
                zero_copy(e).wait()

        def tail_copy(t):
            return pltpu.make_async_copy(zero_ref, xs_ref.at[pl.ds(pl.multiple_of(t * tm, tm), tm)], zsem)

        first_unused = seg_ref[N_EXPERTS] // tm
        n_tiles = xs_ref.shape[0] // tm
        lax.fori_loop(first_unused, n_tiles, lambda t, c: (tail_copy(t).start(), c)[1], 0)
        lax.fori_loop(first_unused, n_tiles, lambda t, c: (tail_copy(t).wait(), c)[1], 0)

    def start(i, carry):
        for j in range(ROW_DMA_UNROLL):
            r = i * ROW_DMA_UNROLL + j
            src = h_ref.at[pl.ds(r, 1)]
            pltpu.make_async_copy(src, xs_ref.at[pl.ds(pos_ref[0, r], 1)], sem).start()
            pltpu.make_async_copy(src, xs_ref.at[pl.ds(pos_ref[1, r], 1)], sem).start()
        return carry

    lax.fori_loop(0, td // ROW_DMA_UNROLL, start, 0)
    for _ in range(2):
        pltpu.make_async_copy(h_ref, xs_ref.at[pl.ds(0, td)], sem).wait()


def _dispatch(hn2, pos, seg, cap):
    b_sz, s_len, d = hn2.shape
    td = TD_DISPATCH
    n_s = s_len // td
    return pl.pallas_call(
        _dispatch_kernel,
        grid_spec=pltpu.PrefetchScalarGridSpec(
            num_scalar_prefetch=1,
            grid=(b_sz, n_s),
            in_specs=[
                pl.BlockSpec((None, 2, td), lambda b, s, sg: (b, 0, s), memory_space=pltpu.SMEM),
                pl.BlockSpec((td, d), lambda b, s, sg: (b * n_s + s, 0)),
            ],
            out_specs=pl.BlockSpec(memory_space=pl.ANY),
            scratch_shapes=[pltpu.VMEM((TM_EXPERT, d), hn2.dtype),
                            pltpu.SemaphoreType.DMA(()), pltpu.SemaphoreType.DMA(())],
        ),
        out_shape=jax.ShapeDtypeStruct((cap, d), hn2.dtype),
        compiler_params=pltpu.CompilerParams(
            dimension_semantics=("arbitrary", "arbitrary"), vmem_limit_bytes=VMEM_LIMIT),
        name="dispatch",
    )(seg, pos, hn2.reshape(b_sz * s_len, d))


def _experts_kernel(te_ref, nu_ref, xs_ref, wg_ref, wu_ref, wd_ref, y_ref, wgb_ref, wub_ref, wdb_ref):
    i = pl.program_id(0)
    e = te_ref[i]
    e_prev = te_ref[jnp.maximum(i - 1, 0)]
    new_expert = jnp.logical_or(i == 0, e != e_prev)
    used = i < nu_ref[0]

    @pl.when(jnp.logical_and(used, new_expert))
    def _():
        wgb_ref[...] = wg_ref[...].astype(BF16)
        wub_ref[...] = wu_ref[...].astype(BF16)
        wdb_ref[...] = wd_ref[...].astype(BF16)

    @pl.when(used)
    def _():
        xb = xs_ref[...].astype(BF16)
        gate = jnp.dot(xb, wgb_ref[...], preferred_element_type=F32)
        up = jnp.dot(xb, wub_ref[...], preferred_element_type=F32)
        hid = (gate * jax.nn.sigmoid(gate) * up).astype(BF16)
        y_ref[...] = jnp.dot(hid, wdb_ref[...], preferred_element_type=F32)

    @pl.when(jnp.logical_not(used))
    def _():
        y_ref[...] = jnp.zeros_like(y_ref)


def _experts(xs, tile_e, n_used, w_gate, w_up, w_down):
    cap, d = xs.shape
    tm = TM_EXPERT
    n_tiles = cap // tm

    def x_map(i, te, nu):
        return (jnp.minimum(i, jnp.maximum(nu[0] - 1, 0)), 0)

    def w_map(i, te, nu):
        return (te[i], 0, 0)

    return pl.pallas_call(
        _experts_kernel,
        grid_spec=pltpu.PrefetchScalarGridSpec(
            num_scalar_prefetch=2,
            grid=(n_tiles,),
            in_specs=[
                pl.BlockSpec((tm, d), x_map),
                pl.BlockSpec((None, d, D_FF), w_map),
                pl.BlockSpec((None, d, D_FF), w_map),
                pl.BlockSpec((None, D_FF, d), w_map),
            ],
            out_specs=pl.BlockSpec((tm, d), lambda i, te, nu: (i, 0)),
            scratch_shapes=[pltpu.VMEM((d, D_FF), BF16), pltpu.VMEM((d, D_FF), BF16),
                            pltpu.VMEM((D_FF, d), BF16)],
        ),
        out_shape=jax.ShapeDtypeStruct((cap, d), F32),
        compiler_params=pltpu.CompilerParams(
            dimension_semantics=("arbitrary",), vmem_limit_bytes=VMEM_LIMIT),
        name="experts",
    )(tile_e, n_used, xs, w_gate, w_up, w_down)


def _combine_kernel(pos_ref, pos_next_ref, x2_ref, rw_ref, gf_ref, y_ref, out_ref, y1_ref, y2_ref, sem):
    tc = x2_ref.shape[0]
    step = pl.program_id(0) * pl.num_programs(1) + pl.program_id(1)
    n_steps = pl.num_programs(0) * pl.num_programs(1)
    slot = lax.rem(step, 2)

    def gather(p_ref, slot):
        def body(i, carry):
            for j in range(ROW_DMA_UNROLL):
                r = i * ROW_DMA_UNROLL + j
                pltpu.make_async_copy(y_ref.at[pl.ds(p_ref[0, r], 1)],
                                      y1_ref.at[slot, pl.ds(r, 1)], sem.at[slot]).start()
                pltpu.make_async_copy(y_ref.at[pl.ds(p_ref[1, r], 1)],
                                      y2_ref.at[slot, pl.ds(r, 1)], sem.at[slot]).start()
            return carry
        lax.fori_loop(0, tc // ROW_DMA_UNROLL, body, 0)

    @pl.when(step == 0)
    def _():
        gather(pos_ref, 0)

    @pl.when(step + 1 < n_steps)
    def _():
        gather(pos_next_ref, 1 - slot)

    w_t = jnp.transpose(rw_ref[...])
    pltpu.make_async_copy(y_ref.at[pl.ds(0, tc)], y1_ref.at[slot], sem.at[slot]).wait()
    pltpu.make_async_copy(y_ref.at[pl.ds(0, tc)], y2_ref.at[slot], sem.at[slot]).wait()
    x3 = x2_ref[...] + w_t[:, 0:1] * y1_ref[slot] + w_t[:, 1:2] * y2_ref[slot]
    ms = jnp.mean(x3 * x3, axis=-1, keepdims=True)
    out_ref[...] = x3 * lax.rsqrt(ms + RMS_EPS) * gf_ref[...]


def _combine(x2, pos, rw, y, norm_final):
    b_sz, s_len, d = x2.shape
    tc = TC_COMBINE
    n_s = s_len // tc

    def next_tile(b, s):
        nxt = jnp.minimum(b * n_s + s + 1, b_sz * n_s - 1)
        return (nxt // n_s, 0, nxt % n_s)

    return pl.pallas_call(
        _combine_kernel,
        grid=(b_sz, n_s),
        in_specs=[
            pl.BlockSpec((None, 2, tc), lambda b, s: (b, 0, s), memory_space=pltpu.SMEM),
            pl.BlockSpec((None, 2, tc), next_tile, memory_space=pltpu.SMEM),
            pl.BlockSpec((None, tc, d), lambda b, s: (b, s, 0)),
            pl.BlockSpec((None, 2 * N_HEADS, tc), lambda b, s: (b, 0, s)),
            pl.BlockSpec((1, d), lambda b, s: (0, 0)),
            pl.BlockSpec(memory_space=pl.ANY),
        ],
        out_specs=pl.BlockSpec((None, tc, d), lambda b, s: (b, s, 0)),
        scratch_shapes=[pltpu.VMEM((2, tc, d), F32), pltpu.VMEM((2, tc, d), F32),
                        pltpu.SemaphoreType.DMA((2,))],
        out_shape=jax.ShapeDtypeStruct((b_sz, s_len, d), F32),
        compiler_params=pltpu.CompilerParams(
            dimension_semantics=("arbitrary", "arbitrary"), vmem_limit_bytes=VMEM_LIMIT),
        name="combine",
    )(pos, pos, x2, rw, norm_final[None, :], y)


def _layer(x, norm_mix, w_in, b_in, conv_w, w_out, norm_ffn, w_group, b_group, w_expert, b_expert,
           w_gate, w_up, w_down, norm_out):
    b_sz, s_len, d = x.shape
    n_tok = b_sz * s_len
    a, kt, gt = _inproj(x, norm_mix, w_in, b_in)

    w_router = jnp.zeros((d, ROUTER_COLS), F32)
    w_router = w_router.at[:, 0:N_GROUPS].set(w_group).at[:, EXPERT_COL0:EXPERT_COL0 + N_EXPERTS].set(w_expert)
    b_router = jnp.zeros((1, ROUTER_COLS), F32)
    b_router = b_router.at[0, 0:N_GROUPS].set(b_group).at[0, EXPERT_COL0:EXPERT_COL0 + N_EXPERTS].set(b_expert)
    x2, hn2, ri, rw, counts = _mixer(x, a, kt, gt, conv_w, w_out, norm_ffn, w_router, b_router)

    tm = TM_EXPERT
    cnt = counts[:, 0]
    padded = ((cnt + tm - 1) // tm) * tm
    pends = jnp.cumsum(padded).astype(I32)
    seg = jnp.concatenate([jnp.zeros((1,), I32), pends])
    n_tiles = (n_tok * 2) // tm + N_EXPERTS
    cap = n_tiles * tm
    tile_start = jnp.arange(n_tiles, dtype=I32) * tm
    tile_e = jnp.minimum(jnp.sum((tile_start[:, None] >= pends[None, :]).astype(I32), axis=1),
                         N_EXPERTS - 1)
    n_used = pends[-1:] // tm
    e_sel = ri[:, 0:2, :]
    seg_start = jnp.sum(jnp.where(e_sel[..., None] == jnp.arange(N_EXPERTS, dtype=I32), seg[:-1], 0), axis=-1)
    pos = ri[:, 2:4, :] + seg_start

    xs = _dispatch(hn2, pos, seg, cap)
    y = _experts(xs, tile_e, n_used, w_gate, w_up, w_down)
    return _combine(x2, pos, rw, y, norm_out)


def kernel(x, norm_mix, w_in, b_in, conv_w, w_out, norm_ffn, w_group, b_group, w_expert, b_expert,
           w_gate, w_up, w_down, norm_final):
    depth = norm_mix.shape[0]
    assert depth == 1, "final-norm fusion below assumes a single layer"
    assert x.shape[-1] == 2 * D_MLSTM and x.shape[1] % TS_MIXER == 0
    return _layer(x, norm_mix[0], w_in[0], b_in[0], conv_w[0], w_out[0], norm_ffn[0],
                  w_group[0], b_group[0], w_expert[0], b_expert[0],
                  w_gate[0], w_up[0], w_down[0], norm_final)
```

```python
import jax
import jax.numpy as jnp
from jax import lax
from jax.experimental import pallas as pl
from jax.experimental.pallas import tpu as pltpu

F32 = jnp.float32
BF16 = jnp.bfloat16
I32 = jnp.int32

N_HEADS = 4
HEAD_DIM = 128
D_MLSTM = N_HEADS * HEAD_DIM
D_CONV = 512
N_GROUPS = 4
EXPERTS_PER_GROUP = 8
N_EXPERTS = N_GROUPS * EXPERTS_PER_GROUP
D_FF = 512
RMS_EPS = 1e-6
Q_SCALE = HEAD_DIM ** -0.5

LANES = 128
CHUNK = LANES
ROUTER_COLS = 128
EXPERT_COL0 = 8

TM_INPROJ = 512
TS_MIXER = 512
TD_DISPATCH = 512
OUT_COLS = 256
TM_EXPERT = 512
TC_COMBINE = 256
ROW_DMA_UNROLL = 8

VMEM_LIMIT = 56 * 1024 * 1024


def _nt_dot(a, b):
    return lax.dot_general(a, b, (((1,), (1,)), ((), ())), preferred_element_type=F32)


def _lane_cumsum(x, lane):
    s = 1
    while s < LANES:
        x = x + jnp.where(lane >= s, pltpu.roll(x, s, axis=1), jnp.zeros_like(x))
        s *= 2
    return x


def _inproj_kernel(x_ref, g_ref, wa_ref, ba_ref, wk_ref, bk_ref, wg_ref, bg_ref,
                   a_ref, kt_ref, gt_ref):
    x = x_ref[...]
    ms = jnp.mean(x * x, axis=-1, keepdims=True)
    hn = (x * lax.rsqrt(ms + RMS_EPS) * g_ref[...]).astype(BF16)
    n_blk = wa_ref.shape[1] // D_MLSTM
    for j in range(n_blk):
        sl = slice(j * D_MLSTM, (j + 1) * D_MLSTM)
        acc = jnp.dot(hn, wa_ref[:, sl], preferred_element_type=F32) + ba_ref[:, sl]
        if j == 0:
            acc = acc * Q_SCALE
        a_ref[:, sl] = acc.astype(BF16)
    kt_ref[...] = (_nt_dot(wk_ref[...], hn) + bk_ref[...]).astype(BF16)
    gt_ref[...] = _nt_dot(wg_ref[...], hn) + bg_ref[...]


def _inproj(x, norm_mix, w_in, b_in):
    b_sz, s_len, d = x.shape
    tm = TM_INPROJ
    dm = D_MLSTM
    g0 = 4 * dm
    c0 = g0 + 2 * N_HEADS
    cols_a = jnp.concatenate([w_in[:, 0:dm], w_in[:, 2 * dm:4 * dm], w_in[:, c0:c0 + 3 * D_CONV]], axis=1)
    bias_a = jnp.concatenate([b_in[0:dm], b_in[2 * dm:4 * dm], b_in[c0:c0 + 3 * D_CONV]])[None, :]
    wk_t = w_in[:, dm:2 * dm].T
    bk = b_in[dm:2 * dm][:, None]
    wg_t = w_in[:, g0:c0].T
    bg = b_in[g0:c0][:, None]
    n_a = cols_a.shape[1]
    return pl.pallas_call(
        _inproj_kernel,
        grid=(b_sz, s_len // tm),
        in_specs=[
            pl.BlockSpec((None, tm, d), lambda b, s: (b, s, 0)),
            pl.BlockSpec((1, d), lambda b, s: (0, 0)),
            pl.BlockSpec((d, n_a), lambda b, s: (0, 0)),
            pl.BlockSpec((1, n_a), lambda b, s: (0, 0)),
            pl.BlockSpec((dm, d), lambda b, s: (0, 0)),
            pl.BlockSpec((dm, 1), lambda b, s: (0, 0)),
            pl.BlockSpec((2 * N_HEADS, d), lambda b, s: (0, 0)),
            pl.BlockSpec((2 * N_HEADS, 1), lambda b, s: (0, 0)),
        ],
        out_specs=[
            pl.BlockSpec((None, tm, n_a), lambda b, s: (b, s, 0)),
            pl.BlockSpec((None, dm, tm), lambda b, s: (b, 0, s)),
            pl.BlockSpec((None, 2 * N_HEADS, tm), lambda b, s: (b, 0, s)),
        ],
        out_shape=[
            jax.ShapeDtypeStruct((b_sz, s_len, n_a), BF16),
            jax.ShapeDtypeStruct((b_sz, dm, s_len), BF16),
            jax.ShapeDtypeStruct((b_sz, 2 * N_HEADS, s_len), F32),
        ],
        compiler_params=pltpu.CompilerParams(
            dimension_semantics=("arbitrary", "arbitrary"), vmem_limit_bytes=VMEM_LIMIT),
        name="inproj",
    )(x, norm_mix[None, :], cols_a.astype(BF16), bias_a, wk_t.astype(BF16), bk, wg_t.astype(BF16), bg)


def _mixer_kernel(q_ref, v_ref, o_ref, cb_ref, cc_ref, cx_ref, kt_ref, gt_ref, x_ref,
                  cw_ref, wo_ref, g2_ref, wr_ref, br_ref,
                  x2_ref, hn2_ref, ri_ref, rw_ref, cnt_out_ref,
                  ct_ref, m_ref, carry_ref, cnt_ref, mix_ref):
    ts = x_ref.shape[0]
    n_chunks = ts // CHUNK
    first_tile = pl.program_id(1) == 0

    @pl.when(first_tile)
    def _():
        ct_ref[...] = jnp.zeros_like(ct_ref)
        m_ref[...] = jnp.zeros_like(m_ref)
        carry_ref[...] = jnp.zeros_like(carry_ref)

    @pl.when(jnp.logical_and(first_tile, pl.program_id(0) == 0))
    def _():
        cnt_ref[...] = jnp.zeros_like(cnt_ref)

    gates = gt_ref[...]
    li_all = gates[0:N_HEADS]
    fg = gates[N_HEADS:2 * N_HEADS]
    lf_all = jnp.minimum(fg, 0.0) - jnp.log1p(jnp.exp(-jnp.abs(fg)))
    lane4 = lax.broadcasted_iota(I32, (N_HEADS, CHUNK), 1)
    row = lax.broadcasted_iota(I32, (CHUNK, CHUNK), 0)
    col = lax.broadcasted_iota(I32, (CHUNK, CHUNK), 1)
    causal = row >= col
    ones_col = (lax.broadcasted_iota(I32, (CHUNK, HEAD_DIM), 1) == 0).astype(BF16)
    neg_inf = jnp.float32(-jnp.inf)

    m_prev = m_ref[0:N_HEADS, 0:1]
    for c in range(n_chunks):
        rows = slice(c * CHUNK, (c + 1) * CHUNK)
        li = li_all[:, rows]
        lf = lf_all[:, rows]
        b_cum = _lane_cumsum(lf, lane4)
        g_tot = b_cum[:, CHUNK - 1:CHUNK]
        a_end = g_tot - b_cum + li
        m_loc = jnp.max(a_end, axis=1, keepdims=True)
        u = li - b_cum
        m_new = jnp.maximum(g_tot + m_prev, m_loc)
        decay = jnp.exp(g_tot + m_prev - m_new)
        s_fac = jnp.exp(m_loc - m_new)
        w_key = jnp.exp(a_end - m_loc)
        for h in range(N_HEADS):
            hc = slice(h * HEAD_DIM, (h + 1) * HEAD_DIM)
            qh = q_ref[rows, hc]
            kth = kt_ref[hc, rows]
            v_aug = jnp.concatenate([v_ref[rows, hc], ones_col], axis=1)
            s_qk = jnp.dot(qh, kth, preferred_element_type=F32)
            u_m = jnp.where(causal, u[h:h + 1, :], neg_inf)
            m_row = jnp.max(u_m, axis=1, keepdims=True)
            p = (s_qk * jnp.exp(u_m - m_row)).astype(BF16)
            r_intra = jnp.dot(p, v_aug, preferred_element_type=F32)
            ct_old = ct_ref[h]
            r_inter = jnp.dot(qh, ct_old.astype(BF16), preferred_element_type=F32)
            ktw = (kth.astype(F32) * w_key[h:h + 1, :]).astype(BF16)
            kv = jnp.dot(ktw, v_aug, preferred_element_type=F32)
            ct_ref[h] = decay[h:h + 1, :] * ct_old + s_fac[h:h + 1, :] * kv

            mp = m_prev[h:h + 1, :]
            m_tot = jnp.maximum(m_row, mp)
            f_intra = jnp.exp(m_row - m_tot)
            f_inter = jnp.exp(mp - m_tot)
            b_col = jnp.sum(jnp.where(causal, lf[h:h + 1, :], 0.0), axis=1, keepdims=True)
            num = f_intra * r_intra[:, 0:HEAD_DIM] + f_inter * r_inter[:, 0:HEAD_DIM]
            den = (f_intra * r_intra[:, HEAD_DIM:HEAD_DIM + 1]
                   + f_inter * r_inter[:, HEAD_DIM:HEAD_DIM + 1])
            inv = 1.0 / jnp.maximum(jnp.abs(den), jnp.exp(-b_col - m_tot))
            o_gate = jax.nn.sigmoid(o_ref[rows, hc].astype(F32))
            mix_ref[rows, hc] = (o_gate * (num * inv)).astype(BF16)
        m_prev = m_new
    m_ref[0:N_HEADS, :] = jnp.broadcast_to(m_prev, (N_HEADS, LANES))

    u_c = cc_ref[...].astype(F32) * cx_ref[...].astype(F32)
    rid = lax.broadcasted_iota(I32, u_c.shape, 0)
    prev1 = carry_ref[0:1, :]
    prev2 = carry_ref[1:2, :]
    u1 = jnp.where(rid == 0, prev1, pltpu.roll(u_c, 1, axis=0))
    u2 = jnp.where(rid == 0, prev2, jnp.where(rid == 1, prev1, pltpu.roll(u_c, 2, axis=0)))
    z = cw_ref[0:1, :] * u2 + cw_ref[1:2, :] * u1 + cw_ref[2:3, :] * u_c
    mix_ref[:, D_MLSTM:] = (cb_ref[...].astype(F32) * z).astype(BF16)
    carry_ref[0:1, :] = u_c[ts - 1:ts, :]
    carry_ref[1:2, :] = u_c[ts - 2:ts - 1, :]

    d_model = x_ref.shape[1]
    mix = mix_ref[...]
    ssq = jnp.zeros((ts, 1), F32)
    for j in range(d_model // OUT_COLS):
        cs = slice(j * OUT_COLS, (j + 1) * OUT_COLS)
        x2c = x_ref[:, cs] + jnp.dot(mix, wo_ref[:, cs], preferred_element_type=F32)
        x2_ref[:, cs] = x2c
        ssq = ssq + jnp.sum(x2c * x2c, axis=-1, keepdims=True)
    hn2 = x2_ref[...] * lax.rsqrt(ssq * (1.0 / d_model) + RMS_EPS) * g2_ref[...]
    hn2_ref[...] = hn2

    h_hi = hn2.astype(BF16)
    h_lo = (hn2 - h_hi.astype(F32)).astype(BF16)
    hi_both = jnp.dot(h_hi, wr_ref[...], preferred_element_type=F32)
    lo_hi = jnp.dot(h_lo, wr_ref[:, 0:ROUTER_COLS], preferred_element_type=F32)
    logits = hi_both[:, 0:ROUTER_COLS] + (hi_both[:, ROUTER_COLS:] + lo_hi) + br_ref[...]
    lt = jnp.transpose(logits)
    gl = lt[0:N_GROUPS]
    gmax = jnp.max(gl, axis=0, keepdims=True)
    gi = lax.broadcasted_iota(I32, gl.shape, 0)
    g_sel = jnp.min(jnp.where(gl == gmax, gi, N_GROUPS), axis=0, keepdims=True)
    p_g = 1.0 / jnp.sum(jnp.exp(gl - gmax), axis=0, keepdims=True)
    in_g = lt[EXPERT_COL0:EXPERT_COL0 + EXPERTS_PER_GROUP]
    for g in range(1, N_GROUPS):
        lo = EXPERT_COL0 + g * EXPERTS_PER_GROUP
        in_g = jnp.where(g_sel == g, lt[lo:lo + EXPERTS_PER_GROUP], in_g)
    ei = lax.broadcasted_iota(I32, in_g.shape, 0)
    v1 = jnp.max(in_g, axis=0, keepdims=True)
    i1 = jnp.min(jnp.where(in_g == v1, ei, EXPERTS_PER_GROUP), axis=0, keepdims=True)
    rest = jnp.where(ei == i1, neg_inf, in_g)
    v2 = jnp.max(rest, axis=0, keepdims=True)
    i2 = jnp.min(jnp.where(rest == v2, ei, EXPERTS_PER_GROUP), axis=0, keepdims=True)
    t = jnp.exp(v2 - v1)
    w1 = p_g / (1.0 + t)
    w2 = p_g * t / (1.0 + t)
    e1 = g_sel * EXPERTS_PER_GROUP + i1
    e2 = g_sel * EXPERTS_PER_GROUP + i2

    eio = lax.broadcasted_iota(I32, (N_EXPERTS, LANES), 0)
    lane_e = lax.broadcasted_iota(I32, (N_EXPERTS, LANES), 1)
    base = cnt_ref[:, 0:1]
    zero_i = jnp.zeros((1, LANES), I32)
    for k in range(ts // LANES):
        ls = slice(k * LANES, (k + 1) * LANES)
        is1 = eio == e1[:, ls]
        is2 = eio == e2[:, ls]
        onehot = jnp.logical_or(is1, is2).astype(I32)
        incl = _lane_cumsum(onehot, lane_e)
        rank_e = base + incl - onehot
        r1 = jnp.sum(jnp.where(is1, rank_e, 0), axis=0, keepdims=True)
        r2 = jnp.sum(jnp.where(is2, rank_e, 0), axis=0, keepdims=True)
        ri_ref[:, ls] = jnp.concatenate(
            [e1[:, ls], e2[:, ls], r1, r2, zero_i, zero_i, zero_i, zero_i], axis=0)
        base = base + incl[:, LANES - 1:LANES]
    zero_f = jnp.zeros_like(w1)
    rw_ref[...] = jnp.concatenate([w1, w2, zero_f, zero_f, zero_f, zero_f, zero_f, zero_f], axis=0)
    cnt_new = jnp.broadcast_to(base, (N_EXPERTS, LANES))
    cnt_ref[...] = cnt_new
    cnt_out_ref[...] = cnt_new


def _mixer(x, a, kt, gt, conv_w, w_out, norm_ffn, w_router, b_router):
    b_sz, s_len, d = x.shape
    ts = TS_MIXER
    dm = D_MLSTM

    def a_spec(j):
        return pl.BlockSpec((None, ts, dm), lambda b, s, j=j: (b, s, j))

    tok = lambda w: pl.BlockSpec((None, ts, w), lambda b, s: (b, s, 0))
    rowsp = pl.BlockSpec((None, 2 * N_HEADS, ts), lambda b, s: (b, 0, s))
    const = lambda shape: pl.BlockSpec(shape, lambda b, s: tuple(0 for _ in shape))
    wr_hi = w_router.astype(BF16)
    wr_lo = (w_router - wr_hi.astype(F32)).astype(BF16)
    w_router2 = jnp.concatenate([wr_hi, wr_lo], axis=1)
    return pl.pallas_call(
        _mixer_kernel,
        grid=(b_sz, s_len // ts),
        in_specs=[a_spec(0), a_spec(1), a_spec(2), a_spec(3), a_spec(4), a_spec(5),
                  pl.BlockSpec((None, dm, ts), lambda b, s: (b, 0, s)),
                  rowsp, tok(d),
                  const((3, D_CONV)), const((d, d)), const((1, d)),
                  const((d, 2 * ROUTER_COLS)), const((1, ROUTER_COLS))],
        out_specs=[tok(d), tok(d), rowsp, rowsp, const((N_EXPERTS, LANES))],
        out_shape=[
            jax.ShapeDtypeStruct((b_sz, s_len, d), F32),
            jax.ShapeDtypeStruct((b_sz, s_len, d), F32),
            jax.ShapeDtypeStruct((b_sz, 2 * N_HEADS, s_len), I32),
            jax.ShapeDtypeStruct((b_sz, 2 * N_HEADS, s_len), F32),
            jax.ShapeDtypeStruct((N_EXPERTS, LANES), I32),
        ],
        scratch_shapes=[
            pltpu.VMEM((N_HEADS, HEAD_DIM, 2 * HEAD_DIM), F32),
            pltpu.VMEM((2 * N_HEADS, LANES), F32),
            pltpu.VMEM((8, D_CONV), F32),
            pltpu.VMEM((N_EXPERTS, LANES), I32),
            pltpu.VMEM((ts, d), BF16),
        ],
        compiler_params=pltpu.CompilerParams(
            dimension_semantics=("arbitrary", "arbitrary"), vmem_limit_bytes=VMEM_LIMIT),
        name="mixer",
    )(a, a, a, a, a, a, kt, gt, x, conv_w, w_out.astype(BF16), norm_ffn[None, :], w_router2, b_router)


def _dispatch_kernel(seg_ref, pos_ref, h_ref, xs_ref, zero_ref, sem, zsem):
    td = h_ref.shape[0]
    tm = zero_ref.shape[0]

    @pl.when(jnp.logical_and(pl.program_id(0) == 0, pl.program_id(1) == 0))
    def _():
        zero_ref[...] = jnp.zeros_like(zero_ref)

        def zero_copy(e):
            last_tile = pl.multiple_of(seg_ref[e + 1] - tm, tm)
            return pltpu.make_async_copy(zero_ref, xs_ref.at[pl.ds(last_tile, tm)], zsem)

        for e in range(N_EXPERTS):
            @pl.when(seg_ref[e + 1] > seg_ref[e])
            def _(e=e):
                zero_copy(e).start()
        for e in range(N_EXPERTS):
            @pl.when(seg_ref[e + 1] > seg_ref[e])
            def _(e=e):
                zero_copy(e).wait()

        def tail_copy(t):
            return pltpu.make_async_copy(zero_ref, xs_ref.at[pl.ds(pl.multiple_of(t * tm, tm), tm)], zsem)

        first_unused = seg_ref[N_EXPERTS] // tm
        n_tiles = xs_ref.shape[0] // tm
        lax.fori_loop(first_unused, n_tiles, lambda t, c: (tail_copy(t).start(), c)[1], 0)
        lax.fori_loop(first_unused, n_tiles, lambda t, c: (tail_copy(t).wait(), c)[1], 0)

    def start(i, carry):
        for j in range(ROW_DMA_UNROLL):
            r = i * ROW_DMA_UNROLL + j
            src = h_ref.at[pl.ds(r, 1)]
            pltpu.make_async_copy(src, xs_ref.at[pl.ds(pos_ref[0, r], 1)], sem).start()
            pltpu.make_async_copy(src, xs_ref.at[pl.ds(pos_ref[1, r], 1)], sem).start()
        return carry

    lax.fori_loop(0, td // ROW_DMA_UNROLL, start, 0)
    for _ in range(2):
        pltpu.make_async_copy(h_ref, xs_ref.at[pl.ds(0, td)], sem).wait()


def _dispatch(hn2, pos, seg, cap):
    b_sz, s_len, d = hn2.shape
    td = TD_DISPATCH
    n_s = s_len // td
    return pl.pallas_call(
        _dispatch_kernel,
        grid_spec=pltpu.PrefetchScalarGridSpec(
            num_scalar_prefetch=1,
            grid=(b_sz, n_s),
            in_specs=[
                pl.BlockSpec((None, 2, td), lambda b, s, sg: (b, 0, s), memory_space=pltpu.SMEM),
                pl.BlockSpec((td, d), lambda b, s, sg: (b * n_s + s, 0)),
            ],
            out_specs=pl.BlockSpec(memory_space=pl.ANY),
            scratch_shapes=[pltpu.VMEM((TM_EXPERT, d), hn2.dtype),
                            pltpu.SemaphoreType.DMA(()), pltpu.SemaphoreType.DMA(())],
        ),
        out_shape=jax.ShapeDtypeStruct((cap, d), hn2.dtype),
        compiler_params=pltpu.CompilerParams(
            dimension_semantics=("arbitrary", "arbitrary"), vmem_limit_bytes=VMEM_LIMIT),
        name="dispatch",
    )(seg, pos, hn2.reshape(b_sz * s_len, d))


def _experts_kernel(te_ref, nu_ref, xs_ref, wg_ref, wu_ref, wd_ref, y_ref, wgb_ref, wub_ref, wdb_ref):
    i = pl.program_id(0)
    e = te_ref[i]
    e_prev = te_ref[jnp.maximum(i - 1, 0)]
    new_expert = jnp.logical_or(i == 0, e != e_prev)
    used = i < nu_ref[0]

    @pl.when(jnp.logical_and(used, new_expert))
    def _():
        wgb_ref[...] = wg_ref[...].astype(BF16)
        wub_ref[...] = wu_ref[...].astype(BF16)
        wdb_ref[...] = wd_ref[...].astype(BF16)

    @pl.when(used)
    def _():
        xb = xs_ref[...].astype(BF16)
        gate = jnp.dot(xb, wgb_ref[...], preferred_element_type=F32)
        up = jnp.dot(xb, wub_ref[...], preferred_element_type=F32)
        hid = (gate * jax.nn.sigmoid(gate) * up).astype(BF16)
        y_ref[...] = jnp.dot(hid, wdb_ref[...], preferred_element_type=F32)

    @pl.when(jnp.logical_not(used))
    def _():
        y_ref[...] = jnp.zeros_like(y_ref)


def _experts(xs, tile_e, n_used, w_gate, w_up, w_down):
    cap, d = xs.shape
    tm = TM_EXPERT
    n_tiles = cap // tm

    def x_map(i, te, nu):
        return (jnp.minimum(i, jnp.maximum(nu[0] - 1, 0)), 0)

    def w_map(i, te, nu):
        return (te[i], 0, 0)

    return pl.pallas_call(
        _experts_kernel,
        grid_spec=pltpu.PrefetchScalarGridSpec(
            num_scalar_prefetch=2,
            grid=(n_tiles,),
            in_specs=[
                pl.BlockSpec((tm, d), x_map),
                pl.BlockSpec((None, d, D_FF), w_map),
                pl.BlockSpec((None, d, D_FF), w_map),
                pl.BlockSpec((None, D_FF, d), w_map),
            ],
            out_specs=pl.BlockSpec((tm, d), lambda i, te, nu: (i, 0)),
            scratch_shapes=[pltpu.VMEM((d, D_FF), BF16), pltpu.VMEM((d, D_FF), BF16),
                            pltpu.VMEM((D_FF, d), BF16)],
        ),
        out_shape=jax.ShapeDtypeStruct((cap, d), F32),
        compiler_params=pltpu.CompilerParams(
            dimension_semantics=("arbitrary",), vmem_limit_bytes=VMEM_LIMIT),
        name="experts",
    )(tile_e, n_used, xs, w_gate, w_up, w_down)


def _combine_kernel(pos_ref, pos_next_ref, x2_ref, rw_ref, gf_ref, y_ref, out_ref, y1_ref, y2_ref, sem):
    tc = x2_ref.shape[0]
    step = pl.program_id(0) * pl.num_programs(1) + pl.program_id(1)
    n_steps = pl.num_programs(0) * pl.num_programs(1)
    slot = lax.rem(step, 2)

    def gather(p_ref, slot):
        def body(i, carry):
            for j in range(ROW_DMA_UNROLL):
                r = i * ROW_DMA_UNROLL + j
                pltpu.make_async_copy(y_ref.at[pl.ds(p_ref[0, r], 1)],
                                      y1_ref.at[slot, pl.ds(r, 1)], sem.at[slot]).start()
                pltpu.make_async_copy(y_ref.at[pl.ds(p_ref[1, r], 1)],
                                      y2_ref.at[slot, pl.ds(r, 1)], sem.at[slot]).start()
            return carry
        lax.fori_loop(0, tc // ROW_DMA_UNROLL, body, 0)

    @pl.when(step == 0)
    def _():
        gather(pos_ref, 0)

    @pl.when(step + 1 < n_steps)
    def _():
        gather(pos_next_ref, 1 - slot)

    w_t = jnp.transpose(rw_ref[...])
    pltpu.make_async_copy(y_ref.at[pl.ds(0, tc)], y1_ref.at[slot], sem.at[slot]).wait()
    pltpu.make_async_copy(y_ref.at[pl.ds(0, tc)], y2_ref.at[slot], sem.at[slot]).wait()
    x3 = x2_ref[...] + w_t[:, 0:1] * y1_ref[slot] + w_t[:, 1:2] * y2_ref[slot]
    ms = jnp.mean(x3 * x3, axis=-1, keepdims=True)
    out_ref[...] = x3 * lax.rsqrt(ms + RMS_EPS) * gf_ref[...]


def _combine(x2, pos, rw, y, norm_final):
    b_sz, s_len, d = x2.shape
    tc = TC_COMBINE
    n_s = s_len // tc

    def next_tile(b, s):
        nxt = jnp.minimum(b * n_s + s + 1, b_sz * n_s - 1)
        return (nxt // n_s, 0, nxt % n_s)

    return pl.pallas_call(
        _combine_kernel,
        grid=(b_sz, n_s),
        in_specs=[
            pl.BlockSpec((None, 2, tc), lambda b, s: (b, 0, s), memory_space=pltpu.SMEM),
            pl.BlockSpec((None, 2, tc), next_tile, memory_space=pltpu.SMEM),
            pl.BlockSpec((None, tc, d), lambda b, s: (b, s, 0)),
            pl.BlockSpec((None, 2 * N_HEADS, tc), lambda b, s: (b, 0, s)),
            pl.BlockSpec((1, d), lambda b, s: (0, 0)),
            pl.BlockSpec(memory_space=pl.ANY),
        ],
        out_specs=pl.BlockSpec((None, tc, d), lambda b, s: (b, s, 0)),
        scratch_shapes=[pltpu.VMEM((2, tc, d), F32), pltpu.VMEM((2, tc, d), F32),
                        pltpu.SemaphoreType.DMA((2,))],
        out_shape=jax.ShapeDtypeStruct((b_sz, s_len, d), F32),
        compiler_params=pltpu.CompilerParams(
            dimension_semantics=("arbitrary", "arbitrary"), vmem_limit_bytes=VMEM_LIMIT),
        name="combine",
    )(pos, pos, x2, rw, norm_final[None, :], y)


def _layer(x, norm_mix, w_in, b_in, conv_w, w_out, norm_ffn, w_group, b_group, w_expert, b_expert,
           w_gate, w_up, w_down, norm_out):
    b_sz, s_len, d = x.shape
    n_tok = b_sz * s_len
    a, kt, gt = _inproj(x, norm_mix, w_in, b_in)

    w_router = jnp.zeros((d, ROUTER_COLS), F32)
    w_router = w_router.at[:, 0:N_GROUPS].set(w_group).at[:, EXPERT_COL0:EXPERT_COL0 + N_EXPERTS].set(w_expert)
    b_router = jnp.zeros((1, ROUTER_COLS), F32)
    b_router = b_router.at[0, 0:N_GROUPS].set(b_group).at[0, EXPERT_COL0:EXPERT_COL0 + N_EXPERTS].set(b_expert)
    x2, hn2, ri, rw, counts = _mixer(x, a, kt, gt, conv_w, w_out, norm_ffn, w_router, b_router)

    tm = TM_EXPERT
    cnt = counts[:, 0]
    padded = ((cnt + tm - 1) // tm) * tm
    pends = jnp.cumsum(padded).astype(I32)
    seg = jnp.concatenate([jnp.zeros((1,), I32), pends])
    n_tiles = (n_tok * 2) // tm + N_EXPERTS
    cap = n_tiles * tm
    tile_start = jnp.arange(n_tiles, dtype=I32) * tm
    tile_e = jnp.minimum(jnp.sum((tile_start[:, None] >= pends[None, :]).astype(I32), axis=1),
                         N_EXPERTS - 1)
    n_used = pends[-1:] // tm
    e_sel = ri[:, 0:2, :]
    seg_start = jnp.sum(jnp.where(e_sel[..., None] == jnp.arange(N_EXPERTS, dtype=I32), seg[:-1], 0), axis=-1)
    pos = ri[:, 2:4, :] + seg_start

    xs = _dispatch(hn2, pos, seg, cap)
    y = _experts(xs, tile_e, n_used, w_gate, w_up, w_down)
    return _combine(x2, pos, rw, y, norm_out)


def kernel(x, norm_mix, w_in, b_in, conv_w, w_out, norm_ffn, w_group, b_group, w_expert, b_expert,
           w_gate, w_up, w_down, norm_final):
    depth = norm_mix.shape[0]
    assert depth == 1, "final-norm fusion below assumes a single layer"
    assert x.shape[-1] == 2 * D_MLSTM and x.shape[1] % TS_MIXER == 0
    return _layer(x, norm_mix[0], w_in[0], b_in[0], conv_w[0], w_out[0], norm_ffn[0],
                  w_group[0], b_group[0], w_expert[0], b_expert[0],
                  w_gate[0], w_up[0], w_down[0], norm_final)
```

```python
import jax
import jax.numpy as jnp
from jax import lax
from jax.experimental import pallas as pl
from jax.experimental.pallas import tpu as pltpu

F32 = jnp.float32
BF16 = jnp.bfloat16
I32 = jnp.int32

N_HEADS = 4
HEAD_DIM = 128
D_MLSTM = N_HEADS * HEAD_DIM
D_CONV = 512
N_GROUPS = 4
EXPERTS_PER_GROUP = 8
N_EXPERTS = N_GROUPS * EXPERTS_PER_GROUP
D_FF = 512
RMS_EPS = 1e-6
Q_SCALE = HEAD_DIM ** -0.5

LANES = 128
CHUNK = LANES
ROUTER_COLS = 128
EXPERT_COL0 = 8

TM_INPROJ = 512
TS_MIXER = 512
TQ_INVERT = 512
OUT_COLS = 256
TM_EXPERT = 512
TC_COMBINE = 256
ROW_DMA_UNROLL = 8

VMEM_LIMIT = 56 * 1024 * 1024


def _nt_dot(a, b):
    return lax.dot_general(a, b, (((1,), (1,)), ((), ())), preferred_element_type=F32)


def _lane_cumsum(x, lane):
    s = 1
    while s < LANES:
        x = x + jnp.where(lane >= s, pltpu.roll(x, s, axis=1), jnp.zeros_like(x))
        s *= 2
    return x


def _inproj_kernel(x_ref, g_ref, wa_ref, ba_ref, wk_ref, bk_ref, wg_ref, bg_ref,
                   a_ref, kt_ref, gt_ref):
    x = x_ref[...]
    ms = jnp.mean(x * x, axis=-1, keepdims=True)
    hn = (x * lax.rsqrt(ms + RMS_EPS) * g_ref[...]).astype(BF16)
    n_blk = wa_ref.shape[1] // D_MLSTM
    for j in range(n_blk):
        sl = slice(j * D_MLSTM, (j + 1) * D_MLSTM)
        acc = jnp.dot(hn, wa_ref[:, sl], preferred_element_type=F32) + ba_ref[:, sl]
        if j == 0:
            acc = acc * Q_SCALE
        a_ref[:, sl] = acc.astype(BF16)
    kt_ref[...] = (_nt_dot(wk_ref[...], hn) + bk_ref[...]).astype(BF16)
    gt_ref[...] = _nt_dot(wg_ref[...], hn) + bg_ref[...]


def _inproj(x, norm_mix, w_in, b_in):
    b_sz, s_len, d = x.shape
    tm = TM_INPROJ
    dm = D_MLSTM
    g0 = 4 * dm
    c0 = g0 + 2 * N_HEADS
    cols_a = jnp.concatenate([w_in[:, 0:dm], w_in[:, 2 * dm:4 * dm], w_in[:, c0:c0 + 3 * D_CONV]], axis=1)
    bias_a = jnp.concatenate([b_in[0:dm], b_in[2 * dm:4 * dm], b_in[c0:c0 + 3 * D_CONV]])[None, :]
    wk_t = w_in[:, dm:2 * dm].T
    bk = b_in[dm:2 * dm][:, None]
    wg_t = w_in[:, g0:c0].T
    bg = b_in[g0:c0][:, None]
    n_a = cols_a.shape[1]
    return pl.pallas_call(
        _inproj_kernel,
        grid=(b_sz, s_len // tm),
        in_specs=[
            pl.BlockSpec((None, tm, d), lambda b, s: (b, s, 0)),
            pl.BlockSpec((1, d), lambda b, s: (0, 0)),
            pl.BlockSpec((d, n_a), lambda b, s: (0, 0)),
            pl.BlockSpec((1, n_a), lambda b, s: (0, 0)),
            pl.BlockSpec((dm, d), lambda b, s: (0, 0)),
            pl.BlockSpec((dm, 1), lambda b, s: (0, 0)),
            pl.BlockSpec((2 * N_HEADS, d), lambda b, s: (0, 0)),
            pl.BlockSpec((2 * N_HEADS, 1), lambda b, s: (0, 0)),
        ],
        out_specs=[
            pl.BlockSpec((None, tm, n_a), lambda b, s: (b, s, 0)),
            pl.BlockSpec((None, dm, tm), lambda b, s: (b, 0, s)),
            pl.BlockSpec((None, 2 * N_HEADS, tm), lambda b, s: (b, 0, s)),
        ],
        out_shape=[
            jax.ShapeDtypeStruct((b_sz, s_len, n_a), BF16),
            jax.ShapeDtypeStruct((b_sz, dm, s_len), BF16),
            jax.ShapeDtypeStruct((b_sz, 2 * N_HEADS, s_len), F32),
        ],
        compiler_params=pltpu.CompilerParams(
            dimension_semantics=("arbitrary", "arbitrary"), vmem_limit_bytes=VMEM_LIMIT),
        name="inproj",
    )(x, norm_mix[None, :], cols_a.astype(BF16), bias_a, wk_t.astype(BF16), bk, wg_t.astype(BF16), bg)


def _mixer_kernel(q_ref, v_ref, o_ref, cb_ref, cc_ref, cx_ref, kt_ref, gt_ref, x_ref,
                  cw_ref, wo_ref, g2_ref, wr_ref, br_ref,
                  x2_ref, hn2_ref, ri_ref, rw_ref, cnt_out_ref,
                  ct_ref, m_ref, carry_ref, cnt_ref, mix_ref):
    ts = x_ref.shape[0]
    n_chunks = ts // CHUNK
    first_tile = pl.program_id(1) == 0

    @pl.when(first_tile)
    def _():
        ct_ref[...] = jnp.zeros_like(ct_ref)
        m_ref[...] = jnp.zeros_like(m_ref)
        carry_ref[...] = jnp.zeros_like(carry_ref)

    @pl.when(jnp.logical_and(first_tile, pl.program_id(0) == 0))
    def _():
        cnt_ref[...] = jnp.zeros_like(cnt_ref)

    gates = gt_ref[...]
    li_all = gates[0:N_HEADS]
    fg = gates[N_HEADS:2 * N_HEADS]
    lf_all = jnp.minimum(fg, 0.0) - jnp.log1p(jnp.exp(-jnp.abs(fg)))
    lane4 = lax.broadcasted_iota(I32, (N_HEADS, CHUNK), 1)
    row = lax.broadcasted_iota(I32, (CHUNK, CHUNK), 0)
    col = lax.broadcasted_iota(I32, (CHUNK, CHUNK), 1)
    causal = row >= col
    ones_col = (lax.broadcasted_iota(I32, (CHUNK, HEAD_DIM), 1) == 0).astype(BF16)
    neg_inf = jnp.float32(-jnp.inf)

    m_prev = m_ref[0:N_HEADS, 0:1]
    for c in range(n_chunks):
        rows = slice(c * CHUNK, (c + 1) * CHUNK)
        li = li_all[:, rows]
        lf = lf_all[:, rows]
        b_cum = _lane_cumsum(lf, lane4)
        g_tot = b_cum[:, CHUNK - 1:CHUNK]
        a_end = g_tot - b_cum + li
        m_loc = jnp.max(a_end, axis=1, keepdims=True)
        u = li - b_cum
        m_new = jnp.maximum(g_tot + m_prev, m_loc)
        decay = jnp.exp(g_tot + m_prev - m_new)
        s_fac = jnp.exp(m_loc - m_new)
        w_key = jnp.exp(a_end - m_loc)
        for h in range(N_HEADS):
            hc = slice(h * HEAD_DIM, (h + 1) * HEAD_DIM)
            qh = q_ref[rows, hc]
            kth = kt_ref[hc, rows]
            v_aug = jnp.concatenate([v_ref[rows, hc], ones_col], axis=1)
            s_qk = jnp.dot(qh, kth, preferred_element_type=F32)
            u_m = jnp.where(causal, u[h:h + 1, :], neg_inf)
            m_row = jnp.max(u_m, axis=1, keepdims=True)
            p = (s_qk * jnp.exp(u_m - m_row)).astype(BF16)
            r_intra = jnp.dot(p, v_aug, preferred_element_type=F32)
            ct_old = ct_ref[h]
            r_inter = jnp.dot(qh, ct_old.astype(BF16), preferred_element_type=F32)
            ktw = (kth.astype(F32) * w_key[h:h + 1, :]).astype(BF16)
            kv = jnp.dot(ktw, v_aug, preferred_element_type=F32)
            ct_ref[h] = decay[h:h + 1, :] * ct_old + s_fac[h:h + 1, :] * kv

            mp = m_prev[h:h + 1, :]
            m_tot = jnp.maximum(m_row, mp)
            f_intra = jnp.exp(m_row - m_tot)
            f_inter = jnp.exp(mp - m_tot)
            b_col = jnp.sum(jnp.where(causal, lf[h:h + 1, :], 0.0), axis=1, keepdims=True)
            num = f_intra * r_intra[:, 0:HEAD_DIM] + f_inter * r_inter[:, 0:HEAD_DIM]
            den = (f_intra * r_intra[:, HEAD_DIM:HEAD_DIM + 1]
                   + f_inter * r_inter[:, HEAD_DIM:HEAD_DIM + 1])
            inv = 1.0 / jnp.maximum(jnp.abs(den), jnp.exp(-b_col - m_tot))
            o_gate = jax.nn.sigmoid(o_ref[rows, hc].astype(F32))
            mix_ref[rows, hc] = (o_gate * (num * inv)).astype(BF16)
        m_prev = m_new
    m_ref[0:N_HEADS, :] = jnp.broadcast_to(m_prev, (N_HEADS, LANES))

    u_c = cc_ref[...].astype(F32) * cx_ref[...].astype(F32)
    rid = lax.broadcasted_iota(I32, u_c.shape, 0)
    prev1 = carry_ref[0:1, :]
    prev2 = carry_ref[1:2, :]
    u1 = jnp.where(rid == 0, prev1, pltpu.roll(u_c, 1, axis=0))
    u2 = jnp.where(rid == 0, prev2, jnp.where(rid == 1, prev1, pltpu.roll(u_c, 2, axis=0)))
    z = cw_ref[0:1, :] * u2 + cw_ref[1:2, :] * u1 + cw_ref[2:3, :] * u_c
    mix_ref[:, D_MLSTM:] = (cb_ref[...].astype(F32) * z).astype(BF16)
    carry_ref[0:1, :] = u_c[ts - 1:ts, :]
    carry_ref[1:2, :] = u_c[ts - 2:ts - 1, :]

    d_model = x_ref.shape[1]
    mix = mix_ref[...]
    ssq = jnp.zeros((ts, 1), F32)
    for j in range(d_model // OUT_COLS):
        cs = slice(j * OUT_COLS, (j + 1) * OUT_COLS)
        x2c = x_ref[:, cs] + jnp.dot(mix, wo_ref[:, cs], preferred_element_type=F32)
        x2_ref[:, cs] = x2c
        ssq = ssq + jnp.sum(x2c * x2c, axis=-1, keepdims=True)
    hn2 = x2_ref[...] * lax.rsqrt(ssq * (1.0 / d_model) + RMS_EPS) * g2_ref[...]
    hn2_ref[...] = hn2

    h_hi = hn2.astype(BF16)
    h_lo = (hn2 - h_hi.astype(F32)).astype(BF16)
    hi_both = jnp.dot(h_hi, wr_ref[...], preferred_element_type=F32)
    lo_hi = jnp.dot(h_lo, wr_ref[:, 0:ROUTER_COLS], preferred_element_type=F32)
    logits = hi_both[:, 0:ROUTER_COLS] + (hi_both[:, ROUTER_COLS:] + lo_hi) + br_ref[...]
    lt = jnp.transpose(logits)
    gl = lt[0:N_GROUPS]
    gmax = jnp.max(gl, axis=0, keepdims=True)
    gi = lax.broadcasted_iota(I32, gl.shape, 0)
    g_sel = jnp.min(jnp.where(gl == gmax, gi, N_GROUPS), axis=0, keepdims=True)
    p_g = 1.0 / jnp.sum(jnp.exp(gl - gmax), axis=0, keepdims=True)
    in_g = lt[EXPERT_COL0:EXPERT_COL0 + EXPERTS_PER_GROUP]
    for g in range(1, N_GROUPS):
        lo = EXPERT_COL0 + g * EXPERTS_PER_GROUP
        in_g = jnp.where(g_sel == g, lt[lo:lo + EXPERTS_PER_GROUP], in_g)
    ei = lax.broadcasted_iota(I32, in_g.shape, 0)
    v1 = jnp.max(in_g, axis=0, keepdims=True)
    i1 = jnp.min(jnp.where(in_g == v1, ei, EXPERTS_PER_GROUP), axis=0, keepdims=True)
    rest = jnp.where(ei == i1, neg_inf, in_g)
    v2 = jnp.max(rest, axis=0, keepdims=True)
    i2 = jnp.min(jnp.where(rest == v2, ei, EXPERTS_PER_GROUP), axis=0, keepdims=True)
    t = jnp.exp(v2 - v1)
    w1 = p_g / (1.0 + t)
    w2 = p_g * t / (1.0 + t)
    e1 = g_sel * EXPERTS_PER_GROUP + i1
    e2 = g_sel * EXPERTS_PER_GROUP + i2

    eio = lax.broadcasted_iota(I32, (N_EXPERTS, LANES), 0)
    lane_e = lax.broadcasted_iota(I32, (N_EXPERTS, LANES), 1)
    base = cnt_ref[:, 0:1]
    zero_i = jnp.zeros((1, LANES), I32)
    for k in range(ts // LANES):
        ls = slice(k * LANES, (k + 1) * LANES)
        is1 = eio == e1[:, ls]
        is2 = eio == e2[:, ls]
        onehot = jnp.logical_or(is1, is2).astype(I32)
        incl = _lane_cumsum(onehot, lane_e)
        rank_e = base + incl - onehot
        r1 = jnp.sum(jnp.where(is1, rank_e, 0), axis=0, keepdims=True)
        r2 = jnp.sum(jnp.where(is2, rank_e, 0), axis=0, keepdims=True)
        ri_ref[:, ls] = jnp.concatenate(
            [e1[:, ls], e2[:, ls], r1, r2, zero_i, zero_i, zero_i, zero_i], axis=0)
        base = base + incl[:, LANES - 1:LANES]
    zero_f = jnp.zeros_like(w1)
    rw_ref[...] = jnp.concatenate([w1, w2, zero_f, zero_f, zero_f, zero_f, zero_f, zero_f], axis=0)
    cnt_new = jnp.broadcast_to(base, (N_EXPERTS, LANES))
    cnt_ref[...] = cnt_new
    cnt_out_ref[...] = cnt_new


def _mixer(x, a, kt, gt, conv_w, w_out, norm_ffn, w_router, b_router):
    b_sz, s_len, d = x.shape
    ts = TS_MIXER
    dm = D_MLSTM

    def a_spec(j):
        return pl.BlockSpec((None, ts, dm), lambda b, s, j=j: (b, s, j))

    tok = lambda w: pl.BlockSpec((None, ts, w), lambda b, s: (b, s, 0))
    rowsp = pl.BlockSpec((None, 2 * N_HEADS, ts), lambda b, s: (b, 0, s))
    const = lambda shape: pl.BlockSpec(shape, lambda b, s: tuple(0 for _ in shape))
    wr_hi = w_router.astype(BF16)
    wr_lo = (w_router - wr_hi.astype(F32)).astype(BF16)
    w_router2 = jnp.concatenate([wr_hi, wr_lo], axis=1)
    return pl.pallas_call(
        _mixer_kernel,
        grid=(b_sz, s_len // ts),
        in_specs=[a_spec(0), a_spec(1), a_spec(2), a_spec(3), a_spec(4), a_spec(5),
                  pl.BlockSpec((None, dm, ts), lambda b, s: (b, 0, s)),
                  rowsp, tok(d),
                  const((3, D_CONV)), const((d, d)), const((1, d)),
                  const((d, 2 * ROUTER_COLS)), const((1, ROUTER_COLS))],
        out_specs=[tok(d), tok(d), rowsp, rowsp, const((N_EXPERTS, LANES))],
        out_shape=[
            jax.ShapeDtypeStruct((b_sz, s_len, d), F32),
            jax.ShapeDtypeStruct((b_sz, s_len, d), F32),
            jax.ShapeDtypeStruct((b_sz, 2 * N_HEADS, s_len), I32),
            jax.ShapeDtypeStruct((b_sz, 2 * N_HEADS, s_len), F32),
            jax.ShapeDtypeStruct((N_EXPERTS, LANES), I32),
        ],
        scratch_shapes=[
            pltpu.VMEM((N_HEADS, HEAD_DIM, 2 * HEAD_DIM), F32),
            pltpu.VMEM((2 * N_HEADS, LANES), F32),
            pltpu.VMEM((8, D_CONV), F32),
            pltpu.VMEM((N_EXPERTS, LANES), I32),
            pltpu.VMEM((ts, d), BF16),
        ],
        compiler_params=pltpu.CompilerParams(
            dimension_semantics=("arbitrary", "arbitrary"), vmem_limit_bytes=VMEM_LIMIT),
        name="mixer",
    )(a, a, a, a, a, a, kt, gt, x, conv_w, w_out.astype(BF16), norm_ffn[None, :], w_router2, b_router)


def _invert_kernel(seg_ref, cnt_ref, pos_ref, src_ref):
    tq = pos_ref.shape[1]
    step = pl.program_id(0) * pl.num_programs(1) + pl.program_id(1)

    @pl.when(step == 0)
    def _():
        def zero(s, carry):
            src_ref[s] = 0
            return carry
        for e in range(N_EXPERTS):
            lax.fori_loop(seg_ref[e] + cnt_ref[e], seg_ref[e + 1], zero, 0)
        lax.fori_loop(seg_ref[N_EXPERTS], src_ref.shape[0], zero, 0)

    tok0 = step * tq

    def body(i, carry):
        for j in range(ROW_DMA_UNROLL):
            r = i * ROW_DMA_UNROLL + j
            src_ref[pos_ref[0, r]] = tok0 + r
            src_ref[pos_ref[1, r]] = tok0 + r
        return carry

    lax.fori_loop(0, tq // ROW_DMA_UNROLL, body, 0)


def _invert(pos, seg, cnt, cap):
    b_sz, _, s_len = pos.shape
    tq = TQ_INVERT
    return pl.pallas_call(
        _invert_kernel,
        grid_spec=pltpu.PrefetchScalarGridSpec(
            num_scalar_prefetch=2,
            grid=(b_sz, s_len // tq),
            in_specs=[pl.BlockSpec((None, 2, tq), lambda b, s, sg, ct: (b, 0, s), memory_space=pltpu.SMEM)],
            out_specs=pl.BlockSpec(memory_space=pltpu.SMEM),
        ),
        out_shape=jax.ShapeDtypeStruct((cap,), I32),
        compiler_params=pltpu.CompilerParams(dimension_semantics=("arbitrary", "arbitrary")),
        name="invert",
    )(seg, cnt, pos)


def _experts_kernel(te_ref, nu_ref, src_ref, h_ref, wg_ref, wu_ref, wd_ref, y_ref,
                    wgb_ref, wub_ref, wdb_ref, xbuf0_ref, xbuf1_ref, gsem):
    tm = y_ref.shape[0]
    i = pl.program_id(0)
    n_used = nu_ref[0]
    e = te_ref[i]
    e_prev = te_ref[jnp.maximum(i - 1, 0)]
    new_expert = jnp.logical_or(i == 0, e != e_prev)
    used = i < n_used
    parity = lax.rem(i, 2)
    bufs = (xbuf0_ref, xbuf1_ref)

    def gather(tile, slot):
        base = tile * tm
        for r in range(tm):
            pltpu.make_async_copy(h_ref.at[pl.ds(src_ref[base + r], 1)],
                                  bufs[slot].at[pl.ds(r, 1)], gsem.at[slot]).start()

    def wait_tile(slot):
        pltpu.make_async_copy(h_ref.at[pl.ds(0, tm)], bufs[slot], gsem.at[slot]).wait()

    @pl.when(i == 0)
    def _():
        gather(0, 0)

    @pl.when(jnp.logical_and(used, new_expert))
    def _():
        wgb_ref[...] = wg_ref[...].astype(BF16)
        wub_ref[...] = wu_ref[...].astype(BF16)
        wdb_ref[...] = wd_ref[...].astype(BF16)

    for slot in range(2):
        @pl.when(jnp.logical_and(used, parity == slot))
        def _(slot=slot):
            wait_tile(slot)
            gather(i + 1, 1 - slot)
            xb = bufs[slot][...].astype(BF16)
            gate = jnp.dot(xb, wgb_ref[...], preferred_element_type=F32)
            up = jnp.dot(xb, wub_ref[...], preferred_element_type=F32)
            hid = (gate * jax.nn.sigmoid(gate) * up).astype(BF16)
            y_ref[...] = jnp.dot(hid, wdb_ref[...], preferred_element_type=F32)

        @pl.when(jnp.logical_and(i == n_used - 1, parity == slot))
        def _(slot=slot):
            wait_tile(1 - slot)

    @pl.when(jnp.logical_not(used))
    def _():
        y_ref[...] = jnp.zeros_like(y_ref)


def _experts(hn2, src_row, tile_e, n_used, w_gate, w_up, w_down):
    n_tok, d = hn2.shape
    tm = TM_EXPERT
    cap = src_row.shape[0]
    n_tiles = cap // tm

    def w_map(i, te, nu, src):
        return (te[i], 0, 0)

    return pl.pallas_call(
        _experts_kernel,
        grid_spec=pltpu.PrefetchScalarGridSpec(
            num_scalar_prefetch=3,
            grid=(n_tiles,),
            in_specs=[
                pl.BlockSpec(memory_space=pl.ANY),
                pl.BlockSpec((None, d, D_FF), w_map),
                pl.BlockSpec((None, d, D_FF), w_map),
                pl.BlockSpec((None, D_FF, d), w_map),
            ],
            out_specs=pl.BlockSpec((tm, d), lambda i, te, nu, src: (i, 0)),
            scratch_shapes=[pltpu.VMEM((d, D_FF), BF16), pltpu.VMEM((d, D_FF), BF16),
                            pltpu.VMEM((D_FF, d), BF16),
                            pltpu.VMEM((tm, d), F32), pltpu.VMEM((tm, d), F32),
                            pltpu.SemaphoreType.DMA((2,))],
        ),
        out_shape=jax.ShapeDtypeStruct((cap, d), F32),
        compiler_params=pltpu.CompilerParams(
            dimension_semantics=("arbitrary",), vmem_limit_bytes=VMEM_LIMIT),
        name="experts",
    )(tile_e, n_used, src_row, hn2, w_gate, w_up, w_down)


def _combine_kernel(pos_ref, pos_next_ref, x2_ref, rw_ref, gf_ref, y_ref, out_ref, y1_ref, y2_ref, sem):
    tc = x2_ref.shape[0]
    step = pl.program_id(0) * pl.num_programs(1) + pl.program_id(1)
    n_steps = pl.num_programs(0) * pl.num_programs(1)
    slot = lax.rem(step, 2)

    def gather(p_ref, slot):
        def body(i, carry):
            for j in range(ROW_DMA_UNROLL):
                r = i * ROW_DMA_UNROLL + j
                pltpu.make_async_copy(y_ref.at[pl.ds(p_ref[0, r], 1)],
                                      y1_ref.at[slot, pl.ds(r, 1)], sem.at[slot]).start()
                pltpu.make_async_copy(y_ref.at[pl.ds(p_ref[1, r], 1)],
                                      y2_ref.at[slot, pl.ds(r, 1)], sem.at[slot]).start()
            return carry
        lax.fori_loop(0, tc // ROW_DMA_UNROLL, body, 0)

    @pl.when(step == 0)
    def _():
        gather(pos_ref, 0)

    @pl.when(step + 1 < n_steps)
    def _():
        gather(pos_next_ref, 1 - slot)

    w_t = jnp.transpose(rw_ref[...])
    pltpu.make_async_copy(y_ref.at[pl.ds(0, tc)], y1_ref.at[slot], sem.at[slot]).wait()
    pltpu.make_async_copy(y_ref.at[pl.ds(0, tc)], y2_ref.at[slot], sem.at[slot]).wait()
    x3 = x2_ref[...] + w_t[:, 0:1] * y1_ref[slot] + w_t[:, 1:2] * y2_ref[slot]
    ms = jnp.mean(x3 * x3, axis=-1, keepdims=True)
    out_ref[...] = x3 * lax.rsqrt(ms + RMS_EPS) * gf_ref[...]


def _combine(x2, pos, rw, y, norm_final):
    b_sz, s_len, d = x2.shape
    tc = TC_COMBINE
    n_s = s_len // tc

    def next_tile(b, s):
        nxt = jnp.minimum(b * n_s + s + 1, b_sz * n_s - 1)
        return (nxt // n_s, 0, nxt % n_s)

    return pl.pallas_call(
        _combine_kernel,
        grid=(b_sz, n_s),
        in_specs=[
            pl.BlockSpec((None, 2, tc), lambda b, s: (b, 0, s), memory_space=pltpu.SMEM),
            pl.BlockSpec((None, 2, tc), next_tile, memory_space=pltpu.SMEM),
            pl.BlockSpec((None, tc, d), lambda b, s: (b, s, 0)),
            pl.BlockSpec((None, 2 * N_HEADS, tc), lambda b, s: (b, 0, s)),
            pl.BlockSpec((1, d), lambda b, s: (0, 0)),
            pl.BlockSpec(memory_space=pl.ANY),
        ],
        out_specs=pl.BlockSpec((None, tc, d), lambda b, s: (b, s, 0)),
        scratch_shapes=[pltpu.VMEM((2, tc, d), F32), pltpu.VMEM((2, tc, d), F32),
                        pltpu.SemaphoreType.DMA((2,))],
        out_shape=jax.ShapeDtypeStruct((b_sz, s_len, d), F32),
        compiler_params=pltpu.CompilerParams(
            dimension_semantics=("arbitrary", "arbitrary"), vmem_limit_bytes=VMEM_LIMIT),
        name="combine",
    )(pos, pos, x2, rw, norm_final[None, :], y)


def _layer(x, norm_mix, w_in, b_in, conv_w, w_out, norm_ffn, w_group, b_group, w_expert, b_expert,
           w_gate, w_up, w_down, norm_out):
    b_sz, s_len, d = x.shape
    n_tok = b_sz * s_len
    a, kt, gt = _inproj(x, norm_mix, w_in, b_in)

    w_router = jnp.zeros((d, ROUTER_COLS), F32)
    w_router = w_router.at[:, 0:N_GROUPS].set(w_group).at[:, EXPERT_COL0:EXPERT_COL0 + N_EXPERTS].set(w_expert)
    b_router = jnp.zeros((1, ROUTER_COLS), F32)
    b_router = b_router.at[0, 0:N_GROUPS].set(b_group).at[0, EXPERT_COL0:EXPERT_COL0 + N_EXPERTS].set(b_expert)
    x2, hn2, ri, rw, counts = _mixer(x, a, kt, gt, conv_w, w_out, norm_ffn, w_router, b_router)

    tm = TM_EXPERT
    cnt = counts[:, 0]
    padded = ((cnt + tm - 1) // tm) * tm
    pends = jnp.cumsum(padded).astype(I32)
    seg = jnp.concatenate([jnp.zeros((1,), I32), pends])
    n_tiles = (n_tok * 2) // tm + N_EXPERTS
    cap = n_tiles * tm
    tile_start = jnp.arange(n_tiles, dtype=I32) * tm
    tile_e = jnp.minimum(jnp.sum((tile_start[:, None] >= pends[None, :]).astype(I32), axis=1),
                         N_EXPERTS - 1)
    n_used = pends[-1:] // tm
    e_sel = ri[:, 0:2, :]
    seg_start = jnp.sum(jnp.where(e_sel[..., None] == jnp.arange(N_EXPERTS, dtype=I32), seg[:-1], 0), axis=-1)
    pos = ri[:, 2:4, :] + seg_start

    src_row = _invert(pos, seg, cnt, cap)
    y = _experts(hn2.reshape(n_tok, d), src_row, tile_e, n_used, w_gate, w_up, w_down)
    return _combine(x2, pos, rw, y, norm_out)


def kernel(x, norm_mix, w_in, b_in, conv_w, w_out, norm_ffn, w_group, b_group, w_expert, b_expert,
           w_gate, w_up, w_down, norm_final):
    depth = norm_mix.shape[0]
    assert depth == 1, "final-norm fusion below assumes a single layer"
    assert x.shape[-1] == 2 * D_MLSTM and x.shape[1] % TS_MIXER == 0
    return _layer(x, norm_mix[0], w_in[0], b_in[0], conv_w[0], w_out[0], norm_ffn[0],
                  w_group[0], b_group[0], w_expert[0], b_expert[0],
                  w_gate[0], w_up[0], w_down[0], norm_final)
```

```python
import jax
import jax.numpy as jnp
from jax import lax
from jax.experimental import pallas as pl
from jax.experimental.pallas import tpu as pltpu

F32 = jnp.float32
BF16 = jnp.bfloat16
I32 = jnp.int32

N_HEADS = 4
HEAD_DIM = 128
D_MLSTM = N_HEADS * HEAD_DIM
D_CONV = 512
N_GROUPS = 4
EXPERTS_PER_GROUP = 8
N_EXPERTS = N_GROUPS * EXPERTS_PER_GROUP
D_FF = 512
RMS_EPS = 1e-6
Q_SCALE = HEAD_DIM ** -0.5

LANES = 128
CHUNK = LANES
ROUTER_COLS = 128
EXPERT_COL0 = 8

TM_INPROJ = 512
TS_MIXER = 512
TQ_INVERT = 512
OUT_COLS = 256
TM_EXPERT = 512
TC_COMBINE = 256
ROW_DMA_UNROLL = 8

VMEM_LIMIT = 56 * 1024 * 1024


def _nt_dot(a, b):
    return lax.dot_general(a, b, (((1,), (1,)), ((), ())), preferred_element_type=F32)


def _lane_cumsum(x, lane):
    s = 1
    while s < LANES:
        x = x + jnp.where(lane >= s, pltpu.roll(x, s, axis=1), jnp.zeros_like(x))
        s *= 2
    return x


def _inproj_kernel(x_ref, g_ref, wa_ref, ba_ref, wk_ref, bk_ref, wg_ref, bg_ref,
                   a_ref, kt_ref, gt_ref):
    x = x_ref[...]
    ms = jnp.mean(x * x, axis=-1, keepdims=True)
    hn = (x * lax.rsqrt(ms + RMS_EPS) * g_ref[...]).astype(BF16)
    n_blk = wa_ref.shape[1] // D_MLSTM
    for j in range(n_blk):
        sl = slice(j * D_MLSTM, (j + 1) * D_MLSTM)
        acc = jnp.dot(hn, wa_ref[:, sl], preferred_element_type=F32) + ba_ref[:, sl]
        if j == 0:
            acc = acc * Q_SCALE
        a_ref[:, sl] = acc.astype(BF16)
    kt_ref[...] = (_nt_dot(wk_ref[...], hn) + bk_ref[...]).astype(BF16)
    gt_ref[...] = _nt_dot(wg_ref[...], hn) + bg_ref[...]


def _inproj(x, norm_mix, w_in, b_in):
    b_sz, s_len, d = x.shape
    tm = TM_INPROJ
    dm = D_MLSTM
    g0 = 4 * dm
    c0 = g0 + 2 * N_HEADS
    cols_a = jnp.concatenate([w_in[:, 0:dm], w_in[:, 2 * dm:4 * dm], w_in[:, c0:c0 + 3 * D_CONV]], axis=1)
    bias_a = jnp.concatenate([b_in[0:dm], b_in[2 * dm:4 * dm], b_in[c0:c0 + 3 * D_CONV]])[None, :]
    wk_t = w_in[:, dm:2 * dm].T
    bk = b_in[dm:2 * dm][:, None]
    wg_t = w_in[:, g0:c0].T
    bg = b_in[g0:c0][:, None]
    n_a = cols_a.shape[1]
    return pl.pallas_call(
        _inproj_kernel,
        grid=(b_sz, s_len // tm),
        in_specs=[
            pl.BlockSpec((None, tm, d), lambda b, s: (b, s, 0)),
            pl.BlockSpec((1, d), lambda b, s: (0, 0)),
            pl.BlockSpec((d, n_a), lambda b, s: (0, 0)),
            pl.BlockSpec((1, n_a), lambda b, s: (0, 0)),
            pl.BlockSpec((dm, d), lambda b, s: (0, 0)),
            pl.BlockSpec((dm, 1), lambda b, s: (0, 0)),
            pl.BlockSpec((2 * N_HEADS, d), lambda b, s: (0, 0)),
            pl.BlockSpec((2 * N_HEADS, 1), lambda b, s: (0, 0)),
        ],
        out_specs=[
            pl.BlockSpec((None, tm, n_a), lambda b, s: (b, s, 0)),
            pl.BlockSpec((None, dm, tm), lambda b, s: (b, 0, s)),
            pl.BlockSpec((None, 2 * N_HEADS, tm), lambda b, s: (b, 0, s)),
        ],
        out_shape=[
            jax.ShapeDtypeStruct((b_sz, s_len, n_a), BF16),
            jax.ShapeDtypeStruct((b_sz, dm, s_len), BF16),
            jax.ShapeDtypeStruct((b_sz, 2 * N_HEADS, s_len), F32),
        ],
        compiler_params=pltpu.CompilerParams(
            dimension_semantics=("arbitrary", "arbitrary"), vmem_limit_bytes=VMEM_LIMIT),
        name="inproj",
    )(x, norm_mix[None, :], cols_a.astype(BF16), bias_a, wk_t.astype(BF16), bk, wg_t.astype(BF16), bg)


def _mixer_kernel(q_ref, v_ref, o_ref, cb_ref, cc_ref, cx_ref, kt_ref, gt_ref, x_ref,
                  cw_ref, wo_ref, g2_ref, wr_ref, br_ref,
                  x2_ref, hn2_ref, ri_ref, rw_ref, cnt_out_ref,
                  ct_ref, m_ref, carry_ref, cnt_ref, mix_ref):
    ts = x_ref.shape[0]
    n_chunks = ts // CHUNK
    first_tile = pl.program_id(1) == 0

    @pl.when(first_tile)
    def _():
        ct_ref[...] = jnp.zeros_like(ct_ref)
        m_ref[...] = jnp.zeros_like(m_ref)
        carry_ref[...] = jnp.zeros_like(carry_ref)

    @pl.when(jnp.logical_and(first_tile, pl.program_id(0) == 0))
    def _():
        cnt_ref[...] = jnp.zeros_like(cnt_ref)

    gates = gt_ref[...]
    li_all = gates[0:N_HEADS]
    fg = gates[N_HEADS:2 * N_HEADS]
    lf_all = jnp.minimum(fg, 0.0) - jnp.log1p(jnp.exp(-jnp.abs(fg)))
    lane4 = lax.broadcasted_iota(I32, (N_HEADS, CHUNK), 1)
    row = lax.broadcasted_iota(I32, (CHUNK, CHUNK), 0)
    col = lax.broadcasted_iota(I32, (CHUNK, CHUNK), 1)
    causal = row >= col
    ones_col = (lax.broadcasted_iota(I32, (CHUNK, HEAD_DIM), 1) == 0).astype(BF16)
    neg_inf = jnp.float32(-jnp.inf)

    m_prev = m_ref[0:N_HEADS, 0:1]
    for c in range(n_chunks):
        rows = slice(c * CHUNK, (c + 1) * CHUNK)
        li = li_all[:, rows]
        lf = lf_all[:, rows]
        b_cum = _lane_cumsum(lf, lane4)
        g_tot = b_cum[:, CHUNK - 1:CHUNK]
        a_end = g_tot - b_cum + li
        m_loc = jnp.max(a_end, axis=1, keepdims=True)
        u = li - b_cum
        m_new = jnp.maximum(g_tot + m_prev, m_loc)
        decay = jnp.exp(g_tot + m_prev - m_new)
        s_fac = jnp.exp(m_loc - m_new)
        w_key = jnp.exp(a_end - m_loc)
        for h in range(N_HEADS):
            hc = slice(h * HEAD_DIM, (h + 1) * HEAD_DIM)
            qh = q_ref[rows, hc]
            kth = kt_ref[hc, rows]
            v_aug = jnp.concatenate([v_ref[rows, hc], ones_col], axis=1)
            s_qk = jnp.dot(qh, kth, preferred_element_type=F32)
            u_m = jnp.where(causal, u[h:h + 1, :], neg_inf)
            m_row = jnp.max(u_m, axis=1, keepdims=True)
            p = (s_qk * jnp.exp(u_m - m_row)).astype(BF16)
            r_intra = jnp.dot(p, v_aug, preferred_element_type=F32)
            ct_old = ct_ref[h]
            r_inter = jnp.dot(qh, ct_old.astype(BF16), preferred_element_type=F32)
            ktw = (kth.astype(F32) * w_key[h:h + 1, :]).astype(BF16)
            kv = jnp.dot(ktw, v_aug, preferred_element_type=F32)
            ct_ref[h] = decay[h:h + 1, :] * ct_old + s_fac[h:h + 1, :] * kv

            mp = m_prev[h:h + 1, :]
            m_tot = jnp.maximum(m_row, mp)
            f_intra = jnp.exp(m_row - m_tot)
            f_inter = jnp.exp(mp - m_tot)
            b_col = jnp.sum(jnp.where(causal, lf[h:h + 1, :], 0.0), axis=1, keepdims=True)
            num = f_intra * r_intra[:, 0:HEAD_DIM] + f_inter * r_inter[:, 0:HEAD_DIM]
            den = (f_intra * r_intra[:, HEAD_DIM:HEAD_DIM + 1]
                   + f_inter * r_inter[:, HEAD_DIM:HEAD_DIM + 1])
            inv = 1.0 / jnp.maximum(jnp.abs(den), jnp.exp(-b_col - m_tot))
            o_gate = jax.nn.sigmoid(o_ref[rows, hc].astype(F32))
            mix_ref[rows, hc] = (o_gate * (num * inv)).astype(BF16)
        m_prev = m_new
    m_ref[0:N_HEADS, :] = jnp.broadcast_to(m_prev, (N_HEADS, LANES))

    u_c = cc_ref[...].astype(F32) * cx_ref[...].astype(F32)
    rid = lax.broadcasted_iota(I32, u_c.shape, 0)
    prev1 = carry_ref[0:1, :]
    prev2 = carry_ref[1:2, :]
    u1 = jnp.where(rid == 0, prev1, pltpu.roll(u_c, 1, axis=0))
    u2 = jnp.where(rid == 0, prev2, jnp.where(rid == 1, prev1, pltpu.roll(u_c, 2, axis=0)))
    z = cw_ref[0:1, :] * u2 + cw_ref[1:2, :] * u1 + cw_ref[2:3, :] * u_c
    mix_ref[:, D_MLSTM:] = (cb_ref[...].astype(F32) * z).astype(BF16)
    carry_ref[0:1, :] = u_c[ts - 1:ts, :]
    carry_ref[1:2, :] = u_c[ts - 2:ts - 1, :]

    d_model = x_ref.shape[1]
    mix = mix_ref[...]
    ssq = jnp.zeros((ts, 1), F32)
    for j in range(d_model // OUT_COLS):
        cs = slice(j * OUT_COLS, (j + 1) * OUT_COLS)
        x2c = x_ref[:, cs] + jnp.dot(mix, wo_ref[:, cs], preferred_element_type=F32)
        x2_ref[:, cs] = x2c
        ssq = ssq + jnp.sum(x2c * x2c, axis=-1, keepdims=True)
    hn2 = x2_ref[...] * lax.rsqrt(ssq * (1.0 / d_model) + RMS_EPS) * g2_ref[...]
    hn2_ref[...] = hn2

    h_hi = hn2.astype(BF16)
    h_lo = (hn2 - h_hi.astype(F32)).astype(BF16)
    hi_both = jnp.dot(h_hi, wr_ref[...], preferred_element_type=F32)
    lo_hi = jnp.dot(h_lo, wr_ref[:, 0:ROUTER_COLS], preferred_element_type=F32)
    logits = hi_both[:, 0:ROUTER_COLS] + (hi_both[:, ROUTER_COLS:] + lo_hi) + br_ref[...]
    lt = jnp.transpose(logits)
    gl = lt[0:N_GROUPS]
    gmax = jnp.max(gl, axis=0, keepdims=True)
    gi = lax.broadcasted_iota(I32, gl.shape, 0)
    g_sel = jnp.min(jnp.where(gl == gmax, gi, N_GROUPS), axis=0, keepdims=True)
    p_g = 1.0 / jnp.sum(jnp.exp(gl - gmax), axis=0, keepdims=True)
    in_g = lt[EXPERT_COL0:EXPERT_COL0 + EXPERTS_PER_GROUP]
    for g in range(1, N_GROUPS):
        lo = EXPERT_COL0 + g * EXPERTS_PER_GROUP
        in_g = jnp.where(g_sel == g, lt[lo:lo + EXPERTS_PER_GROUP], in_g)
    ei = lax.broadcasted_iota(I32, in_g.shape, 0)
    v1 = jnp.max(in_g, axis=0, keepdims=True)
    i1 = jnp.min(jnp.where(in_g == v1, ei, EXPERTS_PER_GROUP), axis=0, keepdims=True)
    rest = jnp.where(ei == i1, neg_inf, in_g)
    v2 = jnp.max(rest, axis=0, keepdims=True)
    i2 = jnp.min(jnp.where(rest == v2, ei, EXPERTS_PER_GROUP), axis=0, keepdims=True)
    t = jnp.exp(v2 - v1)
    w1 = p_g / (1.0 + t)
    w2 = p_g * t / (1.0 + t)
    e1 = g_sel * EXPERTS_PER_GROUP + i1
    e2 = g_sel * EXPERTS_PER_GROUP + i2

    eio = lax.broadcasted_iota(I32, (N_EXPERTS, LANES), 0)
    lane_e = lax.broadcasted_iota(I32, (N_EXPERTS, LANES), 1)
    base = cnt_ref[:, 0:1]
    zero_i = jnp.zeros((1, LANES), I32)
    for k in range(ts // LANES):
        ls = slice(k * LANES, (k + 1) * LANES)
        is1 = eio == e1[:, ls]
        is2 = eio == e2[:, ls]
        onehot = jnp.logical_or(is1, is2).astype(I32)
        incl = _lane_cumsum(onehot, lane_e)
        rank_e = base + incl - onehot
        r1 = jnp.sum(jnp.where(is1, rank_e, 0), axis=0, keepdims=True)
        r2 = jnp.sum(jnp.where(is2, rank_e, 0), axis=0, keepdims=True)
        ri_ref[:, ls] = jnp.concatenate(
            [e1[:, ls], e2[:, ls], r1, r2, zero_i, zero_i, zero_i, zero_i], axis=0)
        base = base + incl[:, LANES - 1:LANES]
    zero_f = jnp.zeros_like(w1)
    rw_ref[...] = jnp.concatenate([w1, w2, zero_f, zero_f, zero_f, zero_f, zero_f, zero_f], axis=0)
    cnt_new = jnp.broadcast_to(base, (N_EXPERTS, LANES))
    cnt_ref[...] = cnt_new
    cnt_out_ref[...] = cnt_new


def _mixer(x, a, kt, gt, conv_w, w_out, norm_ffn, w_router, b_router):
    b_sz, s_len, d = x.shape
    ts = TS_MIXER
    dm = D_MLSTM

    def a_spec(j):
        return pl.BlockSpec((None, ts, dm), lambda b, s, j=j: (b, s, j))

    tok = lambda w: pl.BlockSpec((None, ts, w), lambda b, s: (b, s, 0))
    rowsp = pl.BlockSpec((None, 2 * N_HEADS, ts), lambda b, s: (b, 0, s))
    const = lambda shape: pl.BlockSpec(shape, lambda b, s: tuple(0 for _ in shape))
    wr_hi = w_router.astype(BF16)
    wr_lo = (w_router - wr_hi.astype(F32)).astype(BF16)
    w_router2 = jnp.concatenate([wr_hi, wr_lo], axis=1)
    return pl.pallas_call(
        _mixer_kernel,
        grid=(b_sz, s_len // ts),
        in_specs=[a_spec(0), a_spec(1), a_spec(2), a_spec(3), a_spec(4), a_spec(5),
                  pl.BlockSpec((None, dm, ts), lambda b, s: (b, 0, s)),
                  rowsp, tok(d),
                  const((3, D_CONV)), const((d, d)), const((1, d)),
                  const((d, 2 * ROUTER_COLS)), const((1, ROUTER_COLS))],
        out_specs=[tok(d), tok(d), rowsp, rowsp, const((N_EXPERTS, LANES))],
        out_shape=[
            jax.ShapeDtypeStruct((b_sz, s_len, d), F32),
            jax.ShapeDtypeStruct((b_sz, s_len, d), F32),
            jax.ShapeDtypeStruct((b_sz, 2 * N_HEADS, s_len), I32),
            jax.ShapeDtypeStruct((b_sz, 2 * N_HEADS, s_len), F32),
            jax.ShapeDtypeStruct((N_EXPERTS, LANES), I32),
        ],
        scratch_shapes=[
            pltpu.VMEM((N_HEADS, HEAD_DIM, 2 * HEAD_DIM), F32),
            pltpu.VMEM((2 * N_HEADS, LANES), F32),
            pltpu.VMEM((8, D_CONV), F32),
            pltpu.VMEM((N_EXPERTS, LANES), I32),
            pltpu.VMEM((ts, d), BF16),
        ],
        compiler_params=pltpu.CompilerParams(
            dimension_semantics=("arbitrary", "arbitrary"), vmem_limit_bytes=VMEM_LIMIT),
        name="mixer",
    )(a, a, a, a, a, a, kt, gt, x, conv_w, w_out.astype(BF16), norm_ffn[None, :], w_router2, b_router)


def _invert_kernel(seg_ref, pos_ref, src_ref):
    tq = pos_ref.shape[0] // 2
    tm = TM_EXPERT
    step = pl.program_id(0)

    @pl.when(step == 0)
    def _():
        def zero_tile(first):
            def body(i, carry):
                for j in range(ROW_DMA_UNROLL):
                    src_ref[first + i * ROW_DMA_UNROLL + j] = 0
                return carry
            lax.fori_loop(0, tm // ROW_DMA_UNROLL, body, 0)

        for e in range(N_EXPERTS):
            @pl.when(seg_ref[e + 1] > seg_ref[e])
            def _(e=e):
                zero_tile(seg_ref[e + 1] - tm)

        def tail(t, carry):
            zero_tile(t * tm)
            return carry
        lax.fori_loop(seg_ref[N_EXPERTS] // tm, src_ref.shape[0] // tm, tail, 0)

    tok0 = step * tq

    def body(i, carry):
        for j in range(ROW_DMA_UNROLL):
            r = i * ROW_DMA_UNROLL + j
            src_ref[pos_ref[r]] = tok0 + r
            src_ref[pos_ref[tq + r]] = tok0 + r
        return carry

    lax.fori_loop(0, tq // ROW_DMA_UNROLL, body, 0)


def _tile_major(pos, t):
    b_sz, _, s_len = pos.shape
    return pos.reshape(b_sz, 2, s_len // t, t).transpose(0, 2, 1, 3).reshape(-1)


def _invert(pos, seg, cap):
    b_sz, _, s_len = pos.shape
    tq = TQ_INVERT
    return pl.pallas_call(
        _invert_kernel,
        grid_spec=pltpu.PrefetchScalarGridSpec(
            num_scalar_prefetch=1,
            grid=(b_sz * s_len // tq,),
            in_specs=[pl.BlockSpec((2 * tq,), lambda g, sg: (g,), memory_space=pltpu.SMEM)],
            out_specs=pl.BlockSpec(memory_space=pltpu.SMEM),
        ),
        out_shape=jax.ShapeDtypeStruct((cap,), I32),
        compiler_params=pltpu.CompilerParams(dimension_semantics=("arbitrary",)),
        name="invert",
    )(seg, _tile_major(pos, tq))


def _experts_kernel(te_ref, nu_ref, src_ref, h_ref, wg_ref, wu_ref, wd_ref, y_ref,
                    wgb_ref, wub_ref, wdb_ref, xbuf0_ref, xbuf1_ref, gsem):
    tm = y_ref.shape[0]
    i = pl.program_id(0)
    n_used = nu_ref[0]
    e = te_ref[i]
    e_prev = te_ref[jnp.maximum(i - 1, 0)]
    new_expert = jnp.logical_or(i == 0, e != e_prev)
    used = i < n_used
    parity = lax.rem(i, 2)
    bufs = (xbuf0_ref, xbuf1_ref)

    def gather(tile, slot):
        base = tile * tm
        for r in range(tm):
            pltpu.make_async_copy(h_ref.at[pl.ds(src_ref[base + r], 1)],
                                  bufs[slot].at[pl.ds(r, 1)], gsem.at[slot]).start(priority=1)

    def wait_tile(slot):
        pltpu.make_async_copy(h_ref.at[pl.ds(0, tm)], bufs[slot], gsem.at[slot]).wait()

    @pl.when(i == 0)
    def _():
        gather(0, 0)

    @pl.when(jnp.logical_and(used, new_expert))
    def _():
        wgb_ref[...] = wg_ref[...].astype(BF16)
        wub_ref[...] = wu_ref[...].astype(BF16)
        wdb_ref[...] = wd_ref[...].astype(BF16)

    for slot in range(2):
        @pl.when(jnp.logical_and(used, parity == slot))
        def _(slot=slot):
            wait_tile(slot)
            gather(i + 1, 1 - slot)
            xb = bufs[slot][...].astype(BF16)
            gate = jnp.dot(xb, wgb_ref[...], preferred_element_type=F32)
            up = jnp.dot(xb, wub_ref[...], preferred_element_type=F32)
            hid = (gate * jax.nn.sigmoid(gate) * up).astype(BF16)
            y_ref[...] = jnp.dot(hid, wdb_ref[...], preferred_element_type=F32)

        @pl.when(jnp.logical_and(i == n_used - 1, parity == slot))
        def _(slot=slot):
            wait_tile(1 - slot)

    @pl.when(jnp.logical_not(used))
    def _():
        y_ref[...] = jnp.zeros_like(y_ref)


def _experts(hn2, src_row, tile_e, n_used, w_gate, w_up, w_down):
    n_tok, d = hn2.shape
    tm = TM_EXPERT
    cap = src_row.shape[0]
    n_tiles = cap // tm

    def w_map(i, te, nu, src):
        return (te[i], 0, 0)

    return pl.pallas_call(
        _experts_kernel,
        grid_spec=pltpu.PrefetchScalarGridSpec(
            num_scalar_prefetch=3,
            grid=(n_tiles,),
            in_specs=[
                pl.BlockSpec(memory_space=pl.ANY),
                pl.BlockSpec((None, d, D_FF), w_map),
                pl.BlockSpec((None, d, D_FF), w_map),
                pl.BlockSpec((None, D_FF, d), w_map),
            ],
            out_specs=pl.BlockSpec((tm, d), lambda i, te, nu, src: (i, 0)),
            scratch_shapes=[pltpu.VMEM((d, D_FF), BF16), pltpu.VMEM((d, D_FF), BF16),
                            pltpu.VMEM((D_FF, d), BF16),
                            pltpu.VMEM((tm, d), F32), pltpu.VMEM((tm, d), F32),
                            pltpu.SemaphoreType.DMA((2,))],
        ),
        out_shape=jax.ShapeDtypeStruct((cap, d), F32),
        compiler_params=pltpu.CompilerParams(
            dimension_semantics=("arbitrary",), vmem_limit_bytes=VMEM_LIMIT),
        name="experts",
    )(tile_e, n_used, src_row, hn2, w_gate, w_up, w_down)


def _combine_kernel(pos_ref, pos_next_ref, x2_ref, rw_ref, gf_ref, y_ref, out_ref, y1_ref, y2_ref, sem):
    tc = x2_ref.shape[0]
    step = pl.program_id(0) * pl.num_programs(1) + pl.program_id(1)
    n_steps = pl.num_programs(0) * pl.num_programs(1)
    slot = lax.rem(step, 2)

    def gather(p_ref, slot):
        def body(i, carry):
            for j in range(ROW_DMA_UNROLL):
                r = i * ROW_DMA_UNROLL + j
                pltpu.make_async_copy(y_ref.at[pl.ds(p_ref[0, r], 1)],
                                      y1_ref.at[slot, pl.ds(r, 1)], sem.at[slot]).start(priority=0)
                pltpu.make_async_copy(y_ref.at[pl.ds(p_ref[1, r], 1)],
                                      y2_ref.at[slot, pl.ds(r, 1)], sem.at[slot]).start(priority=1)
            return carry
        lax.fori_loop(0, tc // ROW_DMA_UNROLL, body, 0)

    @pl.when(step == 0)
    def _():
        gather(pos_ref, 0)

    @pl.when(step + 1 < n_steps)
    def _():
        gather(pos_next_ref, 1 - slot)

    w_t = jnp.transpose(rw_ref[...])
    pltpu.make_async_copy(y_ref.at[pl.ds(0, tc)], y1_ref.at[slot], sem.at[slot]).wait()
    pltpu.make_async_copy(y_ref.at[pl.ds(0, tc)], y2_ref.at[slot], sem.at[slot]).wait()
    x3 = x2_ref[...] + w_t[:, 0:1] * y1_ref[slot] + w_t[:, 1:2] * y2_ref[slot]
    ms = jnp.mean(x3 * x3, axis=-1, keepdims=True)
    out_ref[...] = x3 * lax.rsqrt(ms + RMS_EPS) * gf_ref[...]


def _combine(x2, pos, rw, y, norm_final):
    b_sz, s_len, d = x2.shape
    tc = TC_COMBINE
    n_s = s_len // tc

    def next_tile(b, s):
        nxt = jnp.minimum(b * n_s + s + 1, b_sz * n_s - 1)
        return (nxt // n_s, 0, nxt % n_s)

    return pl.pallas_call(
        _combine_kernel,
        grid=(b_sz, n_s),
        in_specs=[
            pl.BlockSpec((None, 2, tc), lambda b, s: (b, 0, s), memory_space=pltpu.SMEM),
            pl.BlockSpec((None, 2, tc), next_tile, memory_space=pltpu.SMEM),
            pl.BlockSpec((None, tc, d), lambda b, s: (b, s, 0)),
            pl.BlockSpec((None, 2 * N_HEADS, tc), lambda b, s: (b, 0, s)),
            pl.BlockSpec((1, d), lambda b, s: (0, 0)),
            pl.BlockSpec(memory_space=pl.ANY),
        ],
        out_specs=pl.BlockSpec((None, tc, d), lambda b, s: (b, s, 0)),
        scratch_shapes=[pltpu.VMEM((2, tc, d), F32), pltpu.VMEM((2, tc, d), F32),
                        pltpu.SemaphoreType.DMA((2,))],
        out_shape=jax.ShapeDtypeStruct((b_sz, s_len, d), F32),
        compiler_params=pltpu.CompilerParams(
            dimension_semantics=("arbitrary", "arbitrary"), vmem_limit_bytes=VMEM_LIMIT),
        name="combine",
    )(pos, pos, x2, rw, norm_final[None, :], y)


def _layer(x, norm_mix, w_in, b_in, conv_w, w_out, norm_ffn, w_group, b_group, w_expert, b_expert,
           w_gate, w_up, w_down, norm_out):
    b_sz, s_len, d = x.shape
    n_tok = b_sz * s_len
    a, kt, gt = _inproj(x, norm_mix, w_in, b_in)

    w_router = jnp.zeros((d, ROUTER_COLS), F32)
    w_router = w_router.at[:, 0:N_GROUPS].set(w_group).at[:, EXPERT_COL0:EXPERT_COL0 + N_EXPERTS].set(w_expert)
    b_router = jnp.zeros((1, ROUTER_COLS), F32)
    b_router = b_router.at[0, 0:N_GROUPS].set(b_group).at[0, EXPERT_COL0:EXPERT_COL0 + N_EXPERTS].set(b_expert)
    x2, hn2, ri, rw, counts = _mixer(x, a, kt, gt, conv_w, w_out, norm_ffn, w_router, b_router)

    tm = TM_EXPERT
    cnt = counts[:, 0]
    padded = ((cnt + tm - 1) // tm) * tm
    pends = jnp.cumsum(padded).astype(I32)
    seg = jnp.concatenate([jnp.zeros((1,), I32), pends])
    n_tiles = (n_tok * 2) // tm + N_EXPERTS
    cap = n_tiles * tm
    tile_start = jnp.arange(n_tiles, dtype=I32) * tm
    tile_e = jnp.minimum(jnp.sum((tile_start[:, None] >= pends[None, :]).astype(I32), axis=1),
                         N_EXPERTS - 1)
    n_used = pends[-1:] // tm
    e_sel = ri[:, 0:2, :]
    seg_start = jnp.sum(jnp.where(e_sel[..., None] == jnp.arange(N_EXPERTS, dtype=I32), seg[:-1], 0), axis=-1)
    pos = ri[:, 2:4, :] + seg_start

    src_row = _invert(pos, seg, cap)
    y = _experts(hn2.reshape(n_tok, d), src_row, tile_e, n_used, w_gate, w_up, w_down)
    return _combine(x2, pos, rw, y, norm_out)


def kernel(x, norm_mix, w_in, b_in, conv_w, w_out, norm_ffn, w_group, b_group, w_expert, b_expert,
           w_gate, w_up, w_down, norm_final):
    depth = norm_mix.shape[0]
    assert depth == 1, "final-norm fusion below assumes a single layer"
    assert x.shape[-1] == 2 * D_MLSTM and x.shape[1] % TS_MIXER == 0
    return _layer(x, norm_mix[0], w_in[0], b_in[0], conv_w[0], w_out[0], norm_ffn[0],
                  w_group[0], b_group[0], w_expert[0], b_expert[0],
                  w_gate[0], w_up[0], w_down[0], norm_final)
```

```python
import jax
import jax.numpy as jnp
from jax import lax
from jax.experimental import pallas as pl
from jax.experimental.pallas import tpu as pltpu

F32 = jnp.float32
BF16 = jnp.bfloat16
I32 = jnp.int32

N_HEADS = 4
HEAD_DIM = 128
D_MLSTM = N_HEADS * HEAD_DIM
D_CONV = 512
N_GROUPS = 4
EXPERTS_PER_GROUP = 8
N_EXPERTS = N_GROUPS * EXPERTS_PER_GROUP
D_FF = 512
RMS_EPS = 1e-6
Q_SCALE = HEAD_DIM ** -0.5

LANES = 128
CHUNK = LANES
ROUTER_COLS = 128
EXPERT_COL0 = 8

TM_INPROJ = 512
TS_MIXER = 512
TQ_INVERT = 512
OUT_COLS = 256
TM_EXPERT = 512
TC_COMBINE = 256
ROW_DMA_UNROLL = 8

VMEM_LIMIT = 56 * 1024 * 1024


def _nt_dot(a, b):
    return lax.dot_general(a, b, (((1,), (1,)), ((), ())), preferred_element_type=F32)


def _lane_cumsum(x, lane):
    s = 1
    while s < LANES:
        x = x + jnp.where(lane >= s, pltpu.roll(x, s, axis=1), jnp.zeros_like(x))
        s *= 2
    return x


def _inproj_kernel(x_ref, g_ref, wa_ref, ba_ref, wk_ref, bk_ref, wg_ref, bg_ref,
                   a_ref, kt_ref, gt_ref):
    x = x_ref[...]
    ms = jnp.mean(x * x, axis=-1, keepdims=True)
    hn = (x * lax.rsqrt(ms + RMS_EPS) * g_ref[...]).astype(BF16)
    n_blk = wa_ref.shape[1] // D_MLSTM
    for j in range(n_blk):
        sl = slice(j * D_MLSTM, (j + 1) * D_MLSTM)
        acc = jnp.dot(hn, wa_ref[:, sl], preferred_element_type=F32) + ba_ref[:, sl]
        if j == 0:
            acc = acc * Q_SCALE
        a_ref[:, sl] = acc.astype(BF16)
    kt_ref[...] = (_nt_dot(wk_ref[...], hn) + bk_ref[...]).astype(BF16)
    gt_ref[...] = _nt_dot(wg_ref[...], hn) + bg_ref[...]


def _inproj(x, norm_mix, w_in, b_in):
    b_sz, s_len, d = x.shape
    tm = TM_INPROJ
    dm = D_MLSTM
    g0 = 4 * dm
    c0 = g0 + 2 * N_HEADS
    cols_a = jnp.concatenate([w_in[:, 0:dm], w_in[:, 2 * dm:4 * dm], w_in[:, c0:c0 + 3 * D_CONV]], axis=1)
    bias_a = jnp.concatenate([b_in[0:dm], b_in[2 * dm:4 * dm], b_in[c0:c0 + 3 * D_CONV]])[None, :]
    wk_t = w_in[:, dm:2 * dm].T
    bk = b_in[dm:2 * dm][:, None]
    wg_t = w_in[:, g0:c0].T
    bg = b_in[g0:c0][:, None]
    n_a = cols_a.shape[1]
    return pl.pallas_call(
        _inproj_kernel,
        grid=(b_sz, s_len // tm),
        in_specs=[
            pl.BlockSpec((None, tm, d), lambda b, s: (b, s, 0)),
            pl.BlockSpec((1, d), lambda b, s: (0, 0)),
            pl.BlockSpec((d, n_a), lambda b, s: (0, 0)),
            pl.BlockSpec((1, n_a), lambda b, s: (0, 0)),
            pl.BlockSpec((dm, d), lambda b, s: (0, 0)),
            pl.BlockSpec((dm, 1), lambda b, s: (0, 0)),
            pl.BlockSpec((2 * N_HEADS, d), lambda b, s: (0, 0)),
            pl.BlockSpec((2 * N_HEADS, 1), lambda b, s: (0, 0)),
        ],
        out_specs=[
            pl.BlockSpec((None, tm, n_a), lambda b, s: (b, s, 0)),
            pl.BlockSpec((None, dm, tm), lambda b, s: (b, 0, s)),
            pl.BlockSpec((None, 2 * N_HEADS, tm), lambda b, s: (b, 0, s)),
        ],
        out_shape=[
            jax.ShapeDtypeStruct((b_sz, s_len, n_a), BF16),
            jax.ShapeDtypeStruct((b_sz, dm, s_len), BF16),
            jax.ShapeDtypeStruct((b_sz, 2 * N_HEADS, s_len), F32),
        ],
        compiler_params=pltpu.CompilerParams(
            dimension_semantics=("arbitrary", "arbitrary"), vmem_limit_bytes=VMEM_LIMIT),
        name="inproj",
    )(x, norm_mix[None, :], cols_a.astype(BF16), bias_a, wk_t.astype(BF16), bk, wg_t.astype(BF16), bg)


def _mixer_kernel(q_ref, v_ref, o_ref, cb_ref, cc_ref, cx_ref, kt_ref, gt_ref, x_ref,
                  cw_ref, wo_ref, g2_ref, wr_ref, br_ref,
                  x2_ref, hn2_ref, ri_ref, rw_ref, cnt_out_ref,
                  ct_ref, m_ref, carry_ref, cnt_ref, mix_ref):
    ts = x_ref.shape[0]
    n_chunks = ts // CHUNK
    first_tile = pl.program_id(1) == 0

    @pl.when(first_tile)
    def _():
        ct_ref[...] = jnp.zeros_like(ct_ref)
        m_ref[...] = jnp.zeros_like(m_ref)
        carry_ref[...] = jnp.zeros_like(carry_ref)

    @pl.when(jnp.logical_and(first_tile, pl.program_id(0) == 0))
    def _():
        cnt_ref[...] = jnp.zeros_like(cnt_ref)

    gates = gt_ref[...]
    li_all = gates[0:N_HEADS]
    fg = gates[N_HEADS:2 * N_HEADS]
    lf_all = jnp.minimum(fg, 0.0) - jnp.log1p(jnp.exp(-jnp.abs(fg)))
    lane4 = lax.broadcasted_iota(I32, (N_HEADS, CHUNK), 1)
    row = lax.broadcasted_iota(I32, (CHUNK, CHUNK), 0)
    col = lax.broadcasted_iota(I32, (CHUNK, CHUNK), 1)
    causal = row >= col
    ones_col = (lax.broadcasted_iota(I32, (CHUNK, HEAD_DIM), 1) == 0).astype(BF16)
    neg_inf = jnp.float32(-jnp.inf)

    m_prev = m_ref[0:N_HEADS, 0:1]
    for c in range(n_chunks):
        rows = slice(c * CHUNK, (c + 1) * CHUNK)
        li = li_all[:, rows]
        lf = lf_all[:, rows]
        b_cum = _lane_cumsum(lf, lane4)
        g_tot = b_cum[:, CHUNK - 1:CHUNK]
        a_end = g_tot - b_cum + li
        m_loc = jnp.max(a_end, axis=1, keepdims=True)
        u = li - b_cum
        m_new = jnp.maximum(g_tot + m_prev, m_loc)
        decay = jnp.exp(g_tot + m_prev - m_new)
        s_fac = jnp.exp(m_loc - m_new)
        w_key = jnp.exp(a_end - m_loc)
        for h in range(N_HEADS):
            hc = slice(h * HEAD_DIM, (h + 1) * HEAD_DIM)
            qh = q_ref[rows, hc]
            kth = kt_ref[hc, rows]
            v_aug = jnp.concatenate([v_ref[rows, hc], ones_col], axis=1)
            s_qk = jnp.dot(qh, kth, preferred_element_type=F32)
            u_m = jnp.where(causal, u[h:h + 1, :], neg_inf)
            m_row = jnp.max(u_m, axis=1, keepdims=True)
            p = (s_qk * jnp.exp(u_m - m_row)).astype(BF16)
            r_intra = jnp.dot(p, v_aug, preferred_element_type=F32)
            ct_old = ct_ref[h]
            r_inter = jnp.dot(qh, ct_old.astype(BF16), preferred_element_type=F32)
            ktw = (kth.astype(F32) * w_key[h:h + 1, :]).astype(BF16)
            kv = jnp.dot(ktw, v_aug, preferred_element_type=F32)
            ct_ref[h] = decay[h:h + 1, :] * ct_old + s_fac[h:h + 1, :] * kv

            mp = m_prev[h:h + 1, :]
            m_tot = jnp.maximum(m_row, mp)
            f_intra = jnp.exp(m_row - m_tot)
            f_inter = jnp.exp(mp - m_tot)
            b_col = jnp.sum(jnp.where(causal, lf[h:h + 1, :], 0.0), axis=1, keepdims=True)
            num = f_intra * r_intra[:, 0:HEAD_DIM] + f_inter * r_inter[:, 0:HEAD_DIM]
            den = (f_intra * r_intra[:, HEAD_DIM:HEAD_DIM + 1]
                   + f_inter * r_inter[:, HEAD_DIM:HEAD_DIM + 1])
            inv = 1.0 / jnp.maximum(jnp.abs(den), jnp.exp(-b_col - m_tot))
            o_gate = jax.nn.sigmoid(o_ref[rows, hc].astype(F32))
            mix_ref[rows, hc] = (o_gate * (num * inv)).astype(BF16)
        m_prev = m_new
    m_ref[0:N_HEADS, :] = jnp.broadcast_to(m_prev, (N_HEADS, LANES))

    u_c = cc_ref[...].astype(F32) * cx_ref[...].astype(F32)
    rid = lax.broadcasted_iota(I32, u_c.shape, 0)
    prev1 = carry_ref[0:1, :]
    prev2 = carry_ref[1:2, :]
    u1 = jnp.where(rid == 0, prev1, pltpu.roll(u_c, 1, axis=0))
    u2 = jnp.where(rid == 0, prev2, jnp.where(rid == 1, prev1, pltpu.roll(u_c, 2, axis=0)))
    z = cw_ref[0:1, :] * u2 + cw_ref[1:2, :] * u1 + cw_ref[2:3, :] * u_c
    mix_ref[:, D_MLSTM:] = (cb_ref[...].astype(F32) * z).astype(BF16)
    carry_ref[0:1, :] = u_c[ts - 1:ts, :]
    carry_ref[1:2, :] = u_c[ts - 2:ts - 1, :]

    d_model = x_ref.shape[1]
    mix = mix_ref[...]
    ssq = jnp.zeros((ts, 1), F32)
    for j in range(d_model // OUT_COLS):
        cs = slice(j * OUT_COLS, (j + 1) * OUT_COLS)
        x2c = x_ref[:, cs] + jnp.dot(mix, wo_ref[:, cs], preferred_element_type=F32)
        x2_ref[:, cs] = x2c
        ssq = ssq + jnp.sum(x2c * x2c, axis=-1, keepdims=True)
    hn2 = x2_ref[...] * lax.rsqrt(ssq * (1.0 / d_model) + RMS_EPS) * g2_ref[...]
    hn2_ref[...] = hn2

    h_hi = hn2.astype(BF16)
    h_lo = (hn2 - h_hi.astype(F32)).astype(BF16)
    hi_both = jnp.dot(h_hi, wr_ref[...], preferred_element_type=F32)
    lo_hi = jnp.dot(h_lo, wr_ref[:, 0:ROUTER_COLS], preferred_element_type=F32)
    logits = hi_both[:, 0:ROUTER_COLS] + (hi_both[:, ROUTER_COLS:] + lo_hi) + br_ref[...]
    lt = jnp.transpose(logits)
    gl = lt[0:N_GROUPS]
    gmax = jnp.max(gl, axis=0, keepdims=True)
    gi = lax.broadcasted_iota(I32, gl.shape, 0)
    g_sel = jnp.min(jnp.where(gl == gmax, gi, N_GROUPS), axis=0, keepdims=True)
    p_g = 1.0 / jnp.sum(jnp.exp(gl - gmax), axis=0, keepdims=True)
    in_g = lt[EXPERT_COL0:EXPERT_COL0 + EXPERTS_PER_GROUP]
    for g in range(1, N_GROUPS):
        lo = EXPERT_COL0 + g * EXPERTS_PER_GROUP
        in_g = jnp.where(g_sel == g, lt[lo:lo + EXPERTS_PER_GROUP], in_g)
    ei = lax.broadcasted_iota(I32, in_g.shape, 0)
    v1 = jnp.max(in_g, axis=0, keepdims=True)
    i1 = jnp.min(jnp.where(in_g == v1, ei, EXPERTS_PER_GROUP), axis=0, keepdims=True)
    rest = jnp.where(ei == i1, neg_inf, in_g)
    v2 = jnp.max(rest, axis=0, keepdims=True)
    i2 = jnp.min(jnp.where(rest == v2, ei, EXPERTS_PER_GROUP), axis=0, keepdims=True)
    t = jnp.exp(v2 - v1)
    w1 = p_g / (1.0 + t)
    w2 = p_g * t / (1.0 + t)
    e1 = g_sel * EXPERTS_PER_GROUP + i1
    e2 = g_sel * EXPERTS_PER_GROUP + i2

    eio = lax.broadcasted_iota(I32, (N_EXPERTS, LANES), 0)
    lane_e = lax.broadcasted_iota(I32, (N_EXPERTS, LANES), 1)
    base = cnt_ref[:, 0:1]
    zero_i = jnp.zeros((1, LANES), I32)
    for k in range(ts // LANES):
        ls = slice(k * LANES, (k + 1) * LANES)
        is1 = eio == e1[:, ls]
        is2 = eio == e2[:, ls]
        onehot = jnp.logical_or(is1, is2).astype(I32)
        incl = _lane_cumsum(onehot, lane_e)
        rank_e = base + incl - onehot
        r1 = jnp.sum(jnp.where(is1, rank_e, 0), axis=0, keepdims=True)
        r2 = jnp.sum(jnp.where(is2, rank_e, 0), axis=0, keepdims=True)
        ri_ref[:, ls] = jnp.concatenate(
            [e1[:, ls], e2[:, ls], r1, r2, zero_i, zero_i, zero_i, zero_i], axis=0)
        base = base + incl[:, LANES - 1:LANES]
    zero_f = jnp.zeros_like(w1)
    rw_ref[...] = jnp.concatenate([w1, w2, zero_f, zero_f, zero_f, zero_f, zero_f, zero_f], axis=0)
    cnt_new = jnp.broadcast_to(base, (N_EXPERTS, LANES))
    cnt_ref[...] = cnt_new
    cnt_out_ref[...] = cnt_new


def _mixer(x, a, kt, gt, conv_w, w_out, norm_ffn, w_router, b_router):
    b_sz, s_len, d = x.shape
    ts = TS_MIXER
    dm = D_MLSTM

    def a_spec(j):
        return pl.BlockSpec((None, ts, dm), lambda b, s, j=j: (b, s, j))

    tok = lambda w: pl.BlockSpec((None, ts, w), lambda b, s: (b, s, 0))
    rowsp = pl.BlockSpec((None, 2 * N_HEADS, ts), lambda b, s: (b, 0, s))
    const = lambda shape: pl.BlockSpec(shape, lambda b, s: tuple(0 for _ in shape))
    wr_hi = w_router.astype(BF16)
    wr_lo = (w_router - wr_hi.astype(F32)).astype(BF16)
    w_router2 = jnp.concatenate([wr_hi, wr_lo], axis=1)
    return pl.pallas_call(
        _mixer_kernel,
        grid=(b_sz, s_len // ts),
        in_specs=[a_spec(0), a_spec(1), a_spec(2), a_spec(3), a_spec(4), a_spec(5),
                  pl.BlockSpec((None, dm, ts), lambda b, s: (b, 0, s)),
                  rowsp, tok(d),
                  const((3, D_CONV)), const((d, d)), const((1, d)),
                  const((d, 2 * ROUTER_COLS)), const((1, ROUTER_COLS))],
        out_specs=[tok(d), tok(d), rowsp, rowsp, const((N_EXPERTS, LANES))],
        out_shape=[
            jax.ShapeDtypeStruct((b_sz, s_len, d), F32),
            jax.ShapeDtypeStruct((b_sz, s_len, d), F32),
            jax.ShapeDtypeStruct((b_sz, 2 * N_HEADS, s_len), I32),
            jax.ShapeDtypeStruct((b_sz, 2 * N_HEADS, s_len), F32),
            jax.ShapeDtypeStruct((N_EXPERTS, LANES), I32),
        ],
        scratch_shapes=[
            pltpu.VMEM((N_HEADS, HEAD_DIM, 2 * HEAD_DIM), F32),
            pltpu.VMEM((2 * N_HEADS, LANES), F32),
            pltpu.VMEM((8, D_CONV), F32),
            pltpu.VMEM((N_EXPERTS, LANES), I32),
            pltpu.VMEM((ts, d), BF16),
        ],
        compiler_params=pltpu.CompilerParams(
            dimension_semantics=("arbitrary", "arbitrary"), vmem_limit_bytes=VMEM_LIMIT),
        name="mixer",
    )(a, a, a, a, a, a, kt, gt, x, conv_w, w_out.astype(BF16), norm_ffn[None, :], w_router2, b_router)


def _invert_kernel(seg_ref, pos_ref, src_ref):
    tq = pos_ref.shape[0] // 2
    tm = TM_EXPERT
    step = pl.program_id(0)

    @pl.when(step == 0)
    def _():
        def zero_tile(first):
            def body(i, carry):
                for j in range(ROW_DMA_UNROLL):
                    src_ref[first + i * ROW_DMA_UNROLL + j] = 0
                return carry
            lax.fori_loop(0, tm // ROW_DMA_UNROLL, body, 0)

        for e in range(N_EXPERTS):
            @pl.when(seg_ref[e + 1] > seg_ref[e])
            def _(e=e):
                zero_tile(seg_ref[e + 1] - tm)

        def tail(t, carry):
            zero_tile(t * tm)
            return carry
        lax.fori_loop(seg_ref[N_EXPERTS] // tm, src_ref.shape[0] // tm, tail, 0)

    tok0 = step * tq

    def body(i, carry):
        for j in range(ROW_DMA_UNROLL):
            r = i * ROW_DMA_UNROLL + j
            src_ref[pos_ref[r]] = tok0 + r
            src_ref[pos_ref[tq + r]] = tok0 + r
        return carry

    lax.fori_loop(0, tq // ROW_DMA_UNROLL, body, 0)


def _tile_major(pos, t):
    b_sz, _, s_len = pos.shape
    return pos.reshape(b_sz, 2, s_len // t, t).transpose(0, 2, 1, 3).reshape(-1)


def _invert(pos, seg, cap):
    b_sz, _, s_len = pos.shape
    tq = TQ_INVERT
    return pl.pallas_call(
        _invert_kernel,
        grid_spec=pltpu.PrefetchScalarGridSpec(
            num_scalar_prefetch=1,
            grid=(b_sz * s_len // tq,),
            in_specs=[pl.BlockSpec((2 * tq,), lambda g, sg: (g,), memory_space=pltpu.SMEM)],
            out_specs=pl.BlockSpec(memory_space=pltpu.SMEM),
        ),
        out_shape=jax.ShapeDtypeStruct((cap,), I32),
        compiler_params=pltpu.CompilerParams(dimension_semantics=("arbitrary",)),
        name="invert",
    )(seg, _tile_major(pos, tq))


def _experts_kernel(te_ref, nu_ref, src_ref, h_ref, wg_ref, wu_ref, wd_ref, y_ref,
                    wgb_ref, wub_ref, wdb_ref, xbuf0_ref, xbuf1_ref, gsem):
    tm = y_ref.shape[0]
    i = pl.program_id(0)
    n_used = nu_ref[0]
    e = te_ref[i]
    e_prev = te_ref[jnp.maximum(i - 1, 0)]
    new_expert = jnp.logical_or(i == 0, e != e_prev)
    used = i < n_used
    parity = lax.rem(i, 2)
    bufs = (xbuf0_ref, xbuf1_ref)

    def gather(tile, slot):
        base = tile * tm
        for r in range(tm):
            pltpu.make_async_copy(h_ref.at[pl.ds(src_ref[base + r], 1)],
                                  bufs[slot].at[pl.ds(r, 1)], gsem.at[slot]).start(priority=r % 2)

    def wait_tile(slot):
        pltpu.make_async_copy(h_ref.at[pl.ds(0, tm)], bufs[slot], gsem.at[slot]).wait()

    @pl.when(i == 0)
    def _():
        gather(0, 0)

    @pl.when(jnp.logical_and(used, new_expert))
    def _():
        wgb_ref[...] = wg_ref[...].astype(BF16)
        wub_ref[...] = wu_ref[...].astype(BF16)
        wdb_ref[...] = wd_ref[...].astype(BF16)

    for slot in range(2):
        @pl.when(jnp.logical_and(used, parity == slot))
        def _(slot=slot):
            wait_tile(slot)
            gather(i + 1, 1 - slot)
            xb = bufs[slot][...].astype(BF16)
            gate = jnp.dot(xb, wgb_ref[...], preferred_element_type=F32)
            up = jnp.dot(xb, wub_ref[...], preferred_element_type=F32)
            hid = (gate * jax.nn.sigmoid(gate) * up).astype(BF16)
            y_ref[...] = jnp.dot(hid, wdb_ref[...], preferred_element_type=F32)

        @pl.when(jnp.logical_and(i == n_used - 1, parity == slot))
        def _(slot=slot):
            wait_tile(1 - slot)

    @pl.when(jnp.logical_not(used))
    def _():
        y_ref[...] = jnp.zeros_like(y_ref)


def _experts(hn2, src_row, tile_e, n_used, w_gate, w_up, w_down):
    n_tok, d = hn2.shape
    tm = TM_EXPERT
    cap = src_row.shape[0]
    n_tiles = cap // tm

    def w_map(i, te, nu, src):
        return (te[i], 0, 0)

    return pl.pallas_call(
        _experts_kernel,
        grid_spec=pltpu.PrefetchScalarGridSpec(
            num_scalar_prefetch=3,
            grid=(n_tiles,),
            in_specs=[
                pl.BlockSpec(memory_space=pl.ANY),
                pl.BlockSpec((None, d, D_FF), w_map),
                pl.BlockSpec((None, d, D_FF), w_map),
                pl.BlockSpec((None, D_FF, d), w_map),
            ],
            out_specs=pl.BlockSpec((tm, d), lambda i, te, nu, src: (i, 0)),
            scratch_shapes=[pltpu.VMEM((d, D_FF), BF16), pltpu.VMEM((d, D_FF), BF16),
                            pltpu.VMEM((D_FF, d), BF16),
                            pltpu.VMEM((tm, d), F32), pltpu.VMEM((tm, d), F32),
                            pltpu.SemaphoreType.DMA((2,))],
        ),
        out_shape=jax.ShapeDtypeStruct((cap, d), F32),
        compiler_params=pltpu.CompilerParams(
            dimension_semantics=("arbitrary",), vmem_limit_bytes=VMEM_LIMIT),
        name="experts",
    )(tile_e, n_used, src_row, hn2, w_gate, w_up, w_down)


def _combine_kernel(pos_ref, pos_next_ref, x2_ref, rw_ref, gf_ref, y_ref, out_ref, y1_ref, y2_ref, sem):
    tc = x2_ref.shape[0]
    step = pl.program_id(0) * pl.num_programs(1) + pl.program_id(1)
    n_steps = pl.num_programs(0) * pl.num_programs(1)
    slot = lax.rem(step, 2)

    def gather(p_ref, slot):
        def body(i, carry):
            for j in range(ROW_DMA_UNROLL):
                r = i * ROW_DMA_UNROLL + j
                pltpu.make_async_copy(y_ref.at[pl.ds(p_ref[0, r], 1)],
                                      y1_ref.at[slot, pl.ds(r, 1)], sem.at[slot]).start(priority=0)
                pltpu.make_async_copy(y_ref.at[pl.ds(p_ref[1, r], 1)],
                                      y2_ref.at[slot, pl.ds(r, 1)], sem.at[slot]).start(priority=1)
            return carry
        lax.fori_loop(0, tc // ROW_DMA_UNROLL, body, 0)

    @pl.when(step == 0)
    def _():
        gather(pos_ref, 0)

    @pl.when(step + 1 < n_steps)
    def _():
        gather(pos_next_ref, 1 - slot)

    w_t = jnp.transpose(rw_ref[...])
    pltpu.make_async_copy(y_ref.at[pl.ds(0, tc)], y1_ref.at[slot], sem.at[slot]).wait()
    pltpu.make_async_copy(y_ref.at[pl.ds(0, tc)], y2_ref.at[slot], sem.at[slot]).wait()
    x3 = x2_ref[...] + w_t[:, 0:1] * y1_ref[slot] + w_t[:, 1:2] * y2_ref[slot]
    ms = jnp.mean(x3 * x3, axis=-1, keepdims=True)
    out_ref[...] = x3 * lax.rsqrt(ms + RMS_EPS) * gf_ref[...]


def _combine(x2, pos, rw, y, norm_final):
    b_sz, s_len, d = x2.shape
    tc = TC_COMBINE
    n_s = s_len // tc

    def next_tile(b, s):
        nxt = jnp.minimum(b * n_s + s + 1, b_sz * n_s - 1)
        return (nxt // n_s, 0, nxt % n_s)

    return pl.pallas_call(
        _combine_kernel,
        grid=(b_sz, n_s),
        in_specs=[
            pl.BlockSpec((None, 2, tc), lambda b, s: (b, 0, s), memory_space=pltpu.SMEM),
            pl.BlockSpec((None, 2, tc), next_tile, memory_space=pltpu.SMEM),
            pl.BlockSpec((None, tc, d), lambda b, s: (b, s, 0)),
            pl.BlockSpec((None, 2 * N_HEADS, tc), lambda b, s: (b, 0, s)),
            pl.BlockSpec((1, d), lambda b, s: (0, 0)),
            pl.BlockSpec(memory_space=pl.ANY),
        ],
        out_specs=pl.BlockSpec((None, tc, d), lambda b, s: (b, s, 0)),
        scratch_shapes=[pltpu.VMEM((2, tc, d), F32), pltpu.VMEM((2, tc, d), F32),
                        pltpu.SemaphoreType.DMA((2,))],
        out_shape=jax.ShapeDtypeStruct((b_sz, s_len, d), F32),
        compiler_params=pltpu.CompilerParams(
            dimension_semantics=("arbitrary", "arbitrary"), vmem_limit_bytes=VMEM_LIMIT),
        name="combine",
    )(pos, pos, x2, rw, norm_final[None, :], y)


def _layer(x, norm_mix, w_in, b_in, conv_w, w_out, norm_ffn, w_group, b_group, w_expert, b_expert,
           w_gate, w_up, w_down, norm_out):
    b_sz, s_len, d = x.shape
    n_tok = b_sz * s_len
    a, kt, gt = _inproj(x, norm_mix, w_in, b_in)

    w_router = jnp.zeros((d, ROUTER_COLS), F32)
    w_router = w_router.at[:, 0:N_GROUPS].set(w_group).at[:, EXPERT_COL0:EXPERT_COL0 + N_EXPERTS].set(w_expert)
    b_router = jnp.zeros((1, ROUTER_COLS), F32)
    b_router = b_router.at[0, 0:N_GROUPS].set(b_group).at[0, EXPERT_COL0:EXPERT_COL0 + N_EXPERTS].set(b_expert)
    x2, hn2, ri, rw, counts = _mixer(x, a, kt, gt, conv_w, w_out, norm_ffn, w_router, b_router)

    tm = TM_EXPERT
    cnt = counts[:, 0]
    padded = ((cnt + tm - 1) // tm) * tm
    pends = jnp.cumsum(padded).astype(I32)
    seg = jnp.concatenate([jnp.zeros((1,), I32), pends])
    n_tiles = (n_tok * 2) // tm + N_EXPERTS
    cap = n_tiles * tm
    tile_start = jnp.arange(n_tiles, dtype=I32) * tm
    tile_e = jnp.minimum(jnp.sum((tile_start[:, None] >= pends[None, :]).astype(I32), axis=1),
                         N_EXPERTS - 1)
    n_used = pends[-1:] // tm
    e_sel = ri[:, 0:2, :]
    seg_start = jnp.sum(jnp.where(e_sel[..., None] == jnp.arange(N_EXPERTS, dtype=I32), seg[:-1], 0), axis=-1)
    pos = ri[:, 2:4, :] + seg_start

    src_row = _invert(pos, seg, cap)
    y = _experts(hn2.reshape(n_tok, d), src_row, tile_e, n_used, w_gate, w_up, w_down)
    return _combine(x2, pos, rw, y, norm_out)


def kernel(x, norm_mix, w_in, b_in, conv_w, w_out, norm_ffn, w_group, b_group, w_expert, b_expert,
           w_gate, w_up, w_down, norm_final):
    depth = norm_mix.shape[0]
    assert depth == 1, "final-norm fusion below assumes a single layer"
    assert x.shape[-1] == 2 * D_MLSTM and x.shape[1] % TS_MIXER == 0
    return _layer(x, norm_mix[0], w_in[0], b_in[0], conv_w[0], w_out[0], norm_ffn[0],
                  w_group[0], b_group[0], w_expert[0], b_expert[0],
                  w_gate[0], w_up[0], w_down[0], norm_final)
```

```python
import jax
import jax.numpy as jnp
from jax import lax
from jax.experimental import pallas as pl
from jax.experimental.pallas import tpu as pltpu

F32 = jnp.float32
BF16 = jnp.bfloat16
I32 = jnp.int32

N_HEADS = 4
HEAD_DIM = 128
D_MLSTM = N_HEADS * HEAD_DIM
D_CONV = 512
N_GROUPS = 4
EXPERTS_PER_GROUP = 8
N_EXPERTS = N_GROUPS * EXPERTS_PER_GROUP
D_FF = 512
RMS_EPS = 1e-6
Q_SCALE = HEAD_DIM ** -0.5

LANES = 128
CHUNK = LANES
ROUTER_COLS = 128
EXPERT_COL0 = 8

TM_INPROJ = 512
TS_MIXER = 512
TD_DISPATCH = 512
OUT_COLS = 256
TM_EXPERT = 512
TC_COMBINE = 512

VMEM_LIMIT = 56 * 1024 * 1024


def _nt_dot(a, b):
    return lax.dot_general(a, b, (((1,), (1,)), ((), ())), preferred_element_type=F32)


def _lane_cumsum(x, lane):
    s = 1
    while s < LANES:
        x = x + jnp.where(lane >= s, pltpu.roll(x, s, axis=1), jnp.zeros_like(x))
        s *= 2
    return x


def _inproj_kernel(x_ref, g_ref, wa_ref, ba_ref, wk_ref, bk_ref, wg_ref, bg_ref,
                   a_ref, kt_ref, gt_ref):
    x = x_ref[...]
    ms = jnp.mean(x * x, axis=-1, keepdims=True)
    hn = (x * lax.rsqrt(ms + RMS_EPS) * g_ref[...]).astype(BF16)
    n_blk = wa_ref.shape[1] // D_MLSTM
    for j in range(n_blk):
        sl = slice(j * D_MLSTM, (j + 1) * D_MLSTM)
        acc = jnp.dot(hn, wa_ref[:, sl], preferred_element_type=F32) + ba_ref[:, sl]
        if j == 0:
            acc = acc * Q_SCALE
        a_ref[:, sl] = acc.astype(BF16)
    kt_ref[...] = (_nt_dot(wk_ref[...], hn) + bk_ref[...]).astype(BF16)
    gt_ref[...] = _nt_dot(wg_ref[...], hn) + bg_ref[...]


def _inproj(x, norm_mix, w_in, b_in):
    b_sz, s_len, d = x.shape
    tm = TM_INPROJ
    dm = D_MLSTM
    g0 = 4 * dm
    c0 = g0 + 2 * N_HEADS
    cols_a = jnp.concatenate([w_in[:, 0:dm], w_in[:, 2 * dm:4 * dm], w_in[:, c0:c0 + 3 * D_CONV]], axis=1)
    bias_a = jnp.concatenate([b_in[0:dm], b_in[2 * dm:4 * dm], b_in[c0:c0 + 3 * D_CONV]])[None, :]
    wk_t = w_in[:, dm:2 * dm].T
    bk = b_in[dm:2 * dm][:, None]
    wg_t = w_in[:, g0:c0].T
    bg = b_in[g0:c0][:, None]
    n_a = cols_a.shape[1]
    return pl.pallas_call(
        _inproj_kernel,
        grid=(b_sz, s_len // tm),
        in_specs=[
            pl.BlockSpec((None, tm, d), lambda b, s: (b, s, 0)),
            pl.BlockSpec((1, d), lambda b, s: (0, 0)),
            pl.BlockSpec((d, n_a), lambda b, s: (0, 0)),
            pl.BlockSpec((1, n_a), lambda b, s: (0, 0)),
            pl.BlockSpec((dm, d), lambda b, s: (0, 0)),
            pl.BlockSpec((dm, 1), lambda b, s: (0, 0)),
            pl.BlockSpec((2 * N_HEADS, d), lambda b, s: (0, 0)),
            pl.BlockSpec((2 * N_HEADS, 1), lambda b, s: (0, 0)),
        ],
        out_specs=[
            pl.BlockSpec((None, tm, n_a), lambda b, s: (b, s, 0)),
            pl.BlockSpec((None, dm, tm), lambda b, s: (b, 0, s)),
            pl.BlockSpec((None, 2 * N_HEADS, tm), lambda b, s: (b, 0, s)),
        ],
        out_shape=[
            jax.ShapeDtypeStruct((b_sz, s_len, n_a), BF16),
            jax.ShapeDtypeStruct((b_sz, dm, s_len), BF16),
            jax.ShapeDtypeStruct((b_sz, 2 * N_HEADS, s_len), F32),
        ],
        compiler_params=pltpu.CompilerParams(
            dimension_semantics=("arbitrary", "arbitrary"), vmem_limit_bytes=VMEM_LIMIT),
        name="inproj",
    )(x, norm_mix[None, :], cols_a.astype(BF16), bias_a, wk_t.astype(BF16), bk, wg_t.astype(BF16), bg)


def _mixer_kernel(q_ref, v_ref, o_ref, cb_ref, cc_ref, cx_ref, kt_ref, gt_ref, x_ref,
                  cw_ref, wo_ref, g2_ref, wr_ref, br_ref,
                  x2_ref, hn2_ref, ri_ref, rw_ref, cnt_out_ref,
                  ct_ref, m_ref, carry_ref, cnt_ref, mix_ref):
    ts = x_ref.shape[0]
    n_chunks = ts // CHUNK
    first_tile = pl.program_id(1) == 0

    @pl.when(first_tile)
    def _():
        ct_ref[...] = jnp.zeros_like(ct_ref)
        m_ref[...] = jnp.zeros_like(m_ref)
        carry_ref[...] = jnp.zeros_like(carry_ref)

    @pl.when(jnp.logical_and(first_tile, pl.program_id(0) == 0))
    def _():
        cnt_ref[...] = jnp.zeros_like(cnt_ref)

    gates = gt_ref[...]
    li_all = gates[0:N_HEADS]
    fg = gates[N_HEADS:2 * N_HEADS]
    lf_all = jnp.minimum(fg, 0.0) - jnp.log1p(jnp.exp(-jnp.abs(fg)))
    lane4 = lax.broadcasted_iota(I32, (N_HEADS, CHUNK), 1)
    row = lax.broadcasted_iota(I32, (CHUNK, CHUNK), 0)
    col = lax.broadcasted_iota(I32, (CHUNK, CHUNK), 1)
    causal = row >= col
    ones_col = (lax.broadcasted_iota(I32, (CHUNK, HEAD_DIM), 1) == 0).astype(BF16)
    neg_inf = jnp.float32(-jnp.inf)

    m_prev = m_ref[0:N_HEADS, 0:1]
    for c in range(n_chunks):
        rows = slice(c * CHUNK, (c + 1) * CHUNK)
        li = li_all[:, rows]
        lf = lf_all[:, rows]
        b_cum = _lane_cumsum(lf, lane4)
        g_tot = b_cum[:, CHUNK - 1:CHUNK]
        a_end = g_tot - b_cum + li
        m_loc = jnp.max(a_end, axis=1, keepdims=True)
        u = li - b_cum
        m_new = jnp.maximum(g_tot + m_prev, m_loc)
        decay = jnp.exp(g_tot + m_prev - m_new)
        s_fac = jnp.exp(m_loc - m_new)
        w_key = jnp.exp(a_end - m_loc)
        for h in range(N_HEADS):
            hc = slice(h * HEAD_DIM, (h + 1) * HEAD_DIM)
            qh = q_ref[rows, hc]
            kth = kt_ref[hc, rows]
            v_aug = jnp.concatenate([v_ref[rows, hc], ones_col], axis=1)
            s_qk = jnp.dot(qh, kth, preferred_element_type=F32)
            u_m = jnp.where(causal, u[h:h + 1, :], neg_inf)
            m_row = jnp.max(u_m, axis=1, keepdims=True)
            p = (s_qk * jnp.exp(u_m - m_row)).astype(BF16)
            r_intra = jnp.dot(p, v_aug, preferred_element_type=F32)
            ct_old = ct_ref[h]
            r_inter = jnp.dot(qh, ct_old.astype(BF16), preferred_element_type=F32)
            ktw = (kth.astype(F32) * w_key[h:h + 1, :]).astype(BF16)
            kv = jnp.dot(ktw, v_aug, preferred_element_type=F32)
            ct_ref[h] = decay[h:h + 1, :] * ct_old + s_fac[h:h + 1, :] * kv

            mp = m_prev[h:h + 1, :]
            m_tot = jnp.maximum(m_row, mp)
            f_intra = jnp.exp(m_row - m_tot)
            f_inter = jnp.exp(mp - m_tot)
            b_col = jnp.sum(jnp.where(causal, lf[h:h + 1, :], 0.0), axis=1, keepdims=True)
            num = f_intra * r_intra[:, 0:HEAD_DIM] + f_inter * r_inter[:, 0:HEAD_DIM]
            den = (f_intra * r_intra[:, HEAD_DIM:HEAD_DIM + 1]
                   + f_inter * r_inter[:, HEAD_DIM:HEAD_DIM + 1])
            inv = 1.0 / jnp.maximum(jnp.abs(den), jnp.exp(-b_col - m_tot))
            o_gate = jax.nn.sigmoid(o_ref[rows, hc].astype(F32))
            mix_ref[rows, hc] = (o_gate * (num * inv)).astype(BF16)
        m_prev = m_new
    m_ref[0:N_HEADS, :] = jnp.broadcast_to(m_prev, (N_HEADS, LANES))

    u_c = cc_ref[...].astype(F32) * cx_ref[...].astype(F32)
    rid = lax.broadcasted_iota(I32, u_c.shape, 0)
    prev1 = carry_ref[0:1, :]
    prev2 = carry_ref[1:2, :]
    u1 = jnp.where(rid == 0, prev1, pltpu.roll(u_c, 1, axis=0))
    u2 = jnp.where(rid == 0, prev2, jnp.where(rid == 1, prev1, pltpu.roll(u_c, 2, axis=0)))
    z = cw_ref[0:1, :] * u2 + cw_ref[1:2, :] * u1 + cw_ref[2:3, :] * u_c
    mix_ref[:, D_MLSTM:] = (cb_ref[...].astype(F32) * z).astype(BF16)
    carry_ref[0:1, :] = u_c[ts - 1:ts, :]
    carry_ref[1:2, :] = u_c[ts - 2:ts - 1, :]

    d_model = x_ref.shape[1]
    mix = mix_ref[...]
    ssq = jnp.zeros((ts, 1), F32)
    for j in range(d_model // OUT_COLS):
        cs = slice(j * OUT_COLS, (j + 1) * OUT_COLS)
        x2c = x_ref[:, cs] + jnp.dot(mix, wo_ref[:, cs], preferred_element_type=F32)
        x2_ref[:, cs] = x2c
        ssq = ssq + jnp.sum(x2c * x2c, axis=-1, keepdims=True)
    hn2 = x2_ref[...] * lax.rsqrt(ssq * (1.0 / d_model) + RMS_EPS) * g2_ref[...]
    hn2_ref[...] = hn2

    h_hi = hn2.astype(BF16)
    h_lo = (hn2 - h_hi.astype(F32)).astype(BF16)
    hi_both = jnp.dot(h_hi, wr_ref[...], preferred_element_type=F32)
    lo_hi = jnp.dot(h_lo, wr_ref[:, 0:ROUTER_COLS], preferred_element_type=F32)
    logits = hi_both[:, 0:ROUTER_COLS] + (hi_both[:, ROUTER_COLS:] + lo_hi) + br_ref[...]
    lt = jnp.transpose(logits)
    gl = lt[0:N_GROUPS]
    gmax = jnp.max(gl, axis=0, keepdims=True)
    gi = lax.broadcasted_iota(I32, gl.shape, 0)
    g_sel = jnp.min(jnp.where(gl == gmax, gi, N_GROUPS), axis=0, keepdims=True)
    p_g = 1.0 / jnp.sum(jnp.exp(gl - gmax), axis=0, keepdims=True)
    in_g = lt[EXPERT_COL0:EXPERT_COL0 + EXPERTS_PER_GROUP]
    for g in range(1, N_GROUPS):
        lo = EXPERT_COL0 + g * EXPERTS_PER_GROUP
        in_g = jnp.where(g_sel == g, lt[lo:lo + EXPERTS_PER_GROUP], in_g)
    ei = lax.broadcasted_iota(I32, in_g.shape, 0)
    v1 = jnp.max(in_g, axis=0, keepdims=True)
    i1 = jnp.min(jnp.where(in_g == v1, ei, EXPERTS_PER_GROUP), axis=0, keepdims=True)
    rest = jnp.where(ei == i1, neg_inf, in_g)
    v2 = jnp.max(rest, axis=0, keepdims=True)
    i2 = jnp.min(jnp.where(rest == v2, ei, EXPERTS_PER_GROUP), axis=0, keepdims=True)
    t = jnp.exp(v2 - v1)
    w1 = p_g / (1.0 + t)
    w2 = p_g * t / (1.0 + t)
    e1 = g_sel * EXPERTS_PER_GROUP + i1
    e2 = g_sel * EXPERTS_PER_GROUP + i2

    eio = lax.broadcasted_iota(I32, (N_EXPERTS, LANES), 0)
    lane_e = lax.broadcasted_iota(I32, (N_EXPERTS, LANES), 1)
    base = cnt_ref[:, 0:1]
    zero_i = jnp.zeros((1, LANES), I32)
    for k in range(ts // LANES):
        ls = slice(k * LANES, (k + 1) * LANES)
        is1 = eio == e1[:, ls]
        is2 = eio == e2[:, ls]
        onehot = jnp.logical_or(is1, is2).astype(I32)
        incl = _lane_cumsum(onehot, lane_e)
        rank_e = base + incl - onehot
        r1 = jnp.sum(jnp.where(is1, rank_e, 0), axis=0, keepdims=True)
        r2 = jnp.sum(jnp.where(is2, rank_e, 0), axis=0, keepdims=True)
        ri_ref[:, ls] = jnp.concatenate(
            [e1[:, ls], e2[:, ls], r1, r2, zero_i, zero_i, zero_i, zero_i], axis=0)
        base = base + incl[:, LANES - 1:LANES]
    zero_f = jnp.zeros_like(w1)
    rw_ref[...] = jnp.concatenate([w1, w2, zero_f, zero_f, zero_f, zero_f, zero_f, zero_f], axis=0)
    cnt_new = jnp.broadcast_to(base, (N_EXPERTS, LANES))
    cnt_ref[...] = cnt_new
    cnt_out_ref[...] = cnt_new


def _mixer(x, a, kt, gt, conv_w, w_out, norm_ffn, w_router, b_router):
    b_sz, s_len, d = x.shape
    ts = TS_MIXER
    dm = D_MLSTM

    def a_spec(j):
        return pl.BlockSpec((None, ts, dm), lambda b, s, j=j: (b, s, j))

    tok = lambda w: pl.BlockSpec((None, ts, w), lambda b, s: (b, s, 0))
    rowsp = pl.BlockSpec((None, 2 * N_HEADS, ts), lambda b, s: (b, 0, s))
    const = lambda shape: pl.BlockSpec(shape, lambda b, s: tuple(0 for _ in shape))
    wr_hi = w_router.astype(BF16)
    wr_lo = (w_router - wr_hi.astype(F32)).astype(BF16)
    w_router2 = jnp.concatenate([wr_hi, wr_lo], axis=1)
    return pl.pallas_call(
        _mixer_kernel,
        grid=(b_sz, s_len // ts),
        in_specs=[a_spec(0), a_spec(1), a_spec(2), a_spec(3), a_spec(4), a_spec(5),
                  pl.BlockSpec((None, dm, ts), lambda b, s: (b, 0, s)),
                  rowsp, tok(d),
                  const((3, D_CONV)), const((d, d)), const((1, d)),
                  const((d, 2 * ROUTER_COLS)), const((1, ROUTER_COLS))],
        out_specs=[tok(d), tok(d), rowsp, rowsp, const((N_EXPERTS, LANES))],
        out_shape=[
            jax.ShapeDtypeStruct((b_sz, s_len, d), F32),
            jax.ShapeDtypeStruct((b_sz, s_len, d), F32),
            jax.ShapeDtypeStruct((b_sz, 2 * N_HEADS, s_len), I32),
            jax.ShapeDtypeStruct((b_sz, 2 * N_HEADS, s_len), F32),
            jax.ShapeDtypeStruct((N_EXPERTS, LANES), I32),
        ],
        scratch_shapes=[
            pltpu.VMEM((N_HEADS, HEAD_DIM, 2 * HEAD_DIM), F32),
            pltpu.VMEM((2 * N_HEADS, LANES), F32),
            pltpu.VMEM((8, D_CONV), F32),
            pltpu.VMEM((N_EXPERTS, LANES), I32),
            pltpu.VMEM((ts, d), BF16),
        ],
        compiler_params=pltpu.CompilerParams(
            dimension_semantics=("arbitrary", "arbitrary"), vmem_limit_bytes=VMEM_LIMIT),
        name="mixer",
    )(a, a, a, a, a, a, kt, gt, x, conv_w, w_out.astype(BF16), norm_ffn[None, :], w_router2, b_router)


def _tile_major(pos, t):
    b_sz, _, s_len = pos.shape
    return pos.reshape(b_sz, 2, s_len // t, t).transpose(0, 2, 1, 3).reshape(-1)


def _dispatch_kernel(seg_ref, pos_ref, h_ref, xs_ref, zero_ref, sem, zsem):
    td = h_ref.shape[0]
    tm = zero_ref.shape[0]

    @pl.when(pl.program_id(0) == 0)
    def _():
        zero_ref[...] = jnp.zeros_like(zero_ref)

        def zero_copy(e):
            last_tile = pl.multiple_of(seg_ref[e + 1] - tm, tm)
            return pltpu.make_async_copy(zero_ref, xs_ref.at[pl.ds(last_tile, tm)], zsem)

        for e in range(N_EXPERTS):
            @pl.when(seg_ref[e + 1] > seg_ref[e])
            def _(e=e):
                zero_copy(e).start()
        for e in range(N_EXPERTS):
            @pl.when(seg_ref[e + 1] > seg_ref[e])
            def _(e=e):
                zero_copy(e).wait()

        def tail_copy(t):
            return pltpu.make_async_copy(zero_ref, xs_ref.at[pl.ds(pl.multiple_of(t * tm, tm), tm)], zsem)

        first_unused = seg_ref[N_EXPERTS] // tm
        n_tiles = xs_ref.shape[0] // tm
        lax.fori_loop(first_unused, n_tiles, lambda t, c: (tail_copy(t).start(), c)[1], 0)
        lax.fori_loop(first_unused, n_tiles, lambda t, c: (tail_copy(t).wait(), c)[1], 0)

    for r in range(td):
        src = h_ref.at[pl.ds(r, 1)]
        pltpu.make_async_copy(src, xs_ref.at[pl.ds(pos_ref[r], 1)], sem).start()
        pltpu.make_async_copy(src, xs_ref.at[pl.ds(pos_ref[td + r], 1)], sem).start()
    for _ in range(2):
        pltpu.make_async_copy(h_ref, xs_ref.at[pl.ds(0, td)], sem).wait()


def _dispatch(hn2, pos, seg, cap):
    b_sz, s_len, d = hn2.shape
    td = TD_DISPATCH
    return pl.pallas_call(
        _dispatch_kernel,
        grid_spec=pltpu.PrefetchScalarGridSpec(
            num_scalar_prefetch=1,
            grid=(b_sz * s_len // td,),
            in_specs=[
                pl.BlockSpec((2 * td,), lambda g, sg: (g,), memory_space=pltpu.SMEM),
                pl.BlockSpec((td, d), lambda g, sg: (g, 0)),
            ],
            out_specs=pl.BlockSpec(memory_space=pl.ANY),
            scratch_shapes=[pltpu.VMEM((TM_EXPERT, d), hn2.dtype),
                            pltpu.SemaphoreType.DMA(()), pltpu.SemaphoreType.DMA(())],
        ),
        out_shape=jax.ShapeDtypeStruct((cap, d), hn2.dtype),
        compiler_params=pltpu.CompilerParams(
            dimension_semantics=("arbitrary",), vmem_limit_bytes=VMEM_LIMIT),
        name="dispatch",
    )(seg, _tile_major(pos, td), hn2.reshape(b_sz * s_len, d))


def _experts_kernel(te_ref, nu_ref, xs_ref, wg_ref, wu_ref, wd_ref, y_ref, wgb_ref, wub_ref, wdb_ref):
    i = pl.program_id(0)
    e = te_ref[i]
    e_prev = te_ref[jnp.maximum(i - 1, 0)]
    new_expert = jnp.logical_or(i == 0, e != e_prev)
    used = i < nu_ref[0]

    @pl.when(jnp.logical_and(used, new_expert))
    def _():
        wgb_ref[...] = wg_ref[...].astype(BF16)
        wub_ref[...] = wu_ref[...].astype(BF16)
        wdb_ref[...] = wd_ref[...].astype(BF16)

    @pl.when(used)
    def _():
        xb = xs_ref[...].astype(BF16)
        gate = jnp.dot(xb, wgb_ref[...], preferred_element_type=F32)
        up = jnp.dot(xb, wub_ref[...], preferred_element_type=F32)
        hid = (gate * jax.nn.sigmoid(gate) * up).astype(BF16)
        y_ref[...] = jnp.dot(hid, wdb_ref[...], preferred_element_type=F32)

    @pl.when(jnp.logical_not(used))
    def _():
        y_ref[...] = jnp.zeros_like(y_ref)


def _experts(xs, tile_e, n_used, w_gate, w_up, w_down):
    cap, d = xs.shape
    tm = TM_EXPERT
    n_tiles = cap // tm

    def x_map(i, te, nu):
        return (jnp.minimum(i, jnp.maximum(nu[0] - 1, 0)), 0)

    def w_map(i, te, nu):
        return (te[i], 0, 0)

    return pl.pallas_call(
        _experts_kernel,
        grid_spec=pltpu.PrefetchScalarGridSpec(
            num_scalar_prefetch=2,
            grid=(n_tiles,),
            in_specs=[
                pl.BlockSpec((tm, d), x_map),
                pl.BlockSpec((None, d, D_FF), w_map),
                pl.BlockSpec((None, d, D_FF), w_map),
                pl.BlockSpec((None, D_FF, d), w_map),
            ],
            out_specs=pl.BlockSpec((tm, d), lambda i, te, nu: (i, 0)),
            scratch_shapes=[pltpu.VMEM((d, D_FF), BF16), pltpu.VMEM((d, D_FF), BF16),
                            pltpu.VMEM((D_FF, d), BF16)],
        ),
        out_shape=jax.ShapeDtypeStruct((cap, d), F32),
        compiler_params=pltpu.CompilerParams(
            dimension_semantics=("arbitrary",), vmem_limit_bytes=VMEM_LIMIT),
        name="experts",
    )(tile_e, n_used, xs, w_gate, w_up, w_down)


def _combine_kernel(pos_ref, pos_next_ref, x2_ref, rw_ref, gf_ref, y_ref, out_ref, y1_ref, y2_ref, sem):
    tc = x2_ref.shape[0]
    step = pl.program_id(0)
    n_steps = pl.num_programs(0)
    slot = lax.rem(step, 2)

    def gather(p_ref, slot):
        for r in range(tc):
            pltpu.make_async_copy(y_ref.at[pl.ds(p_ref[r], 1)],
                                  y1_ref.at[slot, pl.ds(r, 1)], sem.at[slot]).start()
            pltpu.make_async_copy(y_ref.at[pl.ds(p_ref[tc + r], 1)],
                                  y2_ref.at[slot, pl.ds(r, 1)], sem.at[slot]).start()

    @pl.when(step == 0)
    def _():
        gather(pos_ref, 0)

    @pl.when(step + 1 < n_steps)
    def _():
        gather(pos_next_ref, 1 - slot)

    w_t = jnp.transpose(rw_ref[...])
    pltpu.make_async_copy(y_ref.at[pl.ds(0, tc)], y1_ref.at[slot], sem.at[slot]).wait()
    pltpu.make_async_copy(y_ref.at[pl.ds(0, tc)], y2_ref.at[slot], sem.at[slot]).wait()
    x3 = x2_ref[...] + w_t[:, 0:1] * y1_ref[slot] + w_t[:, 1:2] * y2_ref[slot]
    ms = jnp.mean(x3 * x3, axis=-1, keepdims=True)
    out_ref[...] = x3 * lax.rsqrt(ms + RMS_EPS) * gf_ref[...]


def _combine(x2, pos, rw, y, norm_final):
    b_sz, s_len, d = x2.shape
    tc = TC_COMBINE
    n_s = s_len // tc
    n_steps = b_sz * n_s
    pos_tiles = _tile_major(pos, tc)
    return pl.pallas_call(
        _combine_kernel,
        grid=(n_steps,),
        in_specs=[
            pl.BlockSpec((2 * tc,), lambda g: (g,), memory_space=pltpu.SMEM),
            pl.BlockSpec((2 * tc,), lambda g: (jnp.minimum(g + 1, n_steps - 1),), memory_space=pltpu.SMEM),
            pl.BlockSpec((tc, d), lambda g: (g, 0)),
            pl.BlockSpec((None, 2 * N_HEADS, tc), lambda g: (g // n_s, 0, g % n_s)),
            pl.BlockSpec((1, d), lambda g: (0, 0)),
            pl.BlockSpec(memory_space=pl.ANY),
        ],
        out_specs=pl.BlockSpec((tc, d), lambda g: (g, 0)),
        scratch_shapes=[pltpu.VMEM((2, tc, d), F32), pltpu.VMEM((2, tc, d), F32),
                        pltpu.SemaphoreType.DMA((2,))],
        out_shape=jax.ShapeDtypeStruct((b_sz * s_len, d), F32),
        compiler_params=pltpu.CompilerParams(
            dimension_semantics=("arbitrary",), vmem_limit_bytes=VMEM_LIMIT),
        name="combine",
    )(pos_tiles, pos_tiles, x2.reshape(b_sz * s_len, d), rw, norm_final[None, :], y).reshape(b_sz, s_len, d)


def _layer(x, norm_mix, w_in, b_in, conv_w, w_out, norm_ffn, w_group, b_group, w_expert, b_expert,
           w_gate, w_up, w_down, norm_out):
    b_sz, s_len, d = x.shape
    n_tok = b_sz * s_len
    a, kt, gt = _inproj(x, norm_mix, w_in, b_in)

    w_router = jnp.zeros((d, ROUTER_COLS), F32)
    w_router = w_router.at[:, 0:N_GROUPS].set(w_group).at[:, EXPERT_COL0:EXPERT_COL0 + N_EXPERTS].set(w_expert)
    b_router = jnp.zeros((1, ROUTER_COLS), F32)
    b_router = b_router.at[0, 0:N_GROUPS].set(b_group).at[0, EXPERT_COL0:EXPERT_COL0 + N_EXPERTS].set(b_expert)
    x2, hn2, ri, rw, counts = _mixer(x, a, kt, gt, conv_w, w_out, norm_ffn, w_router, b_router)

    tm = TM_EXPERT
    cnt = counts[:, 0]
    padded = ((cnt + tm - 1) // tm) * tm
    pends = jnp.cumsum(padded).astype(I32)
    seg = jnp.concatenate([jnp.zeros((1,), I32), pends])
    n_tiles = (n_tok * 2) // tm + N_EXPERTS
    cap = n_tiles * tm
    tile_start = jnp.arange(n_tiles, dtype=I32) * tm
    tile_e = jnp.minimum(jnp.sum((tile_start[:, None] >= pends[None, :]).astype(I32), axis=1),
                         N_EXPERTS - 1)
    n_used = pends[-1:] // tm
    e_sel = ri[:, 0:2, :]
    seg_start = jnp.sum(jnp.where(e_sel[..., None] == jnp.arange(N_EXPERTS, dtype=I32), seg[:-1], 0), axis=-1)
    pos = ri[:, 2:4, :] + seg_start

    xs = _dispatch(hn2, pos, seg, cap)
    y = _experts(xs, tile_e, n_used, w_gate, w_up, w_down)
    return _combine(x2, pos, rw, y, norm_out)


def kernel(x, norm_mix, w_in, b_in, conv_w, w_out, norm_ffn, w_group, b_group, w_expert, b_expert,
           w_gate, w_up, w_down, norm_final):
    depth = norm_mix.shape[0]
    assert depth == 1, "final-norm fusion below assumes a single layer"
    assert x.shape[-1] == 2 * D_MLSTM and x.shape[1] % TS_MIXER == 0
    return _layer(x, norm_mix[0], w_in[0], b_in[0], conv_w[0], w_out[0], norm_ffn[0],
                  w_group[0], b_group[0], w_expert[0], b_expert[0],
                  w_gate[0], w_up[0], w_down[0], norm_final)
```

```python
import jax
import jax.numpy as jnp
from jax import lax
from jax.experimental import pallas as pl
from jax.experimental.pallas import tpu as pltpu

F32 = jnp.float32
BF16 = jnp.bfloat16
I32 = jnp.int32

N_HEADS = 4
HEAD_DIM = 128
D_MLSTM = N_HEADS * HEAD_DIM
D_CONV = 512
N_GROUPS = 4
EXPERTS_PER_GROUP = 8
N_EXPERTS = N_GROUPS * EXPERTS_PER_GROUP
D_FF = 512
RMS_EPS = 1e-6
Q_SCALE = HEAD_DIM ** -0.5

LANES = 128
CHUNK = LANES
ROUTER_COLS = 128
EXPERT_COL0 = 8

TM_INPROJ = 512
TS_MIXER = 512
TD_DISPATCH = 512
OUT_COLS = 256
TM_EXPERT = 512
TC_COMBINE = 512

VMEM_LIMIT = 56 * 1024 * 1024


def _nt_dot(a, b):
    return lax.dot_general(a, b, (((1,), (1,)), ((), ())), preferred_element_type=F32)


def _lane_cumsum(x, lane):
    s = 1
    while s < LANES:
        x = x + jnp.where(lane >= s, pltpu.roll(x, s, axis=1), jnp.zeros_like(x))
        s *= 2
    return x


def _inproj_kernel(x_ref, g_ref, wa_ref, ba_ref, wk_ref, bk_ref, wg_ref, bg_ref,
                   a_ref, kt_ref, gt_ref):
    x = x_ref[...]
    ms = jnp.mean(x * x, axis=-1, keepdims=True)
    hn = (x * lax.rsqrt(ms + RMS_EPS) * g_ref[...]).astype(BF16)
    n_blk = wa_ref.shape[1] // D_MLSTM
    for j in range(n_blk):
        sl = slice(j * D_MLSTM, (j + 1) * D_MLSTM)
        acc = jnp.dot(hn, wa_ref[:, sl], preferred_element_type=F32) + ba_ref[:, sl]
        if j == 0:
            acc = acc * Q_SCALE
        a_ref[:, sl] = acc.astype(BF16)
    kt_ref[...] = (_nt_dot(wk_ref[...], hn) + bk_ref[...]).astype(BF16)
    gt_ref[...] = _nt_dot(wg_ref[...], hn) + bg_ref[...]


def _inproj(x, norm_mix, w_in, b_in):
    b_sz, s_len, d = x.shape
    tm = TM_INPROJ
    dm = D_MLSTM
    g0 = 4 * dm
    c0 = g0 + 2 * N_HEADS
    cols_a = jnp.concatenate([w_in[:, 0:dm], w_in[:, 2 * dm:4 * dm], w_in[:, c0:c0 + 3 * D_CONV]], axis=1)
    bias_a = jnp.concatenate([b_in[0:dm], b_in[2 * dm:4 * dm], b_in[c0:c0 + 3 * D_CONV]])[None, :]
    wk_t = w_in[:, dm:2 * dm].T
    bk = b_in[dm:2 * dm][:, None]
    wg_t = w_in[:, g0:c0].T
    bg = b_in[g0:c0][:, None]
    n_a = cols_a.shape[1]
    return pl.pallas_call(
        _inproj_kernel,
        grid=(b_sz, s_len // tm),
        in_specs=[
            pl.BlockSpec((None, tm, d), lambda b, s: (b, s, 0)),
            pl.BlockSpec((1, d), lambda b, s: (0, 0)),
            pl.BlockSpec((d, n_a), lambda b, s: (0, 0)),
            pl.BlockSpec((1, n_a), lambda b, s: (0, 0)),
            pl.BlockSpec((dm, d), lambda b, s: (0, 0)),
            pl.BlockSpec((dm, 1), lambda b, s: (0, 0)),
            pl.BlockSpec((2 * N_HEADS, d), lambda b, s: (0, 0)),
            pl.BlockSpec((2 * N_HEADS, 1), lambda b, s: (0, 0)),
        ],
        out_specs=[
            pl.BlockSpec((None, tm, n_a), lambda b, s: (b, s, 0)),
            pl.BlockSpec((None, dm, tm), lambda b, s: (b, 0, s)),
            pl.BlockSpec((None, 2 * N_HEADS, tm), lambda b, s: (b, 0, s)),
        ],
        out_shape=[
            jax.ShapeDtypeStruct((b_sz, s_len, n_a), BF16),
            jax.ShapeDtypeStruct((b_sz, dm, s_len), BF16),
            jax.ShapeDtypeStruct((b_sz, 2 * N_HEADS, s_len), F32),
        ],
        compiler_params=pltpu.CompilerParams(
            dimension_semantics=("arbitrary", "arbitrary"), vmem_limit_bytes=VMEM_LIMIT),
        name="inproj",
    )(x, norm_mix[None, :], cols_a.astype(BF16), bias_a, wk_t.astype(BF16), bk, wg_t.astype(BF16), bg)


def _mixer_kernel(q_ref, v_ref, o_ref, cb_ref, cc_ref, cx_ref, kt_ref, gt_ref, x_ref,
                  cw_ref, wo_ref, g2_ref, wr_ref, br_ref,
                  x2_ref, hn2_ref, ri_ref, rw_ref, cnt_out_ref,
                  ct_ref, m_ref, carry_ref, cnt_ref, mix_ref):
    ts = x_ref.shape[0]
    n_chunks = ts // CHUNK
    first_tile = pl.program_id(1) == 0

    @pl.when(first_tile)
    def _():
        ct_ref[...] = jnp.zeros_like(ct_ref)
        m_ref[...] = jnp.zeros_like(m_ref)
        carry_ref[...] = jnp.zeros_like(carry_ref)

    @pl.when(jnp.logical_and(first_tile, pl.program_id(0) == 0))
    def _():
        cnt_ref[...] = jnp.zeros_like(cnt_ref)

    gates = gt_ref[...]
    li_all = gates[0:N_HEADS]
    fg = gates[N_HEADS:2 * N_HEADS]
    lf_all = jnp.minimum(fg, 0.0) - jnp.log1p(jnp.exp(-jnp.abs(fg)))
    lane4 = lax.broadcasted_iota(I32, (N_HEADS, CHUNK), 1)
    row = lax.broadcasted_iota(I32, (CHUNK, CHUNK), 0)
    col = lax.broadcasted_iota(I32, (CHUNK, CHUNK), 1)
    causal = row >= col
    ones_col = (lax.broadcasted_iota(I32, (CHUNK, HEAD_DIM), 1) == 0).astype(BF16)
    neg_inf = jnp.float32(-jnp.inf)

    m_prev = m_ref[0:N_HEADS, 0:1]
    for c in range(n_chunks):
        rows = slice(c * CHUNK, (c + 1) * CHUNK)
        li = li_all[:, rows]
        lf = lf_all[:, rows]
        b_cum = _lane_cumsum(lf, lane4)
        g_tot = b_cum[:, CHUNK - 1:CHUNK]
        a_end = g_tot - b_cum + li
        m_loc = jnp.max(a_end, axis=1, keepdims=True)
        u = li - b_cum
        m_new = jnp.maximum(g_tot + m_prev, m_loc)
        decay = jnp.exp(g_tot + m_prev - m_new)
        s_fac = jnp.exp(m_loc - m_new)
        w_key = jnp.exp(a_end - m_loc)
        for h in range(N_HEADS):
            hc = slice(h * HEAD_DIM, (h + 1) * HEAD_DIM)
            qh = q_ref[rows, hc]
            kth = kt_ref[hc, rows]
            v_aug = jnp.concatenate([v_ref[rows, hc], ones_col], axis=1)
            s_qk = jnp.dot(qh, kth, preferred_element_type=F32)
            u_m = jnp.where(causal, u[h:h + 1, :], neg_inf)
            m_row = jnp.max(u_m, axis=1, keepdims=True)
            p = (s_qk * jnp.exp(u_m - m_row)).astype(BF16)
            r_intra = jnp.dot(p, v_aug, preferred_element_type=F32)
            ct_old = ct_ref[h]
            r_inter = jnp.dot(qh, ct_old.astype(BF16), preferred_element_type=F32)
            ktw = (kth.astype(F32) * w_key[h:h + 1, :]).astype(BF16)
            kv = jnp.dot(ktw, v_aug, preferred_element_type=F32)
            ct_ref[h] = decay[h:h + 1, :] * ct_old + s_fac[h:h + 1, :] * kv

            mp = m_prev[h:h + 1, :]
            m_tot = jnp.maximum(m_row, mp)
            f_intra = jnp.exp(m_row - m_tot)
            f_inter = jnp.exp(mp - m_tot)
            b_col = jnp.sum(jnp.where(causal, lf[h:h + 1, :], 0.0), axis=1, keepdims=True)
            num = f_intra * r_intra[:, 0:HEAD_DIM] + f_inter * r_inter[:, 0:HEAD_DIM]
            den = (f_intra * r_intra[:, HEAD_DIM:HEAD_DIM + 1]
                   + f_inter * r_inter[:, HEAD_DIM:HEAD_DIM + 1])
            inv = 1.0 / jnp.maximum(jnp.abs(den), jnp.exp(-b_col - m_tot))
            o_gate = jax.nn.sigmoid(o_ref[rows, hc].astype(F32))
            mix_ref[rows, hc] = (o_gate * (num * inv)).astype(BF16)
        m_prev = m_new
    m_ref[0:N_HEADS, :] = jnp.broadcast_to(m_prev, (N_HEADS, LANES))

    u_c = cc_ref[...].astype(F32) * cx_ref[...].astype(F32)
    rid = lax.broadcasted_iota(I32, u_c.shape, 0)
    prev1 = carry_ref[0:1, :]
    prev2 = carry_ref[1:2, :]
    u1 = jnp.where(rid == 0, prev1, pltpu.roll(u_c, 1, axis=0))
    u2 = jnp.where(rid == 0, prev2, jnp.where(rid == 1, prev1, pltpu.roll(u_c, 2, axis=0)))
    z = cw_ref[0:1, :] * u2 + cw_ref[1:2, :] * u1 + cw_ref[2:3, :] * u_c
    mix_ref[:, D_MLSTM:] = (cb_ref[...].astype(F32) * z).astype(BF16)
    carry_ref[0:1, :] = u_c[ts - 1:ts, :]
    carry_ref[1:2, :] = u_c[ts - 2:ts - 1, :]

    d_model = x_ref.shape[1]
    mix = mix_ref[...]
    ssq = jnp.zeros((ts, 1), F32)
    for j in range(d_model // OUT_COLS):
        cs = slice(j * OUT_COLS, (j + 1) * OUT_COLS)
        x2c = x_ref[:, cs] + jnp.dot(mix, wo_ref[:, cs], preferred_element_type=F32)
        x2_ref[:, cs] = x2c
        ssq = ssq + jnp.sum(x2c * x2c, axis=-1, keepdims=True)
    hn2 = x2_ref[...] * lax.rsqrt(ssq * (1.0 / d_model) + RMS_EPS) * g2_ref[...]
    hn2_ref[...] = hn2

    h_hi = hn2.astype(BF16)
    h_lo = (hn2 - h_hi.astype(F32)).astype(BF16)
    hi_both = jnp.dot(h_hi, wr_ref[...], preferred_element_type=F32)
    lo_hi = jnp.dot(h_lo, wr_ref[:, 0:ROUTER_COLS], preferred_element_type=F32)
    logits = hi_both[:, 0:ROUTER_COLS] + (hi_both[:, ROUTER_COLS:] + lo_hi) + br_ref[...]
    lt = jnp.transpose(logits)
    gl = lt[0:N_GROUPS]
    gmax = jnp.max(gl, axis=0, keepdims=True)
    gi = lax.broadcasted_iota(I32, gl.shape, 0)
    g_sel = jnp.min(jnp.where(gl == gmax, gi, N_GROUPS), axis=0, keepdims=True)
    p_g = 1.0 / jnp.sum(jnp.exp(gl - gmax), axis=0, keepdims=True)
    in_g = lt[EXPERT_COL0:EXPERT_COL0 + EXPERTS_PER_GROUP]
    for g in range(1, N_GROUPS):
        lo = EXPERT_COL0 + g * EXPERTS_PER_GROUP
        in_g = jnp.where(g_sel == g, lt[lo:lo + EXPERTS_PER_GROUP], in_g)
    ei = lax.broadcasted_iota(I32, in_g.shape, 0)
    v1 = jnp.max(in_g, axis=0, keepdims=True)
    i1 = jnp.min(jnp.where(in_g == v1, ei, EXPERTS_PER_GROUP), axis=0, keepdims=True)
    rest = jnp.where(ei == i1, neg_inf, in_g)
    v2 = jnp.max(rest, axis=0, keepdims=True)
    i2 = jnp.min(jnp.where(rest == v2, ei, EXPERTS_PER_GROUP), axis=0, keepdims=True)
    t = jnp.exp(v2 - v1)
    w1 = p_g / (1.0 + t)
    w2 = p_g * t / (1.0 + t)
    e1 = g_sel * EXPERTS_PER_GROUP + i1
    e2 = g_sel * EXPERTS_PER_GROUP + i2

    eio = lax.broadcasted_iota(I32, (N_EXPERTS, LANES), 0)
    lane_e = lax.broadcasted_iota(I32, (N_EXPERTS, LANES), 1)
    base = cnt_ref[:, 0:1]
    zero_i = jnp.zeros((1, LANES), I32)
    for k in range(ts // LANES):
        ls = slice(k * LANES, (k + 1) * LANES)
        is1 = eio == e1[:, ls]
        is2 = eio == e2[:, ls]
        onehot = jnp.logical_or(is1, is2).astype(I32)
        incl = _lane_cumsum(onehot, lane_e)
        rank_e = base + incl - onehot
        r1 = jnp.sum(jnp.where(is1, rank_e, 0), axis=0, keepdims=True)
        r2 = jnp.sum(jnp.where(is2, rank_e, 0), axis=0, keepdims=True)
        ri_ref[:, ls] = jnp.concatenate(
            [e1[:, ls], e2[:, ls], r1, r2, zero_i, zero_i, zero_i, zero_i], axis=0)
        base = base + incl[:, LANES - 1:LANES]
    zero_f = jnp.zeros_like(w1)
    rw_ref[...] = jnp.concatenate([w1, w2, zero_f, zero_f, zero_f, zero_f, zero_f, zero_f], axis=0)
    cnt_new = jnp.broadcast_to(base, (N_EXPERTS, LANES))
    cnt_ref[...] = cnt_new
    cnt_out_ref[...] = cnt_new


def _mixer(x, a, kt, gt, conv_w, w_out, norm_ffn, w_router, b_router):
    b_sz, s_len, d = x.shape
    ts = TS_MIXER
    dm = D_MLSTM

    def a_spec(j):
        return pl.BlockSpec((None, ts, dm), lambda b, s, j=j: (b, s, j))

    tok = lambda w: pl.BlockSpec((None, ts, w), lambda b, s: (b, s, 0))
    rowsp = pl.BlockSpec((None, 2 * N_HEADS, ts), lambda b, s: (b, 0, s))
    const = lambda shape: pl.BlockSpec(shape, lambda b, s: tuple(0 for _ in shape))
    wr_hi = w_router.astype(BF16)
    wr_lo = (w_router - wr_hi.astype(F32)).astype(BF16)
    w_router2 = jnp.concatenate([wr_hi, wr_lo], axis=1)
    return pl.pallas_call(
        _mixer_kernel,
        grid=(b_sz, s_len // ts),
        in_specs=[a_spec(0), a_spec(1), a_spec(2), a_spec(3), a_spec(4), a_spec(5),
                  pl.BlockSpec((None, dm, ts), lambda b, s: (b, 0, s)),
                  rowsp, tok(d),
                  const((3, D_CONV)), const((d, d)), const((1, d)),
                  const((d, 2 * ROUTER_COLS)), const((1, ROUTER_COLS))],
        out_specs=[tok(d), tok(d), rowsp, rowsp, const((N_EXPERTS, LANES))],
        out_shape=[
            jax.ShapeDtypeStruct((b_sz, s_len, d), F32),
            jax.ShapeDtypeStruct((b_sz, s_len, d), F32),
            jax.ShapeDtypeStruct((b_sz, 2 * N_HEADS, s_len), I32),
            jax.ShapeDtypeStruct((b_sz, 2 * N_HEADS, s_len), F32),
            jax.ShapeDtypeStruct((N_EXPERTS, LANES), I32),
        ],
        scratch_shapes=[
            pltpu.VMEM((N_HEADS, HEAD_DIM, 2 * HEAD_DIM), F32),
            pltpu.VMEM((2 * N_HEADS, LANES), F32),
            pltpu.VMEM((8, D_CONV), F32),
            pltpu.VMEM((N_EXPERTS, LANES), I32),
            pltpu.VMEM((ts, d), BF16),
        ],
        compiler_params=pltpu.CompilerParams(
            dimension_semantics=("arbitrary", "arbitrary"), vmem_limit_bytes=VMEM_LIMIT),
        name="mixer",
    )(a, a, a, a, a, a, kt, gt, x, conv_w, w_out.astype(BF16), norm_ffn[None, :], w_router2, b_router)


def _tile_major(pos, t):
    b_sz, _, s_len = pos.shape
    return pos.reshape(b_sz, 2, s_len // t, t).transpose(0, 2, 1, 3).reshape(-1)


def _dispatch_kernel(seg_ref, pos_ref, h_ref, xs_ref, zero_ref, sem, zsem):
    td = h_ref.shape[0]
    tm = zero_ref.shape[0]

    @pl.when(pl.program_id(0) == 0)
    def _():
        zero_ref[...] = jnp.zeros_like(zero_ref)

        def zero_copy(e):
            last_tile = pl.multiple_of(seg_ref[e + 1] - tm, tm)
            return pltpu.make_async_copy(zero_ref, xs_ref.at[pl.ds(last_tile, tm)], zsem)

        for e in range(N_EXPERTS):
            @pl.when(seg_ref[e + 1] > seg_ref[e])
            def _(e=e):
                zero_copy(e).start()
        for e in range(N_EXPERTS):
            @pl.when(seg_ref[e + 1] > seg_ref[e])
            def _(e=e):
                zero_copy(e).wait()

        def tail_copy(t):
            return pltpu.make_async_copy(zero_ref, xs_ref.at[pl.ds(pl.multiple_of(t * tm, tm), tm)], zsem)

        first_unused = seg_ref[N_EXPERTS] // tm
        n_tiles = xs_ref.shape[0] // tm
        lax.fori_loop(first_unused, n_tiles, lambda t, c: (tail_copy(t).start(), c)[1], 0)
        lax.fori_loop(first_unused, n_tiles, lambda t, c: (tail_copy(t).wait(), c)[1], 0)

    for r in range(td):
        src = h_ref.at[pl.ds(r, 1)]
        pltpu.make_async_copy(src, xs_ref.at[pl.ds(pos_ref[r], 1)], sem).start()
        pltpu.make_async_copy(src, xs_ref.at[pl.ds(pos_ref[td + r], 1)], sem).start()
    for _ in range(2):
        pltpu.make_async_copy(h_ref, xs_ref.at[pl.ds(0, td)], sem).wait()


def _dispatch(hn2, pos, seg, cap):
    b_sz, s_len, d = hn2.shape
    td = TD_DISPATCH
    return pl.pallas_call(
        _dispatch_kernel,
        grid_spec=pltpu.PrefetchScalarGridSpec(
            num_scalar_prefetch=1,
            grid=(b_sz * s_len // td,),
            in_specs=[
                pl.BlockSpec((2 * td,), lambda g, sg: (g,), memory_space=pltpu.SMEM),
                pl.BlockSpec((td, d), lambda g, sg: (g, 0)),
            ],
            out_specs=pl.BlockSpec(memory_space=pl.ANY),
            scratch_shapes=[pltpu.VMEM((TM_EXPERT, d), hn2.dtype),
                            pltpu.SemaphoreType.DMA(()), pltpu.SemaphoreType.DMA(())],
        ),
        out_shape=jax.ShapeDtypeStruct((cap, d), hn2.dtype),
        compiler_params=pltpu.CompilerParams(
            dimension_semantics=("arbitrary",), vmem_limit_bytes=VMEM_LIMIT),
        name="dispatch",
    )(seg, _tile_major(pos, td), hn2.reshape(b_sz * s_len, d))


def _experts_kernel(te_ref, nu_ref, xs_ref, wg_ref, wu_ref, wd_ref, y_ref, wgb_ref, wub_ref, wdb_ref):
    i = pl.program_id(0)
    e = te_ref[i]
    e_prev = te_ref[jnp.maximum(i - 1, 0)]
    new_expert = jnp.logical_or(i == 0, e != e_prev)
    used = i < nu_ref[0]

    @pl.when(jnp.logical_and(used, new_expert))
    def _():
        wgb_ref[...] = wg_ref[...].astype(BF16)
        wub_ref[...] = wu_ref[...].astype(BF16)
        wdb_ref[...] = wd_ref[...].astype(BF16)

    @pl.when(used)
    def _():
        xb = xs_ref[...].astype(BF16)
        gate = jnp.dot(xb, wgb_ref[...], preferred_element_type=F32)
        up = jnp.dot(xb, wub_ref[...], preferred_element_type=F32)
        hid = (gate * jax.nn.sigmoid(gate) * up).astype(BF16)
        y_ref[...] = jnp.dot(hid, wdb_ref[...], preferred_element_type=F32)

    @pl.when(jnp.logical_not(used))
    def _():
        y_ref[...] = jnp.zeros_like(y_ref)


def _experts(xs, tile_e, n_used, w_gate, w_up, w_down):
    cap, d = xs.shape
    tm = TM_EXPERT
    n_tiles = cap // tm

    def x_map(i, te, nu):
        return (jnp.minimum(i, jnp.maximum(nu[0] - 1, 0)), 0)

    def w_map(i, te, nu):
        return (te[i], 0, 0)

    return pl.pallas_call(
        _experts_kernel,
        grid_spec=pltpu.PrefetchScalarGridSpec(
            num_scalar_prefetch=2,
            grid=(n_tiles,),
            in_specs=[
                pl.BlockSpec((tm, d), x_map),
                pl.BlockSpec((None, d, D_FF), w_map),
                pl.BlockSpec((None, d, D_FF), w_map),
                pl.BlockSpec((None, D_FF, d), w_map),
            ],
            out_specs=pl.BlockSpec((tm, d), lambda i, te, nu: (i, 0)),
            scratch_shapes=[pltpu.VMEM((d, D_FF), BF16), pltpu.VMEM((d, D_FF), BF16),
                            pltpu.VMEM((D_FF, d), BF16)],
        ),
        out_shape=jax.ShapeDtypeStruct((cap, d), F32),
        compiler_params=pltpu.CompilerParams(
            dimension_semantics=("arbitrary",), vmem_limit_bytes=VMEM_LIMIT),
        name="experts",
    )(tile_e, n_used, xs, w_gate, w_up, w_down)


def _combine_kernel(pos_ref, pos_next_ref, x2_ref, rw_ref, gf_ref, y_ref, out_ref,
                    y1a_ref, y2a_ref, y1b_ref, y2b_ref, sem):
    tc = x2_ref.shape[0]
    step = pl.program_id(0)
    n_steps = pl.num_programs(0)
    parity = lax.rem(step, 2)
    bufs = ((y1a_ref, y2a_ref), (y1b_ref, y2b_ref))

    def gather(p_ref, slot):
        y1_ref, y2_ref = bufs[slot]
        for r in range(tc):
            pltpu.make_async_copy(y_ref.at[pl.ds(p_ref[r], 1)], y1_ref.at[pl.ds(r, 1)], sem.at[slot]).start()
            pltpu.make_async_copy(y_ref.at[pl.ds(p_ref[tc + r], 1)], y2_ref.at[pl.ds(r, 1)], sem.at[slot]).start()

    @pl.when(step == 0)
    def _():
        gather(pos_ref, 0)

    for slot in range(2):
        @pl.when(parity == slot)
        def _(slot=slot):
            y1_ref, y2_ref = bufs[slot]
            pltpu.make_async_copy(y_ref.at[pl.ds(0, tc)], y1_ref, sem.at[slot]).wait()
            pltpu.make_async_copy(y_ref.at[pl.ds(0, tc)], y2_ref, sem.at[slot]).wait()

            gather(pos_next_ref, 1 - slot)
            w_t = jnp.transpose(rw_ref[...])
            x3 = x2_ref[...] + w_t[:, 0:1] * y1_ref[...] + w_t[:, 1:2] * y2_ref[...]
            ms = jnp.mean(x3 * x3, axis=-1, keepdims=True)
            out_ref[...] = x3 * lax.rsqrt(ms + RMS_EPS) * gf_ref[...]

            @pl.when(step == n_steps - 1)
            def _():
                o1_ref, o2_ref = bufs[1 - slot]
                pltpu.make_async_copy(y_ref.at[pl.ds(0, tc)], o1_ref, sem.at[1 - slot]).wait()
                pltpu.make_async_copy(y_ref.at[pl.ds(0, tc)], o2_ref, sem.at[1 - slot]).wait()


def _combine(x2, pos, rw, y, norm_final):
    b_sz, s_len, d = x2.shape
    tc = TC_COMBINE
    n_s = s_len // tc
    n_steps = b_sz * n_s
    pos_tiles = _tile_major(pos, tc)
    return pl.pallas_call(
        _combine_kernel,
        grid=(n_steps,),
        in_specs=[
            pl.BlockSpec((2 * tc,), lambda g: (g,), memory_space=pltpu.SMEM),
            pl.BlockSpec((2 * tc,), lambda g: (jnp.minimum(g + 1, n_steps - 1),), memory_space=pltpu.SMEM),
            pl.BlockSpec((tc, d), lambda g: (g, 0)),
            pl.BlockSpec((None, 2 * N_HEADS, tc), lambda g: (g // n_s, 0, g % n_s)),
            pl.BlockSpec((1, d), lambda g: (0, 0)),
            pl.BlockSpec(memory_space=pl.ANY),
        ],
        out_specs=pl.BlockSpec((tc, d), lambda g: (g, 0)),
        scratch_shapes=[pltpu.VMEM((tc, d), F32), pltpu.VMEM((tc, d), F32),
                        pltpu.VMEM((tc, d), F32), pltpu.VMEM((tc, d), F32),
                        pltpu.SemaphoreType.DMA((2,))],
        out_shape=jax.ShapeDtypeStruct((b_sz * s_len, d), F32),
        compiler_params=pltpu.CompilerParams(
            dimension_semantics=("arbitrary",), vmem_limit_bytes=VMEM_LIMIT),
        name="combine",
    )(pos_tiles, pos_tiles, x2.reshape(b_sz * s_len, d), rw, norm_final[None, :], y).reshape(b_sz, s_len, d)


def _layer(x, norm_mix, w_in, b_in, conv_w, w_out, norm_ffn, w_group, b_group, w_expert, b_expert,
           w_gate, w_up, w_down, norm_out):
    b_sz, s_len, d = x.shape
    n_tok = b_sz * s_len
    a, kt, gt = _inproj(x, norm_mix, w_in, b_in)

    w_router = jnp.zeros((d, ROUTER_COLS), F32)
    w_router = w_router.at[:, 0:N_GROUPS].set(w_group).at[:, EXPERT_COL0:EXPERT_COL0 + N_EXPERTS].set(w_expert)
    b_router = jnp.zeros((1, ROUTER_COLS), F32)
    b_router = b_router.at[0, 0:N_GROUPS].set(b_group).at[0, EXPERT_COL0:EXPERT_COL0 + N_EXPERTS].set(b_expert)
    x2, hn2, ri, rw, counts = _mixer(x, a, kt, gt, conv_w, w_out, norm_ffn, w_router, b_router)

    tm = TM_EXPERT
    cnt = counts[:, 0]
    padded = ((cnt + tm - 1) // tm) * tm
    pends = jnp.cumsum(padded).astype(I32)
    seg = jnp.concatenate([jnp.zeros((1,), I32), pends])
    n_tiles = (n_tok * 2) // tm + N_EXPERTS
    cap = n_tiles * tm
    tile_start = jnp.arange(n_tiles, dtype=I32) * tm
    tile_e = jnp.minimum(jnp.sum((tile_start[:, None] >= pends[None, :]).astype(I32), axis=1),
                         N_EXPERTS - 1)
    n_used = pends[-1:] // tm
    e_sel = ri[:, 0:2, :]
    seg_start = jnp.sum(jnp.where(e_sel[..., None] == jnp.arange(N_EXPERTS, dtype=I32), seg[:-1], 0), axis=-1)
    pos = ri[:, 2:4, :] + seg_start

    xs = _dispatch(hn2, pos, seg, cap)
    y = _experts(xs, tile_e, n_used, w_gate, w_up, w_down)
    return _combine(x2, pos, rw, y, norm_out)


def kernel(x, norm_mix, w_in, b_in, conv_w, w_out, norm_ffn, w_group, b_group, w_expert, b_expert,
           w_gate, w_up, w_down, norm_final):
    depth = norm_mix.shape[0]
    assert depth == 1, "final-norm fusion below assumes a single layer"
    assert x.shape[-1] == 2 * D_MLSTM and x.shape[1] % TS_MIXER == 0
    return _layer(x, norm_mix[0], w_in[0], b_in[0], conv_w[0], w_out[0], norm_ffn[0],
                  w_group[0], b_group[0], w_expert[0], b_expert[0],
                  w_gate[0], w_up[0], w_down[0], norm_final)
```

```python
import jax
import jax.numpy as jnp
from jax import lax
from jax.experimental import pallas as pl
from jax.experimental.pallas import tpu as pltpu

F32 = jnp.float32
BF16 = jnp.bfloat16
I32 = jnp.int32

N_HEADS = 4
HEAD_DIM = 128
D_MLSTM = N_HEADS * HEAD_DIM
D_CONV = 512
N_GROUPS = 4
EXPERTS_PER_GROUP = 8
N_EXPERTS = N_GROUPS * EXPERTS_PER_GROUP
D_FF = 512
RMS_EPS = 1e-6
Q_SCALE = HEAD_DIM ** -0.5

LANES = 128
CHUNK = LANES
AUG_ROWS = 16
ROUTER_COLS = 128
EXPERT_COL0 = 8

TM_INPROJ = 512
TS_MIXER = 512
TD_DISPATCH = 512
OUT_COLS = 256
TM_EXPERT = 512
TC_COMBINE = 512

VMEM_LIMIT = 56 * 1024 * 1024


def _nt_dot(a, b):
    return lax.dot_general(a, b, (((1,), (1,)), ((), ())), preferred_element_type=F32)


def _inproj_kernel(x_ref, g_ref, wa_ref, ba_ref, wqv_ref, bqv_ref, wg_ref, bg_ref,
                   a_ref, qvt_ref, gt_ref):
    x = x_ref[...]
    ms = jnp.mean(x * x, axis=-1, keepdims=True)
    hn = (x * lax.rsqrt(ms + RMS_EPS) * g_ref[...]).astype(BF16)
    n_blk = wa_ref.shape[1] // D_MLSTM
    for j in range(n_blk):
        sl = slice(j * D_MLSTM, (j + 1) * D_MLSTM)
        acc = jnp.dot(hn, wa_ref[:, sl], preferred_element_type=F32) + ba_ref[:, sl]
        a_ref[:, sl] = acc.astype(BF16)
    q_rows = slice(0, D_MLSTM)
    v_rows = slice(D_MLSTM, 2 * D_MLSTM)
    qt = (_nt_dot(wqv_ref[q_rows, :], hn) + bqv_ref[q_rows, :]) * Q_SCALE
    qvt_ref[q_rows, :] = qt.astype(BF16)
    qvt_ref[v_rows, :] = (_nt_dot(wqv_ref[v_rows, :], hn) + bqv_ref[v_rows, :]).astype(BF16)
    gt_ref[...] = _nt_dot(wg_ref[...], hn) + bg_ref[...]


def _inproj(x, norm_mix, w_in, b_in):
    b_sz, s_len, d = x.shape
    tm = TM_INPROJ
    dm = D_MLSTM
    g0 = 4 * dm
    c0 = g0 + 2 * N_HEADS
    cols_a = jnp.concatenate([w_in[:, dm:2 * dm], w_in[:, 3 * dm:4 * dm], w_in[:, c0:c0 + 3 * D_CONV]], axis=1)
    bias_a = jnp.concatenate([b_in[dm:2 * dm], b_in[3 * dm:4 * dm], b_in[c0:c0 + 3 * D_CONV]])[None, :]
    wqv_t = jnp.concatenate([w_in[:, 0:dm], w_in[:, 2 * dm:3 * dm]], axis=1).T
    bqv = jnp.concatenate([b_in[0:dm], b_in[2 * dm:3 * dm]])[:, None]
    wg_t = w_in[:, g0:c0].T
    bg = b_in[g0:c0][:, None]
    n_a = cols_a.shape[1]
    return pl.pallas_call(
        _inproj_kernel,
        grid=(b_sz, s_len // tm),
        in_specs=[
            pl.BlockSpec((None, tm, d), lambda b, s: (b, s, 0)),
            pl.BlockSpec((1, d), lambda b, s: (0, 0)),
            pl.BlockSpec((d, n_a), lambda b, s: (0, 0)),
            pl.BlockSpec((1, n_a), lambda b, s: (0, 0)),
            pl.BlockSpec((2 * dm, d), lambda b, s: (0, 0)),
            pl.BlockSpec((2 * dm, 1), lambda b, s: (0, 0)),
            pl.BlockSpec((2 * N_HEADS, d), lambda b, s: (0, 0)),
            pl.BlockSpec((2 * N_HEADS, 1), lambda b, s: (0, 0)),
        ],
        out_specs=[
            pl.BlockSpec((None, tm, n_a), lambda b, s: (b, s, 0)),
            pl.BlockSpec((None, 2 * dm, tm), lambda b, s: (b, 0, s)),
            pl.BlockSpec((None, 2 * N_HEADS, tm), lambda b, s: (b, 0, s)),
        ],
        out_shape=[
            jax.ShapeDtypeStruct((b_sz, s_len, n_a), BF16),
            jax.ShapeDtypeStruct((b_sz, 2 * dm, s_len), BF16),
            jax.ShapeDtypeStruct((b_sz, 2 * N_HEADS, s_len), F32),
        ],
        compiler_params=pltpu.CompilerParams(
            dimension_semantics=("arbitrary", "arbitrary"), vmem_limit_bytes=VMEM_LIMIT),
        name="inproj",
    )(x, norm_mix[None, :], cols_a.astype(BF16), bias_a, wqv_t.astype(BF16), bqv, wg_t.astype(BF16), bg)


def _mixer_kernel(k_ref, o_ref, cb_ref, cc_ref, cx_ref, qt_ref, vt_ref, gt_ref, x_ref,
                  cw_ref, wo_ref, g2_ref, wr_ref, br_ref,
                  x2_ref, hn2_ref, ri_ref, rw_ref, cnt_out_ref,
                  ct_ref, m_ref, carry_ref, cnt_ref, mix_ref):
    ts = x_ref.shape[0]
    n_chunks = ts // CHUNK
    first_tile = pl.program_id(1) == 0

    @pl.when(first_tile)
    def _():
        ct_ref[...] = jnp.zeros_like(ct_ref)
        m_ref[...] = jnp.zeros_like(m_ref)
        carry_ref[...] = jnp.zeros_like(carry_ref)

    @pl.when(jnp.logical_and(first_tile, pl.program_id(0) == 0))
    def _():
        cnt_ref[...] = jnp.zeros_like(cnt_ref)

    gates = gt_ref[...]
    li_all = gates[0:N_HEADS]
    fg = gates[N_HEADS:2 * N_HEADS]
    lf_all = jnp.minimum(fg, 0.0) - jnp.log1p(jnp.exp(-jnp.abs(fg)))
    key_j = lax.broadcasted_iota(I32, (CHUNK, CHUNK), 0)
    qry_i = lax.broadcasted_iota(I32, (CHUNK, CHUNK), 1)
    causal_t = key_j <= qry_i
    neg_inf = jnp.float32(-jnp.inf)
    ones_rows = (lax.broadcasted_iota(I32, (AUG_ROWS, CHUNK), 0) == 0).astype(BF16)

    prefix_ones = (key_j <= qry_i).astype(BF16)

    lf_hi = lf_all.astype(BF16)
    rem = lf_all - lf_hi.astype(F32)
    lf_mid = rem.astype(BF16)
    lf_lo = (rem - lf_mid.astype(F32)).astype(BF16)
    zero_bf = jnp.zeros((N_HEADS, CHUNK), BF16)
    parts = []
    for c in range(n_chunks):
        rows = slice(c * CHUNK, (c + 1) * CHUNK)
        parts += [lf_hi[:, rows], lf_mid[:, rows], lf_lo[:, rows], zero_bf]
    cum = jnp.dot(jnp.concatenate(parts, axis=0), prefix_ones, preferred_element_type=F32)

    m_prev = m_ref[0:N_HEADS, 0:1]
    for c in range(n_chunks):
        rows = slice(c * CHUNK, (c + 1) * CHUNK)
        li = li_all[:, rows]
        r0 = 4 * N_HEADS * c
        b_cum = (cum[r0:r0 + N_HEADS] + cum[r0 + N_HEADS:r0 + 2 * N_HEADS]
                 + cum[r0 + 2 * N_HEADS:r0 + 3 * N_HEADS])
        g_tot = b_cum[:, CHUNK - 1:CHUNK]
        a_end = g_tot - b_cum + li
        m_loc = jnp.max(a_end, axis=1, keepdims=True)
        u = li - b_cum
        m_new = jnp.maximum(g_tot + m_prev, m_loc)
        decay = jnp.exp(g_tot + m_prev - m_new)
        s_fac = jnp.exp(m_loc - m_new)
        w_key = jnp.exp(a_end - m_loc)
        u_cols = jnp.transpose(jnp.concatenate([u, jnp.zeros_like(u)], axis=0))
        for h in range(N_HEADS):
            hc = slice(h * HEAD_DIM, (h + 1) * HEAD_DIM)
            kh = k_ref[rows, hc]
            qth = qt_ref[hc, rows]
            vt_aug = jnp.concatenate([vt_ref[hc, rows], ones_rows], axis=0)
            s_t = jnp.dot(kh, qth, preferred_element_type=F32)
            u_t = jnp.where(causal_t, u_cols[:, h:h + 1], neg_inf)
            mp = m_prev[h:h + 1, :]
            m_tot = jnp.maximum(jnp.max(u_t, axis=0, keepdims=True), mp)
            f_inter = jnp.exp(mp - m_tot)
            floor = jnp.exp(-b_cum[h:h + 1, :] - m_tot)
            p_t = (s_t * jnp.exp(u_t - m_tot)).astype(BF16)
            q_f = (qth.astype(F32) * f_inter).astype(BF16)
            c_old = ct_ref[h]
            lhs = jnp.concatenate([vt_aug, c_old.astype(BF16)], axis=1)
            rhs = jnp.concatenate([p_t, q_f], axis=0)
            r_t = jnp.dot(lhs, rhs, preferred_element_type=F32)
            vtw = (vt_aug.astype(F32) * w_key[h:h + 1, :]).astype(BF16)
            kv = jnp.dot(vtw, kh, preferred_element_type=F32)
            ct_ref[h] = decay[h:h + 1, :] * c_old + s_fac[h:h + 1, :] * kv

            den = r_t[HEAD_DIM:HEAD_DIM + 1, :]
            inv = 1.0 / jnp.maximum(jnp.abs(den), floor)
            h_nat = jnp.transpose(r_t[0:HEAD_DIM, :] * inv)
            o_gate = jax.nn.sigmoid(o_ref[rows, hc].astype(F32))
            mix_ref[rows, hc] = (o_gate * h_nat).astype(BF16)
        m_prev = m_new
    m_ref[0:N_HEADS, :] = jnp.broadcast_to(m_prev, (N_HEADS, LANES))

    u_c = cc_ref[...].astype(F32) * cx_ref[...].astype(F32)
    rid = lax.broadcasted_iota(I32, u_c.shape, 0)
    prev1 = carry_ref[0:1, :]
    prev2 = carry_ref[1:2, :]
    u1 = jnp.where(rid == 0, prev1, pltpu.roll(u_c, 1, axis=0))
    u2 = jnp.where(rid == 0, prev2, jnp.where(rid == 1, prev1, pltpu.roll(u_c, 2, axis=0)))
    z = cw_ref[0:1, :] * u2 + cw_ref[1:2, :] * u1 + cw_ref[2:3, :] * u_c
    mix_ref[:, D_MLSTM:] = (cb_ref[...].astype(F32) * z).astype(BF16)
    carry_ref[0:1, :] = u_c[ts - 1:ts, :]
    carry_ref[1:2, :] = u_c[ts - 2:ts - 1, :]

    d_model = x_ref.shape[1]
    mix = mix_ref[...]
    ssq = jnp.zeros((ts, 1), F32)
    for j in range(d_model // OUT_COLS):
        cs = slice(j * OUT_COLS, (j + 1) * OUT_COLS)
        x2c = x_ref[:, cs] + jnp.dot(mix, wo_ref[:, cs], preferred_element_type=F32)
        x2_ref[:, cs] = x2c
        ssq = ssq + jnp.sum(x2c * x2c, axis=-1, keepdims=True)
    hn2 = x2_ref[...] * lax.rsqrt(ssq * (1.0 / d_model) + RMS_EPS) * g2_ref[...]
    hn2_ref[...] = hn2

    h_hi = hn2.astype(BF16)
    h_lo = (hn2 - h_hi.astype(F32)).astype(BF16)
    hi_both = jnp.dot(h_hi, wr_ref[...], preferred_element_type=F32)
    lo_hi = jnp.dot(h_lo, wr_ref[:, 0:ROUTER_COLS], preferred_element_type=F32)
    logits = hi_both[:, 0:ROUTER_COLS] + (hi_both[:, ROUTER_COLS:] + lo_hi) + br_ref[...]
    lt = jnp.transpose(logits)
    gl = lt[0:N_GROUPS]
    gmax = jnp.max(gl, axis=0, keepdims=True)
    gi = lax.broadcasted_iota(I32, gl.shape, 0)
    g_sel = jnp.min(jnp.where(gl == gmax, gi, N_GROUPS), axis=0, keepdims=True)
    p_g = 1.0 / jnp.sum(jnp.exp(gl - gmax), axis=0, keepdims=True)
    in_g = lt[EXPERT_COL0:EXPERT_COL0 + EXPERTS_PER_GROUP]
    for g in range(1, N_GROUPS):
        lo = EXPERT_COL0 + g * EXPERTS_PER_GROUP
        in_g = jnp.where(g_sel == g, lt[lo:lo + EXPERTS_PER_GROUP], in_g)
    ei = lax.broadcasted_iota(I32, in_g.shape, 0)
    v1 = jnp.max(in_g, axis=0, keepdims=True)
    i1 = jnp.min(jnp.where(in_g == v1, ei, EXPERTS_PER_GROUP), axis=0, keepdims=True)
    rest = jnp.where(ei == i1, neg_inf, in_g)
    v2 = jnp.max(rest, axis=0, keepdims=True)
    i2 = jnp.min(jnp.where(rest == v2, ei, EXPERTS_PER_GROUP), axis=0, keepdims=True)
    t = jnp.exp(v2 - v1)
    w1 = p_g / (1.0 + t)
    w2 = p_g * t / (1.0 + t)
    e1 = g_sel * EXPERTS_PER_GROUP + i1
    e2 = g_sel * EXPERTS_PER_GROUP + i2

    eio = lax.broadcasted_iota(I32, (N_EXPERTS, LANES), 0)
    base = cnt_ref[:, 0:1]
    zero_i = jnp.zeros((1, LANES), I32)
    n_blk = ts // LANES
    hits = []
    for k in range(n_blk):
        ls = slice(k * LANES, (k + 1) * LANES)
        hits.append((eio == e1[:, ls], eio == e2[:, ls]))
    onehots = [jnp.where(jnp.logical_or(a1, a2), 1.0, 0.0) for a1, a2 in hits]
    incl_all = jnp.dot(jnp.concatenate(onehots, axis=0).astype(BF16), prefix_ones,
                       preferred_element_type=F32).astype(I32)
    for k in range(n_blk):
        ls = slice(k * LANES, (k + 1) * LANES)
        is1, is2 = hits[k]
        onehot = onehots[k].astype(I32)
        incl = incl_all[k * N_EXPERTS:(k + 1) * N_EXPERTS]
        rank_e = base + incl - onehot
        r1 = jnp.sum(jnp.where(is1, rank_e, 0), axis=0, keepdims=True)
        r2 = jnp.sum(jnp.where(is2, rank_e, 0), axis=0, keepdims=True)
        ri_ref[:, ls] = jnp.concatenate(
            [e1[:, ls], e2[:, ls], r1, r2, zero_i, zero_i, zero_i, zero_i], axis=0)
        base = base + incl[:, LANES - 1:LANES]
    zero_f = jnp.zeros_like(w1)
    rw_ref[...] = jnp.concatenate([w1, w2, zero_f, zero_f, zero_f, zero_f, zero_f, zero_f], axis=0)
    cnt_new = jnp.broadcast_to(base, (N_EXPERTS, LANES))
    cnt_ref[...] = cnt_new
    cnt_out_ref[...] = cnt_new


def _mixer(x, a, qvt, gt, conv_w, w_out, norm_ffn, w_router, b_router):
    b_sz, s_len, d = x.shape
    ts = TS_MIXER
    dm = D_MLSTM

    def a_spec(j):
        return pl.BlockSpec((None, ts, dm), lambda b, s, j=j: (b, s, j))

    tok = lambda w: pl.BlockSpec((None, ts, w), lambda b, s: (b, s, 0))
    rowsp = pl.BlockSpec((None, 2 * N_HEADS, ts), lambda b, s: (b, 0, s))
    const = lambda shape: pl.BlockSpec(shape, lambda b, s: tuple(0 for _ in shape))
    wr_hi = w_router.astype(BF16)
    wr_lo = (w_router - wr_hi.astype(F32)).astype(BF16)
    w_router2 = jnp.concatenate([wr_hi, wr_lo], axis=1)
    return pl.pallas_call(
        _mixer_kernel,
        grid=(b_sz, s_len // ts),
        in_specs=[a_spec(0), a_spec(1), a_spec(2), a_spec(3), a_spec(4),
                  pl.BlockSpec((None, dm, ts), lambda b, s: (b, 0, s)),
                  pl.BlockSpec((None, dm, ts), lambda b, s: (b, 1, s)),
                  rowsp, tok(d),
                  const((3, D_CONV)), const((d, d)), const((1, d)),
                  const((d, 2 * ROUTER_COLS)), const((1, ROUTER_COLS))],
        out_specs=[tok(d), tok(d), rowsp, rowsp, const((N_EXPERTS, LANES))],
        out_shape=[
            jax.ShapeDtypeStruct((b_sz, s_len, d), F32),
            jax.ShapeDtypeStruct((b_sz, s_len, d), F32),
            jax.ShapeDtypeStruct((b_sz, 2 * N_HEADS, s_len), I32),
            jax.ShapeDtypeStruct((b_sz, 2 * N_HEADS, s_len), F32),
            jax.ShapeDtypeStruct((N_EXPERTS, LANES), I32),
        ],
        scratch_shapes=[
            pltpu.VMEM((N_HEADS, HEAD_DIM + AUG_ROWS, HEAD_DIM), F32),
            pltpu.VMEM((2 * N_HEADS, LANES), F32),
            pltpu.VMEM((8, D_CONV), F32),
            pltpu.VMEM((N_EXPERTS, LANES), I32),
            pltpu.VMEM((ts, d), BF16),
        ],
        compiler_params=pltpu.CompilerParams(
            dimension_semantics=("arbitrary", "arbitrary"), vmem_limit_bytes=VMEM_LIMIT),
        name="mixer",
    )(a, a, a, a, a, qvt, qvt, gt, x, conv_w, w_out.astype(BF16), norm_ffn[None, :], w_router2, b_router)


def _tile_major(pos, t):
    b_sz, _, s_len = pos.shape
    return pos.reshape(b_sz, 2, s_len // t, t).transpose(0, 2, 1, 3).reshape(-1)


def _dispatch_kernel(seg_ref, pos_ref, h_ref, xs_ref, zero_ref, sem, zsem):
    td = h_ref.shape[0]
    tm = zero_ref.shape[0]

    @pl.when(pl.program_id(0) == 0)
    def _():
        zero_ref[...] = jnp.zeros_like(zero_ref)

        def zero_copy(e):
            last_tile = pl.multiple_of(seg_ref[e + 1] - tm, tm)
            return pltpu.make_async_copy(zero_ref, xs_ref.at[pl.ds(last_tile, tm)], zsem)

        for e in range(N_EXPERTS):
            @pl.when(seg_ref[e + 1] > seg_ref[e])
            def _(e=e):
                zero_copy(e).start()
        for e in range(N_EXPERTS):
            @pl.when(seg_ref[e + 1] > seg_ref[e])
            def _(e=e):
                zero_copy(e).wait()

        def tail_copy(t):
            return pltpu.make_async_copy(zero_ref, xs_ref.at[pl.ds(pl.multiple_of(t * tm, tm), tm)], zsem)

        first_unused = seg_ref[N_EXPERTS] // tm
        n_tiles = xs_ref.shape[0] // tm
        lax.fori_loop(first_unused, n_tiles, lambda t, c: (tail_copy(t).start(), c)[1], 0)
        lax.fori_loop(first_unused, n_tiles, lambda t, c: (tail_copy(t).wait(), c)[1], 0)

    for r in range(td):
        src = h_ref.at[pl.ds(r, 1)]
        pltpu.make_async_copy(src, xs_ref.at[pl.ds(pos_ref[r], 1)], sem).start()
        pltpu.make_async_copy(src, xs_ref.at[pl.ds(pos_ref[td + r], 1)], sem).start()
    for _ in range(2):
        pltpu.make_async_copy(h_ref, xs_ref.at[pl.ds(0, td)], sem).wait()


def _dispatch(hn2, pos, seg, cap):
    b_sz, s_len, d = hn2.shape
    td = TD_DISPATCH
    return pl.pallas_call(
        _dispatch_kernel,
        grid_spec=pltpu.PrefetchScalarGridSpec(
            num_scalar_prefetch=1,
            grid=(b_sz * s_len // td,),
            in_specs=[
                pl.BlockSpec((2 * td,), lambda g, sg: (g,), memory_space=pltpu.SMEM),
                pl.BlockSpec((td, d), lambda g, sg: (g, 0)),
            ],
            out_specs=pl.BlockSpec(memory_space=pl.ANY),
            scratch_shapes=[pltpu.VMEM((TM_EXPERT, d), hn2.dtype),
                            pltpu.SemaphoreType.DMA(()), pltpu.SemaphoreType.DMA(())],
        ),
        out_shape=jax.ShapeDtypeStruct((cap, d), hn2.dtype),
        compiler_params=pltpu.CompilerParams(
            dimension_semantics=("arbitrary",), vmem_limit_bytes=VMEM_LIMIT),
        name="dispatch",
    )(seg, _tile_major(pos, td), hn2.reshape(b_sz * s_len, d))


def _experts_kernel(te_ref, nu_ref, xs_ref, wg_ref, wu_ref, wd_ref, y_ref, wgb_ref, wub_ref, wdb_ref):
    i = pl.program_id(0)
    e = te_ref[i]
    e_prev = te_ref[jnp.maximum(i - 1, 0)]
    new_expert = jnp.logical_or(i == 0, e != e_prev)
    used = i < nu_ref[0]

    @pl.when(jnp.logical_and(used, new_expert))
    def _():
        wgb_ref[...] = wg_ref[...].astype(BF16)
        wub_ref[...] = wu_ref[...].astype(BF16)
        wdb_ref[...] = wd_ref[...].astype(BF16)

    @pl.when(used)
    def _():
        xb = xs_ref[...].astype(BF16)
        gate = jnp.dot(xb, wgb_ref[...], preferred_element_type=F32)
        up = jnp.dot(xb, wub_ref[...], preferred_element_type=F32)
        hid = (gate * jax.nn.sigmoid(gate) * up).astype(BF16)
        y_ref[...] = jnp.dot(hid, wdb_ref[...], preferred_element_type=F32)

    @pl.when(jnp.logical_not(used))
    def _():
        y_ref[...] = jnp.zeros_like(y_ref)


def _experts(xs, tile_e, n_used, w_gate, w_up, w_down):
    cap, d = xs.shape
    tm = TM_EXPERT
    n_tiles = cap // tm

    def x_map(i, te, nu):
        return (jnp.minimum(i, jnp.maximum(nu[0] - 1, 0)), 0)

    def w_map(i, te, nu):
        return (te[i], 0, 0)

    return pl.pallas_call(
        _experts_kernel,
        grid_spec=pltpu.PrefetchScalarGridSpec(
            num_scalar_prefetch=2,
            grid=(n_tiles,),
            in_specs=[
                pl.BlockSpec((tm, d), x_map),
                pl.BlockSpec((None, d, D_FF), w_map),
                pl.BlockSpec((None, d, D_FF), w_map),
                pl.BlockSpec((None, D_FF, d), w_map),
            ],
            out_specs=pl.BlockSpec((tm, d), lambda i, te, nu: (i, 0)),
            scratch_shapes=[pltpu.VMEM((d, D_FF), BF16), pltpu.VMEM((d, D_FF), BF16),
                            pltpu.VMEM((D_FF, d), BF16)],
        ),
        out_shape=jax.ShapeDtypeStruct((cap, d), F32),
        compiler_params=pltpu.CompilerParams(
            dimension_semantics=("arbitrary",), vmem_limit_bytes=VMEM_LIMIT),
        name="experts",
    )(tile_e, n_used, xs, w_gate, w_up, w_down)


def _combine_kernel(pos_ref, pos_next_ref, x2_ref, rw_ref, gf_ref, y_ref, out_ref, y1_ref, y2_ref, sem):
    tc = x2_ref.shape[0]
    step = pl.program_id(0)
    n_steps = pl.num_programs(0)
    slot = lax.rem(step, 2)

    def gather(p_ref, slot):
        for r in range(tc):
            pltpu.make_async_copy(y_ref.at[pl.ds(p_ref[r], 1)],
                                  y1_ref.at[slot, pl.ds(r, 1)], sem.at[slot]).start()
            pltpu.make_async_copy(y_ref.at[pl.ds(p_ref[tc + r], 1)],
                                  y2_ref.at[slot, pl.ds(r, 1)], sem.at[slot]).start()

    @pl.when(step == 0)
    def _():
        gather(pos_ref, 0)

    @pl.when(step + 1 < n_steps)
    def _():
        gather(pos_next_ref, 1 - slot)

    w_t = jnp.transpose(rw_ref[...])
    pltpu.make_async_copy(y_ref.at[pl.ds(0, tc)], y1_ref.at[slot], sem.at[slot]).wait()
    pltpu.make_async_copy(y_ref.at[pl.ds(0, tc)], y2_ref.at[slot], sem.at[slot]).wait()
    x3 = x2_ref[...] + w_t[:, 0:1] * y1_ref[slot] + w_t[:, 1:2] * y2_ref[slot]
    ms = jnp.mean(x3 * x3, axis=-1, keepdims=True)
    out_ref[...] = x3 * lax.rsqrt(ms + RMS_EPS) * gf_ref[...]


def _combine(x2, pos, rw, y, norm_final):
    b_sz, s_len, d = x2.shape
    tc = TC_COMBINE
    n_s = s_len // tc
    n_steps = b_sz * n_s
    pos_tiles = _tile_major(pos, tc)
    return pl.pallas_call(
        _combine_kernel,
        grid=(n_steps,),
        in_specs=[
            pl.BlockSpec((2 * tc,), lambda g: (g,), memory_space=pltpu.SMEM),
            pl.BlockSpec((2 * tc,), lambda g: (jnp.minimum(g + 1, n_steps - 1),), memory_space=pltpu.SMEM),
            pl.BlockSpec((tc, d), lambda g: (g, 0)),
            pl.BlockSpec((None, 2 * N_HEADS, tc), lambda g: (g // n_s, 0, g % n_s)),
            pl.BlockSpec((1, d), lambda g: (0, 0)),
            pl.BlockSpec(memory_space=pl.ANY),
        ],
        out_specs=pl.BlockSpec((tc, d), lambda g: (g, 0)),
        scratch_shapes=[pltpu.VMEM((2, tc, d), F32), pltpu.VMEM((2, tc, d), F32),
                        pltpu.SemaphoreType.DMA((2,))],
        out_shape=jax.ShapeDtypeStruct((b_sz * s_len, d), F32),
        compiler_params=pltpu.CompilerParams(
            dimension_semantics=("arbitrary",), vmem_limit_bytes=VMEM_LIMIT),
        name="combine",
    )(pos_tiles, pos_tiles, x2.reshape(b_sz * s_len, d), rw, norm_final[None, :], y).reshape(b_sz, s_len, d)


def _layer(x, norm_mix, w_in, b_in, conv_w, w_out, norm_ffn, w_group, b_group, w_expert, b_expert,
           w_gate, w_up, w_down, norm_out):
    b_sz, s_len, d = x.shape
    n_tok = b_sz * s_len
    a, qvt, gt = _inproj(x, norm_mix, w_in, b_in)

    w_router = jnp.zeros((d, ROUTER_COLS), F32)
    w_router = w_router.at[:, 0:N_GROUPS].set(w_group).at[:, EXPERT_COL0:EXPERT_COL0 + N_EXPERTS].set(w_expert)
    b_router = jnp.zeros((1, ROUTER_COLS), F32)
    b_router = b_router.at[0, 0:N_GROUPS].set(b_group).at[0, EXPERT_COL0:EXPERT_COL0 + N_EXPERTS].set(b_expert)
    x2, hn2, ri, rw, counts = _mixer(x, a, qvt, gt, conv_w, w_out, norm_ffn, w_router, b_router)

    tm = TM_EXPERT
    cnt = counts[:, 0]
    padded = ((cnt + tm - 1) // tm) * tm
    pends = jnp.cumsum(padded).astype(I32)
    seg = jnp.concatenate([jnp.zeros((1,), I32), pends])
    n_tiles = (n_tok * 2) // tm + N_EXPERTS
    cap = n_tiles * tm
    tile_start = jnp.arange(n_tiles, dtype=I32) * tm
    tile_e = jnp.minimum(jnp.sum((tile_start[:, None] >= pends[None, :]).astype(I32), axis=1),
                         N_EXPERTS - 1)
    n_used = pends[-1:] // tm
    e_sel = ri[:, 0:2, :]
    seg_start = jnp.sum(jnp.where(e_sel[..., None] == jnp.arange(N_EXPERTS, dtype=I32), seg[:-1], 0), axis=-1)
    pos = ri[:, 2:4, :] + seg_start

    xs = _dispatch(hn2, pos, seg, cap)
    y = _experts(xs, tile_e, n_used, w_gate, w_up, w_down)
    return _combine(x2, pos, rw, y, norm_out)


def kernel(x, norm_mix, w_in, b_in, conv_w, w_out, norm_ffn, w_group, b_group, w_expert, b_expert,
           w_gate, w_up, w_down, norm_final):
    depth = norm_mix.shape[0]
    assert depth == 1, "final-norm fusion below assumes a single layer"
    assert x.shape[-1] == 2 * D_MLSTM and x.shape[1] % TS_MIXER == 0
    return _layer(x, norm_mix[0], w_in[0], b_in[0], conv_w[0], w_out[0], norm_ffn[0],
                  w_group[0], b_group[0], w_expert[0], b_expert[0],
                  w_gate[0], w_up[0], w_down[0], norm_final)
```

```python
import jax
import jax.numpy as jnp
from jax import lax
from jax.experimental import pallas as pl
from jax.experimental.pallas import tpu as pltpu

F32 = jnp.float32
BF16 = jnp.bfloat16
I32 = jnp.int32

N_HEADS = 4
HEAD_DIM = 128
D_MLSTM = N_HEADS * HEAD_DIM
D_CONV = 512
N_GROUPS = 4
EXPERTS_PER_GROUP = 8
N_EXPERTS = N_GROUPS * EXPERTS_PER_GROUP
D_FF = 512
RMS_EPS = 1e-6
Q_SCALE = HEAD_DIM ** -0.5

LANES = 128
CHUNK = LANES
AUG_ROWS = 16
ROUTER_COLS = 128
EXPERT_COL0 = 8

TM_INPROJ = 512
TS_MIXER = 512
TD_DISPATCH = 512
OUT_COLS = 256
TM_EXPERT = 512
TC_COMBINE = 512

VMEM_LIMIT = 56 * 1024 * 1024


U32 = jnp.uint32
HI_HALF = 0xFFFF0000


def _pack_rows(x):
    w = x.shape[1] // 2
    lo = lax.bitcast_convert_type(x[:, :w].astype(BF16).astype(F32), U32) >> 16
    hi = lax.bitcast_convert_type(x[:, w:].astype(BF16).astype(F32), U32) & jnp.uint32(HI_HALF)
    return lo | hi


def _unpack_rows(words):
    lo = lax.bitcast_convert_type(words << 16, F32)
    hi = lax.bitcast_convert_type(words & jnp.uint32(HI_HALF), F32)
    return lo, hi


def _nt_dot(a, b):
    return lax.dot_general(a, b, (((1,), (1,)), ((), ())), preferred_element_type=F32)


def _inproj_kernel(x_ref, g_ref, wa_ref, ba_ref, wqv_ref, bqv_ref, wg_ref, bg_ref,
                   a_ref, qvt_ref, gt_ref):
    x = x_ref[...]
    ms = jnp.mean(x * x, axis=-1, keepdims=True)
    hn = (x * lax.rsqrt(ms + RMS_EPS) * g_ref[...]).astype(BF16)
    n_blk = wa_ref.shape[1] // D_MLSTM
    for j in range(n_blk):
        sl = slice(j * D_MLSTM, (j + 1) * D_MLSTM)
        acc = jnp.dot(hn, wa_ref[:, sl], preferred_element_type=F32) + ba_ref[:, sl]
        a_ref[:, sl] = acc.astype(BF16)
    q_rows = slice(0, D_MLSTM)
    v_rows = slice(D_MLSTM, 2 * D_MLSTM)
    qt = (_nt_dot(wqv_ref[q_rows, :], hn) + bqv_ref[q_rows, :]) * Q_SCALE
    qvt_ref[q_rows, :] = qt.astype(BF16)
    qvt_ref[v_rows, :] = (_nt_dot(wqv_ref[v_rows, :], hn) + bqv_ref[v_rows, :]).astype(BF16)
    gt_ref[...] = _nt_dot(wg_ref[...], hn) + bg_ref[...]


def _inproj(x, norm_mix, w_in, b_in):
    b_sz, s_len, d = x.shape
    tm = TM_INPROJ
    dm = D_MLSTM
    g0 = 4 * dm
    c0 = g0 + 2 * N_HEADS
    cols_a = jnp.concatenate([w_in[:, dm:2 * dm], w_in[:, 3 * dm:4 * dm], w_in[:, c0:c0 + 3 * D_CONV]], axis=1)
    bias_a = jnp.concatenate([b_in[dm:2 * dm], b_in[3 * dm:4 * dm], b_in[c0:c0 + 3 * D_CONV]])[None, :]
    wqv_t = jnp.concatenate([w_in[:, 0:dm], w_in[:, 2 * dm:3 * dm]], axis=1).T
    bqv = jnp.concatenate([b_in[0:dm], b_in[2 * dm:3 * dm]])[:, None]
    wg_t = w_in[:, g0:c0].T
    bg = b_in[g0:c0][:, None]
    n_a = cols_a.shape[1]
    return pl.pallas_call(
        _inproj_kernel,
        grid=(b_sz, s_len // tm),
        in_specs=[
            pl.BlockSpec((None, tm, d), lambda b, s: (b, s, 0)),
            pl.BlockSpec((1, d), lambda b, s: (0, 0)),
            pl.BlockSpec((d, n_a), lambda b, s: (0, 0)),
            pl.BlockSpec((1, n_a), lambda b, s: (0, 0)),
            pl.BlockSpec((2 * dm, d), lambda b, s: (0, 0)),
            pl.BlockSpec((2 * dm, 1), lambda b, s: (0, 0)),
            pl.BlockSpec((2 * N_HEADS, d), lambda b, s: (0, 0)),
            pl.BlockSpec((2 * N_HEADS, 1), lambda b, s: (0, 0)),
        ],
        out_specs=[
            pl.BlockSpec((None, tm, n_a), lambda b, s: (b, s, 0)),
            pl.BlockSpec((None, 2 * dm, tm), lambda b, s: (b, 0, s)),
            pl.BlockSpec((None, 2 * N_HEADS, tm), lambda b, s: (b, 0, s)),
        ],
        out_shape=[
            jax.ShapeDtypeStruct((b_sz, s_len, n_a), BF16),
            jax.ShapeDtypeStruct((b_sz, 2 * dm, s_len), BF16),
            jax.ShapeDtypeStruct((b_sz, 2 * N_HEADS, s_len), F32),
        ],
        compiler_params=pltpu.CompilerParams(
            dimension_semantics=("arbitrary", "arbitrary"), vmem_limit_bytes=VMEM_LIMIT),
        name="inproj",
    )(x, norm_mix[None, :], cols_a.astype(BF16), bias_a, wqv_t.astype(BF16), bqv, wg_t.astype(BF16), bg)


def _mixer_kernel(k_ref, o_ref, cb_ref, cc_ref, cx_ref, qt_ref, vt_ref, gt_ref, x_ref,
                  cw_ref, wo_ref, g2_ref, wr_ref, br_ref,
                  x2_ref, hn2_ref, ri_ref, rw_ref, cnt_out_ref,
                  ct_ref, m_ref, carry_ref, cnt_ref, mix_ref):
    ts = x_ref.shape[0]
    n_chunks = ts // CHUNK
    first_tile = pl.program_id(1) == 0

    @pl.when(first_tile)
    def _():
        ct_ref[...] = jnp.zeros_like(ct_ref)
        m_ref[...] = jnp.zeros_like(m_ref)
        carry_ref[...] = jnp.zeros_like(carry_ref)

    @pl.when(jnp.logical_and(first_tile, pl.program_id(0) == 0))
    def _():
        cnt_ref[...] = jnp.zeros_like(cnt_ref)

    gates = gt_ref[...]
    li_all = gates[0:N_HEADS]
    fg = gates[N_HEADS:2 * N_HEADS]
    lf_all = jnp.minimum(fg, 0.0) - jnp.log1p(jnp.exp(-jnp.abs(fg)))
    key_j = lax.broadcasted_iota(I32, (CHUNK, CHUNK), 0)
    qry_i = lax.broadcasted_iota(I32, (CHUNK, CHUNK), 1)
    causal_t = key_j <= qry_i
    neg_inf = jnp.float32(-jnp.inf)
    ones_rows = (lax.broadcasted_iota(I32, (AUG_ROWS, CHUNK), 0) == 0).astype(BF16)

    prefix_ones = (key_j <= qry_i).astype(BF16)

    lf_hi = lf_all.astype(BF16)
    rem = lf_all - lf_hi.astype(F32)
    lf_mid = rem.astype(BF16)
    lf_lo = (rem - lf_mid.astype(F32)).astype(BF16)
    zero_bf = jnp.zeros((N_HEADS, CHUNK), BF16)
    parts = []
    for c in range(n_chunks):
        rows = slice(c * CHUNK, (c + 1) * CHUNK)
        parts += [lf_hi[:, rows], lf_mid[:, rows], lf_lo[:, rows], zero_bf]
    cum = jnp.dot(jnp.concatenate(parts, axis=0), prefix_ones, preferred_element_type=F32)

    m_prev = m_ref[0:N_HEADS, 0:1]
    for c in range(n_chunks):
        rows = slice(c * CHUNK, (c + 1) * CHUNK)
        li = li_all[:, rows]
        r0 = 4 * N_HEADS * c
        b_cum = (cum[r0:r0 + N_HEADS] + cum[r0 + N_HEADS:r0 + 2 * N_HEADS]
                 + cum[r0 + 2 * N_HEADS:r0 + 3 * N_HEADS])
        g_tot = b_cum[:, CHUNK - 1:CHUNK]
        a_end = g_tot - b_cum + li
        m_loc = jnp.max(a_end, axis=1, keepdims=True)
        u = li - b_cum
        m_new = jnp.maximum(g_tot + m_prev, m_loc)
        decay = jnp.exp(g_tot + m_prev - m_new)
        s_fac = jnp.exp(m_loc - m_new)
        w_key = jnp.exp(a_end - m_loc)
        u_cols = jnp.transpose(jnp.concatenate([u, jnp.zeros_like(u)], axis=0))
        for h in range(N_HEADS):
            hc = slice(h * HEAD_DIM, (h + 1) * HEAD_DIM)
            kh = k_ref[rows, hc]
            qth = qt_ref[hc, rows]
            vt_aug = jnp.concatenate([vt_ref[hc, rows], ones_rows], axis=0)
            s_t = jnp.dot(kh, qth, preferred_element_type=F32)
            u_t = jnp.where(causal_t, u_cols[:, h:h + 1], neg_inf)
            mp = m_prev[h:h + 1, :]
            m_tot = jnp.maximum(jnp.max(u_t, axis=0, keepdims=True), mp)
            f_inter = jnp.exp(mp - m_tot)
            floor = jnp.exp(-b_cum[h:h + 1, :] - m_tot)
            p_t = (s_t * jnp.exp(u_t - m_tot)).astype(BF16)
            q_f = (qth.astype(F32) * f_inter).astype(BF16)
            c_old = ct_ref[h]
            lhs = jnp.concatenate([vt_aug, c_old.astype(BF16)], axis=1)
            rhs = jnp.concatenate([p_t, q_f], axis=0)
            r_t = jnp.dot(lhs, rhs, preferred_element_type=F32)
            vtw = (vt_aug.astype(F32) * w_key[h:h + 1, :]).astype(BF16)
            kv = jnp.dot(vtw, kh, preferred_element_type=F32)
            ct_ref[h] = decay[h:h + 1, :] * c_old + s_fac[h:h + 1, :] * kv

            den = r_t[HEAD_DIM:HEAD_DIM + 1, :]
            inv = 1.0 / jnp.maximum(jnp.abs(den), floor)
            h_nat = jnp.transpose(r_t[0:HEAD_DIM, :] * inv)
            o_gate = jax.nn.sigmoid(o_ref[rows, hc].astype(F32))
            mix_ref[rows, hc] = (o_gate * h_nat).astype(BF16)
        m_prev = m_new
    m_ref[0:N_HEADS, :] = jnp.broadcast_to(m_prev, (N_HEADS, LANES))

    u_c = cc_ref[...].astype(F32) * cx_ref[...].astype(F32)
    rid = lax.broadcasted_iota(I32, u_c.shape, 0)
    prev1 = carry_ref[0:1, :]
    prev2 = carry_ref[1:2, :]
    u1 = jnp.where(rid == 0, prev1, pltpu.roll(u_c, 1, axis=0))
    u2 = jnp.where(rid == 0, prev2, jnp.where(rid == 1, prev1, pltpu.roll(u_c, 2, axis=0)))
    z = cw_ref[0:1, :] * u2 + cw_ref[1:2, :] * u1 + cw_ref[2:3, :] * u_c
    mix_ref[:, D_MLSTM:] = (cb_ref[...].astype(F32) * z).astype(BF16)
    carry_ref[0:1, :] = u_c[ts - 1:ts, :]
    carry_ref[1:2, :] = u_c[ts - 2:ts - 1, :]

    d_model = x_ref.shape[1]
    mix = mix_ref[...]
    ssq = jnp.zeros((ts, 1), F32)
    for j in range(d_model // OUT_COLS):
        cs = slice(j * OUT_COLS, (j + 1) * OUT_COLS)
        x2c = x_ref[:, cs] + jnp.dot(mix, wo_ref[:, cs], preferred_element_type=F32)
        x2_ref[:, cs] = x2c
        ssq = ssq + jnp.sum(x2c * x2c, axis=-1, keepdims=True)
    hn2 = x2_ref[...] * lax.rsqrt(ssq * (1.0 / d_model) + RMS_EPS) * g2_ref[...]
    hn2_ref[...] = _pack_rows(hn2)

    h_hi = hn2.astype(BF16)
    h_lo = (hn2 - h_hi.astype(F32)).astype(BF16)
    hi_both = jnp.dot(h_hi, wr_ref[...], preferred_element_type=F32)
    lo_hi = jnp.dot(h_lo, wr_ref[:, 0:ROUTER_COLS], preferred_element_type=F32)
    logits = hi_both[:, 0:ROUTER_COLS] + (hi_both[:, ROUTER_COLS:] + lo_hi) + br_ref[...]
    lt = jnp.transpose(logits)
    gl = lt[0:N_GROUPS]
    gmax = jnp.max(gl, axis=0, keepdims=True)
    gi = lax.broadcasted_iota(I32, gl.shape, 0)
    g_sel = jnp.min(jnp.where(gl == gmax, gi, N_GROUPS), axis=0, keepdims=True)
    p_g = 1.0 / jnp.sum(jnp.exp(gl - gmax), axis=0, keepdims=True)
    in_g = lt[EXPERT_COL0:EXPERT_COL0 + EXPERTS_PER_GROUP]
    for g in range(1, N_GROUPS):
        lo = EXPERT_COL0 + g * EXPERTS_PER_GROUP
        in_g = jnp.where(g_sel == g, lt[lo:lo + EXPERTS_PER_GROUP], in_g)
    ei = lax.broadcasted_iota(I32, in_g.shape, 0)
    v1 = jnp.max(in_g, axis=0, keepdims=True)
    i1 = jnp.min(jnp.where(in_g == v1, ei, EXPERTS_PER_GROUP), axis=0, keepdims=True)
    rest = jnp.where(ei == i1, neg_inf, in_g)
    v2 = jnp.max(rest, axis=0, keepdims=True)
    i2 = jnp.min(jnp.where(rest == v2, ei, EXPERTS_PER_GROUP), axis=0, keepdims=True)
    t = jnp.exp(v2 - v1)
    w1 = p_g / (1.0 + t)
    w2 = p_g * t / (1.0 + t)
    e1 = g_sel * EXPERTS_PER_GROUP + i1
    e2 = g_sel * EXPERTS_PER_GROUP + i2

    eio = lax.broadcasted_iota(I32, (N_EXPERTS, LANES), 0)
    base = cnt_ref[:, 0:1]
    zero_i = jnp.zeros((1, LANES), I32)
    n_blk = ts // LANES
    hits = []
    for k in range(n_blk):
        ls = slice(k * LANES, (k + 1) * LANES)
        hits.append((eio == e1[:, ls], eio == e2[:, ls]))
    onehots = [jnp.where(jnp.logical_or(a1, a2), 1.0, 0.0) for a1, a2 in hits]
    incl_all = jnp.dot(jnp.concatenate(onehots, axis=0).astype(BF16), prefix_ones,
                       preferred_element_type=F32).astype(I32)
    for k in range(n_blk):
        ls = slice(k * LANES, (k + 1) * LANES)
        is1, is2 = hits[k]
        onehot = onehots[k].astype(I32)
        incl = incl_all[k * N_EXPERTS:(k + 1) * N_EXPERTS]
        rank_e = base + incl - onehot
        r1 = jnp.sum(jnp.where(is1, rank_e, 0), axis=0, keepdims=True)
        r2 = jnp.sum(jnp.where(is2, rank_e, 0), axis=0, keepdims=True)
        ri_ref[:, ls] = jnp.concatenate(
            [e1[:, ls], e2[:, ls], r1, r2, zero_i, zero_i, zero_i, zero_i], axis=0)
        base = base + incl[:, LANES - 1:LANES]
    zero_f = jnp.zeros_like(w1)
    rw_ref[...] = jnp.concatenate([w1, w2, zero_f, zero_f, zero_f, zero_f, zero_f, zero_f], axis=0)
    cnt_new = jnp.broadcast_to(base, (N_EXPERTS, LANES))
    cnt_ref[...] = cnt_new
    cnt_out_ref[...] = cnt_new


def _mixer(x, a, qvt, gt, conv_w, w_out, norm_ffn, w_router, b_router):
    b_sz, s_len, d = x.shape
    ts = TS_MIXER
    dm = D_MLSTM

    def a_spec(j):
        return pl.BlockSpec((None, ts, dm), lambda b, s, j=j: (b, s, j))

    tok = lambda w: pl.BlockSpec((None, ts, w), lambda b, s: (b, s, 0))
    rowsp = pl.BlockSpec((None, 2 * N_HEADS, ts), lambda b, s: (b, 0, s))
    const = lambda shape: pl.BlockSpec(shape, lambda b, s: tuple(0 for _ in shape))
    wr_hi = w_router.astype(BF16)
    wr_lo = (w_router - wr_hi.astype(F32)).astype(BF16)
    w_router2 = jnp.concatenate([wr_hi, wr_lo], axis=1)
    return pl.pallas_call(
        _mixer_kernel,
        grid=(b_sz, s_len // ts),
        in_specs=[a_spec(0), a_spec(1), a_spec(2), a_spec(3), a_spec(4),
                  pl.BlockSpec((None, dm, ts), lambda b, s: (b, 0, s)),
                  pl.BlockSpec((None, dm, ts), lambda b, s: (b, 1, s)),
                  rowsp, tok(d),
                  const((3, D_CONV)), const((d, d)), const((1, d)),
                  const((d, 2 * ROUTER_COLS)), const((1, ROUTER_COLS))],
        out_specs=[tok(d), tok(d // 2), rowsp, rowsp, const((N_EXPERTS, LANES))],
        out_shape=[
            jax.ShapeDtypeStruct((b_sz, s_len, d), F32),
            jax.ShapeDtypeStruct((b_sz, s_len, d // 2), U32),
            jax.ShapeDtypeStruct((b_sz, 2 * N_HEADS, s_len), I32),
            jax.ShapeDtypeStruct((b_sz, 2 * N_HEADS, s_len), F32),
            jax.ShapeDtypeStruct((N_EXPERTS, LANES), I32),
        ],
        scratch_shapes=[
            pltpu.VMEM((N_HEADS, HEAD_DIM + AUG_ROWS, HEAD_DIM), F32),
            pltpu.VMEM((2 * N_HEADS, LANES), F32),
            pltpu.VMEM((8, D_CONV), F32),
            pltpu.VMEM((N_EXPERTS, LANES), I32),
            pltpu.VMEM((ts, d), BF16),
        ],
        compiler_params=pltpu.CompilerParams(
            dimension_semantics=("arbitrary", "arbitrary"), vmem_limit_bytes=VMEM_LIMIT),
        name="mixer",
    )(a, a, a, a, a, qvt, qvt, gt, x, conv_w, w_out.astype(BF16), norm_ffn[None, :], w_router2, b_router)


def _tile_major(pos, t):
    b_sz, _, s_len = pos.shape
    return pos.reshape(b_sz, 2, s_len // t, t).transpose(0, 2, 1, 3).reshape(-1)


def _dispatch_kernel(seg_ref, pos_ref, h_ref, xs_ref, zero_ref, sem, zsem):
    td = h_ref.shape[0]
    tm = zero_ref.shape[0]

    @pl.when(pl.program_id(0) == 0)
    def _():
        zero_ref[...] = jnp.zeros_like(zero_ref)

        def zero_copy(e):
            last_tile = pl.multiple_of(seg_ref[e + 1] - tm, tm)
            return pltpu.make_async_copy(zero_ref, xs_ref.at[pl.ds(last_tile, tm)], zsem)

        for e in range(N_EXPERTS):
            @pl.when(seg_ref[e + 1] > seg_ref[e])
            def _(e=e):
                zero_copy(e).start()
        for e in range(N_EXPERTS):
            @pl.when(seg_ref[e + 1] > seg_ref[e])
            def _(e=e):
                zero_copy(e).wait()

        def tail_copy(t):
            return pltpu.make_async_copy(zero_ref, xs_ref.at[pl.ds(pl.multiple_of(t * tm, tm), tm)], zsem)

        first_unused = seg_ref[N_EXPERTS] // tm
        n_tiles = xs_ref.shape[0] // tm
        lax.fori_loop(first_unused, n_tiles, lambda t, c: (tail_copy(t).start(), c)[1], 0)
        lax.fori_loop(first_unused, n_tiles, lambda t, c: (tail_copy(t).wait(), c)[1], 0)

    for r in range(td):
        src = h_ref.at[pl.ds(r, 1)]
        pltpu.make_async_copy(src, xs_ref.at[pl.ds(pos_ref[r], 1)], sem).start()
        pltpu.make_async_copy(src, xs_ref.at[pl.ds(pos_ref[td + r], 1)], sem).start()
    for _ in range(2):
        pltpu.make_async_copy(h_ref, xs_ref.at[pl.ds(0, td)], sem).wait()


def _dispatch(hn2, pos, seg, cap):
    b_sz, s_len, d = hn2.shape
    td = TD_DISPATCH
    return pl.pallas_call(
        _dispatch_kernel,
        grid_spec=pltpu.PrefetchScalarGridSpec(
            num_scalar_prefetch=1,
            grid=(b_sz * s_len // td,),
            in_specs=[
                pl.BlockSpec((2 * td,), lambda g, sg: (g,), memory_space=pltpu.SMEM),
                pl.BlockSpec((td, d), lambda g, sg: (g, 0)),
            ],
            out_specs=pl.BlockSpec(memory_space=pl.ANY),
            scratch_shapes=[pltpu.VMEM((TM_EXPERT, d), hn2.dtype),
                            pltpu.SemaphoreType.DMA(()), pltpu.SemaphoreType.DMA(())],
        ),
        out_shape=jax.ShapeDtypeStruct((cap, d), hn2.dtype),
        compiler_params=pltpu.CompilerParams(
            dimension_semantics=("arbitrary",), vmem_limit_bytes=VMEM_LIMIT),
        name="dispatch",
    )(seg, _tile_major(pos, td), hn2.reshape(b_sz * s_len, d))


def _experts_kernel(te_ref, nu_ref, xs_ref, wg_ref, wu_ref, wd_ref, y_ref, wgb_ref, wub_ref, wdb_ref):
    i = pl.program_id(0)
    e = te_ref[i]
    e_prev = te_ref[jnp.maximum(i - 1, 0)]
    new_expert = jnp.logical_or(i == 0, e != e_prev)
    used = i < nu_ref[0]

    @pl.when(jnp.logical_and(used, new_expert))
    def _():
        wgb_ref[...] = wg_ref[...].astype(BF16)
        wub_ref[...] = wu_ref[...].astype(BF16)
        wdb_ref[...] = wd_ref[...].astype(BF16)

    @pl.when(used)
    def _():
        x_lo, x_hi = _unpack_rows(xs_ref[...])
        xb = jnp.concatenate([x_lo.astype(BF16), x_hi.astype(BF16)], axis=1)
        gate = jnp.dot(xb, wgb_ref[...], preferred_element_type=F32)
        up = jnp.dot(xb, wub_ref[...], preferred_element_type=F32)
        hid = (gate * jax.nn.sigmoid(gate) * up).astype(BF16)
        y_ref[...] = _pack_rows(jnp.dot(hid, wdb_ref[...], preferred_element_type=F32))

    @pl.when(jnp.logical_not(used))
    def _():
        y_ref[...] = jnp.zeros_like(y_ref)


def _experts(xs, tile_e, n_used, w_gate, w_up, w_down):
    cap, dw = xs.shape
    d = 2 * dw
    tm = TM_EXPERT
    n_tiles = cap // tm

    def x_map(i, te, nu):
        return (jnp.minimum(i, jnp.maximum(nu[0] - 1, 0)), 0)

    def w_map(i, te, nu):
        return (te[i], 0, 0)

    return pl.pallas_call(
        _experts_kernel,
        grid_spec=pltpu.PrefetchScalarGridSpec(
            num_scalar_prefetch=2,
            grid=(n_tiles,),
            in_specs=[
                pl.BlockSpec((tm, dw), x_map),
                pl.BlockSpec((None, d, D_FF), w_map),
                pl.BlockSpec((None, d, D_FF), w_map),
                pl.BlockSpec((None, D_FF, d), w_map),
            ],
            out_specs=pl.BlockSpec((tm, dw), lambda i, te, nu: (i, 0)),
            scratch_shapes=[pltpu.VMEM((d, D_FF), BF16), pltpu.VMEM((d, D_FF), BF16),
                            pltpu.VMEM((D_FF, d), BF16)],
        ),
        out_shape=jax.ShapeDtypeStruct((cap, dw), U32),
        compiler_params=pltpu.CompilerParams(
            dimension_semantics=("arbitrary",), vmem_limit_bytes=VMEM_LIMIT),
        name="experts",
    )(tile_e, n_used, xs, w_gate, w_up, w_down)


def _combine_kernel(pos_ref, pos_next_ref, x2_ref, rw_ref, gf_ref, y_ref, out_ref, y1_ref, y2_ref, sem):
    tc = x2_ref.shape[0]
    step = pl.program_id(0)
    n_steps = pl.num_programs(0)
    slot = lax.rem(step, 2)

    def gather(p_ref, slot):
        for r in range(tc):
            pltpu.make_async_copy(y_ref.at[pl.ds(p_ref[r], 1)],
                                  y1_ref.at[slot, pl.ds(r, 1)], sem.at[slot]).start()
            pltpu.make_async_copy(y_ref.at[pl.ds(p_ref[tc + r], 1)],
                                  y2_ref.at[slot, pl.ds(r, 1)], sem.at[slot]).start()

    @pl.when(step == 0)
    def _():
        gather(pos_ref, 0)

    @pl.when(step + 1 < n_steps)
    def _():
        gather(pos_next_ref, 1 - slot)

    w_t = jnp.transpose(rw_ref[...])
    pltpu.make_async_copy(y_ref.at[pl.ds(0, tc)], y1_ref.at[slot], sem.at[slot]).wait()
    pltpu.make_async_copy(y_ref.at[pl.ds(0, tc)], y2_ref.at[slot], sem.at[slot]).wait()
    a_lo, a_hi = _unpack_rows(y1_ref[slot])
    b_lo, b_hi = _unpack_rows(y2_ref[slot])
    w1 = w_t[:, 0:1]
    w2 = w_t[:, 1:2]
    moe = jnp.concatenate([w1 * a_lo + w2 * b_lo, w1 * a_hi + w2 * b_hi], axis=1)
    x3 = x2_ref[...] + moe
    ms = jnp.mean(x3 * x3, axis=-1, keepdims=True)
    out_ref[...] = x3 * lax.rsqrt(ms + RMS_EPS) * gf_ref[...]


def _combine(x2, pos, rw, y, norm_final):
    b_sz, s_len, d = x2.shape
    tc = TC_COMBINE
    n_s = s_len // tc
    n_steps = b_sz * n_s
    pos_tiles = _tile_major(pos, tc)
    return pl.pallas_call(
        _combine_kernel,
        grid=(n_steps,),
        in_specs=[
            pl.BlockSpec((2 * tc,), lambda g: (g,), memory_space=pltpu.SMEM),
            pl.BlockSpec((2 * tc,), lambda g: (jnp.minimum(g + 1, n_steps - 1),), memory_space=pltpu.SMEM),
            pl.BlockSpec((tc, d), lambda g: (g, 0)),
            pl.BlockSpec((None, 2 * N_HEADS, tc), lambda g: (g // n_s, 0, g % n_s)),
            pl.BlockSpec((1, d), lambda g: (0, 0)),
            pl.BlockSpec(memory_space=pl.ANY),
        ],
        out_specs=pl.BlockSpec((tc, d), lambda g: (g, 0)),
        scratch_shapes=[pltpu.VMEM((2, tc, d // 2), U32), pltpu.VMEM((2, tc, d // 2), U32),
                        pltpu.SemaphoreType.DMA((2,))],
        out_shape=jax.ShapeDtypeStruct((b_sz * s_len, d), F32),
        compiler_params=pltpu.CompilerParams(
            dimension_semantics=("arbitrary",), vmem_limit_bytes=VMEM_LIMIT),
        name="combine",
    )(pos_tiles, pos_tiles, x2.reshape(b_sz * s_len, d), rw, norm_final[None, :], y).reshape(b_sz, s_len, d)


def _layer(x, norm_mix, w_in, b_in, conv_w, w_out, norm_ffn, w_group, b_group, w_expert, b_expert,
           w_gate, w_up, w_down, norm_out):
    b_sz, s_len, d = x.shape
    n_tok = b_sz * s_len
    a, qvt, gt = _inproj(x, norm_mix, w_in, b_in)

    w_router = jnp.zeros((d, ROUTER_COLS), F32)
    w_router = w_router.at[:, 0:N_GROUPS].set(w_group).at[:, EXPERT_COL0:EXPERT_COL0 + N_EXPERTS].set(w_expert)
    b_router = jnp.zeros((1, ROUTER_COLS), F32)
    b_router = b_router.at[0, 0:N_GROUPS].set(b_group).at[0, EXPERT_COL0:EXPERT_COL0 + N_EXPERTS].set(b_expert)
    x2, hn2, ri, rw, counts = _mixer(x, a, qvt, gt, conv_w, w_out, norm_ffn, w_router, b_router)

    tm = TM_EXPERT
    cnt = counts[:, 0]
    padded = ((cnt + tm - 1) // tm) * tm
    pends = jnp.cumsum(padded).astype(I32)
    seg = jnp.concatenate([jnp.zeros((1,), I32), pends])
    n_tiles = (n_tok * 2) // tm + N_EXPERTS
    cap = n_tiles * tm
    tile_start = jnp.arange(n_tiles, dtype=I32) * tm
    tile_e = jnp.minimum(jnp.sum((tile_start[:, None] >= pends[None, :]).astype(I32), axis=1),
                         N_EXPERTS - 1)
    n_used = pends[-1:] // tm
    e_sel = ri[:, 0:2, :]
    seg_start = jnp.sum(jnp.where(e_sel[..., None] == jnp.arange(N_EXPERTS, dtype=I32), seg[:-1], 0), axis=-1)
    pos = ri[:, 2:4, :] + seg_start

    xs = _dispatch(hn2, pos, seg, cap)
    y = _experts(xs, tile_e, n_used, w_gate, w_up, w_down)
    return _combine(x2, pos, rw, y, norm_out)


def kernel(x, norm_mix, w_in, b_in, conv_w, w_out, norm_ffn, w_group, b_group, w_expert, b_expert,
           w_gate, w_up, w_down, norm_final):
    depth = norm_mix.shape[0]
    assert depth == 1, "final-norm fusion below assumes a single layer"
    assert x.shape[-1] == 2 * D_MLSTM and x.shape[1] % TS_MIXER == 0
    return _layer(x, norm_mix[0], w_in[0], b_in[0], conv_w[0], w_out[0], norm_ffn[0],
                  w_group[0], b_group[0], w_expert[0], b_expert[0],
                  w_gate[0], w_up[0], w_down[0], norm_final)
```

```python
import jax
import jax.numpy as jnp
from jax import lax
from jax.experimental import pallas as pl
from jax.experimental.pallas import tpu as pltpu
from jax.experimental.pallas import tpu_sc as plsc

F32 = jnp.float32
BF16 = jnp.bfloat16
I32 = jnp.int32

N_HEADS = 4
HEAD_DIM = 128
D_MLSTM = N_HEADS * HEAD_DIM
D_CONV = 512
N_GROUPS = 4
EXPERTS_PER_GROUP = 8
N_EXPERTS = N_GROUPS * EXPERTS_PER_GROUP
D_FF = 512
RMS_EPS = 1e-6
Q_SCALE = HEAD_DIM ** -0.5

LANES = 128
CHUNK = LANES
AUG_ROWS = 16
ROUTER_COLS = 128
EXPERT_COL0 = 8

TM_INPROJ = 512
TS_MIXER = 512
TD_DISPATCH = 512
OUT_COLS = 256
TM_EXPERT = 512
TC_COMBINE = 512

VMEM_LIMIT = 56 * 1024 * 1024


U32 = jnp.uint32
HI_HALF = 0xFFFF0000


def _pack_rows(x):
    w = x.shape[1] // 2
    lo = lax.bitcast_convert_type(x[:, :w].astype(BF16).astype(F32), U32) >> 16
    hi = lax.bitcast_convert_type(x[:, w:].astype(BF16).astype(F32), U32) & jnp.uint32(HI_HALF)
    return lo | hi


def _unpack_rows(words):
    lo = lax.bitcast_convert_type(words << 16, F32)
    hi = lax.bitcast_convert_type(words & jnp.uint32(HI_HALF), F32)
    return lo, hi


def _nt_dot(a, b):
    return lax.dot_general(a, b, (((1,), (1,)), ((), ())), preferred_element_type=F32)


def _inproj_kernel(x_ref, g_ref, wa_ref, ba_ref, wqv_ref, bqv_ref, wg_ref, bg_ref,
                   a_ref, qvt_ref, gt_ref):
    x = x_ref[...]
    ms = jnp.mean(x * x, axis=-1, keepdims=True)
    hn = (x * lax.rsqrt(ms + RMS_EPS) * g_ref[...]).astype(BF16)
    n_blk = wa_ref.shape[1] // D_MLSTM
    for j in range(n_blk):
        sl = slice(j * D_MLSTM, (j + 1) * D_MLSTM)
        acc = jnp.dot(hn, wa_ref[:, sl], preferred_element_type=F32) + ba_ref[:, sl]
        a_ref[:, sl] = acc.astype(BF16)
    q_rows = slice(0, D_MLSTM)
    v_rows = slice(D_MLSTM, 2 * D_MLSTM)
    qt = (_nt_dot(wqv_ref[q_rows, :], hn) + bqv_ref[q_rows, :]) * Q_SCALE
    qvt_ref[q_rows, :] = qt.astype(BF16)
    qvt_ref[v_rows, :] = (_nt_dot(wqv_ref[v_rows, :], hn) + bqv_ref[v_rows, :]).astype(BF16)
    gt_ref[...] = _nt_dot(wg_ref[...], hn) + bg_ref[...]


def _inproj(x, norm_mix, w_in, b_in):
    b_sz, s_len, d = x.shape
    tm = TM_INPROJ
    dm = D_MLSTM
    g0 = 4 * dm
    c0 = g0 + 2 * N_HEADS
    cols_a = jnp.concatenate([w_in[:, dm:2 * dm], w_in[:, 3 * dm:4 * dm], w_in[:, c0:c0 + 3 * D_CONV]], axis=1)
    bias_a = jnp.concatenate([b_in[dm:2 * dm], b_in[3 * dm:4 * dm], b_in[c0:c0 + 3 * D_CONV]])[None, :]
    wqv_t = jnp.concatenate([w_in[:, 0:dm], w_in[:, 2 * dm:3 * dm]], axis=1).T
    bqv = jnp.concatenate([b_in[0:dm], b_in[2 * dm:3 * dm]])[:, None]
    wg_t = w_in[:, g0:c0].T
    bg = b_in[g0:c0][:, None]
    n_a = cols_a.shape[1]
    return pl.pallas_call(
        _inproj_kernel,
        grid=(b_sz, s_len // tm),
        in_specs=[
            pl.BlockSpec((None, tm, d), lambda b, s: (b, s, 0)),
            pl.BlockSpec((1, d), lambda b, s: (0, 0)),
            pl.BlockSpec((d, n_a), lambda b, s: (0, 0)),
            pl.BlockSpec((1, n_a), lambda b, s: (0, 0)),
            pl.BlockSpec((2 * dm, d), lambda b, s: (0, 0)),
            pl.BlockSpec((2 * dm, 1), lambda b, s: (0, 0)),
            pl.BlockSpec((2 * N_HEADS, d), lambda b, s: (0, 0)),
            pl.BlockSpec((2 * N_HEADS, 1), lambda b, s: (0, 0)),
        ],
        out_specs=[
            pl.BlockSpec((None, tm, n_a), lambda b, s: (b, s, 0)),
            pl.BlockSpec((None, 2 * dm, tm), lambda b, s: (b, 0, s)),
            pl.BlockSpec((None, 2 * N_HEADS, tm), lambda b, s: (b, 0, s)),
        ],
        out_shape=[
            jax.ShapeDtypeStruct((b_sz, s_len, n_a), BF16),
            jax.ShapeDtypeStruct((b_sz, 2 * dm, s_len), BF16),
            jax.ShapeDtypeStruct((b_sz, 2 * N_HEADS, s_len), F32),
        ],
        compiler_params=pltpu.CompilerParams(
            dimension_semantics=("arbitrary", "arbitrary"), vmem_limit_bytes=VMEM_LIMIT),
        name="inproj",
    )(x, norm_mix[None, :], cols_a.astype(BF16), bias_a, wqv_t.astype(BF16), bqv, wg_t.astype(BF16), bg)


def _mixer_kernel(k_ref, o_ref, cb_ref, cc_ref, cx_ref, qt_ref, vt_ref, gt_ref, x_ref,
                  cw_ref, wo_ref, g2_ref, wr_ref, br_ref,
                  x2_ref, hn2_ref, ri_ref, rw_ref, cnt_out_ref,
                  ct_ref, m_ref, carry_ref, cnt_ref, mix_ref):
    ts = x_ref.shape[0]
    n_chunks = ts // CHUNK
    first_tile = pl.program_id(1) == 0

    @pl.when(first_tile)
    def _():
        ct_ref[...] = jnp.zeros_like(ct_ref)
        m_ref[...] = jnp.zeros_like(m_ref)
        carry_ref[...] = jnp.zeros_like(carry_ref)

    @pl.when(jnp.logical_and(first_tile, pl.program_id(0) == 0))
    def _():
        cnt_ref[...] = jnp.zeros_like(cnt_ref)

    gates = gt_ref[...]
    li_all = gates[0:N_HEADS]
    fg = gates[N_HEADS:2 * N_HEADS]
    lf_all = jnp.minimum(fg, 0.0) - jnp.log1p(jnp.exp(-jnp.abs(fg)))
    key_j = lax.broadcasted_iota(I32, (CHUNK, CHUNK), 0)
    qry_i = lax.broadcasted_iota(I32, (CHUNK, CHUNK), 1)
    causal_t = key_j <= qry_i
    neg_inf = jnp.float32(-jnp.inf)
    ones_rows = (lax.broadcasted_iota(I32, (AUG_ROWS, CHUNK), 0) == 0).astype(BF16)

    prefix_ones = (key_j <= qry_i).astype(BF16)

    lf_hi = lf_all.astype(BF16)
    rem = lf_all - lf_hi.astype(F32)
    lf_mid = rem.astype(BF16)
    lf_lo = (rem - lf_mid.astype(F32)).astype(BF16)
    zero_bf = jnp.zeros((N_HEADS, CHUNK), BF16)
    parts = []
    for c in range(n_chunks):
        rows = slice(c * CHUNK, (c + 1) * CHUNK)
        parts += [lf_hi[:, rows], lf_mid[:, rows], lf_lo[:, rows], zero_bf]
    cum = jnp.dot(jnp.concatenate(parts, axis=0), prefix_ones, preferred_element_type=F32)

    m_prev = m_ref[0:N_HEADS, 0:1]
    for c in range(n_chunks):
        rows = slice(c * CHUNK, (c + 1) * CHUNK)
        li = li_all[:, rows]
        r0 = 4 * N_HEADS * c
        b_cum = (cum[r0:r0 + N_HEADS] + cum[r0 + N_HEADS:r0 + 2 * N_HEADS]
                 + cum[r0 + 2 * N_HEADS:r0 + 3 * N_HEADS])
        g_tot = b_cum[:, CHUNK - 1:CHUNK]
        a_end = g_tot - b_cum + li
        m_loc = jnp.max(a_end, axis=1, keepdims=True)
        u = li - b_cum
        m_new = jnp.maximum(g_tot + m_prev, m_loc)
        decay = jnp.exp(g_tot + m_prev - m_new)
        s_fac = jnp.exp(m_loc - m_new)
        w_key = jnp.exp(a_end - m_loc)
        u_cols = jnp.transpose(jnp.concatenate([u, jnp.zeros_like(u)], axis=0))
        for h in range(N_HEADS):
            hc = slice(h * HEAD_DIM, (h + 1) * HEAD_DIM)
            kh = k_ref[rows, hc]
            qth = qt_ref[hc, rows]
            vt_aug = jnp.concatenate([vt_ref[hc, rows], ones_rows], axis=0)
            s_t = jnp.dot(kh, qth, preferred_element_type=F32)
            u_t = jnp.where(causal_t, u_cols[:, h:h + 1], neg_inf)
            mp = m_prev[h:h + 1, :]
            m_tot = jnp.maximum(jnp.max(u_t, axis=0, keepdims=True), mp)
            f_inter = jnp.exp(mp - m_tot)
            floor = jnp.exp(-b_cum[h:h + 1, :] - m_tot)
            p_t = (s_t * jnp.exp(u_t - m_tot)).astype(BF16)
            q_f = (qth.astype(F32) * f_inter).astype(BF16)
            c_old = ct_ref[h]
            lhs = jnp.concatenate([vt_aug, c_old.astype(BF16)], axis=1)
            rhs = jnp.concatenate([p_t, q_f], axis=0)
            r_t = jnp.dot(lhs, rhs, preferred_element_type=F32)
            vtw = (vt_aug.astype(F32) * w_key[h:h + 1, :]).astype(BF16)
            kv = jnp.dot(vtw, kh, preferred_element_type=F32)
            ct_ref[h] = decay[h:h + 1, :] * c_old + s_fac[h:h + 1, :] * kv

            den = r_t[HEAD_DIM:HEAD_DIM + 1, :]
            inv = 1.0 / jnp.maximum(jnp.abs(den), floor)
            h_nat = jnp.transpose(r_t[0:HEAD_DIM, :] * inv)
            o_gate = jax.nn.sigmoid(o_ref[rows, hc].astype(F32))
            mix_ref[rows, hc] = (o_gate * h_nat).astype(BF16)
        m_prev = m_new
    m_ref[0:N_HEADS, :] = jnp.broadcast_to(m_prev, (N_HEADS, LANES))

    u_c = cc_ref[...].astype(F32) * cx_ref[...].astype(F32)
    rid = lax.broadcasted_iota(I32, u_c.shape, 0)
    prev1 = carry_ref[0:1, :]
    prev2 = carry_ref[1:2, :]
    u1 = jnp.where(rid == 0, prev1, pltpu.roll(u_c, 1, axis=0))
    u2 = jnp.where(rid == 0, prev2, jnp.where(rid == 1, prev1, pltpu.roll(u_c, 2, axis=0)))
    z = cw_ref[0:1, :] * u2 + cw_ref[1:2, :] * u1 + cw_ref[2:3, :] * u_c
    mix_ref[:, D_MLSTM:] = (cb_ref[...].astype(F32) * z).astype(BF16)
    carry_ref[0:1, :] = u_c[ts - 1:ts, :]
    carry_ref[1:2, :] = u_c[ts - 2:ts - 1, :]

    d_model = x_ref.shape[1]
    mix = mix_ref[...]
    ssq = jnp.zeros((ts, 1), F32)
    for j in range(d_model // OUT_COLS):
        cs = slice(j * OUT_COLS, (j + 1) * OUT_COLS)
        x2c = x_ref[:, cs] + jnp.dot(mix, wo_ref[:, cs], preferred_element_type=F32)
        x2_ref[:, cs] = x2c
        ssq = ssq + jnp.sum(x2c * x2c, axis=-1, keepdims=True)
    hn2 = x2_ref[...] * lax.rsqrt(ssq * (1.0 / d_model) + RMS_EPS) * g2_ref[...]
    hn2_ref[...] = _pack_rows(hn2)

    h_hi = hn2.astype(BF16)
    h_lo = (hn2 - h_hi.astype(F32)).astype(BF16)
    hi_both = jnp.dot(h_hi, wr_ref[...], preferred_element_type=F32)
    lo_hi = jnp.dot(h_lo, wr_ref[:, 0:ROUTER_COLS], preferred_element_type=F32)
    logits = hi_both[:, 0:ROUTER_COLS] + (hi_both[:, ROUTER_COLS:] + lo_hi) + br_ref[...]
    lt = jnp.transpose(logits)
    gl = lt[0:N_GROUPS]
    gmax = jnp.max(gl, axis=0, keepdims=True)
    gi = lax.broadcasted_iota(I32, gl.shape, 0)
    g_sel = jnp.min(jnp.where(gl == gmax, gi, N_GROUPS), axis=0, keepdims=True)
    p_g = 1.0 / jnp.sum(jnp.exp(gl - gmax), axis=0, keepdims=True)
    in_g = lt[EXPERT_COL0:EXPERT_COL0 + EXPERTS_PER_GROUP]
    for g in range(1, N_GROUPS):
        lo = EXPERT_COL0 + g * EXPERTS_PER_GROUP
        in_g = jnp.where(g_sel == g, lt[lo:lo + EXPERTS_PER_GROUP], in_g)
    ei = lax.broadcasted_iota(I32, in_g.shape, 0)
    v1 = jnp.max(in_g, axis=0, keepdims=True)
    i1 = jnp.min(jnp.where(in_g == v1, ei, EXPERTS_PER_GROUP), axis=0, keepdims=True)
    rest = jnp.where(ei == i1, neg_inf, in_g)
    v2 = jnp.max(rest, axis=0, keepdims=True)
    i2 = jnp.min(jnp.where(rest == v2, ei, EXPERTS_PER_GROUP), axis=0, keepdims=True)
    t = jnp.exp(v2 - v1)
    w1 = p_g / (1.0 + t)
    w2 = p_g * t / (1.0 + t)
    e1 = g_sel * EXPERTS_PER_GROUP + i1
    e2 = g_sel * EXPERTS_PER_GROUP + i2

    eio = lax.broadcasted_iota(I32, (N_EXPERTS, LANES), 0)
    base = cnt_ref[:, 0:1]
    zero_i = jnp.zeros((1, LANES), I32)
    n_blk = ts // LANES
    hits = []
    for k in range(n_blk):
        ls = slice(k * LANES, (k + 1) * LANES)
        hits.append((eio == e1[:, ls], eio == e2[:, ls]))
    onehots = [jnp.where(jnp.logical_or(a1, a2), 1.0, 0.0) for a1, a2 in hits]
    incl_all = jnp.dot(jnp.concatenate(onehots, axis=0).astype(BF16), prefix_ones,
                       preferred_element_type=F32).astype(I32)
    for k in range(n_blk):
        ls = slice(k * LANES, (k + 1) * LANES)
        is1, is2 = hits[k]
        onehot = onehots[k].astype(I32)
        incl = incl_all[k * N_EXPERTS:(k + 1) * N_EXPERTS]
        rank_e = base + incl - onehot
        r1 = jnp.sum(jnp.where(is1, rank_e, 0), axis=0, keepdims=True)
        r2 = jnp.sum(jnp.where(is2, rank_e, 0), axis=0, keepdims=True)
        ri_ref[:, ls] = jnp.concatenate(
            [e1[:, ls], e2[:, ls], r1, r2, zero_i, zero_i, zero_i, zero_i], axis=0)
        base = base + incl[:, LANES - 1:LANES]
    zero_f = jnp.zeros_like(w1)
    rw_ref[...] = jnp.concatenate([w1, w2, zero_f, zero_f, zero_f, zero_f, zero_f, zero_f], axis=0)
    cnt_new = jnp.broadcast_to(base, (N_EXPERTS, LANES))
    cnt_ref[...] = cnt_new
    cnt_out_ref[...] = cnt_new


def _mixer(x, a, qvt, gt, conv_w, w_out, norm_ffn, w_router, b_router):
    b_sz, s_len, d = x.shape
    ts = TS_MIXER
    dm = D_MLSTM

    def a_spec(j):
        return pl.BlockSpec((None, ts, dm), lambda b, s, j=j: (b, s, j))

    tok = lambda w: pl.BlockSpec((None, ts, w), lambda b, s: (b, s, 0))
    rowsp = pl.BlockSpec((None, 2 * N_HEADS, ts), lambda b, s: (b, 0, s))
    const = lambda shape: pl.BlockSpec(shape, lambda b, s: tuple(0 for _ in shape))
    wr_hi = w_router.astype(BF16)
    wr_lo = (w_router - wr_hi.astype(F32)).astype(BF16)
    w_router2 = jnp.concatenate([wr_hi, wr_lo], axis=1)
    return pl.pallas_call(
        _mixer_kernel,
        grid=(b_sz, s_len // ts),
        in_specs=[a_spec(0), a_spec(1), a_spec(2), a_spec(3), a_spec(4),
                  pl.BlockSpec((None, dm, ts), lambda b, s: (b, 0, s)),
                  pl.BlockSpec((None, dm, ts), lambda b, s: (b, 1, s)),
                  rowsp, tok(d),
                  const((3, D_CONV)), const((d, d)), const((1, d)),
                  const((d, 2 * ROUTER_COLS)), const((1, ROUTER_COLS))],
        out_specs=[tok(d), tok(d // 2), rowsp, rowsp, const((N_EXPERTS, LANES))],
        out_shape=[
            jax.ShapeDtypeStruct((b_sz, s_len, d), F32),
            jax.ShapeDtypeStruct((b_sz, s_len, d // 2), U32),
            jax.ShapeDtypeStruct((b_sz, 2 * N_HEADS, s_len), I32),
            jax.ShapeDtypeStruct((b_sz, 2 * N_HEADS, s_len), F32),
            jax.ShapeDtypeStruct((N_EXPERTS, LANES), I32),
        ],
        scratch_shapes=[
            pltpu.VMEM((N_HEADS, HEAD_DIM + AUG_ROWS, HEAD_DIM), F32),
            pltpu.VMEM((2 * N_HEADS, LANES), F32),
            pltpu.VMEM((8, D_CONV), F32),
            pltpu.VMEM((N_EXPERTS, LANES), I32),
            pltpu.VMEM((ts, d), BF16),
        ],
        compiler_params=pltpu.CompilerParams(
            dimension_semantics=("arbitrary", "arbitrary"), vmem_limit_bytes=VMEM_LIMIT),
        name="mixer",
    )(a, a, a, a, a, qvt, qvt, gt, x, conv_w, w_out.astype(BF16), norm_ffn[None, :], w_router2, b_router)


def _tile_major(pos, t):
    b_sz, _, s_len = pos.shape
    return pos.reshape(b_sz, 2, s_len // t, t).transpose(0, 2, 1, 3).reshape(-1)


def _dispatch_kernel(seg_ref, pos_ref, h_ref, xs_ref, zero_ref, sem, zsem):
    td = h_ref.shape[0]
    tm = zero_ref.shape[0]

    @pl.when(pl.program_id(0) == 0)
    def _():
        zero_ref[...] = jnp.zeros_like(zero_ref)

        def zero_copy(e):
            last_tile = pl.multiple_of(seg_ref[e + 1] - tm, tm)
            return pltpu.make_async_copy(zero_ref, xs_ref.at[pl.ds(last_tile, tm)], zsem)

        for e in range(N_EXPERTS):
            @pl.when(seg_ref[e + 1] > seg_ref[e])
            def _(e=e):
                zero_copy(e).start()
        for e in range(N_EXPERTS):
            @pl.when(seg_ref[e + 1] > seg_ref[e])
            def _(e=e):
                zero_copy(e).wait()

        def tail_copy(t):
            return pltpu.make_async_copy(zero_ref, xs_ref.at[pl.ds(pl.multiple_of(t * tm, tm), tm)], zsem)

        first_unused = seg_ref[N_EXPERTS] // tm
        n_tiles = xs_ref.shape[0] // tm
        lax.fori_loop(first_unused, n_tiles, lambda t, c: (tail_copy(t).start(), c)[1], 0)
        lax.fori_loop(first_unused, n_tiles, lambda t, c: (tail_copy(t).wait(), c)[1], 0)

    for r in range(td):
        src = h_ref.at[pl.ds(r, 1)]
        pltpu.make_async_copy(src, xs_ref.at[pl.ds(pos_ref[r], 1)], sem).start()
        pltpu.make_async_copy(src, xs_ref.at[pl.ds(pos_ref[td + r], 1)], sem).start()
    for _ in range(2):
        pltpu.make_async_copy(h_ref, xs_ref.at[pl.ds(0, td)], sem).wait()


def _dispatch(hn2, pos, seg, cap):
    b_sz, s_len, d = hn2.shape
    td = TD_DISPATCH
    return pl.pallas_call(
        _dispatch_kernel,
        grid_spec=pltpu.PrefetchScalarGridSpec(
            num_scalar_prefetch=1,
            grid=(b_sz * s_len // td,),
            in_specs=[
                pl.BlockSpec((2 * td,), lambda g, sg: (g,), memory_space=pltpu.SMEM),
                pl.BlockSpec((td, d), lambda g, sg: (g, 0)),
            ],
            out_specs=pl.BlockSpec(memory_space=pl.ANY),
            scratch_shapes=[pltpu.VMEM((TM_EXPERT, d), hn2.dtype),
                            pltpu.SemaphoreType.DMA(()), pltpu.SemaphoreType.DMA(())],
        ),
        out_shape=jax.ShapeDtypeStruct((cap, d), hn2.dtype),
        compiler_params=pltpu.CompilerParams(
            dimension_semantics=("arbitrary",), vmem_limit_bytes=VMEM_LIMIT),
        name="dispatch",
    )(seg, _tile_major(pos, td), hn2.reshape(b_sz * s_len, d))


def _experts_kernel(te_ref, nu_ref, xs_ref, wg_ref, wu_ref, wd_ref, y_ref, wgb_ref, wub_ref, wdb_ref):
    i = pl.program_id(0)
    e = te_ref[i]
    e_prev = te_ref[jnp.maximum(i - 1, 0)]
    new_expert = jnp.logical_or(i == 0, e != e_prev)
    used = i < nu_ref[0]

    @pl.when(jnp.logical_and(used, new_expert))
    def _():
        wgb_ref[...] = wg_ref[...].astype(BF16)
        wub_ref[...] = wu_ref[...].astype(BF16)
        wdb_ref[...] = wd_ref[...].astype(BF16)

    @pl.when(used)
    def _():
        x_lo, x_hi = _unpack_rows(xs_ref[...])
        xb = jnp.concatenate([x_lo.astype(BF16), x_hi.astype(BF16)], axis=1)
        gate = jnp.dot(xb, wgb_ref[...], preferred_element_type=F32)
        up = jnp.dot(xb, wub_ref[...], preferred_element_type=F32)
        hid = (gate * jax.nn.sigmoid(gate) * up).astype(BF16)
        y_ref[...] = _pack_rows(jnp.dot(hid, wdb_ref[...], preferred_element_type=F32))

    @pl.when(jnp.logical_not(used))
    def _():
        y_ref[...] = jnp.zeros_like(y_ref)


def _experts(xs, tile_e, n_used, w_gate, w_up, w_down):
    cap, dw = xs.shape
    d = 2 * dw
    tm = TM_EXPERT
    n_tiles = cap // tm

    def x_map(i, te, nu):
        return (jnp.minimum(i, jnp.maximum(nu[0] - 1, 0)), 0)

    def w_map(i, te, nu):
        return (te[i], 0, 0)

    return pl.pallas_call(
        _experts_kernel,
        grid_spec=pltpu.PrefetchScalarGridSpec(
            num_scalar_prefetch=2,
            grid=(n_tiles,),
            in_specs=[
                pl.BlockSpec((tm, dw), x_map),
                pl.BlockSpec((None, d, D_FF), w_map),
                pl.BlockSpec((None, d, D_FF), w_map),
                pl.BlockSpec((None, D_FF, d), w_map),
            ],
            out_specs=pl.BlockSpec((tm, dw), lambda i, te, nu: (i, 0)),
            scratch_shapes=[pltpu.VMEM((d, D_FF), BF16), pltpu.VMEM((d, D_FF), BF16),
                            pltpu.VMEM((D_FF, d), BF16)],
        ),
        out_shape=jax.ShapeDtypeStruct((cap, dw), U32),
        compiler_params=pltpu.CompilerParams(
            dimension_semantics=("arbitrary",), vmem_limit_bytes=VMEM_LIMIT),
        name="experts",
    )(tile_e, n_used, xs, w_gate, w_up, w_down)


SC_WORKERS = 32
SC_CHUNK = 128


def _sc_gather_rows(table, idx):
    n_rows = idx.shape[0]
    width = table.shape[1]
    per_worker = n_rows // SC_WORKERS
    n_chunks = per_worker // SC_CHUNK
    mesh = plsc.VectorSubcoreMesh(core_axis_name="c", subcore_axis_name="s")

    def body(table_hbm, idx_hbm, out_hbm, idx_v, rows_v, sem):
        wid = lax.axis_index("s") * 2 + lax.axis_index("c")

        @pl.loop(0, n_chunks)
        def _(j):
            base = wid * per_worker + j * SC_CHUNK
            pltpu.sync_copy(idx_hbm.at[pl.ds(base, SC_CHUNK)], idx_v)
            pltpu.async_copy(table_hbm.at[idx_v], rows_v, sem).wait()
            pltpu.sync_copy(rows_v, out_hbm.at[pl.ds(base, SC_CHUNK)])

    return pl.kernel(
        body,
        out_type=jax.ShapeDtypeStruct((n_rows, width), I32),
        mesh=mesh,
        scratch_types=[pltpu.VMEM((SC_CHUNK,), I32), pltpu.VMEM((SC_CHUNK, width), I32),
                       pltpu.SemaphoreType.DMA],
        name="sc_gather",
    )(table, idx)


def _combine_kernel(pos_ref, pos_next_ref, x2_ref, rw_ref, gf_ref, y_ref, out_ref, y1_ref, y2_ref, sem):
    tc = x2_ref.shape[0]
    step = pl.program_id(0)
    n_steps = pl.num_programs(0)
    slot = lax.rem(step, 2)

    def gather(p_ref, slot):
        for r in range(tc):
            pltpu.make_async_copy(y_ref.at[pl.ds(p_ref[r], 1)],
                                  y1_ref.at[slot, pl.ds(r, 1)], sem.at[slot]).start()
            pltpu.make_async_copy(y_ref.at[pl.ds(p_ref[tc + r], 1)],
                                  y2_ref.at[slot, pl.ds(r, 1)], sem.at[slot]).start()

    @pl.when(step == 0)
    def _():
        gather(pos_ref, 0)

    @pl.when(step + 1 < n_steps)
    def _():
        gather(pos_next_ref, 1 - slot)

    w_t = jnp.transpose(rw_ref[...])
    pltpu.make_async_copy(y_ref.at[pl.ds(0, tc)], y1_ref.at[slot], sem.at[slot]).wait()
    pltpu.make_async_copy(y_ref.at[pl.ds(0, tc)], y2_ref.at[slot], sem.at[slot]).wait()
    a_lo, a_hi = _unpack_rows(y1_ref[slot])
    b_lo, b_hi = _unpack_rows(y2_ref[slot])
    w1 = w_t[:, 0:1]
    w2 = w_t[:, 1:2]
    moe = jnp.concatenate([w1 * a_lo + w2 * b_lo, w1 * a_hi + w2 * b_hi], axis=1)
    x3 = x2_ref[...] + moe
    ms = jnp.mean(x3 * x3, axis=-1, keepdims=True)
    out_ref[...] = x3 * lax.rsqrt(ms + RMS_EPS) * gf_ref[...]


def _combine(x2, pos, rw, y, norm_final):
    b_sz, s_len, d = x2.shape
    tc = TC_COMBINE
    n_s = s_len // tc
    n_steps = b_sz * n_s
    pos_tiles = _tile_major(pos, tc)
    return pl.pallas_call(
        _combine_kernel,
        grid=(n_steps,),
        in_specs=[
            pl.BlockSpec((2 * tc,), lambda g: (g,), memory_space=pltpu.SMEM),
            pl.BlockSpec((2 * tc,), lambda g: (jnp.minimum(g + 1, n_steps - 1),), memory_space=pltpu.SMEM),
            pl.BlockSpec((tc, d), lambda g: (g, 0)),
            pl.BlockSpec((None, 2 * N_HEADS, tc), lambda g: (g // n_s, 0, g % n_s)),
            pl.BlockSpec((1, d), lambda g: (0, 0)),
            pl.BlockSpec(memory_space=pl.ANY),
        ],
        out_specs=pl.BlockSpec((tc, d), lambda g: (g, 0)),
        scratch_shapes=[pltpu.VMEM((2, tc, d // 2), U32), pltpu.VMEM((2, tc, d // 2), U32),
                        pltpu.SemaphoreType.DMA((2,))],
        out_shape=jax.ShapeDtypeStruct((b_sz * s_len, d), F32),
        compiler_params=pltpu.CompilerParams(
            dimension_semantics=("arbitrary",), vmem_limit_bytes=VMEM_LIMIT),
        name="combine",
    )(pos_tiles, pos_tiles, x2.reshape(b_sz * s_len, d), rw, norm_final[None, :], y).reshape(b_sz, s_len, d)


def _combine_stream_kernel(x2_ref, y1_ref, y2_ref, rw_ref, gf_ref, out_ref):
    w_t = jnp.transpose(rw_ref[...])
    a_lo, a_hi = _unpack_rows(y1_ref[...])
    b_lo, b_hi = _unpack_rows(y2_ref[...])
    w1 = w_t[:, 0:1]
    w2 = w_t[:, 1:2]
    moe = jnp.concatenate([w1 * a_lo + w2 * b_lo, w1 * a_hi + w2 * b_hi], axis=1)
    x3 = x2_ref[...] + moe
    ms = jnp.mean(x3 * x3, axis=-1, keepdims=True)
    out_ref[...] = x3 * lax.rsqrt(ms + RMS_EPS) * gf_ref[...]


def _combine_stream(x2, yg, rw, norm_final):
    b_sz, s_len, d = x2.shape
    tc = TC_COMBINE
    n_s = s_len // tc
    n_steps = b_sz * n_s
    return pl.pallas_call(
        _combine_stream_kernel,
        grid=(n_steps,),
        in_specs=[
            pl.BlockSpec((tc, d), lambda g: (g, 0)),
            pl.BlockSpec((tc, d // 2), lambda g: (g, 0)),
            pl.BlockSpec((tc, d // 2), lambda g: (g + n_steps, 0)),
            pl.BlockSpec((None, 2 * N_HEADS, tc), lambda g: (g // n_s, 0, g % n_s)),
            pl.BlockSpec((1, d), lambda g: (0, 0)),
        ],
        out_specs=pl.BlockSpec((tc, d), lambda g: (g, 0)),
        out_shape=jax.ShapeDtypeStruct((b_sz * s_len, d), F32),
        compiler_params=pltpu.CompilerParams(
            dimension_semantics=("arbitrary",), vmem_limit_bytes=VMEM_LIMIT),
        name="combine_stream",
    )(x2.reshape(b_sz * s_len, d), yg, yg, rw, norm_final[None, :]).reshape(b_sz, s_len, d)


def _layer(x, norm_mix, w_in, b_in, conv_w, w_out, norm_ffn, w_group, b_group, w_expert, b_expert,
           w_gate, w_up, w_down, norm_out):
    b_sz, s_len, d = x.shape
    n_tok = b_sz * s_len
    a, qvt, gt = _inproj(x, norm_mix, w_in, b_in)

    w_router = jnp.zeros((d, ROUTER_COLS), F32)
    w_router = w_router.at[:, 0:N_GROUPS].set(w_group).at[:, EXPERT_COL0:EXPERT_COL0 + N_EXPERTS].set(w_expert)
    b_router = jnp.zeros((1, ROUTER_COLS), F32)
    b_router = b_router.at[0, 0:N_GROUPS].set(b_group).at[0, EXPERT_COL0:EXPERT_COL0 + N_EXPERTS].set(b_expert)
    x2, hn2, ri, rw, counts = _mixer(x, a, qvt, gt, conv_w, w_out, norm_ffn, w_router, b_router)

    tm = TM_EXPERT
    cnt = counts[:, 0]
    padded = ((cnt + tm - 1) // tm) * tm
    pends = jnp.cumsum(padded).astype(I32)
    seg = jnp.concatenate([jnp.zeros((1,), I32), pends])
    n_tiles = (n_tok * 2) // tm + N_EXPERTS
    cap = n_tiles * tm
    tile_start = jnp.arange(n_tiles, dtype=I32) * tm
    tile_e = jnp.minimum(jnp.sum((tile_start[:, None] >= pends[None, :]).astype(I32), axis=1),
                         N_EXPERTS - 1)
    n_used = pends[-1:] // tm
    e_sel = ri[:, 0:2, :]
    seg_start = jnp.sum(jnp.where(e_sel[..., None] == jnp.arange(N_EXPERTS, dtype=I32), seg[:-1], 0), axis=-1)
    pos = ri[:, 2:4, :] + seg_start

    xs = _dispatch(hn2, pos, seg, cap)
    y = _experts(xs, tile_e, n_used, w_gate, w_up, w_down)
    pos_flat = jnp.concatenate([pos[:, 0, :].reshape(-1), pos[:, 1, :].reshape(-1)])
    yg = _sc_gather_rows(lax.bitcast_convert_type(y, I32), pos_flat)
    return _combine_stream(x2, lax.bitcast_convert_type(yg, U32), rw, norm_out)


def kernel(x, norm_mix, w_in, b_in, conv_w, w_out, norm_ffn, w_group, b_group, w_expert, b_expert,
           w_gate, w_up, w_down, norm_final):
    depth = norm_mix.shape[0]
    assert depth == 1, "final-norm fusion below assumes a single layer"
    assert x.shape[-1] == 2 * D_MLSTM and x.shape[1] % TS_MIXER == 0
    return _layer(x, norm_mix[0], w_in[0], b_in[0], conv_w[0], w_out[0], norm_ffn[0],
                  w_group[0], b_group[0], w_expert[0], b_expert[0],
                  w_gate[0], w_up[0], w_down[0], norm_final)
```

```python
import jax
import jax.numpy as jnp
from jax import lax
from jax.experimental import pallas as pl
from jax.experimental.pallas import tpu as pltpu
from jax.experimental.pallas import tpu_sc as plsc

F32 = jnp.float32
BF16 = jnp.bfloat16
I32 = jnp.int32

N_HEADS = 4
HEAD_DIM = 128
D_MLSTM = N_HEADS * HEAD_DIM
D_CONV = 512
N_GROUPS = 4
EXPERTS_PER_GROUP = 8
N_EXPERTS = N_GROUPS * EXPERTS_PER_GROUP
D_FF = 512
RMS_EPS = 1e-6
Q_SCALE = HEAD_DIM ** -0.5

LANES = 128
CHUNK = LANES
AUG_ROWS = 16
ROUTER_COLS = 128
EXPERT_COL0 = 8

TM_INPROJ = 512
TS_MIXER = 512
OUT_COLS = 256
TM_EXPERT = 512
TC_COMBINE = 512

VMEM_LIMIT = 56 * 1024 * 1024


HI_HALF = -65536


def _pack_rows(x):
    w = x.shape[1] // 2
    lo = lax.shift_right_logical(lax.bitcast_convert_type(x[:, :w].astype(BF16).astype(F32), I32), 16)
    hi = lax.bitcast_convert_type(x[:, w:].astype(BF16).astype(F32), I32) & HI_HALF
    return lo | hi


def _unpack_rows(words):
    lo = lax.bitcast_convert_type(words << 16, F32)
    hi = lax.bitcast_convert_type(words & HI_HALF, F32)
    return lo, hi


def _nt_dot(a, b):
    return lax.dot_general(a, b, (((1,), (1,)), ((), ())), preferred_element_type=F32)


def _inproj_kernel(x_ref, g_ref, wa_ref, ba_ref, wqv_ref, bqv_ref, wg_ref, bg_ref,
                   a_ref, qvt_ref, gt_ref):
    x = x_ref[...]
    ms = jnp.mean(x * x, axis=-1, keepdims=True)
    hn = (x * lax.rsqrt(ms + RMS_EPS) * g_ref[...]).astype(BF16)
    n_blk = wa_ref.shape[1] // D_MLSTM
    for j in range(n_blk):
        sl = slice(j * D_MLSTM, (j + 1) * D_MLSTM)
        acc = jnp.dot(hn, wa_ref[:, sl], preferred_element_type=F32) + ba_ref[:, sl]
        a_ref[:, sl] = acc.astype(BF16)
    q_rows = slice(0, D_MLSTM)
    v_rows = slice(D_MLSTM, 2 * D_MLSTM)
    qt = (_nt_dot(wqv_ref[q_rows, :], hn) + bqv_ref[q_rows, :]) * Q_SCALE
    qvt_ref[q_rows, :] = qt.astype(BF16)
    qvt_ref[v_rows, :] = (_nt_dot(wqv_ref[v_rows, :], hn) + bqv_ref[v_rows, :]).astype(BF16)
    gt_ref[...] = _nt_dot(wg_ref[...], hn) + bg_ref[...]


def _inproj(x, norm_mix, w_in, b_in):
    b_sz, s_len, d = x.shape
    tm = TM_INPROJ
    dm = D_MLSTM
    g0 = 4 * dm
    c0 = g0 + 2 * N_HEADS
    cols_a = jnp.concatenate([w_in[:, dm:2 * dm], w_in[:, 3 * dm:4 * dm], w_in[:, c0:c0 + 3 * D_CONV]], axis=1)
    bias_a = jnp.concatenate([b_in[dm:2 * dm], b_in[3 * dm:4 * dm], b_in[c0:c0 + 3 * D_CONV]])[None, :]
    wqv_t = jnp.concatenate([w_in[:, 0:dm], w_in[:, 2 * dm:3 * dm]], axis=1).T
    bqv = jnp.concatenate([b_in[0:dm], b_in[2 * dm:3 * dm]])[:, None]
    wg_t = w_in[:, g0:c0].T
    bg = b_in[g0:c0][:, None]
    n_a = cols_a.shape[1]
    return pl.pallas_call(
        _inproj_kernel,
        grid=(b_sz, s_len // tm),
        in_specs=[
            pl.BlockSpec((None, tm, d), lambda b, s: (b, s, 0)),
            pl.BlockSpec((1, d), lambda b, s: (0, 0)),
            pl.BlockSpec((d, n_a), lambda b, s: (0, 0)),
            pl.BlockSpec((1, n_a), lambda b, s: (0, 0)),
            pl.BlockSpec((2 * dm, d), lambda b, s: (0, 0)),
            pl.BlockSpec((2 * dm, 1), lambda b, s: (0, 0)),
            pl.BlockSpec((2 * N_HEADS, d), lambda b, s: (0, 0)),
            pl.BlockSpec((2 * N_HEADS, 1), lambda b, s: (0, 0)),
        ],
        out_specs=[
            pl.BlockSpec((None, tm, n_a), lambda b, s: (b, s, 0)),
            pl.BlockSpec((None, 2 * dm, tm), lambda b, s: (b, 0, s)),
            pl.BlockSpec((None, 2 * N_HEADS, tm), lambda b, s: (b, 0, s)),
        ],
        out_shape=[
            jax.ShapeDtypeStruct((b_sz, s_len, n_a), BF16),
            jax.ShapeDtypeStruct((b_sz, 2 * dm, s_len), BF16),
            jax.ShapeDtypeStruct((b_sz, 2 * N_HEADS, s_len), F32),
        ],
        compiler_params=pltpu.CompilerParams(
            dimension_semantics=("arbitrary", "arbitrary"), vmem_limit_bytes=VMEM_LIMIT),
        name="inproj",
    )(x, norm_mix[None, :], cols_a.astype(BF16), bias_a, wqv_t.astype(BF16), bqv, wg_t.astype(BF16), bg)


def _mixer_kernel(k_ref, o_ref, cb_ref, cc_ref, cx_ref, qt_ref, vt_ref, gt_ref, x_ref,
                  cw_ref, wo_ref, g2_ref, wr_ref, br_ref,
                  x2_ref, hn2_ref, ri_ref, rw_ref, cnt_out_ref,
                  ct_ref, m_ref, carry_ref, cnt_ref, mix_ref):
    ts = x_ref.shape[0]
    n_chunks = ts // CHUNK
    first_tile = pl.program_id(1) == 0

    @pl.when(first_tile)
    def _():
        ct_ref[...] = jnp.zeros_like(ct_ref)
        m_ref[...] = jnp.zeros_like(m_ref)
        carry_ref[...] = jnp.zeros_like(carry_ref)

    @pl.when(jnp.logical_and(first_tile, pl.program_id(0) == 0))
    def _():
        cnt_ref[...] = jnp.zeros_like(cnt_ref)

    gates = gt_ref[...]
    li_all = gates[0:N_HEADS]
    fg = gates[N_HEADS:2 * N_HEADS]
    lf_all = jnp.minimum(fg, 0.0) - jnp.log1p(jnp.exp(-jnp.abs(fg)))
    key_j = lax.broadcasted_iota(I32, (CHUNK, CHUNK), 0)
    qry_i = lax.broadcasted_iota(I32, (CHUNK, CHUNK), 1)
    causal_t = key_j <= qry_i
    neg_inf = jnp.float32(-jnp.inf)
    ones_rows = (lax.broadcasted_iota(I32, (AUG_ROWS, CHUNK), 0) == 0).astype(BF16)

    prefix_ones = (key_j <= qry_i).astype(BF16)

    lf_hi = lf_all.astype(BF16)
    rem = lf_all - lf_hi.astype(F32)
    lf_mid = rem.astype(BF16)
    lf_lo = (rem - lf_mid.astype(F32)).astype(BF16)
    zero_bf = jnp.zeros((N_HEADS, CHUNK), BF16)
    parts = []
    for c in range(n_chunks):
        rows = slice(c * CHUNK, (c + 1) * CHUNK)
        parts += [lf_hi[:, rows], lf_mid[:, rows], lf_lo[:, rows], zero_bf]
    cum = jnp.dot(jnp.concatenate(parts, axis=0), prefix_ones, preferred_element_type=F32)

    m_prev = m_ref[0:N_HEADS, 0:1]
    for c in range(n_chunks):
        rows = slice(c * CHUNK, (c + 1) * CHUNK)
        li = li_all[:, rows]
        r0 = 4 * N_HEADS * c
        b_cum = (cum[r0:r0 + N_HEADS] + cum[r0 + N_HEADS:r0 + 2 * N_HEADS]
                 + cum[r0 + 2 * N_HEADS:r0 + 3 * N_HEADS])
        g_tot = b_cum[:, CHUNK - 1:CHUNK]
        a_end = g_tot - b_cum + li
        m_loc = jnp.max(a_end, axis=1, keepdims=True)
        u = li - b_cum
        m_new = jnp.maximum(g_tot + m_prev, m_loc)
        decay = jnp.exp(g_tot + m_prev - m_new)
        s_fac = jnp.exp(m_loc - m_new)
        w_key = jnp.exp(a_end - m_loc)
        u_cols = jnp.transpose(jnp.concatenate([u, jnp.zeros_like(u)], axis=0))
        for h in range(N_HEADS):
            hc = slice(h * HEAD_DIM, (h + 1) * HEAD_DIM)
            kh = k_ref[rows, hc]
            qth = qt_ref[hc, rows]
            vt_aug = jnp.concatenate([vt_ref[hc, rows], ones_rows], axis=0)
            s_t = jnp.dot(kh, qth, preferred_element_type=F32)
            u_t = jnp.where(causal_t, u_cols[:, h:h + 1], neg_inf)
            mp = m_prev[h:h + 1, :]
            m_tot = jnp.maximum(jnp.max(u_t, axis=0, keepdims=True), mp)
            f_inter = jnp.exp(mp - m_tot)
            floor = jnp.exp(-b_cum[h:h + 1, :] - m_tot)
            p_t = (s_t * jnp.exp(u_t - m_tot)).astype(BF16)
            q_f = (qth.astype(F32) * f_inter).astype(BF16)
            c_old = ct_ref[h]
            lhs = jnp.concatenate([vt_aug, c_old.astype(BF16)], axis=1)
            rhs = jnp.concatenate([p_t, q_f], axis=0)
            r_t = jnp.dot(lhs, rhs, preferred_element_type=F32)
            vtw = (vt_aug.astype(F32) * w_key[h:h + 1, :]).astype(BF16)
            kv = jnp.dot(vtw, kh, preferred_element_type=F32)
            ct_ref[h] = decay[h:h + 1, :] * c_old + s_fac[h:h + 1, :] * kv

            den = r_t[HEAD_DIM:HEAD_DIM + 1, :]
            inv = 1.0 / jnp.maximum(jnp.abs(den), floor)
            h_nat = jnp.transpose(r_t[0:HEAD_DIM, :] * inv)
            o_gate = jax.nn.sigmoid(o_ref[rows, hc].astype(F32))
            mix_ref[rows, hc] = (o_gate * h_nat).astype(BF16)
        m_prev = m_new
    m_ref[0:N_HEADS, :] = jnp.broadcast_to(m_prev, (N_HEADS, LANES))

    u_c = cc_ref[...].astype(F32) * cx_ref[...].astype(F32)
    rid = lax.broadcasted_iota(I32, u_c.shape, 0)
    prev1 = carry_ref[0:1, :]
    prev2 = carry_ref[1:2, :]
    u1 = jnp.where(rid == 0, prev1, pltpu.roll(u_c, 1, axis=0))
    u2 = jnp.where(rid == 0, prev2, jnp.where(rid == 1, prev1, pltpu.roll(u_c, 2, axis=0)))
    z = cw_ref[0:1, :] * u2 + cw_ref[1:2, :] * u1 + cw_ref[2:3, :] * u_c
    mix_ref[:, D_MLSTM:] = (cb_ref[...].astype(F32) * z).astype(BF16)
    carry_ref[0:1, :] = u_c[ts - 1:ts, :]
    carry_ref[1:2, :] = u_c[ts - 2:ts - 1, :]

    d_model = x_ref.shape[1]
    mix = mix_ref[...]
    ssq = jnp.zeros((ts, 1), F32)
    for j in range(d_model // OUT_COLS):
        cs = slice(j * OUT_COLS, (j + 1) * OUT_COLS)
        x2c = x_ref[:, cs] + jnp.dot(mix, wo_ref[:, cs], preferred_element_type=F32)
        x2_ref[:, cs] = x2c
        ssq = ssq + jnp.sum(x2c * x2c, axis=-1, keepdims=True)
    hn2 = x2_ref[...] * lax.rsqrt(ssq * (1.0 / d_model) + RMS_EPS) * g2_ref[...]
    hn2_ref[...] = _pack_rows(hn2)

    h_hi = hn2.astype(BF16)
    h_lo = (hn2 - h_hi.astype(F32)).astype(BF16)
    hi_both = jnp.dot(h_hi, wr_ref[...], preferred_element_type=F32)
    lo_hi = jnp.dot(h_lo, wr_ref[:, 0:ROUTER_COLS], preferred_element_type=F32)
    logits = hi_both[:, 0:ROUTER_COLS] + (hi_both[:, ROUTER_COLS:] + lo_hi) + br_ref[...]
    lt = jnp.transpose(logits)
    gl = lt[0:N_GROUPS]
    gmax = jnp.max(gl, axis=0, keepdims=True)
    gi = lax.broadcasted_iota(I32, gl.shape, 0)
    g_sel = jnp.min(jnp.where(gl == gmax, gi, N_GROUPS), axis=0, keepdims=True)
    p_g = 1.0 / jnp.sum(jnp.exp(gl - gmax), axis=0, keepdims=True)
    in_g = lt[EXPERT_COL0:EXPERT_COL0 + EXPERTS_PER_GROUP]
    for g in range(1, N_GROUPS):
        lo = EXPERT_COL0 + g * EXPERTS_PER_GROUP
        in_g = jnp.where(g_sel == g, lt[lo:lo + EXPERTS_PER_GROUP], in_g)
    ei = lax.broadcasted_iota(I32, in_g.shape, 0)
    v1 = jnp.max(in_g, axis=0, keepdims=True)
    i1 = jnp.min(jnp.where(in_g == v1, ei, EXPERTS_PER_GROUP), axis=0, keepdims=True)
    rest = jnp.where(ei == i1, neg_inf, in_g)
    v2 = jnp.max(rest, axis=0, keepdims=True)
    i2 = jnp.min(jnp.where(rest == v2, ei, EXPERTS_PER_GROUP), axis=0, keepdims=True)
    t = jnp.exp(v2 - v1)
    w1 = p_g / (1.0 + t)
    w2 = p_g * t / (1.0 + t)
    e1 = g_sel * EXPERTS_PER_GROUP + i1
    e2 = g_sel * EXPERTS_PER_GROUP + i2

    eio = lax.broadcasted_iota(I32, (N_EXPERTS, LANES), 0)
    base = cnt_ref[:, 0:1]
    zero_i = jnp.zeros((1, LANES), I32)
    n_blk = ts // LANES
    hits = []
    for k in range(n_blk):
        ls = slice(k * LANES, (k + 1) * LANES)
        hits.append((eio == e1[:, ls], eio == e2[:, ls]))
    onehots = [jnp.where(jnp.logical_or(a1, a2), 1.0, 0.0) for a1, a2 in hits]
    incl_all = jnp.dot(jnp.concatenate(onehots, axis=0).astype(BF16), prefix_ones,
                       preferred_element_type=F32).astype(I32)
    for k in range(n_blk):
        ls = slice(k * LANES, (k + 1) * LANES)
        is1, is2 = hits[k]
        onehot = onehots[k].astype(I32)
        incl = incl_all[k * N_EXPERTS:(k + 1) * N_EXPERTS]
        rank_e = base + incl - onehot
        r1 = jnp.sum(jnp.where(is1, rank_e, 0), axis=0, keepdims=True)
        r2 = jnp.sum(jnp.where(is2, rank_e, 0), axis=0, keepdims=True)
        ri_ref[:, ls] = jnp.concatenate(
            [e1[:, ls], e2[:, ls], r1, r2, zero_i, zero_i, zero_i, zero_i], axis=0)
        base = base + incl[:, LANES - 1:LANES]
    zero_f = jnp.zeros_like(w1)
    rw_ref[...] = jnp.concatenate([w1, w2, zero_f, zero_f, zero_f, zero_f, zero_f, zero_f], axis=0)
    cnt_new = jnp.broadcast_to(base, (N_EXPERTS, LANES))
    cnt_ref[...] = cnt_new
    cnt_out_ref[...] = cnt_new


def _mixer(x, a, qvt, gt, conv_w, w_out, norm_ffn, w_router, b_router):
    b_sz, s_len, d = x.shape
    ts = TS_MIXER
    dm = D_MLSTM

    def a_spec(j):
        return pl.BlockSpec((None, ts, dm), lambda b, s, j=j: (b, s, j))

    tok = lambda w: pl.BlockSpec((None, ts, w), lambda b, s: (b, s, 0))
    rowsp = pl.BlockSpec((None, 2 * N_HEADS, ts), lambda b, s: (b, 0, s))
    const = lambda shape: pl.BlockSpec(shape, lambda b, s: tuple(0 for _ in shape))
    wr_hi = w_router.astype(BF16)
    wr_lo = (w_router - wr_hi.astype(F32)).astype(BF16)
    w_router2 = jnp.concatenate([wr_hi, wr_lo], axis=1)
    return pl.pallas_call(
        _mixer_kernel,
        grid=(b_sz, s_len // ts),
        in_specs=[a_spec(0), a_spec(1), a_spec(2), a_spec(3), a_spec(4),
                  pl.BlockSpec((None, dm, ts), lambda b, s: (b, 0, s)),
                  pl.BlockSpec((None, dm, ts), lambda b, s: (b, 1, s)),
                  rowsp, tok(d),
                  const((3, D_CONV)), const((d, d)), const((1, d)),
                  const((d, 2 * ROUTER_COLS)), const((1, ROUTER_COLS))],
        out_specs=[tok(d), tok(d // 2), rowsp, rowsp, const((N_EXPERTS, LANES))],
        out_shape=[
            jax.ShapeDtypeStruct((b_sz, s_len, d), F32),
            jax.ShapeDtypeStruct((b_sz, s_len, d // 2), I32),
            jax.ShapeDtypeStruct((b_sz, 2 * N_HEADS, s_len), I32),
            jax.ShapeDtypeStruct((b_sz, 2 * N_HEADS, s_len), F32),
            jax.ShapeDtypeStruct((N_EXPERTS, LANES), I32),
        ],
        scratch_shapes=[
            pltpu.VMEM((N_HEADS, HEAD_DIM + AUG_ROWS, HEAD_DIM), F32),
            pltpu.VMEM((2 * N_HEADS, LANES), F32),
            pltpu.VMEM((8, D_CONV), F32),
            pltpu.VMEM((N_EXPERTS, LANES), I32),
            pltpu.VMEM((ts, d), BF16),
        ],
        compiler_params=pltpu.CompilerParams(
            dimension_semantics=("arbitrary", "arbitrary"), vmem_limit_bytes=VMEM_LIMIT),
        name="mixer",
    )(a, a, a, a, a, qvt, qvt, gt, x, conv_w, w_out.astype(BF16), norm_ffn[None, :], w_router2, b_router)


SC_WORKERS = 32
SC_CHUNK = 128


def _sc_worker_id():
    return lax.axis_index("s") * 2 + lax.axis_index("c")


def _sc_scatter_rows(rows, idx, n_out):
    n_rows, width = rows.shape
    per_worker = n_rows // SC_WORKERS
    n_chunks = per_worker // SC_CHUNK
    mesh = plsc.VectorSubcoreMesh(core_axis_name="c", subcore_axis_name="s")

    def body(rows_hbm, idx_hbm, out_hbm, idx_v, rows_v, sem):
        wid = _sc_worker_id()

        @pl.loop(0, n_chunks)
        def _(j):
            base = wid * per_worker + j * SC_CHUNK
            pltpu.sync_copy(rows_hbm.at[pl.ds(base, SC_CHUNK)], rows_v)
            for k in range(2):
                pltpu.sync_copy(idx_hbm.at[pl.ds(k * n_rows + base, SC_CHUNK)], idx_v)
                pltpu.async_copy(rows_v, out_hbm.at[idx_v], sem).wait()

    return pl.kernel(
        body,
        out_type=jax.ShapeDtypeStruct((n_out, width), I32),
        mesh=mesh,
        scratch_types=[pltpu.VMEM((SC_CHUNK,), I32), pltpu.VMEM((SC_CHUNK, width), I32),
                       pltpu.SemaphoreType.DMA],
        name="sc_scatter",
    )(rows, idx)


def _sc_gather_rows(table, idx):
    n_rows = idx.shape[0]
    width = table.shape[1]
    per_worker = n_rows // SC_WORKERS
    n_chunks = per_worker // SC_CHUNK
    mesh = plsc.VectorSubcoreMesh(core_axis_name="c", subcore_axis_name="s")

    def body(table_hbm, idx_hbm, out_hbm, idx_v, rows_v, sem):
        wid = _sc_worker_id()

        @pl.loop(0, n_chunks)
        def _(j):
            base = wid * per_worker + j * SC_CHUNK
            pltpu.sync_copy(idx_hbm.at[pl.ds(base, SC_CHUNK)], idx_v)
            pltpu.async_copy(table_hbm.at[idx_v], rows_v, sem).wait()
            pltpu.sync_copy(rows_v, out_hbm.at[pl.ds(base, SC_CHUNK)])

    return pl.kernel(
        body,
        out_type=jax.ShapeDtypeStruct((n_rows, width), I32),
        mesh=mesh,
        scratch_types=[pltpu.VMEM((SC_CHUNK,), I32), pltpu.VMEM((SC_CHUNK, width), I32),
                       pltpu.SemaphoreType.DMA],
        name="sc_gather",
    )(table, idx)


def _experts_kernel(te_ref, nu_ref, tv_ref, xs_ref, wg_ref, wu_ref, wd_ref, y_ref,
                    wgb_ref, wub_ref, wdb_ref):
    i = pl.program_id(0)
    e = te_ref[i]
    e_prev = te_ref[jnp.maximum(i - 1, 0)]
    new_expert = jnp.logical_or(i == 0, e != e_prev)
    used = i < nu_ref[0]

    @pl.when(jnp.logical_and(used, new_expert))
    def _():
        wgb_ref[...] = wg_ref[...].astype(BF16)
        wub_ref[...] = wu_ref[...].astype(BF16)
        wdb_ref[...] = wd_ref[...].astype(BF16)

    @pl.when(used)
    def _():
        row = lax.broadcasted_iota(I32, xs_ref.shape, 0)
        x_lo, x_hi = _unpack_rows(jnp.where(row < tv_ref[i], xs_ref[...], 0))
        xb = jnp.concatenate([x_lo.astype(BF16), x_hi.astype(BF16)], axis=1)
        gate = jnp.dot(xb, wgb_ref[...], preferred_element_type=F32)
        up = jnp.dot(xb, wub_ref[...], preferred_element_type=F32)
        hid = (gate * jax.nn.sigmoid(gate) * up).astype(BF16)
        y_ref[...] = _pack_rows(jnp.dot(hid, wdb_ref[...], preferred_element_type=F32))

    @pl.when(jnp.logical_not(used))
    def _():
        y_ref[...] = jnp.zeros_like(y_ref)


def _experts(xs, tile_e, n_used, tile_valid, w_gate, w_up, w_down):
    cap, dw = xs.shape
    d = 2 * dw
    tm = TM_EXPERT
    n_tiles = cap // tm

    def x_map(i, te, nu, tv):
        return (jnp.minimum(i, jnp.maximum(nu[0] - 1, 0)), 0)

    def w_map(i, te, nu, tv):
        return (te[i], 0, 0)

    return pl.pallas_call(
        _experts_kernel,
        grid_spec=pltpu.PrefetchScalarGridSpec(
            num_scalar_prefetch=3,
            grid=(n_tiles,),
            in_specs=[
                pl.BlockSpec((tm, dw), x_map),
                pl.BlockSpec((None, d, D_FF), w_map),
                pl.BlockSpec((None, d, D_FF), w_map),
                pl.BlockSpec((None, D_FF, d), w_map),
            ],
            out_specs=pl.BlockSpec((tm, dw), lambda i, te, nu, tv: (i, 0)),
            scratch_shapes=[pltpu.VMEM((d, D_FF), BF16), pltpu.VMEM((d, D_FF), BF16),
                            pltpu.VMEM((D_FF, d), BF16)],
        ),
        out_shape=jax.ShapeDtypeStruct((cap, dw), I32),
        compiler_params=pltpu.CompilerParams(
            dimension_semantics=("arbitrary",), vmem_limit_bytes=VMEM_LIMIT),
        name="experts",
    )(tile_e, n_used, tile_valid, xs, w_gate, w_up, w_down)


def _combine_stream_kernel(x2_ref, y1_ref, y2_ref, rw_ref, gf_ref, out_ref):
    w_t = jnp.transpose(rw_ref[...])
    a_lo, a_hi = _unpack_rows(y1_ref[...])
    b_lo, b_hi = _unpack_rows(y2_ref[...])
    w1 = w_t[:, 0:1]
    w2 = w_t[:, 1:2]
    moe = jnp.concatenate([w1 * a_lo + w2 * b_lo, w1 * a_hi + w2 * b_hi], axis=1)
    x3 = x2_ref[...] + moe
    ms = jnp.mean(x3 * x3, axis=-1, keepdims=True)
    out_ref[...] = x3 * lax.rsqrt(ms + RMS_EPS) * gf_ref[...]


def _combine_stream(x2, yg, rw, norm_final):
    b_sz, s_len, d = x2.shape
    tc = TC_COMBINE
    n_s = s_len // tc
    n_steps = b_sz * n_s
    return pl.pallas_call(
        _combine_stream_kernel,
        grid=(n_steps,),
        in_specs=[
            pl.BlockSpec((tc, d), lambda g: (g, 0)),
            pl.BlockSpec((tc, d // 2), lambda g: (g, 0)),
            pl.BlockSpec((tc, d // 2), lambda g: (g + n_steps, 0)),
            pl.BlockSpec((None, 2 * N_HEADS, tc), lambda g: (g // n_s, 0, g % n_s)),
            pl.BlockSpec((1, d), lambda g: (0, 0)),
        ],
        out_specs=pl.BlockSpec((tc, d), lambda g: (g, 0)),
        out_shape=jax.ShapeDtypeStruct((b_sz * s_len, d), F32),
        compiler_params=pltpu.CompilerParams(
            dimension_semantics=("arbitrary",), vmem_limit_bytes=VMEM_LIMIT),
        name="combine_stream",
    )(x2.reshape(b_sz * s_len, d), yg, yg, rw, norm_final[None, :]).reshape(b_sz, s_len, d)


def _layer(x, norm_mix, w_in, b_in, conv_w, w_out, norm_ffn, w_group, b_group, w_expert, b_expert,
           w_gate, w_up, w_down, norm_out):
    b_sz, s_len, d = x.shape
    n_tok = b_sz * s_len
    a, qvt, gt = _inproj(x, norm_mix, w_in, b_in)

    w_router = jnp.zeros((d, ROUTER_COLS), F32)
    w_router = w_router.at[:, 0:N_GROUPS].set(w_group).at[:, EXPERT_COL0:EXPERT_COL0 + N_EXPERTS].set(w_expert)
    b_router = jnp.zeros((1, ROUTER_COLS), F32)
    b_router = b_router.at[0, 0:N_GROUPS].set(b_group).at[0, EXPERT_COL0:EXPERT_COL0 + N_EXPERTS].set(b_expert)
    x2, hn2, ri, rw, counts = _mixer(x, a, qvt, gt, conv_w, w_out, norm_ffn, w_router, b_router)

    tm = TM_EXPERT
    cnt = counts[:, 0]
    padded = ((cnt + tm - 1) // tm) * tm
    pends = jnp.cumsum(padded).astype(I32)
    seg = jnp.concatenate([jnp.zeros((1,), I32), pends])
    n_tiles = (n_tok * 2) // tm + N_EXPERTS
    cap = n_tiles * tm
    tile_start = jnp.arange(n_tiles, dtype=I32) * tm
    tile_e = jnp.minimum(jnp.sum((tile_start[:, None] >= pends[None, :]).astype(I32), axis=1),
                         N_EXPERTS - 1)
    n_used = pends[-1:] // tm
    e_sel = ri[:, 0:2, :]
    seg_start = jnp.sum(jnp.where(e_sel[..., None] == jnp.arange(N_EXPERTS, dtype=I32), seg[:-1], 0), axis=-1)
    pos = ri[:, 2:4, :] + seg_start

    seg_tok_end = seg[:-1] + cnt.astype(I32)
    tile_tok_end = jnp.sum(jnp.where(tile_e[:, None] == jnp.arange(N_EXPERTS, dtype=I32), seg_tok_end, 0), axis=-1)
    tile_valid = jnp.clip(tile_tok_end - tile_start, 0, tm).astype(I32)
    pos_flat = jnp.concatenate([pos[:, 0, :].reshape(-1), pos[:, 1, :].reshape(-1)])

    xs = _sc_scatter_rows(hn2.reshape(n_tok, d // 2), pos_flat, cap)
    y = _experts(xs, tile_e, n_used, tile_valid, w_gate, w_up, w_down)
    yg = _sc_gather_rows(y, pos_flat)
    return _combine_stream(x2, yg, rw, norm_out)


def kernel(x, norm_mix, w_in, b_in, conv_w, w_out, norm_ffn, w_group, b_group, w_expert, b_expert,
           w_gate, w_up, w_down, norm_final):
    depth = norm_mix.shape[0]
    assert depth == 1, "final-norm fusion below assumes a single layer"
    assert x.shape[-1] == 2 * D_MLSTM and x.shape[1] % TS_MIXER == 0
    return _layer(x, norm_mix[0], w_in[0], b_in[0], conv_w[0], w_out[0], norm_ffn[0],
                  w_group[0], b_group[0], w_expert[0], b_expert[0],
                  w_gate[0], w_up[0], w_down[0], norm_final)
```

```python
import jax
import jax.numpy as jnp
from jax import lax
from jax.experimental import pallas as pl
from jax.experimental.pallas import tpu as pltpu
from jax.experimental.pallas import tpu_sc as plsc

F32 = jnp.float32
BF16 = jnp.bfloat16
I32 = jnp.int32

N_HEADS = 4
HEAD_DIM = 128
D_MLSTM = N_HEADS * HEAD_DIM
D_CONV = 512
N_GROUPS = 4
EXPERTS_PER_GROUP = 8
N_EXPERTS = N_GROUPS * EXPERTS_PER_GROUP
D_FF = 512
RMS_EPS = 1e-6
Q_SCALE = HEAD_DIM ** -0.5

LANES = 128
CHUNK = LANES
AUG_ROWS = 16
ROUTER_COLS = 128
EXPERT_COL0 = 8

TM_INPROJ = 512
TS_MIXER = 512
OUT_COLS = 256
TM_EXPERT = 512
TC_COMBINE = 512

VMEM_LIMIT = 56 * 1024 * 1024


HI_HALF = -65536


def _pack_rows(x):
    w = x.shape[1] // 2
    lo = lax.shift_right_logical(lax.bitcast_convert_type(x[:, :w].astype(BF16).astype(F32), I32), 16)
    hi = lax.bitcast_convert_type(x[:, w:].astype(BF16).astype(F32), I32) & HI_HALF
    return lo | hi


def _unpack_rows(words):
    lo = lax.bitcast_convert_type(words << 16, F32)
    hi = lax.bitcast_convert_type(words & HI_HALF, F32)
    return lo, hi


def _nt_dot(a, b):
    return lax.dot_general(a, b, (((1,), (1,)), ((), ())), preferred_element_type=F32)


def _inproj_kernel(x_ref, g_ref, wa_ref, ba_ref, wqv_ref, bqv_ref, wg_ref, bg_ref,
                   a_ref, qvt_ref, gt_ref):
    x = x_ref[...]
    ms = jnp.mean(x * x, axis=-1, keepdims=True)
    hn = (x * lax.rsqrt(ms + RMS_EPS) * g_ref[...]).astype(BF16)
    n_blk = wa_ref.shape[1] // D_MLSTM
    for j in range(n_blk):
        sl = slice(j * D_MLSTM, (j + 1) * D_MLSTM)
        acc = jnp.dot(hn, wa_ref[:, sl], preferred_element_type=F32) + ba_ref[:, sl]
        a_ref[:, sl] = acc.astype(BF16)
    q_rows = slice(0, D_MLSTM)
    v_rows = slice(D_MLSTM, 2 * D_MLSTM)
    qt = (_nt_dot(wqv_ref[q_rows, :], hn) + bqv_ref[q_rows, :]) * Q_SCALE
    qvt_ref[q_rows, :] = qt.astype(BF16)
    qvt_ref[v_rows, :] = (_nt_dot(wqv_ref[v_rows, :], hn) + bqv_ref[v_rows, :]).astype(BF16)
    gt_ref[...] = _nt_dot(wg_ref[...], hn) + bg_ref[...]


def _inproj(x, norm_mix, w_in, b_in):
    b_sz, s_len, d = x.shape
    tm = TM_INPROJ
    dm = D_MLSTM
    g0 = 4 * dm
    c0 = g0 + 2 * N_HEADS
    cols_a = jnp.concatenate([w_in[:, dm:2 * dm], w_in[:, 3 * dm:4 * dm], w_in[:, c0:c0 + 3 * D_CONV]], axis=1)
    bias_a = jnp.concatenate([b_in[dm:2 * dm], b_in[3 * dm:4 * dm], b_in[c0:c0 + 3 * D_CONV]])[None, :]
    wqv_t = jnp.concatenate([w_in[:, 0:dm], w_in[:, 2 * dm:3 * dm]], axis=1).T
    bqv = jnp.concatenate([b_in[0:dm], b_in[2 * dm:3 * dm]])[:, None]
    wg_t = w_in[:, g0:c0].T
    bg = b_in[g0:c0][:, None]
    n_a = cols_a.shape[1]
    return pl.pallas_call(
        _inproj_kernel,
        grid=(b_sz, s_len // tm),
        in_specs=[
            pl.BlockSpec((None, tm, d), lambda b, s: (b, s, 0)),
            pl.BlockSpec((1, d), lambda b, s: (0, 0)),
            pl.BlockSpec((d, n_a), lambda b, s: (0, 0)),
            pl.BlockSpec((1, n_a), lambda b, s: (0, 0)),
            pl.BlockSpec((2 * dm, d), lambda b, s: (0, 0)),
            pl.BlockSpec((2 * dm, 1), lambda b, s: (0, 0)),
            pl.BlockSpec((2 * N_HEADS, d), lambda b, s: (0, 0)),
            pl.BlockSpec((2 * N_HEADS, 1), lambda b, s: (0, 0)),
        ],
        out_specs=[
            pl.BlockSpec((None, tm, n_a), lambda b, s: (b, s, 0)),
            pl.BlockSpec((None, 2 * dm, tm), lambda b, s: (b, 0, s)),
            pl.BlockSpec((None, 2 * N_HEADS, tm), lambda b, s: (b, 0, s)),
        ],
        out_shape=[
            jax.ShapeDtypeStruct((b_sz, s_len, n_a), BF16),
            jax.ShapeDtypeStruct((b_sz, 2 * dm, s_len), BF16),
            jax.ShapeDtypeStruct((b_sz, 2 * N_HEADS, s_len), F32),
        ],
        compiler_params=pltpu.CompilerParams(
            dimension_semantics=("arbitrary", "arbitrary"), vmem_limit_bytes=VMEM_LIMIT),
        name="inproj",
    )(x, norm_mix[None, :], cols_a.astype(BF16), bias_a, wqv_t.astype(BF16), bqv, wg_t.astype(BF16), bg)


def _mixer_kernel(k_ref, o_ref, cb_ref, cc_ref, cx_ref, qt_ref, vt_ref, gt_ref, x_ref,
                  cw_ref, wo_ref, g2_ref, wr_ref, br_ref,
                  x2_ref, hn2_ref, ri_ref, rw_ref, cnt_out_ref,
                  ct_ref, m_ref, carry_ref, cnt_ref, mix_ref):
    ts = x_ref.shape[0]
    n_chunks = ts // CHUNK
    first_tile = pl.program_id(1) == 0

    @pl.when(first_tile)
    def _():
        ct_ref[...] = jnp.zeros_like(ct_ref)
        m_ref[...] = jnp.zeros_like(m_ref)
        carry_ref[...] = jnp.zeros_like(carry_ref)

    @pl.when(jnp.logical_and(first_tile, pl.program_id(0) == 0))
    def _():
        cnt_ref[...] = jnp.zeros_like(cnt_ref)

    gates = gt_ref[...]
    li_all = gates[0:N_HEADS]
    fg = gates[N_HEADS:2 * N_HEADS]
    lf_all = jnp.minimum(fg, 0.0) - jnp.log1p(jnp.exp(-jnp.abs(fg)))
    key_j = lax.broadcasted_iota(I32, (CHUNK, CHUNK), 0)
    qry_i = lax.broadcasted_iota(I32, (CHUNK, CHUNK), 1)
    causal_t = key_j <= qry_i
    neg_inf = jnp.float32(-jnp.inf)
    ones_rows = (lax.broadcasted_iota(I32, (AUG_ROWS, CHUNK), 0) == 0).astype(BF16)

    prefix_ones = (key_j <= qry_i).astype(BF16)

    lf_hi = lf_all.astype(BF16)
    rem = lf_all - lf_hi.astype(F32)
    lf_mid = rem.astype(BF16)
    lf_lo = (rem - lf_mid.astype(F32)).astype(BF16)
    zero_bf = jnp.zeros((N_HEADS, CHUNK), BF16)
    parts = []
    for c in range(n_chunks):
        rows = slice(c * CHUNK, (c + 1) * CHUNK)
        parts += [lf_hi[:, rows], lf_mid[:, rows], lf_lo[:, rows], zero_bf]
    cum = jnp.dot(jnp.concatenate(parts, axis=0), prefix_ones, preferred_element_type=F32)

    m_prev = m_ref[0:N_HEADS, 0:1]
    for c in range(n_chunks):
        rows = slice(c * CHUNK, (c + 1) * CHUNK)
        li = li_all[:, rows]
        r0 = 4 * N_HEADS * c
        b_cum = (cum[r0:r0 + N_HEADS] + cum[r0 + N_HEADS:r0 + 2 * N_HEADS]
                 + cum[r0 + 2 * N_HEADS:r0 + 3 * N_HEADS])
        g_tot = b_cum[:, CHUNK - 1:CHUNK]
        a_end = g_tot - b_cum + li
        m_loc = jnp.max(a_end, axis=1, keepdims=True)
        u = li - b_cum
        m_new = jnp.maximum(g_tot + m_prev, m_loc)
        decay = jnp.exp(g_tot + m_prev - m_new)
        s_fac = jnp.exp(m_loc - m_new)
        w_key = jnp.exp(a_end - m_loc)
        u_cols = jnp.transpose(jnp.concatenate([u, jnp.zeros_like(u)], axis=0))
        for h in range(N_HEADS):
            hc = slice(h * HEAD_DIM, (h + 1) * HEAD_DIM)
            kh = k_ref[rows, hc]
            qth = qt_ref[hc, rows]
            vt_aug = jnp.concatenate([vt_ref[hc, rows], ones_rows], axis=0)
            s_t = jnp.dot(kh, qth, preferred_element_type=F32)
            u_t = jnp.where(causal_t, u_cols[:, h:h + 1], neg_inf)
            mp = m_prev[h:h + 1, :]
            m_tot = jnp.maximum(jnp.max(u_t, axis=0, keepdims=True), mp)
            f_inter = jnp.exp(mp - m_tot)
            floor = jnp.exp(-b_cum[h:h + 1, :] - m_tot)
            p_t = (s_t * jnp.exp(u_t - m_tot)).astype(BF16)
            q_f = (qth.astype(F32) * f_inter).astype(BF16)
            c_old = ct_ref[h]
            lhs = jnp.concatenate([vt_aug, c_old.astype(BF16)], axis=1)
            rhs = jnp.concatenate([p_t, q_f], axis=0)
            r_t = jnp.dot(lhs, rhs, preferred_element_type=F32)
            vtw = (vt_aug.astype(F32) * w_key[h:h + 1, :]).astype(BF16)
            kv = jnp.dot(vtw, kh, preferred_element_type=F32)
            ct_ref[h] = decay[h:h + 1, :] * c_old + s_fac[h:h + 1, :] * kv

            den = r_t[HEAD_DIM:HEAD_DIM + 1, :]
            inv = 1.0 / jnp.maximum(jnp.abs(den), floor)
            h_nat = jnp.transpose(r_t[0:HEAD_DIM, :] * inv)
            o_gate = jax.nn.sigmoid(o_ref[rows, hc].astype(F32))
            mix_ref[rows, hc] = (o_gate * h_nat).astype(BF16)
        m_prev = m_new
    m_ref[0:N_HEADS, :] = jnp.broadcast_to(m_prev, (N_HEADS, LANES))

    u_c = cc_ref[...].astype(F32) * cx_ref[...].astype(F32)
    rid = lax.broadcasted_iota(I32, u_c.shape, 0)
    prev1 = carry_ref[0:1, :]
    prev2 = carry_ref[1:2, :]
    u1 = jnp.where(rid == 0, prev1, pltpu.roll(u_c, 1, axis=0))
    u2 = jnp.where(rid == 0, prev2, jnp.where(rid == 1, prev1, pltpu.roll(u_c, 2, axis=0)))
    z = cw_ref[0:1, :] * u2 + cw_ref[1:2, :] * u1 + cw_ref[2:3, :] * u_c
    mix_ref[:, D_MLSTM:] = (cb_ref[...].astype(F32) * z).astype(BF16)
    carry_ref[0:1, :] = u_c[ts - 1:ts, :]
    carry_ref[1:2, :] = u_c[ts - 2:ts - 1, :]

    d_model = x_ref.shape[1]
    mix = mix_ref[...]
    ssq = jnp.zeros((ts, 1), F32)
    for j in range(d_model // OUT_COLS):
        cs = slice(j * OUT_COLS, (j + 1) * OUT_COLS)
        x2c = x_ref[:, cs] + jnp.dot(mix, wo_ref[:, cs], preferred_element_type=F32)
        x2_ref[:, cs] = x2c
        ssq = ssq + jnp.sum(x2c * x2c, axis=-1, keepdims=True)
    hn2 = x2_ref[...] * lax.rsqrt(ssq * (1.0 / d_model) + RMS_EPS) * g2_ref[...]
    hn2_ref[...] = _pack_rows(hn2)

    h_hi = hn2.astype(BF16)
    h_lo = (hn2 - h_hi.astype(F32)).astype(BF16)
    hi_both = jnp.dot(h_hi, wr_ref[...], preferred_element_type=F32)
    lo_hi = jnp.dot(h_lo, wr_ref[:, 0:ROUTER_COLS], preferred_element_type=F32)
    logits = hi_both[:, 0:ROUTER_COLS] + (hi_both[:, ROUTER_COLS:] + lo_hi) + br_ref[...]
    lt = jnp.transpose(logits)
    gl = lt[0:N_GROUPS]
    gmax = jnp.max(gl, axis=0, keepdims=True)
    gi = lax.broadcasted_iota(I32, gl.shape, 0)
    g_sel = jnp.min(jnp.where(gl == gmax, gi, N_GROUPS), axis=0, keepdims=True)
    p_g = 1.0 / jnp.sum(jnp.exp(gl - gmax), axis=0, keepdims=True)
    in_g = lt[EXPERT_COL0:EXPERT_COL0 + EXPERTS_PER_GROUP]
    for g in range(1, N_GROUPS):
        lo = EXPERT_COL0 + g * EXPERTS_PER_GROUP
        in_g = jnp.where(g_sel == g, lt[lo:lo + EXPERTS_PER_GROUP], in_g)
    ei = lax.broadcasted_iota(I32, in_g.shape, 0)
    v1 = jnp.max(in_g, axis=0, keepdims=True)
    i1 = jnp.min(jnp.where(in_g == v1, ei, EXPERTS_PER_GROUP), axis=0, keepdims=True)
    rest = jnp.where(ei == i1, neg_inf, in_g)
    v2 = jnp.max(rest, axis=0, keepdims=True)
    i2 = jnp.min(jnp.where(rest == v2, ei, EXPERTS_PER_GROUP), axis=0, keepdims=True)
    t = jnp.exp(v2 - v1)
    w1 = p_g / (1.0 + t)
    w2 = p_g * t / (1.0 + t)
    e1 = g_sel * EXPERTS_PER_GROUP + i1
    e2 = g_sel * EXPERTS_PER_GROUP + i2

    eio = lax.broadcasted_iota(I32, (N_EXPERTS, LANES), 0)
    base = cnt_ref[:, 0:1]
    zero_i = jnp.zeros((1, LANES), I32)
    n_blk = ts // LANES
    hits = []
    for k in range(n_blk):
        ls = slice(k * LANES, (k + 1) * LANES)
        hits.append((eio == e1[:, ls], eio == e2[:, ls]))
    onehots = [jnp.where(jnp.logical_or(a1, a2), 1.0, 0.0) for a1, a2 in hits]
    incl_all = jnp.dot(jnp.concatenate(onehots, axis=0).astype(BF16), prefix_ones,
                       preferred_element_type=F32).astype(I32)
    for k in range(n_blk):
        ls = slice(k * LANES, (k + 1) * LANES)
        is1, is2 = hits[k]
        onehot = onehots[k].astype(I32)
        incl = incl_all[k * N_EXPERTS:(k + 1) * N_EXPERTS]
        rank_e = base + incl - onehot
        r1 = jnp.sum(jnp.where(is1, rank_e, 0), axis=0, keepdims=True)
        r2 = jnp.sum(jnp.where(is2, rank_e, 0), axis=0, keepdims=True)
        ri_ref[:, ls] = jnp.concatenate(
            [e1[:, ls], e2[:, ls], r1, r2, zero_i, zero_i, zero_i, zero_i], axis=0)
        base = base + incl[:, LANES - 1:LANES]
    zero_f = jnp.zeros_like(w1)
    rw_ref[...] = jnp.concatenate([w1, w2, zero_f, zero_f, zero_f, zero_f, zero_f, zero_f], axis=0)
    cnt_new = jnp.broadcast_to(base, (N_EXPERTS, LANES))
    cnt_ref[...] = cnt_new
    cnt_out_ref[...] = cnt_new


def _mixer(x, a, qvt, gt, conv_w, w_out, norm_ffn, w_router, b_router):
    b_sz, s_len, d = x.shape
    ts = TS_MIXER
    dm = D_MLSTM

    def a_spec(j):
        return pl.BlockSpec((None, ts, dm), lambda b, s, j=j: (b, s, j))

    tok = lambda w: pl.BlockSpec((None, ts, w), lambda b, s: (b, s, 0))
    rowsp = pl.BlockSpec((None, 2 * N_HEADS, ts), lambda b, s: (b, 0, s))
    const = lambda shape: pl.BlockSpec(shape, lambda b, s: tuple(0 for _ in shape))
    wr_hi = w_router.astype(BF16)
    wr_lo = (w_router - wr_hi.astype(F32)).astype(BF16)
    w_router2 = jnp.concatenate([wr_hi, wr_lo], axis=1)
    return pl.pallas_call(
        _mixer_kernel,
        grid=(b_sz, s_len // ts),
        in_specs=[a_spec(0), a_spec(1), a_spec(2), a_spec(3), a_spec(4),
                  pl.BlockSpec((None, dm, ts), lambda b, s: (b, 0, s)),
                  pl.BlockSpec((None, dm, ts), lambda b, s: (b, 1, s)),
                  rowsp, tok(d),
                  const((3, D_CONV)), const((d, d)), const((1, d)),
                  const((d, 2 * ROUTER_COLS)), const((1, ROUTER_COLS))],
        out_specs=[tok(d), tok(d // 2), rowsp, rowsp, const((N_EXPERTS, LANES))],
        out_shape=[
            jax.ShapeDtypeStruct((b_sz, s_len, d), F32),
            jax.ShapeDtypeStruct((b_sz, s_len, d // 2), I32),
            jax.ShapeDtypeStruct((b_sz, 2 * N_HEADS, s_len), I32),
            jax.ShapeDtypeStruct((b_sz, 2 * N_HEADS, s_len), F32),
            jax.ShapeDtypeStruct((N_EXPERTS, LANES), I32),
        ],
        scratch_shapes=[
            pltpu.VMEM((N_HEADS, HEAD_DIM + AUG_ROWS, HEAD_DIM), F32),
            pltpu.VMEM((2 * N_HEADS, LANES), F32),
            pltpu.VMEM((8, D_CONV), F32),
            pltpu.VMEM((N_EXPERTS, LANES), I32),
            pltpu.VMEM((ts, d), BF16),
        ],
        compiler_params=pltpu.CompilerParams(
            dimension_semantics=("arbitrary", "arbitrary"), vmem_limit_bytes=VMEM_LIMIT),
        name="mixer",
    )(a, a, a, a, a, qvt, qvt, gt, x, conv_w, w_out.astype(BF16), norm_ffn[None, :], w_router2, b_router)


SC_WORKERS = 32
SC_CHUNK = 128


def _sc_worker_id():
    return lax.axis_index("s") * 2 + lax.axis_index("c")


def _sc_scatter_rows(rows, idx, n_out):
    n_rows, width = rows.shape
    per_worker = n_rows // SC_WORKERS
    n_chunks = per_worker // SC_CHUNK
    mesh = plsc.VectorSubcoreMesh(core_axis_name="c", subcore_axis_name="s")

    def body(rows_hbm, idx_hbm, out_hbm, idx_v, rows_v, sem):
        wid = _sc_worker_id()

        @pl.loop(0, n_chunks)
        def _(j):
            base = wid * per_worker + j * SC_CHUNK
            pltpu.sync_copy(rows_hbm.at[pl.ds(base, SC_CHUNK)], rows_v)
            for k in range(2):
                pltpu.sync_copy(idx_hbm.at[pl.ds(k * n_rows + base, SC_CHUNK)], idx_v)
                pltpu.async_copy(rows_v, out_hbm.at[idx_v], sem).wait()

    return pl.kernel(
        body,
        out_type=jax.ShapeDtypeStruct((n_out, width), I32),
        mesh=mesh,
        scratch_types=[pltpu.VMEM((SC_CHUNK,), I32), pltpu.VMEM((SC_CHUNK, width), I32),
                       pltpu.SemaphoreType.DMA],
        name="sc_scatter",
    )(rows, idx)


def _sc_gather_rows(table, idx):
    n_rows = idx.shape[0]
    width = table.shape[1]
    per_worker = n_rows // SC_WORKERS
    n_chunks = per_worker // SC_CHUNK
    mesh = plsc.VectorSubcoreMesh(core_axis_name="c", subcore_axis_name="s")

    def body(table_hbm, idx_hbm, out_hbm, idx_v, rows_v, sem):
        wid = _sc_worker_id()

        @pl.loop(0, n_chunks)
        def _(j):
            base = wid * per_worker + j * SC_CHUNK
            pltpu.sync_copy(idx_hbm.at[pl.ds(base, SC_CHUNK)], idx_v)
            pltpu.async_copy(table_hbm.at[idx_v], rows_v, sem).wait()
            pltpu.sync_copy(rows_v, out_hbm.at[pl.ds(base, SC_CHUNK)])

    return pl.kernel(
        body,
        out_type=jax.ShapeDtypeStruct((n_rows, width), I32),
        mesh=mesh,
        scratch_types=[pltpu.VMEM((SC_CHUNK,), I32), pltpu.VMEM((SC_CHUNK, width), I32),
                       pltpu.SemaphoreType.DMA],
        name="sc_gather",
    )(table, idx)


def _experts_kernel(te_ref, nu_ref, tv_ref, ne_ref, xs_ref, wg_hbm, wu_hbm, wd_hbm, y_ref,
                    wgs_ref, wus_ref, wds_ref, wgb_ref, wub_ref, wdb_ref, wsem):
    i = pl.program_id(0)
    e = te_ref[i]
    e_prev = te_ref[jnp.maximum(i - 1, 0)]
    new_expert = jnp.logical_or(i == 0, e != e_prev)
    used = i < nu_ref[0]

    def weight_copies(expert):
        return (pltpu.make_async_copy(wg_hbm.at[expert], wgs_ref, wsem),
                pltpu.make_async_copy(wu_hbm.at[expert], wus_ref, wsem),
                pltpu.make_async_copy(wd_hbm.at[expert], wds_ref, wsem))

    @pl.when(i == 0)
    def _():
        for c in weight_copies(e):
            c.start()

    @pl.when(jnp.logical_and(used, new_expert))
    def _():
        for c in weight_copies(e):
            c.wait()
        wgb_ref[...] = wgs_ref[...].astype(BF16)
        wub_ref[...] = wus_ref[...].astype(BF16)
        wdb_ref[...] = wds_ref[...].astype(BF16)

        @pl.when(ne_ref[i] >= 0)
        def _():
            for c in weight_copies(ne_ref[i]):
                c.start()

    @pl.when(used)
    def _():
        row = lax.broadcasted_iota(I32, xs_ref.shape, 0)
        x_lo, x_hi = _unpack_rows(jnp.where(row < tv_ref[i], xs_ref[...], 0))
        xb = jnp.concatenate([x_lo.astype(BF16), x_hi.astype(BF16)], axis=1)
        gate = jnp.dot(xb, wgb_ref[...], preferred_element_type=F32)
        up = jnp.dot(xb, wub_ref[...], preferred_element_type=F32)
        hid = (gate * jax.nn.sigmoid(gate) * up).astype(BF16)
        y_ref[...] = _pack_rows(jnp.dot(hid, wdb_ref[...], preferred_element_type=F32))

    @pl.when(jnp.logical_not(used))
    def _():
        y_ref[...] = jnp.zeros_like(y_ref)


def _experts(xs, tile_e, n_used, tile_valid, next_e, w_gate, w_up, w_down):
    cap, dw = xs.shape
    d = 2 * dw
    tm = TM_EXPERT
    n_tiles = cap // tm

    def x_map(i, te, nu, tv, ne):
        return (jnp.minimum(i, jnp.maximum(nu[0] - 1, 0)), 0)

    return pl.pallas_call(
        _experts_kernel,
        grid_spec=pltpu.PrefetchScalarGridSpec(
            num_scalar_prefetch=4,
            grid=(n_tiles,),
            in_specs=[
                pl.BlockSpec((tm, dw), x_map),
                pl.BlockSpec(memory_space=pl.ANY),
                pl.BlockSpec(memory_space=pl.ANY),
                pl.BlockSpec(memory_space=pl.ANY),
            ],
            out_specs=pl.BlockSpec((tm, dw), lambda i, te, nu, tv, ne: (i, 0)),
            scratch_shapes=[pltpu.VMEM((d, D_FF), F32), pltpu.VMEM((d, D_FF), F32), pltpu.VMEM((D_FF, d), F32),
                            pltpu.VMEM((d, D_FF), BF16), pltpu.VMEM((d, D_FF), BF16), pltpu.VMEM((D_FF, d), BF16),
                            pltpu.SemaphoreType.DMA(())],
        ),
        out_shape=jax.ShapeDtypeStruct((cap, dw), I32),
        compiler_params=pltpu.CompilerParams(
            dimension_semantics=("arbitrary",), vmem_limit_bytes=VMEM_LIMIT),
        name="experts",
    )(tile_e, n_used, tile_valid, next_e, xs, w_gate, w_up, w_down)


def _combine_stream_kernel(x2_ref, y1_ref, y2_ref, rw_ref, gf_ref, out_ref):
    w_t = jnp.transpose(rw_ref[...])
    a_lo, a_hi = _unpack_rows(y1_ref[...])
    b_lo, b_hi = _unpack_rows(y2_ref[...])
    w1 = w_t[:, 0:1]
    w2 = w_t[:, 1:2]
    moe = jnp.concatenate([w1 * a_lo + w2 * b_lo, w1 * a_hi + w2 * b_hi], axis=1)
    x3 = x2_ref[...] + moe
    ms = jnp.mean(x3 * x3, axis=-1, keepdims=True)
    out_ref[...] = x3 * lax.rsqrt(ms + RMS_EPS) * gf_ref[...]


def _combine_stream(x2, yg, rw, norm_final):
    b_sz, s_len, d = x2.shape
    tc = TC_COMBINE
    n_s = s_len // tc
    n_steps = b_sz * n_s
    return pl.pallas_call(
        _combine_stream_kernel,
        grid=(n_steps,),
        in_specs=[
            pl.BlockSpec((tc, d), lambda g: (g, 0)),
            pl.BlockSpec((tc, d // 2), lambda g: (g, 0)),
            pl.BlockSpec((tc, d // 2), lambda g: (g + n_steps, 0)),
            pl.BlockSpec((None, 2 * N_HEADS, tc), lambda g: (g // n_s, 0, g % n_s)),
            pl.BlockSpec((1, d), lambda g: (0, 0)),
        ],
        out_specs=pl.BlockSpec((tc, d), lambda g: (g, 0)),
        out_shape=jax.ShapeDtypeStruct((b_sz * s_len, d), F32),
        compiler_params=pltpu.CompilerParams(
            dimension_semantics=("arbitrary",), vmem_limit_bytes=VMEM_LIMIT),
        name="combine_stream",
    )(x2.reshape(b_sz * s_len, d), yg, yg, rw, norm_final[None, :]).reshape(b_sz, s_len, d)


def _layer(x, norm_mix, w_in, b_in, conv_w, w_out, norm_ffn, w_group, b_group, w_expert, b_expert,
           w_gate, w_up, w_down, norm_out):
    b_sz, s_len, d = x.shape
    n_tok = b_sz * s_len
    a, qvt, gt = _inproj(x, norm_mix, w_in, b_in)

    w_router = jnp.zeros((d, ROUTER_COLS), F32)
    w_router = w_router.at[:, 0:N_GROUPS].set(w_group).at[:, EXPERT_COL0:EXPERT_COL0 + N_EXPERTS].set(w_expert)
    b_router = jnp.zeros((1, ROUTER_COLS), F32)
    b_router = b_router.at[0, 0:N_GROUPS].set(b_group).at[0, EXPERT_COL0:EXPERT_COL0 + N_EXPERTS].set(b_expert)
    x2, hn2, ri, rw, counts = _mixer(x, a, qvt, gt, conv_w, w_out, norm_ffn, w_router, b_router)

    tm = TM_EXPERT
    cnt = counts[:, 0]
    padded = ((cnt + tm - 1) // tm) * tm
    pends = jnp.cumsum(padded).astype(I32)
    seg = jnp.concatenate([jnp.zeros((1,), I32), pends])
    n_tiles = (n_tok * 2) // tm + N_EXPERTS
    cap = n_tiles * tm
    tile_start = jnp.arange(n_tiles, dtype=I32) * tm
    tile_e = jnp.minimum(jnp.sum((tile_start[:, None] >= pends[None, :]).astype(I32), axis=1),
                         N_EXPERTS - 1)
    n_used = pends[-1:] // tm
    e_sel = ri[:, 0:2, :]
    seg_start = jnp.sum(jnp.where(e_sel[..., None] == jnp.arange(N_EXPERTS, dtype=I32), seg[:-1], 0), axis=-1)
    pos = ri[:, 2:4, :] + seg_start

    seg_tok_end = seg[:-1] + cnt.astype(I32)
    tile_tok_end = jnp.sum(jnp.where(tile_e[:, None] == jnp.arange(N_EXPERTS, dtype=I32), seg_tok_end, 0), axis=-1)
    tile_valid = jnp.clip(tile_tok_end - tile_start, 0, tm).astype(I32)
    pos_flat = jnp.concatenate([pos[:, 0, :].reshape(-1), pos[:, 1, :].reshape(-1)])

    expert_ids = jnp.arange(N_EXPERTS, dtype=I32)
    seg_end_tile = jnp.sum(jnp.where(tile_e[:, None] == expert_ids, pends // tm, 0), axis=-1)
    tile_ids = jnp.arange(n_tiles, dtype=I32)
    e_after = jnp.sum(jnp.where(seg_end_tile[:, None] == tile_ids, tile_e, 0), axis=-1)
    next_e = jnp.where(seg_end_tile < n_used[0], e_after, -1).astype(I32)

    xs = _sc_scatter_rows(hn2.reshape(n_tok, d // 2), pos_flat, cap)
    y = _experts(xs, tile_e, n_used, tile_valid, next_e, w_gate, w_up, w_down)
    yg = _sc_gather_rows(y, pos_flat)
    return _combine_stream(x2, yg, rw, norm_out)


def kernel(x, norm_mix, w_in, b_in, conv_w, w_out, norm_ffn, w_group, b_group, w_expert, b_expert,
           w_gate, w_up, w_down, norm_final):
    depth = norm_mix.shape[0]
    assert depth == 1, "final-norm fusion below assumes a single layer"
    assert x.shape[-1] == 2 * D_MLSTM and x.shape[1] % TS_MIXER == 0
    return _layer(x, norm_mix[0], w_in[0], b_in[0], conv_w[0], w_out[0], norm_ffn[0],
                  w_group[0], b_group[0], w_expert[0], b_expert[0],
                  w_gate[0], w_up[0], w_down[0], norm_final)
```

```python
import jax
import jax.numpy as jnp
from jax import lax
from jax.experimental import pallas as pl
from jax.experimental.pallas import tpu as pltpu
from jax.experimental.pallas import tpu_sc as plsc

F32 = jnp.float32
BF16 = jnp.bfloat16
I32 = jnp.int32

N_HEADS = 4
HEAD_DIM = 128
D_MLSTM = N_HEADS * HEAD_DIM
D_CONV = 512
N_GROUPS = 4
EXPERTS_PER_GROUP = 8
N_EXPERTS = N_GROUPS * EXPERTS_PER_GROUP
D_FF = 512
RMS_EPS = 1e-6
Q_SCALE = HEAD_DIM ** -0.5

LANES = 128
CHUNK = LANES
AUG_ROWS = 16
ROUTER_COLS = 128
EXPERT_COL0 = 8

TM_INPROJ = 512
TS_MIXER = 512
OUT_COLS = 256
TM_EXPERT = 512
TC_COMBINE = 512

VMEM_LIMIT = 56 * 1024 * 1024


HI_HALF = -65536


def _pack_rows(x):
    w = x.shape[1] // 2
    lo = lax.shift_right_logical(lax.bitcast_convert_type(x[:, :w].astype(BF16).astype(F32), I32), 16)
    hi = lax.bitcast_convert_type(x[:, w:].astype(BF16).astype(F32), I32) & HI_HALF
    return lo | hi


def _unpack_rows(words):
    lo = lax.bitcast_convert_type(words << 16, F32)
    hi = lax.bitcast_convert_type(words & HI_HALF, F32)
    return lo, hi


def _nt_dot(a, b):
    return lax.dot_general(a, b, (((1,), (1,)), ((), ())), preferred_element_type=F32)


def _inproj_kernel(x_ref, g_ref, wa_ref, ba_ref, wqv_ref, bqv_ref, wg_ref, bg_ref,
                   a_ref, qvt_ref, gt_ref):
    x = x_ref[...]
    ms = jnp.mean(x * x, axis=-1, keepdims=True)
    hn = (x * lax.rsqrt(ms + RMS_EPS) * g_ref[...]).astype(BF16)
    n_blk = wa_ref.shape[1] // D_MLSTM
    for j in range(n_blk):
        sl = slice(j * D_MLSTM, (j + 1) * D_MLSTM)
        acc = jnp.dot(hn, wa_ref[:, sl], preferred_element_type=F32) + ba_ref[:, sl]
        a_ref[:, sl] = acc.astype(BF16)
    q_rows = slice(0, D_MLSTM)
    v_rows = slice(D_MLSTM, 2 * D_MLSTM)
    qt = (_nt_dot(wqv_ref[q_rows, :], hn) + bqv_ref[q_rows, :]) * Q_SCALE
    qvt_ref[q_rows, :] = qt.astype(BF16)
    qvt_ref[v_rows, :] = (_nt_dot(wqv_ref[v_rows, :], hn) + bqv_ref[v_rows, :]).astype(BF16)
    gt_ref[...] = _nt_dot(wg_ref[...], hn) + bg_ref[...]


def _inproj(x, norm_mix, w_in, b_in):
    b_sz, s_len, d = x.shape
    tm = TM_INPROJ
    dm = D_MLSTM
    g0 = 4 * dm
    c0 = g0 + 2 * N_HEADS
    cols_a = jnp.concatenate([w_in[:, dm:2 * dm], w_in[:, 3 * dm:4 * dm], w_in[:, c0:c0 + 3 * D_CONV]], axis=1)
    bias_a = jnp.concatenate([b_in[dm:2 * dm], b_in[3 * dm:4 * dm], b_in[c0:c0 + 3 * D_CONV]])[None, :]
    wqv_t = jnp.concatenate([w_in[:, 0:dm], w_in[:, 2 * dm:3 * dm]], axis=1).T
    bqv = jnp.concatenate([b_in[0:dm], b_in[2 * dm:3 * dm]])[:, None]
    wg_t = w_in[:, g0:c0].T
    bg = b_in[g0:c0][:, None]
    n_a = cols_a.shape[1]
    return pl.pallas_call(
        _inproj_kernel,
        grid=(b_sz, s_len // tm),
        in_specs=[
            pl.BlockSpec((None, tm, d), lambda b, s: (b, s, 0)),
            pl.BlockSpec((1, d), lambda b, s: (0, 0)),
            pl.BlockSpec((d, n_a), lambda b, s: (0, 0)),
            pl.BlockSpec((1, n_a), lambda b, s: (0, 0)),
            pl.BlockSpec((2 * dm, d), lambda b, s: (0, 0)),
            pl.BlockSpec((2 * dm, 1), lambda b, s: (0, 0)),
            pl.BlockSpec((2 * N_HEADS, d), lambda b, s: (0, 0)),
            pl.BlockSpec((2 * N_HEADS, 1), lambda b, s: (0, 0)),
        ],
        out_specs=[
            pl.BlockSpec((None, tm, n_a), lambda b, s: (b, s, 0)),
            pl.BlockSpec((None, 2 * dm, tm), lambda b, s: (b, 0, s)),
            pl.BlockSpec((None, 2 * N_HEADS, tm), lambda b, s: (b, 0, s)),
        ],
        out_shape=[
            jax.ShapeDtypeStruct((b_sz, s_len, n_a), BF16),
            jax.ShapeDtypeStruct((b_sz, 2 * dm, s_len), BF16),
            jax.ShapeDtypeStruct((b_sz, 2 * N_HEADS, s_len), F32),
        ],
        compiler_params=pltpu.CompilerParams(
            dimension_semantics=("arbitrary", "arbitrary"), vmem_limit_bytes=VMEM_LIMIT),
        name="inproj",
    )(x, norm_mix[None, :], cols_a.astype(BF16), bias_a, wqv_t.astype(BF16), bqv, wg_t.astype(BF16), bg)


def _mixer_kernel(k_ref, o_ref, cb_ref, cc_ref, cx_ref, qt_ref, vt_ref, gt_ref, x_ref,
                  cw_ref, wo_ref, g2_ref, wr_ref, br_ref,
                  x2_ref, hn2_ref, ri_ref, rw_ref, cnt_out_ref,
                  ct_ref, m_ref, carry_ref, cnt_ref, mix_ref):
    ts = x_ref.shape[0]
    n_chunks = ts // CHUNK
    first_tile = pl.program_id(1) == 0

    @pl.when(first_tile)
    def _():
        ct_ref[...] = jnp.zeros_like(ct_ref)
        m_ref[...] = jnp.zeros_like(m_ref)
        carry_ref[...] = jnp.zeros_like(carry_ref)

    @pl.when(jnp.logical_and(first_tile, pl.program_id(0) == 0))
    def _():
        cnt_ref[...] = jnp.zeros_like(cnt_ref)

    gates = gt_ref[...]
    li_all = gates[0:N_HEADS]
    fg = gates[N_HEADS:2 * N_HEADS]
    lf_all = jnp.minimum(fg, 0.0) - jnp.log1p(jnp.exp(-jnp.abs(fg)))
    key_j = lax.broadcasted_iota(I32, (CHUNK, CHUNK), 0)
    qry_i = lax.broadcasted_iota(I32, (CHUNK, CHUNK), 1)
    causal_t = key_j <= qry_i
    neg_inf = jnp.float32(-jnp.inf)
    ones_rows = (lax.broadcasted_iota(I32, (AUG_ROWS, CHUNK), 0) == 0).astype(BF16)

    prefix_ones = (key_j <= qry_i).astype(BF16)

    lf_hi = lf_all.astype(BF16)
    rem = lf_all - lf_hi.astype(F32)
    lf_mid = rem.astype(BF16)
    lf_lo = (rem - lf_mid.astype(F32)).astype(BF16)
    zero_bf = jnp.zeros((N_HEADS, CHUNK), BF16)
    parts = []
    for c in range(n_chunks):
        rows = slice(c * CHUNK, (c + 1) * CHUNK)
        parts += [lf_hi[:, rows], lf_mid[:, rows], lf_lo[:, rows], zero_bf]
    cum = jnp.dot(jnp.concatenate(parts, axis=0), prefix_ones, preferred_element_type=F32)

    m_prev = m_ref[0:N_HEADS, 0:1]
    for c in range(n_chunks):
        rows = slice(c * CHUNK, (c + 1) * CHUNK)
        li = li_all[:, rows]
        r0 = 4 * N_HEADS * c
        b_cum = (cum[r0:r0 + N_HEADS] + cum[r0 + N_HEADS:r0 + 2 * N_HEADS]
                 + cum[r0 + 2 * N_HEADS:r0 + 3 * N_HEADS])
        g_tot = b_cum[:, CHUNK - 1:CHUNK]
        a_end = g_tot - b_cum + li
        m_loc = jnp.max(a_end, axis=1, keepdims=True)
        u = li - b_cum
        m_new = jnp.maximum(g_tot + m_prev, m_loc)
        decay = jnp.exp(g_tot + m_prev - m_new)
        s_fac = jnp.exp(m_loc - m_new)
        w_key = jnp.exp(a_end - m_loc)
        u_cols = jnp.transpose(jnp.concatenate([u, jnp.zeros_like(u)], axis=0))
        for h in range(N_HEADS):
            hc = slice(h * HEAD_DIM, (h + 1) * HEAD_DIM)
            kh = k_ref[rows, hc]
            qth = qt_ref[hc, rows]
            vt_aug = jnp.concatenate([vt_ref[hc, rows], ones_rows], axis=0)
            s_t = jnp.dot(kh, qth, preferred_element_type=F32)
            u_t = jnp.where(causal_t, u_cols[:, h:h + 1], neg_inf)
            mp = m_prev[h:h + 1, :]
            m_tot = jnp.maximum(jnp.max(u_t, axis=0, keepdims=True), mp)
            f_inter = jnp.exp(mp - m_tot)
            floor = jnp.exp(-b_cum[h:h + 1, :] - m_tot)
            p_t = (s_t * jnp.exp(u_t - m_tot)).astype(BF16)
            q_f = (qth.astype(F32) * f_inter).astype(BF16)
            c_old = ct_ref[h]
            lhs = jnp.concatenate([vt_aug, c_old.astype(BF16)], axis=1)
            rhs = jnp.concatenate([p_t, q_f], axis=0)
            r_t = jnp.dot(lhs, rhs, preferred_element_type=F32)
            vtw = (vt_aug.astype(F32) * w_key[h:h + 1, :]).astype(BF16)
            kv = jnp.dot(vtw, kh, preferred_element_type=F32)
            ct_ref[h] = decay[h:h + 1, :] * c_old + s_fac[h:h + 1, :] * kv

            den = r_t[HEAD_DIM:HEAD_DIM + 1, :]
            inv = 1.0 / jnp.maximum(jnp.abs(den), floor)
            h_nat = jnp.transpose(r_t[0:HEAD_DIM, :] * inv)
            o_gate = jax.nn.sigmoid(o_ref[rows, hc].astype(F32))
            mix_ref[rows, hc] = (o_gate * h_nat).astype(BF16)
        m_prev = m_new
    m_ref[0:N_HEADS, :] = jnp.broadcast_to(m_prev, (N_HEADS, LANES))

    u_c = cc_ref[...].astype(F32) * cx_ref[...].astype(F32)
    rid = lax.broadcasted_iota(I32, u_c.shape, 0)
    prev1 = carry_ref[0:1, :]
    prev2 = carry_ref[1:2, :]
    u1 = jnp.where(rid == 0, prev1, pltpu.roll(u_c, 1, axis=0))
    u2 = jnp.where(rid == 0, prev2, jnp.where(rid == 1, prev1, pltpu.roll(u_c, 2, axis=0)))
    z = cw_ref[0:1, :] * u2 + cw_ref[1:2, :] * u1 + cw_ref[2:3, :] * u_c
    mix_ref[:, D_MLSTM:] = (cb_ref[...].astype(F32) * z).astype(BF16)
    carry_ref[0:1, :] = u_c[ts - 1:ts, :]
    carry_ref[1:2, :] = u_c[ts - 2:ts - 1, :]

    d_model = x_ref.shape[1]
    mix = mix_ref[...]
    ssq = jnp.zeros((ts, 1), F32)
    for j in range(d_model // OUT_COLS):
        cs = slice(j * OUT_COLS, (j + 1) * OUT_COLS)
        x2c = x_ref[:, cs] + jnp.dot(mix, wo_ref[:, cs], preferred_element_type=F32)
        x2_ref[:, cs] = x2c
        ssq = ssq + jnp.sum(x2c * x2c, axis=-1, keepdims=True)
    hn2 = x2_ref[...] * lax.rsqrt(ssq * (1.0 / d_model) + RMS_EPS) * g2_ref[...]
    hn2_ref[...] = _pack_rows(hn2)

    h_hi = hn2.astype(BF16)
    h_lo = (hn2 - h_hi.astype(F32)).astype(BF16)
    hi_both = jnp.dot(h_hi, wr_ref[...], preferred_element_type=F32)
    lo_hi = jnp.dot(h_lo, wr_ref[:, 0:ROUTER_COLS], preferred_element_type=F32)
    logits = hi_both[:, 0:ROUTER_COLS] + (hi_both[:, ROUTER_COLS:] + lo_hi) + br_ref[...]
    lt = jnp.transpose(logits)
    gl = lt[0:N_GROUPS]
    gmax = jnp.max(gl, axis=0, keepdims=True)
    gi = lax.broadcasted_iota(I32, gl.shape, 0)
    g_sel = jnp.min(jnp.where(gl == gmax, gi, N_GROUPS), axis=0, keepdims=True)
    p_g = 1.0 / jnp.sum(jnp.exp(gl - gmax), axis=0, keepdims=True)
    in_g = lt[EXPERT_COL0:EXPERT_COL0 + EXPERTS_PER_GROUP]
    for g in range(1, N_GROUPS):
        lo = EXPERT_COL0 + g * EXPERTS_PER_GROUP
        in_g = jnp.where(g_sel == g, lt[lo:lo + EXPERTS_PER_GROUP], in_g)
    ei = lax.broadcasted_iota(I32, in_g.shape, 0)
    v1 = jnp.max(in_g, axis=0, keepdims=True)
    i1 = jnp.min(jnp.where(in_g == v1, ei, EXPERTS_PER_GROUP), axis=0, keepdims=True)
    rest = jnp.where(ei == i1, neg_inf, in_g)
    v2 = jnp.max(rest, axis=0, keepdims=True)
    i2 = jnp.min(jnp.where(rest == v2, ei, EXPERTS_PER_GROUP), axis=0, keepdims=True)
    t = jnp.exp(v2 - v1)
    w1 = p_g / (1.0 + t)
    w2 = p_g * t / (1.0 + t)
    e1 = g_sel * EXPERTS_PER_GROUP + i1
    e2 = g_sel * EXPERTS_PER_GROUP + i2

    eio = lax.broadcasted_iota(I32, (N_EXPERTS, LANES), 0)
    base = cnt_ref[:, 0:1]
    zero_i = jnp.zeros((1, LANES), I32)
    n_blk = ts // LANES
    hits = []
    for k in range(n_blk):
        ls = slice(k * LANES, (k + 1) * LANES)
        hits.append((eio == e1[:, ls], eio == e2[:, ls]))
    onehots = [jnp.where(jnp.logical_or(a1, a2), 1.0, 0.0) for a1, a2 in hits]
    incl_all = jnp.dot(jnp.concatenate(onehots, axis=0).astype(BF16), prefix_ones,
                       preferred_element_type=F32).astype(I32)
    for k in range(n_blk):
        ls = slice(k * LANES, (k + 1) * LANES)
        is1, is2 = hits[k]
        onehot = onehots[k].astype(I32)
        incl = incl_all[k * N_EXPERTS:(k + 1) * N_EXPERTS]
        rank_e = base + incl - onehot
        r1 = jnp.sum(jnp.where(is1, rank_e, 0), axis=0, keepdims=True)
        r2 = jnp.sum(jnp.where(is2, rank_e, 0), axis=0, keepdims=True)
        ri_ref[:, ls] = jnp.concatenate(
            [e1[:, ls], e2[:, ls], r1, r2, zero_i, zero_i, zero_i, zero_i], axis=0)
        base = base + incl[:, LANES - 1:LANES]
    zero_f = jnp.zeros_like(w1)
    rw_ref[...] = jnp.concatenate([w1, w2, zero_f, zero_f, zero_f, zero_f, zero_f, zero_f], axis=0)
    cnt_new = jnp.broadcast_to(base, (N_EXPERTS, LANES))
    cnt_ref[...] = cnt_new
    cnt_out_ref[...] = cnt_new


def _mixer(x, a, qvt, gt, conv_w, w_out, norm_ffn, w_router, b_router):
    b_sz, s_len, d = x.shape
    ts = TS_MIXER
    dm = D_MLSTM

    def a_spec(j):
        return pl.BlockSpec((None, ts, dm), lambda b, s, j=j: (b, s, j))

    tok = lambda w: pl.BlockSpec((None, ts, w), lambda b, s: (b, s, 0))
    rowsp = pl.BlockSpec((None, 2 * N_HEADS, ts), lambda b, s: (b, 0, s))
    const = lambda shape: pl.BlockSpec(shape, lambda b, s: tuple(0 for _ in shape))
    wr_hi = w_router.astype(BF16)
    wr_lo = (w_router - wr_hi.astype(F32)).astype(BF16)
    w_router2 = jnp.concatenate([wr_hi, wr_lo], axis=1)
    return pl.pallas_call(
        _mixer_kernel,
        grid=(b_sz, s_len // ts),
        in_specs=[a_spec(0), a_spec(1), a_spec(2), a_spec(3), a_spec(4),
                  pl.BlockSpec((None, dm, ts), lambda b, s: (b, 0, s)),
                  pl.BlockSpec((None, dm, ts), lambda b, s: (b, 1, s)),
                  rowsp, tok(d),
                  const((3, D_CONV)), const((d, d)), const((1, d)),
                  const((d, 2 * ROUTER_COLS)), const((1, ROUTER_COLS))],
        out_specs=[tok(d), tok(d // 2), rowsp, rowsp, const((N_EXPERTS, LANES))],
        out_shape=[
            jax.ShapeDtypeStruct((b_sz, s_len, d), F32),
            jax.ShapeDtypeStruct((b_sz, s_len, d // 2), I32),
            jax.ShapeDtypeStruct((b_sz, 2 * N_HEADS, s_len), I32),
            jax.ShapeDtypeStruct((b_sz, 2 * N_HEADS, s_len), F32),
            jax.ShapeDtypeStruct((N_EXPERTS, LANES), I32),
        ],
        scratch_shapes=[
            pltpu.VMEM((N_HEADS, HEAD_DIM + AUG_ROWS, HEAD_DIM), F32),
            pltpu.VMEM((2 * N_HEADS, LANES), F32),
            pltpu.VMEM((8, D_CONV), F32),
            pltpu.VMEM((N_EXPERTS, LANES), I32),
            pltpu.VMEM((ts, d), BF16),
        ],
        compiler_params=pltpu.CompilerParams(
            dimension_semantics=("arbitrary", "arbitrary"), vmem_limit_bytes=VMEM_LIMIT),
        name="mixer",
    )(a, a, a, a, a, qvt, qvt, gt, x, conv_w, w_out.astype(BF16), norm_ffn[None, :], w_router2, b_router)


SC_WORKERS = 32
SC_CHUNK = 64


def _sc_worker_id():
    return lax.axis_index("s") * 2 + lax.axis_index("c")


def _sc_scatter_rows(rows, idx, n_out):
    n_rows, width = rows.shape
    per_worker = n_rows // SC_WORKERS
    n_chunks = per_worker // SC_CHUNK
    assert n_chunks % 2 == 0
    mesh = plsc.VectorSubcoreMesh(core_axis_name="c", subcore_axis_name="s")

    def body(rows_hbm, idx_hbm, out_hbm, ia0, ib0, r0, ia1, ib1, r1, ls0, ls1, ss0, ss1):
        wid = _sc_worker_id()
        bufs = ((ia0, ib0, r0, ls0, ss0), (ia1, ib1, r1, ls1, ss1))

        def loads(j, b):
            ia, ib, r, ls, _ = bufs[b]
            base = pl.multiple_of(wid * per_worker + j * SC_CHUNK, SC_CHUNK)
            return (pltpu.make_async_copy(rows_hbm.at[pl.ds(base, SC_CHUNK)], r, ls),
                    pltpu.make_async_copy(idx_hbm.at[pl.ds(base, SC_CHUNK)], ia, ls),
                    pltpu.make_async_copy(idx_hbm.at[pl.ds(n_rows + base, SC_CHUNK)], ib, ls))

        def scatters(b):
            ia, ib, r, _, ss = bufs[b]
            return (pltpu.make_async_copy(r, out_hbm.at[ia], ss),
                    pltpu.make_async_copy(r, out_hbm.at[ib], ss))

        for c in loads(0, 0):
            c.start()

        @pl.loop(0, n_chunks, step=2)
        def _(j0):
            for b in range(2):
                j = j0 + b
                for c in loads(j, b):
                    c.wait()

                @pl.when(j >= 1)
                def _():
                    for c in scatters(1 - b):
                        c.wait()

                @pl.when(j + 1 < n_chunks)
                def _():
                    for c in loads(j + 1, 1 - b):
                        c.start()

                for c in scatters(b):
                    c.start()

        for c in scatters((n_chunks - 1) % 2):
            c.wait()

    idx_t = pltpu.VMEM((SC_CHUNK,), I32)
    row_t = pltpu.VMEM((SC_CHUNK, width), I32)
    sem_t = pltpu.SemaphoreType.DMA
    return pl.kernel(
        body,
        out_type=jax.ShapeDtypeStruct((n_out, width), I32),
        mesh=mesh,
        scratch_types=[idx_t, idx_t, row_t, idx_t, idx_t, row_t, sem_t, sem_t, sem_t, sem_t],
        name="sc_scatter",
    )(rows, idx)


def _sc_gather_rows(table, idx):
    n_rows = idx.shape[0]
    width = table.shape[1]
    per_worker = n_rows // SC_WORKERS
    n_chunks = per_worker // SC_CHUNK
    assert n_chunks % 2 == 0
    mesh = plsc.VectorSubcoreMesh(core_axis_name="c", subcore_axis_name="s")

    def body(table_hbm, idx_hbm, out_hbm, i0, r0, i1, r1, is0, is1, gs0, gs1, ws0, ws1):
        wid = _sc_worker_id()
        bufs = ((i0, r0, is0, gs0, ws0), (i1, r1, is1, gs1, ws1))

        def chunk_base(j):
            return pl.multiple_of(wid * per_worker + j * SC_CHUNK, SC_CHUNK)

        def idx_load(j, b):
            return pltpu.make_async_copy(idx_hbm.at[pl.ds(chunk_base(j), SC_CHUNK)], bufs[b][0], bufs[b][2])

        def gather(b):
            return pltpu.make_async_copy(table_hbm.at[bufs[b][0]], bufs[b][1], bufs[b][3])

        def writeback(j, b):
            return pltpu.make_async_copy(bufs[b][1], out_hbm.at[pl.ds(chunk_base(j), SC_CHUNK)], bufs[b][4])

        idx_load(0, 0).start()
        idx_load(0, 0).wait()
        gather(0).start()
        idx_load(1, 1).start()

        @pl.loop(0, n_chunks, step=2)
        def _(j0):
            for b in range(2):
                j = j0 + b

                @pl.when(j + 1 < n_chunks)
                def _():
                    idx_load(j + 1, 1 - b).wait()

                    @pl.when(j >= 1)
                    def _():
                        writeback(j - 1, 1 - b).wait()
                    gather(1 - b).start()

                gather(b).wait()
                writeback(j, b).start()

                @pl.when(j + 2 < n_chunks)
                def _():
                    idx_load(j + 2, b).start()

        writeback(n_chunks - 2, 0).wait()
        writeback(n_chunks - 1, 1).wait()

    idx_t = pltpu.VMEM((SC_CHUNK,), I32)
    row_t = pltpu.VMEM((SC_CHUNK, width), I32)
    sem_t = pltpu.SemaphoreType.DMA
    return pl.kernel(
        body,
        out_type=jax.ShapeDtypeStruct((n_rows, width), I32),
        mesh=mesh,
        scratch_types=[idx_t, row_t, idx_t, row_t, sem_t, sem_t, sem_t, sem_t, sem_t, sem_t],
        name="sc_gather",
    )(table, idx)


def _experts_kernel(te_ref, nu_ref, tv_ref, ne_ref, xs_ref, wg_hbm, wu_hbm, wd_hbm, y_ref,
                    wgs_ref, wus_ref, wds_ref, wgb_ref, wub_ref, wdb_ref, wsem):
    i = pl.program_id(0)
    e = te_ref[i]
    e_prev = te_ref[jnp.maximum(i - 1, 0)]
    new_expert = jnp.logical_or(i == 0, e != e_prev)
    used = i < nu_ref[0]

    def weight_copies(expert):
        return (pltpu.make_async_copy(wg_hbm.at[expert], wgs_ref, wsem),
                pltpu.make_async_copy(wu_hbm.at[expert], wus_ref, wsem),
                pltpu.make_async_copy(wd_hbm.at[expert], wds_ref, wsem))

    @pl.when(i == 0)
    def _():
        for c in weight_copies(e):
            c.start()

    @pl.when(jnp.logical_and(used, new_expert))
    def _():
        for c in weight_copies(e):
            c.wait()
        wgb_ref[...] = wgs_ref[...].astype(BF16)
        wub_ref[...] = wus_ref[...].astype(BF16)
        wdb_ref[...] = wds_ref[...].astype(BF16)

        @pl.when(ne_ref[i] >= 0)
        def _():
            for c in weight_copies(ne_ref[i]):
                c.start()

    @pl.when(used)
    def _():
        row = lax.broadcasted_iota(I32, xs_ref.shape, 0)
        x_lo, x_hi = _unpack_rows(jnp.where(row < tv_ref[i], xs_ref[...], 0))
        xb = jnp.concatenate([x_lo.astype(BF16), x_hi.astype(BF16)], axis=1)
        gate = jnp.dot(xb, wgb_ref[...], preferred_element_type=F32)
        up = jnp.dot(xb, wub_ref[...], preferred_element_type=F32)
        hid = (gate * jax.nn.sigmoid(gate) * up).astype(BF16)
        y_ref[...] = _pack_rows(jnp.dot(hid, wdb_ref[...], preferred_element_type=F32))

    @pl.when(jnp.logical_not(used))
    def _():
        y_ref[...] = jnp.zeros_like(y_ref)


def _experts(xs, tile_e, n_used, tile_valid, next_e, w_gate, w_up, w_down):
    cap, dw = xs.shape
    d = 2 * dw
    tm = TM_EXPERT
    n_tiles = cap // tm

    def x_map(i, te, nu, tv, ne):
        return (jnp.minimum(i, jnp.maximum(nu[0] - 1, 0)), 0)

    return pl.pallas_call(
        _experts_kernel,
        grid_spec=pltpu.PrefetchScalarGridSpec(
            num_scalar_prefetch=4,
            grid=(n_tiles,),
            in_specs=[
                pl.BlockSpec((tm, dw), x_map),
                pl.BlockSpec(memory_space=pl.ANY),
                pl.BlockSpec(memory_space=pl.ANY),
                pl.BlockSpec(memory_space=pl.ANY),
            ],
            out_specs=pl.BlockSpec((tm, dw), lambda i, te, nu, tv, ne: (i, 0)),
            scratch_shapes=[pltpu.VMEM((d, D_FF), F32), pltpu.VMEM((d, D_FF), F32), pltpu.VMEM((D_FF, d), F32),
                            pltpu.VMEM((d, D_FF), BF16), pltpu.VMEM((d, D_FF), BF16), pltpu.VMEM((D_FF, d), BF16),
                            pltpu.SemaphoreType.DMA(())],
        ),
        out_shape=jax.ShapeDtypeStruct((cap, dw), I32),
        compiler_params=pltpu.CompilerParams(
            dimension_semantics=("arbitrary",), vmem_limit_bytes=VMEM_LIMIT),
        name="experts",
    )(tile_e, n_used, tile_valid, next_e, xs, w_gate, w_up, w_down)


def _combine_stream_kernel(x2_ref, y1_ref, y2_ref, rw_ref, gf_ref, out_ref):
    w_t = jnp.transpose(rw_ref[...])
    a_lo, a_hi = _unpack_rows(y1_ref[...])
    b_lo, b_hi = _unpack_rows(y2_ref[...])
    w1 = w_t[:, 0:1]
    w2 = w_t[:, 1:2]
    moe = jnp.concatenate([w1 * a_lo + w2 * b_lo, w1 * a_hi + w2 * b_hi], axis=1)
    x3 = x2_ref[...] + moe
    ms = jnp.mean(x3 * x3, axis=-1, keepdims=True)
    out_ref[...] = x3 * lax.rsqrt(ms + RMS_EPS) * gf_ref[...]


def _combine_stream(x2, yg, rw, norm_final):
    b_sz, s_len, d = x2.shape
    tc = TC_COMBINE
    n_s = s_len // tc
    n_steps = b_sz * n_s
    return pl.pallas_call(
        _combine_stream_kernel,
        grid=(n_steps,),
        in_specs=[
            pl.BlockSpec((tc, d), lambda g: (g, 0)),
            pl.BlockSpec((tc, d // 2), lambda g: (g, 0)),
            pl.BlockSpec((tc, d // 2), lambda g: (g + n_steps, 0)),
            pl.BlockSpec((None, 2 * N_HEADS, tc), lambda g: (g // n_s, 0, g % n_s)),
            pl.BlockSpec((1, d), lambda g: (0, 0)),
        ],
        out_specs=pl.BlockSpec((tc, d), lambda g: (g, 0)),
        out_shape=jax.ShapeDtypeStruct((b_sz * s_len, d), F32),
        compiler_params=pltpu.CompilerParams(
            dimension_semantics=("arbitrary",), vmem_limit_bytes=VMEM_LIMIT),
        name="combine_stream",
    )(x2.reshape(b_sz * s_len, d), yg, yg, rw, norm_final[None, :]).reshape(b_sz, s_len, d)


def _layer(x, norm_mix, w_in, b_in, conv_w, w_out, norm_ffn, w_group, b_group, w_expert, b_expert,
           w_gate, w_up, w_down, norm_out):
    b_sz, s_len, d = x.shape
    n_tok = b_sz * s_len
    a, qvt, gt = _inproj(x, norm_mix, w_in, b_in)

    w_router = jnp.zeros((d, ROUTER_COLS), F32)
    w_router = w_router.at[:, 0:N_GROUPS].set(w_group).at[:, EXPERT_COL0:EXPERT_COL0 + N_EXPERTS].set(w_expert)
    b_router = jnp.zeros((1, ROUTER_COLS), F32)
    b_router = b_router.at[0, 0:N_GROUPS].set(b_group).at[0, EXPERT_COL0:EXPERT_COL0 + N_EXPERTS].set(b_expert)
    x2, hn2, ri, rw, counts = _mixer(x, a, qvt, gt, conv_w, w_out, norm_ffn, w_router, b_router)

    tm = TM_EXPERT
    cnt = counts[:, 0]
    padded = ((cnt + tm - 1) // tm) * tm
    pends = jnp.cumsum(padded).astype(I32)
    seg = jnp.concatenate([jnp.zeros((1,), I32), pends])
    n_tiles = (n_tok * 2) // tm + N_EXPERTS
    cap = n_tiles * tm
    tile_start = jnp.arange(n_tiles, dtype=I32) * tm
    tile_e = jnp.minimum(jnp.sum((tile_start[:, None] >= pends[None, :]).astype(I32), axis=1),
                         N_EXPERTS - 1)
    n_used = pends[-1:] // tm
    e_sel = ri[:, 0:2, :]
    seg_start = jnp.sum(jnp.where(e_sel[..., None] == jnp.arange(N_EXPERTS, dtype=I32), seg[:-1], 0), axis=-1)
    pos = ri[:, 2:4, :] + seg_start

    seg_tok_end = seg[:-1] + cnt.astype(I32)
    tile_tok_end = jnp.sum(jnp.where(tile_e[:, None] == jnp.arange(N_EXPERTS, dtype=I32), seg_tok_end, 0), axis=-1)
    tile_valid = jnp.clip(tile_tok_end - tile_start, 0, tm).astype(I32)
    pos_flat = jnp.concatenate([pos[:, 0, :].reshape(-1), pos[:, 1, :].reshape(-1)])

    expert_ids = jnp.arange(N_EXPERTS, dtype=I32)
    seg_end_tile = jnp.sum(jnp.where(tile_e[:, None] == expert_ids, pends // tm, 0), axis=-1)
    tile_ids = jnp.arange(n_tiles, dtype=I32)
    e_after = jnp.sum(jnp.where(seg_end_tile[:, None] == tile_ids, tile_e, 0), axis=-1)
    next_e = jnp.where(seg_end_tile < n_used[0], e_after, -1).astype(I32)

    xs = _sc_scatter_rows(hn2.reshape(n_tok, d // 2), pos_flat, cap)
    y = _experts(xs, tile_e, n_used, tile_valid, next_e, w_gate, w_up, w_down)
    yg = _sc_gather_rows(y, pos_flat)
    return _combine_stream(x2, yg, rw, norm_out)


def kernel(x, norm_mix, w_in, b_in, conv_w, w_out, norm_ffn, w_group, b_group, w_expert, b_expert,
           w_gate, w_up, w_down, norm_final):
    depth = norm_mix.shape[0]
    assert depth == 1, "final-norm fusion below assumes a single layer"
    assert x.shape[-1] == 2 * D_MLSTM and x.shape[1] % TS_MIXER == 0
    return _layer(x, norm_mix[0], w_in[0], b_in[0], conv_w[0], w_out[0], norm_ffn[0],
                  w_group[0], b_group[0], w_expert[0], b_expert[0],
                  w_gate[0], w_up[0], w_down[0], norm_final)
```

```python
import jax
import jax.numpy as jnp
from jax import lax
from jax.experimental import pallas as pl
from jax.experimental.pallas import tpu as pltpu
from jax.experimental.pallas import tpu_sc as plsc

F32 = jnp.float32
BF16 = jnp.bfloat16
I32 = jnp.int32

N_HEADS = 4
HEAD_DIM = 128
D_MLSTM = N_HEADS * HEAD_DIM
D_CONV = 512
N_GROUPS = 4
EXPERTS_PER_GROUP = 8
N_EXPERTS = N_GROUPS * EXPERTS_PER_GROUP
D_FF = 512
RMS_EPS = 1e-6
Q_SCALE = HEAD_DIM ** -0.5

LANES = 128
CHUNK = LANES
AUG_ROWS = 16
ROUTER_COLS = 128
EXPERT_COL0 = 8

TM_INPROJ = 512
TS_MIXER = 512
OUT_COLS = 256
TM_EXPERT = 512
TC_COMBINE = 512

VMEM_LIMIT = 56 * 1024 * 1024


HI_HALF = -65536


def _pack_rows(x):
    w = x.shape[1] // 2
    lo = lax.shift_right_logical(lax.bitcast_convert_type(x[:, :w].astype(BF16).astype(F32), I32), 16)
    hi = lax.bitcast_convert_type(x[:, w:].astype(BF16).astype(F32), I32) & HI_HALF
    return lo | hi


def _unpack_rows(words):
    lo = lax.bitcast_convert_type(words << 16, F32)
    hi = lax.bitcast_convert_type(words & HI_HALF, F32)
    return lo, hi


def _nt_dot(a, b):
    return lax.dot_general(a, b, (((1,), (1,)), ((), ())), preferred_element_type=F32)


def _inproj_kernel(x_ref, g_ref, wa_ref, ba_ref, wt_ref, bt_ref, a_ref, qvt_ref, gt_ref):
    x = x_ref[...]
    ms = jnp.mean(x * x, axis=-1, keepdims=True)
    hn = (x * lax.rsqrt(ms + RMS_EPS) * g_ref[...]).astype(BF16)
    n_blk = wa_ref.shape[1] // D_MLSTM
    for j in range(n_blk):
        sl = slice(j * D_MLSTM, (j + 1) * D_MLSTM)
        acc = jnp.dot(hn, wa_ref[:, sl], preferred_element_type=F32) + ba_ref[:, sl]
        a_ref[:, sl] = acc.astype(BF16)
    tr = _nt_dot(wt_ref[...], hn) + bt_ref[...]
    q_rows = slice(0, D_MLSTM)
    v_rows = slice(D_MLSTM, 2 * D_MLSTM)
    qvt_ref[q_rows, :] = (tr[q_rows, :] * Q_SCALE).astype(BF16)
    qvt_ref[v_rows, :] = tr[v_rows, :].astype(BF16)
    gt_ref[...] = tr[2 * D_MLSTM:2 * D_MLSTM + 2 * N_HEADS, :]


def _inproj(x, norm_mix, w_in, b_in):
    b_sz, s_len, d = x.shape
    tm = TM_INPROJ
    dm = D_MLSTM
    g0 = 4 * dm
    c0 = g0 + 2 * N_HEADS
    cols_a = jnp.concatenate([w_in[:, dm:2 * dm], w_in[:, 3 * dm:4 * dm], w_in[:, c0:c0 + 3 * D_CONV]], axis=1)
    bias_a = jnp.concatenate([b_in[dm:2 * dm], b_in[3 * dm:4 * dm], b_in[c0:c0 + 3 * D_CONV]])[None, :]
    n_gate = 2 * N_HEADS
    pad_rows = AUG_ROWS - n_gate
    w_t = jnp.concatenate([w_in[:, 0:dm], w_in[:, 2 * dm:3 * dm], w_in[:, g0:c0],
                           jnp.zeros((d, pad_rows), w_in.dtype)], axis=1).T
    b_t = jnp.concatenate([b_in[0:dm], b_in[2 * dm:3 * dm], b_in[g0:c0],
                           jnp.zeros((pad_rows,), b_in.dtype)])[:, None]
    n_t = 2 * dm + AUG_ROWS
    n_a = cols_a.shape[1]
    return pl.pallas_call(
        _inproj_kernel,
        grid=(b_sz, s_len // tm),
        in_specs=[
            pl.BlockSpec((None, tm, d), lambda b, s: (b, s, 0)),
            pl.BlockSpec((1, d), lambda b, s: (0, 0)),
            pl.BlockSpec((d, n_a), lambda b, s: (0, 0)),
            pl.BlockSpec((1, n_a), lambda b, s: (0, 0)),
            pl.BlockSpec((n_t, d), lambda b, s: (0, 0)),
            pl.BlockSpec((n_t, 1), lambda b, s: (0, 0)),
        ],
        out_specs=[
            pl.BlockSpec((None, tm, n_a), lambda b, s: (b, s, 0)),
            pl.BlockSpec((None, 2 * dm, tm), lambda b, s: (b, 0, s)),
            pl.BlockSpec((None, 2 * N_HEADS, tm), lambda b, s: (b, 0, s)),
        ],
        out_shape=[
            jax.ShapeDtypeStruct((b_sz, s_len, n_a), BF16),
            jax.ShapeDtypeStruct((b_sz, 2 * dm, s_len), BF16),
            jax.ShapeDtypeStruct((b_sz, 2 * N_HEADS, s_len), F32),
        ],
        compiler_params=pltpu.CompilerParams(
            dimension_semantics=("arbitrary", "arbitrary"), vmem_limit_bytes=VMEM_LIMIT),
        name="inproj",
    )(x, norm_mix[None, :], cols_a.astype(BF16), bias_a, w_t.astype(BF16), b_t)


def _mixer_kernel(k_ref, o_ref, cb_ref, cc_ref, cx_ref, qt_ref, vt_ref, gt_ref, x_ref,
                  cw_ref, wo_ref, g2_ref, wr_ref, br_ref,
                  x2_ref, hn2_ref, ri_ref, rw_ref, cnt_out_ref,
                  ct_ref, m_ref, carry_ref, cnt_ref, mix_ref):
    ts = x_ref.shape[0]
    n_chunks = ts // CHUNK
    first_tile = pl.program_id(1) == 0

    @pl.when(first_tile)
    def _():
        ct_ref[...] = jnp.zeros_like(ct_ref)
        m_ref[...] = jnp.zeros_like(m_ref)
        carry_ref[...] = jnp.zeros_like(carry_ref)

    @pl.when(jnp.logical_and(first_tile, pl.program_id(0) == 0))
    def _():
        cnt_ref[...] = jnp.zeros_like(cnt_ref)

    gates = gt_ref[...]
    li_all = gates[0:N_HEADS]
    fg = gates[N_HEADS:2 * N_HEADS]
    lf_all = jnp.minimum(fg, 0.0) - jnp.log1p(jnp.exp(-jnp.abs(fg)))
    key_j = lax.broadcasted_iota(I32, (CHUNK, CHUNK), 0)
    qry_i = lax.broadcasted_iota(I32, (CHUNK, CHUNK), 1)
    causal_t = key_j <= qry_i
    neg_inf = jnp.float32(-jnp.inf)
    ones_rows = (lax.broadcasted_iota(I32, (AUG_ROWS, CHUNK), 0) == 0).astype(BF16)

    prefix_ones = (key_j <= qry_i).astype(BF16)

    lf_hi = lf_all.astype(BF16)
    rem = lf_all - lf_hi.astype(F32)
    lf_mid = rem.astype(BF16)
    lf_lo = (rem - lf_mid.astype(F32)).astype(BF16)
    zero_bf = jnp.zeros((N_HEADS, CHUNK), BF16)
    parts = []
    for c in range(n_chunks):
        rows = slice(c * CHUNK, (c + 1) * CHUNK)
        parts += [lf_hi[:, rows], lf_mid[:, rows], lf_lo[:, rows], zero_bf]
    cum = jnp.dot(jnp.concatenate(parts, axis=0), prefix_ones, preferred_element_type=F32)

    m_prev = m_ref[0:N_HEADS, 0:1]
    for c in range(n_chunks):
        rows = slice(c * CHUNK, (c + 1) * CHUNK)
        li = li_all[:, rows]
        r0 = 4 * N_HEADS * c
        b_cum = (cum[r0:r0 + N_HEADS] + cum[r0 + N_HEADS:r0 + 2 * N_HEADS]
                 + cum[r0 + 2 * N_HEADS:r0 + 3 * N_HEADS])
        g_tot = b_cum[:, CHUNK - 1:CHUNK]
        a_end = g_tot - b_cum + li
        m_loc = jnp.max(a_end, axis=1, keepdims=True)
        u = li - b_cum
        m_new = jnp.maximum(g_tot + m_prev, m_loc)
        decay = jnp.exp(g_tot + m_prev - m_new)
        s_fac = jnp.exp(m_loc - m_new)
        w_key = jnp.exp(a_end - m_loc)
        u_cols = jnp.transpose(jnp.concatenate([u, jnp.zeros_like(u)], axis=0))
        for h in range(N_HEADS):
            hc = slice(h * HEAD_DIM, (h + 1) * HEAD_DIM)
            kh = k_ref[rows, hc]
            qth = qt_ref[hc, rows]
            vt_aug = jnp.concatenate([vt_ref[hc, rows], ones_rows], axis=0)
            s_t = jnp.dot(kh, qth, preferred_element_type=F32)
            u_t = jnp.where(causal_t, u_cols[:, h:h + 1], neg_inf)
            mp = m_prev[h:h + 1, :]
            m_tot = jnp.maximum(jnp.max(u_t, axis=0, keepdims=True), mp)
            f_inter = jnp.exp(mp - m_tot)
            floor = jnp.exp(-b_cum[h:h + 1, :] - m_tot)
            p_t = (s_t * jnp.exp(u_t - m_tot)).astype(BF16)
            q_f = (qth.astype(F32) * f_inter).astype(BF16)
            c_old = ct_ref[h]
            lhs = jnp.concatenate([vt_aug, c_old.astype(BF16)], axis=1)
            rhs = jnp.concatenate([p_t, q_f], axis=0)
            r_t = jnp.dot(lhs, rhs, preferred_element_type=F32)
            vtw = (vt_aug.astype(F32) * w_key[h:h + 1, :]).astype(BF16)
            kv = jnp.dot(vtw, kh, preferred_element_type=F32)
            ct_ref[h] = decay[h:h + 1, :] * c_old + s_fac[h:h + 1, :] * kv

            den = r_t[HEAD_DIM:HEAD_DIM + 1, :]
            inv = 1.0 / jnp.maximum(jnp.abs(den), floor)
            h_nat = jnp.transpose(r_t[0:HEAD_DIM, :] * inv)
            o_gate = jax.nn.sigmoid(o_ref[rows, hc].astype(F32))
            mix_ref[rows, hc] = (o_gate * h_nat).astype(BF16)
        m_prev = m_new
    m_ref[0:N_HEADS, :] = jnp.broadcast_to(m_prev, (N_HEADS, LANES))

    u_c = cc_ref[...].astype(F32) * cx_ref[...].astype(F32)
    rid = lax.broadcasted_iota(I32, u_c.shape, 0)
    prev1 = carry_ref[0:1, :]
    prev2 = carry_ref[1:2, :]
    u1 = jnp.where(rid == 0, prev1, pltpu.roll(u_c, 1, axis=0))
    u2 = jnp.where(rid == 0, prev2, jnp.where(rid == 1, prev1, pltpu.roll(u_c, 2, axis=0)))
    z = cw_ref[0:1, :] * u2 + cw_ref[1:2, :] * u1 + cw_ref[2:3, :] * u_c
    mix_ref[:, D_MLSTM:] = (cb_ref[...].astype(F32) * z).astype(BF16)
    carry_ref[0:1, :] = u_c[ts - 1:ts, :]
    carry_ref[1:2, :] = u_c[ts - 2:ts - 1, :]

    d_model = x_ref.shape[1]
    mix = mix_ref[...]
    ssq = jnp.zeros((ts, 1), F32)
    for j in range(d_model // OUT_COLS):
        cs = slice(j * OUT_COLS, (j + 1) * OUT_COLS)
        x2c = x_ref[:, cs] + jnp.dot(mix, wo_ref[:, cs], preferred_element_type=F32)
        x2_ref[:, cs] = x2c
        ssq = ssq + jnp.sum(x2c * x2c, axis=-1, keepdims=True)
    hn2 = x2_ref[...] * lax.rsqrt(ssq * (1.0 / d_model) + RMS_EPS) * g2_ref[...]
    hn2_ref[...] = _pack_rows(hn2)

    h_hi = hn2.astype(BF16)
    h_lo = (hn2 - h_hi.astype(F32)).astype(BF16)
    hi_both = jnp.dot(h_hi, wr_ref[...], preferred_element_type=F32)
    lo_hi = jnp.dot(h_lo, wr_ref[:, 0:ROUTER_COLS], preferred_element_type=F32)
    logits = hi_both[:, 0:ROUTER_COLS] + (hi_both[:, ROUTER_COLS:] + lo_hi) + br_ref[...]
    lt = jnp.transpose(logits)
    gl = lt[0:N_GROUPS]
    gmax = jnp.max(gl, axis=0, keepdims=True)
    gi = lax.broadcasted_iota(I32, gl.shape, 0)
    g_sel = jnp.min(jnp.where(gl == gmax, gi, N_GROUPS), axis=0, keepdims=True)
    p_g = 1.0 / jnp.sum(jnp.exp(gl - gmax), axis=0, keepdims=True)
    in_g = lt[EXPERT_COL0:EXPERT_COL0 + EXPERTS_PER_GROUP]
    for g in range(1, N_GROUPS):
        lo = EXPERT_COL0 + g * EXPERTS_PER_GROUP
        in_g = jnp.where(g_sel == g, lt[lo:lo + EXPERTS_PER_GROUP], in_g)
    ei = lax.broadcasted_iota(I32, in_g.shape, 0)
    v1 = jnp.max(in_g, axis=0, keepdims=True)
    i1 = jnp.min(jnp.where(in_g == v1, ei, EXPERTS_PER_GROUP), axis=0, keepdims=True)
    rest = jnp.where(ei == i1, neg_inf, in_g)
    v2 = jnp.max(rest, axis=0, keepdims=True)
    i2 = jnp.min(jnp.where(rest == v2, ei, EXPERTS_PER_GROUP), axis=0, keepdims=True)
    t = jnp.exp(v2 - v1)
    w1 = p_g / (1.0 + t)
    w2 = p_g * t / (1.0 + t)
    e1 = g_sel * EXPERTS_PER_GROUP + i1
    e2 = g_sel * EXPERTS_PER_GROUP + i2

    eio = lax.broadcasted_iota(I32, (N_EXPERTS, LANES), 0)
    base = cnt_ref[:, 0:1]
    zero_i = jnp.zeros((1, LANES), I32)
    n_blk = ts // LANES
    hits = []
    for k in range(n_blk):
        ls = slice(k * LANES, (k + 1) * LANES)
        hits.append((eio == e1[:, ls], eio == e2[:, ls]))
    onehots = [jnp.where(jnp.logical_or(a1, a2), 1.0, 0.0) for a1, a2 in hits]
    incl_all = jnp.dot(jnp.concatenate(onehots, axis=0).astype(BF16), prefix_ones,
                       preferred_element_type=F32).astype(I32)
    for k in range(n_blk):
        ls = slice(k * LANES, (k + 1) * LANES)
        is1, is2 = hits[k]
        onehot = onehots[k].astype(I32)
        incl = incl_all[k * N_EXPERTS:(k + 1) * N_EXPERTS]
        rank_e = base + incl - onehot
        r1 = jnp.sum(jnp.where(is1, rank_e, 0), axis=0, keepdims=True)
        r2 = jnp.sum(jnp.where(is2, rank_e, 0), axis=0, keepdims=True)
        ri_ref[:, ls] = jnp.concatenate(
            [e1[:, ls], e2[:, ls], r1, r2, zero_i, zero_i, zero_i, zero_i], axis=0)
        base = base + incl[:, LANES - 1:LANES]
    zero_f = jnp.zeros_like(w1)
    rw_ref[...] = jnp.concatenate([w1, w2, zero_f, zero_f, zero_f, zero_f, zero_f, zero_f], axis=0)
    cnt_new = jnp.broadcast_to(base, (N_EXPERTS, LANES))
    cnt_ref[...] = cnt_new
    cnt_out_ref[...] = cnt_new


def _mixer(x, a, qvt, gt, conv_w, w_out, norm_ffn, w_router, b_router):
    b_sz, s_len, d = x.shape
    ts = TS_MIXER
    dm = D_MLSTM

    def a_spec(j):
        return pl.BlockSpec((None, ts, dm), lambda b, s, j=j: (b, s, j))

    tok = lambda w: pl.BlockSpec((None, ts, w), lambda b, s: (b, s, 0))
    rowsp = pl.BlockSpec((None, 2 * N_HEADS, ts), lambda b, s: (b, 0, s))
    const = lambda shape: pl.BlockSpec(shape, lambda b, s: tuple(0 for _ in shape))
    wr_hi = w_router.astype(BF16)
    wr_lo = (w_router - wr_hi.astype(F32)).astype(BF16)
    w_router2 = jnp.concatenate([wr_hi, wr_lo], axis=1)
    return pl.pallas_call(
        _mixer_kernel,
        grid=(b_sz, s_len // ts),
        in_specs=[a_spec(0), a_spec(1), a_spec(2), a_spec(3), a_spec(4),
                  pl.BlockSpec((None, dm, ts), lambda b, s: (b, 0, s)),
                  pl.BlockSpec((None, dm, ts), lambda b, s: (b, 1, s)),
                  rowsp, tok(d),
                  const((3, D_CONV)), const((d, d)), const((1, d)),
                  const((d, 2 * ROUTER_COLS)), const((1, ROUTER_COLS))],
        out_specs=[tok(d), tok(d // 2), rowsp, rowsp, const((N_EXPERTS, LANES))],
        out_shape=[
            jax.ShapeDtypeStruct((b_sz, s_len, d), F32),
            jax.ShapeDtypeStruct((b_sz, s_len, d // 2), I32),
            jax.ShapeDtypeStruct((b_sz, 2 * N_HEADS, s_len), I32),
            jax.ShapeDtypeStruct((b_sz, 2 * N_HEADS, s_len), F32),
            jax.ShapeDtypeStruct((N_EXPERTS, LANES), I32),
        ],
        scratch_shapes=[
            pltpu.VMEM((N_HEADS, HEAD_DIM + AUG_ROWS, HEAD_DIM), F32),
            pltpu.VMEM((2 * N_HEADS, LANES), F32),
            pltpu.VMEM((8, D_CONV), F32),
            pltpu.VMEM((N_EXPERTS, LANES), I32),
            pltpu.VMEM((ts, d), BF16),
        ],
        compiler_params=pltpu.CompilerParams(
            dimension_semantics=("arbitrary", "arbitrary"), vmem_limit_bytes=VMEM_LIMIT),
        name="mixer",
    )(a, a, a, a, a, qvt, qvt, gt, x, conv_w, w_out.astype(BF16), norm_ffn[None, :], w_router2, b_router)


SC_CORES = 2
SC_SUBCORES = 16
SC_WORKERS = SC_CORES * SC_SUBCORES
SC_CHUNK = 64


def _sc_worker_id():
    return lax.axis_index("s") * SC_CORES + lax.axis_index("c")


def _sc_scatter_rows(rows, idx, n_out):
    n_rows, width = rows.shape
    per_worker = n_rows // SC_WORKERS
    n_chunks = per_worker // SC_CHUNK
    assert n_chunks % 2 == 0
    mesh = plsc.VectorSubcoreMesh(core_axis_name="c", subcore_axis_name="s")

    def body(rows_hbm, idx_hbm, out_hbm, ia0, ib0, r0, ia1, ib1, r1, ls0, ls1, ss0, ss1):
        wid = _sc_worker_id()
        bufs = ((ia0, ib0, r0, ls0, ss0), (ia1, ib1, r1, ls1, ss1))

        def loads(j, b):
            ia, ib, r, ls, _ = bufs[b]
            base = pl.multiple_of(wid * per_worker + j * SC_CHUNK, SC_CHUNK)
            return (pltpu.make_async_copy(rows_hbm.at[pl.ds(base, SC_CHUNK)], r, ls),
                    pltpu.make_async_copy(idx_hbm.at[pl.ds(base, SC_CHUNK)], ia, ls),
                    pltpu.make_async_copy(idx_hbm.at[pl.ds(n_rows + base, SC_CHUNK)], ib, ls))

        def scatters(b):
            ia, ib, r, _, ss = bufs[b]
            return (pltpu.make_async_copy(r, out_hbm.at[ia], ss),
                    pltpu.make_async_copy(r, out_hbm.at[ib], ss))

        for c in loads(0, 0):
            c.start()

        @pl.loop(0, n_chunks, step=2)
        def _(j0):
            for b in range(2):
                j = j0 + b
                for c in loads(j, b):
                    c.wait()

                @pl.when(j >= 1)
                def _():
                    for c in scatters(1 - b):
                        c.wait()

                @pl.when(j + 1 < n_chunks)
                def _():
                    for c in loads(j + 1, 1 - b):
                        c.start()

                for c in scatters(b):
                    c.start()

        for c in scatters((n_chunks - 1) % 2):
            c.wait()

    idx_t = pltpu.VMEM((SC_CHUNK,), I32)
    row_t = pltpu.VMEM((SC_CHUNK, width), I32)
    sem_t = pltpu.SemaphoreType.DMA
    return pl.kernel(
        body,
        out_type=jax.ShapeDtypeStruct((n_out, width), I32),
        mesh=mesh,
        scratch_types=[idx_t, idx_t, row_t, idx_t, idx_t, row_t, sem_t, sem_t, sem_t, sem_t],
        name="sc_scatter",
    )(rows, idx)


def _sc_gather_rows(table, idx):
    n_rows = idx.shape[0]
    width = table.shape[1]
    per_worker = n_rows // SC_WORKERS
    n_chunks = per_worker // SC_CHUNK
    assert n_chunks % 2 == 0
    mesh = plsc.VectorSubcoreMesh(core_axis_name="c", subcore_axis_name="s")

    def body(table_hbm, idx_hbm, out_hbm, i0, r0, i1, r1, is0, is1, gs0, gs1, ws0, ws1):
        wid = _sc_worker_id()
        bufs = ((i0, r0, is0, gs0, ws0), (i1, r1, is1, gs1, ws1))

        def chunk_base(j):
            return pl.multiple_of(wid * per_worker + j * SC_CHUNK, SC_CHUNK)

        def idx_load(j, b):
            return pltpu.make_async_copy(idx_hbm.at[pl.ds(chunk_base(j), SC_CHUNK)], bufs[b][0], bufs[b][2])

        def gather(b):
            return pltpu.make_async_copy(table_hbm.at[bufs[b][0]], bufs[b][1], bufs[b][3])

        def writeback(j, b):
            return pltpu.make_async_copy(bufs[b][1], out_hbm.at[pl.ds(chunk_base(j), SC_CHUNK)], bufs[b][4])

        idx_load(0, 0).start()
        idx_load(0, 0).wait()
        gather(0).start()
        idx_load(1, 1).start()

        @pl.loop(0, n_chunks, step=2)
        def _(j0):
            for b in range(2):
                j = j0 + b

                @pl.when(j + 1 < n_chunks)
                def _():
                    idx_load(j + 1, 1 - b).wait()

                    @pl.when(j >= 1)
                    def _():
                        writeback(j - 1, 1 - b).wait()
                    gather(1 - b).start()

                gather(b).wait()
                writeback(j, b).start()

                @pl.when(j + 2 < n_chunks)
                def _():
                    idx_load(j + 2, b).start()

        writeback(n_chunks - 2, 0).wait()
        writeback(n_chunks - 1, 1).wait()

    idx_t = pltpu.VMEM((SC_CHUNK,), I32)
    row_t = pltpu.VMEM((SC_CHUNK, width), I32)
    sem_t = pltpu.SemaphoreType.DMA
    return pl.kernel(
        body,
        out_type=jax.ShapeDtypeStruct((n_rows, width), I32),
        mesh=mesh,
        scratch_types=[idx_t, row_t, idx_t, row_t, sem_t, sem_t, sem_t, sem_t, sem_t, sem_t],
        name="sc_gather",
    )(table, idx)


def _experts_kernel(te_ref, nu_ref, tv_ref, ne_ref, xs_ref, wg_hbm, wu_hbm, wd_hbm, y_ref,
                    wgs_ref, wus_ref, wds_ref, wgb_ref, wub_ref, wdb_ref, wsem):
    i = pl.program_id(0)
    e = te_ref[i]
    e_prev = te_ref[jnp.maximum(i - 1, 0)]
    new_expert = jnp.logical_or(i == 0, e != e_prev)
    used = i < nu_ref[0]

    def weight_copies(expert):
        return (pltpu.make_async_copy(wg_hbm.at[expert], wgs_ref, wsem),
                pltpu.make_async_copy(wu_hbm.at[expert], wus_ref, wsem),
                pltpu.make_async_copy(wd_hbm.at[expert], wds_ref, wsem))

    @pl.when(i == 0)
    def _():
        for c in weight_copies(e):
            c.start()

    @pl.when(jnp.logical_and(used, new_expert))
    def _():
        for c in weight_copies(e):
            c.wait()
        wgb_ref[...] = wgs_ref[...].astype(BF16)
        wub_ref[...] = wus_ref[...].astype(BF16)
        wdb_ref[...] = wds_ref[...].astype(BF16)

        @pl.when(ne_ref[i] >= 0)
        def _():
            for c in weight_copies(ne_ref[i]):
                c.start()

    @pl.when(used)
    def _():
        row = lax.broadcasted_iota(I32, xs_ref.shape, 0)
        x_lo, x_hi = _unpack_rows(jnp.where(row < tv_ref[i], xs_ref[...], 0))
        xb = jnp.concatenate([x_lo.astype(BF16), x_hi.astype(BF16)], axis=1)
        gate = jnp.dot(xb, wgb_ref[...], preferred_element_type=F32)
        up = jnp.dot(xb, wub_ref[...], preferred_element_type=F32)
        hid = (gate * jax.nn.sigmoid(gate) * up).astype(BF16)
        y_ref[...] = _pack_rows(jnp.dot(hid, wdb_ref[...], preferred_element_type=F32))

    @pl.when(jnp.logical_not(used))
    def _():
        y_ref[...] = jnp.zeros_like(y_ref)


def _experts(xs, tile_e, n_used, tile_valid, next_e, w_gate, w_up, w_down):
    cap, dw = xs.shape
    d = 2 * dw
    tm = TM_EXPERT
    n_tiles = cap // tm

    def x_map(i, te, nu, tv, ne):
        return (jnp.minimum(i, jnp.maximum(nu[0] - 1, 0)), 0)

    return pl.pallas_call(
        _experts_kernel,
        grid_spec=pltpu.PrefetchScalarGridSpec(
            num_scalar_prefetch=4,
            grid=(n_tiles,),
            in_specs=[
                pl.BlockSpec((tm, dw), x_map),
                pl.BlockSpec(memory_space=pl.ANY),
                pl.BlockSpec(memory_space=pl.ANY),
                pl.BlockSpec(memory_space=pl.ANY),
            ],
            out_specs=pl.BlockSpec((tm, dw), lambda i, te, nu, tv, ne: (i, 0)),
            scratch_shapes=[pltpu.VMEM((d, D_FF), F32), pltpu.VMEM((d, D_FF), F32), pltpu.VMEM((D_FF, d), F32),
                            pltpu.VMEM((d, D_FF), BF16), pltpu.VMEM((d, D_FF), BF16), pltpu.VMEM((D_FF, d), BF16),
                            pltpu.SemaphoreType.DMA(())],
        ),
        out_shape=jax.ShapeDtypeStruct((cap, dw), I32),
        compiler_params=pltpu.CompilerParams(
            dimension_semantics=("arbitrary",), vmem_limit_bytes=VMEM_LIMIT),
        name="experts",
    )(tile_e, n_used, tile_valid, next_e, xs, w_gate, w_up, w_down)


def _combine_stream_kernel(x2_ref, y1_ref, y2_ref, rw_ref, gf_ref, out_ref):
    w_t = jnp.transpose(rw_ref[...])
    a_lo, a_hi = _unpack_rows(y1_ref[...])
    b_lo, b_hi = _unpack_rows(y2_ref[...])
    w1 = w_t[:, 0:1]
    w2 = w_t[:, 1:2]
    moe = jnp.concatenate([w1 * a_lo + w2 * b_lo, w1 * a_hi + w2 * b_hi], axis=1)
    x3 = x2_ref[...] + moe
    ms = jnp.mean(x3 * x3, axis=-1, keepdims=True)
    out_ref[...] = x3 * lax.rsqrt(ms + RMS_EPS) * gf_ref[...]


def _combine_stream(x2, yg, rw, norm_final):
    b_sz, s_len, d = x2.shape
    tc = TC_COMBINE
    n_s = s_len // tc
    n_steps = b_sz * n_s
    return pl.pallas_call(
        _combine_stream_kernel,
        grid=(n_steps,),
        in_specs=[
            pl.BlockSpec((tc, d), lambda g: (g, 0)),
            pl.BlockSpec((tc, d // 2), lambda g: (g, 0)),
            pl.BlockSpec((tc, d // 2), lambda g: (g + n_steps, 0)),
            pl.BlockSpec((None, 2 * N_HEADS, tc), lambda g: (g // n_s, 0, g % n_s)),
            pl.BlockSpec((1, d), lambda g: (0, 0)),
        ],
        out_specs=pl.BlockSpec((tc, d), lambda g: (g, 0)),
        out_shape=jax.ShapeDtypeStruct((b_sz * s_len, d), F32),
        compiler_params=pltpu.CompilerParams(
            dimension_semantics=("arbitrary",), vmem_limit_bytes=VMEM_LIMIT),
        name="combine_stream",
    )(x2.reshape(b_sz * s_len, d), yg, yg, rw, norm_final[None, :]).reshape(b_sz, s_len, d)


def _layer(x, norm_mix, w_in, b_in, conv_w, w_out, norm_ffn, w_group, b_group, w_expert, b_expert,
           w_gate, w_up, w_down, norm_out):
    b_sz, s_len, d = x.shape
    n_tok = b_sz * s_len
    a, qvt, gt = _inproj(x, norm_mix, w_in, b_in)

    w_router = jnp.zeros((d, ROUTER_COLS), F32)
    w_router = w_router.at[:, 0:N_GROUPS].set(w_group).at[:, EXPERT_COL0:EXPERT_COL0 + N_EXPERTS].set(w_expert)
    b_router = jnp.zeros((1, ROUTER_COLS), F32)
    b_router = b_router.at[0, 0:N_GROUPS].set(b_group).at[0, EXPERT_COL0:EXPERT_COL0 + N_EXPERTS].set(b_expert)
    x2, hn2, ri, rw, counts = _mixer(x, a, qvt, gt, conv_w, w_out, norm_ffn, w_router, b_router)

    tm = TM_EXPERT
    cnt = counts[:, 0]
    padded = ((cnt + tm - 1) // tm) * tm
    pends = jnp.cumsum(padded).astype(I32)
    seg = jnp.concatenate([jnp.zeros((1,), I32), pends])
    n_tiles = (n_tok * 2) // tm + N_EXPERTS
    cap = n_tiles * tm
    tile_start = jnp.arange(n_tiles, dtype=I32) * tm
    tile_e = jnp.minimum(jnp.sum((tile_start[:, None] >= pends[None, :]).astype(I32), axis=1),
                         N_EXPERTS - 1)
    n_used = pends[-1:] // tm
    e_sel = ri[:, 0:2, :]
    seg_start = jnp.sum(jnp.where(e_sel[..., None] == jnp.arange(N_EXPERTS, dtype=I32), seg[:-1], 0), axis=-1)
    pos = ri[:, 2:4, :] + seg_start

    seg_tok_end = seg[:-1] + cnt.astype(I32)
    tile_tok_end = jnp.sum(jnp.where(tile_e[:, None] == jnp.arange(N_EXPERTS, dtype=I32), seg_tok_end, 0), axis=-1)
    tile_valid = jnp.clip(tile_tok_end - tile_start, 0, tm).astype(I32)
    pos_flat = jnp.concatenate([pos[:, 0, :].reshape(-1), pos[:, 1, :].reshape(-1)])

    expert_ids = jnp.arange(N_EXPERTS, dtype=I32)
    seg_end_tile = jnp.sum(jnp.where(tile_e[:, None] == expert_ids, pends // tm, 0), axis=-1)
    tile_ids = jnp.arange(n_tiles, dtype=I32)
    e_after = jnp.sum(jnp.where(seg_end_tile[:, None] == tile_ids, tile_e, 0), axis=-1)
    next_e = jnp.where(seg_end_tile < n_used[0], e_after, -1).astype(I32)

    xs = _sc_scatter_rows(hn2.reshape(n_tok, d // 2), pos_flat, cap)
    y = _experts(xs, tile_e, n_used, tile_valid, next_e, w_gate, w_up, w_down)
    yg = _sc_gather_rows(y, pos_flat)
    return _combine_stream(x2, yg, rw, norm_out)


def kernel(x, norm_mix, w_in, b_in, conv_w, w_out, norm_ffn, w_group, b_group, w_expert, b_expert,
           w_gate, w_up, w_down, norm_final):
    depth = norm_mix.shape[0]
    assert depth == 1, "final-norm fusion below assumes a single layer"
    assert x.shape[-1] == 2 * D_MLSTM and x.shape[1] % TS_MIXER == 0
    return _layer(x, norm_mix[0], w_in[0], b_in[0], conv_w[0], w_out[0], norm_ffn[0],
                  w_group[0], b_group[0], w_expert[0], b_expert[0],
                  w_gate[0], w_up[0], w_down[0], norm_final)
```

```python
import jax
import jax.numpy as jnp
from jax import lax
from jax.experimental import pallas as pl
from jax.experimental.pallas import tpu as pltpu
from jax.experimental.pallas import tpu_sc as plsc

F32 = jnp.float32
BF16 = jnp.bfloat16
I32 = jnp.int32

N_HEADS = 4
HEAD_DIM = 128
D_MLSTM = N_HEADS * HEAD_DIM
D_CONV = 512
N_GROUPS = 4
EXPERTS_PER_GROUP = 8
N_EXPERTS = N_GROUPS * EXPERTS_PER_GROUP
D_FF = 512
RMS_EPS = 1e-6
Q_SCALE = HEAD_DIM ** -0.5

LANES = 128
CHUNK = LANES
AUG_ROWS = 16
ROUTER_COLS = 128
EXPERT_COL0 = 8

TM_INPROJ = 1024
TS_MIXER = 512
OUT_COLS = 256
TM_EXPERT = 512
TC_COMBINE = 1024

VMEM_LIMIT = 56 * 1024 * 1024


HI_HALF = -65536


def _pack_rows(x):
    w = x.shape[1] // 2
    lo = lax.shift_right_logical(lax.bitcast_convert_type(x[:, :w].astype(BF16).astype(F32), I32), 16)
    hi = lax.bitcast_convert_type(x[:, w:].astype(BF16).astype(F32), I32) & HI_HALF
    return lo | hi


def _unpack_rows(words):
    lo = lax.bitcast_convert_type(words << 16, F32)
    hi = lax.bitcast_convert_type(words & HI_HALF, F32)
    return lo, hi


def _nt_dot(a, b):
    return lax.dot_general(a, b, (((1,), (1,)), ((), ())), preferred_element_type=F32)


def _inproj_kernel(x_ref, g_ref, wa_ref, ba_ref, wt_ref, bt_ref, a_ref, qvt_ref, gt_ref):
    x = x_ref[...]
    ms = jnp.mean(x * x, axis=-1, keepdims=True)
    hn = (x * lax.rsqrt(ms + RMS_EPS) * g_ref[...]).astype(BF16)
    n_blk = wa_ref.shape[1] // D_MLSTM
    for j in range(n_blk):
        sl = slice(j * D_MLSTM, (j + 1) * D_MLSTM)
        acc = jnp.dot(hn, wa_ref[:, sl], preferred_element_type=F32) + ba_ref[:, sl]
        a_ref[:, sl] = acc.astype(BF16)
    tr = _nt_dot(wt_ref[...], hn) + bt_ref[...]
    q_rows = slice(0, D_MLSTM)
    v_rows = slice(D_MLSTM, 2 * D_MLSTM)
    qvt_ref[q_rows, :] = (tr[q_rows, :] * Q_SCALE).astype(BF16)
    qvt_ref[v_rows, :] = tr[v_rows, :].astype(BF16)
    gt_ref[...] = tr[2 * D_MLSTM:2 * D_MLSTM + 2 * N_HEADS, :]


def _inproj(x, norm_mix, w_in, b_in):
    b_sz, s_len, d = x.shape
    tm = TM_INPROJ
    dm = D_MLSTM
    g0 = 4 * dm
    c0 = g0 + 2 * N_HEADS
    cols_a = jnp.concatenate([w_in[:, dm:2 * dm], w_in[:, 3 * dm:4 * dm], w_in[:, c0:c0 + 3 * D_CONV]], axis=1)
    bias_a = jnp.concatenate([b_in[dm:2 * dm], b_in[3 * dm:4 * dm], b_in[c0:c0 + 3 * D_CONV]])[None, :]
    n_gate = 2 * N_HEADS
    pad_rows = AUG_ROWS - n_gate
    w_t = jnp.concatenate([w_in[:, 0:dm], w_in[:, 2 * dm:3 * dm], w_in[:, g0:c0],
                           jnp.zeros((d, pad_rows), w_in.dtype)], axis=1).T
    b_t = jnp.concatenate([b_in[0:dm], b_in[2 * dm:3 * dm], b_in[g0:c0],
                           jnp.zeros((pad_rows,), b_in.dtype)])[:, None]
    n_t = 2 * dm + AUG_ROWS
    n_a = cols_a.shape[1]
    return pl.pallas_call(
        _inproj_kernel,
        grid=(b_sz, s_len // tm),
        in_specs=[
            pl.BlockSpec((None, tm, d), lambda b, s: (b, s, 0)),
            pl.BlockSpec((1, d), lambda b, s: (0, 0)),
            pl.BlockSpec((d, n_a), lambda b, s: (0, 0)),
            pl.BlockSpec((1, n_a), lambda b, s: (0, 0)),
            pl.BlockSpec((n_t, d), lambda b, s: (0, 0)),
            pl.BlockSpec((n_t, 1), lambda b, s: (0, 0)),
        ],
        out_specs=[
            pl.BlockSpec((None, tm, n_a), lambda b, s: (b, s, 0)),
            pl.BlockSpec((None, 2 * dm, tm), lambda b, s: (b, 0, s)),
            pl.BlockSpec((None, 2 * N_HEADS, tm), lambda b, s: (b, 0, s)),
        ],
        out_shape=[
            jax.ShapeDtypeStruct((b_sz, s_len, n_a), BF16),
            jax.ShapeDtypeStruct((b_sz, 2 * dm, s_len), BF16),
            jax.ShapeDtypeStruct((b_sz, 2 * N_HEADS, s_len), F32),
        ],
        compiler_params=pltpu.CompilerParams(
            dimension_semantics=("arbitrary", "arbitrary"), vmem_limit_bytes=VMEM_LIMIT),
        name="inproj",
    )(x, norm_mix[None, :], cols_a.astype(BF16), bias_a, w_t.astype(BF16), b_t)


def _mixer_kernel(k_ref, o_ref, cb_ref, cc_ref, cx_ref, qt_ref, vt_ref, gt_ref, x_ref,
                  cw_ref, wo_ref, g2_ref, wr_ref, br_ref,
                  x2_ref, hn2_ref, ri_ref, rw_ref, cnt_out_ref,
                  ct_ref, m_ref, carry_ref, cnt_ref, mix_ref):
    ts = x_ref.shape[0]
    n_chunks = ts // CHUNK
    first_tile = pl.program_id(1) == 0

    @pl.when(first_tile)
    def _():
        ct_ref[...] = jnp.zeros_like(ct_ref)
        m_ref[...] = jnp.zeros_like(m_ref)
        carry_ref[...] = jnp.zeros_like(carry_ref)

    @pl.when(jnp.logical_and(first_tile, pl.program_id(0) == 0))
    def _():
        cnt_ref[...] = jnp.zeros_like(cnt_ref)

    gates = gt_ref[...]
    li_all = gates[0:N_HEADS]
    fg = gates[N_HEADS:2 * N_HEADS]
    lf_all = jnp.minimum(fg, 0.0) - jnp.log1p(jnp.exp(-jnp.abs(fg)))
    key_j = lax.broadcasted_iota(I32, (CHUNK, CHUNK), 0)
    qry_i = lax.broadcasted_iota(I32, (CHUNK, CHUNK), 1)
    causal_t = key_j <= qry_i
    neg_inf = jnp.float32(-jnp.inf)
    ones_rows = (lax.broadcasted_iota(I32, (AUG_ROWS, CHUNK), 0) == 0).astype(BF16)

    prefix_ones = (key_j <= qry_i).astype(BF16)

    lf_hi = lf_all.astype(BF16)
    rem = lf_all - lf_hi.astype(F32)
    lf_mid = rem.astype(BF16)
    lf_lo = (rem - lf_mid.astype(F32)).astype(BF16)
    zero_bf = jnp.zeros((N_HEADS, CHUNK), BF16)
    parts = []
    for c in range(n_chunks):
        rows = slice(c * CHUNK, (c + 1) * CHUNK)
        parts += [lf_hi[:, rows], lf_mid[:, rows], lf_lo[:, rows], zero_bf]
    cum = jnp.dot(jnp.concatenate(parts, axis=0), prefix_ones, preferred_element_type=F32)

    m_prev = m_ref[0:N_HEADS, 0:1]
    for c in range(n_chunks):
        rows = slice(c * CHUNK, (c + 1) * CHUNK)
        li = li_all[:, rows]
        r0 = 4 * N_HEADS * c
        b_cum = (cum[r0:r0 + N_HEADS] + cum[r0 + N_HEADS:r0 + 2 * N_HEADS]
                 + cum[r0 + 2 * N_HEADS:r0 + 3 * N_HEADS])
        g_tot = b_cum[:, CHUNK - 1:CHUNK]
        a_end = g_tot - b_cum + li
        m_loc = jnp.max(a_end, axis=1, keepdims=True)
        u = li - b_cum
        m_new = jnp.maximum(g_tot + m_prev, m_loc)
        decay = jnp.exp(g_tot + m_prev - m_new)
        s_fac = jnp.exp(m_loc - m_new)
        w_key = jnp.exp(a_end - m_loc)
        u_cols = jnp.transpose(jnp.concatenate([u, jnp.zeros_like(u)], axis=0))
        for h in range(N_HEADS):
            hc = slice(h * HEAD_DIM, (h + 1) * HEAD_DIM)
            kh = k_ref[rows, hc]
            qth = qt_ref[hc, rows]
            vt_aug = jnp.concatenate([vt_ref[hc, rows], ones_rows], axis=0)
            s_t = jnp.dot(kh, qth, preferred_element_type=F32)
            u_t = jnp.where(causal_t, u_cols[:, h:h + 1], neg_inf)
            mp = m_prev[h:h + 1, :]
            m_tot = jnp.maximum(jnp.max(u_t, axis=0, keepdims=True), mp)
            f_inter = jnp.exp(mp - m_tot)
            floor = jnp.exp(-b_cum[h:h + 1, :] - m_tot)
            p_t = (s_t * jnp.exp(u_t - m_tot)).astype(BF16)
            q_f = (qth.astype(F32) * f_inter).astype(BF16)
            c_old = ct_ref[h]
            lhs = jnp.concatenate([vt_aug, c_old.astype(BF16)], axis=1)
            rhs = jnp.concatenate([p_t, q_f], axis=0)
            r_t = jnp.dot(lhs, rhs, preferred_element_type=F32)
            vtw = (vt_aug.astype(F32) * w_key[h:h + 1, :]).astype(BF16)
            kv = jnp.dot(vtw, kh, preferred_element_type=F32)
            ct_ref[h] = decay[h:h + 1, :] * c_old + s_fac[h:h + 1, :] * kv

            den = r_t[HEAD_DIM:HEAD_DIM + 1, :]
            inv = 1.0 / jnp.maximum(jnp.abs(den), floor)
            h_nat = jnp.transpose(r_t[0:HEAD_DIM, :] * inv)
            o_gate = jax.nn.sigmoid(o_ref[rows, hc].astype(F32))
            mix_ref[rows, hc] = (o_gate * h_nat).astype(BF16)
        m_prev = m_new
    m_ref[0:N_HEADS, :] = jnp.broadcast_to(m_prev, (N_HEADS, LANES))

    u_c = cc_ref[...].astype(F32) * cx_ref[...].astype(F32)
    rid = lax.broadcasted_iota(I32, u_c.shape, 0)
    prev1 = carry_ref[0:1, :]
    prev2 = carry_ref[1:2, :]
    u1 = jnp.where(rid == 0, prev1, pltpu.roll(u_c, 1, axis=0))
    u2 = jnp.where(rid == 0, prev2, jnp.where(rid == 1, prev1, pltpu.roll(u_c, 2, axis=0)))
    z = cw_ref[0:1, :] * u2 + cw_ref[1:2, :] * u1 + cw_ref[2:3, :] * u_c
    mix_ref[:, D_MLSTM:] = (cb_ref[...].astype(F32) * z).astype(BF16)
    carry_ref[0:1, :] = u_c[ts - 1:ts, :]
    carry_ref[1:2, :] = u_c[ts - 2:ts - 1, :]

    d_model = x_ref.shape[1]
    mix = mix_ref[...]
    ssq = jnp.zeros((ts, 1), F32)
    for j in range(d_model // OUT_COLS):
        cs = slice(j * OUT_COLS, (j + 1) * OUT_COLS)
        x2c = x_ref[:, cs] + jnp.dot(mix, wo_ref[:, cs], preferred_element_type=F32)
        x2_ref[:, cs] = x2c
        ssq = ssq + jnp.sum(x2c * x2c, axis=-1, keepdims=True)
    hn2 = x2_ref[...] * lax.rsqrt(ssq * (1.0 / d_model) + RMS_EPS) * g2_ref[...]
    hn2_ref[...] = _pack_rows(hn2)

    h_hi = hn2.astype(BF16)
    h_lo = (hn2 - h_hi.astype(F32)).astype(BF16)
    hi_both = jnp.dot(h_hi, wr_ref[...], preferred_element_type=F32)
    lo_hi = jnp.dot(h_lo, wr_ref[:, 0:ROUTER_COLS], preferred_element_type=F32)
    logits = hi_both[:, 0:ROUTER_COLS] + (hi_both[:, ROUTER_COLS:] + lo_hi) + br_ref[...]
    lt = jnp.transpose(logits)
    gl = lt[0:N_GROUPS]
    gmax = jnp.max(gl, axis=0, keepdims=True)
    gi = lax.broadcasted_iota(I32, gl.shape, 0)
    g_sel = jnp.min(jnp.where(gl == gmax, gi, N_GROUPS), axis=0, keepdims=True)
    p_g = 1.0 / jnp.sum(jnp.exp(gl - gmax), axis=0, keepdims=True)
    in_g = lt[EXPERT_COL0:EXPERT_COL0 + EXPERTS_PER_GROUP]
    for g in range(1, N_GROUPS):
        lo = EXPERT_COL0 + g * EXPERTS_PER_GROUP
        in_g = jnp.where(g_sel == g, lt[lo:lo + EXPERTS_PER_GROUP], in_g)
    ei = lax.broadcasted_iota(I32, in_g.shape, 0)
    v1 = jnp.max(in_g, axis=0, keepdims=True)
    i1 = jnp.min(jnp.where(in_g == v1, ei, EXPERTS_PER_GROUP), axis=0, keepdims=True)
    rest = jnp.where(ei == i1, neg_inf, in_g)
    v2 = jnp.max(rest, axis=0, keepdims=True)
    i2 = jnp.min(jnp.where(rest == v2, ei, EXPERTS_PER_GROUP), axis=0, keepdims=True)
    t = jnp.exp(v2 - v1)
    w1 = p_g / (1.0 + t)
    w2 = p_g * t / (1.0 + t)
    e1 = g_sel * EXPERTS_PER_GROUP + i1
    e2 = g_sel * EXPERTS_PER_GROUP + i2

    eio = lax.broadcasted_iota(I32, (N_EXPERTS, LANES), 0)
    base = cnt_ref[:, 0:1]
    zero_i = jnp.zeros((1, LANES), I32)
    n_blk = ts // LANES
    hits = []
    for k in range(n_blk):
        ls = slice(k * LANES, (k + 1) * LANES)
        hits.append((eio == e1[:, ls], eio == e2[:, ls]))
    onehots = [jnp.where(jnp.logical_or(a1, a2), 1.0, 0.0) for a1, a2 in hits]
    incl_all = jnp.dot(jnp.concatenate(onehots, axis=0).astype(BF16), prefix_ones,
                       preferred_element_type=F32).astype(I32)
    for k in range(n_blk):
        ls = slice(k * LANES, (k + 1) * LANES)
        is1, is2 = hits[k]
        onehot = onehots[k].astype(I32)
        incl = incl_all[k * N_EXPERTS:(k + 1) * N_EXPERTS]
        rank_e = base + incl - onehot
        r1 = jnp.sum(jnp.where(is1, rank_e, 0), axis=0, keepdims=True)
        r2 = jnp.sum(jnp.where(is2, rank_e, 0), axis=0, keepdims=True)
        ri_ref[:, ls] = jnp.concatenate(
            [e1[:, ls], e2[:, ls], r1, r2, zero_i, zero_i, zero_i, zero_i], axis=0)
        base = base + incl[:, LANES - 1:LANES]
    zero_f = jnp.zeros_like(w1)
    rw_ref[...] = jnp.concatenate([w1, w2, zero_f, zero_f, zero_f, zero_f, zero_f, zero_f], axis=0)
    cnt_new = jnp.broadcast_to(base, (N_EXPERTS, LANES))
    cnt_ref[...] = cnt_new
    cnt_out_ref[...] = cnt_new


def _mixer(x, a, qvt, gt, conv_w, w_out, norm_ffn, w_router, b_router):
    b_sz, s_len, d = x.shape
    ts = TS_MIXER
    dm = D_MLSTM

    def a_spec(j):
        return pl.BlockSpec((None, ts, dm), lambda b, s, j=j: (b, s, j))

    tok = lambda w: pl.BlockSpec((None, ts, w), lambda b, s: (b, s, 0))
    rowsp = pl.BlockSpec((None, 2 * N_HEADS, ts), lambda b, s: (b, 0, s))
    const = lambda shape: pl.BlockSpec(shape, lambda b, s: tuple(0 for _ in shape))
    wr_hi = w_router.astype(BF16)
    wr_lo = (w_router - wr_hi.astype(F32)).astype(BF16)
    w_router2 = jnp.concatenate([wr_hi, wr_lo], axis=1)
    return pl.pallas_call(
        _mixer_kernel,
        grid=(b_sz, s_len // ts),
        in_specs=[a_spec(0), a_spec(1), a_spec(2), a_spec(3), a_spec(4),
                  pl.BlockSpec((None, dm, ts), lambda b, s: (b, 0, s)),
                  pl.BlockSpec((None, dm, ts), lambda b, s: (b, 1, s)),
                  rowsp, tok(d),
                  const((3, D_CONV)), const((d, d)), const((1, d)),
                  const((d, 2 * ROUTER_COLS)), const((1, ROUTER_COLS))],
        out_specs=[tok(d), tok(d // 2), rowsp, rowsp, const((N_EXPERTS, LANES))],
        out_shape=[
            jax.ShapeDtypeStruct((b_sz, s_len, d), F32),
            jax.ShapeDtypeStruct((b_sz, s_len, d // 2), I32),
            jax.ShapeDtypeStruct((b_sz, 2 * N_HEADS, s_len), I32),
            jax.ShapeDtypeStruct((b_sz, 2 * N_HEADS, s_len), F32),
            jax.ShapeDtypeStruct((N_EXPERTS, LANES), I32),
        ],
        scratch_shapes=[
            pltpu.VMEM((N_HEADS, HEAD_DIM + AUG_ROWS, HEAD_DIM), F32),
            pltpu.VMEM((2 * N_HEADS, LANES), F32),
            pltpu.VMEM((8, D_CONV), F32),
            pltpu.VMEM((N_EXPERTS, LANES), I32),
            pltpu.VMEM((ts, d), BF16),
        ],
        compiler_params=pltpu.CompilerParams(
            dimension_semantics=("arbitrary", "arbitrary"), vmem_limit_bytes=VMEM_LIMIT),
        name="mixer",
    )(a, a, a, a, a, qvt, qvt, gt, x, conv_w, w_out.astype(BF16), norm_ffn[None, :], w_router2, b_router)


SC_CORES = 2
SC_SUBCORES = 16
SC_WORKERS = SC_CORES * SC_SUBCORES
SC_CHUNK = 64


def _sc_worker_id():
    return lax.axis_index("s") * SC_CORES + lax.axis_index("c")


def _sc_scatter_rows(rows, idx, n_out):
    n_rows, width = rows.shape
    per_worker = n_rows // SC_WORKERS
    n_chunks = per_worker // SC_CHUNK
    assert n_chunks % 2 == 0
    mesh = plsc.VectorSubcoreMesh(core_axis_name="c", subcore_axis_name="s")

    def body(rows_hbm, idx_hbm, out_hbm, ia0, ib0, r0, ia1, ib1, r1, ls0, ls1, ss0, ss1):
        wid = _sc_worker_id()
        bufs = ((ia0, ib0, r0, ls0, ss0), (ia1, ib1, r1, ls1, ss1))

        def loads(j, b):
            ia, ib, r, ls, _ = bufs[b]
            base = pl.multiple_of(wid * per_worker + j * SC_CHUNK, SC_CHUNK)
            return (pltpu.make_async_copy(rows_hbm.at[pl.ds(base, SC_CHUNK)], r, ls),
                    pltpu.make_async_copy(idx_hbm.at[pl.ds(base, SC_CHUNK)], ia, ls),
                    pltpu.make_async_copy(idx_hbm.at[pl.ds(n_rows + base, SC_CHUNK)], ib, ls))

        def scatters(b):
            ia, ib, r, _, ss = bufs[b]
            return (pltpu.make_async_copy(r, out_hbm.at[ia], ss),
                    pltpu.make_async_copy(r, out_hbm.at[ib], ss))

        for c in loads(0, 0):
            c.start()

        @pl.loop(0, n_chunks, step=2)
        def _(j0):
            for b in range(2):
                j = j0 + b
                for c in loads(j, b):
                    c.wait()

                @pl.when(j >= 1)
                def _():
                    for c in scatters(1 - b):
                        c.wait()

                @pl.when(j + 1 < n_chunks)
                def _():
                    for c in loads(j + 1, 1 - b):
                        c.start()

                for c in scatters(b):
                    c.start()

        for c in scatters((n_chunks - 1) % 2):
            c.wait()

    idx_t = pltpu.VMEM((SC_CHUNK,), I32)
    row_t = pltpu.VMEM((SC_CHUNK, width), I32)
    sem_t = pltpu.SemaphoreType.DMA
    return pl.kernel(
        body,
        out_type=jax.ShapeDtypeStruct((n_out, width), I32),
        mesh=mesh,
        scratch_types=[idx_t, idx_t, row_t, idx_t, idx_t, row_t, sem_t, sem_t, sem_t, sem_t],
        name="sc_scatter",
    )(rows, idx)


def _sc_gather_rows(table, idx):
    n_rows = idx.shape[0]
    width = table.shape[1]
    per_worker = n_rows // SC_WORKERS
    n_chunks = per_worker // SC_CHUNK
    assert n_chunks % 2 == 0
    mesh = plsc.VectorSubcoreMesh(core_axis_name="c", subcore_axis_name="s")

    def body(table_hbm, idx_hbm, out_hbm, i0, r0, i1, r1, is0, is1, gs0, gs1, ws0, ws1):
        wid = _sc_worker_id()
        bufs = ((i0, r0, is0, gs0, ws0), (i1, r1, is1, gs1, ws1))

        def chunk_base(j):
            return pl.multiple_of(wid * per_worker + j * SC_CHUNK, SC_CHUNK)

        def idx_load(j, b):
            return pltpu.make_async_copy(idx_hbm.at[pl.ds(chunk_base(j), SC_CHUNK)], bufs[b][0], bufs[b][2])

        def gather(b):
            return pltpu.make_async_copy(table_hbm.at[bufs[b][0]], bufs[b][1], bufs[b][3])

        def writeback(j, b):
            return pltpu.make_async_copy(bufs[b][1], out_hbm.at[pl.ds(chunk_base(j), SC_CHUNK)], bufs[b][4])

        idx_load(0, 0).start()
        idx_load(0, 0).wait()
        gather(0).start()
        idx_load(1, 1).start()

        @pl.loop(0, n_chunks, step=2)
        def _(j0):
            for b in range(2):
                j = j0 + b

                @pl.when(j + 1 < n_chunks)
                def _():
                    idx_load(j + 1, 1 - b).wait()

                    @pl.when(j >= 1)
                    def _():
                        writeback(j - 1, 1 - b).wait()
                    gather(1 - b).start()

                gather(b).wait()
                writeback(j, b).start()

                @pl.when(j + 2 < n_chunks)
                def _():
                    idx_load(j + 2, b).start()

        writeback(n_chunks - 2, 0).wait()
        writeback(n_chunks - 1, 1).wait()

    idx_t = pltpu.VMEM((SC_CHUNK,), I32)
    row_t = pltpu.VMEM((SC_CHUNK, width), I32)
    sem_t = pltpu.SemaphoreType.DMA
    return pl.kernel(
        body,
        out_type=jax.ShapeDtypeStruct((n_rows, width), I32),
        mesh=mesh,
        scratch_types=[idx_t, row_t, idx_t, row_t, sem_t, sem_t, sem_t, sem_t, sem_t, sem_t],
        name="sc_gather",
    )(table, idx)


def _experts_kernel(te_ref, nu_ref, tv_ref, ne_ref, xs_ref, wg_hbm, wu_hbm, wd_hbm, y_ref,
                    wgs_ref, wus_ref, wds_ref, wgb_ref, wub_ref, wdb_ref, wsem):
    i = pl.program_id(0)
    e = te_ref[i]
    e_prev = te_ref[jnp.maximum(i - 1, 0)]
    new_expert = jnp.logical_or(i == 0, e != e_prev)
    used = i < nu_ref[0]

    def weight_copies(expert):
        return (pltpu.make_async_copy(wg_hbm.at[expert], wgs_ref, wsem),
                pltpu.make_async_copy(wu_hbm.at[expert], wus_ref, wsem),
                pltpu.make_async_copy(wd_hbm.at[expert], wds_ref, wsem))

    @pl.when(i == 0)
    def _():
        for c in weight_copies(e):
            c.start()

    @pl.when(jnp.logical_and(used, new_expert))
    def _():
        for c in weight_copies(e):
            c.wait()
        wgb_ref[...] = wgs_ref[...].astype(BF16)
        wub_ref[...] = wus_ref[...].astype(BF16)
        wdb_ref[...] = wds_ref[...].astype(BF16)

        @pl.when(ne_ref[i] >= 0)
        def _():
            for c in weight_copies(ne_ref[i]):
                c.start()

    @pl.when(used)
    def _():
        row = lax.broadcasted_iota(I32, xs_ref.shape, 0)
        x_lo, x_hi = _unpack_rows(jnp.where(row < tv_ref[i], xs_ref[...], 0))
        xb = jnp.concatenate([x_lo.astype(BF16), x_hi.astype(BF16)], axis=1)
        gate = jnp.dot(xb, wgb_ref[...], preferred_element_type=F32)
        up = jnp.dot(xb, wub_ref[...], preferred_element_type=F32)
        hid = (gate * jax.nn.sigmoid(gate) * up).astype(BF16)
        y_ref[...] = _pack_rows(jnp.dot(hid, wdb_ref[...], preferred_element_type=F32))

    @pl.when(jnp.logical_not(used))
    def _():
        y_ref[...] = jnp.zeros_like(y_ref)


def _experts(xs, tile_e, n_used, tile_valid, next_e, w_gate, w_up, w_down):
    cap, dw = xs.shape
    d = 2 * dw
    tm = TM_EXPERT
    n_tiles = cap // tm

    def x_map(i, te, nu, tv, ne):
        return (jnp.minimum(i, jnp.maximum(nu[0] - 1, 0)), 0)

    return pl.pallas_call(
        _experts_kernel,
        grid_spec=pltpu.PrefetchScalarGridSpec(
            num_scalar_prefetch=4,
            grid=(n_tiles,),
            in_specs=[
                pl.BlockSpec((tm, dw), x_map),
                pl.BlockSpec(memory_space=pl.ANY),
                pl.BlockSpec(memory_space=pl.ANY),
                pl.BlockSpec(memory_space=pl.ANY),
            ],
            out_specs=pl.BlockSpec((tm, dw), lambda i, te, nu, tv, ne: (i, 0)),
            scratch_shapes=[pltpu.VMEM((d, D_FF), F32), pltpu.VMEM((d, D_FF), F32), pltpu.VMEM((D_FF, d), F32),
                            pltpu.VMEM((d, D_FF), BF16), pltpu.VMEM((d, D_FF), BF16), pltpu.VMEM((D_FF, d), BF16),
                            pltpu.SemaphoreType.DMA(())],
        ),
        out_shape=jax.ShapeDtypeStruct((cap, dw), I32),
        compiler_params=pltpu.CompilerParams(
            dimension_semantics=("arbitrary",), vmem_limit_bytes=VMEM_LIMIT),
        name="experts",
    )(tile_e, n_used, tile_valid, next_e, xs, w_gate, w_up, w_down)


def _combine_stream_kernel(x2_ref, y1_ref, y2_ref, rw_ref, gf_ref, out_ref):
    w_t = jnp.transpose(rw_ref[...])
    a_lo, a_hi = _unpack_rows(y1_ref[...])
    b_lo, b_hi = _unpack_rows(y2_ref[...])
    w1 = w_t[:, 0:1]
    w2 = w_t[:, 1:2]
    moe = jnp.concatenate([w1 * a_lo + w2 * b_lo, w1 * a_hi + w2 * b_hi], axis=1)
    x3 = x2_ref[...] + moe
    ms = jnp.mean(x3 * x3, axis=-1, keepdims=True)
    out_ref[...] = x3 * lax.rsqrt(ms + RMS_EPS) * gf_ref[...]


def _combine_stream(x2, yg, rw, norm_final):
    b_sz, s_len, d = x2.shape
    tc = TC_COMBINE
    n_s = s_len // tc
    n_steps = b_sz * n_s
    return pl.pallas_call(
        _combine_stream_kernel,
        grid=(n_steps,),
        in_specs=[
            pl.BlockSpec((tc, d), lambda g: (g, 0)),
            pl.BlockSpec((tc, d // 2), lambda g: (g, 0)),
            pl.BlockSpec((tc, d // 2), lambda g: (g + n_steps, 0)),
            pl.BlockSpec((None, 2 * N_HEADS, tc), lambda g: (g // n_s, 0, g % n_s)),
            pl.BlockSpec((1, d), lambda g: (0, 0)),
        ],
        out_specs=pl.BlockSpec((tc, d), lambda g: (g, 0)),
        out_shape=jax.ShapeDtypeStruct((b_sz * s_len, d), F32),
        compiler_params=pltpu.CompilerParams(
            dimension_semantics=("arbitrary",), vmem_limit_bytes=VMEM_LIMIT),
        name="combine_stream",
    )(x2.reshape(b_sz * s_len, d), yg, yg, rw, norm_final[None, :]).reshape(b_sz, s_len, d)


def _layer(x, norm_mix, w_in, b_in, conv_w, w_out, norm_ffn, w_group, b_group, w_expert, b_expert,
           w_gate, w_up, w_down, norm_out):
    b_sz, s_len, d = x.shape
    n_tok = b_sz * s_len
    a, qvt, gt = _inproj(x, norm_mix, w_in, b_in)

    w_router = jnp.zeros((d, ROUTER_COLS), F32)
    w_router = w_router.at[:, 0:N_GROUPS].set(w_group).at[:, EXPERT_COL0:EXPERT_COL0 + N_EXPERTS].set(w_expert)
    b_router = jnp.zeros((1, ROUTER_COLS), F32)
    b_router = b_router.at[0, 0:N_GROUPS].set(b_group).at[0, EXPERT_COL0:EXPERT_COL0 + N_EXPERTS].set(b_expert)
    x2, hn2, ri, rw, counts = _mixer(x, a, qvt, gt, conv_w, w_out, norm_ffn, w_router, b_router)

    tm = TM_EXPERT
    cnt = counts[:, 0]
    padded = ((cnt + tm - 1) // tm) * tm
    pends = jnp.cumsum(padded).astype(I32)
    seg = jnp.concatenate([jnp.zeros((1,), I32), pends])
    n_tiles = (n_tok * 2) // tm + N_EXPERTS
    cap = n_tiles * tm
    tile_start = jnp.arange(n_tiles, dtype=I32) * tm
    tile_e = jnp.minimum(jnp.sum((tile_start[:, None] >= pends[None, :]).astype(I32), axis=1),
                         N_EXPERTS - 1)
    n_used = pends[-1:] // tm
    e_sel = ri[:, 0:2, :]
    seg_start = jnp.sum(jnp.where(e_sel[..., None] == jnp.arange(N_EXPERTS, dtype=I32), seg[:-1], 0), axis=-1)
    pos = ri[:, 2:4, :] + seg_start

    seg_tok_end = seg[:-1] + cnt.astype(I32)
    tile_tok_end = jnp.sum(jnp.where(tile_e[:, None] == jnp.arange(N_EXPERTS, dtype=I32), seg_tok_end, 0), axis=-1)
    tile_valid = jnp.clip(tile_tok_end - tile_start, 0, tm).astype(I32)
    pos_flat = jnp.concatenate([pos[:, 0, :].reshape(-1), pos[:, 1, :].reshape(-1)])

    expert_ids = jnp.arange(N_EXPERTS, dtype=I32)
    seg_end_tile = jnp.sum(jnp.where(tile_e[:, None] == expert_ids, pends // tm, 0), axis=-1)
    tile_ids = jnp.arange(n_tiles, dtype=I32)
    e_after = jnp.sum(jnp.where(seg_end_tile[:, None] == tile_ids, tile_e, 0), axis=-1)
    next_e = jnp.where(seg_end_tile < n_used[0], e_after, -1).astype(I32)

    xs = _sc_scatter_rows(hn2.reshape(n_tok, d // 2), pos_flat, cap)
    y = _experts(xs, tile_e, n_used, tile_valid, next_e, w_gate, w_up, w_down)
    yg = _sc_gather_rows(y, pos_flat)
    return _combine_stream(x2, yg, rw, norm_out)


def kernel(x, norm_mix, w_in, b_in, conv_w, w_out, norm_ffn, w_group, b_group, w_expert, b_expert,
           w_gate, w_up, w_down, norm_final):
    depth = norm_mix.shape[0]
    assert depth == 1, "final-norm fusion below assumes a single layer"
    assert x.shape[-1] == 2 * D_MLSTM
    assert all(x.shape[1] % t == 0 for t in (TS_MIXER, TM_INPROJ, TC_COMBINE))
    return _layer(x, norm_mix[0], w_in[0], b_in[0], conv_w[0], w_out[0], norm_ffn[0],
                  w_group[0], b_group[0], w_expert[0], b_expert[0],
                  w_gate[0], w_up[0], w_down[0], norm_final)
```

```python
import jax
import jax.numpy as jnp
from jax import lax
from jax.experimental import pallas as pl
from jax.experimental.pallas import tpu as pltpu
from jax.experimental.pallas import tpu_sc as plsc

F32 = jnp.float32
BF16 = jnp.bfloat16
I32 = jnp.int32

N_HEADS = 4
HEAD_DIM = 128
D_MLSTM = N_HEADS * HEAD_DIM
D_CONV = 512
N_GROUPS = 4
EXPERTS_PER_GROUP = 8
N_EXPERTS = N_GROUPS * EXPERTS_PER_GROUP
D_FF = 512
RMS_EPS = 1e-6
Q_SCALE = HEAD_DIM ** -0.5

LANES = 128
CHUNK = LANES
AUG_ROWS = 16
ROUTER_COLS = 128
EXPERT_COL0 = 8

TM_INPROJ = 1024
TS_MIXER = 1024
OUT_COLS = 256
TM_EXPERT = 512
TC_COMBINE = 1024

VMEM_LIMIT = 56 * 1024 * 1024


HI_HALF = -65536


def _pack_rows(x):
    w = x.shape[1] // 2
    lo = lax.shift_right_logical(lax.bitcast_convert_type(x[:, :w].astype(BF16).astype(F32), I32), 16)
    hi = lax.bitcast_convert_type(x[:, w:].astype(BF16).astype(F32), I32) & HI_HALF
    return lo | hi


def _unpack_rows(words):
    lo = lax.bitcast_convert_type(words << 16, F32)
    hi = lax.bitcast_convert_type(words & HI_HALF, F32)
    return lo, hi


def _nt_dot(a, b):
    return lax.dot_general(a, b, (((1,), (1,)), ((), ())), preferred_element_type=F32)


def _inproj_kernel(x_ref, g_ref, wa_ref, ba_ref, wt_ref, bt_ref, a_ref, qvt_ref, gt_ref):
    x = x_ref[...]
    ms = jnp.mean(x * x, axis=-1, keepdims=True)
    hn = (x * lax.rsqrt(ms + RMS_EPS) * g_ref[...]).astype(BF16)
    n_blk = wa_ref.shape[1] // D_MLSTM
    for j in range(n_blk):
        sl = slice(j * D_MLSTM, (j + 1) * D_MLSTM)
        acc = jnp.dot(hn, wa_ref[:, sl], preferred_element_type=F32) + ba_ref[:, sl]
        a_ref[:, sl] = acc.astype(BF16)
    tr = _nt_dot(wt_ref[...], hn) + bt_ref[...]
    q_rows = slice(0, D_MLSTM)
    v_rows = slice(D_MLSTM, 2 * D_MLSTM)
    qvt_ref[q_rows, :] = (tr[q_rows, :] * Q_SCALE).astype(BF16)
    qvt_ref[v_rows, :] = tr[v_rows, :].astype(BF16)
    gt_ref[...] = tr[2 * D_MLSTM:2 * D_MLSTM + 2 * N_HEADS, :]


def _inproj(x, norm_mix, w_in, b_in):
    b_sz, s_len, d = x.shape
    tm = TM_INPROJ
    dm = D_MLSTM
    g0 = 4 * dm
    c0 = g0 + 2 * N_HEADS
    cols_a = jnp.concatenate([w_in[:, dm:2 * dm], w_in[:, 3 * dm:4 * dm], w_in[:, c0:c0 + 3 * D_CONV]], axis=1)
    bias_a = jnp.concatenate([b_in[dm:2 * dm], b_in[3 * dm:4 * dm], b_in[c0:c0 + 3 * D_CONV]])[None, :]
    n_gate = 2 * N_HEADS
    pad_rows = AUG_ROWS - n_gate
    w_t = jnp.concatenate([w_in[:, 0:dm], w_in[:, 2 * dm:3 * dm], w_in[:, g0:c0],
                           jnp.zeros((d, pad_rows), w_in.dtype)], axis=1).T
    b_t = jnp.concatenate([b_in[0:dm], b_in[2 * dm:3 * dm], b_in[g0:c0],
                           jnp.zeros((pad_rows,), b_in.dtype)])[:, None]
    n_t = 2 * dm + AUG_ROWS
    n_a = cols_a.shape[1]
    return pl.pallas_call(
        _inproj_kernel,
        grid=(b_sz, s_len // tm),
        in_specs=[
            pl.BlockSpec((None, tm, d), lambda b, s: (b, s, 0)),
            pl.BlockSpec((1, d), lambda b, s: (0, 0)),
            pl.BlockSpec((d, n_a), lambda b, s: (0, 0)),
            pl.BlockSpec((1, n_a), lambda b, s: (0, 0)),
            pl.BlockSpec((n_t, d), lambda b, s: (0, 0)),
            pl.BlockSpec((n_t, 1), lambda b, s: (0, 0)),
        ],
        out_specs=[
            pl.BlockSpec((None, tm, n_a), lambda b, s: (b, s, 0)),
            pl.BlockSpec((None, 2 * dm, tm), lambda b, s: (b, 0, s)),
            pl.BlockSpec((None, 2 * N_HEADS, tm), lambda b, s: (b, 0, s)),
        ],
        out_shape=[
            jax.ShapeDtypeStruct((b_sz, s_len, n_a), BF16),
            jax.ShapeDtypeStruct((b_sz, 2 * dm, s_len), BF16),
            jax.ShapeDtypeStruct((b_sz, 2 * N_HEADS, s_len), F32),
        ],
        compiler_params=pltpu.CompilerParams(
            dimension_semantics=("arbitrary", "arbitrary"), vmem_limit_bytes=VMEM_LIMIT),
        name="inproj",
    )(x, norm_mix[None, :], cols_a.astype(BF16), bias_a, w_t.astype(BF16), b_t)


def _mixer_kernel(k_ref, o_ref, cb_ref, cc_ref, cx_ref, qt_ref, vt_ref, gt_ref, x_ref,
                  cw_ref, wo_ref, g2_ref, wr_ref, br_ref,
                  x2_ref, hn2_ref, ri_ref, rw_ref, cnt_out_ref,
                  ct_ref, m_ref, carry_ref, cnt_ref, mix_ref):
    ts = x_ref.shape[0]
    n_chunks = ts // CHUNK
    first_tile = pl.program_id(1) == 0

    @pl.when(first_tile)
    def _():
        ct_ref[...] = jnp.zeros_like(ct_ref)
        m_ref[...] = jnp.zeros_like(m_ref)
        carry_ref[...] = jnp.zeros_like(carry_ref)

    @pl.when(jnp.logical_and(first_tile, pl.program_id(0) == 0))
    def _():
        cnt_ref[...] = jnp.zeros_like(cnt_ref)

    gates = gt_ref[...]
    li_all = gates[0:N_HEADS]
    fg = gates[N_HEADS:2 * N_HEADS]
    lf_all = jnp.minimum(fg, 0.0) - jnp.log1p(jnp.exp(-jnp.abs(fg)))
    key_j = lax.broadcasted_iota(I32, (CHUNK, CHUNK), 0)
    qry_i = lax.broadcasted_iota(I32, (CHUNK, CHUNK), 1)
    causal_t = key_j <= qry_i
    neg_inf = jnp.float32(-jnp.inf)
    ones_rows = (lax.broadcasted_iota(I32, (AUG_ROWS, CHUNK), 0) == 0).astype(BF16)

    prefix_ones = (key_j <= qry_i).astype(BF16)

    lf_hi = lf_all.astype(BF16)
    rem = lf_all - lf_hi.astype(F32)
    lf_mid = rem.astype(BF16)
    lf_lo = (rem - lf_mid.astype(F32)).astype(BF16)
    zero_bf = jnp.zeros((N_HEADS, CHUNK), BF16)
    parts = []
    for c in range(n_chunks):
        rows = slice(c * CHUNK, (c + 1) * CHUNK)
        parts += [lf_hi[:, rows], lf_mid[:, rows], lf_lo[:, rows], zero_bf]
    cum = jnp.dot(jnp.concatenate(parts, axis=0), prefix_ones, preferred_element_type=F32)

    m_prev = m_ref[0:N_HEADS, 0:1]
    for c in range(n_chunks):
        rows = slice(c * CHUNK, (c + 1) * CHUNK)
        li = li_all[:, rows]
        r0 = 4 * N_HEADS * c
        b_cum = (cum[r0:r0 + N_HEADS] + cum[r0 + N_HEADS:r0 + 2 * N_HEADS]
                 + cum[r0 + 2 * N_HEADS:r0 + 3 * N_HEADS])
        g_tot = b_cum[:, CHUNK - 1:CHUNK]
        a_end = g_tot - b_cum + li
        m_loc = jnp.max(a_end, axis=1, keepdims=True)
        u = li - b_cum
        m_new = jnp.maximum(g_tot + m_prev, m_loc)
        decay = jnp.exp(g_tot + m_prev - m_new)
        s_fac = jnp.exp(m_loc - m_new)
        w_key = jnp.exp(a_end - m_loc)
        u_cols = jnp.transpose(jnp.concatenate([u, jnp.zeros_like(u)], axis=0))
        for h in range(N_HEADS):
            hc = slice(h * HEAD_DIM, (h + 1) * HEAD_DIM)
            kh = k_ref[rows, hc]
            qth = qt_ref[hc, rows]
            vt_aug = jnp.concatenate([vt_ref[hc, rows], ones_rows], axis=0)
            s_t = jnp.dot(kh, qth, preferred_element_type=F32)
            u_t = jnp.where(causal_t, u_cols[:, h:h + 1], neg_inf)
            mp = m_prev[h:h + 1, :]
            m_tot = jnp.maximum(jnp.max(u_t, axis=0, keepdims=True), mp)
            f_inter = jnp.exp(mp - m_tot)
            floor = jnp.exp(-b_cum[h:h + 1, :] - m_tot)
            p_t = (s_t * jnp.exp(u_t - m_tot)).astype(BF16)
            q_f = (qth.astype(F32) * f_inter).astype(BF16)
            c_old = ct_ref[h]
            lhs = jnp.concatenate([vt_aug, c_old.astype(BF16)], axis=1)
            rhs = jnp.concatenate([p_t, q_f], axis=0)
            r_t = jnp.dot(lhs, rhs, preferred_element_type=F32)
            vtw = (vt_aug.astype(F32) * w_key[h:h + 1, :]).astype(BF16)
            kv = jnp.dot(vtw, kh, preferred_element_type=F32)
            ct_ref[h] = decay[h:h + 1, :] * c_old + s_fac[h:h + 1, :] * kv

            den = r_t[HEAD_DIM:HEAD_DIM + 1, :]
            inv = 1.0 / jnp.maximum(jnp.abs(den), floor)
            h_nat = jnp.transpose(r_t[0:HEAD_DIM, :] * inv)
            o_gate = jax.nn.sigmoid(o_ref[rows, hc].astype(F32))
            mix_ref[rows, hc] = (o_gate * h_nat).astype(BF16)
        m_prev = m_new
    m_ref[0:N_HEADS, :] = jnp.broadcast_to(m_prev, (N_HEADS, LANES))

    u_c = cc_ref[...].astype(F32) * cx_ref[...].astype(F32)
    rid = lax.broadcasted_iota(I32, u_c.shape, 0)
    prev1 = carry_ref[0:1, :]
    prev2 = carry_ref[1:2, :]
    u1 = jnp.where(rid == 0, prev1, pltpu.roll(u_c, 1, axis=0))
    u2 = jnp.where(rid == 0, prev2, jnp.where(rid == 1, prev1, pltpu.roll(u_c, 2, axis=0)))
    z = cw_ref[0:1, :] * u2 + cw_ref[1:2, :] * u1 + cw_ref[2:3, :] * u_c
    mix_ref[:, D_MLSTM:] = (cb_ref[...].astype(F32) * z).astype(BF16)
    carry_ref[0:1, :] = u_c[ts - 1:ts, :]
    carry_ref[1:2, :] = u_c[ts - 2:ts - 1, :]

    d_model = x_ref.shape[1]
    mix = mix_ref[...]
    ssq = jnp.zeros((ts, 1), F32)
    for j in range(d_model // OUT_COLS):
        cs = slice(j * OUT_COLS, (j + 1) * OUT_COLS)
        x2c = x_ref[:, cs] + jnp.dot(mix, wo_ref[:, cs], preferred_element_type=F32)
        x2_ref[:, cs] = x2c
        ssq = ssq + jnp.sum(x2c * x2c, axis=-1, keepdims=True)
    hn2 = x2_ref[...] * lax.rsqrt(ssq * (1.0 / d_model) + RMS_EPS) * g2_ref[...]
    hn2_ref[...] = _pack_rows(hn2)

    h_hi = hn2.astype(BF16)
    h_lo = (hn2 - h_hi.astype(F32)).astype(BF16)
    hi_both = jnp.dot(h_hi, wr_ref[...], preferred_element_type=F32)
    lo_hi = jnp.dot(h_lo, wr_ref[:, 0:ROUTER_COLS], preferred_element_type=F32)
    logits = hi_both[:, 0:ROUTER_COLS] + (hi_both[:, ROUTER_COLS:] + lo_hi) + br_ref[...]
    lt = jnp.transpose(logits)
    gl = lt[0:N_GROUPS]
    gmax = jnp.max(gl, axis=0, keepdims=True)
    gi = lax.broadcasted_iota(I32, gl.shape, 0)
    g_sel = jnp.min(jnp.where(gl == gmax, gi, N_GROUPS), axis=0, keepdims=True)
    p_g = 1.0 / jnp.sum(jnp.exp(gl - gmax), axis=0, keepdims=True)
    in_g = lt[EXPERT_COL0:EXPERT_COL0 + EXPERTS_PER_GROUP]
    for g in range(1, N_GROUPS):
        lo = EXPERT_COL0 + g * EXPERTS_PER_GROUP
        in_g = jnp.where(g_sel == g, lt[lo:lo + EXPERTS_PER_GROUP], in_g)
    ei = lax.broadcasted_iota(I32, in_g.shape, 0)
    v1 = jnp.max(in_g, axis=0, keepdims=True)
    i1 = jnp.min(jnp.where(in_g == v1, ei, EXPERTS_PER_GROUP), axis=0, keepdims=True)
    rest = jnp.where(ei == i1, neg_inf, in_g)
    v2 = jnp.max(rest, axis=0, keepdims=True)
    i2 = jnp.min(jnp.where(rest == v2, ei, EXPERTS_PER_GROUP), axis=0, keepdims=True)
    t = jnp.exp(v2 - v1)
    w1 = p_g / (1.0 + t)
    w2 = p_g * t / (1.0 + t)
    e1 = g_sel * EXPERTS_PER_GROUP + i1
    e2 = g_sel * EXPERTS_PER_GROUP + i2

    eio = lax.broadcasted_iota(I32, (N_EXPERTS, LANES), 0)
    base = cnt_ref[:, 0:1]
    zero_i = jnp.zeros((1, LANES), I32)
    n_blk = ts // LANES
    hits = []
    for k in range(n_blk):
        ls = slice(k * LANES, (k + 1) * LANES)
        hits.append((eio == e1[:, ls], eio == e2[:, ls]))
    onehots = [jnp.where(jnp.logical_or(a1, a2), 1.0, 0.0) for a1, a2 in hits]
    incl_all = jnp.dot(jnp.concatenate(onehots, axis=0).astype(BF16), prefix_ones,
                       preferred_element_type=F32).astype(I32)
    for k in range(n_blk):
        ls = slice(k * LANES, (k + 1) * LANES)
        is1, is2 = hits[k]
        onehot = onehots[k].astype(I32)
        incl = incl_all[k * N_EXPERTS:(k + 1) * N_EXPERTS]
        rank_e = base + incl - onehot
        r1 = jnp.sum(jnp.where(is1, rank_e, 0), axis=0, keepdims=True)
        r2 = jnp.sum(jnp.where(is2, rank_e, 0), axis=0, keepdims=True)
        ri_ref[:, ls] = jnp.concatenate(
            [e1[:, ls], e2[:, ls], r1, r2, zero_i, zero_i, zero_i, zero_i], axis=0)
        base = base + incl[:, LANES - 1:LANES]
    zero_f = jnp.zeros_like(w1)
    rw_ref[...] = jnp.concatenate([w1, w2, zero_f, zero_f, zero_f, zero_f, zero_f, zero_f], axis=0)
    cnt_new = jnp.broadcast_to(base, (N_EXPERTS, LANES))
    cnt_ref[...] = cnt_new
    cnt_out_ref[...] = cnt_new


def _mixer(x, a, qvt, gt, conv_w, w_out, norm_ffn, w_router, b_router):
    b_sz, s_len, d = x.shape
    ts = TS_MIXER
    dm = D_MLSTM

    def a_spec(j):
        return pl.BlockSpec((None, ts, dm), lambda b, s, j=j: (b, s, j))

    tok = lambda w: pl.BlockSpec((None, ts, w), lambda b, s: (b, s, 0))
    rowsp = pl.BlockSpec((None, 2 * N_HEADS, ts), lambda b, s: (b, 0, s))
    const = lambda shape: pl.BlockSpec(shape, lambda b, s: tuple(0 for _ in shape))
    wr_hi = w_router.astype(BF16)
    wr_lo = (w_router - wr_hi.astype(F32)).astype(BF16)
    w_router2 = jnp.concatenate([wr_hi, wr_lo], axis=1)
    return pl.pallas_call(
        _mixer_kernel,
        grid=(b_sz, s_len // ts),
        in_specs=[a_spec(0), a_spec(1), a_spec(2), a_spec(3), a_spec(4),
                  pl.BlockSpec((None, dm, ts), lambda b, s: (b, 0, s)),
                  pl.BlockSpec((None, dm, ts), lambda b, s: (b, 1, s)),
                  rowsp, tok(d),
                  const((3, D_CONV)), const((d, d)), const((1, d)),
                  const((d, 2 * ROUTER_COLS)), const((1, ROUTER_COLS))],
        out_specs=[tok(d), tok(d // 2), rowsp, rowsp, const((N_EXPERTS, LANES))],
        out_shape=[
            jax.ShapeDtypeStruct((b_sz, s_len, d), F32),
            jax.ShapeDtypeStruct((b_sz, s_len, d // 2), I32),
            jax.ShapeDtypeStruct((b_sz, 2 * N_HEADS, s_len), I32),
            jax.ShapeDtypeStruct((b_sz, 2 * N_HEADS, s_len), F32),
            jax.ShapeDtypeStruct((N_EXPERTS, LANES), I32),
        ],
        scratch_shapes=[
            pltpu.VMEM((N_HEADS, HEAD_DIM + AUG_ROWS, HEAD_DIM), F32),
            pltpu.VMEM((2 * N_HEADS, LANES), F32),
            pltpu.VMEM((8, D_CONV), F32),
            pltpu.VMEM((N_EXPERTS, LANES), I32),
            pltpu.VMEM((ts, d), BF16),
        ],
        compiler_params=pltpu.CompilerParams(
            dimension_semantics=("arbitrary", "arbitrary"), vmem_limit_bytes=VMEM_LIMIT),
        name="mixer",
    )(a, a, a, a, a, qvt, qvt, gt, x, conv_w, w_out.astype(BF16), norm_ffn[None, :], w_router2, b_router)


SC_CORES = 2
SC_SUBCORES = 16
SC_WORKERS = SC_CORES * SC_SUBCORES
SC_CHUNK = 64


def _sc_worker_id():
    return lax.axis_index("s") * SC_CORES + lax.axis_index("c")


def _sc_scatter_rows(rows, idx, n_out):
    n_rows, width = rows.shape
    per_worker = n_rows // SC_WORKERS
    n_chunks = per_worker // SC_CHUNK
    assert n_chunks % 2 == 0
    mesh = plsc.VectorSubcoreMesh(core_axis_name="c", subcore_axis_name="s")

    def body(rows_hbm, idx_hbm, out_hbm, ia0, ib0, r0, ia1, ib1, r1, ls0, ls1, ss0, ss1):
        wid = _sc_worker_id()
        bufs = ((ia0, ib0, r0, ls0, ss0), (ia1, ib1, r1, ls1, ss1))

        def loads(j, b):
            ia, ib, r, ls, _ = bufs[b]
            base = pl.multiple_of(wid * per_worker + j * SC_CHUNK, SC_CHUNK)
            return (pltpu.make_async_copy(rows_hbm.at[pl.ds(base, SC_CHUNK)], r, ls),
                    pltpu.make_async_copy(idx_hbm.at[pl.ds(base, SC_CHUNK)], ia, ls),
                    pltpu.make_async_copy(idx_hbm.at[pl.ds(n_rows + base, SC_CHUNK)], ib, ls))

        def scatters(b):
            ia, ib, r, _, ss = bufs[b]
            return (pltpu.make_async_copy(r, out_hbm.at[ia], ss),
                    pltpu.make_async_copy(r, out_hbm.at[ib], ss))

        for c in loads(0, 0):
            c.start()

        @pl.loop(0, n_chunks, step=2)
        def _(j0):
            for b in range(2):
                j = j0 + b
                for c in loads(j, b):
                    c.wait()

                @pl.when(j >= 1)
                def _():
                    for c in scatters(1 - b):
                        c.wait()

                @pl.when(j + 1 < n_chunks)
                def _():
                    for c in loads(j + 1, 1 - b):
                        c.start()

                for c in scatters(b):
                    c.start()

        for c in scatters((n_chunks - 1) % 2):
            c.wait()

    idx_t = pltpu.VMEM((SC_CHUNK,), I32)
    row_t = pltpu.VMEM((SC_CHUNK, width), I32)
    sem_t = pltpu.SemaphoreType.DMA
    return pl.kernel(
        body,
        out_type=jax.ShapeDtypeStruct((n_out, width), I32),
        mesh=mesh,
        scratch_types=[idx_t, idx_t, row_t, idx_t, idx_t, row_t, sem_t, sem_t, sem_t, sem_t],
        name="sc_scatter",
    )(rows, idx)


def _sc_gather_rows(table, idx):
    n_rows = idx.shape[0]
    width = table.shape[1]
    per_worker = n_rows // SC_WORKERS
    n_chunks = per_worker // SC_CHUNK
    assert n_chunks % 2 == 0
    mesh = plsc.VectorSubcoreMesh(core_axis_name="c", subcore_axis_name="s")

    def body(table_hbm, idx_hbm, out_hbm, i0, r0, i1, r1, is0, is1, gs0, gs1, ws0, ws1):
        wid = _sc_worker_id()
        bufs = ((i0, r0, is0, gs0, ws0), (i1, r1, is1, gs1, ws1))

        def chunk_base(j):
            return pl.multiple_of(wid * per_worker + j * SC_CHUNK, SC_CHUNK)

        def idx_load(j, b):
            return pltpu.make_async_copy(idx_hbm.at[pl.ds(chunk_base(j), SC_CHUNK)], bufs[b][0], bufs[b][2])

        def gather(b):
            return pltpu.make_async_copy(table_hbm.at[bufs[b][0]], bufs[b][1], bufs[b][3])

        def writeback(j, b):
            return pltpu.make_async_copy(bufs[b][1], out_hbm.at[pl.ds(chunk_base(j), SC_CHUNK)], bufs[b][4])

        idx_load(0, 0).start()
        idx_load(0, 0).wait()
        gather(0).start()
        idx_load(1, 1).start()

        @pl.loop(0, n_chunks, step=2)
        def _(j0):
            for b in range(2):
                j = j0 + b

                @pl.when(j + 1 < n_chunks)
                def _():
                    idx_load(j + 1, 1 - b).wait()

                    @pl.when(j >= 1)
                    def _():
                        writeback(j - 1, 1 - b).wait()
                    gather(1 - b).start()

                gather(b).wait()
                writeback(j, b).start()

                @pl.when(j + 2 < n_chunks)
                def _():
                    idx_load(j + 2, b).start()

        writeback(n_chunks - 2, 0).wait()
        writeback(n_chunks - 1, 1).wait()

    idx_t = pltpu.VMEM((SC_CHUNK,), I32)
    row_t = pltpu.VMEM((SC_CHUNK, width), I32)
    sem_t = pltpu.SemaphoreType.DMA
    return pl.kernel(
        body,
        out_type=jax.ShapeDtypeStruct((n_rows, width), I32),
        mesh=mesh,
        scratch_types=[idx_t, row_t, idx_t, row_t, sem_t, sem_t, sem_t, sem_t, sem_t, sem_t],
        name="sc_gather",
    )(table, idx)


def _experts_kernel(te_ref, nu_ref, tv_ref, ne_ref, xs_ref, wg_hbm, wu_hbm, wd_hbm, y_ref,
                    wgs_ref, wus_ref, wds_ref, wgb_ref, wub_ref, wdb_ref, wsem):
    i = pl.program_id(0)
    e = te_ref[i]
    e_prev = te_ref[jnp.maximum(i - 1, 0)]
    new_expert = jnp.logical_or(i == 0, e != e_prev)
    used = i < nu_ref[0]

    def weight_copies(expert):
        return (pltpu.make_async_copy(wg_hbm.at[expert], wgs_ref, wsem),
                pltpu.make_async_copy(wu_hbm.at[expert], wus_ref, wsem),
                pltpu.make_async_copy(wd_hbm.at[expert], wds_ref, wsem))

    @pl.when(i == 0)
    def _():
        for c in weight_copies(e):
            c.start()

    @pl.when(jnp.logical_and(used, new_expert))
    def _():
        for c in weight_copies(e):
            c.wait()
        wgb_ref[...] = wgs_ref[...].astype(BF16)
        wub_ref[...] = wus_ref[...].astype(BF16)
        wdb_ref[...] = wds_ref[...].astype(BF16)

        @pl.when(ne_ref[i] >= 0)
        def _():
            for c in weight_copies(ne_ref[i]):
                c.start()

    @pl.when(used)
    def _():
        row = lax.broadcasted_iota(I32, xs_ref.shape, 0)
        x_lo, x_hi = _unpack_rows(jnp.where(row < tv_ref[i], xs_ref[...], 0))
        xb = jnp.concatenate([x_lo.astype(BF16), x_hi.astype(BF16)], axis=1)
        gate = jnp.dot(xb, wgb_ref[...], preferred_element_type=F32)
        up = jnp.dot(xb, wub_ref[...], preferred_element_type=F32)
        hid = (gate * jax.nn.sigmoid(gate) * up).astype(BF16)
        y_ref[...] = _pack_rows(jnp.dot(hid, wdb_ref[...], preferred_element_type=F32))

    @pl.when(jnp.logical_not(used))
    def _():
        y_ref[...] = jnp.zeros_like(y_ref)


def _experts(xs, tile_e, n_used, tile_valid, next_e, w_gate, w_up, w_down):
    cap, dw = xs.shape
    d = 2 * dw
    tm = TM_EXPERT
    n_tiles = cap // tm

    def x_map(i, te, nu, tv, ne):
        return (jnp.minimum(i, jnp.maximum(nu[0] - 1, 0)), 0)

    return pl.pallas_call(
        _experts_kernel,
        grid_spec=pltpu.PrefetchScalarGridSpec(
            num_scalar_prefetch=4,
            grid=(n_tiles,),
            in_specs=[
                pl.BlockSpec((tm, dw), x_map),
                pl.BlockSpec(memory_space=pl.ANY),
                pl.BlockSpec(memory_space=pl.ANY),
                pl.BlockSpec(memory_space=pl.ANY),
            ],
            out_specs=pl.BlockSpec((tm, dw), lambda i, te, nu, tv, ne: (i, 0)),
            scratch_shapes=[pltpu.VMEM((d, D_FF), F32), pltpu.VMEM((d, D_FF), F32), pltpu.VMEM((D_FF, d), F32),
                            pltpu.VMEM((d, D_FF), BF16), pltpu.VMEM((d, D_FF), BF16), pltpu.VMEM((D_FF, d), BF16),
                            pltpu.SemaphoreType.DMA(())],
        ),
        out_shape=jax.ShapeDtypeStruct((cap, dw), I32),
        compiler_params=pltpu.CompilerParams(
            dimension_semantics=("arbitrary",), vmem_limit_bytes=VMEM_LIMIT),
        name="experts",
    )(tile_e, n_used, tile_valid, next_e, xs, w_gate, w_up, w_down)


def _combine_stream_kernel(x2_ref, y1_ref, y2_ref, rw_ref, gf_ref, out_ref):
    w_t = jnp.transpose(rw_ref[...])
    a_lo, a_hi = _unpack_rows(y1_ref[...])
    b_lo, b_hi = _unpack_rows(y2_ref[...])
    w1 = w_t[:, 0:1]
    w2 = w_t[:, 1:2]
    moe = jnp.concatenate([w1 * a_lo + w2 * b_lo, w1 * a_hi + w2 * b_hi], axis=1)
    x3 = x2_ref[...] + moe
    ms = jnp.mean(x3 * x3, axis=-1, keepdims=True)
    out_ref[...] = x3 * lax.rsqrt(ms + RMS_EPS) * gf_ref[...]


def _combine_stream(x2, yg, rw, norm_final):
    b_sz, s_len, d = x2.shape
    tc = TC_COMBINE
    n_s = s_len // tc
    n_steps = b_sz * n_s
    return pl.pallas_call(
        _combine_stream_kernel,
        grid=(n_steps,),
        in_specs=[
            pl.BlockSpec((tc, d), lambda g: (g, 0)),
            pl.BlockSpec((tc, d // 2), lambda g: (g, 0)),
            pl.BlockSpec((tc, d // 2), lambda g: (g + n_steps, 0)),
            pl.BlockSpec((None, 2 * N_HEADS, tc), lambda g: (g // n_s, 0, g % n_s)),
            pl.BlockSpec((1, d), lambda g: (0, 0)),
        ],
        out_specs=pl.BlockSpec((tc, d), lambda g: (g, 0)),
        out_shape=jax.ShapeDtypeStruct((b_sz * s_len, d), F32),
        compiler_params=pltpu.CompilerParams(
            dimension_semantics=("arbitrary",), vmem_limit_bytes=VMEM_LIMIT),
        name="combine_stream",
    )(x2.reshape(b_sz * s_len, d), yg, yg, rw, norm_final[None, :]).reshape(b_sz, s_len, d)


def _layer(x, norm_mix, w_in, b_in, conv_w, w_out, norm_ffn, w_group, b_group, w_expert, b_expert,
           w_gate, w_up, w_down, norm_out):
    b_sz, s_len, d = x.shape
    n_tok = b_sz * s_len
    a, qvt, gt = _inproj(x, norm_mix, w_in, b_in)

    w_router = jnp.zeros((d, ROUTER_COLS), F32)
    w_router = w_router.at[:, 0:N_GROUPS].set(w_group).at[:, EXPERT_COL0:EXPERT_COL0 + N_EXPERTS].set(w_expert)
    b_router = jnp.zeros((1, ROUTER_COLS), F32)
    b_router = b_router.at[0, 0:N_GROUPS].set(b_group).at[0, EXPERT_COL0:EXPERT_COL0 + N_EXPERTS].set(b_expert)
    x2, hn2, ri, rw, counts = _mixer(x, a, qvt, gt, conv_w, w_out, norm_ffn, w_router, b_router)

    tm = TM_EXPERT
    cnt = counts[:, 0]
    padded = ((cnt + tm - 1) // tm) * tm
    pends = jnp.cumsum(padded).astype(I32)
    seg = jnp.concatenate([jnp.zeros((1,), I32), pends])
    n_tiles = (n_tok * 2) // tm + N_EXPERTS
    cap = n_tiles * tm
    tile_start = jnp.arange(n_tiles, dtype=I32) * tm
    tile_e = jnp.minimum(jnp.sum((tile_start[:, None] >= pends[None, :]).astype(I32), axis=1),
                         N_EXPERTS - 1)
    n_used = pends[-1:] // tm
    e_sel = ri[:, 0:2, :]
    seg_start = jnp.sum(jnp.where(e_sel[..., None] == jnp.arange(N_EXPERTS, dtype=I32), seg[:-1], 0), axis=-1)
    pos = ri[:, 2:4, :] + seg_start

    seg_tok_end = seg[:-1] + cnt.astype(I32)
    tile_tok_end = jnp.sum(jnp.where(tile_e[:, None] == jnp.arange(N_EXPERTS, dtype=I32), seg_tok_end, 0), axis=-1)
    tile_valid = jnp.clip(tile_tok_end - tile_start, 0, tm).astype(I32)
    pos_flat = jnp.concatenate([pos[:, 0, :].reshape(-1), pos[:, 1, :].reshape(-1)])

    expert_ids = jnp.arange(N_EXPERTS, dtype=I32)
    seg_end_tile = jnp.sum(jnp.where(tile_e[:, None] == expert_ids, pends // tm, 0), axis=-1)
    tile_ids = jnp.arange(n_tiles, dtype=I32)
    e_after = jnp.sum(jnp.where(seg_end_tile[:, None] == tile_ids, tile_e, 0), axis=-1)
    next_e = jnp.where(seg_end_tile < n_used[0], e_after, -1).astype(I32)

    xs = _sc_scatter_rows(hn2.reshape(n_tok, d // 2), pos_flat, cap)
    y = _experts(xs, tile_e, n_used, tile_valid, next_e, w_gate, w_up, w_down)
    yg = _sc_gather_rows(y, pos_flat)
    return _combine_stream(x2, yg, rw, norm_out)


def kernel(x, norm_mix, w_in, b_in, conv_w, w_out, norm_ffn, w_group, b_group, w_expert, b_expert,
           w_gate, w_up, w_down, norm_final):
    depth = norm_mix.shape[0]
    assert depth == 1, "final-norm fusion below assumes a single layer"
    assert x.shape[-1] == 2 * D_MLSTM
    assert all(x.shape[1] % t == 0 for t in (TS_MIXER, TM_INPROJ, TC_COMBINE))
    return _layer(x, norm_mix[0], w_in[0], b_in[0], conv_w[0], w_out[0], norm_ffn[0],
                  w_group[0], b_group[0], w_expert[0], b_expert[0],
                  w_gate[0], w_up[0], w_down[0], norm_final)
```

```python
import jax
import jax.numpy as jnp
from jax import lax
from jax.experimental import pallas as pl
from jax.experimental.pallas import tpu as pltpu
from jax.experimental.pallas import tpu_sc as plsc

F32 = jnp.float32
BF16 = jnp.bfloat16
I32 = jnp.int32

N_HEADS = 4
HEAD_DIM = 128
D_MLSTM = N_HEADS * HEAD_DIM
D_CONV = 512
N_GROUPS = 4
EXPERTS_PER_GROUP = 8
N_EXPERTS = N_GROUPS * EXPERTS_PER_GROUP
D_FF = 512
RMS_EPS = 1e-6
Q_SCALE = HEAD_DIM ** -0.5

LANES = 128
CHUNK = LANES
AUG_ROWS = 16
ROUTER_COLS = 128
EXPERT_COL0 = 8

TM_INPROJ = 1024
TS_MIXER = 1024
OUT_COLS = 256
TM_EXPERT = 512
TC_COMBINE = 1024

VMEM_LIMIT = 56 * 1024 * 1024


HI_HALF = -65536


def _pack_rows(x):
    w = x.shape[1] // 2
    lo = lax.shift_right_logical(lax.bitcast_convert_type(x[:, :w].astype(BF16).astype(F32), I32), 16)
    hi = lax.bitcast_convert_type(x[:, w:].astype(BF16).astype(F32), I32) & HI_HALF
    return lo | hi


def _unpack_rows(words):
    lo = lax.bitcast_convert_type(words << 16, F32)
    hi = lax.bitcast_convert_type(words & HI_HALF, F32)
    return lo, hi


def _nt_dot(a, b):
    return lax.dot_general(a, b, (((1,), (1,)), ((), ())), preferred_element_type=F32)


def _inproj_kernel(x_ref, g_ref, wa_ref, ba_ref, wt_ref, bt_ref, a_ref, qvt_ref, gt_ref):
    x = x_ref[...]
    ms = jnp.mean(x * x, axis=-1, keepdims=True)
    hn = (x * lax.rsqrt(ms + RMS_EPS) * g_ref[...]).astype(BF16)
    n_blk = wa_ref.shape[1] // D_MLSTM
    for j in range(n_blk):
        sl = slice(j * D_MLSTM, (j + 1) * D_MLSTM)
        acc = jnp.dot(hn, wa_ref[:, sl], preferred_element_type=F32) + ba_ref[:, sl]
        a_ref[:, sl] = acc.astype(BF16)
    tr = _nt_dot(wt_ref[...], hn) + bt_ref[...]
    q_rows = slice(0, D_MLSTM)
    v_rows = slice(D_MLSTM, 2 * D_MLSTM)
    qvt_ref[q_rows, :] = (tr[q_rows, :] * Q_SCALE).astype(BF16)
    qvt_ref[v_rows, :] = tr[v_rows, :].astype(BF16)
    gt_ref[...] = tr[2 * D_MLSTM:2 * D_MLSTM + 2 * N_HEADS, :]


def _inproj(x, norm_mix, w_in, b_in):
    b_sz, s_len, d = x.shape
    tm = TM_INPROJ
    dm = D_MLSTM
    g0 = 4 * dm
    c0 = g0 + 2 * N_HEADS
    cols_a = jnp.concatenate([w_in[:, dm:2 * dm], w_in[:, 3 * dm:4 * dm], w_in[:, c0:c0 + 3 * D_CONV]], axis=1)
    bias_a = jnp.concatenate([b_in[dm:2 * dm], b_in[3 * dm:4 * dm], b_in[c0:c0 + 3 * D_CONV]])[None, :]
    n_gate = 2 * N_HEADS
    pad_rows = AUG_ROWS - n_gate
    w_t = jnp.concatenate([w_in[:, 0:dm], w_in[:, 2 * dm:3 * dm], w_in[:, g0:c0],
                           jnp.zeros((d, pad_rows), w_in.dtype)], axis=1).T
    b_t = jnp.concatenate([b_in[0:dm], b_in[2 * dm:3 * dm], b_in[g0:c0],
                           jnp.zeros((pad_rows,), b_in.dtype)])[:, None]
    n_t = 2 * dm + AUG_ROWS
    n_a = cols_a.shape[1]
    return pl.pallas_call(
        _inproj_kernel,
        grid=(b_sz, s_len // tm),
        in_specs=[
            pl.BlockSpec((None, tm, d), lambda b, s: (b, s, 0)),
            pl.BlockSpec((1, d), lambda b, s: (0, 0)),
            pl.BlockSpec((d, n_a), lambda b, s: (0, 0)),
            pl.BlockSpec((1, n_a), lambda b, s: (0, 0)),
            pl.BlockSpec((n_t, d), lambda b, s: (0, 0)),
            pl.BlockSpec((n_t, 1), lambda b, s: (0, 0)),
        ],
        out_specs=[
            pl.BlockSpec((None, tm, n_a), lambda b, s: (b, s, 0)),
            pl.BlockSpec((None, 2 * dm, tm), lambda b, s: (b, 0, s)),
            pl.BlockSpec((None, 2 * N_HEADS, tm), lambda b, s: (b, 0, s)),
        ],
        out_shape=[
            jax.ShapeDtypeStruct((b_sz, s_len, n_a), BF16),
            jax.ShapeDtypeStruct((b_sz, 2 * dm, s_len), BF16),
            jax.ShapeDtypeStruct((b_sz, 2 * N_HEADS, s_len), F32),
        ],
        compiler_params=pltpu.CompilerParams(
            dimension_semantics=("arbitrary", "arbitrary"), vmem_limit_bytes=VMEM_LIMIT),
        name="inproj",
    )(x, norm_mix[None, :], cols_a.astype(BF16), bias_a, w_t.astype(BF16), b_t)


def _mixer_kernel(k_ref, o_ref, cb_ref, cc_ref, cx_ref, qt_ref, vt_ref, gt_ref, x_ref,
                  cw_ref, wo_ref, g2_ref, wr_ref, br_ref,
                  x2_ref, hn2_ref, ri_ref, rw_ref, cnt_out_ref,
                  ct_ref, m_ref, carry_ref, cnt_ref, mix_ref):
    ts = x_ref.shape[0]
    n_chunks = ts // CHUNK
    first_tile = pl.program_id(1) == 0

    @pl.when(first_tile)
    def _():
        ct_ref[...] = jnp.zeros_like(ct_ref)
        m_ref[...] = jnp.zeros_like(m_ref)
        carry_ref[...] = jnp.zeros_like(carry_ref)

    @pl.when(jnp.logical_and(first_tile, pl.program_id(0) == 0))
    def _():
        cnt_ref[...] = jnp.zeros_like(cnt_ref)

    gates = gt_ref[...]
    li_all = gates[0:N_HEADS]
    fg = gates[N_HEADS:2 * N_HEADS]
    lf_all = jnp.minimum(fg, 0.0) - jnp.log1p(jnp.exp(-jnp.abs(fg)))
    key_j = lax.broadcasted_iota(I32, (CHUNK, CHUNK), 0)
    qry_i = lax.broadcasted_iota(I32, (CHUNK, CHUNK), 1)
    causal_t = key_j <= qry_i
    neg_inf = jnp.float32(-jnp.inf)
    ones_rows = (lax.broadcasted_iota(I32, (AUG_ROWS, CHUNK), 0) == 0).astype(BF16)

    prefix_ones = (key_j <= qry_i).astype(BF16)

    lf_hi = lf_all.astype(BF16)
    rem = lf_all - lf_hi.astype(F32)
    lf_mid = rem.astype(BF16)
    lf_lo = (rem - lf_mid.astype(F32)).astype(BF16)
    zero_bf = jnp.zeros((N_HEADS, CHUNK), BF16)
    parts = []
    for c in range(n_chunks):
        rows = slice(c * CHUNK, (c + 1) * CHUNK)
        parts += [lf_hi[:, rows], lf_mid[:, rows], lf_lo[:, rows], zero_bf]
    cum = jnp.dot(jnp.concatenate(parts, axis=0), prefix_ones, preferred_element_type=F32)

    m_prev = m_ref[0:N_HEADS, 0:1]
    for c in range(n_chunks):
        rows = slice(c * CHUNK, (c + 1) * CHUNK)
        li = li_all[:, rows]
        r0 = 4 * N_HEADS * c
        b_cum = (cum[r0:r0 + N_HEADS] + cum[r0 + N_HEADS:r0 + 2 * N_HEADS]
                 + cum[r0 + 2 * N_HEADS:r0 + 3 * N_HEADS])
        g_tot = b_cum[:, CHUNK - 1:CHUNK]
        a_end = g_tot - b_cum + li
        m_loc = jnp.max(a_end, axis=1, keepdims=True)
        u = li - b_cum
        m_new = jnp.maximum(g_tot + m_prev, m_loc)
        decay = jnp.exp(g_tot + m_prev - m_new)
        s_fac = jnp.exp(m_loc - m_new)
        w_key = jnp.exp(a_end - m_loc)
        u_cols = jnp.transpose(jnp.concatenate([u, jnp.zeros_like(u)], axis=0))
        for h in range(N_HEADS):
            hc = slice(h * HEAD_DIM, (h + 1) * HEAD_DIM)
            kh = k_ref[rows, hc]
            qth = qt_ref[hc, rows]
            vt_aug = jnp.concatenate([vt_ref[hc, rows], ones_rows], axis=0)
            s_t = jnp.dot(kh, qth, preferred_element_type=F32)
            u_t = jnp.where(causal_t, u_cols[:, h:h + 1], neg_inf)
            mp = m_prev[h:h + 1, :]
            m_tot = jnp.maximum(jnp.max(u_t, axis=0, keepdims=True), mp)
            f_inter = jnp.exp(mp - m_tot)
            floor = jnp.exp(-b_cum[h:h + 1, :] - m_tot)
            p_t = (s_t * jnp.exp(u_t - m_tot)).astype(BF16)
            q_f = (qth.astype(F32) * f_inter).astype(BF16)
            c_old = ct_ref[h]
            lhs = jnp.concatenate([vt_aug, c_old.astype(BF16)], axis=1)
            rhs = jnp.concatenate([p_t, q_f], axis=0)
            r_t = jnp.dot(lhs, rhs, preferred_element_type=F32)
            vtw = (vt_aug.astype(F32) * w_key[h:h + 1, :]).astype(BF16)
            kv = jnp.dot(vtw, kh, preferred_element_type=F32)
            ct_ref[h] = decay[h:h + 1, :] * c_old + s_fac[h:h + 1, :] * kv

            den = r_t[HEAD_DIM:HEAD_DIM + 1, :]
            inv = 1.0 / jnp.maximum(jnp.abs(den), floor)
            h_nat = jnp.transpose(r_t[0:HEAD_DIM, :] * inv)
            o_gate = jax.nn.sigmoid(o_ref[rows, hc].astype(F32))
            mix_ref[rows, hc] = (o_gate * h_nat).astype(BF16)
        m_prev = m_new
    m_ref[0:N_HEADS, :] = jnp.broadcast_to(m_prev, (N_HEADS, LANES))

    u_c = cc_ref[...].astype(F32) * cx_ref[...].astype(F32)
    rid = lax.broadcasted_iota(I32, u_c.shape, 0)
    prev1 = carry_ref[0:1, :]
    prev2 = carry_ref[1:2, :]
    u1 = jnp.where(rid == 0, prev1, pltpu.roll(u_c, 1, axis=0))
    u2 = jnp.where(rid == 0, prev2, jnp.where(rid == 1, prev1, pltpu.roll(u_c, 2, axis=0)))
    z = cw_ref[0:1, :] * u2 + cw_ref[1:2, :] * u1 + cw_ref[2:3, :] * u_c
    mix_ref[:, D_MLSTM:] = (cb_ref[...].astype(F32) * z).astype(BF16)
    carry_ref[0:1, :] = u_c[ts - 1:ts, :]
    carry_ref[1:2, :] = u_c[ts - 2:ts - 1, :]

    d_model = x_ref.shape[1]
    mix = mix_ref[...]
    ssq = jnp.zeros((ts, 1), F32)
    for j in range(d_model // OUT_COLS):
        cs = slice(j * OUT_COLS, (j + 1) * OUT_COLS)
        x2c = x_ref[:, cs] + jnp.dot(mix, wo_ref[:, cs], preferred_element_type=F32)
        x2_ref[:, cs] = x2c
        ssq = ssq + jnp.sum(x2c * x2c, axis=-1, keepdims=True)
    hn2 = x2_ref[...] * lax.rsqrt(ssq * (1.0 / d_model) + RMS_EPS) * g2_ref[...]
    hn2_ref[...] = _pack_rows(hn2)

    logits = jnp.dot(hn2.astype(BF16), wr_ref[...], preferred_element_type=F32) + br_ref[...]
    lt = jnp.transpose(logits)
    gl = lt[0:N_GROUPS]
    gmax = jnp.max(gl, axis=0, keepdims=True)
    gi = lax.broadcasted_iota(I32, gl.shape, 0)
    g_sel = jnp.min(jnp.where(gl == gmax, gi, N_GROUPS), axis=0, keepdims=True)
    p_g = 1.0 / jnp.sum(jnp.exp(gl - gmax), axis=0, keepdims=True)
    in_g = lt[EXPERT_COL0:EXPERT_COL0 + EXPERTS_PER_GROUP]
    for g in range(1, N_GROUPS):
        lo = EXPERT_COL0 + g * EXPERTS_PER_GROUP
        in_g = jnp.where(g_sel == g, lt[lo:lo + EXPERTS_PER_GROUP], in_g)
    ei = lax.broadcasted_iota(I32, in_g.shape, 0)
    v1 = jnp.max(in_g, axis=0, keepdims=True)
    i1 = jnp.min(jnp.where(in_g == v1, ei, EXPERTS_PER_GROUP), axis=0, keepdims=True)
    rest = jnp.where(ei == i1, neg_inf, in_g)
    v2 = jnp.max(rest, axis=0, keepdims=True)
    i2 = jnp.min(jnp.where(rest == v2, ei, EXPERTS_PER_GROUP), axis=0, keepdims=True)
    t = jnp.exp(v2 - v1)
    w1 = p_g / (1.0 + t)
    w2 = p_g * t / (1.0 + t)
    e1 = g_sel * EXPERTS_PER_GROUP + i1
    e2 = g_sel * EXPERTS_PER_GROUP + i2

    eio = lax.broadcasted_iota(I32, (N_EXPERTS, LANES), 0)
    base = cnt_ref[:, 0:1]
    zero_i = jnp.zeros((1, LANES), I32)
    n_blk = ts // LANES
    hits = []
    for k in range(n_blk):
        ls = slice(k * LANES, (k + 1) * LANES)
        hits.append((eio == e1[:, ls], eio == e2[:, ls]))
    onehots = [jnp.where(jnp.logical_or(a1, a2), 1.0, 0.0) for a1, a2 in hits]
    incl_all = jnp.dot(jnp.concatenate(onehots, axis=0).astype(BF16), prefix_ones,
                       preferred_element_type=F32).astype(I32)
    for k in range(n_blk):
        ls = slice(k * LANES, (k + 1) * LANES)
        is1, is2 = hits[k]
        onehot = onehots[k].astype(I32)
        incl = incl_all[k * N_EXPERTS:(k + 1) * N_EXPERTS]
        rank_e = base + incl - onehot
        r1 = jnp.sum(jnp.where(is1, rank_e, 0), axis=0, keepdims=True)
        r2 = jnp.sum(jnp.where(is2, rank_e, 0), axis=0, keepdims=True)
        ri_ref[:, ls] = jnp.concatenate(
            [e1[:, ls], e2[:, ls], r1, r2, zero_i, zero_i, zero_i, zero_i], axis=0)
        base = base + incl[:, LANES - 1:LANES]
    zero_f = jnp.zeros_like(w1)
    rw_ref[...] = jnp.concatenate([w1, w2, zero_f, zero_f, zero_f, zero_f, zero_f, zero_f], axis=0)
    cnt_new = jnp.broadcast_to(base, (N_EXPERTS, LANES))
    cnt_ref[...] = cnt_new
    cnt_out_ref[...] = cnt_new


def _mixer(x, a, qvt, gt, conv_w, w_out, norm_ffn, w_router, b_router):
    b_sz, s_len, d = x.shape
    ts = TS_MIXER
    dm = D_MLSTM

    def a_spec(j):
        return pl.BlockSpec((None, ts, dm), lambda b, s, j=j: (b, s, j))

    tok = lambda w: pl.BlockSpec((None, ts, w), lambda b, s: (b, s, 0))
    rowsp = pl.BlockSpec((None, 2 * N_HEADS, ts), lambda b, s: (b, 0, s))
    const = lambda shape: pl.BlockSpec(shape, lambda b, s: tuple(0 for _ in shape))
    return pl.pallas_call(
        _mixer_kernel,
        grid=(b_sz, s_len // ts),
        in_specs=[a_spec(0), a_spec(1), a_spec(2), a_spec(3), a_spec(4),
                  pl.BlockSpec((None, dm, ts), lambda b, s: (b, 0, s)),
                  pl.BlockSpec((None, dm, ts), lambda b, s: (b, 1, s)),
                  rowsp, tok(d),
                  const((3, D_CONV)), const((d, d)), const((1, d)),
                  const((d, ROUTER_COLS)), const((1, ROUTER_COLS))],
        out_specs=[tok(d), tok(d // 2), rowsp, rowsp, const((N_EXPERTS, LANES))],
        out_shape=[
            jax.ShapeDtypeStruct((b_sz, s_len, d), F32),
            jax.ShapeDtypeStruct((b_sz, s_len, d // 2), I32),
            jax.ShapeDtypeStruct((b_sz, 2 * N_HEADS, s_len), I32),
            jax.ShapeDtypeStruct((b_sz, 2 * N_HEADS, s_len), F32),
            jax.ShapeDtypeStruct((N_EXPERTS, LANES), I32),
        ],
        scratch_shapes=[
            pltpu.VMEM((N_HEADS, HEAD_DIM + AUG_ROWS, HEAD_DIM), F32),
            pltpu.VMEM((2 * N_HEADS, LANES), F32),
            pltpu.VMEM((8, D_CONV), F32),
            pltpu.VMEM((N_EXPERTS, LANES), I32),
            pltpu.VMEM((ts, d), BF16),
        ],
        compiler_params=pltpu.CompilerParams(
            dimension_semantics=("arbitrary", "arbitrary"), vmem_limit_bytes=VMEM_LIMIT),
        name="mixer",
    )(a, a, a, a, a, qvt, qvt, gt, x, conv_w, w_out.astype(BF16), norm_ffn[None, :], w_router.astype(BF16), b_router)


SC_CORES = 2
SC_SUBCORES = 16
SC_WORKERS = SC_CORES * SC_SUBCORES
SC_CHUNK = 64


def _sc_worker_id():
    return lax.axis_index("s") * SC_CORES + lax.axis_index("c")


def _sc_scatter_rows(rows, idx, n_out):
    n_rows, width = rows.shape
    per_worker = n_rows // SC_WORKERS
    n_chunks = per_worker // SC_CHUNK
    assert n_chunks % 2 == 0
    mesh = plsc.VectorSubcoreMesh(core_axis_name="c", subcore_axis_name="s")

    def body(rows_hbm, idx_hbm, out_hbm, ia0, ib0, r0, ia1, ib1, r1, ls0, ls1, ss0, ss1):
        wid = _sc_worker_id()
        bufs = ((ia0, ib0, r0, ls0, ss0), (ia1, ib1, r1, ls1, ss1))

        def loads(j, b):
            ia, ib, r, ls, _ = bufs[b]
            base = pl.multiple_of(wid * per_worker + j * SC_CHUNK, SC_CHUNK)
            return (pltpu.make_async_copy(rows_hbm.at[pl.ds(base, SC_CHUNK)], r, ls),
                    pltpu.make_async_copy(idx_hbm.at[pl.ds(base, SC_CHUNK)], ia, ls),
                    pltpu.make_async_copy(idx_hbm.at[pl.ds(n_rows + base, SC_CHUNK)], ib, ls))

        def scatters(b):
            ia, ib, r, _, ss = bufs[b]
            return (pltpu.make_async_copy(r, out_hbm.at[ia], ss),
                    pltpu.make_async_copy(r, out_hbm.at[ib], ss))

        for c in loads(0, 0):
            c.start()

        @pl.loop(0, n_chunks, step=2)
        def _(j0):
            for b in range(2):
                j = j0 + b
                for c in loads(j, b):
                    c.wait()

                @pl.when(j >= 1)
                def _():
                    for c in scatters(1 - b):
                        c.wait()

                @pl.when(j + 1 < n_chunks)
                def _():
                    for c in loads(j + 1, 1 - b):
                        c.start()

                for c in scatters(b):
                    c.start()

        for c in scatters((n_chunks - 1) % 2):
            c.wait()

    idx_t = pltpu.VMEM((SC_CHUNK,), I32)
    row_t = pltpu.VMEM((SC_CHUNK, width), I32)
    sem_t = pltpu.SemaphoreType.DMA
    return pl.kernel(
        body,
        out_type=jax.ShapeDtypeStruct((n_out, width), I32),
        mesh=mesh,
        scratch_types=[idx_t, idx_t, row_t, idx_t, idx_t, row_t, sem_t, sem_t, sem_t, sem_t],
        name="sc_scatter",
    )(rows, idx)


def _sc_gather_rows(table, idx):
    n_rows = idx.shape[0]
    width = table.shape[1]
    per_worker = n_rows // SC_WORKERS
    n_chunks = per_worker // SC_CHUNK
    assert n_chunks % 2 == 0
    mesh = plsc.VectorSubcoreMesh(core_axis_name="c", subcore_axis_name="s")

    def body(table_hbm, idx_hbm, out_hbm, i0, r0, i1, r1, is0, is1, gs0, gs1, ws0, ws1):
        wid = _sc_worker_id()
        bufs = ((i0, r0, is0, gs0, ws0), (i1, r1, is1, gs1, ws1))

        def chunk_base(j):
            return pl.multiple_of(wid * per_worker + j * SC_CHUNK, SC_CHUNK)

        def idx_load(j, b):
            return pltpu.make_async_copy(idx_hbm.at[pl.ds(chunk_base(j), SC_CHUNK)], bufs[b][0], bufs[b][2])

        def gather(b):
            return pltpu.make_async_copy(table_hbm.at[bufs[b][0]], bufs[b][1], bufs[b][3])

        def writeback(j, b):
            return pltpu.make_async_copy(bufs[b][1], out_hbm.at[pl.ds(chunk_base(j), SC_CHUNK)], bufs[b][4])

        idx_load(0, 0).start()
        idx_load(0, 0).wait()
        gather(0).start()
        idx_load(1, 1).start()

        @pl.loop(0, n_chunks, step=2)
        def _(j0):
            for b in range(2):
                j = j0 + b

                @pl.when(j + 1 < n_chunks)
                def _():
                    idx_load(j + 1, 1 - b).wait()

                    @pl.when(j >= 1)
                    def _():
                        writeback(j - 1, 1 - b).wait()
                    gather(1 - b).start()

                gather(b).wait()
                writeback(j, b).start()

                @pl.when(j + 2 < n_chunks)
                def _():
                    idx_load(j + 2, b).start()

        writeback(n_chunks - 2, 0).wait()
        writeback(n_chunks - 1, 1).wait()

    idx_t = pltpu.VMEM((SC_CHUNK,), I32)
    row_t = pltpu.VMEM((SC_CHUNK, width), I32)
    sem_t = pltpu.SemaphoreType.DMA
    return pl.kernel(
        body,
        out_type=jax.ShapeDtypeStruct((n_rows, width), I32),
        mesh=mesh,
        scratch_types=[idx_t, row_t, idx_t, row_t, sem_t, sem_t, sem_t, sem_t, sem_t, sem_t],
        name="sc_gather",
    )(table, idx)


def _experts_kernel(te_ref, nu_ref, tv_ref, ne_ref, xs_ref, wg_hbm, wu_hbm, wd_hbm, y_ref,
                    wgs_ref, wus_ref, wds_ref, wgb_ref, wub_ref, wdb_ref, wsem):
    i = pl.program_id(0)
    e = te_ref[i]
    e_prev = te_ref[jnp.maximum(i - 1, 0)]
    new_expert = jnp.logical_or(i == 0, e != e_prev)
    used = i < nu_ref[0]

    def weight_copies(expert):
        return (pltpu.make_async_copy(wg_hbm.at[expert], wgs_ref, wsem),
                pltpu.make_async_copy(wu_hbm.at[expert], wus_ref, wsem),
                pltpu.make_async_copy(wd_hbm.at[expert], wds_ref, wsem))

    @pl.when(i == 0)
    def _():
        for c in weight_copies(e):
            c.start()

    @pl.when(jnp.logical_and(used, new_expert))
    def _():
        for c in weight_copies(e):
            c.wait()
        wgb_ref[...] = wgs_ref[...].astype(BF16)
        wub_ref[...] = wus_ref[...].astype(BF16)
        wdb_ref[...] = wds_ref[...].astype(BF16)

        @pl.when(ne_ref[i] >= 0)
        def _():
            for c in weight_copies(ne_ref[i]):
                c.start()

    @pl.when(used)
    def _():
        row = lax.broadcasted_iota(I32, xs_ref.shape, 0)
        x_lo, x_hi = _unpack_rows(jnp.where(row < tv_ref[i], xs_ref[...], 0))
        xb = jnp.concatenate([x_lo.astype(BF16), x_hi.astype(BF16)], axis=1)
        gate = jnp.dot(xb, wgb_ref[...], preferred_element_type=F32)
        up = jnp.dot(xb, wub_ref[...], preferred_element_type=F32)
        hid = (gate * jax.nn.sigmoid(gate) * up).astype(BF16)
        y_ref[...] = _pack_rows(jnp.dot(hid, wdb_ref[...], preferred_element_type=F32))

    @pl.when(jnp.logical_not(used))
    def _():
        y_ref[...] = jnp.zeros_like(y_ref)


def _experts(xs, tile_e, n_used, tile_valid, next_e, w_gate, w_up, w_down):
    cap, dw = xs.shape
    d = 2 * dw
    tm = TM_EXPERT
    n_tiles = cap // tm

    def x_map(i, te, nu, tv, ne):
        return (jnp.minimum(i, jnp.maximum(nu[0] - 1, 0)), 0)

    return pl.pallas_call(
        _experts_kernel,
        grid_spec=pltpu.PrefetchScalarGridSpec(
            num_scalar_prefetch=4,
            grid=(n_tiles,),
            in_specs=[
                pl.BlockSpec((tm, dw), x_map),
                pl.BlockSpec(memory_space=pl.ANY),
                pl.BlockSpec(memory_space=pl.ANY),
                pl.BlockSpec(memory_space=pl.ANY),
            ],
            out_specs=pl.BlockSpec((tm, dw), lambda i, te, nu, tv, ne: (i, 0)),
            scratch_shapes=[pltpu.VMEM((d, D_FF), F32), pltpu.VMEM((d, D_FF), F32), pltpu.VMEM((D_FF, d), F32),
                            pltpu.VMEM((d, D_FF), BF16), pltpu.VMEM((d, D_FF), BF16), pltpu.VMEM((D_FF, d), BF16),
                            pltpu.SemaphoreType.DMA(())],
        ),
        out_shape=jax.ShapeDtypeStruct((cap, dw), I32),
        compiler_params=pltpu.CompilerParams(
            dimension_semantics=("arbitrary",), vmem_limit_bytes=VMEM_LIMIT),
        name="experts",
    )(tile_e, n_used, tile_valid, next_e, xs, w_gate, w_up, w_down)


def _combine_stream_kernel(x2_ref, y1_ref, y2_ref, rw_ref, gf_ref, out_ref):
    w_t = jnp.transpose(rw_ref[...])
    a_lo, a_hi = _unpack_rows(y1_ref[...])
    b_lo, b_hi = _unpack_rows(y2_ref[...])
    w1 = w_t[:, 0:1]
    w2 = w_t[:, 1:2]
    moe = jnp.concatenate([w1 * a_lo + w2 * b_lo, w1 * a_hi + w2 * b_hi], axis=1)
    x3 = x2_ref[...] + moe
    ms = jnp.mean(x3 * x3, axis=-1, keepdims=True)
    out_ref[...] = x3 * lax.rsqrt(ms + RMS_EPS) * gf_ref[...]


def _combine_stream(x2, yg, rw, norm_final):
    b_sz, s_len, d = x2.shape
    tc = TC_COMBINE
    n_s = s_len // tc
    n_steps = b_sz * n_s
    return pl.pallas_call(
        _combine_stream_kernel,
        grid=(n_steps,),
        in_specs=[
            pl.BlockSpec((tc, d), lambda g: (g, 0)),
            pl.BlockSpec((tc, d // 2), lambda g: (g, 0)),
            pl.BlockSpec((tc, d // 2), lambda g: (g + n_steps, 0)),
            pl.BlockSpec((None, 2 * N_HEADS, tc), lambda g: (g // n_s, 0, g % n_s)),
            pl.BlockSpec((1, d), lambda g: (0, 0)),
        ],
        out_specs=pl.BlockSpec((tc, d), lambda g: (g, 0)),
        out_shape=jax.ShapeDtypeStruct((b_sz * s_len, d), F32),
        compiler_params=pltpu.CompilerParams(
            dimension_semantics=("arbitrary",), vmem_limit_bytes=VMEM_LIMIT),
        name="combine_stream",
    )(x2.reshape(b_sz * s_len, d), yg, yg, rw, norm_final[None, :]).reshape(b_sz, s_len, d)


def _layer(x, norm_mix, w_in, b_in, conv_w, w_out, norm_ffn, w_group, b_group, w_expert, b_expert,
           w_gate, w_up, w_down, norm_out):
    b_sz, s_len, d = x.shape
    n_tok = b_sz * s_len
    a, qvt, gt = _inproj(x, norm_mix, w_in, b_in)

    pad_g = EXPERT_COL0 - N_GROUPS
    pad_e = ROUTER_COLS - EXPERT_COL0 - N_EXPERTS
    w_router = jnp.concatenate([w_group, jnp.zeros((d, pad_g), F32), w_expert, jnp.zeros((d, pad_e), F32)], axis=1)
    b_router = jnp.concatenate([b_group, jnp.zeros((pad_g,), F32), b_expert, jnp.zeros((pad_e,), F32)])[None, :]
    x2, hn2, ri, rw, counts = _mixer(x, a, qvt, gt, conv_w, w_out, norm_ffn, w_router, b_router)

    tm = TM_EXPERT
    cnt = counts[:, 0]
    padded = ((cnt + tm - 1) // tm) * tm
    pends = jnp.cumsum(padded).astype(I32)
    seg = jnp.concatenate([jnp.zeros((1,), I32), pends])
    n_tiles = (n_tok * 2) // tm + N_EXPERTS
    cap = n_tiles * tm
    tile_start = jnp.arange(n_tiles, dtype=I32) * tm
    tile_e = jnp.minimum(jnp.sum((tile_start[:, None] >= pends[None, :]).astype(I32), axis=1),
                         N_EXPERTS - 1)
    n_used = pends[-1:] // tm
    e_sel = ri[:, 0:2, :]
    seg_start = jnp.sum(jnp.where(e_sel[..., None] == jnp.arange(N_EXPERTS, dtype=I32), seg[:-1], 0), axis=-1)
    pos = ri[:, 2:4, :] + seg_start

    seg_tok_end = seg[:-1] + cnt.astype(I32)
    tile_tok_end = jnp.sum(jnp.where(tile_e[:, None] == jnp.arange(N_EXPERTS, dtype=I32), seg_tok_end, 0), axis=-1)
    tile_valid = jnp.clip(tile_tok_end - tile_start, 0, tm).astype(I32)
    pos_flat = jnp.concatenate([pos[:, 0, :].reshape(-1), pos[:, 1, :].reshape(-1)])

    expert_ids = jnp.arange(N_EXPERTS, dtype=I32)
    seg_end_tile = jnp.sum(jnp.where(tile_e[:, None] == expert_ids, pends // tm, 0), axis=-1)
    tile_ids = jnp.arange(n_tiles, dtype=I32)
    e_after = jnp.sum(jnp.where(seg_end_tile[:, None] == tile_ids, tile_e, 0), axis=-1)
    next_e = jnp.where(seg_end_tile < n_used[0], e_after, -1).astype(I32)

    xs = _sc_scatter_rows(hn2.reshape(n_tok, d // 2), pos_flat, cap)
    y = _experts(xs, tile_e, n_used, tile_valid, next_e, w_gate, w_up, w_down)
    yg = _sc_gather_rows(y, pos_flat)
    return _combine_stream(x2, yg, rw, norm_out)


def kernel(x, norm_mix, w_in, b_in, conv_w, w_out, norm_ffn, w_group, b_group, w_expert, b_expert,
           w_gate, w_up, w_down, norm_final):
    depth = norm_mix.shape[0]
    assert depth == 1, "final-norm fusion below assumes a single layer"
    assert x.shape[-1] == 2 * D_MLSTM
    assert all(x.shape[1] % t == 0 for t in (TS_MIXER, TM_INPROJ, TC_COMBINE))
    return _layer(x, norm_mix[0], w_in[0], b_in[0], conv_w[0], w_out[0], norm_ffn[0],
                  w_group[0], b_group[0], w_expert[0], b_expert[0],
                  w_gate[0], w_up[0], w_down[0], norm_final)
```

```python
import jax
import jax.numpy as jnp
from jax import lax
from jax.experimental import pallas as pl
from jax.experimental.pallas import tpu as pltpu
from jax.experimental.pallas import tpu_sc as plsc

F32 = jnp.float32
BF16 = jnp.bfloat16
I32 = jnp.int32

N_HEADS = 4
HEAD_DIM = 128
D_MLSTM = N_HEADS * HEAD_DIM
D_CONV = 512
N_GROUPS = 4
EXPERTS_PER_GROUP = 8
N_EXPERTS = N_GROUPS * EXPERTS_PER_GROUP
D_FF = 512
RMS_EPS = 1e-6
Q_SCALE = HEAD_DIM ** -0.5

LANES = 128
CHUNK = LANES
AUG_ROWS = 16
ROUTER_COLS = 128
EXPERT_COL0 = 8

TM_INPROJ = 1024
TS_MIXER = 1024
OUT_COLS = 256
TM_EXPERT = 512
TC_COMBINE = 1024

VMEM_LIMIT = 56 * 1024 * 1024


HI_HALF = -65536


def _pack_rows(x):
    w = x.shape[1] // 2
    lo = lax.shift_right_logical(lax.bitcast_convert_type(x[:, :w].astype(BF16).astype(F32), I32), 16)
    hi = lax.bitcast_convert_type(x[:, w:].astype(BF16).astype(F32), I32) & HI_HALF
    return lo | hi


def _unpack_rows(words):
    lo = lax.bitcast_convert_type(words << 16, F32)
    hi = lax.bitcast_convert_type(words & HI_HALF, F32)
    return lo, hi


def _nt_dot(a, b):
    return lax.dot_general(a, b, (((1,), (1,)), ((), ())), preferred_element_type=F32)


def _inproj_kernel(x_ref, g_ref, wa_ref, ba_ref, wt_ref, bt_ref, a_ref, qvt_ref, gt_ref):
    x = x_ref[...]
    ms = jnp.mean(x * x, axis=-1, keepdims=True)
    hn = (x * lax.rsqrt(ms + RMS_EPS) * g_ref[...]).astype(BF16)
    n_blk = wa_ref.shape[1] // D_MLSTM
    for j in range(n_blk):
        sl = slice(j * D_MLSTM, (j + 1) * D_MLSTM)
        acc = jnp.dot(hn, wa_ref[:, sl], preferred_element_type=F32) + ba_ref[:, sl]
        a_ref[:, sl] = acc.astype(BF16)
    tr = _nt_dot(wt_ref[...], hn) + bt_ref[...]
    q_rows = slice(0, D_MLSTM)
    v_rows = slice(D_MLSTM, 2 * D_MLSTM)
    qvt_ref[q_rows, :] = (tr[q_rows, :] * Q_SCALE).astype(BF16)
    qvt_ref[v_rows, :] = tr[v_rows, :].astype(BF16)
    gt_ref[...] = tr[2 * D_MLSTM:2 * D_MLSTM + 2 * N_HEADS, :]


def _inproj(x, norm_mix, w_in, b_in):
    b_sz, s_len, d = x.shape
    tm = TM_INPROJ
    dm = D_MLSTM
    g0 = 4 * dm
    c0 = g0 + 2 * N_HEADS
    cols_a = jnp.concatenate([w_in[:, dm:2 * dm], w_in[:, 3 * dm:4 * dm], w_in[:, c0:c0 + 3 * D_CONV]], axis=1)
    bias_a = jnp.concatenate([b_in[dm:2 * dm], b_in[3 * dm:4 * dm], b_in[c0:c0 + 3 * D_CONV]])[None, :]
    n_gate = 2 * N_HEADS
    pad_rows = AUG_ROWS - n_gate
    w_t = jnp.concatenate([w_in[:, 0:dm], w_in[:, 2 * dm:3 * dm], w_in[:, g0:c0],
                           jnp.zeros((d, pad_rows), w_in.dtype)], axis=1).T
    b_t = jnp.concatenate([b_in[0:dm], b_in[2 * dm:3 * dm], b_in[g0:c0],
                           jnp.zeros((pad_rows,), b_in.dtype)])[:, None]
    n_t = 2 * dm + AUG_ROWS
    n_a = cols_a.shape[1]
    return pl.pallas_call(
        _inproj_kernel,
        grid=(b_sz, s_len // tm),
        in_specs=[
            pl.BlockSpec((None, tm, d), lambda b, s: (b, s, 0)),
            pl.BlockSpec((1, d), lambda b, s: (0, 0)),
            pl.BlockSpec((d, n_a), lambda b, s: (0, 0)),
            pl.BlockSpec((1, n_a), lambda b, s: (0, 0)),
            pl.BlockSpec((n_t, d), lambda b, s: (0, 0)),
            pl.BlockSpec((n_t, 1), lambda b, s: (0, 0)),
        ],
        out_specs=[
            pl.BlockSpec((None, tm, n_a), lambda b, s: (b, s, 0)),
            pl.BlockSpec((None, 2 * dm, tm), lambda b, s: (b, 0, s)),
            pl.BlockSpec((None, 2 * N_HEADS, tm), lambda b, s: (b, 0, s)),
        ],
        out_shape=[
            jax.ShapeDtypeStruct((b_sz, s_len, n_a), BF16),
            jax.ShapeDtypeStruct((b_sz, 2 * dm, s_len), BF16),
            jax.ShapeDtypeStruct((b_sz, 2 * N_HEADS, s_len), F32),
        ],
        compiler_params=pltpu.CompilerParams(
            dimension_semantics=("arbitrary", "arbitrary"), vmem_limit_bytes=VMEM_LIMIT),
        name="inproj",
    )(x, norm_mix[None, :], cols_a.astype(BF16), bias_a, w_t.astype(BF16), b_t)


def _mixer_kernel(k_ref, o_ref, cb_ref, cc_ref, cx_ref, qt_ref, vt_ref, gt_ref, x_ref,
                  cw_ref, wo_ref, g2_ref, wr_ref, br_ref,
                  x2_ref, hn2_ref, ri_ref, rw_ref, cnt_out_ref,
                  ct_ref, m_ref, carry_ref, cnt_ref, mix_ref):
    ts = x_ref.shape[0]
    n_chunks = ts // CHUNK
    first_tile = pl.program_id(1) == 0

    @pl.when(first_tile)
    def _():
        ct_ref[...] = jnp.zeros_like(ct_ref)
        m_ref[...] = jnp.zeros_like(m_ref)
        carry_ref[...] = jnp.zeros_like(carry_ref)

    @pl.when(jnp.logical_and(first_tile, pl.program_id(0) == 0))
    def _():
        cnt_ref[...] = jnp.zeros_like(cnt_ref)

    gates = gt_ref[...]
    li_all = gates[0:N_HEADS]
    fg = gates[N_HEADS:2 * N_HEADS]
    lf_all = jnp.minimum(fg, 0.0) - jnp.log1p(jnp.exp(-jnp.abs(fg)))
    key_j = lax.broadcasted_iota(I32, (CHUNK, CHUNK), 0)
    qry_i = lax.broadcasted_iota(I32, (CHUNK, CHUNK), 1)
    causal_t = key_j <= qry_i
    neg_inf = jnp.float32(-jnp.inf)
    ones_rows = (lax.broadcasted_iota(I32, (AUG_ROWS, CHUNK), 0) == 0).astype(BF16)

    prefix_ones = (key_j <= qry_i).astype(BF16)

    lf_hi = lf_all.astype(BF16)
    rem = lf_all - lf_hi.astype(F32)
    lf_mid = rem.astype(BF16)
    lf_lo = (rem - lf_mid.astype(F32)).astype(BF16)
    zero_bf = jnp.zeros((N_HEADS, CHUNK), BF16)
    parts = []
    for c in range(n_chunks):
        rows = slice(c * CHUNK, (c + 1) * CHUNK)
        parts += [lf_hi[:, rows], lf_mid[:, rows], lf_lo[:, rows], zero_bf]
    cum = jnp.dot(jnp.concatenate(parts, axis=0), prefix_ones, preferred_element_type=F32)

    m_prev = m_ref[0:N_HEADS, 0:1]
    for c in range(n_chunks):
        rows = slice(c * CHUNK, (c + 1) * CHUNK)
        li = li_all[:, rows]
        r0 = 4 * N_HEADS * c
        b_cum = (cum[r0:r0 + N_HEADS] + cum[r0 + N_HEADS:r0 + 2 * N_HEADS]
                 + cum[r0 + 2 * N_HEADS:r0 + 3 * N_HEADS])
        g_tot = b_cum[:, CHUNK - 1:CHUNK]
        a_end = g_tot - b_cum + li
        m_loc = jnp.max(a_end, axis=1, keepdims=True)
        u = li - b_cum
        m_new = jnp.maximum(g_tot + m_prev, m_loc)
        decay = jnp.exp(g_tot + m_prev - m_new)
        s_fac = jnp.exp(m_loc - m_new)
        w_key = jnp.exp(a_end - m_loc)
        u_cols = jnp.transpose(jnp.concatenate([u, jnp.zeros_like(u)], axis=0))

        def pair_row(x, h0):
            parts = [jnp.broadcast_to(x[h:h + 1, :], (1, CHUNK)) for h in (h0, h0 + 1)]
            return jnp.concatenate(parts, axis=1)

        def block_diag(a, b):
            z = jnp.zeros_like(a)
            return jnp.concatenate([jnp.concatenate([a, z], axis=1), jnp.concatenate([z, b], axis=1)], axis=0)

        for p in range(N_HEADS // 2):
            h0 = 2 * p
            pc = slice(h0 * HEAD_DIM, (h0 + 2) * HEAD_DIM)
            lo_l, hi_l = slice(0, CHUNK), slice(CHUNK, 2 * CHUNK)
            k2 = k_ref[rows, pc]
            q2 = qt_ref[pc, rows]
            v2 = vt_ref[pc, rows]
            s2 = jnp.dot(k2, block_diag(q2[0:HEAD_DIM], q2[HEAD_DIM:]), preferred_element_type=F32)
            u_t = jnp.concatenate([jnp.where(causal_t, u_cols[:, h:h + 1], neg_inf) for h in (h0, h0 + 1)],
                                  axis=1)
            mp = pair_row(m_prev, h0)
            m_tot = jnp.maximum(jnp.max(u_t, axis=0, keepdims=True), mp)
            f_inter = jnp.exp(mp - m_tot)
            floor = jnp.exp(-pair_row(b_cum, h0) - m_tot)
            p2 = (s2 * jnp.exp(u_t - m_tot)).astype(BF16)
            qf = [(q2[r].astype(F32) * f_inter[:, l]).astype(BF16)
                  for r, l in ((slice(0, HEAD_DIM), lo_l), (slice(HEAD_DIM, None), hi_l))]
            c_old = ct_ref[p]
            cb16 = c_old.astype(BF16)
            vt_aug = [jnp.concatenate([v2[r], ones_rows], axis=0)
                      for r in (slice(0, HEAD_DIM), slice(HEAD_DIM, None))]
            lhs = jnp.concatenate([vt_aug[0], cb16[:, lo_l], vt_aug[1], cb16[:, hi_l]], axis=1)
            rhs = block_diag(jnp.concatenate([p2[:, lo_l], qf[0]], axis=0),
                             jnp.concatenate([p2[:, hi_l], qf[1]], axis=0))
            r2 = jnp.dot(lhs, rhs, preferred_element_type=F32)
            w2 = pair_row(w_key, h0)
            vtw = jnp.concatenate([(vt_aug[0].astype(F32) * w2[:, lo_l]).astype(BF16),
                                   (vt_aug[1].astype(F32) * w2[:, hi_l]).astype(BF16)], axis=1)
            kv = jnp.dot(vtw, block_diag(k2[:, lo_l], k2[:, hi_l]), preferred_element_type=F32)
            ct_ref[p] = pair_row(decay, h0) * c_old + pair_row(s_fac, h0) * kv

            den = r2[HEAD_DIM:HEAD_DIM + 1, :]
            inv = 1.0 / jnp.maximum(jnp.abs(den), floor)
            h_t = r2[0:HEAD_DIM, :] * inv
            for h, l in ((h0, lo_l), (h0 + 1, hi_l)):
                hc = slice(h * HEAD_DIM, (h + 1) * HEAD_DIM)
                o_gate = jax.nn.sigmoid(o_ref[rows, hc].astype(F32))
                mix_ref[rows, hc] = (o_gate * jnp.transpose(h_t[:, l])).astype(BF16)
        m_prev = m_new
    m_ref[0:N_HEADS, :] = jnp.broadcast_to(m_prev, (N_HEADS, LANES))

    u_c = cc_ref[...].astype(F32) * cx_ref[...].astype(F32)
    rid = lax.broadcasted_iota(I32, u_c.shape, 0)
    prev1 = carry_ref[0:1, :]
    prev2 = carry_ref[1:2, :]
    u1 = jnp.where(rid == 0, prev1, pltpu.roll(u_c, 1, axis=0))
    u2 = jnp.where(rid == 0, prev2, jnp.where(rid == 1, prev1, pltpu.roll(u_c, 2, axis=0)))
    z = cw_ref[0:1, :] * u2 + cw_ref[1:2, :] * u1 + cw_ref[2:3, :] * u_c
    mix_ref[:, D_MLSTM:] = (cb_ref[...].astype(F32) * z).astype(BF16)
    carry_ref[0:1, :] = u_c[ts - 1:ts, :]
    carry_ref[1:2, :] = u_c[ts - 2:ts - 1, :]

    d_model = x_ref.shape[1]
    mix = mix_ref[...]
    ssq = jnp.zeros((ts, 1), F32)
    for j in range(d_model // OUT_COLS):
        cs = slice(j * OUT_COLS, (j + 1) * OUT_COLS)
        x2c = x_ref[:, cs] + jnp.dot(mix, wo_ref[:, cs], preferred_element_type=F32)
        x2_ref[:, cs] = x2c
        ssq = ssq + jnp.sum(x2c * x2c, axis=-1, keepdims=True)
    hn2 = x2_ref[...] * lax.rsqrt(ssq * (1.0 / d_model) + RMS_EPS) * g2_ref[...]
    hn2_ref[...] = _pack_rows(hn2)

    logits = jnp.dot(hn2.astype(BF16), wr_ref[...], preferred_element_type=F32) + br_ref[...]
    lt = jnp.transpose(logits)
    gl = lt[0:N_GROUPS]
    gmax = jnp.max(gl, axis=0, keepdims=True)
    gi = lax.broadcasted_iota(I32, gl.shape, 0)
    g_sel = jnp.min(jnp.where(gl == gmax, gi, N_GROUPS), axis=0, keepdims=True)
    p_g = 1.0 / jnp.sum(jnp.exp(gl - gmax), axis=0, keepdims=True)
    in_g = lt[EXPERT_COL0:EXPERT_COL0 + EXPERTS_PER_GROUP]
    for g in range(1, N_GROUPS):
        lo = EXPERT_COL0 + g * EXPERTS_PER_GROUP
        in_g = jnp.where(g_sel == g, lt[lo:lo + EXPERTS_PER_GROUP], in_g)
    ei = lax.broadcasted_iota(I32, in_g.shape, 0)
    v1 = jnp.max(in_g, axis=0, keepdims=True)
    i1 = jnp.min(jnp.where(in_g == v1, ei, EXPERTS_PER_GROUP), axis=0, keepdims=True)
    rest = jnp.where(ei == i1, neg_inf, in_g)
    v2 = jnp.max(rest, axis=0, keepdims=True)
    i2 = jnp.min(jnp.where(rest == v2, ei, EXPERTS_PER_GROUP), axis=0, keepdims=True)
    t = jnp.exp(v2 - v1)
    w1 = p_g / (1.0 + t)
    w2 = p_g * t / (1.0 + t)
    e1 = g_sel * EXPERTS_PER_GROUP + i1
    e2 = g_sel * EXPERTS_PER_GROUP + i2

    eio = lax.broadcasted_iota(I32, (N_EXPERTS, LANES), 0)
    base = cnt_ref[:, 0:1]
    zero_i = jnp.zeros((1, LANES), I32)
    n_blk = ts // LANES
    hits = []
    for k in range(n_blk):
        ls = slice(k * LANES, (k + 1) * LANES)
        hits.append((eio == e1[:, ls], eio == e2[:, ls]))
    onehots = [jnp.where(jnp.logical_or(a1, a2), 1.0, 0.0) for a1, a2 in hits]
    incl_all = jnp.dot(jnp.concatenate(onehots, axis=0).astype(BF16), prefix_ones,
                       preferred_element_type=F32).astype(I32)
    for k in range(n_blk):
        ls = slice(k * LANES, (k + 1) * LANES)
        is1, is2 = hits[k]
        onehot = onehots[k].astype(I32)
        incl = incl_all[k * N_EXPERTS:(k + 1) * N_EXPERTS]
        rank_e = base + incl - onehot
        r1 = jnp.sum(jnp.where(is1, rank_e, 0), axis=0, keepdims=True)
        r2 = jnp.sum(jnp.where(is2, rank_e, 0), axis=0, keepdims=True)
        ri_ref[:, ls] = jnp.concatenate(
            [e1[:, ls], e2[:, ls], r1, r2, zero_i, zero_i, zero_i, zero_i], axis=0)
        base = base + incl[:, LANES - 1:LANES]
    zero_f = jnp.zeros_like(w1)
    rw_ref[...] = jnp.concatenate([w1, w2, zero_f, zero_f, zero_f, zero_f, zero_f, zero_f], axis=0)
    cnt_new = jnp.broadcast_to(base, (N_EXPERTS, LANES))
    cnt_ref[...] = cnt_new
    cnt_out_ref[...] = cnt_new


def _mixer(x, a, qvt, gt, conv_w, w_out, norm_ffn, w_router, b_router):
    b_sz, s_len, d = x.shape
    ts = TS_MIXER
    dm = D_MLSTM

    def a_spec(j):
        return pl.BlockSpec((None, ts, dm), lambda b, s, j=j: (b, s, j))

    tok = lambda w: pl.BlockSpec((None, ts, w), lambda b, s: (b, s, 0))
    rowsp = pl.BlockSpec((None, 2 * N_HEADS, ts), lambda b, s: (b, 0, s))
    const = lambda shape: pl.BlockSpec(shape, lambda b, s: tuple(0 for _ in shape))
    return pl.pallas_call(
        _mixer_kernel,
        grid=(b_sz, s_len // ts),
        in_specs=[a_spec(0), a_spec(1), a_spec(2), a_spec(3), a_spec(4),
                  pl.BlockSpec((None, dm, ts), lambda b, s: (b, 0, s)),
                  pl.BlockSpec((None, dm, ts), lambda b, s: (b, 1, s)),
                  rowsp, tok(d),
                  const((3, D_CONV)), const((d, d)), const((1, d)),
                  const((d, ROUTER_COLS)), const((1, ROUTER_COLS))],
        out_specs=[tok(d), tok(d // 2), rowsp, rowsp, const((N_EXPERTS, LANES))],
        out_shape=[
            jax.ShapeDtypeStruct((b_sz, s_len, d), F32),
            jax.ShapeDtypeStruct((b_sz, s_len, d // 2), I32),
            jax.ShapeDtypeStruct((b_sz, 2 * N_HEADS, s_len), I32),
            jax.ShapeDtypeStruct((b_sz, 2 * N_HEADS, s_len), F32),
            jax.ShapeDtypeStruct((N_EXPERTS, LANES), I32),
        ],
        scratch_shapes=[
            pltpu.VMEM((N_HEADS // 2, HEAD_DIM + AUG_ROWS, 2 * HEAD_DIM), F32),
            pltpu.VMEM((2 * N_HEADS, LANES), F32),
            pltpu.VMEM((8, D_CONV), F32),
            pltpu.VMEM((N_EXPERTS, LANES), I32),
            pltpu.VMEM((ts, d), BF16),
        ],
        compiler_params=pltpu.CompilerParams(
            dimension_semantics=("arbitrary", "arbitrary"), vmem_limit_bytes=VMEM_LIMIT),
        name="mixer",
    )(a, a, a, a, a, qvt, qvt, gt, x, conv_w, w_out.astype(BF16), norm_ffn[None, :], w_router.astype(BF16), b_router)


SC_CORES = 2
SC_SUBCORES = 16
SC_WORKERS = SC_CORES * SC_SUBCORES
SC_CHUNK = 64


def _sc_worker_id():
    return lax.axis_index("s") * SC_CORES + lax.axis_index("c")


def _sc_scatter_rows(rows, idx, n_out):
    n_rows, width = rows.shape
    per_worker = n_rows // SC_WORKERS
    n_chunks = per_worker // SC_CHUNK
    assert n_chunks % 2 == 0
    mesh = plsc.VectorSubcoreMesh(core_axis_name="c", subcore_axis_name="s")

    def body(rows_hbm, idx_hbm, out_hbm, ia0, ib0, r0, ia1, ib1, r1, ls0, ls1, ss0, ss1):
        wid = _sc_worker_id()
        bufs = ((ia0, ib0, r0, ls0, ss0), (ia1, ib1, r1, ls1, ss1))

        def loads(j, b):
            ia, ib, r, ls, _ = bufs[b]
            base = pl.multiple_of(wid * per_worker + j * SC_CHUNK, SC_CHUNK)
            return (pltpu.make_async_copy(rows_hbm.at[pl.ds(base, SC_CHUNK)], r, ls),
                    pltpu.make_async_copy(idx_hbm.at[pl.ds(base, SC_CHUNK)], ia, ls),
                    pltpu.make_async_copy(idx_hbm.at[pl.ds(n_rows + base, SC_CHUNK)], ib, ls))

        def scatters(b):
            ia, ib, r, _, ss = bufs[b]
            return (pltpu.make_async_copy(r, out_hbm.at[ia], ss),
                    pltpu.make_async_copy(r, out_hbm.at[ib], ss))

        for c in loads(0, 0):
            c.start()

        @pl.loop(0, n_chunks, step=2)
        def _(j0):
            for b in range(2):
                j = j0 + b
                for c in loads(j, b):
                    c.wait()

                @pl.when(j >= 1)
                def _():
                    for c in scatters(1 - b):
                        c.wait()

                @pl.when(j + 1 < n_chunks)
                def _():
                    for c in loads(j + 1, 1 - b):
                        c.start()

                for c in scatters(b):
                    c.start()

        for c in scatters((n_chunks - 1) % 2):
            c.wait()

    idx_t = pltpu.VMEM((SC_CHUNK,), I32)
    row_t = pltpu.VMEM((SC_CHUNK, width), I32)
    sem_t = pltpu.SemaphoreType.DMA
    return pl.kernel(
        body,
        out_type=jax.ShapeDtypeStruct((n_out, width), I32),
        mesh=mesh,
        scratch_types=[idx_t, idx_t, row_t, idx_t, idx_t, row_t, sem_t, sem_t, sem_t, sem_t],
        name="sc_scatter",
    )(rows, idx)


def _sc_gather_rows(table, idx):
    n_rows = idx.shape[0]
    width = table.shape[1]
    per_worker = n_rows // SC_WORKERS
    n_chunks = per_worker // SC_CHUNK
    assert n_chunks % 2 == 0
    mesh = plsc.VectorSubcoreMesh(core_axis_name="c", subcore_axis_name="s")

    def body(table_hbm, idx_hbm, out_hbm, i0, r0, i1, r1, is0, is1, gs0, gs1, ws0, ws1):
        wid = _sc_worker_id()
        bufs = ((i0, r0, is0, gs0, ws0), (i1, r1, is1, gs1, ws1))

        def chunk_base(j):
            return pl.multiple_of(wid * per_worker + j * SC_CHUNK, SC_CHUNK)

        def idx_load(j, b):
            return pltpu.make_async_copy(idx_hbm.at[pl.ds(chunk_base(j), SC_CHUNK)], bufs[b][0], bufs[b][2])

        def gather(b):
            return pltpu.make_async_copy(table_hbm.at[bufs[b][0]], bufs[b][1], bufs[b][3])

        def writeback(j, b):
            return pltpu.make_async_copy(bufs[b][1], out_hbm.at[pl.ds(chunk_base(j), SC_CHUNK)], bufs[b][4])

        idx_load(0, 0).start()
        idx_load(0, 0).wait()
        gather(0).start()
        idx_load(1, 1).start()

        @pl.loop(0, n_chunks, step=2)
        def _(j0):
            for b in range(2):
                j = j0 + b

                @pl.when(j + 1 < n_chunks)
                def _():
                    idx_load(j + 1, 1 - b).wait()

                    @pl.when(j >= 1)
                    def _():
                        writeback(j - 1, 1 - b).wait()
                    gather(1 - b).start()

                gather(b).wait()
                writeback(j, b).start()

                @pl.when(j + 2 < n_chunks)
                def _():
                    idx_load(j + 2, b).start()

        writeback(n_chunks - 2, 0).wait()
        writeback(n_chunks - 1, 1).wait()

    idx_t = pltpu.VMEM((SC_CHUNK,), I32)
    row_t = pltpu.VMEM((SC_CHUNK, width), I32)
    sem_t = pltpu.SemaphoreType.DMA
    return pl.kernel(
        body,
        out_type=jax.ShapeDtypeStruct((n_rows, width), I32),
        mesh=mesh,
        scratch_types=[idx_t, row_t, idx_t, row_t, sem_t, sem_t, sem_t, sem_t, sem_t, sem_t],
        name="sc_gather",
    )(table, idx)


def _experts_kernel(te_ref, nu_ref, tv_ref, ne_ref, xs_ref, wg_hbm, wu_hbm, wd_hbm, y_ref,
                    wgs_ref, wus_ref, wds_ref, wgb_ref, wub_ref, wdb_ref, wsem):
    i = pl.program_id(0)
    e = te_ref[i]
    e_prev = te_ref[jnp.maximum(i - 1, 0)]
    new_expert = jnp.logical_or(i == 0, e != e_prev)
    used = i < nu_ref[0]

    def weight_copies(expert):
        return (pltpu.make_async_copy(wg_hbm.at[expert], wgs_ref, wsem),
                pltpu.make_async_copy(wu_hbm.at[expert], wus_ref, wsem),
                pltpu.make_async_copy(wd_hbm.at[expert], wds_ref, wsem))

    @pl.when(i == 0)
    def _():
        for c in weight_copies(e):
            c.start()

    @pl.when(jnp.logical_and(used, new_expert))
    def _():
        for c in weight_copies(e):
            c.wait()
        wgb_ref[...] = wgs_ref[...].astype(BF16)
        wub_ref[...] = wus_ref[...].astype(BF16)
        wdb_ref[...] = wds_ref[...].astype(BF16)

        @pl.when(ne_ref[i] >= 0)
        def _():
            for c in weight_copies(ne_ref[i]):
                c.start()

    @pl.when(used)
    def _():
        row = lax.broadcasted_iota(I32, xs_ref.shape, 0)
        x_lo, x_hi = _unpack_rows(jnp.where(row < tv_ref[i], xs_ref[...], 0))
        xb = jnp.concatenate([x_lo.astype(BF16), x_hi.astype(BF16)], axis=1)
        gate = jnp.dot(xb, wgb_ref[...], preferred_element_type=F32)
        up = jnp.dot(xb, wub_ref[...], preferred_element_type=F32)
        hid = (gate * jax.nn.sigmoid(gate) * up).astype(BF16)
        y_ref[...] = _pack_rows(jnp.dot(hid, wdb_ref[...], preferred_element_type=F32))

    @pl.when(jnp.logical_not(used))
    def _():
        y_ref[...] = jnp.zeros_like(y_ref)


def _experts(xs, tile_e, n_used, tile_valid, next_e, w_gate, w_up, w_down):
    cap, dw = xs.shape
    d = 2 * dw
    tm = TM_EXPERT
    n_tiles = cap // tm

    def x_map(i, te, nu, tv, ne):
        return (jnp.minimum(i, jnp.maximum(nu[0] - 1, 0)), 0)

    return pl.pallas_call(
        _experts_kernel,
        grid_spec=pltpu.PrefetchScalarGridSpec(
            num_scalar_prefetch=4,
            grid=(n_tiles,),
            in_specs=[
                pl.BlockSpec((tm, dw), x_map),
                pl.BlockSpec(memory_space=pl.ANY),
                pl.BlockSpec(memory_space=pl.ANY),
                pl.BlockSpec(memory_space=pl.ANY),
            ],
            out_specs=pl.BlockSpec((tm, dw), lambda i, te, nu, tv, ne: (i, 0)),
            scratch_shapes=[pltpu.VMEM((d, D_FF), F32), pltpu.VMEM((d, D_FF), F32), pltpu.VMEM((D_FF, d), F32),
                            pltpu.VMEM((d, D_FF), BF16), pltpu.VMEM((d, D_FF), BF16), pltpu.VMEM((D_FF, d), BF16),
                            pltpu.SemaphoreType.DMA(())],
        ),
        out_shape=jax.ShapeDtypeStruct((cap, dw), I32),
        compiler_params=pltpu.CompilerParams(
            dimension_semantics=("arbitrary",), vmem_limit_bytes=VMEM_LIMIT),
        name="experts",
    )(tile_e, n_used, tile_valid, next_e, xs, w_gate, w_up, w_down)


def _combine_stream_kernel(x2_ref, y1_ref, y2_ref, rw_ref, gf_ref, out_ref):
    w_t = jnp.transpose(rw_ref[...])
    a_lo, a_hi = _unpack_rows(y1_ref[...])
    b_lo, b_hi = _unpack_rows(y2_ref[...])
    w1 = w_t[:, 0:1]
    w2 = w_t[:, 1:2]
    moe = jnp.concatenate([w1 * a_lo + w2 * b_lo, w1 * a_hi + w2 * b_hi], axis=1)
    x3 = x2_ref[...] + moe
    ms = jnp.mean(x3 * x3, axis=-1, keepdims=True)
    out_ref[...] = x3 * lax.rsqrt(ms + RMS_EPS) * gf_ref[...]


def _combine_stream(x2, yg, rw, norm_final):
    b_sz, s_len, d = x2.shape
    tc = TC_COMBINE
    n_s = s_len // tc
    n_steps = b_sz * n_s
    return pl.pallas_call(
        _combine_stream_kernel,
        grid=(n_steps,),
        in_specs=[
            pl.BlockSpec((tc, d), lambda g: (g, 0)),
            pl.BlockSpec((tc, d // 2), lambda g: (g, 0)),
            pl.BlockSpec((tc, d // 2), lambda g: (g + n_steps, 0)),
            pl.BlockSpec((None, 2 * N_HEADS, tc), lambda g: (g // n_s, 0, g % n_s)),
            pl.BlockSpec((1, d), lambda g: (0, 0)),
        ],
        out_specs=pl.BlockSpec((tc, d), lambda g: (g, 0)),
        out_shape=jax.ShapeDtypeStruct((b_sz * s_len, d), F32),
        compiler_params=pltpu.CompilerParams(
            dimension_semantics=("arbitrary",), vmem_limit_bytes=VMEM_LIMIT),
        name="combine_stream",
    )(x2.reshape(b_sz * s_len, d), yg, yg, rw, norm_final[None, :]).reshape(b_sz, s_len, d)


def _layer(x, norm_mix, w_in, b_in, conv_w, w_out, norm_ffn, w_group, b_group, w_expert, b_expert,
           w_gate, w_up, w_down, norm_out):
    b_sz, s_len, d = x.shape
    n_tok = b_sz * s_len
    a, qvt, gt = _inproj(x, norm_mix, w_in, b_in)

    pad_g = EXPERT_COL0 - N_GROUPS
    pad_e = ROUTER_COLS - EXPERT_COL0 - N_EXPERTS
    w_router = jnp.concatenate([w_group, jnp.zeros((d, pad_g), F32), w_expert, jnp.zeros((d, pad_e), F32)], axis=1)
    b_router = jnp.concatenate([b_group, jnp.zeros((pad_g,), F32), b_expert, jnp.zeros((pad_e,), F32)])[None, :]
    x2, hn2, ri, rw, counts = _mixer(x, a, qvt, gt, conv_w, w_out, norm_ffn, w_router, b_router)

    tm = TM_EXPERT
    cnt = counts[:, 0]
    padded = ((cnt + tm - 1) // tm) * tm
    pends = jnp.cumsum(padded).astype(I32)
    seg = jnp.concatenate([jnp.zeros((1,), I32), pends])
    n_tiles = (n_tok * 2) // tm + N_EXPERTS
    cap = n_tiles * tm
    tile_start = jnp.arange(n_tiles, dtype=I32) * tm
    tile_e = jnp.minimum(jnp.sum((tile_start[:, None] >= pends[None, :]).astype(I32), axis=1),
                         N_EXPERTS - 1)
    n_used = pends[-1:] // tm
    e_sel = ri[:, 0:2, :]
    seg_start = jnp.sum(jnp.where(e_sel[..., None] == jnp.arange(N_EXPERTS, dtype=I32), seg[:-1], 0), axis=-1)
    pos = ri[:, 2:4, :] + seg_start

    seg_tok_end = seg[:-1] + cnt.astype(I32)
    tile_tok_end = jnp.sum(jnp.where(tile_e[:, None] == jnp.arange(N_EXPERTS, dtype=I32), seg_tok_end, 0), axis=-1)
    tile_valid = jnp.clip(tile_tok_end - tile_start, 0, tm).astype(I32)
    pos_flat = jnp.concatenate([pos[:, 0, :].reshape(-1), pos[:, 1, :].reshape(-1)])

    expert_ids = jnp.arange(N_EXPERTS, dtype=I32)
    seg_end_tile = jnp.sum(jnp.where(tile_e[:, None] == expert_ids, pends // tm, 0), axis=-1)
    tile_ids = jnp.arange(n_tiles, dtype=I32)
    e_after = jnp.sum(jnp.where(seg_end_tile[:, None] == tile_ids, tile_e, 0), axis=-1)
    next_e = jnp.where(seg_end_tile < n_used[0], e_after, -1).astype(I32)

    xs = _sc_scatter_rows(hn2.reshape(n_tok, d // 2), pos_flat, cap)
    y = _experts(xs, tile_e, n_used, tile_valid, next_e, w_gate, w_up, w_down)
    yg = _sc_gather_rows(y, pos_flat)
    return _combine_stream(x2, yg, rw, norm_out)


def kernel(x, norm_mix, w_in, b_in, conv_w, w_out, norm_ffn, w_group, b_group, w_expert, b_expert,
           w_gate, w_up, w_down, norm_final):
    depth = norm_mix.shape[0]
    assert depth == 1, "final-norm fusion below assumes a single layer"
    assert x.shape[-1] == 2 * D_MLSTM
    assert all(x.shape[1] % t == 0 for t in (TS_MIXER, TM_INPROJ, TC_COMBINE))
    return _layer(x, norm_mix[0], w_in[0], b_in[0], conv_w[0], w_out[0], norm_ffn[0],
                  w_group[0], b_group[0], w_expert[0], b_expert[0],
                  w_gate[0], w_up[0], w_down[0], norm_final)
```

```python
import jax
import jax.numpy as jnp
from jax import lax
from jax.experimental import pallas as pl
from jax.experimental.pallas import tpu as pltpu
from jax.experimental.pallas import tpu_sc as plsc

F32 = jnp.float32
BF16 = jnp.bfloat16
I32 = jnp.int32

N_HEADS = 4
HEAD_DIM = 128
D_MLSTM = N_HEADS * HEAD_DIM
D_CONV = 512
N_GROUPS = 4
EXPERTS_PER_GROUP = 8
N_EXPERTS = N_GROUPS * EXPERTS_PER_GROUP
D_FF = 512
RMS_EPS = 1e-6
Q_SCALE = HEAD_DIM ** -0.5

LANES = 128
CHUNK = LANES
AUG_ROWS = 16
ROUTER_COLS = 128
EXPERT_COL0 = 8

TM_INPROJ = 1024
TS_MIXER = 1024
OUT_COLS = 256
TM_EXPERT = 512
TC_COMBINE = 1024

VMEM_LIMIT = 56 * 1024 * 1024


HI_HALF = -65536


def _pack_rows(x):
    w = x.shape[1] // 2
    lo = lax.shift_right_logical(lax.bitcast_convert_type(x[:, :w].astype(BF16).astype(F32), I32), 16)
    hi = lax.bitcast_convert_type(x[:, w:].astype(BF16).astype(F32), I32) & HI_HALF
    return lo | hi


def _unpack_rows(words):
    lo = lax.bitcast_convert_type(words << 16, F32)
    hi = lax.bitcast_convert_type(words & HI_HALF, F32)
    return lo, hi


def _nt_dot(a, b):
    return lax.dot_general(a, b, (((1,), (1,)), ((), ())), preferred_element_type=F32)


def _inproj_kernel(x_ref, g_ref, wa_ref, ba_ref, wt_ref, bt_ref, a_ref, qvt_ref, gt_ref):
    x = x_ref[...]
    ms = jnp.mean(x * x, axis=-1, keepdims=True)
    hn = (x * lax.rsqrt(ms + RMS_EPS) * g_ref[...]).astype(BF16)
    n_blk = wa_ref.shape[1] // D_MLSTM
    for j in range(n_blk):
        sl = slice(j * D_MLSTM, (j + 1) * D_MLSTM)
        acc = jnp.dot(hn, wa_ref[:, sl], preferred_element_type=F32) + ba_ref[:, sl]
        a_ref[:, sl] = acc.astype(BF16)
    tr = _nt_dot(wt_ref[...], hn) + bt_ref[...]
    q_rows = slice(0, D_MLSTM)
    v_rows = slice(D_MLSTM, 2 * D_MLSTM)
    qvt_ref[q_rows, :] = (tr[q_rows, :] * Q_SCALE).astype(BF16)
    qvt_ref[v_rows, :] = tr[v_rows, :].astype(BF16)
    gt_ref[...] = tr[2 * D_MLSTM:2 * D_MLSTM + 2 * N_HEADS, :]


def _inproj(x, norm_mix, w_in, b_in):
    b_sz, s_len, d = x.shape
    tm = TM_INPROJ
    dm = D_MLSTM
    g0 = 4 * dm
    c0 = g0 + 2 * N_HEADS
    cols_a = jnp.concatenate([w_in[:, dm:2 * dm], w_in[:, 3 * dm:4 * dm], w_in[:, c0:c0 + 3 * D_CONV]], axis=1)
    bias_a = jnp.concatenate([b_in[dm:2 * dm], b_in[3 * dm:4 * dm], b_in[c0:c0 + 3 * D_CONV]])[None, :]
    n_gate = 2 * N_HEADS
    pad_rows = AUG_ROWS - n_gate
    w_t = jnp.concatenate([w_in[:, 0:dm], w_in[:, 2 * dm:3 * dm], w_in[:, g0:c0],
                           jnp.zeros((d, pad_rows), w_in.dtype)], axis=1).T
    b_t = jnp.concatenate([b_in[0:dm], b_in[2 * dm:3 * dm], b_in[g0:c0],
                           jnp.zeros((pad_rows,), b_in.dtype)])[:, None]
    n_t = 2 * dm + AUG_ROWS
    n_a = cols_a.shape[1]
    return pl.pallas_call(
        _inproj_kernel,
        grid=(b_sz, s_len // tm),
        in_specs=[
            pl.BlockSpec((None, tm, d), lambda b, s: (b, s, 0)),
            pl.BlockSpec((1, d), lambda b, s: (0, 0)),
            pl.BlockSpec((d, n_a), lambda b, s: (0, 0)),
            pl.BlockSpec((1, n_a), lambda b, s: (0, 0)),
            pl.BlockSpec((n_t, d), lambda b, s: (0, 0)),
            pl.BlockSpec((n_t, 1), lambda b, s: (0, 0)),
        ],
        out_specs=[
            pl.BlockSpec((None, tm, n_a), lambda b, s: (b, s, 0)),
            pl.BlockSpec((None, 2 * dm, tm), lambda b, s: (b, 0, s)),
            pl.BlockSpec((None, 2 * N_HEADS, tm), lambda b, s: (b, 0, s)),
        ],
        out_shape=[
            jax.ShapeDtypeStruct((b_sz, s_len, n_a), BF16),
            jax.ShapeDtypeStruct((b_sz, 2 * dm, s_len), BF16),
            jax.ShapeDtypeStruct((b_sz, 2 * N_HEADS, s_len), F32),
        ],
        compiler_params=pltpu.CompilerParams(
            dimension_semantics=("arbitrary", "arbitrary"), vmem_limit_bytes=VMEM_LIMIT),
        name="inproj",
    )(x, norm_mix[None, :], cols_a.astype(BF16), bias_a, w_t.astype(BF16), b_t)


def _mixer_kernel(k_ref, o_ref, cb_ref, cc_ref, cx_ref, qt_ref, vt_ref, gt_ref, x_ref,
                  cw_ref, wo_ref, g2_ref, wr_ref, br_ref,
                  x2_ref, hn2_ref, ri_ref, rw_ref, cnt_out_ref,
                  ct_ref, m_ref, carry_ref, cnt_ref, mix_ref):
    ts = x_ref.shape[0]
    n_chunks = ts // CHUNK
    first_tile = pl.program_id(1) == 0

    @pl.when(first_tile)
    def _():
        ct_ref[...] = jnp.zeros_like(ct_ref)
        m_ref[...] = jnp.zeros_like(m_ref)
        carry_ref[...] = jnp.zeros_like(carry_ref)

    @pl.when(jnp.logical_and(first_tile, pl.program_id(0) == 0))
    def _():
        cnt_ref[...] = jnp.zeros_like(cnt_ref)

    gates = gt_ref[...]
    li_all = gates[0:N_HEADS]
    fg = gates[N_HEADS:2 * N_HEADS]
    lf_all = jnp.minimum(fg, 0.0) - jnp.log1p(jnp.exp(-jnp.abs(fg)))
    key_j = lax.broadcasted_iota(I32, (CHUNK, CHUNK), 0)
    qry_i = lax.broadcasted_iota(I32, (CHUNK, CHUNK), 1)
    causal_t = key_j <= qry_i
    neg_inf = jnp.float32(-jnp.inf)
    ones_rows = (lax.broadcasted_iota(I32, (AUG_ROWS, CHUNK), 0) == 0).astype(BF16)

    prefix_ones = (key_j <= qry_i).astype(BF16)

    lf_hi = lf_all.astype(BF16)
    rem = lf_all - lf_hi.astype(F32)
    lf_mid = rem.astype(BF16)
    lf_lo = (rem - lf_mid.astype(F32)).astype(BF16)
    zero_bf = jnp.zeros((N_HEADS, CHUNK), BF16)
    parts = []
    for c in range(n_chunks):
        rows = slice(c * CHUNK, (c + 1) * CHUNK)
        parts += [lf_hi[:, rows], lf_mid[:, rows], lf_lo[:, rows], zero_bf]
    cum = jnp.dot(jnp.concatenate(parts, axis=0), prefix_ones, preferred_element_type=F32)

    m_prev = m_ref[0:N_HEADS, 0:1]
    for c in range(n_chunks):
        rows = slice(c * CHUNK, (c + 1) * CHUNK)
        li = li_all[:, rows]
        r0 = 4 * N_HEADS * c
        b_cum = (cum[r0:r0 + N_HEADS] + cum[r0 + N_HEADS:r0 + 2 * N_HEADS]
                 + cum[r0 + 2 * N_HEADS:r0 + 3 * N_HEADS])
        g_tot = b_cum[:, CHUNK - 1:CHUNK]
        a_end = g_tot - b_cum + li
        m_loc = jnp.max(a_end, axis=1, keepdims=True)
        u = li - b_cum
        m_new = jnp.maximum(g_tot + m_prev, m_loc)
        decay = jnp.exp(g_tot + m_prev - m_new)
        s_fac = jnp.exp(m_loc - m_new)
        w_key = jnp.exp(a_end - m_loc)
        u_cols = jnp.transpose(jnp.concatenate([u, jnp.zeros_like(u)], axis=0))

        def pair_row(x, h0):
            parts = [jnp.broadcast_to(x[h:h + 1, :], (1, CHUNK)) for h in (h0, h0 + 1)]
            return jnp.concatenate(parts, axis=1)

        def block_diag(a, b):
            z = jnp.zeros_like(a)
            return jnp.concatenate([jnp.concatenate([a, z], axis=1), jnp.concatenate([z, b], axis=1)], axis=0)

        for p in range(N_HEADS // 2):
            h0 = 2 * p
            pc = slice(h0 * HEAD_DIM, (h0 + 2) * HEAD_DIM)
            lo_l, hi_l = slice(0, CHUNK), slice(CHUNK, 2 * CHUNK)
            k2 = k_ref[rows, pc]
            q2 = qt_ref[pc, rows]
            v2 = vt_ref[pc, rows]
            s2 = jnp.dot(k2, block_diag(q2[0:HEAD_DIM], q2[HEAD_DIM:]), preferred_element_type=F32)
            u_t = jnp.concatenate([jnp.where(causal_t, u_cols[:, h:h + 1], neg_inf) for h in (h0, h0 + 1)],
                                  axis=1)
            mp = pair_row(m_prev, h0)
            m_tot = jnp.maximum(jnp.max(u_t, axis=0, keepdims=True), mp)
            f_inter = jnp.exp(mp - m_tot)
            floor = jnp.exp(-pair_row(b_cum, h0) - m_tot)
            p2 = (s2 * jnp.exp(u_t - m_tot)).astype(BF16)
            qf = [(q2[r].astype(F32) * f_inter[:, l]).astype(BF16)
                  for r, l in ((slice(0, HEAD_DIM), lo_l), (slice(HEAD_DIM, None), hi_l))]
            c_old = ct_ref[p]
            cb16 = c_old.astype(BF16)
            vt_aug = [jnp.concatenate([v2[r], ones_rows], axis=0)
                      for r in (slice(0, HEAD_DIM), slice(HEAD_DIM, None))]
            lhs = jnp.concatenate([vt_aug[0], cb16[:, lo_l], vt_aug[1], cb16[:, hi_l]], axis=1)
            rhs = block_diag(jnp.concatenate([p2[:, lo_l], qf[0]], axis=0),
                             jnp.concatenate([p2[:, hi_l], qf[1]], axis=0))
            r2 = jnp.dot(lhs, rhs, preferred_element_type=F32)
            w2 = pair_row(w_key, h0)
            vtw = jnp.concatenate([(vt_aug[0].astype(F32) * w2[:, lo_l]).astype(BF16),
                                   (vt_aug[1].astype(F32) * w2[:, hi_l]).astype(BF16)], axis=1)
            kv = jnp.dot(vtw, block_diag(k2[:, lo_l], k2[:, hi_l]), preferred_element_type=F32)
            ct_ref[p] = pair_row(decay, h0) * c_old + pair_row(s_fac, h0) * kv

            den = r2[HEAD_DIM:HEAD_DIM + 1, :]
            inv = 1.0 / jnp.maximum(jnp.abs(den), floor)
            h_t = r2[0:HEAD_DIM, :] * inv
            for h, l in ((h0, lo_l), (h0 + 1, hi_l)):
                hc = slice(h * HEAD_DIM, (h + 1) * HEAD_DIM)
                o_gate = jax.nn.sigmoid(o_ref[rows, hc].astype(F32))
                mix_ref[rows, hc] = (o_gate * jnp.transpose(h_t[:, l])).astype(BF16)
        m_prev = m_new
    m_ref[0:N_HEADS, :] = jnp.broadcast_to(m_prev, (N_HEADS, LANES))

    u_c = cc_ref[...].astype(F32) * cx_ref[...].astype(F32)
    rid = lax.broadcasted_iota(I32, u_c.shape, 0)
    prev1 = carry_ref[0:1, :]
    prev2 = carry_ref[1:2, :]
    u1 = jnp.where(rid == 0, prev1, pltpu.roll(u_c, 1, axis=0))
    u2 = jnp.where(rid == 0, prev2, jnp.where(rid == 1, prev1, pltpu.roll(u_c, 2, axis=0)))
    z = cw_ref[0:1, :] * u2 + cw_ref[1:2, :] * u1 + cw_ref[2:3, :] * u_c
    mix_ref[:, D_MLSTM:] = (cb_ref[...].astype(F32) * z).astype(BF16)
    carry_ref[0:1, :] = u_c[ts - 1:ts, :]
    carry_ref[1:2, :] = u_c[ts - 2:ts - 1, :]

    d_model = x_ref.shape[1]
    mix = mix_ref[...]
    ssq = jnp.zeros((ts, 1), F32)
    for j in range(d_model // OUT_COLS):
        cs = slice(j * OUT_COLS, (j + 1) * OUT_COLS)
        x2c = x_ref[:, cs] + jnp.dot(mix, wo_ref[:, cs], preferred_element_type=F32)
        x2_ref[:, cs] = x2c
        ssq = ssq + jnp.sum(x2c * x2c, axis=-1, keepdims=True)
    hn2 = x2_ref[...] * lax.rsqrt(ssq * (1.0 / d_model) + RMS_EPS) * g2_ref[...]
    hn2_ref[...] = _pack_rows(hn2)

    logits = jnp.dot(hn2.astype(BF16), wr_ref[...], preferred_element_type=F32) + br_ref[...]
    lt = jnp.transpose(logits)
    gl = lt[0:N_GROUPS]
    gmax = jnp.max(gl, axis=0, keepdims=True)
    gi = lax.broadcasted_iota(I32, gl.shape, 0)
    g_sel = jnp.min(jnp.where(gl == gmax, gi, N_GROUPS), axis=0, keepdims=True)
    p_g = 1.0 / jnp.sum(jnp.exp(gl - gmax), axis=0, keepdims=True)
    in_g = lt[EXPERT_COL0:EXPERT_COL0 + EXPERTS_PER_GROUP]
    for g in range(1, N_GROUPS):
        lo = EXPERT_COL0 + g * EXPERTS_PER_GROUP
        in_g = jnp.where(g_sel == g, lt[lo:lo + EXPERTS_PER_GROUP], in_g)
    ei = lax.broadcasted_iota(I32, in_g.shape, 0)
    v1 = jnp.max(in_g, axis=0, keepdims=True)
    i1 = jnp.min(jnp.where(in_g == v1, ei, EXPERTS_PER_GROUP), axis=0, keepdims=True)
    rest = jnp.where(ei == i1, neg_inf, in_g)
    v2 = jnp.max(rest, axis=0, keepdims=True)
    i2 = jnp.min(jnp.where(rest == v2, ei, EXPERTS_PER_GROUP), axis=0, keepdims=True)
    t = jnp.exp(v2 - v1)
    w1 = p_g / (1.0 + t)
    w2 = p_g * t / (1.0 + t)
    e1 = g_sel * EXPERTS_PER_GROUP + i1
    e2 = g_sel * EXPERTS_PER_GROUP + i2

    eio = lax.broadcasted_iota(I32, (N_EXPERTS, LANES), 0)
    base = cnt_ref[:, 0:1]
    zero_i = jnp.zeros((1, LANES), I32)
    n_blk = ts // LANES
    hits = []
    for k in range(n_blk):
        ls = slice(k * LANES, (k + 1) * LANES)
        hits.append((eio == e1[:, ls], eio == e2[:, ls]))
    onehots = [jnp.where(jnp.logical_or(a1, a2), 1.0, 0.0) for a1, a2 in hits]
    incl_all = jnp.dot(jnp.concatenate(onehots, axis=0).astype(BF16), prefix_ones,
                       preferred_element_type=F32).astype(I32)
    for k in range(n_blk):
        ls = slice(k * LANES, (k + 1) * LANES)
        is1, is2 = hits[k]
        onehot = onehots[k].astype(I32)
        incl = incl_all[k * N_EXPERTS:(k + 1) * N_EXPERTS]
        rank_e = base + incl - onehot
        r1 = jnp.sum(jnp.where(is1, rank_e, 0), axis=0, keepdims=True)
        r2 = jnp.sum(jnp.where(is2, rank_e, 0), axis=0, keepdims=True)
        ri_ref[:, ls] = jnp.concatenate(
            [e1[:, ls], e2[:, ls], r1, r2, zero_i, zero_i, zero_i, zero_i], axis=0)
        base = base + incl[:, LANES - 1:LANES]
    zero_f = jnp.zeros_like(w1)
    rw_ref[...] = jnp.concatenate([w1, w2, zero_f, zero_f, zero_f, zero_f, zero_f, zero_f], axis=0)
    cnt_new = jnp.broadcast_to(base, (N_EXPERTS, LANES))
    cnt_ref[...] = cnt_new
    cnt_out_ref[...] = cnt_new


def _mixer(x, a, qvt, gt, conv_w, w_out, norm_ffn, w_router, b_router):
    b_sz, s_len, d = x.shape
    ts = TS_MIXER
    dm = D_MLSTM

    def a_spec(j):
        return pl.BlockSpec((None, ts, dm), lambda b, s, j=j: (b, s, j))

    tok = lambda w: pl.BlockSpec((None, ts, w), lambda b, s: (b, s, 0))
    rowsp = pl.BlockSpec((None, 2 * N_HEADS, ts), lambda b, s: (b, 0, s))
    const = lambda shape: pl.BlockSpec(shape, lambda b, s: tuple(0 for _ in shape))
    return pl.pallas_call(
        _mixer_kernel,
        grid=(b_sz, s_len // ts),
        in_specs=[a_spec(0), a_spec(1), a_spec(2), a_spec(3), a_spec(4),
                  pl.BlockSpec((None, dm, ts), lambda b, s: (b, 0, s)),
                  pl.BlockSpec((None, dm, ts), lambda b, s: (b, 1, s)),
                  rowsp, tok(d),
                  const((3, D_CONV)), const((d, d)), const((1, d)),
                  const((d, ROUTER_COLS)), const((1, ROUTER_COLS))],
        out_specs=[tok(d), tok(d // 2), rowsp, rowsp, const((N_EXPERTS, LANES))],
        out_shape=[
            jax.ShapeDtypeStruct((b_sz, s_len, d), F32),
            jax.ShapeDtypeStruct((b_sz, s_len, d // 2), I32),
            jax.ShapeDtypeStruct((b_sz, 2 * N_HEADS, s_len), I32),
            jax.ShapeDtypeStruct((b_sz, 2 * N_HEADS, s_len), F32),
            jax.ShapeDtypeStruct((N_EXPERTS, LANES), I32),
        ],
        scratch_shapes=[
            pltpu.VMEM((N_HEADS // 2, HEAD_DIM + AUG_ROWS, 2 * HEAD_DIM), F32),
            pltpu.VMEM((2 * N_HEADS, LANES), F32),
            pltpu.VMEM((8, D_CONV), F32),
            pltpu.VMEM((N_EXPERTS, LANES), I32),
            pltpu.VMEM((ts, d), BF16),
        ],
        compiler_params=pltpu.CompilerParams(
            dimension_semantics=("arbitrary", "arbitrary"), vmem_limit_bytes=VMEM_LIMIT),
        name="mixer",
    )(a, a, a, a, a, qvt, qvt, gt, x, conv_w, w_out.astype(BF16), norm_ffn[None, :], w_router.astype(BF16), b_router)


SC_CORES = 2
SC_SUBCORES = 16
SC_WORKERS = SC_CORES * SC_SUBCORES
SC_CHUNK = 64


def _sc_worker_id():
    return lax.axis_index("s") * SC_CORES + lax.axis_index("c")


def _sc_scatter_rows(rows, idx, n_out):
    n_rows, width = rows.shape
    per_worker = n_rows // SC_WORKERS
    n_chunks = per_worker // SC_CHUNK
    assert n_chunks % 2 == 0
    mesh = plsc.VectorSubcoreMesh(core_axis_name="c", subcore_axis_name="s")

    def body(rows_hbm, idx_hbm, out_hbm, ia0, ib0, r0, ia1, ib1, r1, ls0, ls1, ss0, ss1):
        wid = _sc_worker_id()
        bufs = ((ia0, ib0, r0, ls0, ss0), (ia1, ib1, r1, ls1, ss1))

        def loads(j, b):
            ia, ib, r, ls, _ = bufs[b]
            base = pl.multiple_of(wid * per_worker + j * SC_CHUNK, SC_CHUNK)
            return (pltpu.make_async_copy(rows_hbm.at[pl.ds(base, SC_CHUNK)], r, ls),
                    pltpu.make_async_copy(idx_hbm.at[pl.ds(base, SC_CHUNK)], ia, ls),
                    pltpu.make_async_copy(idx_hbm.at[pl.ds(n_rows + base, SC_CHUNK)], ib, ls))

        def scatters(b):
            ia, ib, r, _, ss = bufs[b]
            return (pltpu.make_async_copy(r, out_hbm.at[ia], ss),
                    pltpu.make_async_copy(r, out_hbm.at[ib], ss))

        for c in loads(0, 0):
            c.start()

        @pl.loop(0, n_chunks, step=2)
        def _(j0):
            for b in range(2):
                j = j0 + b
                for c in loads(j, b):
                    c.wait()

                @pl.when(j >= 1)
                def _():
                    for c in scatters(1 - b):
                        c.wait()

                @pl.when(j + 1 < n_chunks)
                def _():
                    for c in loads(j + 1, 1 - b):
                        c.start()

                for c in scatters(b):
                    c.start()

        for c in scatters((n_chunks - 1) % 2):
            c.wait()

    idx_t = pltpu.VMEM((SC_CHUNK,), I32)
    row_t = pltpu.VMEM((SC_CHUNK, width), I32)
    sem_t = pltpu.SemaphoreType.DMA
    return pl.kernel(
        body,
        out_type=jax.ShapeDtypeStruct((n_out, width), I32),
        mesh=mesh,
        scratch_types=[idx_t, idx_t, row_t, idx_t, idx_t, row_t, sem_t, sem_t, sem_t, sem_t],
        name="sc_scatter",
    )(rows, idx)


def _sc_gather_rows(table, idx):
    n_rows = idx.shape[0]
    width = table.shape[1]
    per_worker = n_rows // SC_WORKERS
    n_chunks = per_worker // SC_CHUNK
    assert n_chunks % 2 == 0
    mesh = plsc.VectorSubcoreMesh(core_axis_name="c", subcore_axis_name="s")

    def body(table_hbm, idx_hbm, out_hbm, i0, r0, i1, r1, is0, is1, gs0, gs1, ws0, ws1):
        wid = _sc_worker_id()
        bufs = ((i0, r0, is0, gs0, ws0), (i1, r1, is1, gs1, ws1))

        def chunk_base(j):
            return pl.multiple_of(wid * per_worker + j * SC_CHUNK, SC_CHUNK)

        def idx_load(j, b):
            return pltpu.make_async_copy(idx_hbm.at[pl.ds(chunk_base(j), SC_CHUNK)], bufs[b][0], bufs[b][2])

        def gather(b):
            return pltpu.make_async_copy(table_hbm.at[bufs[b][0]], bufs[b][1], bufs[b][3])

        def writeback(j, b):
            return pltpu.make_async_copy(bufs[b][1], out_hbm.at[pl.ds(chunk_base(j), SC_CHUNK)], bufs[b][4])

        idx_load(0, 0).start()
        idx_load(0, 0).wait()
        gather(0).start()
        idx_load(1, 1).start()

        @pl.loop(0, n_chunks, step=2)
        def _(j0):
            for b in range(2):
                j = j0 + b

                @pl.when(j + 1 < n_chunks)
                def _():
                    idx_load(j + 1, 1 - b).wait()

                    @pl.when(j >= 1)
                    def _():
                        writeback(j - 1, 1 - b).wait()
                    gather(1 - b).start()

                gather(b).wait()
                writeback(j, b).start()

                @pl.when(j + 2 < n_chunks)
                def _():
                    idx_load(j + 2, b).start()

        writeback(n_chunks - 2, 0).wait()
        writeback(n_chunks - 1, 1).wait()

    idx_t = pltpu.VMEM((SC_CHUNK,), I32)
    row_t = pltpu.VMEM((SC_CHUNK, width), I32)
    sem_t = pltpu.SemaphoreType.DMA
    return pl.kernel(
        body,
        out_type=jax.ShapeDtypeStruct((n_rows, width), I32),
        mesh=mesh,
        scratch_types=[idx_t, row_t, idx_t, row_t, sem_t, sem_t, sem_t, sem_t, sem_t, sem_t],
        name="sc_gather",
    )(table, idx)


def _experts_kernel(seg_ref, cnt_ref, xs_hbm, wg_ref, wu_ref, wd_ref, y_hbm,
                    wgb_ref, wub_ref, wdb_ref, xbuf_ref, ybuf_ref, in_sem, out_sem):
    tm = xbuf_ref.shape[1]
    e = pl.program_id(0)
    first_tile = seg_ref[e] // tm
    n_t = (seg_ref[e + 1] - seg_ref[e]) // tm
    tok_end = seg_ref[e] + cnt_ref[e]

    def tile_in(j, slot):
        rows = pl.ds(pl.multiple_of((first_tile + j) * tm, tm), tm)
        return pltpu.make_async_copy(xs_hbm.at[rows], xbuf_ref.at[slot], in_sem.at[slot])

    def tile_out(j, slot):
        rows = pl.ds(pl.multiple_of((first_tile + j) * tm, tm), tm)
        return pltpu.make_async_copy(ybuf_ref.at[slot], y_hbm.at[rows], out_sem.at[slot])

    @pl.when(n_t > 0)
    def _():
        tile_in(0, 0).start()
        wgb_ref[...] = wg_ref[...].astype(BF16)
        wub_ref[...] = wu_ref[...].astype(BF16)
        wdb_ref[...] = wd_ref[...].astype(BF16)

        def body(j, carry):
            slot = lax.rem(j, 2)
            tile_in(j, slot).wait()

            @pl.when(j + 1 < n_t)
            def _():
                tile_in(j + 1, 1 - slot).start()

            @pl.when(j >= 2)
            def _():
                tile_out(j - 2, slot).wait()

            valid = tok_end - (first_tile + j) * tm
            row = lax.broadcasted_iota(I32, (tm, xbuf_ref.shape[2]), 0)
            x_lo, x_hi = _unpack_rows(jnp.where(row < valid, xbuf_ref[slot], 0))
            xb = jnp.concatenate([x_lo.astype(BF16), x_hi.astype(BF16)], axis=1)
            gate = jnp.dot(xb, wgb_ref[...], preferred_element_type=F32)
            up = jnp.dot(xb, wub_ref[...], preferred_element_type=F32)
            hid = (gate * jax.nn.sigmoid(gate) * up).astype(BF16)
            ybuf_ref[slot] = _pack_rows(jnp.dot(hid, wdb_ref[...], preferred_element_type=F32))
            tile_out(j, slot).start()
            return carry

        lax.fori_loop(0, n_t, body, 0)

        @pl.when(n_t >= 2)
        def _():
            tile_out(n_t - 2, lax.rem(n_t, 2)).wait()
        tile_out(n_t - 1, lax.rem(n_t - 1, 2)).wait()

    @pl.when(e == pl.num_programs(0) - 1)
    def _():
        ybuf_ref[0] = jnp.zeros((tm, ybuf_ref.shape[2]), I32)
        n_tiles = y_hbm.shape[0] // tm

        def zero_copy(t):
            return pltpu.make_async_copy(ybuf_ref.at[0], y_hbm.at[pl.ds(pl.multiple_of(t * tm, tm), tm)],
                                         out_sem.at[0])

        first_unused = seg_ref[N_EXPERTS] // tm
        lax.fori_loop(first_unused, n_tiles, lambda t, c: (zero_copy(t).start(), c)[1], 0)
        lax.fori_loop(first_unused, n_tiles, lambda t, c: (zero_copy(t).wait(), c)[1], 0)


def _experts(xs, seg, cnt, w_gate, w_up, w_down):
    cap, dw = xs.shape
    d = 2 * dw
    tm = TM_EXPERT
    w_spec = lambda shape: pl.BlockSpec((None,) + shape, lambda e, sg, ct: (e, 0, 0))
    return pl.pallas_call(
        _experts_kernel,
        grid_spec=pltpu.PrefetchScalarGridSpec(
            num_scalar_prefetch=2,
            grid=(N_EXPERTS,),
            in_specs=[pl.BlockSpec(memory_space=pl.ANY), w_spec((d, D_FF)), w_spec((d, D_FF)), w_spec((D_FF, d))],
            out_specs=pl.BlockSpec(memory_space=pl.ANY),
            scratch_shapes=[pltpu.VMEM((d, D_FF), BF16), pltpu.VMEM((d, D_FF), BF16), pltpu.VMEM((D_FF, d), BF16),
                            pltpu.VMEM((2, tm, dw), I32), pltpu.VMEM((2, tm, dw), I32),
                            pltpu.SemaphoreType.DMA((2,)), pltpu.SemaphoreType.DMA((2,))],
        ),
        out_shape=jax.ShapeDtypeStruct((cap, dw), I32),
        compiler_params=pltpu.CompilerParams(
            dimension_semantics=("arbitrary",), vmem_limit_bytes=VMEM_LIMIT),
        name="experts",
    )(seg, cnt, xs, w_gate, w_up, w_down)


def _combine_stream_kernel(x2_ref, y1_ref, y2_ref, rw_ref, gf_ref, out_ref):
    w_t = jnp.transpose(rw_ref[...])
    a_lo, a_hi = _unpack_rows(y1_ref[...])
    b_lo, b_hi = _unpack_rows(y2_ref[...])
    w1 = w_t[:, 0:1]
    w2 = w_t[:, 1:2]
    moe = jnp.concatenate([w1 * a_lo + w2 * b_lo, w1 * a_hi + w2 * b_hi], axis=1)
    x3 = x2_ref[...] + moe
    ms = jnp.mean(x3 * x3, axis=-1, keepdims=True)
    out_ref[...] = x3 * lax.rsqrt(ms + RMS_EPS) * gf_ref[...]


def _combine_stream(x2, yg, rw, norm_final):
    b_sz, s_len, d = x2.shape
    tc = TC_COMBINE
    n_s = s_len // tc
    n_steps = b_sz * n_s
    return pl.pallas_call(
        _combine_stream_kernel,
        grid=(n_steps,),
        in_specs=[
            pl.BlockSpec((tc, d), lambda g: (g, 0)),
            pl.BlockSpec((tc, d // 2), lambda g: (g, 0)),
            pl.BlockSpec((tc, d // 2), lambda g: (g + n_steps, 0)),
            pl.BlockSpec((None, 2 * N_HEADS, tc), lambda g: (g // n_s, 0, g % n_s)),
            pl.BlockSpec((1, d), lambda g: (0, 0)),
        ],
        out_specs=pl.BlockSpec((tc, d), lambda g: (g, 0)),
        out_shape=jax.ShapeDtypeStruct((b_sz * s_len, d), F32),
        compiler_params=pltpu.CompilerParams(
            dimension_semantics=("arbitrary",), vmem_limit_bytes=VMEM_LIMIT),
        name="combine_stream",
    )(x2.reshape(b_sz * s_len, d), yg, yg, rw, norm_final[None, :]).reshape(b_sz, s_len, d)


def _layer(x, norm_mix, w_in, b_in, conv_w, w_out, norm_ffn, w_group, b_group, w_expert, b_expert,
           w_gate, w_up, w_down, norm_out):
    b_sz, s_len, d = x.shape
    n_tok = b_sz * s_len
    a, qvt, gt = _inproj(x, norm_mix, w_in, b_in)

    pad_g = EXPERT_COL0 - N_GROUPS
    pad_e = ROUTER_COLS - EXPERT_COL0 - N_EXPERTS
    w_router = jnp.concatenate([w_group, jnp.zeros((d, pad_g), F32), w_expert, jnp.zeros((d, pad_e), F32)], axis=1)
    b_router = jnp.concatenate([b_group, jnp.zeros((pad_g,), F32), b_expert, jnp.zeros((pad_e,), F32)])[None, :]
    x2, hn2, ri, rw, counts = _mixer(x, a, qvt, gt, conv_w, w_out, norm_ffn, w_router, b_router)

    tm = TM_EXPERT
    cnt = counts[:, 0]
    padded = ((cnt + tm - 1) // tm) * tm
    pends = jnp.cumsum(padded).astype(I32)
    seg = jnp.concatenate([jnp.zeros((1,), I32), pends])
    cap = n_tok * 2 + N_EXPERTS * tm
    e_sel = ri[:, 0:2, :]
    seg_start = jnp.sum(jnp.where(e_sel[..., None] == jnp.arange(N_EXPERTS, dtype=I32), seg[:-1], 0), axis=-1)
    pos = ri[:, 2:4, :] + seg_start
    pos_flat = jnp.concatenate([pos[:, 0, :].reshape(-1), pos[:, 1, :].reshape(-1)])

    xs = _sc_scatter_rows(hn2.reshape(n_tok, d // 2), pos_flat, cap)
    y = _experts(xs, seg, cnt.astype(I32), w_gate, w_up, w_down)
    yg = _sc_gather_rows(y, pos_flat)
    return _combine_stream(x2, yg, rw, norm_out)


def kernel(x, norm_mix, w_in, b_in, conv_w, w_out, norm_ffn, w_group, b_group, w_expert, b_expert,
           w_gate, w_up, w_down, norm_final):
    depth = norm_mix.shape[0]
    assert depth == 1, "final-norm fusion below assumes a single layer"
    assert x.shape[-1] == 2 * D_MLSTM
    assert all(x.shape[1] % t == 0 for t in (TS_MIXER, TM_INPROJ, TC_COMBINE))
    return _layer(x, norm_mix[0], w_in[0], b_in[0], conv_w[0], w_out[0], norm_ffn[0],
                  w_group[0], b_group[0], w_expert[0], b_expert[0],
                  w_gate[0], w_up[0], w_down[0], norm_final)
```

```python
import jax
import jax.numpy as jnp
from jax import lax
from jax.experimental import pallas as pl
from jax.experimental.pallas import tpu as pltpu
from jax.experimental.pallas import tpu_sc as plsc

F32 = jnp.float32
BF16 = jnp.bfloat16
I32 = jnp.int32

N_HEADS = 4
HEAD_DIM = 128
D_MLSTM = N_HEADS * HEAD_DIM
D_CONV = 512
N_GROUPS = 4
EXPERTS_PER_GROUP = 8
N_EXPERTS = N_GROUPS * EXPERTS_PER_GROUP
D_FF = 512
RMS_EPS = 1e-6
Q_SCALE = HEAD_DIM ** -0.5

LANES = 128
CHUNK = LANES
AUG_ROWS = 16
ROUTER_COLS = 128
EXPERT_COL0 = 8

TM_INPROJ = 1024
TS_MIXER = 1024
OUT_COLS = 256
TM_EXPERT = 512
TC_COMBINE = 1024

VMEM_LIMIT = 56 * 1024 * 1024


HI_HALF = -65536


def _pack_rows(x):
    w = x.shape[1] // 2
    lo = lax.shift_right_logical(lax.bitcast_convert_type(x[:, :w].astype(BF16).astype(F32), I32), 16)
    hi = lax.bitcast_convert_type(x[:, w:].astype(BF16).astype(F32), I32) & HI_HALF
    return lo | hi


def _unpack_rows(words):
    lo = lax.bitcast_convert_type(words << 16, F32)
    hi = lax.bitcast_convert_type(words & HI_HALF, F32)
    return lo, hi


def _nt_dot(a, b):
    return lax.dot_general(a, b, (((1,), (1,)), ((), ())), preferred_element_type=F32)


def _inproj_kernel(x_ref, g_ref, wa_ref, ba_ref, wt_ref, bt_ref, a_ref, qvt_ref, gt_ref):
    x = x_ref[...]
    ms = jnp.mean(x * x, axis=-1, keepdims=True)
    hn = (x * lax.rsqrt(ms + RMS_EPS) * g_ref[...]).astype(BF16)
    n_blk = wa_ref.shape[1] // D_MLSTM
    for j in range(n_blk):
        sl = slice(j * D_MLSTM, (j + 1) * D_MLSTM)
        acc = jnp.dot(hn, wa_ref[:, sl], preferred_element_type=F32) + ba_ref[:, sl]
        a_ref[:, sl] = acc.astype(BF16)
    tr = _nt_dot(wt_ref[...], hn) + bt_ref[...]
    q_rows = slice(0, D_MLSTM)
    v_rows = slice(D_MLSTM, 2 * D_MLSTM)
    qvt_ref[q_rows, :] = (tr[q_rows, :] * Q_SCALE).astype(BF16)
    qvt_ref[v_rows, :] = tr[v_rows, :].astype(BF16)
    gt_ref[...] = tr[2 * D_MLSTM:2 * D_MLSTM + 2 * N_HEADS, :]


def _inproj(x, norm_mix, w_in, b_in):
    b_sz, s_len, d = x.shape
    tm = TM_INPROJ
    dm = D_MLSTM
    g0 = 4 * dm
    c0 = g0 + 2 * N_HEADS
    cols_a = jnp.concatenate([w_in[:, dm:2 * dm], w_in[:, 3 * dm:4 * dm], w_in[:, c0:c0 + 3 * D_CONV]], axis=1)
    bias_a = jnp.concatenate([b_in[dm:2 * dm], b_in[3 * dm:4 * dm], b_in[c0:c0 + 3 * D_CONV]])[None, :]
    n_gate = 2 * N_HEADS
    pad_rows = AUG_ROWS - n_gate
    w_t = jnp.concatenate([w_in[:, 0:dm], w_in[:, 2 * dm:3 * dm], w_in[:, g0:c0],
                           jnp.zeros((d, pad_rows), w_in.dtype)], axis=1).T
    b_t = jnp.concatenate([b_in[0:dm], b_in[2 * dm:3 * dm], b_in[g0:c0],
                           jnp.zeros((pad_rows,), b_in.dtype)])[:, None]
    n_t = 2 * dm + AUG_ROWS
    n_a = cols_a.shape[1]
    return pl.pallas_call(
        _inproj_kernel,
        grid=(b_sz, s_len // tm),
        in_specs=[
            pl.BlockSpec((None, tm, d), lambda b, s: (b, s, 0)),
            pl.BlockSpec((1, d), lambda b, s: (0, 0)),
            pl.BlockSpec((d, n_a), lambda b, s: (0, 0)),
            pl.BlockSpec((1, n_a), lambda b, s: (0, 0)),
            pl.BlockSpec((n_t, d), lambda b, s: (0, 0)),
            pl.BlockSpec((n_t, 1), lambda b, s: (0, 0)),
        ],
        out_specs=[
            pl.BlockSpec((None, tm, n_a), lambda b, s: (b, s, 0)),
            pl.BlockSpec((None, 2 * dm, tm), lambda b, s: (b, 0, s)),
            pl.BlockSpec((None, 2 * N_HEADS, tm), lambda b, s: (b, 0, s)),
        ],
        out_shape=[
            jax.ShapeDtypeStruct((b_sz, s_len, n_a), BF16),
            jax.ShapeDtypeStruct((b_sz, 2 * dm, s_len), BF16),
            jax.ShapeDtypeStruct((b_sz, 2 * N_HEADS, s_len), F32),
        ],
        compiler_params=pltpu.CompilerParams(
            dimension_semantics=("arbitrary", "arbitrary"), vmem_limit_bytes=VMEM_LIMIT),
        name="inproj",
    )(x, norm_mix[None, :], cols_a.astype(BF16), bias_a, w_t.astype(BF16), b_t)


def _mixer_kernel(k_ref, o_ref, cb_ref, cc_ref, cx_ref, qt_ref, vt_ref, gt_ref, x_ref,
                  cw_ref, wo_ref, g2_ref, wr_ref, br_ref,
                  x2_ref, hn2_ref, ri_ref, rw_ref, cnt_out_ref,
                  ct_ref, m_ref, carry_ref, cnt_ref, mix_ref):
    ts = x_ref.shape[0]
    n_chunks = ts // CHUNK
    first_tile = pl.program_id(1) == 0

    @pl.when(first_tile)
    def _():
        ct_ref[...] = jnp.zeros_like(ct_ref)
        m_ref[...] = jnp.zeros_like(m_ref)
        carry_ref[...] = jnp.zeros_like(carry_ref)

    @pl.when(jnp.logical_and(first_tile, pl.program_id(0) == 0))
    def _():
        cnt_ref[...] = jnp.zeros_like(cnt_ref)

    gates = gt_ref[...]
    li_all = gates[0:N_HEADS]
    fg = gates[N_HEADS:2 * N_HEADS]
    lf_all = jnp.minimum(fg, 0.0) - jnp.log1p(jnp.exp(-jnp.abs(fg)))
    key_j = lax.broadcasted_iota(I32, (CHUNK, CHUNK), 0)
    qry_i = lax.broadcasted_iota(I32, (CHUNK, CHUNK), 1)
    causal_t = key_j <= qry_i
    neg_inf = jnp.float32(-jnp.inf)
    ones_rows = (lax.broadcasted_iota(I32, (AUG_ROWS, CHUNK), 0) == 0).astype(BF16)

    prefix_ones = (key_j <= qry_i).astype(BF16)

    lf_hi = lf_all.astype(BF16)
    rem = lf_all - lf_hi.astype(F32)
    lf_mid = rem.astype(BF16)
    lf_lo = (rem - lf_mid.astype(F32)).astype(BF16)
    zero_bf = jnp.zeros((N_HEADS, CHUNK), BF16)
    parts = []
    for c in range(n_chunks):
        rows = slice(c * CHUNK, (c + 1) * CHUNK)
        parts += [lf_hi[:, rows], lf_mid[:, rows], lf_lo[:, rows], zero_bf]
    cum = jnp.dot(jnp.concatenate(parts, axis=0), prefix_ones, preferred_element_type=F32)

    m_prev = m_ref[0:N_HEADS, 0:1]
    for c in range(n_chunks):
        rows = slice(c * CHUNK, (c + 1) * CHUNK)
        li = li_all[:, rows]
        r0 = 4 * N_HEADS * c
        b_cum = (cum[r0:r0 + N_HEADS] + cum[r0 + N_HEADS:r0 + 2 * N_HEADS]
                 + cum[r0 + 2 * N_HEADS:r0 + 3 * N_HEADS])
        g_tot = b_cum[:, CHUNK - 1:CHUNK]
        a_end = g_tot - b_cum + li
        m_loc = jnp.max(a_end, axis=1, keepdims=True)
        u = li - b_cum
        m_new = jnp.maximum(g_tot + m_prev, m_loc)
        decay = jnp.exp(g_tot + m_prev - m_new)
        s_fac = jnp.exp(m_loc - m_new)
        w_key = jnp.exp(a_end - m_loc)
        u_cols = jnp.transpose(jnp.concatenate([u, jnp.zeros_like(u)], axis=0))

        def pair_row(x, h0):
            parts = [jnp.broadcast_to(x[h:h + 1, :], (1, CHUNK)) for h in (h0, h0 + 1)]
            return jnp.concatenate(parts, axis=1)

        def block_diag(a, b):
            z = jnp.zeros_like(a)
            return jnp.concatenate([jnp.concatenate([a, z], axis=1), jnp.concatenate([z, b], axis=1)], axis=0)

        for p in range(N_HEADS // 2):
            h0 = 2 * p
            pc = slice(h0 * HEAD_DIM, (h0 + 2) * HEAD_DIM)
            lo_l, hi_l = slice(0, CHUNK), slice(CHUNK, 2 * CHUNK)
            k2 = k_ref[rows, pc]
            q2 = qt_ref[pc, rows]
            v2 = vt_ref[pc, rows]
            s2 = jnp.dot(k2, block_diag(q2[0:HEAD_DIM], q2[HEAD_DIM:]), preferred_element_type=F32)
            u_t = jnp.concatenate([jnp.where(causal_t, u_cols[:, h:h + 1], neg_inf) for h in (h0, h0 + 1)],
                                  axis=1)
            mp = pair_row(m_prev, h0)
            m_tot = jnp.maximum(jnp.max(u_t, axis=0, keepdims=True), mp)
            f_inter = jnp.exp(mp - m_tot)
            floor = jnp.exp(-pair_row(b_cum, h0) - m_tot)
            p2 = (s2 * jnp.exp(u_t - m_tot)).astype(BF16)
            qf = [(q2[r].astype(F32) * f_inter[:, l]).astype(BF16)
                  for r, l in ((slice(0, HEAD_DIM), lo_l), (slice(HEAD_DIM, None), hi_l))]
            c_old = ct_ref[p]
            cb16 = c_old.astype(BF16)
            vt_aug = [jnp.concatenate([v2[r], ones_rows], axis=0)
                      for r in (slice(0, HEAD_DIM), slice(HEAD_DIM, None))]
            lhs = jnp.concatenate([vt_aug[0], cb16[:, lo_l], vt_aug[1], cb16[:, hi_l]], axis=1)
            rhs = block_diag(jnp.concatenate([p2[:, lo_l], qf[0]], axis=0),
                             jnp.concatenate([p2[:, hi_l], qf[1]], axis=0))
            r2 = jnp.dot(lhs, rhs, preferred_element_type=F32)
            w2 = pair_row(w_key, h0)
            vtw = jnp.concatenate([(vt_aug[0].astype(F32) * w2[:, lo_l]).astype(BF16),
                                   (vt_aug[1].astype(F32) * w2[:, hi_l]).astype(BF16)], axis=1)
            kv = jnp.dot(vtw, block_diag(k2[:, lo_l], k2[:, hi_l]), preferred_element_type=F32)
            ct_ref[p] = pair_row(decay, h0) * c_old + pair_row(s_fac, h0) * kv

            den = r2[HEAD_DIM:HEAD_DIM + 1, :]
            inv = 1.0 / jnp.maximum(jnp.abs(den), floor)
            h_t = r2[0:HEAD_DIM, :] * inv
            for h, l in ((h0, lo_l), (h0 + 1, hi_l)):
                hc = slice(h * HEAD_DIM, (h + 1) * HEAD_DIM)
                o_gate = jax.nn.sigmoid(o_ref[rows, hc].astype(F32))
                mix_ref[rows, hc] = (o_gate * jnp.transpose(h_t[:, l])).astype(BF16)
        m_prev = m_new
    m_ref[0:N_HEADS, :] = jnp.broadcast_to(m_prev, (N_HEADS, LANES))

    u_c = cc_ref[...].astype(F32) * cx_ref[...].astype(F32)
    rid = lax.broadcasted_iota(I32, u_c.shape, 0)
    prev1 = carry_ref[0:1, :]
    prev2 = carry_ref[1:2, :]
    u1 = jnp.where(rid == 0, prev1, pltpu.roll(u_c, 1, axis=0))
    u2 = jnp.where(rid == 0, prev2, jnp.where(rid == 1, prev1, pltpu.roll(u_c, 2, axis=0)))
    z = cw_ref[0:1, :] * u2 + cw_ref[1:2, :] * u1 + cw_ref[2:3, :] * u_c
    mix_ref[:, D_MLSTM:] = (cb_ref[...].astype(F32) * z).astype(BF16)
    carry_ref[0:1, :] = u_c[ts - 1:ts, :]
    carry_ref[1:2, :] = u_c[ts - 2:ts - 1, :]

    d_model = x_ref.shape[1]
    mix = mix_ref[...]
    ssq = jnp.zeros((ts, 1), F32)
    for j in range(d_model // OUT_COLS):
        cs = slice(j * OUT_COLS, (j + 1) * OUT_COLS)
        x2c = x_ref[:, cs] + jnp.dot(mix, wo_ref[:, cs], preferred_element_type=F32)
        x2_ref[:, cs] = x2c
        ssq = ssq + jnp.sum(x2c * x2c, axis=-1, keepdims=True)
    hn2 = x2_ref[...] * lax.rsqrt(ssq * (1.0 / d_model) + RMS_EPS) * g2_ref[...]
    hn2_ref[...] = _pack_rows(hn2)

    logits = jnp.dot(hn2.astype(BF16), wr_ref[...], preferred_element_type=F32) + br_ref[...]
    lt = jnp.transpose(logits)
    gl = lt[0:N_GROUPS]
    gmax = jnp.max(gl, axis=0, keepdims=True)
    gi = lax.broadcasted_iota(I32, gl.shape, 0)
    g_sel = jnp.min(jnp.where(gl == gmax, gi, N_GROUPS), axis=0, keepdims=True)
    p_g = 1.0 / jnp.sum(jnp.exp(gl - gmax), axis=0, keepdims=True)
    in_g = lt[EXPERT_COL0:EXPERT_COL0 + EXPERTS_PER_GROUP]
    for g in range(1, N_GROUPS):
        lo = EXPERT_COL0 + g * EXPERTS_PER_GROUP
        in_g = jnp.where(g_sel == g, lt[lo:lo + EXPERTS_PER_GROUP], in_g)
    ei = lax.broadcasted_iota(I32, in_g.shape, 0)
    v1 = jnp.max(in_g, axis=0, keepdims=True)
    i1 = jnp.min(jnp.where(in_g == v1, ei, EXPERTS_PER_GROUP), axis=0, keepdims=True)
    rest = jnp.where(ei == i1, neg_inf, in_g)
    v2 = jnp.max(rest, axis=0, keepdims=True)
    i2 = jnp.min(jnp.where(rest == v2, ei, EXPERTS_PER_GROUP), axis=0, keepdims=True)
    t = jnp.exp(v2 - v1)
    w1 = p_g / (1.0 + t)
    w2 = p_g * t / (1.0 + t)
    e1 = g_sel * EXPERTS_PER_GROUP + i1
    e2 = g_sel * EXPERTS_PER_GROUP + i2

    eio = lax.broadcasted_iota(I32, (N_EXPERTS, LANES), 0)
    base = cnt_ref[:, 0:1]
    zero_i = jnp.zeros((1, LANES), I32)
    n_blk = ts // LANES
    hits = []
    for k in range(n_blk):
        ls = slice(k * LANES, (k + 1) * LANES)
        hits.append((eio == e1[:, ls], eio == e2[:, ls]))
    onehots = [jnp.where(jnp.logical_or(a1, a2), 1.0, 0.0) for a1, a2 in hits]
    incl_all = jnp.dot(jnp.concatenate(onehots, axis=0).astype(BF16), prefix_ones,
                       preferred_element_type=F32).astype(I32)
    for k in range(n_blk):
        ls = slice(k * LANES, (k + 1) * LANES)
        is1, is2 = hits[k]
        onehot = onehots[k].astype(I32)
        incl = incl_all[k * N_EXPERTS:(k + 1) * N_EXPERTS]
        rank_e = base + incl - onehot
        r1 = jnp.sum(jnp.where(is1, rank_e, 0), axis=0, keepdims=True)
        r2 = jnp.sum(jnp.where(is2, rank_e, 0), axis=0, keepdims=True)
        ri_ref[:, ls] = jnp.concatenate(
            [e1[:, ls], e2[:, ls], r1, r2, zero_i, zero_i, zero_i, zero_i], axis=0)
        base = base + incl[:, LANES - 1:LANES]
    zero_f = jnp.zeros_like(w1)
    rw_ref[...] = jnp.concatenate([w1, w2, zero_f, zero_f, zero_f, zero_f, zero_f, zero_f], axis=0)
    cnt_new = jnp.broadcast_to(base, (N_EXPERTS, LANES))
    cnt_ref[...] = cnt_new
    cnt_out_ref[...] = cnt_new


def _mixer(x, a, qvt, gt, conv_w, w_out, norm_ffn, w_router, b_router):
    b_sz, s_len, d = x.shape
    ts = TS_MIXER
    dm = D_MLSTM

    def a_spec(j):
        return pl.BlockSpec((None, ts, dm), lambda b, s, j=j: (b, s, j))

    tok = lambda w: pl.BlockSpec((None, ts, w), lambda b, s: (b, s, 0))
    rowsp = pl.BlockSpec((None, 2 * N_HEADS, ts), lambda b, s: (b, 0, s))
    const = lambda shape: pl.BlockSpec(shape, lambda b, s: tuple(0 for _ in shape))
    return pl.pallas_call(
        _mixer_kernel,
        grid=(b_sz, s_len // ts),
        in_specs=[a_spec(0), a_spec(1), a_spec(2), a_spec(3), a_spec(4),
                  pl.BlockSpec((None, dm, ts), lambda b, s: (b, 0, s)),
                  pl.BlockSpec((None, dm, ts), lambda b, s: (b, 1, s)),
                  rowsp, tok(d),
                  const((3, D_CONV)), const((d, d)), const((1, d)),
                  const((d, ROUTER_COLS)), const((1, ROUTER_COLS))],
        out_specs=[tok(d), tok(d // 2), rowsp, rowsp, const((N_EXPERTS, LANES))],
        out_shape=[
            jax.ShapeDtypeStruct((b_sz, s_len, d), F32),
            jax.ShapeDtypeStruct((b_sz, s_len, d // 2), I32),
            jax.ShapeDtypeStruct((b_sz, 2 * N_HEADS, s_len), I32),
            jax.ShapeDtypeStruct((b_sz, 2 * N_HEADS, s_len), F32),
            jax.ShapeDtypeStruct((N_EXPERTS, LANES), I32),
        ],
        scratch_shapes=[
            pltpu.VMEM((N_HEADS // 2, HEAD_DIM + AUG_ROWS, 2 * HEAD_DIM), F32),
            pltpu.VMEM((2 * N_HEADS, LANES), F32),
            pltpu.VMEM((8, D_CONV), F32),
            pltpu.VMEM((N_EXPERTS, LANES), I32),
            pltpu.VMEM((ts, d), BF16),
        ],
        compiler_params=pltpu.CompilerParams(
            dimension_semantics=("arbitrary", "arbitrary"), vmem_limit_bytes=VMEM_LIMIT),
        name="mixer",
    )(a, a, a, a, a, qvt, qvt, gt, x, conv_w, w_out.astype(BF16), norm_ffn[None, :], w_router.astype(BF16), b_router)


SC_CORES = 2
SC_SUBCORES = 16
SC_WORKERS = SC_CORES * SC_SUBCORES
SC_CHUNK = 64


def _sc_worker_id():
    return lax.axis_index("s") * SC_CORES + lax.axis_index("c")


def _sc_scatter_rows(rows, idx, n_out):
    n_rows, width = rows.shape
    per_worker = n_rows // SC_WORKERS
    n_chunks = per_worker // SC_CHUNK
    assert n_chunks % 2 == 0
    mesh = plsc.VectorSubcoreMesh(core_axis_name="c", subcore_axis_name="s")

    def body(rows_hbm, idx_hbm, out_hbm, ia0, ib0, r0, ia1, ib1, r1, ls0, ls1, ss0, ss1):
        wid = _sc_worker_id()
        bufs = ((ia0, ib0, r0, ls0, ss0), (ia1, ib1, r1, ls1, ss1))

        def loads(j, b):
            ia, ib, r, ls, _ = bufs[b]
            base = pl.multiple_of(wid * per_worker + j * SC_CHUNK, SC_CHUNK)
            return (pltpu.make_async_copy(rows_hbm.at[pl.ds(base, SC_CHUNK)], r, ls),
                    pltpu.make_async_copy(idx_hbm.at[pl.ds(base, SC_CHUNK)], ia, ls),
                    pltpu.make_async_copy(idx_hbm.at[pl.ds(n_rows + base, SC_CHUNK)], ib, ls))

        def scatters(b):
            ia, ib, r, _, ss = bufs[b]
            return (pltpu.make_async_copy(r, out_hbm.at[ia], ss),
                    pltpu.make_async_copy(r, out_hbm.at[ib], ss))

        for c in loads(0, 0):
            c.start()

        @pl.loop(0, n_chunks, step=2)
        def _(j0):
            for b in range(2):
                j = j0 + b
                for c in loads(j, b):
                    c.wait()

                @pl.when(j >= 1)
                def _():
                    for c in scatters(1 - b):
                        c.wait()

                @pl.when(j + 1 < n_chunks)
                def _():
                    for c in loads(j + 1, 1 - b):
                        c.start()

                for c in scatters(b):
                    c.start()

        for c in scatters((n_chunks - 1) % 2):
            c.wait()

    idx_t = pltpu.VMEM((SC_CHUNK,), I32)
    row_t = pltpu.VMEM((SC_CHUNK, width), I32)
    sem_t = pltpu.SemaphoreType.DMA
    return pl.kernel(
        body,
        out_type=jax.ShapeDtypeStruct((n_out, width), I32),
        mesh=mesh,
        scratch_types=[idx_t, idx_t, row_t, idx_t, idx_t, row_t, sem_t, sem_t, sem_t, sem_t],
        name="sc_scatter",
    )(rows, idx)


def _sc_gather_rows(table, idx):
    n_rows = idx.shape[0]
    width = table.shape[1]
    per_worker = n_rows // SC_WORKERS
    n_chunks = per_worker // SC_CHUNK
    assert n_chunks % 2 == 0
    mesh = plsc.VectorSubcoreMesh(core_axis_name="c", subcore_axis_name="s")

    def body(table_hbm, idx_hbm, out_hbm, i0, r0, i1, r1, is0, is1, gs0, gs1, ws0, ws1):
        wid = _sc_worker_id()
        bufs = ((i0, r0, is0, gs0, ws0), (i1, r1, is1, gs1, ws1))

        def chunk_base(j):
            return pl.multiple_of(wid * per_worker + j * SC_CHUNK, SC_CHUNK)

        def idx_load(j, b):
            return pltpu.make_async_copy(idx_hbm.at[pl.ds(chunk_base(j), SC_CHUNK)], bufs[b][0], bufs[b][2])

        def gather(b):
            return pltpu.make_async_copy(table_hbm.at[bufs[b][0]], bufs[b][1], bufs[b][3])

        def writeback(j, b):
            return pltpu.make_async_copy(bufs[b][1], out_hbm.at[pl.ds(chunk_base(j), SC_CHUNK)], bufs[b][4])

        idx_load(0, 0).start()
        idx_load(0, 0).wait()
        gather(0).start()
        idx_load(1, 1).start()

        @pl.loop(0, n_chunks, step=2)
        def _(j0):
            for b in range(2):
                j = j0 + b

                @pl.when(j + 1 < n_chunks)
                def _():
                    idx_load(j + 1, 1 - b).wait()

                    @pl.when(j >= 1)
                    def _():
                        writeback(j - 1, 1 - b).wait()
                    gather(1 - b).start()

                gather(b).wait()
                writeback(j, b).start()

                @pl.when(j + 2 < n_chunks)
                def _():
                    idx_load(j + 2, b).start()

        writeback(n_chunks - 2, 0).wait()
        writeback(n_chunks - 1, 1).wait()

    idx_t = pltpu.VMEM((SC_CHUNK,), I32)
    row_t = pltpu.VMEM((SC_CHUNK, width), I32)
    sem_t = pltpu.SemaphoreType.DMA
    return pl.kernel(
        body,
        out_type=jax.ShapeDtypeStruct((n_rows, width), I32),
        mesh=mesh,
        scratch_types=[idx_t, row_t, idx_t, row_t, sem_t, sem_t, sem_t, sem_t, sem_t, sem_t],
        name="sc_gather",
    )(table, idx)


def _experts_kernel(te_ref, nu_ref, tv_ref, ne_ref, xs_hbm, wg_hbm, wu_hbm, wd_hbm, y_hbm,
                    wgs_ref, wus_ref, wds_ref, wgb_ref, wub_ref, wdb_ref, xbuf_ref, ybuf_ref,
                    wsem, in_sem, out_sem):
    tm = xbuf_ref.shape[1]
    n_used = nu_ref[0]

    def weight_copies(expert):
        return (pltpu.make_async_copy(wg_hbm.at[expert], wgs_ref, wsem),
                pltpu.make_async_copy(wu_hbm.at[expert], wus_ref, wsem),
                pltpu.make_async_copy(wd_hbm.at[expert], wds_ref, wsem))

    def tile_in(j, slot):
        rows = pl.ds(pl.multiple_of(j * tm, tm), tm)
        return pltpu.make_async_copy(xs_hbm.at[rows], xbuf_ref.at[slot], in_sem.at[slot])

    def tile_out(j, slot):
        rows = pl.ds(pl.multiple_of(j * tm, tm), tm)
        return pltpu.make_async_copy(ybuf_ref.at[slot], y_hbm.at[rows], out_sem.at[slot])

    @pl.when(n_used > 0)
    def _():
        n_t = n_used
        for c in weight_copies(te_ref[0]):
            c.start()
        tile_in(0, 0).start()

        def body(j, carry):
            slot = lax.rem(j, 2)
            e = te_ref[j]
            new_expert = jnp.logical_or(j == 0, te_ref[jnp.maximum(j - 1, 0)] != e)

            @pl.when(new_expert)
            def _():
                for c in weight_copies(e):
                    c.wait()
                wgb_ref[...] = wgs_ref[...].astype(BF16)
                wub_ref[...] = wus_ref[...].astype(BF16)
                wdb_ref[...] = wds_ref[...].astype(BF16)

                @pl.when(ne_ref[j] >= 0)
                def _():
                    for c in weight_copies(ne_ref[j]):
                        c.start()

            tile_in(j, slot).wait()

            @pl.when(j + 1 < n_t)
            def _():
                tile_in(j + 1, 1 - slot).start()

            @pl.when(j >= 2)
            def _():
                tile_out(j - 2, slot).wait()

            valid = tv_ref[j]
            row = lax.broadcasted_iota(I32, (tm, xbuf_ref.shape[2]), 0)
            x_lo, x_hi = _unpack_rows(jnp.where(row < valid, xbuf_ref[slot], 0))
            xb = jnp.concatenate([x_lo.astype(BF16), x_hi.astype(BF16)], axis=1)
            gate = jnp.dot(xb, wgb_ref[...], preferred_element_type=F32)
            up = jnp.dot(xb, wub_ref[...], preferred_element_type=F32)
            hid = (gate * jax.nn.sigmoid(gate) * up).astype(BF16)
            ybuf_ref[slot] = _pack_rows(jnp.dot(hid, wdb_ref[...], preferred_element_type=F32))
            tile_out(j, slot).start()
            return carry

        lax.fori_loop(0, n_t, body, 0)

        @pl.when(n_t >= 2)
        def _():
            tile_out(n_t - 2, lax.rem(n_t, 2)).wait()
        tile_out(n_t - 1, lax.rem(n_t - 1, 2)).wait()

    ybuf_ref[0] = jnp.zeros((tm, ybuf_ref.shape[2]), I32)
    n_tiles = y_hbm.shape[0] // tm

    def zero_copy(t):
        return pltpu.make_async_copy(ybuf_ref.at[0], y_hbm.at[pl.ds(pl.multiple_of(t * tm, tm), tm)],
                                     out_sem.at[0])

    lax.fori_loop(n_used, n_tiles, lambda t, c: (zero_copy(t).start(), c)[1], 0)
    lax.fori_loop(n_used, n_tiles, lambda t, c: (zero_copy(t).wait(), c)[1], 0)


def _experts(xs, tile_e, n_used, tile_valid, next_e, w_gate, w_up, w_down):
    cap, dw = xs.shape
    d = 2 * dw
    tm = TM_EXPERT
    any_spec = pl.BlockSpec(memory_space=pl.ANY)
    return pl.pallas_call(
        _experts_kernel,
        grid_spec=pltpu.PrefetchScalarGridSpec(
            num_scalar_prefetch=4,
            grid=(1,),
            in_specs=[any_spec, any_spec, any_spec, any_spec],
            out_specs=any_spec,
            scratch_shapes=[pltpu.VMEM((d, D_FF), F32), pltpu.VMEM((d, D_FF), F32), pltpu.VMEM((D_FF, d), F32),
                            pltpu.VMEM((d, D_FF), BF16), pltpu.VMEM((d, D_FF), BF16), pltpu.VMEM((D_FF, d), BF16),
                            pltpu.VMEM((2, tm, dw), I32), pltpu.VMEM((2, tm, dw), I32),
                            pltpu.SemaphoreType.DMA(()), pltpu.SemaphoreType.DMA((2,)),
                            pltpu.SemaphoreType.DMA((2,))],
        ),
        out_shape=jax.ShapeDtypeStruct((cap, dw), I32),
        compiler_params=pltpu.CompilerParams(
            dimension_semantics=("arbitrary",), vmem_limit_bytes=VMEM_LIMIT),
        name="experts",
    )(tile_e, n_used, tile_valid, next_e, xs, w_gate, w_up, w_down)


def _combine_stream_kernel(x2_ref, y1_ref, y2_ref, rw_ref, gf_ref, out_ref):
    w_t = jnp.transpose(rw_ref[...])
    a_lo, a_hi = _unpack_rows(y1_ref[...])
    b_lo, b_hi = _unpack_rows(y2_ref[...])
    w1 = w_t[:, 0:1]
    w2 = w_t[:, 1:2]
    moe = jnp.concatenate([w1 * a_lo + w2 * b_lo, w1 * a_hi + w2 * b_hi], axis=1)
    x3 = x2_ref[...] + moe
    ms = jnp.mean(x3 * x3, axis=-1, keepdims=True)
    out_ref[...] = x3 * lax.rsqrt(ms + RMS_EPS) * gf_ref[...]


def _combine_stream(x2, yg, rw, norm_final):
    b_sz, s_len, d = x2.shape
    tc = TC_COMBINE
    n_s = s_len // tc
    n_steps = b_sz * n_s
    return pl.pallas_call(
        _combine_stream_kernel,
        grid=(n_steps,),
        in_specs=[
            pl.BlockSpec((tc, d), lambda g: (g, 0)),
            pl.BlockSpec((tc, d // 2), lambda g: (g, 0)),
            pl.BlockSpec((tc, d // 2), lambda g: (g + n_steps, 0)),
            pl.BlockSpec((None, 2 * N_HEADS, tc), lambda g: (g // n_s, 0, g % n_s)),
            pl.BlockSpec((1, d), lambda g: (0, 0)),
        ],
        out_specs=pl.BlockSpec((tc, d), lambda g: (g, 0)),
        out_shape=jax.ShapeDtypeStruct((b_sz * s_len, d), F32),
        compiler_params=pltpu.CompilerParams(
            dimension_semantics=("arbitrary",), vmem_limit_bytes=VMEM_LIMIT),
        name="combine_stream",
    )(x2.reshape(b_sz * s_len, d), yg, yg, rw, norm_final[None, :]).reshape(b_sz, s_len, d)


def _layer(x, norm_mix, w_in, b_in, conv_w, w_out, norm_ffn, w_group, b_group, w_expert, b_expert,
           w_gate, w_up, w_down, norm_out):
    b_sz, s_len, d = x.shape
    n_tok = b_sz * s_len
    a, qvt, gt = _inproj(x, norm_mix, w_in, b_in)

    pad_g = EXPERT_COL0 - N_GROUPS
    pad_e = ROUTER_COLS - EXPERT_COL0 - N_EXPERTS
    w_router = jnp.concatenate([w_group, jnp.zeros((d, pad_g), F32), w_expert, jnp.zeros((d, pad_e), F32)], axis=1)
    b_router = jnp.concatenate([b_group, jnp.zeros((pad_g,), F32), b_expert, jnp.zeros((pad_e,), F32)])[None, :]
    x2, hn2, ri, rw, counts = _mixer(x, a, qvt, gt, conv_w, w_out, norm_ffn, w_router, b_router)

    tm = TM_EXPERT
    cnt = counts[:, 0]
    padded = ((cnt + tm - 1) // tm) * tm
    pends = jnp.cumsum(padded).astype(I32)
    seg = jnp.concatenate([jnp.zeros((1,), I32), pends])
    n_tiles = (n_tok * 2) // tm + N_EXPERTS
    cap = n_tiles * tm
    tile_start = jnp.arange(n_tiles, dtype=I32) * tm
    tile_e = jnp.minimum(jnp.sum((tile_start[:, None] >= pends[None, :]).astype(I32), axis=1),
                         N_EXPERTS - 1)
    n_used = pends[-1:] // tm
    e_sel = ri[:, 0:2, :]
    seg_start = jnp.sum(jnp.where(e_sel[..., None] == jnp.arange(N_EXPERTS, dtype=I32), seg[:-1], 0), axis=-1)
    pos = ri[:, 2:4, :] + seg_start

    seg_tok_end = seg[:-1] + cnt.astype(I32)
    tile_tok_end = jnp.sum(jnp.where(tile_e[:, None] == jnp.arange(N_EXPERTS, dtype=I32), seg_tok_end, 0), axis=-1)
    tile_valid = jnp.clip(tile_tok_end - tile_start, 0, tm).astype(I32)
    pos_flat = jnp.concatenate([pos[:, 0, :].reshape(-1), pos[:, 1, :].reshape(-1)])

    expert_ids = jnp.arange(N_EXPERTS, dtype=I32)
    seg_end_tile = jnp.sum(jnp.where(tile_e[:, None] == expert_ids, pends // tm, 0), axis=-1)
    tile_ids = jnp.arange(n_tiles, dtype=I32)
    e_after = jnp.sum(jnp.where(seg_end_tile[:, None] == tile_ids, tile_e, 0), axis=-1)
    next_e = jnp.where(seg_end_tile < n_used[0], e_after, -1).astype(I32)

    xs = _sc_scatter_rows(hn2.reshape(n_tok, d // 2), pos_flat, cap)
    y = _experts(xs, tile_e, n_used, tile_valid, next_e, w_gate, w_up, w_down)
    yg = _sc_gather_rows(y, pos_flat)
    return _combine_stream(x2, yg, rw, norm_out)


def kernel(x, norm_mix, w_in, b_in, conv_w, w_out, norm_ffn, w_group, b_group, w_expert, b_expert,
           w_gate, w_up, w_down, norm_final):
    depth = norm_mix.shape[0]
    assert depth == 1, "final-norm fusion below assumes a single layer"
    assert x.shape[-1] == 2 * D_MLSTM
    assert all(x.shape[1] % t == 0 for t in (TS_MIXER, TM_INPROJ, TC_COMBINE))
    return _layer(x, norm_mix[0], w_in[0], b_in[0], conv_w[0], w_out[0], norm_ffn[0],
                  w_group[0], b_group[0], w_expert[0], b_expert[0],
                  w_gate[0], w_up[0], w_down[0], norm_final)
```

```python
from functools import partial
from typing import NamedTuple

import jax
import jax.numpy as jnp
from jax import lax
from jax.experimental import pallas as pl
from jax.experimental.pallas import tpu as pltpu
from jax.experimental.pallas import tpu_sc as plsc

F32 = jnp.float32
BF16 = jnp.bfloat16
I32 = jnp.int32

N_HEADS = 4
HEAD_DIM = 128
D_MLSTM = N_HEADS * HEAD_DIM
D_CONV = 512
N_GROUPS = 4
EXPERTS_PER_GROUP = 8
N_EXPERTS = N_GROUPS * EXPERTS_PER_GROUP
D_FF = 512
RMS_EPS = 1e-6
Q_SCALE = HEAD_DIM ** -0.5

LANES = 128
CHUNK = LANES
AUG_ROWS = 16
ROUTER_COLS = 128
EXPERT_COL0 = 8

TM_INPROJ = 1024
TS_MIXER = 1024
OUT_COLS = 256
TM_EXPERT = 512
TC_COMBINE = 1024
N_PARTS = 2

VMEM_LIMIT = 56 * 1024 * 1024


HI_HALF = -65536


def _pack_rows(x):
    w = x.shape[1] // 2
    lo = lax.shift_right_logical(lax.bitcast_convert_type(x[:, :w].astype(BF16).astype(F32), I32), 16)
    hi = lax.bitcast_convert_type(x[:, w:].astype(BF16).astype(F32), I32) & HI_HALF
    return lo | hi


def _unpack_rows(words):
    lo = lax.bitcast_convert_type(words << 16, F32)
    hi = lax.bitcast_convert_type(words & HI_HALF, F32)
    return lo, hi


def _nt_dot(a, b):
    return lax.dot_general(a, b, (((1,), (1,)), ((), ())), preferred_element_type=F32)


def _inproj_kernel(x_ref, g_ref, wa_ref, ba_ref, wt_ref, bt_ref, a_ref, qvt_ref, gt_ref):
    x = x_ref[...]
    ms = jnp.mean(x * x, axis=-1, keepdims=True)
    hn = (x * lax.rsqrt(ms + RMS_EPS) * g_ref[...]).astype(BF16)
    n_blk = wa_ref.shape[1] // D_MLSTM
    for j in range(n_blk):
        sl = slice(j * D_MLSTM, (j + 1) * D_MLSTM)
        acc = jnp.dot(hn, wa_ref[:, sl], preferred_element_type=F32) + ba_ref[:, sl]
        a_ref[:, sl] = acc.astype(BF16)
    tr = _nt_dot(wt_ref[...], hn) + bt_ref[...]
    q_rows = slice(0, D_MLSTM)
    v_rows = slice(D_MLSTM, 2 * D_MLSTM)
    qvt_ref[q_rows, :] = (tr[q_rows, :] * Q_SCALE).astype(BF16)
    qvt_ref[v_rows, :] = tr[v_rows, :].astype(BF16)
    gt_ref[...] = tr[2 * D_MLSTM:2 * D_MLSTM + 2 * N_HEADS, :]


def _inproj(x, norm_mix, w_in, b_in):
    b_sz, s_len, d = x.shape
    tm = TM_INPROJ
    dm = D_MLSTM
    g0 = 4 * dm
    c0 = g0 + 2 * N_HEADS
    cols_a = jnp.concatenate([w_in[:, dm:2 * dm], w_in[:, 3 * dm:4 * dm], w_in[:, c0:c0 + 3 * D_CONV]], axis=1)
    bias_a = jnp.concatenate([b_in[dm:2 * dm], b_in[3 * dm:4 * dm], b_in[c0:c0 + 3 * D_CONV]])[None, :]
    n_gate = 2 * N_HEADS
    pad_rows = AUG_ROWS - n_gate
    w_t = jnp.concatenate([w_in[:, 0:dm], w_in[:, 2 * dm:3 * dm], w_in[:, g0:c0],
                           jnp.zeros((d, pad_rows), w_in.dtype)], axis=1).T
    b_t = jnp.concatenate([b_in[0:dm], b_in[2 * dm:3 * dm], b_in[g0:c0],
                           jnp.zeros((pad_rows,), b_in.dtype)])[:, None]
    n_t = 2 * dm + AUG_ROWS
    n_a = cols_a.shape[1]
    return pl.pallas_call(
        _inproj_kernel,
        grid=(b_sz, s_len // tm),
        in_specs=[
            pl.BlockSpec((None, tm, d), lambda b, s: (b, s, 0)),
            pl.BlockSpec((1, d), lambda b, s: (0, 0)),
            pl.BlockSpec((d, n_a), lambda b, s: (0, 0)),
            pl.BlockSpec((1, n_a), lambda b, s: (0, 0)),
            pl.BlockSpec((n_t, d), lambda b, s: (0, 0)),
            pl.BlockSpec((n_t, 1), lambda b, s: (0, 0)),
        ],
        out_specs=[
            pl.BlockSpec((None, tm, n_a), lambda b, s: (b, s, 0)),
            pl.BlockSpec((None, 2 * dm, tm), lambda b, s: (b, 0, s)),
            pl.BlockSpec((None, 2 * N_HEADS, tm), lambda b, s: (b, 0, s)),
        ],
        out_shape=[
            jax.ShapeDtypeStruct((b_sz, s_len, n_a), BF16),
            jax.ShapeDtypeStruct((b_sz, 2 * dm, s_len), BF16),
            jax.ShapeDtypeStruct((b_sz, 2 * N_HEADS, s_len), F32),
        ],
        compiler_params=pltpu.CompilerParams(
            dimension_semantics=("arbitrary", "arbitrary"), vmem_limit_bytes=VMEM_LIMIT),
        name="inproj",
    )(x, norm_mix[None, :], cols_a.astype(BF16), bias_a, w_t.astype(BF16), b_t)


def _mixer_kernel(k_ref, o_ref, cb_ref, cc_ref, cx_ref, qt_ref, vt_ref, gt_ref, x_ref,
                  cw_ref, wo_ref, g2_ref, wr_ref, br_ref,
                  x2_ref, hn2_ref, ri_ref, rw_ref, cnt_out_ref,
                  ct_ref, m_ref, carry_ref, cnt_ref, mix_ref):
    ts = x_ref.shape[0]
    n_chunks = ts // CHUNK
    first_tile = pl.program_id(1) == 0

    @pl.when(first_tile)
    def _():
        ct_ref[...] = jnp.zeros_like(ct_ref)
        m_ref[...] = jnp.zeros_like(m_ref)
        carry_ref[...] = jnp.zeros_like(carry_ref)

    @pl.when(jnp.logical_and(first_tile, pl.program_id(0) == 0))
    def _():
        cnt_ref[...] = jnp.zeros_like(cnt_ref)

    gates = gt_ref[...]
    li_all = gates[0:N_HEADS]
    fg = gates[N_HEADS:2 * N_HEADS]
    lf_all = jnp.minimum(fg, 0.0) - jnp.log1p(jnp.exp(-jnp.abs(fg)))
    key_j = lax.broadcasted_iota(I32, (CHUNK, CHUNK), 0)
    qry_i = lax.broadcasted_iota(I32, (CHUNK, CHUNK), 1)
    causal_t = key_j <= qry_i
    neg_inf = jnp.float32(-jnp.inf)
    ones_rows = (lax.broadcasted_iota(I32, (AUG_ROWS, CHUNK), 0) == 0).astype(BF16)

    prefix_ones = (key_j <= qry_i).astype(BF16)

    lf_hi = lf_all.astype(BF16)
    rem = lf_all - lf_hi.astype(F32)
    lf_mid = rem.astype(BF16)
    lf_lo = (rem - lf_mid.astype(F32)).astype(BF16)
    zero_bf = jnp.zeros((N_HEADS, CHUNK), BF16)
    parts = []
    for c in range(n_chunks):
        rows = slice(c * CHUNK, (c + 1) * CHUNK)
        parts += [lf_hi[:, rows], lf_mid[:, rows], lf_lo[:, rows], zero_bf]
    cum = jnp.dot(jnp.concatenate(parts, axis=0), prefix_ones, preferred_element_type=F32)

    m_prev = m_ref[0:N_HEADS, 0:1]
    for c in range(n_chunks):
        rows = slice(c * CHUNK, (c + 1) * CHUNK)
        li = li_all[:, rows]
        r0 = 4 * N_HEADS * c
        b_cum = (cum[r0:r0 + N_HEADS] + cum[r0 + N_HEADS:r0 + 2 * N_HEADS]
                 + cum[r0 + 2 * N_HEADS:r0 + 3 * N_HEADS])
        g_tot = b_cum[:, CHUNK - 1:CHUNK]
        a_end = g_tot - b_cum + li
        m_loc = jnp.max(a_end, axis=1, keepdims=True)
        u = li - b_cum
        m_new = jnp.maximum(g_tot + m_prev, m_loc)
        decay = jnp.exp(g_tot + m_prev - m_new)
        s_fac = jnp.exp(m_loc - m_new)
        w_key = jnp.exp(a_end - m_loc)
        u_cols = jnp.transpose(jnp.concatenate([u, jnp.zeros_like(u)], axis=0))

        def pair_row(x, h0):
            parts = [jnp.broadcast_to(x[h:h + 1, :], (1, CHUNK)) for h in (h0, h0 + 1)]
            return jnp.concatenate(parts, axis=1)

        def block_diag(a, b):
            z = jnp.zeros_like(a)
            return jnp.concatenate([jnp.concatenate([a, z], axis=1), jnp.concatenate([z, b], axis=1)], axis=0)

        for p in range(N_HEADS // 2):
            h0 = 2 * p
            pc = slice(h0 * HEAD_DIM, (h0 + 2) * HEAD_DIM)
            lo_l, hi_l = slice(0, CHUNK), slice(CHUNK, 2 * CHUNK)
            k2 = k_ref[rows, pc]
            q2 = qt_ref[pc, rows]
            v2 = vt_ref[pc, rows]
            s2 = jnp.dot(k2, block_diag(q2[0:HEAD_DIM], q2[HEAD_DIM:]), preferred_element_type=F32)
            u_t = jnp.concatenate([jnp.where(causal_t, u_cols[:, h:h + 1], neg_inf) for h in (h0, h0 + 1)],
                                  axis=1)
            mp = pair_row(m_prev, h0)
            m_tot = jnp.maximum(jnp.max(u_t, axis=0, keepdims=True), mp)
            f_inter = jnp.exp(mp - m_tot)
            floor = jnp.exp(-pair_row(b_cum, h0) - m_tot)
            p2 = (s2 * jnp.exp(u_t - m_tot)).astype(BF16)
            qf = [(q2[r].astype(F32) * f_inter[:, l]).astype(BF16)
                  for r, l in ((slice(0, HEAD_DIM), lo_l), (slice(HEAD_DIM, None), hi_l))]
            c_old = ct_ref[p]
            cb16 = c_old.astype(BF16)
            vt_aug = [jnp.concatenate([v2[r], ones_rows], axis=0)
                      for r in (slice(0, HEAD_DIM), slice(HEAD_DIM, None))]
            lhs = jnp.concatenate([vt_aug[0], cb16[:, lo_l], vt_aug[1], cb16[:, hi_l]], axis=1)
            rhs = block_diag(jnp.concatenate([p2[:, lo_l], qf[0]], axis=0),
                             jnp.concatenate([p2[:, hi_l], qf[1]], axis=0))
            r2 = jnp.dot(lhs, rhs, preferred_element_type=F32)
            w2 = pair_row(w_key, h0)
            vtw = jnp.concatenate([(vt_aug[0].astype(F32) * w2[:, lo_l]).astype(BF16),
                                   (vt_aug[1].astype(F32) * w2[:, hi_l]).astype(BF16)], axis=1)
            kv = jnp.dot(vtw, block_diag(k2[:, lo_l], k2[:, hi_l]), preferred_element_type=F32)
            ct_ref[p] = pair_row(decay, h0) * c_old + pair_row(s_fac, h0) * kv

            den = r2[HEAD_DIM:HEAD_DIM + 1, :]
            inv = 1.0 / jnp.maximum(jnp.abs(den), floor)
            h_t = r2[0:HEAD_DIM, :] * inv
            for h, l in ((h0, lo_l), (h0 + 1, hi_l)):
                hc = slice(h * HEAD_DIM, (h + 1) * HEAD_DIM)
                o_gate = jax.nn.sigmoid(o_ref[rows, hc].astype(F32))
                mix_ref[rows, hc] = (o_gate * jnp.transpose(h_t[:, l])).astype(BF16)
        m_prev = m_new
    m_ref[0:N_HEADS, :] = jnp.broadcast_to(m_prev, (N_HEADS, LANES))

    u_c = cc_ref[...].astype(F32) * cx_ref[...].astype(F32)
    rid = lax.broadcasted_iota(I32, u_c.shape, 0)
    prev1 = carry_ref[0:1, :]
    prev2 = carry_ref[1:2, :]
    u1 = jnp.where(rid == 0, prev1, pltpu.roll(u_c, 1, axis=0))
    u2 = jnp.where(rid == 0, prev2, jnp.where(rid == 1, prev1, pltpu.roll(u_c, 2, axis=0)))
    z = cw_ref[0:1, :] * u2 + cw_ref[1:2, :] * u1 + cw_ref[2:3, :] * u_c
    mix_ref[:, D_MLSTM:] = (cb_ref[...].astype(F32) * z).astype(BF16)
    carry_ref[0:1, :] = u_c[ts - 1:ts, :]
    carry_ref[1:2, :] = u_c[ts - 2:ts - 1, :]

    d_model = x_ref.shape[1]
    mix = mix_ref[...]
    ssq = jnp.zeros((ts, 1), F32)
    for j in range(d_model // OUT_COLS):
        cs = slice(j * OUT_COLS, (j + 1) * OUT_COLS)
        x2c = x_ref[:, cs] + jnp.dot(mix, wo_ref[:, cs], preferred_element_type=F32)
        x2_ref[:, cs] = x2c
        ssq = ssq + jnp.sum(x2c * x2c, axis=-1, keepdims=True)
    hn2 = x2_ref[...] * lax.rsqrt(ssq * (1.0 / d_model) + RMS_EPS) * g2_ref[...]
    hn2_ref[...] = _pack_rows(hn2)

    logits = jnp.dot(hn2.astype(BF16), wr_ref[...], preferred_element_type=F32) + br_ref[...]
    lt = jnp.transpose(logits)
    gl = lt[0:N_GROUPS]
    gmax = jnp.max(gl, axis=0, keepdims=True)
    gi = lax.broadcasted_iota(I32, gl.shape, 0)
    g_sel = jnp.min(jnp.where(gl == gmax, gi, N_GROUPS), axis=0, keepdims=True)
    p_g = 1.0 / jnp.sum(jnp.exp(gl - gmax), axis=0, keepdims=True)
    in_g = lt[EXPERT_COL0:EXPERT_COL0 + EXPERTS_PER_GROUP]
    for g in range(1, N_GROUPS):
        lo = EXPERT_COL0 + g * EXPERTS_PER_GROUP
        in_g = jnp.where(g_sel == g, lt[lo:lo + EXPERTS_PER_GROUP], in_g)
    ei = lax.broadcasted_iota(I32, in_g.shape, 0)
    v1 = jnp.max(in_g, axis=0, keepdims=True)
    i1 = jnp.min(jnp.where(in_g == v1, ei, EXPERTS_PER_GROUP), axis=0, keepdims=True)
    rest = jnp.where(ei == i1, neg_inf, in_g)
    v2 = jnp.max(rest, axis=0, keepdims=True)
    i2 = jnp.min(jnp.where(rest == v2, ei, EXPERTS_PER_GROUP), axis=0, keepdims=True)
    t = jnp.exp(v2 - v1)
    w1 = p_g / (1.0 + t)
    w2 = p_g * t / (1.0 + t)
    e1 = g_sel * EXPERTS_PER_GROUP + i1
    e2 = g_sel * EXPERTS_PER_GROUP + i2

    eio = lax.broadcasted_iota(I32, (N_EXPERTS, LANES), 0)
    base = cnt_ref[:, 0:1]
    zero_i = jnp.zeros((1, LANES), I32)
    n_blk = ts // LANES
    hits = []
    for k in range(n_blk):
        ls = slice(k * LANES, (k + 1) * LANES)
        hits.append((eio == e1[:, ls], eio == e2[:, ls]))
    onehots = [jnp.where(jnp.logical_or(a1, a2), 1.0, 0.0) for a1, a2 in hits]
    incl_all = jnp.dot(jnp.concatenate(onehots, axis=0).astype(BF16), prefix_ones,
                       preferred_element_type=F32).astype(I32)
    for k in range(n_blk):
        ls = slice(k * LANES, (k + 1) * LANES)
        is1, is2 = hits[k]
        onehot = onehots[k].astype(I32)
        incl = incl_all[k * N_EXPERTS:(k + 1) * N_EXPERTS]
        rank_e = base + incl - onehot
        r1 = jnp.sum(jnp.where(is1, rank_e, 0), axis=0, keepdims=True)
        r2 = jnp.sum(jnp.where(is2, rank_e, 0), axis=0, keepdims=True)
        ri_ref[:, ls] = jnp.concatenate(
            [e1[:, ls], e2[:, ls], r1, r2, zero_i, zero_i, zero_i, zero_i], axis=0)
        base = base + incl[:, LANES - 1:LANES]
    zero_f = jnp.zeros_like(w1)
    rw_ref[...] = jnp.concatenate([w1, w2, zero_f, zero_f, zero_f, zero_f, zero_f, zero_f], axis=0)
    cnt_new = jnp.broadcast_to(base, (N_EXPERTS, LANES))
    cnt_ref[...] = cnt_new
    cnt_out_ref[...] = cnt_new


def _mixer(x, a, qvt, gt, conv_w, w_out, norm_ffn, w_router, b_router, b0, b_sz):
    _, s_len, d = x.shape
    ts = TS_MIXER
    dm = D_MLSTM

    def a_spec(j):
        return pl.BlockSpec((None, ts, dm), lambda b, s, j=j: (b + b0, s, j))

    tok_in = pl.BlockSpec((None, ts, d), lambda b, s: (b + b0, s, 0))
    row_in = pl.BlockSpec((None, 2 * N_HEADS, ts), lambda b, s: (b + b0, 0, s))
    tok = lambda w: pl.BlockSpec((None, ts, w), lambda b, s: (b, s, 0))
    rowsp = pl.BlockSpec((None, 2 * N_HEADS, ts), lambda b, s: (b, 0, s))
    const = lambda shape: pl.BlockSpec(shape, lambda b, s: tuple(0 for _ in shape))
    return pl.pallas_call(
        _mixer_kernel,
        grid=(b_sz, s_len // ts),
        in_specs=[a_spec(0), a_spec(1), a_spec(2), a_spec(3), a_spec(4),
                  pl.BlockSpec((None, dm, ts), lambda b, s: (b + b0, 0, s)),
                  pl.BlockSpec((None, dm, ts), lambda b, s: (b + b0, 1, s)),
                  row_in, tok_in,
                  const((3, D_CONV)), const((d, d)), const((1, d)),
                  const((d, ROUTER_COLS)), const((1, ROUTER_COLS))],
        out_specs=[tok(d), tok(d // 2), rowsp, rowsp, const((N_EXPERTS, LANES))],
        out_shape=[
            jax.ShapeDtypeStruct((b_sz, s_len, d), F32),
            jax.ShapeDtypeStruct((b_sz, s_len, d // 2), I32),
            jax.ShapeDtypeStruct((b_sz, 2 * N_HEADS, s_len), I32),
            jax.ShapeDtypeStruct((b_sz, 2 * N_HEADS, s_len), F32),
            jax.ShapeDtypeStruct((N_EXPERTS, LANES), I32),
        ],
        scratch_shapes=[
            pltpu.VMEM((N_HEADS // 2, HEAD_DIM + AUG_ROWS, 2 * HEAD_DIM), F32),
            pltpu.VMEM((2 * N_HEADS, LANES), F32),
            pltpu.VMEM((8, D_CONV), F32),
            pltpu.VMEM((N_EXPERTS, LANES), I32),
            pltpu.VMEM((ts, d), BF16),
        ],
        compiler_params=pltpu.CompilerParams(
            dimension_semantics=("arbitrary", "arbitrary"), vmem_limit_bytes=VMEM_LIMIT),
        name="mixer",
    )(a, a, a, a, a, qvt, qvt, gt, x, conv_w, w_out, norm_ffn[None, :], w_router, b_router)


SC_CORES = 2
SC_SUBCORES = 16
SC_WORKERS = SC_CORES * SC_SUBCORES
SC_CHUNK = 64


def _sc_worker_id():
    return lax.axis_index("s") * SC_CORES + lax.axis_index("c")


def _sc_scatter_rows(rows, idx, n_out):
    n_rows, width = rows.shape
    per_worker = n_rows // SC_WORKERS
    n_chunks = per_worker // SC_CHUNK
    assert n_chunks % 2 == 0
    mesh = plsc.VectorSubcoreMesh(core_axis_name="c", subcore_axis_name="s")

    def body(rows_hbm, idx_hbm, out_hbm, ia0, ib0, r0, ia1, ib1, r1, ls0, ls1, ss0, ss1):
        wid = _sc_worker_id()
        bufs = ((ia0, ib0, r0, ls0, ss0), (ia1, ib1, r1, ls1, ss1))

        def loads(j, b):
            ia, ib, r, ls, _ = bufs[b]
            base = pl.multiple_of(wid * per_worker + j * SC_CHUNK, SC_CHUNK)
            return (pltpu.make_async_copy(rows_hbm.at[pl.ds(base, SC_CHUNK)], r, ls),
                    pltpu.make_async_copy(idx_hbm.at[pl.ds(base, SC_CHUNK)], ia, ls),
                    pltpu.make_async_copy(idx_hbm.at[pl.ds(n_rows + base, SC_CHUNK)], ib, ls))

        def scatters(b):
            ia, ib, r, _, ss = bufs[b]
            return (pltpu.make_async_copy(r, out_hbm.at[ia], ss),
                    pltpu.make_async_copy(r, out_hbm.at[ib], ss))

        for c in loads(0, 0):
            c.start()

        @pl.loop(0, n_chunks, step=2)
        def _(j0):
            for b in range(2):
                j = j0 + b
                for c in loads(j, b):
                    c.wait()

                @pl.when(j >= 1)
                def _():
                    for c in scatters(1 - b):
                        c.wait()

                @pl.when(j + 1 < n_chunks)
                def _():
                    for c in loads(j + 1, 1 - b):
                        c.start()

                for c in scatters(b):
                    c.start()

        for c in scatters((n_chunks - 1) % 2):
            c.wait()

    idx_t = pltpu.VMEM((SC_CHUNK,), I32)
    row_t = pltpu.VMEM((SC_CHUNK, width), I32)
    sem_t = pltpu.SemaphoreType.DMA
    return pl.kernel(
        body,
        out_type=jax.ShapeDtypeStruct((n_out, width), I32),
        mesh=mesh,
        scratch_types=[idx_t, idx_t, row_t, idx_t, idx_t, row_t, sem_t, sem_t, sem_t, sem_t],
        name="sc_scatter",
    )(rows, idx)


def _sc_gather_rows(table, idx):
    n_rows = idx.shape[0]
    width = table.shape[1]
    per_worker = n_rows // SC_WORKERS
    n_chunks = per_worker // SC_CHUNK
    assert n_chunks % 2 == 0
    mesh = plsc.VectorSubcoreMesh(core_axis_name="c", subcore_axis_name="s")

    def body(table_hbm, idx_hbm, out_hbm, i0, r0, i1, r1, is0, is1, gs0, gs1, ws0, ws1):
        wid = _sc_worker_id()
        bufs = ((i0, r0, is0, gs0, ws0), (i1, r1, is1, gs1, ws1))

        def chunk_base(j):
            return pl.multiple_of(wid * per_worker + j * SC_CHUNK, SC_CHUNK)

        def idx_load(j, b):
            return pltpu.make_async_copy(idx_hbm.at[pl.ds(chunk_base(j), SC_CHUNK)], bufs[b][0], bufs[b][2])

        def gather(b):
            return pltpu.make_async_copy(table_hbm.at[bufs[b][0]], bufs[b][1], bufs[b][3])

        def writeback(j, b):
            return pltpu.make_async_copy(bufs[b][1], out_hbm.at[pl.ds(chunk_base(j), SC_CHUNK)], bufs[b][4])

        idx_load(0, 0).start()
        idx_load(0, 0).wait()
        gather(0).start()
        idx_load(1, 1).start()

        @pl.loop(0, n_chunks, step=2)
        def _(j0):
            for b in range(2):
                j = j0 + b

                @pl.when(j + 1 < n_chunks)
                def _():
                    idx_load(j + 1, 1 - b).wait()

                    @pl.when(j >= 1)
                    def _():
                        writeback(j - 1, 1 - b).wait()
                    gather(1 - b).start()

                gather(b).wait()
                writeback(j, b).start()

                @pl.when(j + 2 < n_chunks)
                def _():
                    idx_load(j + 2, b).start()

        writeback(n_chunks - 2, 0).wait()
        writeback(n_chunks - 1, 1).wait()

    idx_t = pltpu.VMEM((SC_CHUNK,), I32)
    row_t = pltpu.VMEM((SC_CHUNK, width), I32)
    sem_t = pltpu.SemaphoreType.DMA
    return pl.kernel(
        body,
        out_type=jax.ShapeDtypeStruct((n_rows, width), I32),
        mesh=mesh,
        scratch_types=[idx_t, row_t, idx_t, row_t, sem_t, sem_t, sem_t, sem_t, sem_t, sem_t],
        name="sc_gather",
    )(table, idx)


def _experts_kernel(te_ref, nu_ref, tv_ref, ne_ref, xs_ref, wg_hbm, wu_hbm, wd_hbm, y_ref,
                    wgs_ref, wus_ref, wds_ref, wgb_ref, wub_ref, wdb_ref, wsem):
    i = pl.program_id(0)
    e = te_ref[i]
    e_prev = te_ref[jnp.maximum(i - 1, 0)]
    new_expert = jnp.logical_or(i == 0, e != e_prev)
    used = i < nu_ref[0]

    def weight_copies(expert):
        return (pltpu.make_async_copy(wg_hbm.at[expert], wgs_ref, wsem),
                pltpu.make_async_copy(wu_hbm.at[expert], wus_ref, wsem),
                pltpu.make_async_copy(wd_hbm.at[expert], wds_ref, wsem))

    @pl.when(i == 0)
    def _():
        for c in weight_copies(e):
            c.start()

    @pl.when(jnp.logical_and(used, new_expert))
    def _():
        for c in weight_copies(e):
            c.wait()
        wgb_ref[...] = wgs_ref[...].astype(BF16)
        wub_ref[...] = wus_ref[...].astype(BF16)
        wdb_ref[...] = wds_ref[...].astype(BF16)

        @pl.when(ne_ref[i] >= 0)
        def _():
            for c in weight_copies(ne_ref[i]):
                c.start()

    @pl.when(used)
    def _():
        row = lax.broadcasted_iota(I32, xs_ref.shape, 0)
        x_lo, x_hi = _unpack_rows(jnp.where(row < tv_ref[i], xs_ref[...], 0))
        xb = jnp.concatenate([x_lo.astype(BF16), x_hi.astype(BF16)], axis=1)
        gate = jnp.dot(xb, wgb_ref[...], preferred_element_type=F32)
        up = jnp.dot(xb, wub_ref[...], preferred_element_type=F32)
        hid = (gate * jax.nn.sigmoid(gate) * up).astype(BF16)
        y_ref[...] = _pack_rows(jnp.dot(hid, wdb_ref[...], preferred_element_type=F32))

    @pl.when(jnp.logical_not(used))
    def _():
        y_ref[...] = jnp.zeros_like(y_ref)


def _experts(xs, tile_e, n_used, tile_valid, next_e, w_gate, w_up, w_down):
    cap, dw = xs.shape
    d = 2 * dw
    tm = TM_EXPERT
    n_tiles = cap // tm

    def x_map(i, te, nu, tv, ne):
        return (jnp.minimum(i, jnp.maximum(nu[0] - 1, 0)), 0)

    return pl.pallas_call(
        _experts_kernel,
        grid_spec=pltpu.PrefetchScalarGridSpec(
            num_scalar_prefetch=4,
            grid=(n_tiles,),
            in_specs=[
                pl.BlockSpec((tm, dw), x_map),
                pl.BlockSpec(memory_space=pl.ANY),
                pl.BlockSpec(memory_space=pl.ANY),
                pl.BlockSpec(memory_space=pl.ANY),
            ],
            out_specs=pl.BlockSpec((tm, dw), lambda i, te, nu, tv, ne: (i, 0)),
            scratch_shapes=[pltpu.VMEM((d, D_FF), F32), pltpu.VMEM((d, D_FF), F32), pltpu.VMEM((D_FF, d), F32),
                            pltpu.VMEM((d, D_FF), BF16), pltpu.VMEM((d, D_FF), BF16), pltpu.VMEM((D_FF, d), BF16),
                            pltpu.SemaphoreType.DMA(())],
        ),
        out_shape=jax.ShapeDtypeStruct((cap, dw), I32),
        compiler_params=pltpu.CompilerParams(
            dimension_semantics=("arbitrary",), vmem_limit_bytes=VMEM_LIMIT),
        name="experts",
    )(tile_e, n_used, tile_valid, next_e, xs, w_gate, w_up, w_down)


def _combine_stream_kernel(*refs, part_steps):
    gf_ref, out_ref = refs[-2:]
    for p in range(len(refs) // 4):
        x2_ref, y1_ref, y2_ref, rw_ref = refs[4 * p:4 * p + 4]

        @pl.when(pl.program_id(0) // part_steps == p)
        def _():
            w_t = jnp.transpose(rw_ref[...])
            a_lo, a_hi = _unpack_rows(y1_ref[...])
            b_lo, b_hi = _unpack_rows(y2_ref[...])
            w1 = w_t[:, 0:1]
            w2 = w_t[:, 1:2]
            moe = jnp.concatenate([w1 * a_lo + w2 * b_lo, w1 * a_hi + w2 * b_hi], axis=1)
            x3 = x2_ref[...] + moe
            ms = jnp.mean(x3 * x3, axis=-1, keepdims=True)
            out_ref[...] = x3 * lax.rsqrt(ms + RMS_EPS) * gf_ref[...]


def _combine_stream(parts, norm_final):
    b_part, s_len, d = parts[0][0].shape
    tc = TC_COMBINE
    n_s = s_len // tc
    part_steps = b_part * n_s
    n_steps = part_steps * len(parts)
    in_specs, args = [], []
    for p, (x2, yg, rw) in enumerate(parts):
        def local(g, p=p):
            return jnp.clip(g - p * part_steps, 0, part_steps - 1)

        in_specs += [
            pl.BlockSpec((tc, d), lambda g, local=local: (local(g), 0)),
            pl.BlockSpec((tc, d // 2), lambda g, local=local: (local(g), 0)),
            pl.BlockSpec((tc, d // 2), lambda g, local=local: (local(g) + part_steps, 0)),
            pl.BlockSpec((None, 2 * N_HEADS, tc), lambda g, local=local: (local(g) // n_s, 0, local(g) % n_s)),
        ]
        args += [x2.reshape(b_part * s_len, d), yg, yg, rw]
    in_specs.append(pl.BlockSpec((1, d), lambda g: (0, 0)))
    return pl.pallas_call(
        partial(_combine_stream_kernel, part_steps=part_steps),
        grid=(n_steps,),
        in_specs=in_specs,
        out_specs=pl.BlockSpec((tc, d), lambda g: (g, 0)),
        out_shape=jax.ShapeDtypeStruct((n_steps * tc, d), F32),
        compiler_params=pltpu.CompilerParams(
            dimension_semantics=("arbitrary",), vmem_limit_bytes=VMEM_LIMIT),
        name="combine_stream",
    )(*args, norm_final[None, :]).reshape(len(parts) * b_part, s_len, d)


class _Plan(NamedTuple):
    pos_flat: jax.Array
    tile_e: jax.Array
    n_used: jax.Array
    tile_valid: jax.Array
    next_e: jax.Array
    cap: int


def _dispatch_plan(ri, counts, n_tok):
    tm = TM_EXPERT
    cnt = counts[:, 0]
    padded = ((cnt + tm - 1) // tm) * tm
    pends = jnp.cumsum(padded).astype(I32)
    seg = jnp.concatenate([jnp.zeros((1,), I32), pends])
    n_tiles = (n_tok * 2) // tm + N_EXPERTS
    cap = n_tiles * tm
    tile_start = jnp.arange(n_tiles, dtype=I32) * tm
    tile_e = jnp.minimum(jnp.sum((tile_start[:, None] >= pends[None, :]).astype(I32), axis=1),
                         N_EXPERTS - 1)
    n_used = pends[-1:] // tm
    e_sel = ri[:, 0:2, :]
    seg_start = jnp.sum(jnp.where(e_sel[..., None] == jnp.arange(N_EXPERTS, dtype=I32), seg[:-1], 0), axis=-1)
    pos = ri[:, 2:4, :] + seg_start

    seg_tok_end = seg[:-1] + cnt.astype(I32)
    tile_tok_end = jnp.sum(jnp.where(tile_e[:, None] == jnp.arange(N_EXPERTS, dtype=I32), seg_tok_end, 0), axis=-1)
    tile_valid = jnp.clip(tile_tok_end - tile_start, 0, tm).astype(I32)
    pos_flat = jnp.concatenate([pos[:, 0, :].reshape(-1), pos[:, 1, :].reshape(-1)])

    expert_ids = jnp.arange(N_EXPERTS, dtype=I32)
    seg_end_tile = jnp.sum(jnp.where(tile_e[:, None] == expert_ids, pends // tm, 0), axis=-1)
    tile_ids = jnp.arange(n_tiles, dtype=I32)
    e_after = jnp.sum(jnp.where(seg_end_tile[:, None] == tile_ids, tile_e, 0), axis=-1)
    next_e = jnp.where(seg_end_tile < n_used[0], e_after, -1).astype(I32)
    return _Plan(pos_flat, tile_e, n_used, tile_valid, next_e, cap)


def _layer(x, norm_mix, w_in, b_in, conv_w, w_out, norm_ffn, w_group, b_group, w_expert, b_expert,
           w_gate, w_up, w_down, norm_out):
    b_sz, s_len, d = x.shape
    a, qvt, gt = _inproj(x, norm_mix, w_in, b_in)

    pad_g = EXPERT_COL0 - N_GROUPS
    pad_e = ROUTER_COLS - EXPERT_COL0 - N_EXPERTS
    w_router = jnp.concatenate([w_group, jnp.zeros((d, pad_g), F32), w_expert, jnp.zeros((d, pad_e), F32)], axis=1)
    b_router = jnp.concatenate([b_group, jnp.zeros((pad_g,), F32), b_expert, jnp.zeros((pad_e,), F32)])[None, :]
    w_out_b = w_out.astype(BF16)
    w_router_b = w_router.astype(BF16)

    b_part = b_sz // N_PARTS
    n_part = b_part * s_len
    parts = []
    for p in range(N_PARTS):
        x2, hn2, ri, rw, counts = _mixer(x, a, qvt, gt, conv_w, w_out_b, norm_ffn, w_router_b, b_router,
                                         p * b_part, b_part)
        plan = _dispatch_plan(ri, counts, n_part)
        xs = _sc_scatter_rows(hn2.reshape(n_part, d // 2), plan.pos_flat, plan.cap)
        parts.append((x2, rw, plan, xs))
    ys = [_experts(xs, plan.tile_e, plan.n_used, plan.tile_valid, plan.next_e, w_gate, w_up, w_down)
          for _, _, plan, xs in parts]
    ygs = [_sc_gather_rows(y, plan.pos_flat) for y, (_, _, plan, _) in zip(ys, parts)]
    return _combine_stream([(x2, yg, rw) for (x2, rw, _, _), yg in zip(parts, ygs)], norm_out)


def kernel(x, norm_mix, w_in, b_in, conv_w, w_out, norm_ffn, w_group, b_group, w_expert, b_expert,
           w_gate, w_up, w_down, norm_final):
    depth = norm_mix.shape[0]
    assert depth == 1, "final-norm fusion below assumes a single layer"
    assert x.shape[-1] == 2 * D_MLSTM
    assert x.shape[0] % N_PARTS == 0
    assert all(x.shape[1] % t == 0 for t in (TS_MIXER, TM_INPROJ, TC_COMBINE))
    return _layer(x, norm_mix[0], w_in[0], b_in[0], conv_w[0], w_out[0], norm_ffn[0],
                  w_group[0], b_group[0], w_expert[0], b_expert[0],
                  w_gate[0], w_up[0], w_down[0], norm_final)
```

```python
import jax
import jax.numpy as jnp
from jax import lax
from jax.experimental import pallas as pl
from jax.experimental.pallas import tpu as pltpu
from jax.experimental.pallas import tpu_sc as plsc

F32 = jnp.float32
BF16 = jnp.bfloat16
I32 = jnp.int32

N_HEADS = 4
HEAD_DIM = 128
D_MLSTM = N_HEADS * HEAD_DIM
D_CONV = 512
N_GROUPS = 4
EXPERTS_PER_GROUP = 8
N_EXPERTS = N_GROUPS * EXPERTS_PER_GROUP
D_FF = 512
RMS_EPS = 1e-6
Q_SCALE = HEAD_DIM ** -0.5

LANES = 128
CHUNK = LANES
AUG_ROWS = 16
ROUTER_COLS = 128
EXPERT_COL0 = 8

TM_INPROJ = 1024
TS_MIXER = 1024
OUT_COLS = 256
TM_EXPERT = 512
TC_COMBINE = 1024

VMEM_LIMIT = 56 * 1024 * 1024


HI_HALF = -65536


def _pack_rows(x):
    w = x.shape[1] // 2
    lo = lax.shift_right_logical(lax.bitcast_convert_type(x[:, :w].astype(BF16).astype(F32), I32), 16)
    hi = lax.bitcast_convert_type(x[:, w:].astype(BF16).astype(F32), I32) & HI_HALF
    return lo | hi


def _unpack_rows(words):
    lo = lax.bitcast_convert_type(words << 16, F32)
    hi = lax.bitcast_convert_type(words & HI_HALF, F32)
    return lo, hi


def _nt_dot(a, b):
    return lax.dot_general(a, b, (((1,), (1,)), ((), ())), preferred_element_type=F32)


def _inproj_kernel(x_ref, g_ref, wa_ref, ba_ref, wt_ref, bt_ref, a_ref, qvt_ref, gt_ref):
    x = x_ref[...]
    ms = jnp.mean(x * x, axis=-1, keepdims=True)
    hn = (x * lax.rsqrt(ms + RMS_EPS) * g_ref[...]).astype(BF16)
    n_blk = wa_ref.shape[1] // D_MLSTM
    for j in range(n_blk):
        sl = slice(j * D_MLSTM, (j + 1) * D_MLSTM)
        acc = jnp.dot(hn, wa_ref[:, sl], preferred_element_type=F32) + ba_ref[:, sl]
        a_ref[:, sl] = acc.astype(BF16)
    tr = _nt_dot(wt_ref[...], hn) + bt_ref[...]
    q_rows = slice(0, D_MLSTM)
    v_rows = slice(D_MLSTM, 2 * D_MLSTM)
    qvt_ref[q_rows, :] = (tr[q_rows, :] * Q_SCALE).astype(BF16)
    qvt_ref[v_rows, :] = tr[v_rows, :].astype(BF16)
    gt_ref[...] = tr[2 * D_MLSTM:2 * D_MLSTM + 2 * N_HEADS, :]


def _inproj(x, norm_mix, w_in, b_in):
    b_sz, s_len, d = x.shape
    tm = TM_INPROJ
    dm = D_MLSTM
    g0 = 4 * dm
    c0 = g0 + 2 * N_HEADS
    cols_a = jnp.concatenate([w_in[:, dm:2 * dm], w_in[:, 3 * dm:4 * dm], w_in[:, c0:c0 + 3 * D_CONV]], axis=1)
    bias_a = jnp.concatenate([b_in[dm:2 * dm], b_in[3 * dm:4 * dm], b_in[c0:c0 + 3 * D_CONV]])[None, :]
    n_gate = 2 * N_HEADS
    pad_rows = AUG_ROWS - n_gate
    w_t = jnp.concatenate([w_in[:, 0:dm], w_in[:, 2 * dm:3 * dm], w_in[:, g0:c0],
                           jnp.zeros((d, pad_rows), w_in.dtype)], axis=1).T
    b_t = jnp.concatenate([b_in[0:dm], b_in[2 * dm:3 * dm], b_in[g0:c0],
                           jnp.zeros((pad_rows,), b_in.dtype)])[:, None]
    n_t = 2 * dm + AUG_ROWS
    n_a = cols_a.shape[1]
    return pl.pallas_call(
        _inproj_kernel,
        grid=(b_sz, s_len // tm),
        in_specs=[
            pl.BlockSpec((None, tm, d), lambda b, s: (b, s, 0)),
            pl.BlockSpec((1, d), lambda b, s: (0, 0)),
            pl.BlockSpec((d, n_a), lambda b, s: (0, 0)),
            pl.BlockSpec((1, n_a), lambda b, s: (0, 0)),
            pl.BlockSpec((n_t, d), lambda b, s: (0, 0)),
            pl.BlockSpec((n_t, 1), lambda b, s: (0, 0)),
        ],
        out_specs=[
            pl.BlockSpec((None, tm, n_a), lambda b, s: (b, s, 0)),
            pl.BlockSpec((None, 2 * dm, tm), lambda b, s: (b, 0, s)),
            pl.BlockSpec((None, 2 * N_HEADS, tm), lambda b, s: (b, 0, s)),
        ],
        out_shape=[
            jax.ShapeDtypeStruct((b_sz, s_len, n_a), BF16),
            jax.ShapeDtypeStruct((b_sz, 2 * dm, s_len), BF16),
            jax.ShapeDtypeStruct((b_sz, 2 * N_HEADS, s_len), F32),
        ],
        compiler_params=pltpu.CompilerParams(
            dimension_semantics=("arbitrary", "arbitrary"), vmem_limit_bytes=VMEM_LIMIT),
        name="inproj",
    )(x, norm_mix[None, :], cols_a.astype(BF16), bias_a, w_t.astype(BF16), b_t)


def _mixer_kernel(k_ref, o_ref, cb_ref, cc_ref, cx_ref, qt_ref, vt_ref, gt_ref, x_ref,
                  cw_ref, wo_ref, g2_ref, wr_ref, br_ref,
                  x2_ref, hn2_ref, ri_ref, rw_ref, cnt_out_ref,
                  ct_ref, m_ref, carry_ref, cnt_ref, mix_ref):
    ts = x_ref.shape[0]
    n_chunks = ts // CHUNK
    first_tile = pl.program_id(1) == 0

    @pl.when(first_tile)
    def _():
        ct_ref[...] = jnp.zeros_like(ct_ref)
        m_ref[...] = jnp.zeros_like(m_ref)
        carry_ref[...] = jnp.zeros_like(carry_ref)

    @pl.when(jnp.logical_and(first_tile, pl.program_id(0) == 0))
    def _():
        cnt_ref[...] = jnp.zeros_like(cnt_ref)

    gates = gt_ref[...]
    li_all = gates[0:N_HEADS]
    fg = gates[N_HEADS:2 * N_HEADS]
    lf_all = jnp.minimum(fg, 0.0) - jnp.log1p(jnp.exp(-jnp.abs(fg)))
    key_j = lax.broadcasted_iota(I32, (CHUNK, CHUNK), 0)
    qry_i = lax.broadcasted_iota(I32, (CHUNK, CHUNK), 1)
    causal_t = key_j <= qry_i
    neg_inf = jnp.float32(-jnp.inf)
    ones_rows = (lax.broadcasted_iota(I32, (AUG_ROWS, CHUNK), 0) == 0).astype(BF16)

    prefix_ones = (key_j <= qry_i).astype(BF16)

    lf_hi = lf_all.astype(BF16)
    rem = lf_all - lf_hi.astype(F32)
    lf_mid = rem.astype(BF16)
    lf_lo = (rem - lf_mid.astype(F32)).astype(BF16)
    zero_bf = jnp.zeros((N_HEADS, CHUNK), BF16)
    parts = []
    for c in range(n_chunks):
        rows = slice(c * CHUNK, (c + 1) * CHUNK)
        parts += [lf_hi[:, rows], lf_mid[:, rows], lf_lo[:, rows], zero_bf]
    cum = jnp.dot(jnp.concatenate(parts, axis=0), prefix_ones, preferred_element_type=F32)

    m_prev = m_ref[0:N_HEADS, 0:1]
    for c in range(n_chunks):
        rows = slice(c * CHUNK, (c + 1) * CHUNK)
        li = li_all[:, rows]
        r0 = 4 * N_HEADS * c
        b_cum = (cum[r0:r0 + N_HEADS] + cum[r0 + N_HEADS:r0 + 2 * N_HEADS]
                 + cum[r0 + 2 * N_HEADS:r0 + 3 * N_HEADS])
        g_tot = b_cum[:, CHUNK - 1:CHUNK]
        a_end = g_tot - b_cum + li
        m_loc = jnp.max(a_end, axis=1, keepdims=True)
        u = li - b_cum
        m_new = jnp.maximum(g_tot + m_prev, m_loc)
        decay = jnp.exp(g_tot + m_prev - m_new)
        s_fac = jnp.exp(m_loc - m_new)
        w_key = jnp.exp(a_end - m_loc)
        u_cols = jnp.transpose(jnp.concatenate([u, jnp.zeros_like(u)], axis=0))

        def pair_row(x, h0):
            parts = [jnp.broadcast_to(x[h:h + 1, :], (1, CHUNK)) for h in (h0, h0 + 1)]
            return jnp.concatenate(parts, axis=1)

        def block_diag(a, b):
            z = jnp.zeros_like(a)
            return jnp.concatenate([jnp.concatenate([a, z], axis=1), jnp.concatenate([z, b], axis=1)], axis=0)

        for p in range(N_HEADS // 2):
            h0 = 2 * p
            pc = slice(h0 * HEAD_DIM, (h0 + 2) * HEAD_DIM)
            lo_l, hi_l = slice(0, CHUNK), slice(CHUNK, 2 * CHUNK)
            k2 = k_ref[rows, pc]
            q2 = qt_ref[pc, rows]
            v2 = vt_ref[pc, rows]
            s2 = jnp.dot(k2, block_diag(q2[0:HEAD_DIM], q2[HEAD_DIM:]), preferred_element_type=F32)
            u_t = jnp.concatenate([jnp.where(causal_t, u_cols[:, h:h + 1], neg_inf) for h in (h0, h0 + 1)],
                                  axis=1)
            mp = pair_row(m_prev, h0)
            m_tot = jnp.maximum(jnp.max(u_t, axis=0, keepdims=True), mp)
            f_inter = jnp.exp(mp - m_tot)
            floor = jnp.exp(-pair_row(b_cum, h0) - m_tot)
            p2 = (s2 * jnp.exp(u_t - m_tot)).astype(BF16)
            qf = [(q2[r].astype(F32) * f_inter[:, l]).astype(BF16)
                  for r, l in ((slice(0, HEAD_DIM), lo_l), (slice(HEAD_DIM, None), hi_l))]
            c_old = ct_ref[p]
            cb16 = c_old.astype(BF16)
            vt_aug = [jnp.concatenate([v2[r], ones_rows], axis=0)
                      for r in (slice(0, HEAD_DIM), slice(HEAD_DIM, None))]
            lhs = jnp.concatenate([vt_aug[0], cb16[:, lo_l], vt_aug[1], cb16[:, hi_l]], axis=1)
            rhs = block_diag(jnp.concatenate([p2[:, lo_l], qf[0]], axis=0),
                             jnp.concatenate([p2[:, hi_l], qf[1]], axis=0))
            r2 = jnp.dot(lhs, rhs, preferred_element_type=F32)
            w2 = pair_row(w_key, h0)
            vtw = jnp.concatenate([(vt_aug[0].astype(F32) * w2[:, lo_l]).astype(BF16),
                                   (vt_aug[1].astype(F32) * w2[:, hi_l]).astype(BF16)], axis=1)
            kv = jnp.dot(vtw, block_diag(k2[:, lo_l], k2[:, hi_l]), preferred_element_type=F32)
            ct_ref[p] = pair_row(decay, h0) * c_old + pair_row(s_fac, h0) * kv

            den = r2[HEAD_DIM:HEAD_DIM + 1, :]
            inv = 1.0 / jnp.maximum(jnp.abs(den), floor)
            h_t = r2[0:HEAD_DIM, :] * inv
            for h, l in ((h0, lo_l), (h0 + 1, hi_l)):
                hc = slice(h * HEAD_DIM, (h + 1) * HEAD_DIM)
                o_gate = jax.nn.sigmoid(o_ref[rows, hc].astype(F32))
                mix_ref[rows, hc] = (o_gate * jnp.transpose(h_t[:, l])).astype(BF16)
        m_prev = m_new
    m_ref[0:N_HEADS, :] = jnp.broadcast_to(m_prev, (N_HEADS, LANES))

    u_c = cc_ref[...].astype(F32) * cx_ref[...].astype(F32)
    rid = lax.broadcasted_iota(I32, u_c.shape, 0)
    prev1 = carry_ref[0:1, :]
    prev2 = carry_ref[1:2, :]
    u1 = jnp.where(rid == 0, prev1, pltpu.roll(u_c, 1, axis=0))
    u2 = jnp.where(rid == 0, prev2, jnp.where(rid == 1, prev1, pltpu.roll(u_c, 2, axis=0)))
    z = cw_ref[0:1, :] * u2 + cw_ref[1:2, :] * u1 + cw_ref[2:3, :] * u_c
    mix_ref[:, D_MLSTM:] = (cb_ref[...].astype(F32) * z).astype(BF16)
    carry_ref[0:1, :] = u_c[ts - 1:ts, :]
    carry_ref[1:2, :] = u_c[ts - 2:ts - 1, :]

    d_model = x_ref.shape[1]
    mix = mix_ref[...]
    ssq = jnp.zeros((ts, 1), F32)
    for j in range(d_model // OUT_COLS):
        cs = slice(j * OUT_COLS, (j + 1) * OUT_COLS)
        x2c = x_ref[:, cs] + jnp.dot(mix, wo_ref[:, cs], preferred_element_type=F32)
        x2_ref[:, cs] = x2c
        ssq = ssq + jnp.sum(x2c * x2c, axis=-1, keepdims=True)
    hn2 = x2_ref[...] * lax.rsqrt(ssq * (1.0 / d_model) + RMS_EPS) * g2_ref[...]
    hn2_ref[...] = _pack_rows(hn2)

    logits = jnp.dot(hn2.astype(BF16), wr_ref[...], preferred_element_type=F32) + br_ref[...]
    lt = jnp.transpose(logits)
    gl = lt[0:N_GROUPS]
    gmax = jnp.max(gl, axis=0, keepdims=True)
    gi = lax.broadcasted_iota(I32, gl.shape, 0)
    g_sel = jnp.min(jnp.where(gl == gmax, gi, N_GROUPS), axis=0, keepdims=True)
    p_g = 1.0 / jnp.sum(jnp.exp(gl - gmax), axis=0, keepdims=True)
    in_g = lt[EXPERT_COL0:EXPERT_COL0 + EXPERTS_PER_GROUP]
    for g in range(1, N_GROUPS):
        lo = EXPERT_COL0 + g * EXPERTS_PER_GROUP
        in_g = jnp.where(g_sel == g, lt[lo:lo + EXPERTS_PER_GROUP], in_g)
    ei = lax.broadcasted_iota(I32, in_g.shape, 0)
    v1 = jnp.max(in_g, axis=0, keepdims=True)
    i1 = jnp.min(jnp.where(in_g == v1, ei, EXPERTS_PER_GROUP), axis=0, keepdims=True)
    rest = jnp.where(ei == i1, neg_inf, in_g)
    v2 = jnp.max(rest, axis=0, keepdims=True)
    i2 = jnp.min(jnp.where(rest == v2, ei, EXPERTS_PER_GROUP), axis=0, keepdims=True)
    t = jnp.exp(v2 - v1)
    w1 = p_g / (1.0 + t)
    w2 = p_g * t / (1.0 + t)
    e1 = g_sel * EXPERTS_PER_GROUP + i1
    e2 = g_sel * EXPERTS_PER_GROUP + i2

    eio = lax.broadcasted_iota(I32, (N_EXPERTS, LANES), 0)
    base = cnt_ref[:, 0:1]
    zero_i = jnp.zeros((1, LANES), I32)
    n_blk = ts // LANES
    hits = []
    for k in range(n_blk):
        ls = slice(k * LANES, (k + 1) * LANES)
        hits.append((eio == e1[:, ls], eio == e2[:, ls]))
    onehots = [jnp.where(jnp.logical_or(a1, a2), 1.0, 0.0) for a1, a2 in hits]
    incl_all = jnp.dot(jnp.concatenate(onehots, axis=0).astype(BF16), prefix_ones,
                       preferred_element_type=F32).astype(I32)
    for k in range(n_blk):
        ls = slice(k * LANES, (k + 1) * LANES)
        is1, is2 = hits[k]
        onehot = onehots[k].astype(I32)
        incl = incl_all[k * N_EXPERTS:(k + 1) * N_EXPERTS]
        rank_e = base + incl - onehot
        r1 = jnp.sum(jnp.where(is1, rank_e, 0), axis=0, keepdims=True)
        r2 = jnp.sum(jnp.where(is2, rank_e, 0), axis=0, keepdims=True)
        ri_ref[:, ls] = jnp.concatenate(
            [e1[:, ls], e2[:, ls], r1, r2, zero_i, zero_i, zero_i, zero_i], axis=0)
        base = base + incl[:, LANES - 1:LANES]
    zero_f = jnp.zeros_like(w1)
    rw_ref[...] = jnp.concatenate([w1, w2, zero_f, zero_f, zero_f, zero_f, zero_f, zero_f], axis=0)
    cnt_new = jnp.broadcast_to(base, (N_EXPERTS, LANES))
    cnt_ref[...] = cnt_new
    cnt_out_ref[...] = cnt_new


def _mixer(x, a, qvt, gt, conv_w, w_out, norm_ffn, w_router, b_router):
    b_sz, s_len, d = x.shape
    ts = TS_MIXER
    dm = D_MLSTM

    def a_spec(j):
        return pl.BlockSpec((None, ts, dm), lambda b, s, j=j: (b, s, j))

    tok = lambda w: pl.BlockSpec((None, ts, w), lambda b, s: (b, s, 0))
    rowsp = pl.BlockSpec((None, 2 * N_HEADS, ts), lambda b, s: (b, 0, s))
    const = lambda shape: pl.BlockSpec(shape, lambda b, s: tuple(0 for _ in shape))
    return pl.pallas_call(
        _mixer_kernel,
        grid=(b_sz, s_len // ts),
        in_specs=[a_spec(0), a_spec(1), a_spec(2), a_spec(3), a_spec(4),
                  pl.BlockSpec((None, dm, ts), lambda b, s: (b, 0, s)),
                  pl.BlockSpec((None, dm, ts), lambda b, s: (b, 1, s)),
                  rowsp, tok(d),
                  const((3, D_CONV)), const((d, d)), const((1, d)),
                  const((d, ROUTER_COLS)), const((1, ROUTER_COLS))],
        out_specs=[tok(d), tok(d // 2), rowsp, rowsp, const((N_EXPERTS, LANES))],
        out_shape=[
            jax.ShapeDtypeStruct((b_sz, s_len, d), F32),
            jax.ShapeDtypeStruct((b_sz, s_len, d // 2), I32),
            jax.ShapeDtypeStruct((b_sz, 2 * N_HEADS, s_len), I32),
            jax.ShapeDtypeStruct((b_sz, 2 * N_HEADS, s_len), F32),
            jax.ShapeDtypeStruct((N_EXPERTS, LANES), I32),
        ],
        scratch_shapes=[
            pltpu.VMEM((N_HEADS // 2, HEAD_DIM + AUG_ROWS, 2 * HEAD_DIM), F32),
            pltpu.VMEM((2 * N_HEADS, LANES), F32),
            pltpu.VMEM((8, D_CONV), F32),
            pltpu.VMEM((N_EXPERTS, LANES), I32),
            pltpu.VMEM((ts, d), BF16),
        ],
        compiler_params=pltpu.CompilerParams(
            dimension_semantics=("arbitrary", "arbitrary"), vmem_limit_bytes=VMEM_LIMIT),
        name="mixer",
    )(a, a, a, a, a, qvt, qvt, gt, x, conv_w, w_out.astype(BF16), norm_ffn[None, :], w_router.astype(BF16), b_router)


SC_CORES = 2
SC_SUBCORES = 16
SC_WORKERS = SC_CORES * SC_SUBCORES
SC_CHUNK = 64


def _sc_worker_id():
    return lax.axis_index("s") * SC_CORES + lax.axis_index("c")


def _sc_scatter_rows(rows, idx, n_out):
    n_rows, width = rows.shape
    per_worker = n_rows // SC_WORKERS
    n_chunks = per_worker // SC_CHUNK
    assert n_chunks % 2 == 0
    mesh = plsc.VectorSubcoreMesh(core_axis_name="c", subcore_axis_name="s")

    def body(rows_hbm, idx_hbm, out_hbm, ia0, ib0, r0, ia1, ib1, r1, ls0, ls1, ss0, ss1):
        wid = _sc_worker_id()
        bufs = ((ia0, ib0, r0, ls0, ss0), (ia1, ib1, r1, ls1, ss1))

        def loads(j, b):
            ia, ib, r, ls, _ = bufs[b]
            base = pl.multiple_of(wid * per_worker + j * SC_CHUNK, SC_CHUNK)
            return (pltpu.make_async_copy(rows_hbm.at[pl.ds(base, SC_CHUNK)], r, ls),
                    pltpu.make_async_copy(idx_hbm.at[pl.ds(base, SC_CHUNK)], ia, ls),
                    pltpu.make_async_copy(idx_hbm.at[pl.ds(n_rows + base, SC_CHUNK)], ib, ls))

        def scatters(b):
            ia, ib, r, _, ss = bufs[b]
            return (pltpu.make_async_copy(r, out_hbm.at[ia], ss),
                    pltpu.make_async_copy(r, out_hbm.at[ib], ss))

        for c in loads(0, 0):
            c.start()

        @pl.loop(0, n_chunks, step=2)
        def _(j0):
            for b in range(2):
                j = j0 + b
                for c in loads(j, b):
                    c.wait()

                @pl.when(j >= 1)
                def _():
                    for c in scatters(1 - b):
                        c.wait()

                @pl.when(j + 1 < n_chunks)
                def _():
                    for c in loads(j + 1, 1 - b):
                        c.start()

                for c in scatters(b):
                    c.start()

        for c in scatters((n_chunks - 1) % 2):
            c.wait()

    idx_t = pltpu.VMEM((SC_CHUNK,), I32)
    row_t = pltpu.VMEM((SC_CHUNK, width), I32)
    sem_t = pltpu.SemaphoreType.DMA
    return pl.kernel(
        body,
        out_type=jax.ShapeDtypeStruct((n_out, width), I32),
        mesh=mesh,
        scratch_types=[idx_t, idx_t, row_t, idx_t, idx_t, row_t, sem_t, sem_t, sem_t, sem_t],
        name="sc_scatter",
    )(rows, idx)


def _sc_gather_rows(table, idx):
    n_rows = idx.shape[0]
    width = table.shape[1]
    per_worker = n_rows // SC_WORKERS
    n_chunks = per_worker // SC_CHUNK
    assert n_chunks % 2 == 0
    mesh = plsc.VectorSubcoreMesh(core_axis_name="c", subcore_axis_name="s")

    def body(table_hbm, idx_hbm, out_hbm, i0, r0, i1, r1, is0, is1, gs0, gs1, ws0, ws1):
        wid = _sc_worker_id()
        bufs = ((i0, r0, is0, gs0, ws0), (i1, r1, is1, gs1, ws1))

        def chunk_base(j):
            return pl.multiple_of(wid * per_worker + j * SC_CHUNK, SC_CHUNK)

        def idx_load(j, b):
            return pltpu.make_async_copy(idx_hbm.at[pl.ds(chunk_base(j), SC_CHUNK)], bufs[b][0], bufs[b][2])

        def gather(b):
            return pltpu.make_async_copy(table_hbm.at[bufs[b][0]], bufs[b][1], bufs[b][3])

        def writeback(j, b):
            return pltpu.make_async_copy(bufs[b][1], out_hbm.at[pl.ds(chunk_base(j), SC_CHUNK)], bufs[b][4])

        idx_load(0, 0).start()
        idx_load(0, 0).wait()
        gather(0).start()
        idx_load(1, 1).start()

        @pl.loop(0, n_chunks, step=2)
        def _(j0):
            for b in range(2):
                j = j0 + b

                @pl.when(j + 1 < n_chunks)
                def _():
                    idx_load(j + 1, 1 - b).wait()

                    @pl.when(j >= 1)
                    def _():
                        writeback(j - 1, 1 - b).wait()
                    gather(1 - b).start()

                gather(b).wait()
                writeback(j, b).start()

                @pl.when(j + 2 < n_chunks)
                def _():
                    idx_load(j + 2, b).start()

        writeback(n_chunks - 2, 0).wait()
        writeback(n_chunks - 1, 1).wait()

    idx_t = pltpu.VMEM((SC_CHUNK,), I32)
    row_t = pltpu.VMEM((SC_CHUNK, width), I32)
    sem_t = pltpu.SemaphoreType.DMA
    return pl.kernel(
        body,
        out_type=jax.ShapeDtypeStruct((n_rows, width), I32),
        mesh=mesh,
        scratch_types=[idx_t, row_t, idx_t, row_t, sem_t, sem_t, sem_t, sem_t, sem_t, sem_t],
        name="sc_gather",
    )(table, idx)


def _experts_kernel(te_ref, nu_ref, tv_ref, ne_ref, xs_ref, wg_hbm, wu_hbm, wd_hbm, y_ref,
                    wgs_ref, wus_ref, wds_ref, wgb_ref, wub_ref, wdb_ref, wsem):
    i = pl.program_id(0)
    e = te_ref[i]
    e_prev = te_ref[jnp.maximum(i - 1, 0)]
    new_expert = jnp.logical_or(i == 0, e != e_prev)
    used = i < nu_ref[0]

    def weight_copies(expert):
        return (pltpu.make_async_copy(wg_hbm.at[expert], wgs_ref, wsem),
                pltpu.make_async_copy(wu_hbm.at[expert], wus_ref, wsem),
                pltpu.make_async_copy(wd_hbm.at[expert], wds_ref, wsem))

    @pl.when(i == 0)
    def _():
        for c in weight_copies(e):
            c.start()

    @pl.when(jnp.logical_and(used, new_expert))
    def _():
        for c in weight_copies(e):
            c.wait()
        wgb_ref[...] = wgs_ref[...].astype(BF16)
        wub_ref[...] = wus_ref[...].astype(BF16)
        wdb_ref[...] = wds_ref[...].astype(BF16)

        @pl.when(ne_ref[i] >= 0)
        def _():
            for c in weight_copies(ne_ref[i]):
                c.start()

    @pl.when(used)
    def _():
        row = lax.broadcasted_iota(I32, xs_ref.shape, 0)
        x_lo, x_hi = _unpack_rows(jnp.where(row < tv_ref[i], xs_ref[...], 0))
        xb = jnp.concatenate([x_lo.astype(BF16), x_hi.astype(BF16)], axis=1)
        gate = jnp.dot(xb, wgb_ref[...], preferred_element_type=F32)
        up = jnp.dot(xb, wub_ref[...], preferred_element_type=F32)
        hid = (gate * jax.nn.sigmoid(gate) * up).astype(BF16)
        y_ref[...] = _pack_rows(jnp.dot(hid, wdb_ref[...], preferred_element_type=F32))

    @pl.when(jnp.logical_not(used))
    def _():
        y_ref[...] = jnp.zeros_like(y_ref)


def _experts(xs, tile_e, n_used, tile_valid, next_e, w_gate, w_up, w_down):
    cap, dw = xs.shape
    d = 2 * dw
    tm = TM_EXPERT
    n_tiles = cap // tm

    def x_map(i, te, nu, tv, ne):
        return (jnp.minimum(i, jnp.maximum(nu[0] - 1, 0)), 0)

    return pl.pallas_call(
        _experts_kernel,
        grid_spec=pltpu.PrefetchScalarGridSpec(
            num_scalar_prefetch=4,
            grid=(n_tiles,),
            in_specs=[
                pl.BlockSpec((tm, dw), x_map),
                pl.BlockSpec(memory_space=pl.ANY),
                pl.BlockSpec(memory_space=pl.ANY),
                pl.BlockSpec(memory_space=pl.ANY),
            ],
            out_specs=pl.BlockSpec((tm, dw), lambda i, te, nu, tv, ne: (i, 0)),
            scratch_shapes=[pltpu.VMEM((d, D_FF), F32), pltpu.VMEM((d, D_FF), F32), pltpu.VMEM((D_FF, d), F32),
                            pltpu.VMEM((d, D_FF), BF16), pltpu.VMEM((d, D_FF), BF16), pltpu.VMEM((D_FF, d), BF16),
                            pltpu.SemaphoreType.DMA(())],
        ),
        out_shape=jax.ShapeDtypeStruct((cap, dw), I32),
        compiler_params=pltpu.CompilerParams(
            dimension_semantics=("arbitrary",), vmem_limit_bytes=VMEM_LIMIT),
        name="experts",
    )(tile_e, n_used, tile_valid, next_e, xs, w_gate, w_up, w_down)


def _combine_stream_kernel(x2_ref, y1_ref, y2_ref, rw_ref, gf_ref, out_ref):
    w_t = jnp.transpose(rw_ref[...])
    a_lo, a_hi = _unpack_rows(y1_ref[...])
    b_lo, b_hi = _unpack_rows(y2_ref[...])
    w1 = w_t[:, 0:1]
    w2 = w_t[:, 1:2]
    moe = jnp.concatenate([w1 * a_lo + w2 * b_lo, w1 * a_hi + w2 * b_hi], axis=1)
    x3 = x2_ref[...] + moe
    ms = jnp.mean(x3 * x3, axis=-1, keepdims=True)
    out_ref[...] = x3 * lax.rsqrt(ms + RMS_EPS) * gf_ref[...]


def _combine_stream(x2, yg, rw, norm_final):
    b_sz, s_len, d = x2.shape
    tc = TC_COMBINE
    n_s = s_len // tc
    n_steps = b_sz * n_s
    return pl.pallas_call(
        _combine_stream_kernel,
        grid=(n_steps,),
        in_specs=[
            pl.BlockSpec((tc, d), lambda g: (g, 0)),
            pl.BlockSpec((tc, d // 2), lambda g: (g, 0)),
            pl.BlockSpec((tc, d // 2), lambda g: (g + n_steps, 0)),
            pl.BlockSpec((None, 2 * N_HEADS, tc), lambda g: (g // n_s, 0, g % n_s)),
            pl.BlockSpec((1, d), lambda g: (0, 0)),
        ],
        out_specs=pl.BlockSpec((tc, d), lambda g: (g, 0)),
        out_shape=jax.ShapeDtypeStruct((b_sz * s_len, d), F32),
        compiler_params=pltpu.CompilerParams(
            dimension_semantics=("arbitrary",), vmem_limit_bytes=VMEM_LIMIT),
        name="combine_stream",
    )(x2.reshape(b_sz * s_len, d), yg, yg, rw, norm_final[None, :]).reshape(b_sz, s_len, d)


TM_EXPERT_SHIFT = TM_EXPERT.bit_length() - 1
assert 1 << TM_EXPERT_SHIFT == TM_EXPERT


def _plan_kernel(cnt_ref, ri_ref, pos_ref, meta_ref):
    tm = TM_EXPERT
    n_e, lanes = cnt_ref.shape
    cnt = cnt_ref[...]
    padded = lax.shift_right_logical(cnt + (tm - 1), TM_EXPERT_SHIFT) * tm
    sub = lax.broadcasted_iota(I32, (n_e, lanes), 0)
    lane = lax.broadcasted_iota(I32, (n_e, lanes), 1)
    padded_row = jnp.sum(jnp.where(sub == lane, padded, 0), axis=0, keepdims=True)
    below = jnp.where(lane <= sub, jnp.broadcast_to(padded_row, (n_e, lanes)), 0)
    pends = jnp.sum(below.astype(F32), axis=1, keepdims=True).astype(I32)
    seg = pends - padded[:, 0:1]
    tok_end = seg + cnt[:, 0:1]
    total = pends[n_e - 1:n_e, :]

    n_b, _, s_len = ri_ref.shape
    e_ids = lax.broadcasted_iota(I32, (n_e, s_len), 0)
    for b in range(n_b):
        for k in range(2):
            start = jnp.sum(jnp.where(ri_ref[b, k:k + 1, :] == e_ids, seg, 0), axis=0, keepdims=True)
            pos_ref[k, b:b + 1, :] = ri_ref[b, 2 + k:3 + k, :] + start

    n_t = meta_ref.shape[1]
    t_start = lax.broadcasted_iota(I32, (1, n_t), 1) * tm
    experts_before = lambda row: jnp.minimum(
        jnp.sum(jnp.where(row >= pends, 1, 0), axis=0, keepdims=True), n_e - 1)
    tile_e = experts_before(t_start)
    mine = tile_e == lax.broadcasted_iota(I32, (n_e, n_t), 0)
    pick = lambda col: jnp.sum(jnp.where(mine, col, 0), axis=0, keepdims=True)
    tile_valid = jnp.clip(pick(tok_end) - t_start, 0, tm)
    seg_end = pick(pends)
    next_e = jnp.where(seg_end < total, experts_before(seg_end), -1)
    n_used = jnp.broadcast_to(lax.shift_right_logical(total, TM_EXPERT_SHIFT), (1, n_t))
    zero = jnp.zeros_like(tile_e)
    meta_ref[...] = jnp.concatenate([tile_e, tile_valid, next_e, n_used, zero, zero, zero, zero], axis=0)


def _plan(ri, counts, n_tiles):
    b_sz, _, s_len = ri.shape
    assert n_tiles * TM_EXPERT < 2 ** 24
    n_t = -(-n_tiles // LANES) * LANES
    pos, meta = pl.pallas_call(
        _plan_kernel,
        out_shape=[jax.ShapeDtypeStruct((2, b_sz, s_len), I32), jax.ShapeDtypeStruct((8, n_t), I32)],
        name="plan",
    )(counts, ri)
    return pos.reshape(-1), meta[0, :n_tiles], meta[3, :1], meta[1, :n_tiles], meta[2, :n_tiles]


def _layer(x, norm_mix, w_in, b_in, conv_w, w_out, norm_ffn, w_group, b_group, w_expert, b_expert,
           w_gate, w_up, w_down, norm_out):
    b_sz, s_len, d = x.shape
    n_tok = b_sz * s_len
    a, qvt, gt = _inproj(x, norm_mix, w_in, b_in)

    pad_g = EXPERT_COL0 - N_GROUPS
    pad_e = ROUTER_COLS - EXPERT_COL0 - N_EXPERTS
    w_router = jnp.concatenate([w_group, jnp.zeros((d, pad_g), F32), w_expert, jnp.zeros((d, pad_e), F32)], axis=1)
    b_router = jnp.concatenate([b_group, jnp.zeros((pad_g,), F32), b_expert, jnp.zeros((pad_e,), F32)])[None, :]
    x2, hn2, ri, rw, counts = _mixer(x, a, qvt, gt, conv_w, w_out, norm_ffn, w_router, b_router)

    n_tiles = (n_tok * 2) // TM_EXPERT + N_EXPERTS
    cap = n_tiles * TM_EXPERT
    pos_flat, tile_e, n_used, tile_valid, next_e = _plan(ri, counts, n_tiles)
    xs = _sc_scatter_rows(hn2.reshape(n_tok, d // 2), pos_flat, cap)
    y = _experts(xs, tile_e, n_used, tile_valid, next_e, w_gate, w_up, w_down)
    yg = _sc_gather_rows(y, pos_flat)
    return _combine_stream(x2, yg, rw, norm_out)


def kernel(x, norm_mix, w_in, b_in, conv_w, w_out, norm_ffn, w_group, b_group, w_expert, b_expert,
           w_gate, w_up, w_down, norm_final):
    depth = norm_mix.shape[0]
    assert depth == 1, "final-norm fusion below assumes a single layer"
    assert x.shape[-1] == 2 * D_MLSTM
    assert all(x.shape[1] % t == 0 for t in (TS_MIXER, TM_INPROJ, TC_COMBINE))
    return _layer(x, norm_mix[0], w_in[0], b_in[0], conv_w[0], w_out[0], norm_ffn[0],
                  w_group[0], b_group[0], w_expert[0], b_expert[0],
                  w_gate[0], w_up[0], w_down[0], norm_final)
```

```python
import jax
import jax.numpy as jnp
from jax import lax
from jax.experimental import pallas as pl
from jax.experimental.pallas import tpu as pltpu
from jax.experimental.pallas import tpu_sc as plsc

F32 = jnp.float32
BF16 = jnp.bfloat16
I32 = jnp.int32

N_HEADS = 4
HEAD_DIM = 128
D_MLSTM = N_HEADS * HEAD_DIM
D_CONV = 512
N_GROUPS = 4
EXPERTS_PER_GROUP = 8
N_EXPERTS = N_GROUPS * EXPERTS_PER_GROUP
D_FF = 512
RMS_EPS = 1e-6
Q_SCALE = HEAD_DIM ** -0.5

LANES = 128
CHUNK = LANES
AUG_ROWS = 16
ROUTER_COLS = 128
EXPERT_COL0 = 8

TM_INPROJ = 1024
TS_MIXER = 1024
OUT_COLS = 256
TM_EXPERT = 512
TC_COMBINE = 1024

VMEM_LIMIT = 56 * 1024 * 1024


HI_HALF = -65536


def _pack_rows(x):
    w = x.shape[1] // 2
    lo = lax.shift_right_logical(lax.bitcast_convert_type(x[:, :w].astype(BF16).astype(F32), I32), 16)
    hi = lax.bitcast_convert_type(x[:, w:].astype(BF16).astype(F32), I32) & HI_HALF
    return lo | hi


def _unpack_rows(words):
    lo = lax.bitcast_convert_type(words << 16, F32)
    hi = lax.bitcast_convert_type(words & HI_HALF, F32)
    return lo, hi


def _nt_dot(a, b):
    return lax.dot_general(a, b, (((1,), (1,)), ((), ())), preferred_element_type=F32)


def _inproj_kernel(x_ref, g_ref, wa_ref, ba_ref, wqv_ref, wgt_ref, bt_ref, a_ref, qvt_ref, gt_ref, wt_ref):
    @pl.when(jnp.logical_and(pl.program_id(0) == 0, pl.program_id(1) == 0))
    def _():
        n_qv = wqv_ref.shape[1]
        for c in range(0, n_qv, 2 * LANES):
            cols = slice(c, c + 2 * LANES)
            wt_ref[cols, :] = jnp.transpose(wqv_ref[:, cols].astype(F32)).astype(BF16)
        wt_ref[n_qv:, :] = wgt_ref[...]

    x = x_ref[...]
    ms = jnp.mean(x * x, axis=-1, keepdims=True)
    hn = (x * lax.rsqrt(ms + RMS_EPS) * g_ref[...]).astype(BF16)
    n_blk = wa_ref.shape[1] // D_MLSTM
    for j in range(n_blk):
        sl = slice(j * D_MLSTM, (j + 1) * D_MLSTM)
        acc = jnp.dot(hn, wa_ref[:, sl], preferred_element_type=F32) + ba_ref[:, sl]
        a_ref[:, sl] = acc.astype(BF16)
    tr = _nt_dot(wt_ref[...], hn) + bt_ref[...]
    q_rows = slice(0, D_MLSTM)
    v_rows = slice(D_MLSTM, 2 * D_MLSTM)
    qvt_ref[q_rows, :] = (tr[q_rows, :] * Q_SCALE).astype(BF16)
    qvt_ref[v_rows, :] = tr[v_rows, :].astype(BF16)
    gt_ref[...] = tr[2 * D_MLSTM:2 * D_MLSTM + 2 * N_HEADS, :]


def _inproj(x, norm_mix, w_in, b_in):
    b_sz, s_len, d = x.shape
    tm = TM_INPROJ
    dm = D_MLSTM
    g0 = 4 * dm
    c0 = g0 + 2 * N_HEADS
    cols_a = jnp.concatenate([w_in[:, dm:2 * dm], w_in[:, 3 * dm:4 * dm], w_in[:, c0:c0 + 3 * D_CONV]], axis=1)
    bias_a = jnp.concatenate([b_in[dm:2 * dm], b_in[3 * dm:4 * dm], b_in[c0:c0 + 3 * D_CONV]])[None, :]
    n_gate = 2 * N_HEADS
    pad_rows = AUG_ROWS - n_gate
    w_qv = jnp.concatenate([w_in[:, 0:dm], w_in[:, 2 * dm:3 * dm]], axis=1).astype(BF16)
    w_gt = jnp.concatenate([w_in[:, g0:c0], jnp.zeros((d, pad_rows), w_in.dtype)], axis=1).T.astype(BF16)
    b_t = jnp.concatenate([b_in[0:dm], b_in[2 * dm:3 * dm], b_in[g0:c0],
                           jnp.zeros((pad_rows,), b_in.dtype)])[:, None]
    n_t = 2 * dm + AUG_ROWS
    n_a = cols_a.shape[1]
    return pl.pallas_call(
        _inproj_kernel,
        grid=(b_sz, s_len // tm),
        in_specs=[
            pl.BlockSpec((None, tm, d), lambda b, s: (b, s, 0)),
            pl.BlockSpec((1, d), lambda b, s: (0, 0)),
            pl.BlockSpec((d, n_a), lambda b, s: (0, 0)),
            pl.BlockSpec((1, n_a), lambda b, s: (0, 0)),
            pl.BlockSpec((d, 2 * dm), lambda b, s: (0, 0)),
            pl.BlockSpec((AUG_ROWS, d), lambda b, s: (0, 0)),
            pl.BlockSpec((n_t, 1), lambda b, s: (0, 0)),
        ],
        out_specs=[
            pl.BlockSpec((None, tm, n_a), lambda b, s: (b, s, 0)),
            pl.BlockSpec((None, 2 * dm, tm), lambda b, s: (b, 0, s)),
            pl.BlockSpec((None, 2 * N_HEADS, tm), lambda b, s: (b, 0, s)),
        ],
        out_shape=[
            jax.ShapeDtypeStruct((b_sz, s_len, n_a), BF16),
            jax.ShapeDtypeStruct((b_sz, 2 * dm, s_len), BF16),
            jax.ShapeDtypeStruct((b_sz, 2 * N_HEADS, s_len), F32),
        ],
        scratch_shapes=[pltpu.VMEM((n_t, d), BF16)],
        compiler_params=pltpu.CompilerParams(
            dimension_semantics=("arbitrary", "arbitrary"), vmem_limit_bytes=VMEM_LIMIT),
        name="inproj",
    )(x, norm_mix[None, :], cols_a.astype(BF16), bias_a, w_qv, w_gt, b_t)


def _mixer_kernel(k_ref, o_ref, cb_ref, cc_ref, cx_ref, qt_ref, vt_ref, gt_ref, x_ref,
                  cw_ref, wo_ref, g2_ref, wr_ref, br_ref,
                  x2_ref, hn2_ref, ri_ref, rw_ref, cnt_out_ref,
                  ct_ref, m_ref, carry_ref, cnt_ref, mix_ref):
    ts = x_ref.shape[0]
    n_chunks = ts // CHUNK
    first_tile = pl.program_id(1) == 0

    @pl.when(first_tile)
    def _():
        ct_ref[...] = jnp.zeros_like(ct_ref)
        m_ref[...] = jnp.zeros_like(m_ref)
        carry_ref[...] = jnp.zeros_like(carry_ref)

    @pl.when(jnp.logical_and(first_tile, pl.program_id(0) == 0))
    def _():
        cnt_ref[...] = jnp.zeros_like(cnt_ref)

    gates = gt_ref[...]
    li_all = gates[0:N_HEADS]
    fg = gates[N_HEADS:2 * N_HEADS]
    lf_all = jnp.minimum(fg, 0.0) - jnp.log1p(jnp.exp(-jnp.abs(fg)))
    key_j = lax.broadcasted_iota(I32, (CHUNK, CHUNK), 0)
    qry_i = lax.broadcasted_iota(I32, (CHUNK, CHUNK), 1)
    causal_t = key_j <= qry_i
    neg_inf = jnp.float32(-jnp.inf)
    ones_rows = (lax.broadcasted_iota(I32, (AUG_ROWS, CHUNK), 0) == 0).astype(BF16)

    prefix_ones = (key_j <= qry_i).astype(BF16)

    lf_hi = lf_all.astype(BF16)
    rem = lf_all - lf_hi.astype(F32)
    lf_mid = rem.astype(BF16)
    lf_lo = (rem - lf_mid.astype(F32)).astype(BF16)
    zero_bf = jnp.zeros((N_HEADS, CHUNK), BF16)
    parts = []
    for c in range(n_chunks):
        rows = slice(c * CHUNK, (c + 1) * CHUNK)
        parts += [lf_hi[:, rows], lf_mid[:, rows], lf_lo[:, rows], zero_bf]
    cum = jnp.dot(jnp.concatenate(parts, axis=0), prefix_ones, preferred_element_type=F32)

    m_prev = m_ref[0:N_HEADS, 0:1]
    for c in range(n_chunks):
        rows = slice(c * CHUNK, (c + 1) * CHUNK)
        li = li_all[:, rows]
        r0 = 4 * N_HEADS * c
        b_cum = (cum[r0:r0 + N_HEADS] + cum[r0 + N_HEADS:r0 + 2 * N_HEADS]
                 + cum[r0 + 2 * N_HEADS:r0 + 3 * N_HEADS])
        g_tot = b_cum[:, CHUNK - 1:CHUNK]
        a_end = g_tot - b_cum + li
        m_loc = jnp.max(a_end, axis=1, keepdims=True)
        u = li - b_cum
        m_new = jnp.maximum(g_tot + m_prev, m_loc)
        decay = jnp.exp(g_tot + m_prev - m_new)
        s_fac = jnp.exp(m_loc - m_new)
        w_key = jnp.exp(a_end - m_loc)
        u_cols = jnp.transpose(jnp.concatenate([u, jnp.zeros_like(u)], axis=0))

        def pair_row(x, h0):
            parts = [jnp.broadcast_to(x[h:h + 1, :], (1, CHUNK)) for h in (h0, h0 + 1)]
            return jnp.concatenate(parts, axis=1)

        def block_diag(a, b):
            z = jnp.zeros_like(a)
            return jnp.concatenate([jnp.concatenate([a, z], axis=1), jnp.concatenate([z, b], axis=1)], axis=0)

        for p in range(N_HEADS // 2):
            h0 = 2 * p
            pc = slice(h0 * HEAD_DIM, (h0 + 2) * HEAD_DIM)
            lo_l, hi_l = slice(0, CHUNK), slice(CHUNK, 2 * CHUNK)
            k2 = k_ref[rows, pc]
            q2 = qt_ref[pc, rows]
            v2 = vt_ref[pc, rows]
            s2 = jnp.dot(k2, block_diag(q2[0:HEAD_DIM], q2[HEAD_DIM:]), preferred_element_type=F32)
            u_t = jnp.concatenate([jnp.where(causal_t, u_cols[:, h:h + 1], neg_inf) for h in (h0, h0 + 1)],
                                  axis=1)
            mp = pair_row(m_prev, h0)
            m_tot = jnp.maximum(jnp.max(u_t, axis=0, keepdims=True), mp)
            f_inter = jnp.exp(mp - m_tot)
            floor = jnp.exp(-pair_row(b_cum, h0) - m_tot)
            p2 = (s2 * jnp.exp(u_t - m_tot)).astype(BF16)
            qf = [(q2[r].astype(F32) * f_inter[:, l]).astype(BF16)
                  for r, l in ((slice(0, HEAD_DIM), lo_l), (slice(HEAD_DIM, None), hi_l))]
            c_old = ct_ref[p]
            cb16 = c_old.astype(BF16)
            vt_aug = [jnp.concatenate([v2[r], ones_rows], axis=0)
                      for r in (slice(0, HEAD_DIM), slice(HEAD_DIM, None))]
            lhs = jnp.concatenate([vt_aug[0], cb16[:, lo_l], vt_aug[1], cb16[:, hi_l]], axis=1)
            rhs = block_diag(jnp.concatenate([p2[:, lo_l], qf[0]], axis=0),
                             jnp.concatenate([p2[:, hi_l], qf[1]], axis=0))
            r2 = jnp.dot(lhs, rhs, preferred_element_type=F32)
            w2 = pair_row(w_key, h0)
            vtw = jnp.concatenate([(vt_aug[0].astype(F32) * w2[:, lo_l]).astype(BF16),
                                   (vt_aug[1].astype(F32) * w2[:, hi_l]).astype(BF16)], axis=1)
            kv = jnp.dot(vtw, block_diag(k2[:, lo_l], k2[:, hi_l]), preferred_element_type=F32)
            ct_ref[p] = pair_row(decay, h0) * c_old + pair_row(s_fac, h0) * kv

            den = r2[HEAD_DIM:HEAD_DIM + 1, :]
            inv = 1.0 / jnp.maximum(jnp.abs(den), floor)
            h_t = r2[0:HEAD_DIM, :] * inv
            for h, l in ((h0, lo_l), (h0 + 1, hi_l)):
                hc = slice(h * HEAD_DIM, (h + 1) * HEAD_DIM)
                o_gate = jax.nn.sigmoid(o_ref[rows, hc].astype(F32))
                mix_ref[rows, hc] = (o_gate * jnp.transpose(h_t[:, l])).astype(BF16)
        m_prev = m_new
    m_ref[0:N_HEADS, :] = jnp.broadcast_to(m_prev, (N_HEADS, LANES))

    u_c = cc_ref[...].astype(F32) * cx_ref[...].astype(F32)
    rid = lax.broadcasted_iota(I32, u_c.shape, 0)
    prev1 = carry_ref[0:1, :]
    prev2 = carry_ref[1:2, :]
    u1 = jnp.where(rid == 0, prev1, pltpu.roll(u_c, 1, axis=0))
    u2 = jnp.where(rid == 0, prev2, jnp.where(rid == 1, prev1, pltpu.roll(u_c, 2, axis=0)))
    z = cw_ref[0:1, :] * u2 + cw_ref[1:2, :] * u1 + cw_ref[2:3, :] * u_c
    mix_ref[:, D_MLSTM:] = (cb_ref[...].astype(F32) * z).astype(BF16)
    carry_ref[0:1, :] = u_c[ts - 1:ts, :]
    carry_ref[1:2, :] = u_c[ts - 2:ts - 1, :]

    d_model = x_ref.shape[1]
    mix = mix_ref[...]
    ssq = jnp.zeros((ts, 1), F32)
    for j in range(d_model // OUT_COLS):
        cs = slice(j * OUT_COLS, (j + 1) * OUT_COLS)
        x2c = x_ref[:, cs] + jnp.dot(mix, wo_ref[:, cs], preferred_element_type=F32)
        x2_ref[:, cs] = x2c
        ssq = ssq + jnp.sum(x2c * x2c, axis=-1, keepdims=True)
    hn2 = x2_ref[...] * lax.rsqrt(ssq * (1.0 / d_model) + RMS_EPS) * g2_ref[...]
    hn2_ref[...] = _pack_rows(hn2)

    logits = jnp.dot(hn2.astype(BF16), wr_ref[...], preferred_element_type=F32) + br_ref[...]
    lt = jnp.transpose(logits)
    gl = lt[0:N_GROUPS]
    gmax = jnp.max(gl, axis=0, keepdims=True)
    gi = lax.broadcasted_iota(I32, gl.shape, 0)
    g_sel = jnp.min(jnp.where(gl == gmax, gi, N_GROUPS), axis=0, keepdims=True)
    p_g = 1.0 / jnp.sum(jnp.exp(gl - gmax), axis=0, keepdims=True)
    in_g = lt[EXPERT_COL0:EXPERT_COL0 + EXPERTS_PER_GROUP]
    for g in range(1, N_GROUPS):
        lo = EXPERT_COL0 + g * EXPERTS_PER_GROUP
        in_g = jnp.where(g_sel == g, lt[lo:lo + EXPERTS_PER_GROUP], in_g)
    ei = lax.broadcasted_iota(I32, in_g.shape, 0)
    v1 = jnp.max(in_g, axis=0, keepdims=True)
    i1 = jnp.min(jnp.where(in_g == v1, ei, EXPERTS_PER_GROUP), axis=0, keepdims=True)
    rest = jnp.where(ei == i1, neg_inf, in_g)
    v2 = jnp.max(rest, axis=0, keepdims=True)
    i2 = jnp.min(jnp.where(rest == v2, ei, EXPERTS_PER_GROUP), axis=0, keepdims=True)
    t = jnp.exp(v2 - v1)
    w1 = p_g / (1.0 + t)
    w2 = p_g * t / (1.0 + t)
    e1 = g_sel * EXPERTS_PER_GROUP + i1
    e2 = g_sel * EXPERTS_PER_GROUP + i2

    eio = lax.broadcasted_iota(I32, (N_EXPERTS, LANES), 0)
    base = cnt_ref[:, 0:1]
    zero_i = jnp.zeros((1, LANES), I32)
    n_blk = ts // LANES
    hits = []
    for k in range(n_blk):
        ls = slice(k * LANES, (k + 1) * LANES)
        hits.append((eio == e1[:, ls], eio == e2[:, ls]))
    onehots = [jnp.where(jnp.logical_or(a1, a2), 1.0, 0.0) for a1, a2 in hits]
    incl_all = jnp.dot(jnp.concatenate(onehots, axis=0).astype(BF16), prefix_ones,
                       preferred_element_type=F32).astype(I32)
    for k in range(n_blk):
        ls = slice(k * LANES, (k + 1) * LANES)
        is1, is2 = hits[k]
        onehot = onehots[k].astype(I32)
        incl = incl_all[k * N_EXPERTS:(k + 1) * N_EXPERTS]
        rank_e = base + incl - onehot
        r1 = jnp.sum(jnp.where(is1, rank_e, 0), axis=0, keepdims=True)
        r2 = jnp.sum(jnp.where(is2, rank_e, 0), axis=0, keepdims=True)
        ri_ref[:, ls] = jnp.concatenate(
            [e1[:, ls], e2[:, ls], r1, r2, zero_i, zero_i, zero_i, zero_i], axis=0)
        base = base + incl[:, LANES - 1:LANES]
    zero_f = jnp.zeros_like(w1)
    rw_ref[...] = jnp.concatenate([w1, w2, zero_f, zero_f, zero_f, zero_f, zero_f, zero_f], axis=0)
    cnt_new = jnp.broadcast_to(base, (N_EXPERTS, LANES))
    cnt_ref[...] = cnt_new
    cnt_out_ref[...] = cnt_new


def _mixer(x, a, qvt, gt, conv_w, w_out, norm_ffn, w_router, b_router):
    b_sz, s_len, d = x.shape
    ts = TS_MIXER
    dm = D_MLSTM

    def a_spec(j):
        return pl.BlockSpec((None, ts, dm), lambda b, s, j=j: (b, s, j))

    tok = lambda w: pl.BlockSpec((None, ts, w), lambda b, s: (b, s, 0))
    rowsp = pl.BlockSpec((None, 2 * N_HEADS, ts), lambda b, s: (b, 0, s))
    const = lambda shape: pl.BlockSpec(shape, lambda b, s: tuple(0 for _ in shape))
    return pl.pallas_call(
        _mixer_kernel,
        grid=(b_sz, s_len // ts),
        in_specs=[a_spec(0), a_spec(1), a_spec(2), a_spec(3), a_spec(4),
                  pl.BlockSpec((None, dm, ts), lambda b, s: (b, 0, s)),
                  pl.BlockSpec((None, dm, ts), lambda b, s: (b, 1, s)),
                  rowsp, tok(d),
                  const((3, D_CONV)), const((d, d)), const((1, d)),
                  const((d, ROUTER_COLS)), const((1, ROUTER_COLS))],
        out_specs=[tok(d), tok(d // 2), rowsp, rowsp, const((N_EXPERTS, LANES))],
        out_shape=[
            jax.ShapeDtypeStruct((b_sz, s_len, d), F32),
            jax.ShapeDtypeStruct((b_sz, s_len, d // 2), I32),
            jax.ShapeDtypeStruct((b_sz, 2 * N_HEADS, s_len), I32),
            jax.ShapeDtypeStruct((b_sz, 2 * N_HEADS, s_len), F32),
            jax.ShapeDtypeStruct((N_EXPERTS, LANES), I32),
        ],
        scratch_shapes=[
            pltpu.VMEM((N_HEADS // 2, HEAD_DIM + AUG_ROWS, 2 * HEAD_DIM), F32),
            pltpu.VMEM((2 * N_HEADS, LANES), F32),
            pltpu.VMEM((8, D_CONV), F32),
            pltpu.VMEM((N_EXPERTS, LANES), I32),
            pltpu.VMEM((ts, d), BF16),
        ],
        compiler_params=pltpu.CompilerParams(
            dimension_semantics=("arbitrary", "arbitrary"), vmem_limit_bytes=VMEM_LIMIT),
        name="mixer",
    )(a, a, a, a, a, qvt, qvt, gt, x, conv_w, w_out.astype(BF16), norm_ffn[None, :], w_router.astype(BF16), b_router)


SC_CORES = 2
SC_SUBCORES = 16
SC_WORKERS = SC_CORES * SC_SUBCORES
SC_CHUNK = 64


def _sc_worker_id():
    return lax.axis_index("s") * SC_CORES + lax.axis_index("c")


def _sc_scatter_rows(rows, idx, n_out):
    n_rows, width = rows.shape
    per_worker = n_rows // SC_WORKERS
    n_chunks = per_worker // SC_CHUNK
    assert n_chunks % 2 == 0
    mesh = plsc.VectorSubcoreMesh(core_axis_name="c", subcore_axis_name="s")

    def body(rows_hbm, idx_hbm, out_hbm, ia0, ib0, r0, ia1, ib1, r1, ls0, ls1, ss0, ss1):
        wid = _sc_worker_id()
        bufs = ((ia0, ib0, r0, ls0, ss0), (ia1, ib1, r1, ls1, ss1))

        def loads(j, b):
            ia, ib, r, ls, _ = bufs[b]
            base = pl.multiple_of(wid * per_worker + j * SC_CHUNK, SC_CHUNK)
            return (pltpu.make_async_copy(rows_hbm.at[pl.ds(base, SC_CHUNK)], r, ls),
                    pltpu.make_async_copy(idx_hbm.at[pl.ds(base, SC_CHUNK)], ia, ls),
                    pltpu.make_async_copy(idx_hbm.at[pl.ds(n_rows + base, SC_CHUNK)], ib, ls))

        def scatters(b):
            ia, ib, r, _, ss = bufs[b]
            return (pltpu.make_async_copy(r, out_hbm.at[ia], ss),
                    pltpu.make_async_copy(r, out_hbm.at[ib], ss))

        for c in loads(0, 0):
            c.start()

        @pl.loop(0, n_chunks, step=2)
        def _(j0):
            for b in range(2):
                j = j0 + b
                for c in loads(j, b):
                    c.wait()

                @pl.when(j >= 1)
                def _():
                    for c in scatters(1 - b):
                        c.wait()

                @pl.when(j + 1 < n_chunks)
                def _():
                    for c in loads(j + 1, 1 - b):
                        c.start()

                for c in scatters(b):
                    c.start()

        for c in scatters((n_chunks - 1) % 2):
            c.wait()

    idx_t = pltpu.VMEM((SC_CHUNK,), I32)
    row_t = pltpu.VMEM((SC_CHUNK, width), I32)
    sem_t = pltpu.SemaphoreType.DMA
    return pl.kernel(
        body,
        out_type=jax.ShapeDtypeStruct((n_out, width), I32),
        mesh=mesh,
        scratch_types=[idx_t, idx_t, row_t, idx_t, idx_t, row_t, sem_t, sem_t, sem_t, sem_t],
        name="sc_scatter",
    )(rows, idx)


def _sc_gather_rows(table, idx):
    n_rows = idx.shape[0]
    width = table.shape[1]
    per_worker = n_rows // SC_WORKERS
    n_chunks = per_worker // SC_CHUNK
    assert n_chunks % 2 == 0
    mesh = plsc.VectorSubcoreMesh(core_axis_name="c", subcore_axis_name="s")

    def body(table_hbm, idx_hbm, out_hbm, i0, r0, i1, r1, is0, is1, gs0, gs1, ws0, ws1):
        wid = _sc_worker_id()
        bufs = ((i0, r0, is0, gs0, ws0), (i1, r1, is1, gs1, ws1))

        def chunk_base(j):
            return pl.multiple_of(wid * per_worker + j * SC_CHUNK, SC_CHUNK)

        def idx_load(j, b):
            return pltpu.make_async_copy(idx_hbm.at[pl.ds(chunk_base(j), SC_CHUNK)], bufs[b][0], bufs[b][2])

        def gather(b):
            return pltpu.make_async_copy(table_hbm.at[bufs[b][0]], bufs[b][1], bufs[b][3])

        def writeback(j, b):
            return pltpu.make_async_copy(bufs[b][1], out_hbm.at[pl.ds(chunk_base(j), SC_CHUNK)], bufs[b][4])

        idx_load(0, 0).start()
        idx_load(0, 0).wait()
        gather(0).start()
        idx_load(1, 1).start()

        @pl.loop(0, n_chunks, step=2)
        def _(j0):
            for b in range(2):
                j = j0 + b

                @pl.when(j + 1 < n_chunks)
                def _():
                    idx_load(j + 1, 1 - b).wait()

                    @pl.when(j >= 1)
                    def _():
                        writeback(j - 1, 1 - b).wait()
                    gather(1 - b).start()

                gather(b).wait()
                writeback(j, b).start()

                @pl.when(j + 2 < n_chunks)
                def _():
                    idx_load(j + 2, b).start()

        writeback(n_chunks - 2, 0).wait()
        writeback(n_chunks - 1, 1).wait()

    idx_t = pltpu.VMEM((SC_CHUNK,), I32)
    row_t = pltpu.VMEM((SC_CHUNK, width), I32)
    sem_t = pltpu.SemaphoreType.DMA
    return pl.kernel(
        body,
        out_type=jax.ShapeDtypeStruct((n_rows, width), I32),
        mesh=mesh,
        scratch_types=[idx_t, row_t, idx_t, row_t, sem_t, sem_t, sem_t, sem_t, sem_t, sem_t],
        name="sc_gather",
    )(table, idx)


def _experts_kernel(te_ref, nu_ref, tv_ref, ne_ref, xs_ref, wg_hbm, wu_hbm, wd_hbm, y_ref,
                    wgs_ref, wus_ref, wds_ref, wgb_ref, wub_ref, wdb_ref, wsem):
    i = pl.program_id(0)
    e = te_ref[i]
    e_prev = te_ref[jnp.maximum(i - 1, 0)]
    new_expert = jnp.logical_or(i == 0, e != e_prev)
    used = i < nu_ref[0]

    def weight_copies(expert):
        return (pltpu.make_async_copy(wg_hbm.at[expert], wgs_ref, wsem),
                pltpu.make_async_copy(wu_hbm.at[expert], wus_ref, wsem),
                pltpu.make_async_copy(wd_hbm.at[expert], wds_ref, wsem))

    @pl.when(i == 0)
    def _():
        for c in weight_copies(e):
            c.start()

    @pl.when(jnp.logical_and(used, new_expert))
    def _():
        for c in weight_copies(e):
            c.wait()
        wgb_ref[...] = wgs_ref[...].astype(BF16)
        wub_ref[...] = wus_ref[...].astype(BF16)
        wdb_ref[...] = wds_ref[...].astype(BF16)

        @pl.when(ne_ref[i] >= 0)
        def _():
            for c in weight_copies(ne_ref[i]):
                c.start()

    @pl.when(used)
    def _():
        row = lax.broadcasted_iota(I32, xs_ref.shape, 0)
        x_lo, x_hi = _unpack_rows(jnp.where(row < tv_ref[i], xs_ref[...], 0))
        xb = jnp.concatenate([x_lo.astype(BF16), x_hi.astype(BF16)], axis=1)
        gate = jnp.dot(xb, wgb_ref[...], preferred_element_type=F32)
        up = jnp.dot(xb, wub_ref[...], preferred_element_type=F32)
        hid = (gate * jax.nn.sigmoid(gate) * up).astype(BF16)
        y_ref[...] = _pack_rows(jnp.dot(hid, wdb_ref[...], preferred_element_type=F32))

    @pl.when(jnp.logical_not(used))
    def _():
        y_ref[...] = jnp.zeros_like(y_ref)


def _experts(xs, tile_e, n_used, tile_valid, next_e, w_gate, w_up, w_down):
    cap, dw = xs.shape
    d = 2 * dw
    tm = TM_EXPERT
    n_tiles = cap // tm

    def x_map(i, te, nu, tv, ne):
        return (jnp.minimum(i, jnp.maximum(nu[0] - 1, 0)), 0)

    return pl.pallas_call(
        _experts_kernel,
        grid_spec=pltpu.PrefetchScalarGridSpec(
            num_scalar_prefetch=4,
            grid=(n_tiles,),
            in_specs=[
                pl.BlockSpec((tm, dw), x_map),
                pl.BlockSpec(memory_space=pl.ANY),
                pl.BlockSpec(memory_space=pl.ANY),
                pl.BlockSpec(memory_space=pl.ANY),
            ],
            out_specs=pl.BlockSpec((tm, dw), lambda i, te, nu, tv, ne: (i, 0)),
            scratch_shapes=[pltpu.VMEM((d, D_FF), F32), pltpu.VMEM((d, D_FF), F32), pltpu.VMEM((D_FF, d), F32),
                            pltpu.VMEM((d, D_FF), BF16), pltpu.VMEM((d, D_FF), BF16), pltpu.VMEM((D_FF, d), BF16),
                            pltpu.SemaphoreType.DMA(())],
        ),
        out_shape=jax.ShapeDtypeStruct((cap, dw), I32),
        compiler_params=pltpu.CompilerParams(
            dimension_semantics=("arbitrary",), vmem_limit_bytes=VMEM_LIMIT),
        name="experts",
    )(tile_e, n_used, tile_valid, next_e, xs, w_gate, w_up, w_down)


def _combine_stream_kernel(x2_ref, y1_ref, y2_ref, rw_ref, gf_ref, out_ref):
    w_t = jnp.transpose(rw_ref[...])
    a_lo, a_hi = _unpack_rows(y1_ref[...])
    b_lo, b_hi = _unpack_rows(y2_ref[...])
    w1 = w_t[:, 0:1]
    w2 = w_t[:, 1:2]
    moe = jnp.concatenate([w1 * a_lo + w2 * b_lo, w1 * a_hi + w2 * b_hi], axis=1)
    x3 = x2_ref[...] + moe
    ms = jnp.mean(x3 * x3, axis=-1, keepdims=True)
    out_ref[...] = x3 * lax.rsqrt(ms + RMS_EPS) * gf_ref[...]


def _combine_stream(x2, yg, rw, norm_final):
    b_sz, s_len, d = x2.shape
    tc = TC_COMBINE
    n_s = s_len // tc
    n_steps = b_sz * n_s
    return pl.pallas_call(
        _combine_stream_kernel,
        grid=(n_steps,),
        in_specs=[
            pl.BlockSpec((tc, d), lambda g: (g, 0)),
            pl.BlockSpec((tc, d // 2), lambda g: (g, 0)),
            pl.BlockSpec((tc, d // 2), lambda g: (g + n_steps, 0)),
            pl.BlockSpec((None, 2 * N_HEADS, tc), lambda g: (g // n_s, 0, g % n_s)),
            pl.BlockSpec((1, d), lambda g: (0, 0)),
        ],
        out_specs=pl.BlockSpec((tc, d), lambda g: (g, 0)),
        out_shape=jax.ShapeDtypeStruct((b_sz * s_len, d), F32),
        compiler_params=pltpu.CompilerParams(
            dimension_semantics=("arbitrary",), vmem_limit_bytes=VMEM_LIMIT),
        name="combine_stream",
    )(x2.reshape(b_sz * s_len, d), yg, yg, rw, norm_final[None, :]).reshape(b_sz, s_len, d)


TM_EXPERT_SHIFT = TM_EXPERT.bit_length() - 1
assert 1 << TM_EXPERT_SHIFT == TM_EXPERT


def _plan_kernel(cnt_ref, ri_ref, pos_ref, meta_ref):
    tm = TM_EXPERT
    n_e, lanes = cnt_ref.shape
    cnt = cnt_ref[...]
    padded = lax.shift_right_logical(cnt + (tm - 1), TM_EXPERT_SHIFT) * tm
    sub = lax.broadcasted_iota(I32, (n_e, lanes), 0)
    lane = lax.broadcasted_iota(I32, (n_e, lanes), 1)
    padded_row = jnp.sum(jnp.where(sub == lane, padded, 0), axis=0, keepdims=True)
    below = jnp.where(lane <= sub, jnp.broadcast_to(padded_row, (n_e, lanes)), 0)
    pends = jnp.sum(below.astype(F32), axis=1, keepdims=True).astype(I32)
    seg = pends - padded[:, 0:1]
    tok_end = seg + cnt[:, 0:1]
    total = pends[n_e - 1:n_e, :]

    n_b, _, s_len = ri_ref.shape
    e_ids = lax.broadcasted_iota(I32, (n_e, s_len), 0)
    for b in range(n_b):
        for k in range(2):
            start = jnp.sum(jnp.where(ri_ref[b, k:k + 1, :] == e_ids, seg, 0), axis=0, keepdims=True)
            pos_ref[k, b:b + 1, :] = ri_ref[b, 2 + k:3 + k, :] + start

    n_t = meta_ref.shape[1]
    t_start = lax.broadcasted_iota(I32, (1, n_t), 1) * tm
    experts_before = lambda row: jnp.minimum(
        jnp.sum(jnp.where(row >= pends, 1, 0), axis=0, keepdims=True), n_e - 1)
    tile_e = experts_before(t_start)
    mine = tile_e == lax.broadcasted_iota(I32, (n_e, n_t), 0)
    pick = lambda col: jnp.sum(jnp.where(mine, col, 0), axis=0, keepdims=True)
    tile_valid = jnp.clip(pick(tok_end) - t_start, 0, tm)
    seg_end = pick(pends)
    next_e = jnp.where(seg_end < total, experts_before(seg_end), -1)
    n_used = jnp.broadcast_to(lax.shift_right_logical(total, TM_EXPERT_SHIFT), (1, n_t))
    zero = jnp.zeros_like(tile_e)
    meta_ref[...] = jnp.concatenate([tile_e, tile_valid, next_e, n_used, zero, zero, zero, zero], axis=0)


def _plan(ri, counts, n_tiles):
    b_sz, _, s_len = ri.shape
    assert n_tiles * TM_EXPERT < 2 ** 24
    n_t = -(-n_tiles // LANES) * LANES
    pos, meta = pl.pallas_call(
        _plan_kernel,
        out_shape=[jax.ShapeDtypeStruct((2, b_sz, s_len), I32), jax.ShapeDtypeStruct((8, n_t), I32)],
        name="plan",
    )(counts, ri)
    return pos.reshape(-1), meta[0, :n_tiles], meta[3, :1], meta[1, :n_tiles], meta[2, :n_tiles]


def _layer(x, norm_mix, w_in, b_in, conv_w, w_out, norm_ffn, w_group, b_group, w_expert, b_expert,
           w_gate, w_up, w_down, norm_out):
    b_sz, s_len, d = x.shape
    n_tok = b_sz * s_len
    a, qvt, gt = _inproj(x, norm_mix, w_in, b_in)

    pad_g = EXPERT_COL0 - N_GROUPS
    pad_e = ROUTER_COLS - EXPERT_COL0 - N_EXPERTS
    w_router = jnp.concatenate([w_group, jnp.zeros((d, pad_g), F32), w_expert, jnp.zeros((d, pad_e), F32)], axis=1)
    b_router = jnp.concatenate([b_group, jnp.zeros((pad_g,), F32), b_expert, jnp.zeros((pad_e,), F32)])[None, :]
    x2, hn2, ri, rw, counts = _mixer(x, a, qvt, gt, conv_w, w_out, norm_ffn, w_router, b_router)

    n_tiles = (n_tok * 2) // TM_EXPERT + N_EXPERTS
    cap = n_tiles * TM_EXPERT
    pos_flat, tile_e, n_used, tile_valid, next_e = _plan(ri, counts, n_tiles)
    xs = _sc_scatter_rows(hn2.reshape(n_tok, d // 2), pos_flat, cap)
    y = _experts(xs, tile_e, n_used, tile_valid, next_e, w_gate, w_up, w_down)
    yg = _sc_gather_rows(y, pos_flat)
    return _combine_stream(x2, yg, rw, norm_out)


def kernel(x, norm_mix, w_in, b_in, conv_w, w_out, norm_ffn, w_group, b_group, w_expert, b_expert,
           w_gate, w_up, w_down, norm_final):
    depth = norm_mix.shape[0]
    assert depth == 1, "final-norm fusion below assumes a single layer"
    assert x.shape[-1] == 2 * D_MLSTM
    assert all(x.shape[1] % t == 0 for t in (TS_MIXER, TM_INPROJ, TC_COMBINE))
    return _layer(x, norm_mix[0], w_in[0], b_in[0], conv_w[0], w_out[0], norm_ffn[0],
                  w_group[0], b_group[0], w_expert[0], b_expert[0],
                  w_gate[0], w_up[0], w_down[0], norm_final)
```

```python
import jax
import jax.numpy as jnp
from jax import lax
from jax.experimental import pallas as pl
from jax.experimental.pallas import tpu as pltpu
from jax.experimental.pallas import tpu_sc as plsc

F32 = jnp.float32
BF16 = jnp.bfloat16
I32 = jnp.int32

N_HEADS = 4
HEAD_DIM = 128
D_MLSTM = N_HEADS * HEAD_DIM
D_CONV = 512
N_GROUPS = 4
EXPERTS_PER_GROUP = 8
N_EXPERTS = N_GROUPS * EXPERTS_PER_GROUP
D_FF = 512
RMS_EPS = 1e-6
Q_SCALE = HEAD_DIM ** -0.5

LANES = 128
CHUNK = LANES
AUG_ROWS = 16
ROUTER_COLS = 128
EXPERT_COL0 = 8

TM_INPROJ = 1024
TS_MIXER = 1024
OUT_COLS = 256
TM_EXPERT = 512
TC_COMBINE = 1024

VMEM_LIMIT = 56 * 1024 * 1024


HI_HALF = -65536


def _pack_rows(x):
    w = x.shape[1] // 2
    lo = lax.shift_right_logical(lax.bitcast_convert_type(x[:, :w].astype(BF16).astype(F32), I32), 16)
    hi = lax.bitcast_convert_type(x[:, w:].astype(BF16).astype(F32), I32) & HI_HALF
    return lo | hi


def _unpack_rows(words):
    lo = lax.bitcast_convert_type(words << 16, F32)
    hi = lax.bitcast_convert_type(words & HI_HALF, F32)
    return lo, hi


def _nt_dot(a, b):
    return lax.dot_general(a, b, (((1,), (1,)), ((), ())), preferred_element_type=F32)


def _inproj_kernel(x_ref, g_ref, w_ref, ba_ref, bt_ref, a_ref, qvt_ref, gt_ref, wt_ref, wc_ref):
    dm = D_MLSTM
    d = w_ref.shape[0]
    g0 = 4 * dm
    n_gate = 2 * N_HEADS
    step_cols = 2 * LANES

    @pl.when(jnp.logical_and(pl.program_id(0) == 0, pl.program_id(1) == 0))
    def _():
        for j, c0 in enumerate((0, 2 * dm)):
            for h in range(0, dm, step_cols):
                blk = w_ref[:, c0 + h:c0 + h + step_cols].astype(F32)
                wt_ref[j * dm + h:j * dm + h + step_cols, :] = jnp.transpose(blk).astype(BF16)
        gates_t = jnp.transpose(w_ref[:, g0:g0 + LANES].astype(F32))[0:AUG_ROWS, :]
        row = lax.broadcasted_iota(I32, (AUG_ROWS, d), 0)
        wt_ref[2 * dm:, :] = jnp.where(row < n_gate, gates_t, 0.0).astype(BF16)
        for r in range(0, d, step_cols):
            tail = w_ref[r:r + step_cols, g0:].astype(F32)
            wc_ref[r:r + step_cols, :] = tail[:, n_gate:n_gate + 3 * D_CONV].astype(BF16)

    x = x_ref[...]
    ms = jnp.mean(x * x, axis=-1, keepdims=True)
    hn = (x * lax.rsqrt(ms + RMS_EPS) * g_ref[...]).astype(BF16)
    blocks = [w_ref.at[:, dm:2 * dm], w_ref.at[:, 3 * dm:4 * dm]]
    blocks += [wc_ref.at[:, j * D_CONV:(j + 1) * D_CONV] for j in range(3)]
    for j, blk_ref in enumerate(blocks):
        sl = slice(j * dm, (j + 1) * dm)
        acc = jnp.dot(hn, blk_ref[...], preferred_element_type=F32) + ba_ref[:, sl]
        a_ref[:, sl] = acc.astype(BF16)
    tr = _nt_dot(wt_ref[...], hn) + bt_ref[...]
    q_rows = slice(0, D_MLSTM)
    v_rows = slice(D_MLSTM, 2 * D_MLSTM)
    qvt_ref[q_rows, :] = (tr[q_rows, :] * Q_SCALE).astype(BF16)
    qvt_ref[v_rows, :] = tr[v_rows, :].astype(BF16)
    gt_ref[...] = tr[2 * D_MLSTM:2 * D_MLSTM + 2 * N_HEADS, :]


def _inproj(x, norm_mix, w_in, b_in):
    b_sz, s_len, d = x.shape
    tm = TM_INPROJ
    dm = D_MLSTM
    g0 = 4 * dm
    c0 = g0 + 2 * N_HEADS
    assert D_CONV == dm and w_in.shape[1] == c0 + 3 * D_CONV
    bias_a = jnp.concatenate([b_in[dm:2 * dm], b_in[3 * dm:4 * dm], b_in[c0:c0 + 3 * D_CONV]])[None, :]
    n_gate = 2 * N_HEADS
    pad_rows = AUG_ROWS - n_gate
    b_t = jnp.concatenate([b_in[0:dm], b_in[2 * dm:3 * dm], b_in[g0:c0],
                           jnp.zeros((pad_rows,), b_in.dtype)])[:, None]
    n_t = 2 * dm + AUG_ROWS
    n_a = 2 * dm + 3 * D_CONV
    n_in = w_in.shape[1]
    return pl.pallas_call(
        _inproj_kernel,
        grid=(b_sz, s_len // tm),
        in_specs=[
            pl.BlockSpec((None, tm, d), lambda b, s: (b, s, 0)),
            pl.BlockSpec((1, d), lambda b, s: (0, 0)),
            pl.BlockSpec((d, n_in), lambda b, s: (0, 0)),
            pl.BlockSpec((1, n_a), lambda b, s: (0, 0)),
            pl.BlockSpec((n_t, 1), lambda b, s: (0, 0)),
        ],
        out_specs=[
            pl.BlockSpec((None, tm, n_a), lambda b, s: (b, s, 0)),
            pl.BlockSpec((None, 2 * dm, tm), lambda b, s: (b, 0, s)),
            pl.BlockSpec((None, 2 * N_HEADS, tm), lambda b, s: (b, 0, s)),
        ],
        out_shape=[
            jax.ShapeDtypeStruct((b_sz, s_len, n_a), BF16),
            jax.ShapeDtypeStruct((b_sz, 2 * dm, s_len), BF16),
            jax.ShapeDtypeStruct((b_sz, 2 * N_HEADS, s_len), F32),
        ],
        scratch_shapes=[pltpu.VMEM((n_t, d), BF16),
                        pltpu.VMEM((d, 3 * D_CONV), BF16)],
        compiler_params=pltpu.CompilerParams(
            dimension_semantics=("arbitrary", "arbitrary"), vmem_limit_bytes=VMEM_LIMIT),
        name="inproj",
    )(x, norm_mix[None, :], w_in.astype(BF16), bias_a, b_t)


def _mixer_kernel(k_ref, o_ref, cb_ref, cc_ref, cx_ref, qt_ref, vt_ref, gt_ref, x_ref,
                  cw_ref, wo_ref, g2_ref, wr_ref, br_ref,
                  x2_ref, hn2_ref, ri_ref, rw_ref, cnt_out_ref,
                  ct_ref, m_ref, carry_ref, cnt_ref, mix_ref):
    ts = x_ref.shape[0]
    n_chunks = ts // CHUNK
    first_tile = pl.program_id(1) == 0

    @pl.when(first_tile)
    def _():
        ct_ref[...] = jnp.zeros_like(ct_ref)
        m_ref[...] = jnp.zeros_like(m_ref)
        carry_ref[...] = jnp.zeros_like(carry_ref)

    @pl.when(jnp.logical_and(first_tile, pl.program_id(0) == 0))
    def _():
        cnt_ref[...] = jnp.zeros_like(cnt_ref)

    gates = gt_ref[...]
    li_all = gates[0:N_HEADS]
    fg = gates[N_HEADS:2 * N_HEADS]
    lf_all = jnp.minimum(fg, 0.0) - jnp.log1p(jnp.exp(-jnp.abs(fg)))
    key_j = lax.broadcasted_iota(I32, (CHUNK, CHUNK), 0)
    qry_i = lax.broadcasted_iota(I32, (CHUNK, CHUNK), 1)
    causal_t = key_j <= qry_i
    neg_inf = jnp.float32(-jnp.inf)
    ones_rows = (lax.broadcasted_iota(I32, (AUG_ROWS, CHUNK), 0) == 0).astype(BF16)

    prefix_ones = (key_j <= qry_i).astype(BF16)

    lf_hi = lf_all.astype(BF16)
    rem = lf_all - lf_hi.astype(F32)
    lf_mid = rem.astype(BF16)
    lf_lo = (rem - lf_mid.astype(F32)).astype(BF16)
    zero_bf = jnp.zeros((N_HEADS, CHUNK), BF16)
    parts = []
    for c in range(n_chunks):
        rows = slice(c * CHUNK, (c + 1) * CHUNK)
        parts += [lf_hi[:, rows], lf_mid[:, rows], lf_lo[:, rows], zero_bf]
    cum = jnp.dot(jnp.concatenate(parts, axis=0), prefix_ones, preferred_element_type=F32)

    m_prev = m_ref[0:N_HEADS, 0:1]
    for c in range(n_chunks):
        rows = slice(c * CHUNK, (c + 1) * CHUNK)
        li = li_all[:, rows]
        r0 = 4 * N_HEADS * c
        b_cum = (cum[r0:r0 + N_HEADS] + cum[r0 + N_HEADS:r0 + 2 * N_HEADS]
                 + cum[r0 + 2 * N_HEADS:r0 + 3 * N_HEADS])
        g_tot = b_cum[:, CHUNK - 1:CHUNK]
        a_end = g_tot - b_cum + li
        m_loc = jnp.max(a_end, axis=1, keepdims=True)
        u = li - b_cum
        m_new = jnp.maximum(g_tot + m_prev, m_loc)
        decay = jnp.exp(g_tot + m_prev - m_new)
        s_fac = jnp.exp(m_loc - m_new)
        w_key = jnp.exp(a_end - m_loc)
        u_cols = jnp.transpose(jnp.concatenate([u, jnp.zeros_like(u)], axis=0))

        def pair_row(x, h0):
            parts = [jnp.broadcast_to(x[h:h + 1, :], (1, CHUNK)) for h in (h0, h0 + 1)]
            return jnp.concatenate(parts, axis=1)

        def block_diag(a, b):
            z = jnp.zeros_like(a)
            return jnp.concatenate([jnp.concatenate([a, z], axis=1), jnp.concatenate([z, b], axis=1)], axis=0)

        for p in range(N_HEADS // 2):
            h0 = 2 * p
            pc = slice(h0 * HEAD_DIM, (h0 + 2) * HEAD_DIM)
            lo_l, hi_l = slice(0, CHUNK), slice(CHUNK, 2 * CHUNK)
            k2 = k_ref[rows, pc]
            q2 = qt_ref[pc, rows]
            v2 = vt_ref[pc, rows]
            s2 = jnp.dot(k2, block_diag(q2[0:HEAD_DIM], q2[HEAD_DIM:]), preferred_element_type=F32)
            u_t = jnp.concatenate([jnp.where(causal_t, u_cols[:, h:h + 1], neg_inf) for h in (h0, h0 + 1)],
                                  axis=1)
            mp = pair_row(m_prev, h0)
            m_tot = jnp.maximum(jnp.max(u_t, axis=0, keepdims=True), mp)
            f_inter = jnp.exp(mp - m_tot)
            floor = jnp.exp(-pair_row(b_cum, h0) - m_tot)
            p2 = (s2 * jnp.exp(u_t - m_tot)).astype(BF16)
            qf = [(q2[r].astype(F32) * f_inter[:, l]).astype(BF16)
                  for r, l in ((slice(0, HEAD_DIM), lo_l), (slice(HEAD_DIM, None), hi_l))]
            c_old = ct_ref[p]
            cb16 = c_old.astype(BF16)
            vt_aug = [jnp.concatenate([v2[r], ones_rows], axis=0)
                      for r in (slice(0, HEAD_DIM), slice(HEAD_DIM, None))]
            lhs = jnp.concatenate([vt_aug[0], cb16[:, lo_l], vt_aug[1], cb16[:, hi_l]], axis=1)
            rhs = block_diag(jnp.concatenate([p2[:, lo_l], qf[0]], axis=0),
                             jnp.concatenate([p2[:, hi_l], qf[1]], axis=0))
            r2 = jnp.dot(lhs, rhs, preferred_element_type=F32)
            w2 = pair_row(w_key, h0)
            vtw = jnp.concatenate([(vt_aug[0].astype(F32) * w2[:, lo_l]).astype(BF16),
                                   (vt_aug[1].astype(F32) * w2[:, hi_l]).astype(BF16)], axis=1)
            kv = jnp.dot(vtw, block_diag(k2[:, lo_l], k2[:, hi_l]), preferred_element_type=F32)
            ct_ref[p] = pair_row(decay, h0) * c_old + pair_row(s_fac, h0) * kv

            den = r2[HEAD_DIM:HEAD_DIM + 1, :]
            inv = 1.0 / jnp.maximum(jnp.abs(den), floor)
            h_t = r2[0:HEAD_DIM, :] * inv
            for h, l in ((h0, lo_l), (h0 + 1, hi_l)):
                hc = slice(h * HEAD_DIM, (h + 1) * HEAD_DIM)
                o_gate = jax.nn.sigmoid(o_ref[rows, hc].astype(F32))
                mix_ref[rows, hc] = (o_gate * jnp.transpose(h_t[:, l])).astype(BF16)
        m_prev = m_new
    m_ref[0:N_HEADS, :] = jnp.broadcast_to(m_prev, (N_HEADS, LANES))

    u_c = cc_ref[...].astype(F32) * cx_ref[...].astype(F32)
    rid = lax.broadcasted_iota(I32, u_c.shape, 0)
    prev1 = carry_ref[0:1, :]
    prev2 = carry_ref[1:2, :]
    u1 = jnp.where(rid == 0, prev1, pltpu.roll(u_c, 1, axis=0))
    u2 = jnp.where(rid == 0, prev2, jnp.where(rid == 1, prev1, pltpu.roll(u_c, 2, axis=0)))
    z = cw_ref[0:1, :] * u2 + cw_ref[1:2, :] * u1 + cw_ref[2:3, :] * u_c
    mix_ref[:, D_MLSTM:] = (cb_ref[...].astype(F32) * z).astype(BF16)
    carry_ref[0:1, :] = u_c[ts - 1:ts, :]
    carry_ref[1:2, :] = u_c[ts - 2:ts - 1, :]

    d_model = x_ref.shape[1]
    mix = mix_ref[...]
    ssq = jnp.zeros((ts, 1), F32)
    for j in range(d_model // OUT_COLS):
        cs = slice(j * OUT_COLS, (j + 1) * OUT_COLS)
        x2c = x_ref[:, cs] + jnp.dot(mix, wo_ref[:, cs], preferred_element_type=F32)
        x2_ref[:, cs] = x2c
        ssq = ssq + jnp.sum(x2c * x2c, axis=-1, keepdims=True)
    hn2 = x2_ref[...] * lax.rsqrt(ssq * (1.0 / d_model) + RMS_EPS) * g2_ref[...]
    hn2_ref[...] = _pack_rows(hn2)

    logits = jnp.dot(hn2.astype(BF16), wr_ref[...], preferred_element_type=F32) + br_ref[...]
    lt = jnp.transpose(logits)
    gl = lt[0:N_GROUPS]
    gmax = jnp.max(gl, axis=0, keepdims=True)
    gi = lax.broadcasted_iota(I32, gl.shape, 0)
    g_sel = jnp.min(jnp.where(gl == gmax, gi, N_GROUPS), axis=0, keepdims=True)
    p_g = 1.0 / jnp.sum(jnp.exp(gl - gmax), axis=0, keepdims=True)
    in_g = lt[EXPERT_COL0:EXPERT_COL0 + EXPERTS_PER_GROUP]
    for g in range(1, N_GROUPS):
        lo = EXPERT_COL0 + g * EXPERTS_PER_GROUP
        in_g = jnp.where(g_sel == g, lt[lo:lo + EXPERTS_PER_GROUP], in_g)
    ei = lax.broadcasted_iota(I32, in_g.shape, 0)
    v1 = jnp.max(in_g, axis=0, keepdims=True)
    i1 = jnp.min(jnp.where(in_g == v1, ei, EXPERTS_PER_GROUP), axis=0, keepdims=True)
    rest = jnp.where(ei == i1, neg_inf, in_g)
    v2 = jnp.max(rest, axis=0, keepdims=True)
    i2 = jnp.min(jnp.where(rest == v2, ei, EXPERTS_PER_GROUP), axis=0, keepdims=True)
    t = jnp.exp(v2 - v1)
    w1 = p_g / (1.0 + t)
    w2 = p_g * t / (1.0 + t)
    e1 = g_sel * EXPERTS_PER_GROUP + i1
    e2 = g_sel * EXPERTS_PER_GROUP + i2

    eio = lax.broadcasted_iota(I32, (N_EXPERTS, LANES), 0)
    base = cnt_ref[:, 0:1]
    zero_i = jnp.zeros((1, LANES), I32)
    n_blk = ts // LANES
    hits = []
    for k in range(n_blk):
        ls = slice(k * LANES, (k + 1) * LANES)
        hits.append((eio == e1[:, ls], eio == e2[:, ls]))
    onehots = [jnp.where(jnp.logical_or(a1, a2), 1.0, 0.0) for a1, a2 in hits]
    incl_all = jnp.dot(jnp.concatenate(onehots, axis=0).astype(BF16), prefix_ones,
                       preferred_element_type=F32).astype(I32)
    for k in range(n_blk):
        ls = slice(k * LANES, (k + 1) * LANES)
        is1, is2 = hits[k]
        onehot = onehots[k].astype(I32)
        incl = incl_all[k * N_EXPERTS:(k + 1) * N_EXPERTS]
        rank_e = base + incl - onehot
        r1 = jnp.sum(jnp.where(is1, rank_e, 0), axis=0, keepdims=True)
        r2 = jnp.sum(jnp.where(is2, rank_e, 0), axis=0, keepdims=True)
        ri_ref[:, ls] = jnp.concatenate(
            [e1[:, ls], e2[:, ls], r1, r2, zero_i, zero_i, zero_i, zero_i], axis=0)
        base = base + incl[:, LANES - 1:LANES]
    zero_f = jnp.zeros_like(w1)
    rw_ref[...] = jnp.concatenate([w1, w2, zero_f, zero_f, zero_f, zero_f, zero_f, zero_f], axis=0)
    cnt_new = jnp.broadcast_to(base, (N_EXPERTS, LANES))
    cnt_ref[...] = cnt_new
    cnt_out_ref[...] = cnt_new


def _mixer(x, a, qvt, gt, conv_w, w_out, norm_ffn, w_router, b_router):
    b_sz, s_len, d = x.shape
    ts = TS_MIXER
    dm = D_MLSTM

    def a_spec(j):
        return pl.BlockSpec((None, ts, dm), lambda b, s, j=j: (b, s, j))

    tok = lambda w: pl.BlockSpec((None, ts, w), lambda b, s: (b, s, 0))
    rowsp = pl.BlockSpec((None, 2 * N_HEADS, ts), lambda b, s: (b, 0, s))
    const = lambda shape: pl.BlockSpec(shape, lambda b, s: tuple(0 for _ in shape))
    return pl.pallas_call(
        _mixer_kernel,
        grid=(b_sz, s_len // ts),
        in_specs=[a_spec(0), a_spec(1), a_spec(2), a_spec(3), a_spec(4),
                  pl.BlockSpec((None, dm, ts), lambda b, s: (b, 0, s)),
                  pl.BlockSpec((None, dm, ts), lambda b, s: (b, 1, s)),
                  rowsp, tok(d),
                  const((3, D_CONV)), const((d, d)), const((1, d)),
                  const((d, ROUTER_COLS)), const((1, ROUTER_COLS))],
        out_specs=[tok(d), tok(d // 2), rowsp, rowsp, const((N_EXPERTS, LANES))],
        out_shape=[
            jax.ShapeDtypeStruct((b_sz, s_len, d), F32),
            jax.ShapeDtypeStruct((b_sz, s_len, d // 2), I32),
            jax.ShapeDtypeStruct((b_sz, 2 * N_HEADS, s_len), I32),
            jax.ShapeDtypeStruct((b_sz, 2 * N_HEADS, s_len), F32),
            jax.ShapeDtypeStruct((N_EXPERTS, LANES), I32),
        ],
        scratch_shapes=[
            pltpu.VMEM((N_HEADS // 2, HEAD_DIM + AUG_ROWS, 2 * HEAD_DIM), F32),
            pltpu.VMEM((2 * N_HEADS, LANES), F32),
            pltpu.VMEM((8, D_CONV), F32),
            pltpu.VMEM((N_EXPERTS, LANES), I32),
            pltpu.VMEM((ts, d), BF16),
        ],
        compiler_params=pltpu.CompilerParams(
            dimension_semantics=("arbitrary", "arbitrary"), vmem_limit_bytes=VMEM_LIMIT),
        name="mixer",
    )(a, a, a, a, a, qvt, qvt, gt, x, conv_w, w_out.astype(BF16), norm_ffn[None, :], w_router.astype(BF16), b_router)


SC_CORES = 2
SC_SUBCORES = 16
SC_WORKERS = SC_CORES * SC_SUBCORES
SC_CHUNK = 64


def _sc_worker_id():
    return lax.axis_index("s") * SC_CORES + lax.axis_index("c")


def _sc_scatter_rows(rows, idx, n_out):
    n_rows, width = rows.shape
    per_worker = n_rows // SC_WORKERS
    n_chunks = per_worker // SC_CHUNK
    assert n_chunks % 2 == 0
    mesh = plsc.VectorSubcoreMesh(core_axis_name="c", subcore_axis_name="s")

    def body(rows_hbm, idx_hbm, out_hbm, ia0, ib0, r0, ia1, ib1, r1, ls0, ls1, ss0, ss1):
        wid = _sc_worker_id()
        bufs = ((ia0, ib0, r0, ls0, ss0), (ia1, ib1, r1, ls1, ss1))

        def loads(j, b):
            ia, ib, r, ls, _ = bufs[b]
            base = pl.multiple_of(wid * per_worker + j * SC_CHUNK, SC_CHUNK)
            return (pltpu.make_async_copy(rows_hbm.at[pl.ds(base, SC_CHUNK)], r, ls),
                    pltpu.make_async_copy(idx_hbm.at[pl.ds(base, SC_CHUNK)], ia, ls),
                    pltpu.make_async_copy(idx_hbm.at[pl.ds(n_rows + base, SC_CHUNK)], ib, ls))

        def scatters(b):
            ia, ib, r, _, ss = bufs[b]
            return (pltpu.make_async_copy(r, out_hbm.at[ia], ss),
                    pltpu.make_async_copy(r, out_hbm.at[ib], ss))

        for c in loads(0, 0):
            c.start()

        @pl.loop(0, n_chunks, step=2)
        def _(j0):
            for b in range(2):
                j = j0 + b
                for c in loads(j, b):
                    c.wait()

                @pl.when(j >= 1)
                def _():
                    for c in scatters(1 - b):
                        c.wait()

                @pl.when(j + 1 < n_chunks)
                def _():
                    for c in loads(j + 1, 1 - b):
                        c.start()

                for c in scatters(b):
                    c.start()

        for c in scatters((n_chunks - 1) % 2):
            c.wait()

    idx_t = pltpu.VMEM((SC_CHUNK,), I32)
    row_t = pltpu.VMEM((SC_CHUNK, width), I32)
    sem_t = pltpu.SemaphoreType.DMA
    return pl.kernel(
        body,
        out_type=jax.ShapeDtypeStruct((n_out, width), I32),
        mesh=mesh,
        scratch_types=[idx_t, idx_t, row_t, idx_t, idx_t, row_t, sem_t, sem_t, sem_t, sem_t],
        name="sc_scatter",
    )(rows, idx)


def _sc_gather_rows(table, idx):
    n_rows = idx.shape[0]
    width = table.shape[1]
    per_worker = n_rows // SC_WORKERS
    n_chunks = per_worker // SC_CHUNK
    assert n_chunks % 2 == 0
    mesh = plsc.VectorSubcoreMesh(core_axis_name="c", subcore_axis_name="s")

    def body(table_hbm, idx_hbm, out_hbm, i0, r0, i1, r1, is0, is1, gs0, gs1, ws0, ws1):
        wid = _sc_worker_id()
        bufs = ((i0, r0, is0, gs0, ws0), (i1, r1, is1, gs1, ws1))

        def chunk_base(j):
            return pl.multiple_of(wid * per_worker + j * SC_CHUNK, SC_CHUNK)

        def idx_load(j, b):
            return pltpu.make_async_copy(idx_hbm.at[pl.ds(chunk_base(j), SC_CHUNK)], bufs[b][0], bufs[b][2])

        def gather(b):
            return pltpu.make_async_copy(table_hbm.at[bufs[b][0]], bufs[b][1], bufs[b][3])

        def writeback(j, b):
            return pltpu.make_async_copy(bufs[b][1], out_hbm.at[pl.ds(chunk_base(j), SC_CHUNK)], bufs[b][4])

        idx_load(0, 0).start()
        idx_load(0, 0).wait()
        gather(0).start()
        idx_load(1, 1).start()

        @pl.loop(0, n_chunks, step=2)
        def _(j0):
            for b in range(2):
                j = j0 + b

                @pl.when(j + 1 < n_chunks)
                def _():
                    idx_load(j + 1, 1 - b).wait()

                    @pl.when(j >= 1)
                    def _():
                        writeback(j - 1, 1 - b).wait()
                    gather(1 - b).start()

                gather(b).wait()
                writeback(j, b).start()

                @pl.when(j + 2 < n_chunks)
                def _():
                    idx_load(j + 2, b).start()

        writeback(n_chunks - 2, 0).wait()
        writeback(n_chunks - 1, 1).wait()

    idx_t = pltpu.VMEM((SC_CHUNK,), I32)
    row_t = pltpu.VMEM((SC_CHUNK, width), I32)
    sem_t = pltpu.SemaphoreType.DMA
    return pl.kernel(
        body,
        out_type=jax.ShapeDtypeStruct((n_rows, width), I32),
        mesh=mesh,
        scratch_types=[idx_t, row_t, idx_t, row_t, sem_t, sem_t, sem_t, sem_t, sem_t, sem_t],
        name="sc_gather",
    )(table, idx)


def _experts_kernel(te_ref, nu_ref, tv_ref, ne_ref, xs_ref, wg_hbm, wu_hbm, wd_hbm, y_ref,
                    wgs_ref, wus_ref, wds_ref, wgb_ref, wub_ref, wdb_ref, wsem):
    i = pl.program_id(0)
    e = te_ref[i]
    e_prev = te_ref[jnp.maximum(i - 1, 0)]
    new_expert = jnp.logical_or(i == 0, e != e_prev)
    used = i < nu_ref[0]

    def weight_copies(expert):
        return (pltpu.make_async_copy(wg_hbm.at[expert], wgs_ref, wsem),
                pltpu.make_async_copy(wu_hbm.at[expert], wus_ref, wsem),
                pltpu.make_async_copy(wd_hbm.at[expert], wds_ref, wsem))

    @pl.when(i == 0)
    def _():
        for c in weight_copies(e):
            c.start()

    @pl.when(jnp.logical_and(used, new_expert))
    def _():
        for c in weight_copies(e):
            c.wait()
        wgb_ref[...] = wgs_ref[...].astype(BF16)
        wub_ref[...] = wus_ref[...].astype(BF16)
        wdb_ref[...] = wds_ref[...].astype(BF16)

        @pl.when(ne_ref[i] >= 0)
        def _():
            for c in weight_copies(ne_ref[i]):
                c.start()

    @pl.when(used)
    def _():
        row = lax.broadcasted_iota(I32, xs_ref.shape, 0)
        x_lo, x_hi = _unpack_rows(jnp.where(row < tv_ref[i], xs_ref[...], 0))
        xb = jnp.concatenate([x_lo.astype(BF16), x_hi.astype(BF16)], axis=1)
        gate = jnp.dot(xb, wgb_ref[...], preferred_element_type=F32)
        up = jnp.dot(xb, wub_ref[...], preferred_element_type=F32)
        hid = (gate * jax.nn.sigmoid(gate) * up).astype(BF16)
        y_ref[...] = _pack_rows(jnp.dot(hid, wdb_ref[...], preferred_element_type=F32))

    @pl.when(jnp.logical_not(used))
    def _():
        y_ref[...] = jnp.zeros_like(y_ref)


def _experts(xs, tile_e, n_used, tile_valid, next_e, w_gate, w_up, w_down):
    cap, dw = xs.shape
    d = 2 * dw
    tm = TM_EXPERT
    n_tiles = cap // tm

    def x_map(i, te, nu, tv, ne):
        return (jnp.minimum(i, jnp.maximum(nu[0] - 1, 0)), 0)

    return pl.pallas_call(
        _experts_kernel,
        grid_spec=pltpu.PrefetchScalarGridSpec(
            num_scalar_prefetch=4,
            grid=(n_tiles,),
            in_specs=[
                pl.BlockSpec((tm, dw), x_map),
                pl.BlockSpec(memory_space=pl.ANY),
                pl.BlockSpec(memory_space=pl.ANY),
                pl.BlockSpec(memory_space=pl.ANY),
            ],
            out_specs=pl.BlockSpec((tm, dw), lambda i, te, nu, tv, ne: (i, 0)),
            scratch_shapes=[pltpu.VMEM((d, D_FF), F32), pltpu.VMEM((d, D_FF), F32), pltpu.VMEM((D_FF, d), F32),
                            pltpu.VMEM((d, D_FF), BF16), pltpu.VMEM((d, D_FF), BF16), pltpu.VMEM((D_FF, d), BF16),
                            pltpu.SemaphoreType.DMA(())],
        ),
        out_shape=jax.ShapeDtypeStruct((cap, dw), I32),
        compiler_params=pltpu.CompilerParams(
            dimension_semantics=("arbitrary",), vmem_limit_bytes=VMEM_LIMIT),
        name="experts",
    )(tile_e, n_used, tile_valid, next_e, xs, w_gate, w_up, w_down)


def _combine_stream_kernel(x2_ref, y1_ref, y2_ref, rw_ref, gf_ref, out_ref):
    w_t = jnp.transpose(rw_ref[...])
    a_lo, a_hi = _unpack_rows(y1_ref[...])
    b_lo, b_hi = _unpack_rows(y2_ref[...])
    w1 = w_t[:, 0:1]
    w2 = w_t[:, 1:2]
    moe = jnp.concatenate([w1 * a_lo + w2 * b_lo, w1 * a_hi + w2 * b_hi], axis=1)
    x3 = x2_ref[...] + moe
    ms = jnp.mean(x3 * x3, axis=-1, keepdims=True)
    out_ref[...] = x3 * lax.rsqrt(ms + RMS_EPS) * gf_ref[...]


def _combine_stream(x2, yg, rw, norm_final):
    b_sz, s_len, d = x2.shape
    tc = TC_COMBINE
    n_s = s_len // tc
    n_steps = b_sz * n_s
    return pl.pallas_call(
        _combine_stream_kernel,
        grid=(n_steps,),
        in_specs=[
            pl.BlockSpec((tc, d), lambda g: (g, 0)),
            pl.BlockSpec((tc, d // 2), lambda g: (g, 0)),
            pl.BlockSpec((tc, d // 2), lambda g: (g + n_steps, 0)),
            pl.BlockSpec((None, 2 * N_HEADS, tc), lambda g: (g // n_s, 0, g % n_s)),
            pl.BlockSpec((1, d), lambda g: (0, 0)),
        ],
        out_specs=pl.BlockSpec((tc, d), lambda g: (g, 0)),
        out_shape=jax.ShapeDtypeStruct((b_sz * s_len, d), F32),
        compiler_params=pltpu.CompilerParams(
            dimension_semantics=("arbitrary",), vmem_limit_bytes=VMEM_LIMIT),
        name="combine_stream",
    )(x2.reshape(b_sz * s_len, d), yg, yg, rw, norm_final[None, :]).reshape(b_sz, s_len, d)


TM_EXPERT_SHIFT = TM_EXPERT.bit_length() - 1
assert 1 << TM_EXPERT_SHIFT == TM_EXPERT


def _plan_kernel(cnt_ref, ri_ref, pos_ref, meta_ref):
    tm = TM_EXPERT
    n_e, lanes = cnt_ref.shape
    cnt = cnt_ref[...]
    padded = lax.shift_right_logical(cnt + (tm - 1), TM_EXPERT_SHIFT) * tm
    sub = lax.broadcasted_iota(I32, (n_e, lanes), 0)
    lane = lax.broadcasted_iota(I32, (n_e, lanes), 1)
    padded_row = jnp.sum(jnp.where(sub == lane, padded, 0), axis=0, keepdims=True)
    below = jnp.where(lane <= sub, jnp.broadcast_to(padded_row, (n_e, lanes)), 0)
    pends = jnp.sum(below.astype(F32), axis=1, keepdims=True).astype(I32)
    seg = pends - padded[:, 0:1]
    tok_end = seg + cnt[:, 0:1]
    total = pends[n_e - 1:n_e, :]

    n_b, _, s_len = ri_ref.shape
    e_ids = lax.broadcasted_iota(I32, (n_e, s_len), 0)
    for b in range(n_b):
        for k in range(2):
            start = jnp.sum(jnp.where(ri_ref[b, k:k + 1, :] == e_ids, seg, 0), axis=0, keepdims=True)
            pos_ref[k, b:b + 1, :] = ri_ref[b, 2 + k:3 + k, :] + start

    n_t = meta_ref.shape[1]
    t_start = lax.broadcasted_iota(I32, (1, n_t), 1) * tm
    experts_before = lambda row: jnp.minimum(
        jnp.sum(jnp.where(row >= pends, 1, 0), axis=0, keepdims=True), n_e - 1)
    tile_e = experts_before(t_start)
    mine = tile_e == lax.broadcasted_iota(I32, (n_e, n_t), 0)
    pick = lambda col: jnp.sum(jnp.where(mine, col, 0), axis=0, keepdims=True)
    tile_valid = jnp.clip(pick(tok_end) - t_start, 0, tm)
    seg_end = pick(pends)
    next_e = jnp.where(seg_end < total, experts_before(seg_end), -1)
    n_used = jnp.broadcast_to(lax.shift_right_logical(total, TM_EXPERT_SHIFT), (1, n_t))
    zero = jnp.zeros_like(tile_e)
    meta_ref[...] = jnp.concatenate([tile_e, tile_valid, next_e, n_used, zero, zero, zero, zero], axis=0)


def _plan(ri, counts, n_tiles):
    b_sz, _, s_len = ri.shape
    assert n_tiles * TM_EXPERT < 2 ** 24
    n_t = -(-n_tiles // LANES) * LANES
    pos, meta = pl.pallas_call(
        _plan_kernel,
        out_shape=[jax.ShapeDtypeStruct((2, b_sz, s_len), I32), jax.ShapeDtypeStruct((8, n_t), I32)],
        name="plan",
    )(counts, ri)
    return pos.reshape(-1), meta[0, :n_tiles], meta[3, :1], meta[1, :n_tiles], meta[2, :n_tiles]


def _layer(x, norm_mix, w_in, b_in, conv_w, w_out, norm_ffn, w_group, b_group, w_expert, b_expert,
           w_gate, w_up, w_down, norm_out):
    b_sz, s_len, d = x.shape
    n_tok = b_sz * s_len
    a, qvt, gt = _inproj(x, norm_mix, w_in, b_in)

    pad_g = EXPERT_COL0 - N_GROUPS
    pad_e = ROUTER_COLS - EXPERT_COL0 - N_EXPERTS
    w_router = jnp.concatenate([w_group, jnp.zeros((d, pad_g), F32), w_expert, jnp.zeros((d, pad_e), F32)], axis=1)
    b_router = jnp.concatenate([b_group, jnp.zeros((pad_g,), F32), b_expert, jnp.zeros((pad_e,), F32)])[None, :]
    x2, hn2, ri, rw, counts = _mixer(x, a, qvt, gt, conv_w, w_out, norm_ffn, w_router, b_router)

    n_tiles = (n_tok * 2) // TM_EXPERT + N_EXPERTS
    cap = n_tiles * TM_EXPERT
    pos_flat, tile_e, n_used, tile_valid, next_e = _plan(ri, counts, n_tiles)
    xs = _sc_scatter_rows(hn2.reshape(n_tok, d // 2), pos_flat, cap)
    y = _experts(xs, tile_e, n_used, tile_valid, next_e, w_gate, w_up, w_down)
    yg = _sc_gather_rows(y, pos_flat)
    return _combine_stream(x2, yg, rw, norm_out)


def kernel(x, norm_mix, w_in, b_in, conv_w, w_out, norm_ffn, w_group, b_group, w_expert, b_expert,
           w_gate, w_up, w_down, norm_final):
    depth = norm_mix.shape[0]
    assert depth == 1, "final-norm fusion below assumes a single layer"
    assert x.shape[-1] == 2 * D_MLSTM
    assert all(x.shape[1] % t == 0 for t in (TS_MIXER, TM_INPROJ, TC_COMBINE))
    return _layer(x, norm_mix[0], w_in[0], b_in[0], conv_w[0], w_out[0], norm_ffn[0],
                  w_group[0], b_group[0], w_expert[0], b_expert[0],
                  w_gate[0], w_up[0], w_down[0], norm_final)
```

```python
import jax
import jax.numpy as jnp
from jax import lax
from jax.experimental import pallas as pl
from jax.experimental.pallas import tpu as pltpu
from jax.experimental.pallas import tpu_sc as plsc

F32 = jnp.float32
BF16 = jnp.bfloat16
I32 = jnp.int32

N_HEADS = 4
HEAD_DIM = 128
D_MLSTM = N_HEADS * HEAD_DIM
D_CONV = 512
N_GROUPS = 4
EXPERTS_PER_GROUP = 8
N_EXPERTS = N_GROUPS * EXPERTS_PER_GROUP
D_FF = 512
RMS_EPS = 1e-6
Q_SCALE = HEAD_DIM ** -0.5

LANES = 128
CHUNK = LANES
AUG_ROWS = 16
ROUTER_COLS = 128
EXPERT_COL0 = 8

TM_INPROJ = 1024
TS_MIXER = 1024
OUT_COLS = 256
TM_EXPERT = 512
TC_COMBINE = 1024

VMEM_LIMIT = 56 * 1024 * 1024


HI_HALF = -65536


def _pack_rows(x):
    w = x.shape[1] // 2
    lo = lax.shift_right_logical(lax.bitcast_convert_type(x[:, :w].astype(BF16).astype(F32), I32), 16)
    hi = lax.bitcast_convert_type(x[:, w:].astype(BF16).astype(F32), I32) & HI_HALF
    return lo | hi


def _unpack_rows(words):
    lo = lax.bitcast_convert_type(words << 16, F32)
    hi = lax.bitcast_convert_type(words & HI_HALF, F32)
    return lo, hi


def _nt_dot(a, b):
    return lax.dot_general(a, b, (((1,), (1,)), ((), ())), preferred_element_type=F32)


def _inproj_kernel(x_ref, g_ref, w_ref, ba_ref, bt_ref, a_ref, qvt_ref, gt_ref, wt_ref, wa_ref):
    dm = D_MLSTM
    d = w_ref.shape[0]
    g0 = 4 * dm
    n_gate = 2 * N_HEADS
    step = 2 * LANES

    @pl.when(jnp.logical_and(pl.program_id(0) == 0, pl.program_id(1) == 0))
    def _():
        for j, c0 in enumerate((0, 2 * dm)):
            for h in range(0, dm, step):
                wt_ref[j * dm + h:j * dm + h + step, :] = jnp.transpose(
                    w_ref[:, c0 + h:c0 + h + step]).astype(BF16)
        gates_t = jnp.transpose(w_ref[:, g0:g0 + LANES])[0:AUG_ROWS, :]
        row = lax.broadcasted_iota(I32, (AUG_ROWS, d), 0)
        wt_ref[2 * dm:, :] = jnp.where(row < n_gate, gates_t, 0.0).astype(BF16)
        for r in range(0, d, step):
            rows = slice(r, r + step)
            wa_ref[rows, 0:dm] = w_ref[rows, dm:2 * dm].astype(BF16)
            wa_ref[rows, dm:2 * dm] = w_ref[rows, 3 * dm:4 * dm].astype(BF16)
            wa_ref[rows, 2 * dm:] = w_ref[rows, g0:][:, n_gate:n_gate + 3 * D_CONV].astype(BF16)

    x = x_ref[...]
    ms = jnp.mean(x * x, axis=-1, keepdims=True)
    hn = (x * lax.rsqrt(ms + RMS_EPS) * g_ref[...]).astype(BF16)
    for j in range(wa_ref.shape[1] // dm):
        sl = slice(j * dm, (j + 1) * dm)
        acc = jnp.dot(hn, wa_ref[:, sl], preferred_element_type=F32) + ba_ref[:, sl]
        a_ref[:, sl] = acc.astype(BF16)
    tr = _nt_dot(wt_ref[...], hn) + bt_ref[...]
    q_rows = slice(0, D_MLSTM)
    v_rows = slice(D_MLSTM, 2 * D_MLSTM)
    qvt_ref[q_rows, :] = (tr[q_rows, :] * Q_SCALE).astype(BF16)
    qvt_ref[v_rows, :] = tr[v_rows, :].astype(BF16)
    gt_ref[...] = tr[2 * D_MLSTM:2 * D_MLSTM + 2 * N_HEADS, :]


def _inproj(x, norm_mix, w_in, b_in):
    b_sz, s_len, d = x.shape
    tm = TM_INPROJ
    dm = D_MLSTM
    g0 = 4 * dm
    c0 = g0 + 2 * N_HEADS
    assert D_CONV == dm and w_in.shape[1] == c0 + 3 * D_CONV
    bias_a = jnp.concatenate([b_in[dm:2 * dm], b_in[3 * dm:4 * dm], b_in[c0:c0 + 3 * D_CONV]])[None, :]
    n_gate = 2 * N_HEADS
    pad_rows = AUG_ROWS - n_gate
    b_t = jnp.concatenate([b_in[0:dm], b_in[2 * dm:3 * dm], b_in[g0:c0],
                           jnp.zeros((pad_rows,), b_in.dtype)])[:, None]
    n_t = 2 * dm + AUG_ROWS
    n_a = 2 * dm + 3 * D_CONV
    n_in = w_in.shape[1]
    return pl.pallas_call(
        _inproj_kernel,
        grid=(b_sz, s_len // tm),
        in_specs=[
            pl.BlockSpec((None, tm, d), lambda b, s: (b, s, 0)),
            pl.BlockSpec((1, d), lambda b, s: (0, 0)),
            pl.BlockSpec((d, n_in), lambda b, s: (0, 0), pipeline_mode=pl.Buffered(1)),
            pl.BlockSpec((1, n_a), lambda b, s: (0, 0)),
            pl.BlockSpec((n_t, 1), lambda b, s: (0, 0)),
        ],
        out_specs=[
            pl.BlockSpec((None, tm, n_a), lambda b, s: (b, s, 0)),
            pl.BlockSpec((None, 2 * dm, tm), lambda b, s: (b, 0, s)),
            pl.BlockSpec((None, 2 * N_HEADS, tm), lambda b, s: (b, 0, s)),
        ],
        out_shape=[
            jax.ShapeDtypeStruct((b_sz, s_len, n_a), BF16),
            jax.ShapeDtypeStruct((b_sz, 2 * dm, s_len), BF16),
            jax.ShapeDtypeStruct((b_sz, 2 * N_HEADS, s_len), F32),
        ],
        scratch_shapes=[pltpu.VMEM((n_t, d), BF16),
                        pltpu.VMEM((d, n_a), BF16)],
        compiler_params=pltpu.CompilerParams(
            dimension_semantics=("arbitrary", "arbitrary"), vmem_limit_bytes=VMEM_LIMIT),
        name="inproj",
    )(x, norm_mix[None, :], w_in, bias_a, b_t)


def _mixer_kernel(k_ref, o_ref, cb_ref, cc_ref, cx_ref, qt_ref, vt_ref, gt_ref, x_ref,
                  cw_ref, wo_ref, g2_ref, wr_ref, br_ref,
                  x2_ref, hn2_ref, ri_ref, rw_ref, cnt_out_ref,
                  ct_ref, m_ref, carry_ref, cnt_ref, mix_ref):
    ts = x_ref.shape[0]
    n_chunks = ts // CHUNK
    first_tile = pl.program_id(1) == 0

    @pl.when(first_tile)
    def _():
        ct_ref[...] = jnp.zeros_like(ct_ref)
        m_ref[...] = jnp.zeros_like(m_ref)
        carry_ref[...] = jnp.zeros_like(carry_ref)

    @pl.when(jnp.logical_and(first_tile, pl.program_id(0) == 0))
    def _():
        cnt_ref[...] = jnp.zeros_like(cnt_ref)

    gates = gt_ref[...]
    li_all = gates[0:N_HEADS]
    fg = gates[N_HEADS:2 * N_HEADS]
    lf_all = jnp.minimum(fg, 0.0) - jnp.log1p(jnp.exp(-jnp.abs(fg)))
    key_j = lax.broadcasted_iota(I32, (CHUNK, CHUNK), 0)
    qry_i = lax.broadcasted_iota(I32, (CHUNK, CHUNK), 1)
    causal_t = key_j <= qry_i
    neg_inf = jnp.float32(-jnp.inf)
    ones_rows = (lax.broadcasted_iota(I32, (AUG_ROWS, CHUNK), 0) == 0).astype(BF16)

    prefix_ones = (key_j <= qry_i).astype(BF16)

    lf_hi = lf_all.astype(BF16)
    rem = lf_all - lf_hi.astype(F32)
    lf_mid = rem.astype(BF16)
    lf_lo = (rem - lf_mid.astype(F32)).astype(BF16)
    zero_bf = jnp.zeros((N_HEADS, CHUNK), BF16)
    parts = []
    for c in range(n_chunks):
        rows = slice(c * CHUNK, (c + 1) * CHUNK)
        parts += [lf_hi[:, rows], lf_mid[:, rows], lf_lo[:, rows], zero_bf]
    cum = jnp.dot(jnp.concatenate(parts, axis=0), prefix_ones, preferred_element_type=F32)

    m_prev = m_ref[0:N_HEADS, 0:1]
    for c in range(n_chunks):
        rows = slice(c * CHUNK, (c + 1) * CHUNK)
        li = li_all[:, rows]
        r0 = 4 * N_HEADS * c
        b_cum = (cum[r0:r0 + N_HEADS] + cum[r0 + N_HEADS:r0 + 2 * N_HEADS]
                 + cum[r0 + 2 * N_HEADS:r0 + 3 * N_HEADS])
        g_tot = b_cum[:, CHUNK - 1:CHUNK]
        a_end = g_tot - b_cum + li
        m_loc = jnp.max(a_end, axis=1, keepdims=True)
        u = li - b_cum
        m_new = jnp.maximum(g_tot + m_prev, m_loc)
        decay = jnp.exp(g_tot + m_prev - m_new)
        s_fac = jnp.exp(m_loc - m_new)
        w_key = jnp.exp(a_end - m_loc)
        u_cols = jnp.transpose(jnp.concatenate([u, jnp.zeros_like(u)], axis=0))

        def pair_row(x, h0):
            parts = [jnp.broadcast_to(x[h:h + 1, :], (1, CHUNK)) for h in (h0, h0 + 1)]
            return jnp.concatenate(parts, axis=1)

        def block_diag(a, b):
            z = jnp.zeros_like(a)
            return jnp.concatenate([jnp.concatenate([a, z], axis=1), jnp.concatenate([z, b], axis=1)], axis=0)

        for p in range(N_HEADS // 2):
            h0 = 2 * p
            pc = slice(h0 * HEAD_DIM, (h0 + 2) * HEAD_DIM)
            lo_l, hi_l = slice(0, CHUNK), slice(CHUNK, 2 * CHUNK)
            k2 = k_ref[rows, pc]
            q2 = qt_ref[pc, rows]
            v2 = vt_ref[pc, rows]
            s2 = jnp.dot(k2, block_diag(q2[0:HEAD_DIM], q2[HEAD_DIM:]), preferred_element_type=F32)
            u_t = jnp.concatenate([jnp.where(causal_t, u_cols[:, h:h + 1], neg_inf) for h in (h0, h0 + 1)],
                                  axis=1)
            mp = pair_row(m_prev, h0)
            m_tot = jnp.maximum(jnp.max(u_t, axis=0, keepdims=True), mp)
            f_inter = jnp.exp(mp - m_tot)
            floor = jnp.exp(-pair_row(b_cum, h0) - m_tot)
            p2 = (s2 * jnp.exp(u_t - m_tot)).astype(BF16)
            qf = [(q2[r].astype(F32) * f_inter[:, l]).astype(BF16)
                  for r, l in ((slice(0, HEAD_DIM), lo_l), (slice(HEAD_DIM, None), hi_l))]
            c_old = ct_ref[p]
            cb16 = c_old.astype(BF16)
            vt_aug = [jnp.concatenate([v2[r], ones_rows], axis=0)
                      for r in (slice(0, HEAD_DIM), slice(HEAD_DIM, None))]
            lhs = jnp.concatenate([vt_aug[0], cb16[:, lo_l], vt_aug[1], cb16[:, hi_l]], axis=1)
            rhs = block_diag(jnp.concatenate([p2[:, lo_l], qf[0]], axis=0),
                             jnp.concatenate([p2[:, hi_l], qf[1]], axis=0))
            r2 = jnp.dot(lhs, rhs, preferred_element_type=F32)
            w2 = pair_row(w_key, h0)
            vtw = jnp.concatenate([(vt_aug[0].astype(F32) * w2[:, lo_l]).astype(BF16),
                                   (vt_aug[1].astype(F32) * w2[:, hi_l]).astype(BF16)], axis=1)
            kv = jnp.dot(vtw, block_diag(k2[:, lo_l], k2[:, hi_l]), preferred_element_type=F32)
            ct_ref[p] = pair_row(decay, h0) * c_old + pair_row(s_fac, h0) * kv

            den = r2[HEAD_DIM:HEAD_DIM + 1, :]
            inv = 1.0 / jnp.maximum(jnp.abs(den), floor)
            h_t = r2[0:HEAD_DIM, :] * inv
            for h, l in ((h0, lo_l), (h0 + 1, hi_l)):
                hc = slice(h * HEAD_DIM, (h + 1) * HEAD_DIM)
                o_gate = jax.nn.sigmoid(o_ref[rows, hc].astype(F32))
                mix_ref[rows, hc] = (o_gate * jnp.transpose(h_t[:, l])).astype(BF16)
        m_prev = m_new
    m_ref[0:N_HEADS, :] = jnp.broadcast_to(m_prev, (N_HEADS, LANES))

    u_c = cc_ref[...].astype(F32) * cx_ref[...].astype(F32)
    rid = lax.broadcasted_iota(I32, u_c.shape, 0)
    prev1 = carry_ref[0:1, :]
    prev2 = carry_ref[1:2, :]
    u1 = jnp.where(rid == 0, prev1, pltpu.roll(u_c, 1, axis=0))
    u2 = jnp.where(rid == 0, prev2, jnp.where(rid == 1, prev1, pltpu.roll(u_c, 2, axis=0)))
    z = cw_ref[0:1, :] * u2 + cw_ref[1:2, :] * u1 + cw_ref[2:3, :] * u_c
    mix_ref[:, D_MLSTM:] = (cb_ref[...].astype(F32) * z).astype(BF16)
    carry_ref[0:1, :] = u_c[ts - 1:ts, :]
    carry_ref[1:2, :] = u_c[ts - 2:ts - 1, :]

    d_model = x_ref.shape[1]
    mix = mix_ref[...]
    ssq = jnp.zeros((ts, 1), F32)
    for j in range(d_model // OUT_COLS):
        cs = slice(j * OUT_COLS, (j + 1) * OUT_COLS)
        x2c = x_ref[:, cs] + jnp.dot(mix, wo_ref[:, cs], preferred_element_type=F32)
        x2_ref[:, cs] = x2c
        ssq = ssq + jnp.sum(x2c * x2c, axis=-1, keepdims=True)
    hn2 = x2_ref[...] * lax.rsqrt(ssq * (1.0 / d_model) + RMS_EPS) * g2_ref[...]
    hn2_ref[...] = _pack_rows(hn2)

    logits = jnp.dot(hn2.astype(BF16), wr_ref[...], preferred_element_type=F32) + br_ref[...]
    lt = jnp.transpose(logits)
    gl = lt[0:N_GROUPS]
    gmax = jnp.max(gl, axis=0, keepdims=True)
    gi = lax.broadcasted_iota(I32, gl.shape, 0)
    g_sel = jnp.min(jnp.where(gl == gmax, gi, N_GROUPS), axis=0, keepdims=True)
    p_g = 1.0 / jnp.sum(jnp.exp(gl - gmax), axis=0, keepdims=True)
    in_g = lt[EXPERT_COL0:EXPERT_COL0 + EXPERTS_PER_GROUP]
    for g in range(1, N_GROUPS):
        lo = EXPERT_COL0 + g * EXPERTS_PER_GROUP
        in_g = jnp.where(g_sel == g, lt[lo:lo + EXPERTS_PER_GROUP], in_g)
    ei = lax.broadcasted_iota(I32, in_g.shape, 0)
    v1 = jnp.max(in_g, axis=0, keepdims=True)
    i1 = jnp.min(jnp.where(in_g == v1, ei, EXPERTS_PER_GROUP), axis=0, keepdims=True)
    rest = jnp.where(ei == i1, neg_inf, in_g)
    v2 = jnp.max(rest, axis=0, keepdims=True)
    i2 = jnp.min(jnp.where(rest == v2, ei, EXPERTS_PER_GROUP), axis=0, keepdims=True)
    t = jnp.exp(v2 - v1)
    w1 = p_g / (1.0 + t)
    w2 = p_g * t / (1.0 + t)
    e1 = g_sel * EXPERTS_PER_GROUP + i1
    e2 = g_sel * EXPERTS_PER_GROUP + i2

    eio = lax.broadcasted_iota(I32, (N_EXPERTS, LANES), 0)
    base = cnt_ref[:, 0:1]
    zero_i = jnp.zeros((1, LANES), I32)
    n_blk = ts // LANES
    hits = []
    for k in range(n_blk):
        ls = slice(k * LANES, (k + 1) * LANES)
        hits.append((eio == e1[:, ls], eio == e2[:, ls]))
    onehots = [jnp.where(jnp.logical_or(a1, a2), 1.0, 0.0) for a1, a2 in hits]
    incl_all = jnp.dot(jnp.concatenate(onehots, axis=0).astype(BF16), prefix_ones,
                       preferred_element_type=F32).astype(I32)
    for k in range(n_blk):
        ls = slice(k * LANES, (k + 1) * LANES)
        is1, is2 = hits[k]
        onehot = onehots[k].astype(I32)
        incl = incl_all[k * N_EXPERTS:(k + 1) * N_EXPERTS]
        rank_e = base + incl - onehot
        r1 = jnp.sum(jnp.where(is1, rank_e, 0), axis=0, keepdims=True)
        r2 = jnp.sum(jnp.where(is2, rank_e, 0), axis=0, keepdims=True)
        ri_ref[:, ls] = jnp.concatenate(
            [e1[:, ls], e2[:, ls], r1, r2, zero_i, zero_i, zero_i, zero_i], axis=0)
        base = base + incl[:, LANES - 1:LANES]
    zero_f = jnp.zeros_like(w1)
    rw_ref[...] = jnp.concatenate([w1, w2, zero_f, zero_f, zero_f, zero_f, zero_f, zero_f], axis=0)
    cnt_new = jnp.broadcast_to(base, (N_EXPERTS, LANES))
    cnt_ref[...] = cnt_new
    cnt_out_ref[...] = cnt_new


def _mixer(x, a, qvt, gt, conv_w, w_out, norm_ffn, w_router, b_router):
    b_sz, s_len, d = x.shape
    ts = TS_MIXER
    dm = D_MLSTM

    def a_spec(j):
        return pl.BlockSpec((None, ts, dm), lambda b, s, j=j: (b, s, j))

    tok = lambda w: pl.BlockSpec((None, ts, w), lambda b, s: (b, s, 0))
    rowsp = pl.BlockSpec((None, 2 * N_HEADS, ts), lambda b, s: (b, 0, s))
    const = lambda shape: pl.BlockSpec(shape, lambda b, s: tuple(0 for _ in shape))
    return pl.pallas_call(
        _mixer_kernel,
        grid=(b_sz, s_len // ts),
        in_specs=[a_spec(0), a_spec(1), a_spec(2), a_spec(3), a_spec(4),
                  pl.BlockSpec((None, dm, ts), lambda b, s: (b, 0, s)),
                  pl.BlockSpec((None, dm, ts), lambda b, s: (b, 1, s)),
                  rowsp, tok(d),
                  const((3, D_CONV)), const((d, d)), const((1, d)),
                  const((d, ROUTER_COLS)), const((1, ROUTER_COLS))],
        out_specs=[tok(d), tok(d // 2), rowsp, rowsp, const((N_EXPERTS, LANES))],
        out_shape=[
            jax.ShapeDtypeStruct((b_sz, s_len, d), F32),
            jax.ShapeDtypeStruct((b_sz, s_len, d // 2), I32),
            jax.ShapeDtypeStruct((b_sz, 2 * N_HEADS, s_len), I32),
            jax.ShapeDtypeStruct((b_sz, 2 * N_HEADS, s_len), F32),
            jax.ShapeDtypeStruct((N_EXPERTS, LANES), I32),
        ],
        scratch_shapes=[
            pltpu.VMEM((N_HEADS // 2, HEAD_DIM + AUG_ROWS, 2 * HEAD_DIM), F32),
            pltpu.VMEM((2 * N_HEADS, LANES), F32),
            pltpu.VMEM((8, D_CONV), F32),
            pltpu.VMEM((N_EXPERTS, LANES), I32),
            pltpu.VMEM((ts, d), BF16),
        ],
        compiler_params=pltpu.CompilerParams(
            dimension_semantics=("arbitrary", "arbitrary"), vmem_limit_bytes=VMEM_LIMIT),
        name="mixer",
    )(a, a, a, a, a, qvt, qvt, gt, x, conv_w, w_out.astype(BF16), norm_ffn[None, :], w_router.astype(BF16), b_router)


SC_CORES = 2
SC_SUBCORES = 16
SC_WORKERS = SC_CORES * SC_SUBCORES
SC_CHUNK = 64


def _sc_worker_id():
    return lax.axis_index("s") * SC_CORES + lax.axis_index("c")


def _sc_scatter_rows(rows, idx, n_out):
    n_rows, width = rows.shape
    per_worker = n_rows // SC_WORKERS
    n_chunks = per_worker // SC_CHUNK
    assert n_chunks % 2 == 0
    mesh = plsc.VectorSubcoreMesh(core_axis_name="c", subcore_axis_name="s")

    def body(rows_hbm, idx_hbm, out_hbm, ia0, ib0, r0, ia1, ib1, r1, ls0, ls1, ss0, ss1):
        wid = _sc_worker_id()
        bufs = ((ia0, ib0, r0, ls0, ss0), (ia1, ib1, r1, ls1, ss1))

        def loads(j, b):
            ia, ib, r, ls, _ = bufs[b]
            base = pl.multiple_of(wid * per_worker + j * SC_CHUNK, SC_CHUNK)
            return (pltpu.make_async_copy(rows_hbm.at[pl.ds(base, SC_CHUNK)], r, ls),
                    pltpu.make_async_copy(idx_hbm.at[pl.ds(base, SC_CHUNK)], ia, ls),
                    pltpu.make_async_copy(idx_hbm.at[pl.ds(n_rows + base, SC_CHUNK)], ib, ls))

        def scatters(b):
            ia, ib, r, _, ss = bufs[b]
            return (pltpu.make_async_copy(r, out_hbm.at[ia], ss),
                    pltpu.make_async_copy(r, out_hbm.at[ib], ss))

        for c in loads(0, 0):
            c.start()

        @pl.loop(0, n_chunks, step=2)
        def _(j0):
            for b in range(2):
                j = j0 + b
                for c in loads(j, b):
                    c.wait()

                @pl.when(j >= 1)
                def _():
                    for c in scatters(1 - b):
                        c.wait()

                @pl.when(j + 1 < n_chunks)
                def _():
                    for c in loads(j + 1, 1 - b):
                        c.start()

                for c in scatters(b):
                    c.start()

        for c in scatters((n_chunks - 1) % 2):
            c.wait()

    idx_t = pltpu.VMEM((SC_CHUNK,), I32)
    row_t = pltpu.VMEM((SC_CHUNK, width), I32)
    sem_t = pltpu.SemaphoreType.DMA
    return pl.kernel(
        body,
        out_type=jax.ShapeDtypeStruct((n_out, width), I32),
        mesh=mesh,
        scratch_types=[idx_t, idx_t, row_t, idx_t, idx_t, row_t, sem_t, sem_t, sem_t, sem_t],
        name="sc_scatter",
    )(rows, idx)


def _sc_gather_rows(table, idx):
    n_rows = idx.shape[0]
    width = table.shape[1]
    per_worker = n_rows // SC_WORKERS
    n_chunks = per_worker // SC_CHUNK
    assert n_chunks % 2 == 0
    mesh = plsc.VectorSubcoreMesh(core_axis_name="c", subcore_axis_name="s")

    def body(table_hbm, idx_hbm, out_hbm, i0, r0, i1, r1, is0, is1, gs0, gs1, ws0, ws1):
        wid = _sc_worker_id()
        bufs = ((i0, r0, is0, gs0, ws0), (i1, r1, is1, gs1, ws1))

        def chunk_base(j):
            return pl.multiple_of(wid * per_worker + j * SC_CHUNK, SC_CHUNK)

        def idx_load(j, b):
            return pltpu.make_async_copy(idx_hbm.at[pl.ds(chunk_base(j), SC_CHUNK)], bufs[b][0], bufs[b][2])

        def gather(b):
            return pltpu.make_async_copy(table_hbm.at[bufs[b][0]], bufs[b][1], bufs[b][3])

        def writeback(j, b):
            return pltpu.make_async_copy(bufs[b][1], out_hbm.at[pl.ds(chunk_base(j), SC_CHUNK)], bufs[b][4])

        idx_load(0, 0).start()
        idx_load(0, 0).wait()
        gather(0).start()
        idx_load(1, 1).start()

        @pl.loop(0, n_chunks, step=2)
        def _(j0):
            for b in range(2):
                j = j0 + b

                @pl.when(j + 1 < n_chunks)
                def _():
                    idx_load(j + 1, 1 - b).wait()

                    @pl.when(j >= 1)
                    def _():
                        writeback(j - 1, 1 - b).wait()
                    gather(1 - b).start()

                gather(b).wait()
                writeback(j, b).start()

                @pl.when(j + 2 < n_chunks)
                def _():
                    idx_load(j + 2, b).start()

        writeback(n_chunks - 2, 0).wait()
        writeback(n_chunks - 1, 1).wait()

    idx_t = pltpu.VMEM((SC_CHUNK,), I32)
    row_t = pltpu.VMEM((SC_CHUNK, width), I32)
    sem_t = pltpu.SemaphoreType.DMA
    return pl.kernel(
        body,
        out_type=jax.ShapeDtypeStruct((n_rows, width), I32),
        mesh=mesh,
        scratch_types=[idx_t, row_t, idx_t, row_t, sem_t, sem_t, sem_t, sem_t, sem_t, sem_t],
        name="sc_gather",
    )(table, idx)


def _experts_kernel(te_ref, nu_ref, tv_ref, ne_ref, xs_ref, wg_hbm, wu_hbm, wd_hbm, y_ref,
                    wgs_ref, wus_ref, wds_ref, wgb_ref, wub_ref, wdb_ref, wsem):
    i = pl.program_id(0)
    e = te_ref[i]
    e_prev = te_ref[jnp.maximum(i - 1, 0)]
    new_expert = jnp.logical_or(i == 0, e != e_prev)
    used = i < nu_ref[0]

    def weight_copies(expert):
        return (pltpu.make_async_copy(wg_hbm.at[expert], wgs_ref, wsem),
                pltpu.make_async_copy(wu_hbm.at[expert], wus_ref, wsem),
                pltpu.make_async_copy(wd_hbm.at[expert], wds_ref, wsem))

    @pl.when(i == 0)
    def _():
        for c in weight_copies(e):
            c.start()

    @pl.when(jnp.logical_and(used, new_expert))
    def _():
        for c in weight_copies(e):
            c.wait()
        wgb_ref[...] = wgs_ref[...].astype(BF16)
        wub_ref[...] = wus_ref[...].astype(BF16)
        wdb_ref[...] = wds_ref[...].astype(BF16)

        @pl.when(ne_ref[i] >= 0)
        def _():
            for c in weight_copies(ne_ref[i]):
                c.start()

    @pl.when(used)
    def _():
        row = lax.broadcasted_iota(I32, xs_ref.shape, 0)
        x_lo, x_hi = _unpack_rows(jnp.where(row < tv_ref[i], xs_ref[...], 0))
        xb = jnp.concatenate([x_lo.astype(BF16), x_hi.astype(BF16)], axis=1)
        gate = jnp.dot(xb, wgb_ref[...], preferred_element_type=F32)
        up = jnp.dot(xb, wub_ref[...], preferred_element_type=F32)
        hid = (gate * jax.nn.sigmoid(gate) * up).astype(BF16)
        y_ref[...] = _pack_rows(jnp.dot(hid, wdb_ref[...], preferred_element_type=F32))

    @pl.when(jnp.logical_not(used))
    def _():
        y_ref[...] = jnp.zeros_like(y_ref)


def _experts(xs, tile_e, n_used, tile_valid, next_e, w_gate, w_up, w_down):
    cap, dw = xs.shape
    d = 2 * dw
    tm = TM_EXPERT
    n_tiles = cap // tm

    def x_map(i, te, nu, tv, ne):
        return (jnp.minimum(i, jnp.maximum(nu[0] - 1, 0)), 0)

    return pl.pallas_call(
        _experts_kernel,
        grid_spec=pltpu.PrefetchScalarGridSpec(
            num_scalar_prefetch=4,
            grid=(n_tiles,),
            in_specs=[
                pl.BlockSpec((tm, dw), x_map),
                pl.BlockSpec(memory_space=pl.ANY),
                pl.BlockSpec(memory_space=pl.ANY),
                pl.BlockSpec(memory_space=pl.ANY),
            ],
            out_specs=pl.BlockSpec((tm, dw), lambda i, te, nu, tv, ne: (i, 0)),
            scratch_shapes=[pltpu.VMEM((d, D_FF), F32), pltpu.VMEM((d, D_FF), F32), pltpu.VMEM((D_FF, d), F32),
                            pltpu.VMEM((d, D_FF), BF16), pltpu.VMEM((d, D_FF), BF16), pltpu.VMEM((D_FF, d), BF16),
                            pltpu.SemaphoreType.DMA(())],
        ),
        out_shape=jax.ShapeDtypeStruct((cap, dw), I32),
        compiler_params=pltpu.CompilerParams(
            dimension_semantics=("arbitrary",), vmem_limit_bytes=VMEM_LIMIT),
        name="experts",
    )(tile_e, n_used, tile_valid, next_e, xs, w_gate, w_up, w_down)


def _combine_stream_kernel(x2_ref, y1_ref, y2_ref, rw_ref, gf_ref, out_ref):
    w_t = jnp.transpose(rw_ref[...])
    a_lo, a_hi = _unpack_rows(y1_ref[...])
    b_lo, b_hi = _unpack_rows(y2_ref[...])
    w1 = w_t[:, 0:1]
    w2 = w_t[:, 1:2]
    moe = jnp.concatenate([w1 * a_lo + w2 * b_lo, w1 * a_hi + w2 * b_hi], axis=1)
    x3 = x2_ref[...] + moe
    ms = jnp.mean(x3 * x3, axis=-1, keepdims=True)
    out_ref[...] = x3 * lax.rsqrt(ms + RMS_EPS) * gf_ref[...]


def _combine_stream(x2, yg, rw, norm_final):
    b_sz, s_len, d = x2.shape
    tc = TC_COMBINE
    n_s = s_len // tc
    n_steps = b_sz * n_s
    return pl.pallas_call(
        _combine_stream_kernel,
        grid=(n_steps,),
        in_specs=[
            pl.BlockSpec((tc, d), lambda g: (g, 0)),
            pl.BlockSpec((tc, d // 2), lambda g: (g, 0)),
            pl.BlockSpec((tc, d // 2), lambda g: (g + n_steps, 0)),
            pl.BlockSpec((None, 2 * N_HEADS, tc), lambda g: (g // n_s, 0, g % n_s)),
            pl.BlockSpec((1, d), lambda g: (0, 0)),
        ],
        out_specs=pl.BlockSpec((tc, d), lambda g: (g, 0)),
        out_shape=jax.ShapeDtypeStruct((b_sz * s_len, d), F32),
        compiler_params=pltpu.CompilerParams(
            dimension_semantics=("arbitrary",), vmem_limit_bytes=VMEM_LIMIT),
        name="combine_stream",
    )(x2.reshape(b_sz * s_len, d), yg, yg, rw, norm_final[None, :]).reshape(b_sz, s_len, d)


TM_EXPERT_SHIFT = TM_EXPERT.bit_length() - 1
assert 1 << TM_EXPERT_SHIFT == TM_EXPERT


def _plan_kernel(cnt_ref, ri_ref, pos_ref, meta_ref):
    tm = TM_EXPERT
    n_e, lanes = cnt_ref.shape
    cnt = cnt_ref[...]
    padded = lax.shift_right_logical(cnt + (tm - 1), TM_EXPERT_SHIFT) * tm
    sub = lax.broadcasted_iota(I32, (n_e, lanes), 0)
    lane = lax.broadcasted_iota(I32, (n_e, lanes), 1)
    padded_row = jnp.sum(jnp.where(sub == lane, padded, 0), axis=0, keepdims=True)
    below = jnp.where(lane <= sub, jnp.broadcast_to(padded_row, (n_e, lanes)), 0)
    pends = jnp.sum(below.astype(F32), axis=1, keepdims=True).astype(I32)
    seg = pends - padded[:, 0:1]
    tok_end = seg + cnt[:, 0:1]
    total = pends[n_e - 1:n_e, :]

    n_b, _, s_len = ri_ref.shape
    e_ids = lax.broadcasted_iota(I32, (n_e, s_len), 0)
    for b in range(n_b):
        for k in range(2):
            start = jnp.sum(jnp.where(ri_ref[b, k:k + 1, :] == e_ids, seg, 0), axis=0, keepdims=True)
            pos_ref[k, b:b + 1, :] = ri_ref[b, 2 + k:3 + k, :] + start

    n_t = meta_ref.shape[1]
    t_start = lax.broadcasted_iota(I32, (1, n_t), 1) * tm
    experts_before = lambda row: jnp.minimum(
        jnp.sum(jnp.where(row >= pends, 1, 0), axis=0, keepdims=True), n_e - 1)
    tile_e = experts_before(t_start)
    mine = tile_e == lax.broadcasted_iota(I32, (n_e, n_t), 0)
    pick = lambda col: jnp.sum(jnp.where(mine, col, 0), axis=0, keepdims=True)
    tile_valid = jnp.clip(pick(tok_end) - t_start, 0, tm)
    seg_end = pick(pends)
    next_e = jnp.where(seg_end < total, experts_before(seg_end), -1)
    n_used = jnp.broadcast_to(lax.shift_right_logical(total, TM_EXPERT_SHIFT), (1, n_t))
    zero = jnp.zeros_like(tile_e)
    meta_ref[...] = jnp.concatenate([tile_e, tile_valid, next_e, n_used, zero, zero, zero, zero], axis=0)


def _plan(ri, counts, n_tiles):
    b_sz, _, s_len = ri.shape
    assert n_tiles * TM_EXPERT < 2 ** 24
    n_t = -(-n_tiles // LANES) * LANES
    pos, meta = pl.pallas_call(
        _plan_kernel,
        out_shape=[jax.ShapeDtypeStruct((2, b_sz, s_len), I32), jax.ShapeDtypeStruct((8, n_t), I32)],
        name="plan",
    )(counts, ri)
    return pos.reshape(-1), meta[0, :n_tiles], meta[3, :1], meta[1, :n_tiles], meta[2, :n_tiles]


def _layer(x, norm_mix, w_in, b_in, conv_w, w_out, norm_ffn, w_group, b_group, w_expert, b_expert,
           w_gate, w_up, w_down, norm_out):
    b_sz, s_len, d = x.shape
    n_tok = b_sz * s_len
    a, qvt, gt = _inproj(x, norm_mix, w_in, b_in)

    pad_g = EXPERT_COL0 - N_GROUPS
    pad_e = ROUTER_COLS - EXPERT_COL0 - N_EXPERTS
    w_router = jnp.concatenate([w_group, jnp.zeros((d, pad_g), F32), w_expert, jnp.zeros((d, pad_e), F32)], axis=1)
    b_router = jnp.concatenate([b_group, jnp.zeros((pad_g,), F32), b_expert, jnp.zeros((pad_e,), F32)])[None, :]
    x2, hn2, ri, rw, counts = _mixer(x, a, qvt, gt, conv_w, w_out, norm_ffn, w_router, b_router)

    n_tiles = (n_tok * 2) // TM_EXPERT + N_EXPERTS
    cap = n_tiles * TM_EXPERT
    pos_flat, tile_e, n_used, tile_valid, next_e = _plan(ri, counts, n_tiles)
    xs = _sc_scatter_rows(hn2.reshape(n_tok, d // 2), pos_flat, cap)
    y = _experts(xs, tile_e, n_used, tile_valid, next_e, w_gate, w_up, w_down)
    yg = _sc_gather_rows(y, pos_flat)
    return _combine_stream(x2, yg, rw, norm_out)


def kernel(x, norm_mix, w_in, b_in, conv_w, w_out, norm_ffn, w_group, b_group, w_expert, b_expert,
           w_gate, w_up, w_down, norm_final):
    depth = norm_mix.shape[0]
    assert depth == 1, "final-norm fusion below assumes a single layer"
    assert x.shape[-1] == 2 * D_MLSTM
    assert all(x.shape[1] % t == 0 for t in (TS_MIXER, TM_INPROJ, TC_COMBINE))
    return _layer(x, norm_mix[0], w_in[0], b_in[0], conv_w[0], w_out[0], norm_ffn[0],
                  w_group[0], b_group[0], w_expert[0], b_expert[0],
                  w_gate[0], w_up[0], w_down[0], norm_final)
```

```python
import jax
import jax.numpy as jnp
from jax import lax
from jax.experimental import pallas as pl
from jax.experimental.pallas import tpu as pltpu
from jax.experimental.pallas import tpu_sc as plsc

F32 = jnp.float32
BF16 = jnp.bfloat16
I32 = jnp.int32

N_HEADS = 4
HEAD_DIM = 128
D_MLSTM = N_HEADS * HEAD_DIM
D_CONV = 512
N_GROUPS = 4
EXPERTS_PER_GROUP = 8
N_EXPERTS = N_GROUPS * EXPERTS_PER_GROUP
D_FF = 512
RMS_EPS = 1e-6
Q_SCALE = HEAD_DIM ** -0.5

LANES = 128
CHUNK = LANES
AUG_ROWS = 16
ROUTER_COLS = 128
EXPERT_COL0 = 8

TM_INPROJ = 1024
TS_MIXER = 1024
OUT_COLS = 256
TM_EXPERT = 512
TC_COMBINE = 2048

VMEM_LIMIT = 56 * 1024 * 1024


HI_HALF = -65536


def _pack_rows(x):
    w = x.shape[1] // 2
    lo = lax.shift_right_logical(lax.bitcast_convert_type(x[:, :w].astype(BF16).astype(F32), I32), 16)
    hi = lax.bitcast_convert_type(x[:, w:].astype(BF16).astype(F32), I32) & HI_HALF
    return lo | hi


def _unpack_rows(words):
    lo = lax.bitcast_convert_type(words << 16, F32)
    hi = lax.bitcast_convert_type(words & HI_HALF, F32)
    return lo, hi


def _nt_dot(a, b):
    return lax.dot_general(a, b, (((1,), (1,)), ((), ())), preferred_element_type=F32)


def _inproj_kernel(x_ref, g_ref, w_ref, ba_ref, bt_ref, a_ref, qvt_ref, gt_ref, wt_ref, wc_ref):
    dm = D_MLSTM
    d = w_ref.shape[0]
    g0 = 4 * dm
    n_gate = 2 * N_HEADS
    step_cols = 2 * LANES

    @pl.when(jnp.logical_and(pl.program_id(0) == 0, pl.program_id(1) == 0))
    def _():
        for j, c0 in enumerate((0, 2 * dm)):
            for h in range(0, dm, step_cols):
                blk = w_ref[:, c0 + h:c0 + h + step_cols].astype(F32)
                wt_ref[j * dm + h:j * dm + h + step_cols, :] = jnp.transpose(blk).astype(BF16)
        gates_t = jnp.transpose(w_ref[:, g0:g0 + LANES].astype(F32))[0:AUG_ROWS, :]
        row = lax.broadcasted_iota(I32, (AUG_ROWS, d), 0)
        wt_ref[2 * dm:, :] = jnp.where(row < n_gate, gates_t, 0.0).astype(BF16)
        for r in range(0, d, step_cols):
            tail = w_ref[r:r + step_cols, g0:].astype(F32)
            wc_ref[r:r + step_cols, :] = tail[:, n_gate:n_gate + 3 * D_CONV].astype(BF16)

    x = x_ref[...]
    ms = jnp.mean(x * x, axis=-1, keepdims=True)
    hn = (x * lax.rsqrt(ms + RMS_EPS) * g_ref[...]).astype(BF16)
    blocks = [w_ref.at[:, dm:2 * dm], w_ref.at[:, 3 * dm:4 * dm]]
    blocks += [wc_ref.at[:, j * D_CONV:(j + 1) * D_CONV] for j in range(3)]
    for j, blk_ref in enumerate(blocks):
        sl = slice(j * dm, (j + 1) * dm)
        acc = jnp.dot(hn, blk_ref[...], preferred_element_type=F32) + ba_ref[:, sl]
        a_ref[:, sl] = acc.astype(BF16)
    tr = _nt_dot(wt_ref[...], hn) + bt_ref[...]
    q_rows = slice(0, D_MLSTM)
    v_rows = slice(D_MLSTM, 2 * D_MLSTM)
    qvt_ref[q_rows, :] = (tr[q_rows, :] * Q_SCALE).astype(BF16)
    qvt_ref[v_rows, :] = tr[v_rows, :].astype(BF16)
    gt_ref[...] = tr[2 * D_MLSTM:2 * D_MLSTM + 2 * N_HEADS, :]


def _inproj(x, norm_mix, w_in, b_in):
    b_sz, s_len, d = x.shape
    tm = TM_INPROJ
    dm = D_MLSTM
    g0 = 4 * dm
    c0 = g0 + 2 * N_HEADS
    assert D_CONV == dm and w_in.shape[1] == c0 + 3 * D_CONV
    bias_a = jnp.concatenate([b_in[dm:2 * dm], b_in[3 * dm:4 * dm], b_in[c0:c0 + 3 * D_CONV]])[None, :]
    n_gate = 2 * N_HEADS
    pad_rows = AUG_ROWS - n_gate
    b_t = jnp.concatenate([b_in[0:dm], b_in[2 * dm:3 * dm], b_in[g0:c0],
                           jnp.zeros((pad_rows,), b_in.dtype)])[:, None]
    n_t = 2 * dm + AUG_ROWS
    n_a = 2 * dm + 3 * D_CONV
    n_in = w_in.shape[1]
    return pl.pallas_call(
        _inproj_kernel,
        grid=(b_sz, s_len // tm),
        in_specs=[
            pl.BlockSpec((None, tm, d), lambda b, s: (b, s, 0)),
            pl.BlockSpec((1, d), lambda b, s: (0, 0)),
            pl.BlockSpec((d, n_in), lambda b, s: (0, 0)),
            pl.BlockSpec((1, n_a), lambda b, s: (0, 0)),
            pl.BlockSpec((n_t, 1), lambda b, s: (0, 0)),
        ],
        out_specs=[
            pl.BlockSpec((None, tm, n_a), lambda b, s: (b, s, 0)),
            pl.BlockSpec((None, 2 * dm, tm), lambda b, s: (b, 0, s)),
            pl.BlockSpec((None, 2 * N_HEADS, tm), lambda b, s: (b, 0, s)),
        ],
        out_shape=[
            jax.ShapeDtypeStruct((b_sz, s_len, n_a), BF16),
            jax.ShapeDtypeStruct((b_sz, 2 * dm, s_len), BF16),
            jax.ShapeDtypeStruct((b_sz, 2 * N_HEADS, s_len), F32),
        ],
        scratch_shapes=[pltpu.VMEM((n_t, d), BF16),
                        pltpu.VMEM((d, 3 * D_CONV), BF16)],
        compiler_params=pltpu.CompilerParams(
            dimension_semantics=("arbitrary", "arbitrary"), vmem_limit_bytes=VMEM_LIMIT),
        name="inproj",
    )(x, norm_mix[None, :], w_in.astype(BF16), bias_a, b_t)


def _mixer_kernel(k_ref, o_ref, cb_ref, cc_ref, cx_ref, qt_ref, vt_ref, gt_ref, x_ref,
                  cw_ref, wo_ref, g2_ref, wr_ref, br_ref,
                  x2_ref, hn2_ref, ri_ref, rw_ref, cnt_out_ref,
                  ct_ref, m_ref, carry_ref, cnt_ref, mix_ref):
    ts = x_ref.shape[0]
    n_chunks = ts // CHUNK
    first_tile = pl.program_id(1) == 0

    @pl.when(first_tile)
    def _():
        ct_ref[...] = jnp.zeros_like(ct_ref)
        m_ref[...] = jnp.zeros_like(m_ref)
        carry_ref[...] = jnp.zeros_like(carry_ref)

    @pl.when(jnp.logical_and(first_tile, pl.program_id(0) == 0))
    def _():
        cnt_ref[...] = jnp.zeros_like(cnt_ref)

    gates = gt_ref[...]
    li_all = gates[0:N_HEADS]
    fg = gates[N_HEADS:2 * N_HEADS]
    lf_all = jnp.minimum(fg, 0.0) - jnp.log1p(jnp.exp(-jnp.abs(fg)))
    key_j = lax.broadcasted_iota(I32, (CHUNK, CHUNK), 0)
    qry_i = lax.broadcasted_iota(I32, (CHUNK, CHUNK), 1)
    causal_t = key_j <= qry_i
    neg_inf = jnp.float32(-jnp.inf)
    ones_rows = (lax.broadcasted_iota(I32, (AUG_ROWS, CHUNK), 0) == 0).astype(BF16)

    prefix_ones = (key_j <= qry_i).astype(BF16)

    lf_hi = lf_all.astype(BF16)
    rem = lf_all - lf_hi.astype(F32)
    lf_mid = rem.astype(BF16)
    lf_lo = (rem - lf_mid.astype(F32)).astype(BF16)
    zero_bf = jnp.zeros((N_HEADS, CHUNK), BF16)
    parts = []
    for c in range(n_chunks):
        rows = slice(c * CHUNK, (c + 1) * CHUNK)
        parts += [lf_hi[:, rows], lf_mid[:, rows], lf_lo[:, rows], zero_bf]
    cum = jnp.dot(jnp.concatenate(parts, axis=0), prefix_ones, preferred_element_type=F32)

    m_prev = m_ref[0:N_HEADS, 0:1]
    for c in range(n_chunks):
        rows = slice(c * CHUNK, (c + 1) * CHUNK)
        li = li_all[:, rows]
        r0 = 4 * N_HEADS * c
        b_cum = (cum[r0:r0 + N_HEADS] + cum[r0 + N_HEADS:r0 + 2 * N_HEADS]
                 + cum[r0 + 2 * N_HEADS:r0 + 3 * N_HEADS])
        g_tot = b_cum[:, CHUNK - 1:CHUNK]
        a_end = g_tot - b_cum + li
        m_loc = jnp.max(a_end, axis=1, keepdims=True)
        u = li - b_cum
        m_new = jnp.maximum(g_tot + m_prev, m_loc)
        decay = jnp.exp(g_tot + m_prev - m_new)
        s_fac = jnp.exp(m_loc - m_new)
        w_key = jnp.exp(a_end - m_loc)
        u_cols = jnp.transpose(jnp.concatenate([u, jnp.zeros_like(u)], axis=0))

        def pair_row(x, h0):
            parts = [jnp.broadcast_to(x[h:h + 1, :], (1, CHUNK)) for h in (h0, h0 + 1)]
            return jnp.concatenate(parts, axis=1)

        def block_diag(a, b):
            z = jnp.zeros_like(a)
            return jnp.concatenate([jnp.concatenate([a, z], axis=1), jnp.concatenate([z, b], axis=1)], axis=0)

        for p in range(N_HEADS // 2):
            h0 = 2 * p
            pc = slice(h0 * HEAD_DIM, (h0 + 2) * HEAD_DIM)
            lo_l, hi_l = slice(0, CHUNK), slice(CHUNK, 2 * CHUNK)
            k2 = k_ref[rows, pc]
            q2 = qt_ref[pc, rows]
            v2 = vt_ref[pc, rows]
            s2 = jnp.dot(k2, block_diag(q2[0:HEAD_DIM], q2[HEAD_DIM:]), preferred_element_type=F32)
            u_t = jnp.concatenate([jnp.where(causal_t, u_cols[:, h:h + 1], neg_inf) for h in (h0, h0 + 1)],
                                  axis=1)
            mp = pair_row(m_prev, h0)
            m_tot = jnp.maximum(jnp.max(u_t, axis=0, keepdims=True), mp)
            f_inter = jnp.exp(mp - m_tot)
            floor = jnp.exp(-pair_row(b_cum, h0) - m_tot)
            p2 = (s2 * jnp.exp(u_t - m_tot)).astype(BF16)
            qf = [(q2[r].astype(F32) * f_inter[:, l]).astype(BF16)
                  for r, l in ((slice(0, HEAD_DIM), lo_l), (slice(HEAD_DIM, None), hi_l))]
            c_old = ct_ref[p]
            cb16 = c_old.astype(BF16)
            vt_aug = [jnp.concatenate([v2[r], ones_rows], axis=0)
                      for r in (slice(0, HEAD_DIM), slice(HEAD_DIM, None))]
            lhs = jnp.concatenate([vt_aug[0], cb16[:, lo_l], vt_aug[1], cb16[:, hi_l]], axis=1)
            rhs = block_diag(jnp.concatenate([p2[:, lo_l], qf[0]], axis=0),
                             jnp.concatenate([p2[:, hi_l], qf[1]], axis=0))
            r2 = jnp.dot(lhs, rhs, preferred_element_type=F32)
            w2 = pair_row(w_key, h0)
            vtw = jnp.concatenate([(vt_aug[0].astype(F32) * w2[:, lo_l]).astype(BF16),
                                   (vt_aug[1].astype(F32) * w2[:, hi_l]).astype(BF16)], axis=1)
            kv = jnp.dot(vtw, block_diag(k2[:, lo_l], k2[:, hi_l]), preferred_element_type=F32)
            ct_ref[p] = pair_row(decay, h0) * c_old + pair_row(s_fac, h0) * kv

            den = r2[HEAD_DIM:HEAD_DIM + 1, :]
            inv = 1.0 / jnp.maximum(jnp.abs(den), floor)
            h_t = r2[0:HEAD_DIM, :] * inv
            for h, l in ((h0, lo_l), (h0 + 1, hi_l)):
                hc = slice(h * HEAD_DIM, (h + 1) * HEAD_DIM)
                o_gate = jax.nn.sigmoid(o_ref[rows, hc].astype(F32))
                mix_ref[rows, hc] = (o_gate * jnp.transpose(h_t[:, l])).astype(BF16)
        m_prev = m_new
    m_ref[0:N_HEADS, :] = jnp.broadcast_to(m_prev, (N_HEADS, LANES))

    u_c = cc_ref[...].astype(F32) * cx_ref[...].astype(F32)
    rid = lax.broadcasted_iota(I32, u_c.shape, 0)
    prev1 = carry_ref[0:1, :]
    prev2 = carry_ref[1:2, :]
    u1 = jnp.where(rid == 0, prev1, pltpu.roll(u_c, 1, axis=0))
    u2 = jnp.where(rid == 0, prev2, jnp.where(rid == 1, prev1, pltpu.roll(u_c, 2, axis=0)))
    z = cw_ref[0:1, :] * u2 + cw_ref[1:2, :] * u1 + cw_ref[2:3, :] * u_c
    mix_ref[:, D_MLSTM:] = (cb_ref[...].astype(F32) * z).astype(BF16)
    carry_ref[0:1, :] = u_c[ts - 1:ts, :]
    carry_ref[1:2, :] = u_c[ts - 2:ts - 1, :]

    d_model = x_ref.shape[1]
    mix = mix_ref[...]
    ssq = jnp.zeros((ts, 1), F32)
    for j in range(d_model // OUT_COLS):
        cs = slice(j * OUT_COLS, (j + 1) * OUT_COLS)
        x2c = x_ref[:, cs] + jnp.dot(mix, wo_ref[:, cs], preferred_element_type=F32)
        x2_ref[:, cs] = x2c
        ssq = ssq + jnp.sum(x2c * x2c, axis=-1, keepdims=True)
    hn2 = x2_ref[...] * lax.rsqrt(ssq * (1.0 / d_model) + RMS_EPS) * g2_ref[...]
    hn2_ref[...] = _pack_rows(hn2)

    logits = jnp.dot(hn2.astype(BF16), wr_ref[...], preferred_element_type=F32) + br_ref[...]
    lt = jnp.transpose(logits)
    gl = lt[0:N_GROUPS]
    gmax = jnp.max(gl, axis=0, keepdims=True)
    gi = lax.broadcasted_iota(I32, gl.shape, 0)
    g_sel = jnp.min(jnp.where(gl == gmax, gi, N_GROUPS), axis=0, keepdims=True)
    p_g = 1.0 / jnp.sum(jnp.exp(gl - gmax), axis=0, keepdims=True)
    in_g = lt[EXPERT_COL0:EXPERT_COL0 + EXPERTS_PER_GROUP]
    for g in range(1, N_GROUPS):
        lo = EXPERT_COL0 + g * EXPERTS_PER_GROUP
        in_g = jnp.where(g_sel == g, lt[lo:lo + EXPERTS_PER_GROUP], in_g)
    ei = lax.broadcasted_iota(I32, in_g.shape, 0)
    v1 = jnp.max(in_g, axis=0, keepdims=True)
    i1 = jnp.min(jnp.where(in_g == v1, ei, EXPERTS_PER_GROUP), axis=0, keepdims=True)
    rest = jnp.where(ei == i1, neg_inf, in_g)
    v2 = jnp.max(rest, axis=0, keepdims=True)
    i2 = jnp.min(jnp.where(rest == v2, ei, EXPERTS_PER_GROUP), axis=0, keepdims=True)
    t = jnp.exp(v2 - v1)
    w1 = p_g / (1.0 + t)
    w2 = p_g * t / (1.0 + t)
    e1 = g_sel * EXPERTS_PER_GROUP + i1
    e2 = g_sel * EXPERTS_PER_GROUP + i2

    eio = lax.broadcasted_iota(I32, (N_EXPERTS, LANES), 0)
    base = cnt_ref[:, 0:1]
    zero_i = jnp.zeros((1, LANES), I32)
    n_blk = ts // LANES
    hits = []
    for k in range(n_blk):
        ls = slice(k * LANES, (k + 1) * LANES)
        hits.append((eio == e1[:, ls], eio == e2[:, ls]))
    onehots = [jnp.where(jnp.logical_or(a1, a2), 1.0, 0.0) for a1, a2 in hits]
    incl_all = jnp.dot(jnp.concatenate(onehots, axis=0).astype(BF16), prefix_ones,
                       preferred_element_type=F32).astype(I32)
    for k in range(n_blk):
        ls = slice(k * LANES, (k + 1) * LANES)
        is1, is2 = hits[k]
        onehot = onehots[k].astype(I32)
        incl = incl_all[k * N_EXPERTS:(k + 1) * N_EXPERTS]
        rank_e = base + incl - onehot
        r1 = jnp.sum(jnp.where(is1, rank_e, 0), axis=0, keepdims=True)
        r2 = jnp.sum(jnp.where(is2, rank_e, 0), axis=0, keepdims=True)
        ri_ref[:, ls] = jnp.concatenate(
            [e1[:, ls], e2[:, ls], r1, r2, zero_i, zero_i, zero_i, zero_i], axis=0)
        base = base + incl[:, LANES - 1:LANES]
    zero_f = jnp.zeros_like(w1)
    rw_ref[...] = jnp.concatenate([w1, w2, zero_f, zero_f, zero_f, zero_f, zero_f, zero_f], axis=0)
    cnt_new = jnp.broadcast_to(base, (N_EXPERTS, LANES))
    cnt_ref[...] = cnt_new
    cnt_out_ref[...] = cnt_new


def _mixer(x, a, qvt, gt, conv_w, w_out, norm_ffn, w_router, b_router):
    b_sz, s_len, d = x.shape
    ts = TS_MIXER
    dm = D_MLSTM

    def a_spec(j):
        return pl.BlockSpec((None, ts, dm), lambda b, s, j=j: (b, s, j))

    tok = lambda w: pl.BlockSpec((None, ts, w), lambda b, s: (b, s, 0))
    rowsp = pl.BlockSpec((None, 2 * N_HEADS, ts), lambda b, s: (b, 0, s))
    const = lambda shape: pl.BlockSpec(shape, lambda b, s: tuple(0 for _ in shape))
    return pl.pallas_call(
        _mixer_kernel,
        grid=(b_sz, s_len // ts),
        in_specs=[a_spec(0), a_spec(1), a_spec(2), a_spec(3), a_spec(4),
                  pl.BlockSpec((None, dm, ts), lambda b, s: (b, 0, s)),
                  pl.BlockSpec((None, dm, ts), lambda b, s: (b, 1, s)),
                  rowsp, tok(d),
                  const((3, D_CONV)), const((d, d)), const((1, d)),
                  const((d, ROUTER_COLS)), const((1, ROUTER_COLS))],
        out_specs=[tok(d), tok(d // 2), rowsp, rowsp, const((N_EXPERTS, LANES))],
        out_shape=[
            jax.ShapeDtypeStruct((b_sz, s_len, d), F32),
            jax.ShapeDtypeStruct((b_sz, s_len, d // 2), I32),
            jax.ShapeDtypeStruct((b_sz, 2 * N_HEADS, s_len), I32),
            jax.ShapeDtypeStruct((b_sz, 2 * N_HEADS, s_len), F32),
            jax.ShapeDtypeStruct((N_EXPERTS, LANES), I32),
        ],
        scratch_shapes=[
            pltpu.VMEM((N_HEADS // 2, HEAD_DIM + AUG_ROWS, 2 * HEAD_DIM), F32),
            pltpu.VMEM((2 * N_HEADS, LANES), F32),
            pltpu.VMEM((8, D_CONV), F32),
            pltpu.VMEM((N_EXPERTS, LANES), I32),
            pltpu.VMEM((ts, d), BF16),
        ],
        compiler_params=pltpu.CompilerParams(
            dimension_semantics=("arbitrary", "arbitrary"), vmem_limit_bytes=VMEM_LIMIT),
        name="mixer",
    )(a, a, a, a, a, qvt, qvt, gt, x, conv_w, w_out.astype(BF16), norm_ffn[None, :], w_router.astype(BF16), b_router)


SC_CORES = 2
SC_SUBCORES = 16
SC_WORKERS = SC_CORES * SC_SUBCORES
SC_CHUNK = 64


def _sc_worker_id():
    return lax.axis_index("s") * SC_CORES + lax.axis_index("c")


def _sc_scatter_rows(rows, idx, n_out):
    n_rows, width = rows.shape
    per_worker = n_rows // SC_WORKERS
    n_chunks = per_worker // SC_CHUNK
    assert n_chunks % 2 == 0
    mesh = plsc.VectorSubcoreMesh(core_axis_name="c", subcore_axis_name="s")

    def body(rows_hbm, idx_hbm, out_hbm, ia0, ib0, r0, ia1, ib1, r1, ls0, ls1, ss0, ss1):
        wid = _sc_worker_id()
        bufs = ((ia0, ib0, r0, ls0, ss0), (ia1, ib1, r1, ls1, ss1))

        def loads(j, b):
            ia, ib, r, ls, _ = bufs[b]
            base = pl.multiple_of(wid * per_worker + j * SC_CHUNK, SC_CHUNK)
            return (pltpu.make_async_copy(rows_hbm.at[pl.ds(base, SC_CHUNK)], r, ls),
                    pltpu.make_async_copy(idx_hbm.at[pl.ds(base, SC_CHUNK)], ia, ls),
                    pltpu.make_async_copy(idx_hbm.at[pl.ds(n_rows + base, SC_CHUNK)], ib, ls))

        def scatters(b):
            ia, ib, r, _, ss = bufs[b]
            return (pltpu.make_async_copy(r, out_hbm.at[ia], ss),
                    pltpu.make_async_copy(r, out_hbm.at[ib], ss))

        for c in loads(0, 0):
            c.start()

        @pl.loop(0, n_chunks, step=2)
        def _(j0):
            for b in range(2):
                j = j0 + b
                for c in loads(j, b):
                    c.wait()

                @pl.when(j >= 1)
                def _():
                    for c in scatters(1 - b):
                        c.wait()

                @pl.when(j + 1 < n_chunks)
                def _():
                    for c in loads(j + 1, 1 - b):
                        c.start()

                for c in scatters(b):
                    c.start()

        for c in scatters((n_chunks - 1) % 2):
            c.wait()

    idx_t = pltpu.VMEM((SC_CHUNK,), I32)
    row_t = pltpu.VMEM((SC_CHUNK, width), I32)
    sem_t = pltpu.SemaphoreType.DMA
    return pl.kernel(
        body,
        out_type=jax.ShapeDtypeStruct((n_out, width), I32),
        mesh=mesh,
        scratch_types=[idx_t, idx_t, row_t, idx_t, idx_t, row_t, sem_t, sem_t, sem_t, sem_t],
        name="sc_scatter",
    )(rows, idx)


def _sc_gather_rows(table, idx):
    n_rows = idx.shape[0]
    width = table.shape[1]
    per_worker = n_rows // SC_WORKERS
    n_chunks = per_worker // SC_CHUNK
    assert n_chunks % 2 == 0
    mesh = plsc.VectorSubcoreMesh(core_axis_name="c", subcore_axis_name="s")

    def body(table_hbm, idx_hbm, out_hbm, i0, r0, i1, r1, is0, is1, gs0, gs1, ws0, ws1):
        wid = _sc_worker_id()
        bufs = ((i0, r0, is0, gs0, ws0), (i1, r1, is1, gs1, ws1))

        def chunk_base(j):
            return pl.multiple_of(wid * per_worker + j * SC_CHUNK, SC_CHUNK)

        def idx_load(j, b):
            return pltpu.make_async_copy(idx_hbm.at[pl.ds(chunk_base(j), SC_CHUNK)], bufs[b][0], bufs[b][2])

        def gather(b):
            return pltpu.make_async_copy(table_hbm.at[bufs[b][0]], bufs[b][1], bufs[b][3])

        def writeback(j, b):
            return pltpu.make_async_copy(bufs[b][1], out_hbm.at[pl.ds(chunk_base(j), SC_CHUNK)], bufs[b][4])

        idx_load(0, 0).start()
        idx_load(0, 0).wait()
        gather(0).start()
        idx_load(1, 1).start()

        @pl.loop(0, n_chunks, step=2)
        def _(j0):
            for b in range(2):
                j = j0 + b

                @pl.when(j + 1 < n_chunks)
                def _():
                    idx_load(j + 1, 1 - b).wait()

                    @pl.when(j >= 1)
                    def _():
                        writeback(j - 1, 1 - b).wait()
                    gather(1 - b).start()

                gather(b).wait()
                writeback(j, b).start()

                @pl.when(j + 2 < n_chunks)
                def _():
                    idx_load(j + 2, b).start()

        writeback(n_chunks - 2, 0).wait()
        writeback(n_chunks - 1, 1).wait()

    idx_t = pltpu.VMEM((SC_CHUNK,), I32)
    row_t = pltpu.VMEM((SC_CHUNK, width), I32)
    sem_t = pltpu.SemaphoreType.DMA
    return pl.kernel(
        body,
        out_type=jax.ShapeDtypeStruct((n_rows, width), I32),
        mesh=mesh,
        scratch_types=[idx_t, row_t, idx_t, row_t, sem_t, sem_t, sem_t, sem_t, sem_t, sem_t],
        name="sc_gather",
    )(table, idx)


def _experts_kernel(te_ref, nu_ref, tv_ref, ne_ref, xs_ref, wg_hbm, wu_hbm, wd_hbm, y_ref,
                    wgs_ref, wus_ref, wds_ref, wgb_ref, wub_ref, wdb_ref, wsem):
    i = pl.program_id(0)
    e = te_ref[i]
    e_prev = te_ref[jnp.maximum(i - 1, 0)]
    new_expert = jnp.logical_or(i == 0, e != e_prev)
    used = i < nu_ref[0]

    def weight_copies(expert):
        return (pltpu.make_async_copy(wg_hbm.at[expert], wgs_ref, wsem),
                pltpu.make_async_copy(wu_hbm.at[expert], wus_ref, wsem),
                pltpu.make_async_copy(wd_hbm.at[expert], wds_ref, wsem))

    @pl.when(i == 0)
    def _():
        for c in weight_copies(e):
            c.start()

    @pl.when(jnp.logical_and(used, new_expert))
    def _():
        for c in weight_copies(e):
            c.wait()
        wgb_ref[...] = wgs_ref[...].astype(BF16)
        wub_ref[...] = wus_ref[...].astype(BF16)
        wdb_ref[...] = wds_ref[...].astype(BF16)

        @pl.when(ne_ref[i] >= 0)
        def _():
            for c in weight_copies(ne_ref[i]):
                c.start()

    @pl.when(used)
    def _():
        row = lax.broadcasted_iota(I32, xs_ref.shape, 0)
        x_lo, x_hi = _unpack_rows(jnp.where(row < tv_ref[i], xs_ref[...], 0))
        xb = jnp.concatenate([x_lo.astype(BF16), x_hi.astype(BF16)], axis=1)
        gate = jnp.dot(xb, wgb_ref[...], preferred_element_type=F32)
        up = jnp.dot(xb, wub_ref[...], preferred_element_type=F32)
        hid = (gate * jax.nn.sigmoid(gate) * up).astype(BF16)
        y_ref[...] = _pack_rows(jnp.dot(hid, wdb_ref[...], preferred_element_type=F32))

    @pl.when(jnp.logical_not(used))
    def _():
        y_ref[...] = jnp.zeros_like(y_ref)


def _experts(xs, tile_e, n_used, tile_valid, next_e, w_gate, w_up, w_down):
    cap, dw = xs.shape
    d = 2 * dw
    tm = TM_EXPERT
    n_tiles = cap // tm

    def x_map(i, te, nu, tv, ne):
        return (jnp.minimum(i, jnp.maximum(nu[0] - 1, 0)), 0)

    return pl.pallas_call(
        _experts_kernel,
        grid_spec=pltpu.PrefetchScalarGridSpec(
            num_scalar_prefetch=4,
            grid=(n_tiles,),
            in_specs=[
                pl.BlockSpec((tm, dw), x_map),
                pl.BlockSpec(memory_space=pl.ANY),
                pl.BlockSpec(memory_space=pl.ANY),
                pl.BlockSpec(memory_space=pl.ANY),
            ],
            out_specs=pl.BlockSpec((tm, dw), lambda i, te, nu, tv, ne: (i, 0)),
            scratch_shapes=[pltpu.VMEM((d, D_FF), F32), pltpu.VMEM((d, D_FF), F32), pltpu.VMEM((D_FF, d), F32),
                            pltpu.VMEM((d, D_FF), BF16), pltpu.VMEM((d, D_FF), BF16), pltpu.VMEM((D_FF, d), BF16),
                            pltpu.SemaphoreType.DMA(())],
        ),
        out_shape=jax.ShapeDtypeStruct((cap, dw), I32),
        compiler_params=pltpu.CompilerParams(
            dimension_semantics=("arbitrary",), vmem_limit_bytes=VMEM_LIMIT),
        name="experts",
    )(tile_e, n_used, tile_valid, next_e, xs, w_gate, w_up, w_down)


def _combine_stream_kernel(x2_ref, y1_ref, y2_ref, rw_ref, gf_ref, out_ref):
    w_t = jnp.transpose(rw_ref[...])
    a_lo, a_hi = _unpack_rows(y1_ref[...])
    b_lo, b_hi = _unpack_rows(y2_ref[...])
    w1 = w_t[:, 0:1]
    w2 = w_t[:, 1:2]
    moe = jnp.concatenate([w1 * a_lo + w2 * b_lo, w1 * a_hi + w2 * b_hi], axis=1)
    x3 = x2_ref[...] + moe
    ms = jnp.mean(x3 * x3, axis=-1, keepdims=True)
    out_ref[...] = x3 * lax.rsqrt(ms + RMS_EPS) * gf_ref[...]


def _combine_stream(x2, yg, rw, norm_final):
    b_sz, s_len, d = x2.shape
    tc = TC_COMBINE
    n_s = s_len // tc
    n_steps = b_sz * n_s
    return pl.pallas_call(
        _combine_stream_kernel,
        grid=(n_steps,),
        in_specs=[
            pl.BlockSpec((tc, d), lambda g: (g, 0)),
            pl.BlockSpec((tc, d // 2), lambda g: (g, 0)),
            pl.BlockSpec((tc, d // 2), lambda g: (g + n_steps, 0)),
            pl.BlockSpec((None, 2 * N_HEADS, tc), lambda g: (g // n_s, 0, g % n_s)),
            pl.BlockSpec((1, d), lambda g: (0, 0)),
        ],
        out_specs=pl.BlockSpec((tc, d), lambda g: (g, 0)),
        out_shape=jax.ShapeDtypeStruct((b_sz * s_len, d), F32),
        compiler_params=pltpu.CompilerParams(
            dimension_semantics=("arbitrary",), vmem_limit_bytes=VMEM_LIMIT),
        name="combine_stream",
    )(x2.reshape(b_sz * s_len, d), yg, yg, rw, norm_final[None, :]).reshape(b_sz, s_len, d)


TM_EXPERT_SHIFT = TM_EXPERT.bit_length() - 1
assert 1 << TM_EXPERT_SHIFT == TM_EXPERT


def _plan_kernel(cnt_ref, ri_ref, pos_ref, meta_ref):
    tm = TM_EXPERT
    n_e, lanes = cnt_ref.shape
    cnt = cnt_ref[...]
    padded = lax.shift_right_logical(cnt + (tm - 1), TM_EXPERT_SHIFT) * tm
    sub = lax.broadcasted_iota(I32, (n_e, lanes), 0)
    lane = lax.broadcasted_iota(I32, (n_e, lanes), 1)
    padded_row = jnp.sum(jnp.where(sub == lane, padded, 0), axis=0, keepdims=True)
    below = jnp.where(lane <= sub, jnp.broadcast_to(padded_row, (n_e, lanes)), 0)
    pends = jnp.sum(below.astype(F32), axis=1, keepdims=True).astype(I32)
    seg = pends - padded[:, 0:1]
    tok_end = seg + cnt[:, 0:1]
    total = pends[n_e - 1:n_e, :]

    n_b, _, s_len = ri_ref.shape
    e_ids = lax.broadcasted_iota(I32, (n_e, s_len), 0)
    for b in range(n_b):
        for k in range(2):
            start = jnp.sum(jnp.where(ri_ref[b, k:k + 1, :] == e_ids, seg, 0), axis=0, keepdims=True)
            pos_ref[k, b:b + 1, :] = ri_ref[b, 2 + k:3 + k, :] + start

    n_t = meta_ref.shape[1]
    t_start = lax.broadcasted_iota(I32, (1, n_t), 1) * tm
    experts_before = lambda row: jnp.minimum(
        jnp.sum(jnp.where(row >= pends, 1, 0), axis=0, keepdims=True), n_e - 1)
    tile_e = experts_before(t_start)
    mine = tile_e == lax.broadcasted_iota(I32, (n_e, n_t), 0)
    pick = lambda col: jnp.sum(jnp.where(mine, col, 0), axis=0, keepdims=True)
    tile_valid = jnp.clip(pick(tok_end) - t_start, 0, tm)
    seg_end = pick(pends)
    next_e = jnp.where(seg_end < total, experts_before(seg_end), -1)
    n_used = jnp.broadcast_to(lax.shift_right_logical(total, TM_EXPERT_SHIFT), (1, n_t))
    zero = jnp.zeros_like(tile_e)
    meta_ref[...] = jnp.concatenate([tile_e, tile_valid, next_e, n_used, zero, zero, zero, zero], axis=0)


def _plan(ri, counts, n_tiles):
    b_sz, _, s_len = ri.shape
    assert n_tiles * TM_EXPERT < 2 ** 24
    n_t = -(-n_tiles // LANES) * LANES
    pos, meta = pl.pallas_call(
        _plan_kernel,
        out_shape=[jax.ShapeDtypeStruct((2, b_sz, s_len), I32), jax.ShapeDtypeStruct((8, n_t), I32)],
        name="plan",
    )(counts, ri)
    return pos.reshape(-1), meta[0, :n_tiles], meta[3, :1], meta[1, :n_tiles], meta[2, :n_tiles]


def _layer(x, norm_mix, w_in, b_in, conv_w, w_out, norm_ffn, w_group, b_group, w_expert, b_expert,
           w_gate, w_up, w_down, norm_out):
    b_sz, s_len, d = x.shape
    n_tok = b_sz * s_len
    a, qvt, gt = _inproj(x, norm_mix, w_in, b_in)

    pad_g = EXPERT_COL0 - N_GROUPS
    pad_e = ROUTER_COLS - EXPERT_COL0 - N_EXPERTS
    w_router = jnp.concatenate([w_group, jnp.zeros((d, pad_g), F32), w_expert, jnp.zeros((d, pad_e), F32)], axis=1)
    b_router = jnp.concatenate([b_group, jnp.zeros((pad_g,), F32), b_expert, jnp.zeros((pad_e,), F32)])[None, :]
    x2, hn2, ri, rw, counts = _mixer(x, a, qvt, gt, conv_w, w_out, norm_ffn, w_router, b_router)

    n_tiles = (n_tok * 2) // TM_EXPERT + N_EXPERTS
    cap = n_tiles * TM_EXPERT
    pos_flat, tile_e, n_used, tile_valid, next_e = _plan(ri, counts, n_tiles)
    xs = _sc_scatter_rows(hn2.reshape(n_tok, d // 2), pos_flat, cap)
    y = _experts(xs, tile_e, n_used, tile_valid, next_e, w_gate, w_up, w_down)
    yg = _sc_gather_rows(y, pos_flat)
    return _combine_stream(x2, yg, rw, norm_out)


def kernel(x, norm_mix, w_in, b_in, conv_w, w_out, norm_ffn, w_group, b_group, w_expert, b_expert,
           w_gate, w_up, w_down, norm_final):
    depth = norm_mix.shape[0]
    assert depth == 1, "final-norm fusion below assumes a single layer"
    assert x.shape[-1] == 2 * D_MLSTM
    assert all(x.shape[1] % t == 0 for t in (TS_MIXER, TM_INPROJ, TC_COMBINE))
    return _layer(x, norm_mix[0], w_in[0], b_in[0], conv_w[0], w_out[0], norm_ffn[0],
                  w_group[0], b_group[0], w_expert[0], b_expert[0],
                  w_gate[0], w_up[0], w_down[0], norm_final)
```

```python
import jax
import jax.numpy as jnp
from jax import lax
from jax.experimental import pallas as pl
from jax.experimental.pallas import tpu as pltpu
from jax.experimental.pallas import tpu_sc as plsc

F32 = jnp.float32
BF16 = jnp.bfloat16
I32 = jnp.int32

N_HEADS = 4
HEAD_DIM = 128
D_MLSTM = N_HEADS * HEAD_DIM
D_CONV = 512
N_GROUPS = 4
EXPERTS_PER_GROUP = 8
N_EXPERTS = N_GROUPS * EXPERTS_PER_GROUP
D_FF = 512
RMS_EPS = 1e-6
Q_SCALE = HEAD_DIM ** -0.5

LANES = 128
CHUNK = LANES
AUG_ROWS = 16
ROUTER_COLS = 128
EXPERT_COL0 = 8

TM_INPROJ = 1024
TS_MIXER = 1024
OUT_COLS = 256
TM_EXPERT = 512
EXPERT_TILES_PER_STEP = 2
TC_COMBINE = 2048

VMEM_LIMIT = 56 * 1024 * 1024


HI_HALF = -65536


def _pack_rows(x):
    w = x.shape[1] // 2
    lo = lax.shift_right_logical(lax.bitcast_convert_type(x[:, :w].astype(BF16).astype(F32), I32), 16)
    hi = lax.bitcast_convert_type(x[:, w:].astype(BF16).astype(F32), I32) & HI_HALF
    return lo | hi


def _unpack_rows(words):
    lo = lax.bitcast_convert_type(words << 16, F32)
    hi = lax.bitcast_convert_type(words & HI_HALF, F32)
    return lo, hi


def _nt_dot(a, b):
    return lax.dot_general(a, b, (((1,), (1,)), ((), ())), preferred_element_type=F32)


def _inproj_kernel(x_ref, g_ref, w_ref, ba_ref, bt_ref, a_ref, qvt_ref, gt_ref, wt_ref, wc_ref):
    dm = D_MLSTM
    d = w_ref.shape[0]
    g0 = 4 * dm
    n_gate = 2 * N_HEADS
    step_cols = 2 * LANES

    @pl.when(jnp.logical_and(pl.program_id(0) == 0, pl.program_id(1) == 0))
    def _():
        for j, c0 in enumerate((0, 2 * dm)):
            for h in range(0, dm, step_cols):
                blk = w_ref[:, c0 + h:c0 + h + step_cols].astype(F32)
                wt_ref[j * dm + h:j * dm + h + step_cols, :] = jnp.transpose(blk).astype(BF16)
        gates_t = jnp.transpose(w_ref[:, g0:g0 + LANES].astype(F32))[0:AUG_ROWS, :]
        row = lax.broadcasted_iota(I32, (AUG_ROWS, d), 0)
        wt_ref[2 * dm:, :] = jnp.where(row < n_gate, gates_t, 0.0).astype(BF16)
        for r in range(0, d, step_cols):
            tail = w_ref[r:r + step_cols, g0:].astype(F32)
            wc_ref[r:r + step_cols, :] = tail[:, n_gate:n_gate + 3 * D_CONV].astype(BF16)

    x = x_ref[...]
    ms = jnp.mean(x * x, axis=-1, keepdims=True)
    hn = (x * lax.rsqrt(ms + RMS_EPS) * g_ref[...]).astype(BF16)
    blocks = [w_ref.at[:, dm:2 * dm], w_ref.at[:, 3 * dm:4 * dm]]
    blocks += [wc_ref.at[:, j * D_CONV:(j + 1) * D_CONV] for j in range(3)]
    for j, blk_ref in enumerate(blocks):
        sl = slice(j * dm, (j + 1) * dm)
        acc = jnp.dot(hn, blk_ref[...], preferred_element_type=F32) + ba_ref[:, sl]
        a_ref[:, sl] = acc.astype(BF16)
    tr = _nt_dot(wt_ref[...], hn) + bt_ref[...]
    q_rows = slice(0, D_MLSTM)
    v_rows = slice(D_MLSTM, 2 * D_MLSTM)
    qvt_ref[q_rows, :] = (tr[q_rows, :] * Q_SCALE).astype(BF16)
    qvt_ref[v_rows, :] = tr[v_rows, :].astype(BF16)
    gt_ref[...] = tr[2 * D_MLSTM:2 * D_MLSTM + 2 * N_HEADS, :]


def _inproj(x, norm_mix, w_in, b_in):
    b_sz, s_len, d = x.shape
    tm = TM_INPROJ
    dm = D_MLSTM
    g0 = 4 * dm
    c0 = g0 + 2 * N_HEADS
    assert D_CONV == dm and w_in.shape[1] == c0 + 3 * D_CONV
    bias_a = jnp.concatenate([b_in[dm:2 * dm], b_in[3 * dm:4 * dm], b_in[c0:c0 + 3 * D_CONV]])[None, :]
    n_gate = 2 * N_HEADS
    pad_rows = AUG_ROWS - n_gate
    b_t = jnp.concatenate([b_in[0:dm], b_in[2 * dm:3 * dm], b_in[g0:c0],
                           jnp.zeros((pad_rows,), b_in.dtype)])[:, None]
    n_t = 2 * dm + AUG_ROWS
    n_a = 2 * dm + 3 * D_CONV
    n_in = w_in.shape[1]
    return pl.pallas_call(
        _inproj_kernel,
        grid=(b_sz, s_len // tm),
        in_specs=[
            pl.BlockSpec((None, tm, d), lambda b, s: (b, s, 0)),
            pl.BlockSpec((1, d), lambda b, s: (0, 0)),
            pl.BlockSpec((d, n_in), lambda b, s: (0, 0)),
            pl.BlockSpec((1, n_a), lambda b, s: (0, 0)),
            pl.BlockSpec((n_t, 1), lambda b, s: (0, 0)),
        ],
        out_specs=[
            pl.BlockSpec((None, tm, n_a), lambda b, s: (b, s, 0)),
            pl.BlockSpec((None, 2 * dm, tm), lambda b, s: (b, 0, s)),
            pl.BlockSpec((None, 2 * N_HEADS, tm), lambda b, s: (b, 0, s)),
        ],
        out_shape=[
            jax.ShapeDtypeStruct((b_sz, s_len, n_a), BF16),
            jax.ShapeDtypeStruct((b_sz, 2 * dm, s_len), BF16),
            jax.ShapeDtypeStruct((b_sz, 2 * N_HEADS, s_len), F32),
        ],
        scratch_shapes=[pltpu.VMEM((n_t, d), BF16),
                        pltpu.VMEM((d, 3 * D_CONV), BF16)],
        compiler_params=pltpu.CompilerParams(
            dimension_semantics=("arbitrary", "arbitrary"), vmem_limit_bytes=VMEM_LIMIT),
        name="inproj",
    )(x, norm_mix[None, :], w_in.astype(BF16), bias_a, b_t)


def _mixer_kernel(k_ref, o_ref, cb_ref, cc_ref, cx_ref, qt_ref, vt_ref, gt_ref, x_ref,
                  cw_ref, wo_ref, g2_ref, wr_ref, br_ref,
                  x2_ref, hn2_ref, ri_ref, rw_ref, cnt_out_ref,
                  ct_ref, m_ref, carry_ref, cnt_ref, mix_ref):
    ts = x_ref.shape[0]
    n_chunks = ts // CHUNK
    first_tile = pl.program_id(1) == 0

    @pl.when(first_tile)
    def _():
        ct_ref[...] = jnp.zeros_like(ct_ref)
        m_ref[...] = jnp.zeros_like(m_ref)
        carry_ref[...] = jnp.zeros_like(carry_ref)

    @pl.when(jnp.logical_and(first_tile, pl.program_id(0) == 0))
    def _():
        cnt_ref[...] = jnp.zeros_like(cnt_ref)

    gates = gt_ref[...]
    li_all = gates[0:N_HEADS]
    fg = gates[N_HEADS:2 * N_HEADS]
    lf_all = jnp.minimum(fg, 0.0) - jnp.log1p(jnp.exp(-jnp.abs(fg)))
    key_j = lax.broadcasted_iota(I32, (CHUNK, CHUNK), 0)
    qry_i = lax.broadcasted_iota(I32, (CHUNK, CHUNK), 1)
    causal_t = key_j <= qry_i
    neg_inf = jnp.float32(-jnp.inf)
    ones_rows = (lax.broadcasted_iota(I32, (AUG_ROWS, CHUNK), 0) == 0).astype(BF16)

    prefix_ones = (key_j <= qry_i).astype(BF16)

    lf_hi = lf_all.astype(BF16)
    rem = lf_all - lf_hi.astype(F32)
    lf_mid = rem.astype(BF16)
    lf_lo = (rem - lf_mid.astype(F32)).astype(BF16)
    zero_bf = jnp.zeros((N_HEADS, CHUNK), BF16)
    parts = []
    for c in range(n_chunks):
        rows = slice(c * CHUNK, (c + 1) * CHUNK)
        parts += [lf_hi[:, rows], lf_mid[:, rows], lf_lo[:, rows], zero_bf]
    cum = jnp.dot(jnp.concatenate(parts, axis=0), prefix_ones, preferred_element_type=F32)

    m_prev = m_ref[0:N_HEADS, 0:1]
    for c in range(n_chunks):
        rows = slice(c * CHUNK, (c + 1) * CHUNK)
        li = li_all[:, rows]
        r0 = 4 * N_HEADS * c
        b_cum = (cum[r0:r0 + N_HEADS] + cum[r0 + N_HEADS:r0 + 2 * N_HEADS]
                 + cum[r0 + 2 * N_HEADS:r0 + 3 * N_HEADS])
        g_tot = b_cum[:, CHUNK - 1:CHUNK]
        a_end = g_tot - b_cum + li
        m_loc = jnp.max(a_end, axis=1, keepdims=True)
        u = li - b_cum
        m_new = jnp.maximum(g_tot + m_prev, m_loc)
        decay = jnp.exp(g_tot + m_prev - m_new)
        s_fac = jnp.exp(m_loc - m_new)
        w_key = jnp.exp(a_end - m_loc)
        u_cols = jnp.transpose(jnp.concatenate([u, jnp.zeros_like(u)], axis=0))

        def pair_row(x, h0):
            parts = [jnp.broadcast_to(x[h:h + 1, :], (1, CHUNK)) for h in (h0, h0 + 1)]
            return jnp.concatenate(parts, axis=1)

        def block_diag(a, b):
            z = jnp.zeros_like(a)
            return jnp.concatenate([jnp.concatenate([a, z], axis=1), jnp.concatenate([z, b], axis=1)], axis=0)

        for p in range(N_HEADS // 2):
            h0 = 2 * p
            pc = slice(h0 * HEAD_DIM, (h0 + 2) * HEAD_DIM)
            lo_l, hi_l = slice(0, CHUNK), slice(CHUNK, 2 * CHUNK)
            k2 = k_ref[rows, pc]
            q2 = qt_ref[pc, rows]
            v2 = vt_ref[pc, rows]
            s2 = jnp.dot(k2, block_diag(q2[0:HEAD_DIM], q2[HEAD_DIM:]), preferred_element_type=F32)
            u_t = jnp.concatenate([jnp.where(causal_t, u_cols[:, h:h + 1], neg_inf) for h in (h0, h0 + 1)],
                                  axis=1)
            mp = pair_row(m_prev, h0)
            m_tot = jnp.maximum(jnp.max(u_t, axis=0, keepdims=True), mp)
            f_inter = jnp.exp(mp - m_tot)
            floor = jnp.exp(-pair_row(b_cum, h0) - m_tot)
            p2 = (s2 * jnp.exp(u_t - m_tot)).astype(BF16)
            qf = [(q2[r].astype(F32) * f_inter[:, l]).astype(BF16)
                  for r, l in ((slice(0, HEAD_DIM), lo_l), (slice(HEAD_DIM, None), hi_l))]
            c_old = ct_ref[p]
            cb16 = c_old.astype(BF16)
            vt_aug = [jnp.concatenate([v2[r], ones_rows], axis=0)
                      for r in (slice(0, HEAD_DIM), slice(HEAD_DIM, None))]
            lhs = jnp.concatenate([vt_aug[0], cb16[:, lo_l], vt_aug[1], cb16[:, hi_l]], axis=1)
            rhs = block_diag(jnp.concatenate([p2[:, lo_l], qf[0]], axis=0),
                             jnp.concatenate([p2[:, hi_l], qf[1]], axis=0))
            r2 = jnp.dot(lhs, rhs, preferred_element_type=F32)
            w2 = pair_row(w_key, h0)
            vtw = jnp.concatenate([(vt_aug[0].astype(F32) * w2[:, lo_l]).astype(BF16),
                                   (vt_aug[1].astype(F32) * w2[:, hi_l]).astype(BF16)], axis=1)
            kv = jnp.dot(vtw, block_diag(k2[:, lo_l], k2[:, hi_l]), preferred_element_type=F32)
            ct_ref[p] = pair_row(decay, h0) * c_old + pair_row(s_fac, h0) * kv

            den = r2[HEAD_DIM:HEAD_DIM + 1, :]
            inv = 1.0 / jnp.maximum(jnp.abs(den), floor)
            h_t = r2[0:HEAD_DIM, :] * inv
            for h, l in ((h0, lo_l), (h0 + 1, hi_l)):
                hc = slice(h * HEAD_DIM, (h + 1) * HEAD_DIM)
                o_gate = jax.nn.sigmoid(o_ref[rows, hc].astype(F32))
                mix_ref[rows, hc] = (o_gate * jnp.transpose(h_t[:, l])).astype(BF16)
        m_prev = m_new
    m_ref[0:N_HEADS, :] = jnp.broadcast_to(m_prev, (N_HEADS, LANES))

    u_c = cc_ref[...].astype(F32) * cx_ref[...].astype(F32)
    rid = lax.broadcasted_iota(I32, u_c.shape, 0)
    prev1 = carry_ref[0:1, :]
    prev2 = carry_ref[1:2, :]
    u1 = jnp.where(rid == 0, prev1, pltpu.roll(u_c, 1, axis=0))
    u2 = jnp.where(rid == 0, prev2, jnp.where(rid == 1, prev1, pltpu.roll(u_c, 2, axis=0)))
    z = cw_ref[0:1, :] * u2 + cw_ref[1:2, :] * u1 + cw_ref[2:3, :] * u_c
    mix_ref[:, D_MLSTM:] = (cb_ref[...].astype(F32) * z).astype(BF16)
    carry_ref[0:1, :] = u_c[ts - 1:ts, :]
    carry_ref[1:2, :] = u_c[ts - 2:ts - 1, :]

    d_model = x_ref.shape[1]
    mix = mix_ref[...]
    ssq = jnp.zeros((ts, 1), F32)
    for j in range(d_model // OUT_COLS):
        cs = slice(j * OUT_COLS, (j + 1) * OUT_COLS)
        x2c = x_ref[:, cs] + jnp.dot(mix, wo_ref[:, cs], preferred_element_type=F32)
        x2_ref[:, cs] = x2c
        ssq = ssq + jnp.sum(x2c * x2c, axis=-1, keepdims=True)
    hn2 = x2_ref[...] * lax.rsqrt(ssq * (1.0 / d_model) + RMS_EPS) * g2_ref[...]
    hn2_ref[...] = _pack_rows(hn2)

    logits = jnp.dot(hn2.astype(BF16), wr_ref[...], preferred_element_type=F32) + br_ref[...]
    lt = jnp.transpose(logits)
    gl = lt[0:N_GROUPS]
    gmax = jnp.max(gl, axis=0, keepdims=True)
    gi = lax.broadcasted_iota(I32, gl.shape, 0)
    g_sel = jnp.min(jnp.where(gl == gmax, gi, N_GROUPS), axis=0, keepdims=True)
    p_g = 1.0 / jnp.sum(jnp.exp(gl - gmax), axis=0, keepdims=True)
    in_g = lt[EXPERT_COL0:EXPERT_COL0 + EXPERTS_PER_GROUP]
    for g in range(1, N_GROUPS):
        lo = EXPERT_COL0 + g * EXPERTS_PER_GROUP
        in_g = jnp.where(g_sel == g, lt[lo:lo + EXPERTS_PER_GROUP], in_g)
    ei = lax.broadcasted_iota(I32, in_g.shape, 0)
    v1 = jnp.max(in_g, axis=0, keepdims=True)
    i1 = jnp.min(jnp.where(in_g == v1, ei, EXPERTS_PER_GROUP), axis=0, keepdims=True)
    rest = jnp.where(ei == i1, neg_inf, in_g)
    v2 = jnp.max(rest, axis=0, keepdims=True)
    i2 = jnp.min(jnp.where(rest == v2, ei, EXPERTS_PER_GROUP), axis=0, keepdims=True)
    t = jnp.exp(v2 - v1)
    w1 = p_g / (1.0 + t)
    w2 = p_g * t / (1.0 + t)
    e1 = g_sel * EXPERTS_PER_GROUP + i1
    e2 = g_sel * EXPERTS_PER_GROUP + i2

    eio = lax.broadcasted_iota(I32, (N_EXPERTS, LANES), 0)
    base = cnt_ref[:, 0:1]
    zero_i = jnp.zeros((1, LANES), I32)
    n_blk = ts // LANES
    hits = []
    for k in range(n_blk):
        ls = slice(k * LANES, (k + 1) * LANES)
        hits.append((eio == e1[:, ls], eio == e2[:, ls]))
    onehots = [jnp.where(jnp.logical_or(a1, a2), 1.0, 0.0) for a1, a2 in hits]
    incl_all = jnp.dot(jnp.concatenate(onehots, axis=0).astype(BF16), prefix_ones,
                       preferred_element_type=F32).astype(I32)
    for k in range(n_blk):
        ls = slice(k * LANES, (k + 1) * LANES)
        is1, is2 = hits[k]
        onehot = onehots[k].astype(I32)
        incl = incl_all[k * N_EXPERTS:(k + 1) * N_EXPERTS]
        rank_e = base + incl - onehot
        r1 = jnp.sum(jnp.where(is1, rank_e, 0), axis=0, keepdims=True)
        r2 = jnp.sum(jnp.where(is2, rank_e, 0), axis=0, keepdims=True)
        ri_ref[:, ls] = jnp.concatenate(
            [e1[:, ls], e2[:, ls], r1, r2, zero_i, zero_i, zero_i, zero_i], axis=0)
        base = base + incl[:, LANES - 1:LANES]
    zero_f = jnp.zeros_like(w1)
    rw_ref[...] = jnp.concatenate([w1, w2, zero_f, zero_f, zero_f, zero_f, zero_f, zero_f], axis=0)
    cnt_new = jnp.broadcast_to(base, (N_EXPERTS, LANES))
    cnt_ref[...] = cnt_new
    cnt_out_ref[...] = cnt_new


def _mixer(x, a, qvt, gt, conv_w, w_out, norm_ffn, w_router, b_router):
    b_sz, s_len, d = x.shape
    ts = TS_MIXER
    dm = D_MLSTM

    def a_spec(j):
        return pl.BlockSpec((None, ts, dm), lambda b, s, j=j: (b, s, j))

    tok = lambda w: pl.BlockSpec((None, ts, w), lambda b, s: (b, s, 0))
    rowsp = pl.BlockSpec((None, 2 * N_HEADS, ts), lambda b, s: (b, 0, s))
    const = lambda shape: pl.BlockSpec(shape, lambda b, s: tuple(0 for _ in shape))
    return pl.pallas_call(
        _mixer_kernel,
        grid=(b_sz, s_len // ts),
        in_specs=[a_spec(0), a_spec(1), a_spec(2), a_spec(3), a_spec(4),
                  pl.BlockSpec((None, dm, ts), lambda b, s: (b, 0, s)),
                  pl.BlockSpec((None, dm, ts), lambda b, s: (b, 1, s)),
                  rowsp, tok(d),
                  const((3, D_CONV)), const((d, d)), const((1, d)),
                  const((d, ROUTER_COLS)), const((1, ROUTER_COLS))],
        out_specs=[tok(d), tok(d // 2), rowsp, rowsp, const((N_EXPERTS, LANES))],
        out_shape=[
            jax.ShapeDtypeStruct((b_sz, s_len, d), F32),
            jax.ShapeDtypeStruct((b_sz, s_len, d // 2), I32),
            jax.ShapeDtypeStruct((b_sz, 2 * N_HEADS, s_len), I32),
            jax.ShapeDtypeStruct((b_sz, 2 * N_HEADS, s_len), F32),
            jax.ShapeDtypeStruct((N_EXPERTS, LANES), I32),
        ],
        scratch_shapes=[
            pltpu.VMEM((N_HEADS // 2, HEAD_DIM + AUG_ROWS, 2 * HEAD_DIM), F32),
            pltpu.VMEM((2 * N_HEADS, LANES), F32),
            pltpu.VMEM((8, D_CONV), F32),
            pltpu.VMEM((N_EXPERTS, LANES), I32),
            pltpu.VMEM((ts, d), BF16),
        ],
        compiler_params=pltpu.CompilerParams(
            dimension_semantics=("arbitrary", "arbitrary"), vmem_limit_bytes=VMEM_LIMIT),
        name="mixer",
    )(a, a, a, a, a, qvt, qvt, gt, x, conv_w, w_out.astype(BF16), norm_ffn[None, :], w_router.astype(BF16), b_router)


SC_CORES = 2
SC_SUBCORES = 16
SC_WORKERS = SC_CORES * SC_SUBCORES
SC_CHUNK = 64


def _sc_worker_id():
    return lax.axis_index("s") * SC_CORES + lax.axis_index("c")


def _sc_scatter_rows(rows, idx, n_out):
    n_rows, width = rows.shape
    per_worker = n_rows // SC_WORKERS
    n_chunks = per_worker // SC_CHUNK
    assert n_chunks % 2 == 0
    mesh = plsc.VectorSubcoreMesh(core_axis_name="c", subcore_axis_name="s")

    def body(rows_hbm, idx_hbm, out_hbm, ia0, ib0, r0, ia1, ib1, r1, ls0, ls1, ss0, ss1):
        wid = _sc_worker_id()
        bufs = ((ia0, ib0, r0, ls0, ss0), (ia1, ib1, r1, ls1, ss1))

        def loads(j, b):
            ia, ib, r, ls, _ = bufs[b]
            base = pl.multiple_of(wid * per_worker + j * SC_CHUNK, SC_CHUNK)
            return (pltpu.make_async_copy(rows_hbm.at[pl.ds(base, SC_CHUNK)], r, ls),
                    pltpu.make_async_copy(idx_hbm.at[pl.ds(base, SC_CHUNK)], ia, ls),
                    pltpu.make_async_copy(idx_hbm.at[pl.ds(n_rows + base, SC_CHUNK)], ib, ls))

        def scatters(b):
            ia, ib, r, _, ss = bufs[b]
            return (pltpu.make_async_copy(r, out_hbm.at[ia], ss),
                    pltpu.make_async_copy(r, out_hbm.at[ib], ss))

        for c in loads(0, 0):
            c.start()

        @pl.loop(0, n_chunks, step=2)
        def _(j0):
            for b in range(2):
                j = j0 + b
                for c in loads(j, b):
                    c.wait()

                @pl.when(j >= 1)
                def _():
                    for c in scatters(1 - b):
                        c.wait()

                @pl.when(j + 1 < n_chunks)
                def _():
                    for c in loads(j + 1, 1 - b):
                        c.start()

                for c in scatters(b):
                    c.start()

        for c in scatters((n_chunks - 1) % 2):
            c.wait()

    idx_t = pltpu.VMEM((SC_CHUNK,), I32)
    row_t = pltpu.VMEM((SC_CHUNK, width), I32)
    sem_t = pltpu.SemaphoreType.DMA
    return pl.kernel(
        body,
        out_type=jax.ShapeDtypeStruct((n_out, width), I32),
        mesh=mesh,
        scratch_types=[idx_t, idx_t, row_t, idx_t, idx_t, row_t, sem_t, sem_t, sem_t, sem_t],
        name="sc_scatter",
    )(rows, idx)


def _sc_gather_rows(table, idx):
    n_rows = idx.shape[0]
    width = table.shape[1]
    per_worker = n_rows // SC_WORKERS
    n_chunks = per_worker // SC_CHUNK
    assert n_chunks % 2 == 0
    mesh = plsc.VectorSubcoreMesh(core_axis_name="c", subcore_axis_name="s")

    def body(table_hbm, idx_hbm, out_hbm, i0, r0, i1, r1, is0, is1, gs0, gs1, ws0, ws1):
        wid = _sc_worker_id()
        bufs = ((i0, r0, is0, gs0, ws0), (i1, r1, is1, gs1, ws1))

        def chunk_base(j):
            return pl.multiple_of(wid * per_worker + j * SC_CHUNK, SC_CHUNK)

        def idx_load(j, b):
            return pltpu.make_async_copy(idx_hbm.at[pl.ds(chunk_base(j), SC_CHUNK)], bufs[b][0], bufs[b][2])

        def gather(b):
            return pltpu.make_async_copy(table_hbm.at[bufs[b][0]], bufs[b][1], bufs[b][3])

        def writeback(j, b):
            return pltpu.make_async_copy(bufs[b][1], out_hbm.at[pl.ds(chunk_base(j), SC_CHUNK)], bufs[b][4])

        idx_load(0, 0).start()
        idx_load(0, 0).wait()
        gather(0).start()
        idx_load(1, 1).start()

        @pl.loop(0, n_chunks, step=2)
        def _(j0):
            for b in range(2):
                j = j0 + b

                @pl.when(j + 1 < n_chunks)
                def _():
                    idx_load(j + 1, 1 - b).wait()

                    @pl.when(j >= 1)
                    def _():
                        writeback(j - 1, 1 - b).wait()
                    gather(1 - b).start()

                gather(b).wait()
                writeback(j, b).start()

                @pl.when(j + 2 < n_chunks)
                def _():
                    idx_load(j + 2, b).start()

        writeback(n_chunks - 2, 0).wait()
        writeback(n_chunks - 1, 1).wait()

    idx_t = pltpu.VMEM((SC_CHUNK,), I32)
    row_t = pltpu.VMEM((SC_CHUNK, width), I32)
    sem_t = pltpu.SemaphoreType.DMA
    return pl.kernel(
        body,
        out_type=jax.ShapeDtypeStruct((n_rows, width), I32),
        mesh=mesh,
        scratch_types=[idx_t, row_t, idx_t, row_t, sem_t, sem_t, sem_t, sem_t, sem_t, sem_t],
        name="sc_gather",
    )(table, idx)


def _experts_kernel(te_ref, nu_ref, tv_ref, ne_ref, xs_ref, wg_hbm, wu_hbm, wd_hbm, y_ref,
                    wgs_ref, wus_ref, wds_ref, wgb_ref, wub_ref, wdb_ref, wsem):
    tm = TM_EXPERT

    def weight_copies(expert):
        return (pltpu.make_async_copy(wg_hbm.at[expert], wgs_ref, wsem),
                pltpu.make_async_copy(wu_hbm.at[expert], wus_ref, wsem),
                pltpu.make_async_copy(wd_hbm.at[expert], wds_ref, wsem))

    def tile(t, rows):
        e = te_ref[t]
        e_prev = te_ref[jnp.maximum(t - 1, 0)]
        new_expert = jnp.logical_or(t == 0, e != e_prev)
        used = t < nu_ref[0]

        @pl.when(t == 0)
        def _():
            for c in weight_copies(e):
                c.start()

        @pl.when(jnp.logical_and(used, new_expert))
        def _():
            for c in weight_copies(e):
                c.wait()
            wgb_ref[...] = wgs_ref[...].astype(BF16)
            wub_ref[...] = wus_ref[...].astype(BF16)
            wdb_ref[...] = wds_ref[...].astype(BF16)

            @pl.when(ne_ref[t] >= 0)
            def _():
                for c in weight_copies(ne_ref[t]):
                    c.start()

        @pl.when(used)
        def _():
            row = lax.broadcasted_iota(I32, (tm, xs_ref.shape[1]), 0)
            x_lo, x_hi = _unpack_rows(jnp.where(row < tv_ref[t], xs_ref[rows, :], 0))
            xb = jnp.concatenate([x_lo.astype(BF16), x_hi.astype(BF16)], axis=1)
            gate = jnp.dot(xb, wgb_ref[...], preferred_element_type=F32)
            up = jnp.dot(xb, wub_ref[...], preferred_element_type=F32)
            hid = (gate * jax.nn.sigmoid(gate) * up).astype(BF16)
            y_ref[rows, :] = _pack_rows(jnp.dot(hid, wdb_ref[...], preferred_element_type=F32))

        @pl.when(jnp.logical_not(used))
        def _():
            y_ref[rows, :] = jnp.zeros((tm, y_ref.shape[1]), y_ref.dtype)

    tiles_per_step = xs_ref.shape[0] // tm
    for h in range(tiles_per_step):
        tile(pl.program_id(0) * tiles_per_step + h, slice(h * tm, (h + 1) * tm))


def _experts(xs, tile_e, n_used, tile_valid, next_e, w_gate, w_up, w_down):
    cap, dw = xs.shape
    d = 2 * dw
    tm = TM_EXPERT
    n_tiles = cap // tm
    tps = EXPERT_TILES_PER_STEP
    assert n_tiles % tps == 0

    def x_map(i, te, nu, tv, ne):
        return (jnp.minimum(i, jnp.maximum(nu[0] - 1, 0) // tps), 0)

    return pl.pallas_call(
        _experts_kernel,
        grid_spec=pltpu.PrefetchScalarGridSpec(
            num_scalar_prefetch=4,
            grid=(n_tiles // tps,),
            in_specs=[
                pl.BlockSpec((tps * tm, dw), x_map),
                pl.BlockSpec(memory_space=pl.ANY),
                pl.BlockSpec(memory_space=pl.ANY),
                pl.BlockSpec(memory_space=pl.ANY),
            ],
            out_specs=pl.BlockSpec((tps * tm, dw), lambda i, te, nu, tv, ne: (i, 0)),
            scratch_shapes=[pltpu.VMEM((d, D_FF), F32), pltpu.VMEM((d, D_FF), F32), pltpu.VMEM((D_FF, d), F32),
                            pltpu.VMEM((d, D_FF), BF16), pltpu.VMEM((d, D_FF), BF16), pltpu.VMEM((D_FF, d), BF16),
                            pltpu.SemaphoreType.DMA(())],
        ),
        out_shape=jax.ShapeDtypeStruct((cap, dw), I32),
        compiler_params=pltpu.CompilerParams(
            dimension_semantics=("arbitrary",), vmem_limit_bytes=VMEM_LIMIT),
        name="experts",
    )(tile_e, n_used, tile_valid, next_e, xs, w_gate, w_up, w_down)


def _combine_stream_kernel(x2_ref, y1_ref, y2_ref, rw_ref, gf_ref, out_ref):
    w_t = jnp.transpose(rw_ref[...])
    a_lo, a_hi = _unpack_rows(y1_ref[...])
    b_lo, b_hi = _unpack_rows(y2_ref[...])
    w1 = w_t[:, 0:1]
    w2 = w_t[:, 1:2]
    moe = jnp.concatenate([w1 * a_lo + w2 * b_lo, w1 * a_hi + w2 * b_hi], axis=1)
    x3 = x2_ref[...] + moe
    ms = jnp.mean(x3 * x3, axis=-1, keepdims=True)
    out_ref[...] = x3 * lax.rsqrt(ms + RMS_EPS) * gf_ref[...]


def _combine_stream(x2, yg, rw, norm_final):
    b_sz, s_len, d = x2.shape
    tc = TC_COMBINE
    n_s = s_len // tc
    n_steps = b_sz * n_s
    return pl.pallas_call(
        _combine_stream_kernel,
        grid=(n_steps,),
        in_specs=[
            pl.BlockSpec((tc, d), lambda g: (g, 0)),
            pl.BlockSpec((tc, d // 2), lambda g: (g, 0)),
            pl.BlockSpec((tc, d // 2), lambda g: (g + n_steps, 0)),
            pl.BlockSpec((None, 2 * N_HEADS, tc), lambda g: (g // n_s, 0, g % n_s)),
            pl.BlockSpec((1, d), lambda g: (0, 0)),
        ],
        out_specs=pl.BlockSpec((tc, d), lambda g: (g, 0)),
        out_shape=jax.ShapeDtypeStruct((b_sz * s_len, d), F32),
        compiler_params=pltpu.CompilerParams(
            dimension_semantics=("arbitrary",), vmem_limit_bytes=VMEM_LIMIT),
        name="combine_stream",
    )(x2.reshape(b_sz * s_len, d), yg, yg, rw, norm_final[None, :]).reshape(b_sz, s_len, d)


TM_EXPERT_SHIFT = TM_EXPERT.bit_length() - 1
assert 1 << TM_EXPERT_SHIFT == TM_EXPERT


def _plan_kernel(cnt_ref, ri_ref, pos_ref, meta_ref):
    tm = TM_EXPERT
    n_e, lanes = cnt_ref.shape
    cnt = cnt_ref[...]
    padded = lax.shift_right_logical(cnt + (tm - 1), TM_EXPERT_SHIFT) * tm
    sub = lax.broadcasted_iota(I32, (n_e, lanes), 0)
    lane = lax.broadcasted_iota(I32, (n_e, lanes), 1)
    padded_row = jnp.sum(jnp.where(sub == lane, padded, 0), axis=0, keepdims=True)
    below = jnp.where(lane <= sub, jnp.broadcast_to(padded_row, (n_e, lanes)), 0)
    pends = jnp.sum(below.astype(F32), axis=1, keepdims=True).astype(I32)
    seg = pends - padded[:, 0:1]
    tok_end = seg + cnt[:, 0:1]
    total = pends[n_e - 1:n_e, :]

    n_b, _, s_len = ri_ref.shape
    e_ids = lax.broadcasted_iota(I32, (n_e, s_len), 0)
    for b in range(n_b):
        for k in range(2):
            start = jnp.sum(jnp.where(ri_ref[b, k:k + 1, :] == e_ids, seg, 0), axis=0, keepdims=True)
            pos_ref[k, b:b + 1, :] = ri_ref[b, 2 + k:3 + k, :] + start

    n_t = meta_ref.shape[1]
    t_start = lax.broadcasted_iota(I32, (1, n_t), 1) * tm
    experts_before = lambda row: jnp.minimum(
        jnp.sum(jnp.where(row >= pends, 1, 0), axis=0, keepdims=True), n_e - 1)
    tile_e = experts_before(t_start)
    mine = tile_e == lax.broadcasted_iota(I32, (n_e, n_t), 0)
    pick = lambda col: jnp.sum(jnp.where(mine, col, 0), axis=0, keepdims=True)
    tile_valid = jnp.clip(pick(tok_end) - t_start, 0, tm)
    seg_end = pick(pends)
    next_e = jnp.where(seg_end < total, experts_before(seg_end), -1)
    n_used = jnp.broadcast_to(lax.shift_right_logical(total, TM_EXPERT_SHIFT), (1, n_t))
    zero = jnp.zeros_like(tile_e)
    meta_ref[...] = jnp.concatenate([tile_e, tile_valid, next_e, n_used, zero, zero, zero, zero], axis=0)


def _plan(ri, counts, n_tiles):
    b_sz, _, s_len = ri.shape
    assert n_tiles * TM_EXPERT < 2 ** 24
    n_t = -(-n_tiles // LANES) * LANES
    pos, meta = pl.pallas_call(
        _plan_kernel,
        out_shape=[jax.ShapeDtypeStruct((2, b_sz, s_len), I32), jax.ShapeDtypeStruct((8, n_t), I32)],
        name="plan",
    )(counts, ri)
    return pos.reshape(-1), meta[0, :n_tiles], meta[3, :1], meta[1, :n_tiles], meta[2, :n_tiles]


def _layer(x, norm_mix, w_in, b_in, conv_w, w_out, norm_ffn, w_group, b_group, w_expert, b_expert,
           w_gate, w_up, w_down, norm_out):
    b_sz, s_len, d = x.shape
    n_tok = b_sz * s_len
    a, qvt, gt = _inproj(x, norm_mix, w_in, b_in)

    pad_g = EXPERT_COL0 - N_GROUPS
    pad_e = ROUTER_COLS - EXPERT_COL0 - N_EXPERTS
    w_router = jnp.concatenate([w_group, jnp.zeros((d, pad_g), F32), w_expert, jnp.zeros((d, pad_e), F32)], axis=1)
    b_router = jnp.concatenate([b_group, jnp.zeros((pad_g,), F32), b_expert, jnp.zeros((pad_e,), F32)])[None, :]
    x2, hn2, ri, rw, counts = _mixer(x, a, qvt, gt, conv_w, w_out, norm_ffn, w_router, b_router)

    n_tiles = (n_tok * 2) // TM_EXPERT + N_EXPERTS
    cap = n_tiles * TM_EXPERT
    pos_flat, tile_e, n_used, tile_valid, next_e = _plan(ri, counts, n_tiles)
    xs = _sc_scatter_rows(hn2.reshape(n_tok, d // 2), pos_flat, cap)
    y = _experts(xs, tile_e, n_used, tile_valid, next_e, w_gate, w_up, w_down)
    yg = _sc_gather_rows(y, pos_flat)
    return _combine_stream(x2, yg, rw, norm_out)


def kernel(x, norm_mix, w_in, b_in, conv_w, w_out, norm_ffn, w_group, b_group, w_expert, b_expert,
           w_gate, w_up, w_down, norm_final):
    depth = norm_mix.shape[0]
    assert depth == 1, "final-norm fusion below assumes a single layer"
    assert x.shape[-1] == 2 * D_MLSTM
    assert all(x.shape[1] % t == 0 for t in (TS_MIXER, TM_INPROJ, TC_COMBINE))
    return _layer(x, norm_mix[0], w_in[0], b_in[0], conv_w[0], w_out[0], norm_ffn[0],
                  w_group[0], b_group[0], w_expert[0], b_expert[0],
                  w_gate[0], w_up[0], w_down[0], norm_final)
```
